```python
import math
import jax, jax.numpy as jnp
from jax import lax
import numpy as np

D_MODEL = 1024
BATCH = 8
SEQ = 2048
DEPTH = 4

D_MIX = D_MODEL
N_MIXERS = 4
W_GRP = D_MIX // N_MIXERS
SGU_HEADS = 4
SGU_HEAD_DIM = W_GRP // SGU_HEADS
CHUNK = 128
POOL_WINDOWS = (2, 4, 8, 16)
POOL_GROUPS = len(POOL_WINDOWS)
POOL_GROUP_DIM = W_GRP // POOL_GROUPS
CONV_WIDTH = 3
S5_GROUP_CH = 16
S5_GROUPS = W_GRP // S5_GROUP_CH
S5_STATE = 64
P_IN = 2 * W_GRP + W_GRP + 3 * W_GRP + W_GRP
D_FF = ((8 * D_MODEL // 3 + 255) // 256) * 256
N_ADA = 9
EPS = 1e-6

kernel_name = "hybrid_parallel_mixer_trunk"


def rmsnorm(x, g):
    xf = x.astype(jnp.float32)
    y = xf * lax.rsqrt(jnp.mean(xf * xf, axis=-1, keepdims=True) + EPS)
    return (y * g.astype(jnp.float32)).astype(x.dtype)


def group_rmsnorm(y, g):
    b, s, _ = y.shape
    yf = y.astype(jnp.float32).reshape(b, s, N_MIXERS, W_GRP)
    yf = yf * lax.rsqrt(jnp.mean(yf * yf, axis=-1, keepdims=True) + EPS)
    return (yf.reshape(b, s, D_MIX) * g.astype(jnp.float32)).astype(y.dtype)


def modulate(h, shift, scale):
    return h * (1.0 + scale) + shift


def swiglu(h, w_in, w_out):
    a, b = jnp.split(h @ w_in, 2, axis=-1)
    return (jax.nn.silu(a) * b) @ w_out


def sgu_mixer(z, w_s, b_s):
    bsz, s, _ = z.shape
    z = jax.nn.gelu(z)
    u, v = jnp.split(z, 2, axis=-1)
    v = v.reshape(bsz, s // CHUNK, CHUNK, SGU_HEADS, SGU_HEAD_DIM)
    vf = v.astype(jnp.float32)
    mu = jnp.mean(vf, axis=-1, keepdims=True)
    var = jnp.mean(jnp.square(vf - mu), axis=-1, keepdims=True)
    v = ((vf - mu) * lax.rsqrt(var + EPS)).astype(z.dtype)
    mask = jnp.tril(jnp.ones((CHUNK, CHUNK), dtype=w_s.dtype))
    mixed = jnp.einsum('hts,bnshd->bnthd', w_s * mask, v)
    mixed = mixed + b_s.T[None, None, :, :, None]
    return u * mixed.reshape(bsz, s, W_GRP)


def pool_mixer(z, w_p, scale):
    bsz, s, _ = z.shape
    zf = z.astype(jnp.float32).reshape(bsz, s, POOL_GROUPS, POOL_GROUP_DIM)
    cs = jnp.concatenate([jnp.zeros_like(zf[:, :1]), jnp.cumsum(zf, axis=1)], axis=1)
    t = jnp.arange(s)
    win = jnp.array(POOL_WINDOWS, dtype=jnp.int32)
    lo = jnp.maximum(t[:, None] + 1 - win[None, :], 0)
    cnt = (t[:, None] + 1 - lo).astype(jnp.float32)
    lower = cs[:, lo, jnp.arange(POOL_GROUPS)[None, :]]
    mean = (cs[:, 1:] - lower) / cnt[None, :, :, None]
    p = (mean - zf).astype(z.dtype)
    out = jnp.einsum('bsgc,gcd->bsgd', p, w_p).reshape(bsz, s, W_GRP)
    return out * scale


def conv_mixer(z, conv_w):
    bg, cg, xh = jnp.split(z, 3, axis=-1)
    y = cg * xh
    y = lax.conv_general_dilated(
        y, conv_w[:, None, :], window_strides=(1,), padding=[(CONV_WIDTH - 1, 0)],
        dimension_numbers=('NWC', 'WIO', 'NWC'), feature_group_count=W_GRP)
    return bg * y


def s5_mixer(u, lam_re, lam_im, b_re, b_im, c_re, c_im, d, log_dt, glu_w, glu_b):
    bsz, s, _ = u.shape
    f32 = jnp.float32
    dt = jnp.exp(log_dt.astype(f32))[:, None]
    lre, lim = lam_re.astype(f32), lam_im.astype(f32)
    mag = jnp.exp(lre * dt)
    ang = lim * dt
    a_re, a_im = mag * jnp.cos(ang), mag * jnp.sin(ang)
    nr, ni = a_re - 1.0, a_im
    den = lre * lre + lim * lim
    k_re = (nr * lre + ni * lim) / den
    k_im = (ni * lre - nr * lim) / den
    br, bi = b_re.astype(f32), b_im.astype(f32)
    bb_re = k_re[..., None] * br - k_im[..., None] * bi
    bb_im = k_re[..., None] * bi + k_im[..., None] * br
    uf = u.astype(f32)
    ug = uf.reshape(bsz, s, S5_GROUPS, S5_GROUP_CH)
    bu_re = jnp.einsum('bsgc,gpc->bsgp', ug, bb_re)
    bu_im = jnp.einsum('bsgc,gpc->bsgp', ug, bb_im)
    ar = jnp.broadcast_to(a_re, bu_re.shape)
    ai = jnp.broadcast_to(a_im, bu_re.shape)

    def combine(e1, e2):
        a1r, a1i, b1r, b1i = e1
        a2r, a2i, b2r, b2i = e2
        return (a2r * a1r - a2i * a1i,
                a2r * a1i + a2i * a1r,
                a2r * b1r - a2i * b1i + b2r,
                a2r * b1i + a2i * b1r + b2i)

    _, _, xr, xi = lax.associative_scan(combine, (ar, ai, bu_re, bu_im), axis=1)
    y = (jnp.einsum('gcp,bsgp->bsgc', c_re.astype(f32), xr)
         - jnp.einsum('gcp,bsgp->bsgc', c_im.astype(f32), xi))
    y = y.reshape(bsz, s, W_GRP) + d.astype(f32) * uf
    y = jax.nn.gelu(y).astype(u.dtype)
    return y * jax.nn.sigmoid(y @ glu_w + glu_b)


def _fwd_setup_inputs(seed: int = 0) -> dict:
    key = jax.random.key(seed)
    ks = jax.random.split(key, 32)
    f32 = jnp.float32
    L, D = DEPTH, D_MODEL

    def nrm(k, shape, scale):
        return jax.random.normal(k, shape, f32) * scale

    def gain(k, shape):
        return 1.0 + 0.05 * jax.random.normal(k, shape, f32)

    lam_im0 = jnp.broadcast_to(math.pi * jnp.arange(S5_STATE, dtype=f32), (L, S5_GROUPS, S5_STATE))
    return {
        "x": nrm(ks[0], (BATCH, SEQ, D), 1.0),
        "c": nrm(ks[1], (BATCH, D), 1.0),
        "ada_w": nrm(ks[2], (L, D, N_ADA * D), 0.5 * D ** -0.5),
        "ada_b": nrm(ks[3], (L, N_ADA * D), 0.01),
        "norm1_g": gain(ks[4], (L, D)),
        "ffn1_w_in": nrm(ks[5], (L, D, 2 * D_FF), D ** -0.5),
        "ffn1_w_out": nrm(ks[6], (L, D_FF, D), D_FF ** -0.5),
        "norm2_g": gain(ks[7], (L, D)),
        "w_mix_in": nrm(ks[8], (L, D, P_IN), D ** -0.5),
        "sgu_w": nrm(ks[9], (L, SGU_HEADS, CHUNK, CHUNK), CHUNK ** -0.5),
        "sgu_b": 1.0 + nrm(ks[10], (L, SGU_HEADS, CHUNK), 0.1),
        "pool_w": nrm(ks[11], (L, POOL_GROUPS, POOL_GROUP_DIM, POOL_GROUP_DIM), POOL_GROUP_DIM ** -0.5),
        "pool_scale": 1.0 + nrm(ks[12], (L, W_GRP), 0.1),
        "conv_w": nrm(ks[13], (L, CONV_WIDTH, W_GRP), CONV_WIDTH ** -0.5),
        "s5_lambda_re": -0.5 + nrm(ks[14], (L, S5_GROUPS, S5_STATE), 0.01),
        "s5_lambda_im": lam_im0 + nrm(ks[15], (L, S5_GROUPS, S5_STATE), 0.01),
        "s5_b_re": nrm(ks[16], (L, S5_GROUPS, S5_STATE, S5_GROUP_CH), (2 * S5_GROUP_CH) ** -0.5),
        "s5_b_im": nrm(ks[17], (L, S5_GROUPS, S5_STATE, S5_GROUP_CH), (2 * S5_GROUP_CH) ** -0.5),
        "s5_c_re": nrm(ks[18], (L, S5_GROUPS, S5_GROUP_CH, S5_STATE), (2 * S5_STATE) ** -0.5),
        "s5_c_im": nrm(ks[19], (L, S5_GROUPS, S5_GROUP_CH, S5_STATE), (2 * S5_STATE) ** -0.5),
        "s5_d": nrm(ks[20], (L, W_GRP), 1.0),
        "s5_log_dt": jax.random.uniform(ks[21], (L, S5_GROUPS), f32, math.log(1e-3), math.log(1e-1)),
        "s5_glu_w": nrm(ks[22], (L, W_GRP, W_GRP), W_GRP ** -0.5),
        "s5_glu_b": nrm(ks[23], (L, W_GRP), 0.01),
        "mix_norm_g": gain(ks[24], (L, D_MIX)),
        "w_mix_out": nrm(ks[25], (L, D_MIX, D), D_MIX ** -0.5),
        "norm3_g": gain(ks[26], (L, D)),
        "ffn2_w_in": nrm(ks[27], (L, D, 2 * D_FF), D ** -0.5),
        "ffn2_w_out": nrm(ks[28], (L, D_FF, D), D_FF ** -0.5),
        "final_norm_g": gain(ks[29], (D,)),
    }


def _fwd_reference(x, c, ada_w, ada_b, norm1_g, ffn1_w_in, ffn1_w_out, norm2_g, w_mix_in,
              sgu_w, sgu_b, pool_w, pool_scale, conv_w, s5_lambda_re, s5_lambda_im,
              s5_b_re, s5_b_im, s5_c_re, s5_c_im, s5_d, s5_log_dt, s5_glu_w, s5_glu_b,
              mix_norm_g, w_mix_out, norm3_g, ffn2_w_in, ffn2_w_out, final_norm_g):
    c_act = jax.nn.silu(c)
    for l in range(DEPTH):
        cond = (c_act @ ada_w[l] + ada_b[l])[:, None, :]
        sh1, sc1, g1, sh2, sc2, g2, sh3, sc3, g3 = jnp.split(cond, N_ADA, axis=-1)

        h = modulate(rmsnorm(x, norm1_g[l]), sh1, sc1)
        x = x + 0.5 * g1 * swiglu(h, ffn1_w_in[l], ffn1_w_out[l])

        h = modulate(rmsnorm(x, norm2_g[l]), sh2, sc2)
        z = h @ w_mix_in[l]
        za, zb, zc, zd = jnp.split(z, [2 * W_GRP, 3 * W_GRP, 6 * W_GRP], axis=-1)
        ya = sgu_mixer(za, sgu_w[l], sgu_b[l])
        yb = pool_mixer(zb, pool_w[l], pool_scale[l])
        yc = conv_mixer(zc, conv_w[l])
        yd = s5_mixer(zd, s5_lambda_re[l], s5_lambda_im[l], s5_b_re[l], s5_b_im[l],
                      s5_c_re[l], s5_c_im[l], s5_d[l], s5_log_dt[l], s5_glu_w[l], s5_glu_b[l])
        y = group_rmsnorm(jnp.concatenate([ya, yb, yc, yd], axis=-1), mix_norm_g[l])
        x = x + g2 * (y @ w_mix_out[l])

        h = modulate(rmsnorm(x, norm3_g[l]), sh3, sc3)
        x = x + 0.5 * g3 * swiglu(h, ffn2_w_in[l], ffn2_w_out[l])
    return rmsnorm(x, final_norm_g)


import jax as _jax
import jax.numpy as _jnp

TWIN_FORMAT = 'train_step'
FWD_PARAMS = ['x', 'c', 'ada_w', 'ada_b', 'norm1_g', 'ffn1_w_in', 'ffn1_w_out', 'norm2_g', 'w_mix_in', 'sgu_w', 'sgu_b', 'pool_w', 'pool_scale', 'conv_w', 's5_lambda_re', 's5_lambda_im', 's5_b_re', 's5_b_im', 's5_c_re', 's5_c_im', 's5_d', 's5_log_dt', 's5_glu_w', 's5_glu_b', 'mix_norm_g', 'w_mix_out', 'norm3_g', 'ffn2_w_in', 'ffn2_w_out', 'final_norm_g']
TWIN_WEIGHTS = ['ada_w', 'ada_b', 'norm1_g', 'ffn1_w_in', 'ffn1_w_out', 'norm2_g', 'w_mix_in', 'sgu_w', 'sgu_b', 'pool_w', 'pool_scale', 'conv_w', 's5_lambda_re', 's5_lambda_im', 's5_b_re', 's5_b_im', 's5_c_re', 's5_c_im', 's5_d', 's5_log_dt', 's5_glu_w', 's5_glu_b', 'mix_norm_g', 'w_mix_out', 'norm3_g', 'ffn2_w_in', 'ffn2_w_out', 'final_norm_g']
TWIN_DIFF_INPUT = 'x'
TWIN_INPUTS = ['x', 'c', 'ada_w', 'ada_b', 'norm1_g', 'ffn1_w_in', 'ffn1_w_out', 'norm2_g', 'w_mix_in', 'sgu_w', 'sgu_b', 'pool_w', 'pool_scale', 'conv_w', 's5_lambda_re', 's5_lambda_im', 's5_b_re', 's5_b_im', 's5_c_re', 's5_c_im', 's5_d', 's5_log_dt', 's5_glu_w', 's5_glu_b', 'mix_norm_g', 'w_mix_out', 'norm3_g', 'ffn2_w_in', 'ffn2_w_out', 'final_norm_g', 'loss_target', 'm_ada_w', 'm_ada_b', 'm_norm1_g', 'm_ffn1_w_in', 'm_ffn1_w_out', 'm_norm2_g', 'm_w_mix_in', 'm_sgu_w', 'm_sgu_b', 'm_pool_w', 'm_pool_scale', 'm_conv_w', 'm_s5_lambda_re', 'm_s5_lambda_im', 'm_s5_b_re', 'm_s5_b_im', 'm_s5_c_re', 'm_s5_c_im', 'm_s5_d', 'm_s5_log_dt', 'm_s5_glu_w', 'm_s5_glu_b', 'm_mix_norm_g', 'm_w_mix_out', 'm_norm3_g', 'm_ffn2_w_in', 'm_ffn2_w_out', 'm_final_norm_g', 'v_ada_w', 'v_ada_b', 'v_norm1_g', 'v_ffn1_w_in', 'v_ffn1_w_out', 'v_norm2_g', 'v_w_mix_in', 'v_sgu_w', 'v_sgu_b', 'v_pool_w', 'v_pool_scale', 'v_conv_w', 'v_s5_lambda_re', 'v_s5_lambda_im', 'v_s5_b_re', 'v_s5_b_im', 'v_s5_c_re', 'v_s5_c_im', 'v_s5_d', 'v_s5_log_dt', 'v_s5_glu_w', 'v_s5_glu_b', 'v_mix_norm_g', 'v_w_mix_out', 'v_norm3_g', 'v_ffn2_w_in', 'v_ffn2_w_out', 'v_final_norm_g']
TWIN_OUTPUTS = ['loss', 'grad_x', 'grad_ada_w', 'grad_ada_b', 'grad_norm1_g', 'grad_ffn1_w_in', 'grad_ffn1_w_out', 'grad_norm2_g', 'grad_w_mix_in', 'grad_sgu_w', 'grad_sgu_b', 'grad_pool_w', 'grad_pool_scale', 'grad_conv_w', 'grad_s5_lambda_re', 'grad_s5_lambda_im', 'grad_s5_b_re', 'grad_s5_b_im', 'grad_s5_c_re', 'grad_s5_c_im', 'grad_s5_d', 'grad_s5_log_dt', 'grad_s5_glu_w', 'grad_s5_glu_b', 'grad_mix_norm_g', 'grad_w_mix_out', 'grad_norm3_g', 'grad_ffn2_w_in', 'grad_ffn2_w_out', 'grad_final_norm_g', 'delta_ada_w', 'delta_ada_b', 'delta_norm1_g', 'delta_ffn1_w_in', 'delta_ffn1_w_out', 'delta_norm2_g', 'delta_w_mix_in', 'delta_sgu_w', 'delta_sgu_b', 'delta_pool_w', 'delta_pool_scale', 'delta_conv_w', 'delta_s5_lambda_re', 'delta_s5_lambda_im', 'delta_s5_b_re', 'delta_s5_b_im', 'delta_s5_c_re', 'delta_s5_c_im', 'delta_s5_d', 'delta_s5_log_dt', 'delta_s5_glu_w', 'delta_s5_glu_b', 'delta_mix_norm_g', 'delta_w_mix_out', 'delta_norm3_g', 'delta_ffn2_w_in', 'delta_ffn2_w_out', 'delta_final_norm_g', 'new_m_ada_w', 'new_m_ada_b', 'new_m_norm1_g', 'new_m_ffn1_w_in', 'new_m_ffn1_w_out', 'new_m_norm2_g', 'new_m_w_mix_in', 'new_m_sgu_w', 'new_m_sgu_b', 'new_m_pool_w', 'new_m_pool_scale', 'new_m_conv_w', 'new_m_s5_lambda_re', 'new_m_s5_lambda_im', 'new_m_s5_b_re', 'new_m_s5_b_im', 'new_m_s5_c_re', 'new_m_s5_c_im', 'new_m_s5_d', 'new_m_s5_log_dt', 'new_m_s5_glu_w', 'new_m_s5_glu_b', 'new_m_mix_norm_g', 'new_m_w_mix_out', 'new_m_norm3_g', 'new_m_ffn2_w_in', 'new_m_ffn2_w_out', 'new_m_final_norm_g', 'new_v_ada_w', 'new_v_ada_b', 'new_v_norm1_g', 'new_v_ffn1_w_in', 'new_v_ffn1_w_out', 'new_v_norm2_g', 'new_v_w_mix_in', 'new_v_sgu_w', 'new_v_sgu_b', 'new_v_pool_w', 'new_v_pool_scale', 'new_v_conv_w', 'new_v_s5_lambda_re', 'new_v_s5_lambda_im', 'new_v_s5_b_re', 'new_v_s5_b_im', 'new_v_s5_c_re', 'new_v_s5_c_im', 'new_v_s5_d', 'new_v_s5_log_dt', 'new_v_s5_glu_w', 'new_v_s5_glu_b', 'new_v_mix_norm_g', 'new_v_w_mix_out', 'new_v_norm3_g', 'new_v_ffn2_w_in', 'new_v_ffn2_w_out', 'new_v_final_norm_g']
TWIN_LEAF_KINDS = {'loss': 'loss', 'grad_x': 'grad_x', 'grad_ada_w': 'grad_w', 'grad_ada_b': 'grad_w', 'grad_norm1_g': 'grad_w', 'grad_ffn1_w_in': 'grad_w', 'grad_ffn1_w_out': 'grad_w', 'grad_norm2_g': 'grad_w', 'grad_w_mix_in': 'grad_w', 'grad_sgu_w': 'grad_w', 'grad_sgu_b': 'grad_w', 'grad_pool_w': 'grad_w', 'grad_pool_scale': 'grad_w', 'grad_conv_w': 'grad_w', 'grad_s5_lambda_re': 'grad_w', 'grad_s5_lambda_im': 'grad_w', 'grad_s5_b_re': 'grad_w', 'grad_s5_b_im': 'grad_w', 'grad_s5_c_re': 'grad_w', 'grad_s5_c_im': 'grad_w', 'grad_s5_d': 'grad_w', 'grad_s5_log_dt': 'grad_w', 'grad_s5_glu_w': 'grad_w', 'grad_s5_glu_b': 'grad_w', 'grad_mix_norm_g': 'grad_w', 'grad_w_mix_out': 'grad_w', 'grad_norm3_g': 'grad_w', 'grad_ffn2_w_in': 'grad_w', 'grad_ffn2_w_out': 'grad_w', 'grad_final_norm_g': 'grad_w', 'delta_ada_w': 'delta_w', 'delta_ada_b': 'delta_w', 'delta_norm1_g': 'delta_w', 'delta_ffn1_w_in': 'delta_w', 'delta_ffn1_w_out': 'delta_w', 'delta_norm2_g': 'delta_w', 'delta_w_mix_in': 'delta_w', 'delta_sgu_w': 'delta_w', 'delta_sgu_b': 'delta_w', 'delta_pool_w': 'delta_w', 'delta_pool_scale': 'delta_w', 'delta_conv_w': 'delta_w', 'delta_s5_lambda_re': 'delta_w', 'delta_s5_lambda_im': 'delta_w', 'delta_s5_b_re': 'delta_w', 'delta_s5_b_im': 'delta_w', 'delta_s5_c_re': 'delta_w', 'delta_s5_c_im': 'delta_w', 'delta_s5_d': 'delta_w', 'delta_s5_log_dt': 'delta_w', 'delta_s5_glu_w': 'delta_w', 'delta_s5_glu_b': 'delta_w', 'delta_mix_norm_g': 'delta_w', 'delta_w_mix_out': 'delta_w', 'delta_norm3_g': 'delta_w', 'delta_ffn2_w_in': 'delta_w', 'delta_ffn2_w_out': 'delta_w', 'delta_final_norm_g': 'delta_w', 'new_m_ada_w': 'new_m', 'new_m_ada_b': 'new_m', 'new_m_norm1_g': 'new_m', 'new_m_ffn1_w_in': 'new_m', 'new_m_ffn1_w_out': 'new_m', 'new_m_norm2_g': 'new_m', 'new_m_w_mix_in': 'new_m', 'new_m_sgu_w': 'new_m', 'new_m_sgu_b': 'new_m', 'new_m_pool_w': 'new_m', 'new_m_pool_scale': 'new_m', 'new_m_conv_w': 'new_m', 'new_m_s5_lambda_re': 'new_m', 'new_m_s5_lambda_im': 'new_m', 'new_m_s5_b_re': 'new_m', 'new_m_s5_b_im': 'new_m', 'new_m_s5_c_re': 'new_m', 'new_m_s5_c_im': 'new_m', 'new_m_s5_d': 'new_m', 'new_m_s5_log_dt': 'new_m', 'new_m_s5_glu_w': 'new_m', 'new_m_s5_glu_b': 'new_m', 'new_m_mix_norm_g': 'new_m', 'new_m_w_mix_out': 'new_m', 'new_m_norm3_g': 'new_m', 'new_m_ffn2_w_in': 'new_m', 'new_m_ffn2_w_out': 'new_m', 'new_m_final_norm_g': 'new_m', 'new_v_ada_w': 'new_v', 'new_v_ada_b': 'new_v', 'new_v_norm1_g': 'new_v', 'new_v_ffn1_w_in': 'new_v', 'new_v_ffn1_w_out': 'new_v', 'new_v_norm2_g': 'new_v', 'new_v_w_mix_in': 'new_v', 'new_v_sgu_w': 'new_v', 'new_v_sgu_b': 'new_v', 'new_v_pool_w': 'new_v', 'new_v_pool_scale': 'new_v', 'new_v_conv_w': 'new_v', 'new_v_s5_lambda_re': 'new_v', 'new_v_s5_lambda_im': 'new_v', 'new_v_s5_b_re': 'new_v', 'new_v_s5_b_im': 'new_v', 'new_v_s5_c_re': 'new_v', 'new_v_s5_c_im': 'new_v', 'new_v_s5_d': 'new_v', 'new_v_s5_log_dt': 'new_v', 'new_v_s5_glu_w': 'new_v', 'new_v_s5_glu_b': 'new_v', 'new_v_mix_norm_g': 'new_v', 'new_v_w_mix_out': 'new_v', 'new_v_norm3_g': 'new_v', 'new_v_ffn2_w_in': 'new_v', 'new_v_ffn2_w_out': 'new_v', 'new_v_final_norm_g': 'new_v'}


def _forward(args):
    return _fwd_reference(*[args[k] for k in FWD_PARAMS])


def _output_shape():
    out = _jax.eval_shape(lambda: _forward(_fwd_setup_inputs(0)))
    return out.shape, out.dtype

N_MICROBATCH = 1
ADAM_LR = 0.001
ADAM_B1 = 0.9
ADAM_B2 = 0.999
ADAM_EPS = 1e-08
ADAM_WD = 0.01
ADAM_STEP = 10
PER_EXAMPLE_BATCH_AXIS = {'x': 0, 'c': 0, 'loss_target': 0}
SHARED_INPUTS = []
_WEIGHT_DTYPES = {'ada_w': _jnp.float32, 'ada_b': _jnp.float32, 'norm1_g': _jnp.float32, 'ffn1_w_in': _jnp.float32, 'ffn1_w_out': _jnp.float32, 'norm2_g': _jnp.float32, 'w_mix_in': _jnp.float32, 'sgu_w': _jnp.float32, 'sgu_b': _jnp.float32, 'pool_w': _jnp.float32, 'pool_scale': _jnp.float32, 'conv_w': _jnp.float32, 's5_lambda_re': _jnp.float32, 's5_lambda_im': _jnp.float32, 's5_b_re': _jnp.float32, 's5_b_im': _jnp.float32, 's5_c_re': _jnp.float32, 's5_c_im': _jnp.float32, 's5_d': _jnp.float32, 's5_log_dt': _jnp.float32, 's5_glu_w': _jnp.float32, 's5_glu_b': _jnp.float32, 'mix_norm_g': _jnp.float32, 'w_mix_out': _jnp.float32, 'norm3_g': _jnp.float32, 'ffn2_w_in': _jnp.float32, 'ffn2_w_out': _jnp.float32, 'final_norm_g': _jnp.float32}
MOMENT_SCALE = {'ada_w': 3.904997e-02, 'ada_b': 6.626340e-02, 'norm1_g': 1.943905e-02, 'ffn1_w_in': 8.569315e-03, 'ffn1_w_out': 1.400892e-02, 'norm2_g': 4.629854e-02, 'w_mix_in': 3.605182e-02, 'sgu_w': 1.428381e-02, 'sgu_b': 2.013675e-02, 'pool_w': 3.676261e-02, 'pool_scale': 3.975065e-02, 'conv_w': 3.634821e-02, 's5_lambda_re': 3.912204e-03, 's5_lambda_im': 3.085373e-03, 's5_b_re': 1.849323e-03, 's5_b_im': 1.845502e-03, 's5_c_re': 3.269580e-03, 's5_c_im': 3.429513e-03, 's5_d': 4.543900e-02, 's5_log_dt': 1.372311e+00, 's5_glu_w': 1.058985e-02, 's5_glu_b': 1.557249e-02, 'mix_norm_g': 3.815738e-02, 'w_mix_out': 3.788630e-02, 'norm3_g': 1.887309e-02, 'ffn2_w_in': 8.063618e-03, 'ffn2_w_out': 1.317997e-02, 'final_norm_g': 1.609784e+01}


def _to_microbatches(a, axis):
    t = _jnp.moveaxis(a, axis, 0)
    t = t.reshape((N_MICROBATCH, t.shape[0] // N_MICROBATCH) + t.shape[1:])
    return _jnp.moveaxis(t, 1, axis + 1)


def setup_inputs(seed: int = 0) -> dict:
    inp = _fwd_setup_inputs(seed)
    key = _jax.random.fold_in(_jax.random.key(seed), 7919)
    shape, _ = _output_shape()
    out = dict(inp)
    out["loss_target"] = _jax.random.normal(_jax.random.fold_in(key, 0), shape, _jnp.float32)
    for i, name in enumerate(TWIN_WEIGHTS):
        w = inp[name].astype(_jnp.float32)
        if MOMENT_SCALE is None:
            s = _jnp.sqrt(_jnp.mean(_jnp.square(w)) + 1e-30)
        else:
            s = MOMENT_SCALE[name]
        km, kv = _jax.random.split(_jax.random.fold_in(key, i + 1))
        out[name] = w
        out["m_" + name] = s * _jax.random.normal(km, w.shape, _jnp.float32)
        out["v_" + name] = (s * s) * _jax.random.uniform(kv, w.shape, _jnp.float32, 0.5, 1.5)
    if N_MICROBATCH > 1:
        for name, axis in PER_EXAMPLE_BATCH_AXIS.items():
            out[name] = _to_microbatches(out[name], axis)
    return {'x': out['x'], 'c': out['c'], 'ada_w': out['ada_w'], 'ada_b': out['ada_b'], 'norm1_g': out['norm1_g'], 'ffn1_w_in': out['ffn1_w_in'], 'ffn1_w_out': out['ffn1_w_out'], 'norm2_g': out['norm2_g'], 'w_mix_in': out['w_mix_in'], 'sgu_w': out['sgu_w'], 'sgu_b': out['sgu_b'], 'pool_w': out['pool_w'], 'pool_scale': out['pool_scale'], 'conv_w': out['conv_w'], 's5_lambda_re': out['s5_lambda_re'], 's5_lambda_im': out['s5_lambda_im'], 's5_b_re': out['s5_b_re'], 's5_b_im': out['s5_b_im'], 's5_c_re': out['s5_c_re'], 's5_c_im': out['s5_c_im'], 's5_d': out['s5_d'], 's5_log_dt': out['s5_log_dt'], 's5_glu_w': out['s5_glu_w'], 's5_glu_b': out['s5_glu_b'], 'mix_norm_g': out['mix_norm_g'], 'w_mix_out': out['w_mix_out'], 'norm3_g': out['norm3_g'], 'ffn2_w_in': out['ffn2_w_in'], 'ffn2_w_out': out['ffn2_w_out'], 'final_norm_g': out['final_norm_g'], 'loss_target': out['loss_target'], 'm_ada_w': out['m_ada_w'], 'm_ada_b': out['m_ada_b'], 'm_norm1_g': out['m_norm1_g'], 'm_ffn1_w_in': out['m_ffn1_w_in'], 'm_ffn1_w_out': out['m_ffn1_w_out'], 'm_norm2_g': out['m_norm2_g'], 'm_w_mix_in': out['m_w_mix_in'], 'm_sgu_w': out['m_sgu_w'], 'm_sgu_b': out['m_sgu_b'], 'm_pool_w': out['m_pool_w'], 'm_pool_scale': out['m_pool_scale'], 'm_conv_w': out['m_conv_w'], 'm_s5_lambda_re': out['m_s5_lambda_re'], 'm_s5_lambda_im': out['m_s5_lambda_im'], 'm_s5_b_re': out['m_s5_b_re'], 'm_s5_b_im': out['m_s5_b_im'], 'm_s5_c_re': out['m_s5_c_re'], 'm_s5_c_im': out['m_s5_c_im'], 'm_s5_d': out['m_s5_d'], 'm_s5_log_dt': out['m_s5_log_dt'], 'm_s5_glu_w': out['m_s5_glu_w'], 'm_s5_glu_b': out['m_s5_glu_b'], 'm_mix_norm_g': out['m_mix_norm_g'], 'm_w_mix_out': out['m_w_mix_out'], 'm_norm3_g': out['m_norm3_g'], 'm_ffn2_w_in': out['m_ffn2_w_in'], 'm_ffn2_w_out': out['m_ffn2_w_out'], 'm_final_norm_g': out['m_final_norm_g'], 'v_ada_w': out['v_ada_w'], 'v_ada_b': out['v_ada_b'], 'v_norm1_g': out['v_norm1_g'], 'v_ffn1_w_in': out['v_ffn1_w_in'], 'v_ffn1_w_out': out['v_ffn1_w_out'], 'v_norm2_g': out['v_norm2_g'], 'v_w_mix_in': out['v_w_mix_in'], 'v_sgu_w': out['v_sgu_w'], 'v_sgu_b': out['v_sgu_b'], 'v_pool_w': out['v_pool_w'], 'v_pool_scale': out['v_pool_scale'], 'v_conv_w': out['v_conv_w'], 'v_s5_lambda_re': out['v_s5_lambda_re'], 'v_s5_lambda_im': out['v_s5_lambda_im'], 'v_s5_b_re': out['v_s5_b_re'], 'v_s5_b_im': out['v_s5_b_im'], 'v_s5_c_re': out['v_s5_c_re'], 'v_s5_c_im': out['v_s5_c_im'], 'v_s5_d': out['v_s5_d'], 'v_s5_log_dt': out['v_s5_log_dt'], 'v_s5_glu_w': out['v_s5_glu_w'], 'v_s5_glu_b': out['v_s5_glu_b'], 'v_mix_norm_g': out['v_mix_norm_g'], 'v_w_mix_out': out['v_w_mix_out'], 'v_norm3_g': out['v_norm3_g'], 'v_ffn2_w_in': out['v_ffn2_w_in'], 'v_ffn2_w_out': out['v_ffn2_w_out'], 'v_final_norm_g': out['v_final_norm_g']}


def _loss(weights, diff, rest, loss_target):
    with _jax.named_scope("forward"):
        args = {**rest, TWIN_DIFF_INPUT: diff, **{k: w.astype(_WEIGHT_DTYPES[k]) for k, w in weights.items()}}
        y = _forward(args)
    with _jax.named_scope("loss_head"):
        err = _jnp.square(y.astype(_jnp.float32) - loss_target)
        return 0.5 * _jnp.sum(_jnp.mean(err, axis=-1)) if err.ndim else 0.5 * err


def _adamw(w, g, m, v):
    m = ADAM_B1 * m + (1.0 - ADAM_B1) * g
    v = ADAM_B2 * v + (1.0 - ADAM_B2) * _jnp.square(g)
    m_hat = m / (1.0 - ADAM_B1 ** ADAM_STEP)
    v_hat = v / (1.0 - ADAM_B2 ** ADAM_STEP)
    delta = -ADAM_LR * (m_hat / (_jnp.sqrt(v_hat) + ADAM_EPS) + ADAM_WD * w)
    return delta, m, v


def reference(x, c, ada_w, ada_b, norm1_g, ffn1_w_in, ffn1_w_out, norm2_g, w_mix_in, sgu_w, sgu_b, pool_w, pool_scale, conv_w, s5_lambda_re, s5_lambda_im, s5_b_re, s5_b_im, s5_c_re, s5_c_im, s5_d, s5_log_dt, s5_glu_w, s5_glu_b, mix_norm_g, w_mix_out, norm3_g, ffn2_w_in, ffn2_w_out, final_norm_g, loss_target, m_ada_w, m_ada_b, m_norm1_g, m_ffn1_w_in, m_ffn1_w_out, m_norm2_g, m_w_mix_in, m_sgu_w, m_sgu_b, m_pool_w, m_pool_scale, m_conv_w, m_s5_lambda_re, m_s5_lambda_im, m_s5_b_re, m_s5_b_im, m_s5_c_re, m_s5_c_im, m_s5_d, m_s5_log_dt, m_s5_glu_w, m_s5_glu_b, m_mix_norm_g, m_w_mix_out, m_norm3_g, m_ffn2_w_in, m_ffn2_w_out, m_final_norm_g, v_ada_w, v_ada_b, v_norm1_g, v_ffn1_w_in, v_ffn1_w_out, v_norm2_g, v_w_mix_in, v_sgu_w, v_sgu_b, v_pool_w, v_pool_scale, v_conv_w, v_s5_lambda_re, v_s5_lambda_im, v_s5_b_re, v_s5_b_im, v_s5_c_re, v_s5_c_im, v_s5_d, v_s5_log_dt, v_s5_glu_w, v_s5_glu_b, v_mix_norm_g, v_w_mix_out, v_norm3_g, v_ffn2_w_in, v_ffn2_w_out, v_final_norm_g):
    given = dict(x=x, c=c, ada_w=ada_w, ada_b=ada_b, norm1_g=norm1_g, ffn1_w_in=ffn1_w_in, ffn1_w_out=ffn1_w_out, norm2_g=norm2_g, w_mix_in=w_mix_in, sgu_w=sgu_w, sgu_b=sgu_b, pool_w=pool_w, pool_scale=pool_scale, conv_w=conv_w, s5_lambda_re=s5_lambda_re, s5_lambda_im=s5_lambda_im, s5_b_re=s5_b_re, s5_b_im=s5_b_im, s5_c_re=s5_c_re, s5_c_im=s5_c_im, s5_d=s5_d, s5_log_dt=s5_log_dt, s5_glu_w=s5_glu_w, s5_glu_b=s5_glu_b, mix_norm_g=mix_norm_g, w_mix_out=w_mix_out, norm3_g=norm3_g, ffn2_w_in=ffn2_w_in, ffn2_w_out=ffn2_w_out, final_norm_g=final_norm_g, loss_target=loss_target, m_ada_w=m_ada_w, m_ada_b=m_ada_b, m_norm1_g=m_norm1_g, m_ffn1_w_in=m_ffn1_w_in, m_ffn1_w_out=m_ffn1_w_out, m_norm2_g=m_norm2_g, m_w_mix_in=m_w_mix_in, m_sgu_w=m_sgu_w, m_sgu_b=m_sgu_b, m_pool_w=m_pool_w, m_pool_scale=m_pool_scale, m_conv_w=m_conv_w, m_s5_lambda_re=m_s5_lambda_re, m_s5_lambda_im=m_s5_lambda_im, m_s5_b_re=m_s5_b_re, m_s5_b_im=m_s5_b_im, m_s5_c_re=m_s5_c_re, m_s5_c_im=m_s5_c_im, m_s5_d=m_s5_d, m_s5_log_dt=m_s5_log_dt, m_s5_glu_w=m_s5_glu_w, m_s5_glu_b=m_s5_glu_b, m_mix_norm_g=m_mix_norm_g, m_w_mix_out=m_w_mix_out, m_norm3_g=m_norm3_g, m_ffn2_w_in=m_ffn2_w_in, m_ffn2_w_out=m_ffn2_w_out, m_final_norm_g=m_final_norm_g, v_ada_w=v_ada_w, v_ada_b=v_ada_b, v_norm1_g=v_norm1_g, v_ffn1_w_in=v_ffn1_w_in, v_ffn1_w_out=v_ffn1_w_out, v_norm2_g=v_norm2_g, v_w_mix_in=v_w_mix_in, v_sgu_w=v_sgu_w, v_sgu_b=v_sgu_b, v_pool_w=v_pool_w, v_pool_scale=v_pool_scale, v_conv_w=v_conv_w, v_s5_lambda_re=v_s5_lambda_re, v_s5_lambda_im=v_s5_lambda_im, v_s5_b_re=v_s5_b_re, v_s5_b_im=v_s5_b_im, v_s5_c_re=v_s5_c_re, v_s5_c_im=v_s5_c_im, v_s5_d=v_s5_d, v_s5_log_dt=v_s5_log_dt, v_s5_glu_w=v_s5_glu_w, v_s5_glu_b=v_s5_glu_b, v_mix_norm_g=v_mix_norm_g, v_w_mix_out=v_w_mix_out, v_norm3_g=v_norm3_g, v_ffn2_w_in=v_ffn2_w_in, v_ffn2_w_out=v_ffn2_w_out, v_final_norm_g=v_final_norm_g)
    weights = {n: given[n] for n in TWIN_WEIGHTS}
    shared = {n: given[n] for n in SHARED_INPUTS}
    per_example = {n: given[n] for n in ['x', 'c']}
    grad_fn = _jax.value_and_grad(_loss, argnums=(0, 1))

    def one_microbatch(ex, loss_target):
        ex = dict(ex)
        diff = ex.pop(TWIN_DIFF_INPUT)
        return grad_fn(weights, diff, {**shared, **ex}, loss_target)

    if N_MICROBATCH == 1:
        loss, (grad_w, grad_x) = one_microbatch(per_example, given["loss_target"])
    else:
        def body(carry, xs):
            loss_sum, grad_sum = carry
            l_k, (gw_k, gx_k) = one_microbatch(xs[0], xs[1])
            with _jax.named_scope("update"):
                return (loss_sum + l_k, _jax.tree.map(_jnp.add, grad_sum, gw_k)), gx_k

        init = (_jnp.zeros((), _jnp.float32), _jax.tree.map(_jnp.zeros_like, weights))
        (loss, grad_w), grad_x = _jax.lax.scan(body, init, (per_example, given["loss_target"]))
    with _jax.named_scope("update"):
        delta_w, new_m, new_v = {}, {}, {}
        for n in TWIN_WEIGHTS:
            delta_w[n], new_m[n], new_v[n] = _adamw(weights[n], grad_w[n], given["m_" + n], given["v_" + n])
    return (loss, grad_x, *[grad_w[n] for n in TWIN_WEIGHTS], *[delta_w[n] for n in TWIN_WEIGHTS],
            *[new_m[n] for n in TWIN_WEIGHTS], *[new_v[n] for n in TWIN_WEIGHTS])
```

```python
import functools
import math

import jax
import jax.numpy as jnp
from jax import lax
from jax.experimental import pallas as pl
from jax.experimental.pallas import tpu as pltpu

F32 = jnp.float32
BF16 = jnp.bfloat16

D_MODEL = 1024
SEQ = 2048
DEPTH = 4
N_DEV = 8
W_GRP = 256
CHUNK = 128
SGU_HEADS = 4
SGU_HEAD_DIM = 64
POOL_WINDOWS = (2, 4, 8, 16)
POOL_GROUP_DIM = 64
S5_GROUPS = 16
S5_GROUP_CH = 16
S5_STATE = 64
S5_LANES = S5_GROUPS * S5_STATE
P_IN = 1792
D_FF = 2816
FF_PIECE = 2 * D_FF // N_DEV
N_FF_CHUNK = D_FF // FF_PIECE
N_ADA = 9
EPS = 1e-6
ADAM_LR = 0.001
ADAM_B1 = 0.9
ADAM_B2 = 0.999
ADAM_EPS = 1e-08
ADAM_WD = 0.01
ADAM_STEP = 10

SUBLANES = 8
LANES = 128
VMEM_LIMIT = 56 * 1024 * 1024
TOKEN_TILE = 512
HIGHEST = lax.Precision.HIGHEST
MESH_AXES = ("x", "y", "c")

_GELU_C = math.sqrt(2.0 / math.pi)
_GELU_A = 0.044715


def _params(*sem):
    return pltpu.CompilerParams(dimension_semantics=tuple(sem) if sem else None, vmem_limit_bytes=VMEM_LIMIT)


def _dot(a, b):
    return jnp.dot(a, b, preferred_element_type=F32)


def _dot_nt(a, b):
    return lax.dot_general(a, b, (((1,), (1,)), ((), ())), preferred_element_type=F32)


def _dot_tn(a, b):
    return lax.dot_general(a, b, (((0,), (0,)), ((), ())), preferred_element_type=F32)


def _dot_hi(a, b):
    return jnp.dot(a, b, preferred_element_type=F32, precision=HIGHEST)


def _sigmoid(x):
    return 1.0 / (1.0 + jnp.exp(-x))


def _gelu(x):
    return 0.5 * x * (1.0 + jnp.tanh(_GELU_C * (x + _GELU_A * x * x * x)))


def _gelu_grad(x):
    t = jnp.tanh(_GELU_C * (x + _GELU_A * x * x * x))
    return 0.5 * (1.0 + t) + 0.5 * x * (1.0 - t * t) * (_GELU_C * (1.0 + 3.0 * _GELU_A * x * x))


def _rms(x):
    r = lax.rsqrt(jnp.mean(x * x, axis=-1, keepdims=True) + EPS)
    return x * r, r


def _rms_bwd(xhat, r, dxhat):
    return r * (dxhat - xhat * jnp.mean(dxhat * xhat, axis=-1, keepdims=True))


def _sum0(x):
    return jnp.sum(x, axis=0, keepdims=True)


def _me():
    return 4 * lax.axis_index("x") + 2 * lax.axis_index("y") + lax.axis_index("c")


def _exchange(srcs, scatter, name):
    n = len(srcs)
    out_shapes = []
    for s in srcs:
        piece = s.shape[1:] if scatter else s.shape
        out_shapes.append(jax.ShapeDtypeStruct((N_DEV,) + tuple(piece), s.dtype))

    def body(*refs):
        ins, outs = refs[:n], refs[n:2 * n]
        send_sems, recv_sems, local_sems = refs[2 * n:]
        x, y, c = lax.axis_index("x"), lax.axis_index("y"), lax.axis_index("c")
        me = 4 * x + 2 * y + c

        def src_of(i, dev):
            return ins[i].at[dev] if scatter else ins[i]

        local = [pltpu.make_async_copy(src_of(i, me), outs[i].at[me], local_sems.at[i]) for i in range(n)]
        for cp in local:
            cp.start()
        sends, recvs = [], []
        for k in range(1, N_DEV):
            px = 1 - x if (k >> 2) & 1 else x
            py = 1 - y if (k >> 1) & 1 else y
            pc = 1 - c if k & 1 else c
            peer = 4 * px + 2 * py + pc
            for i in range(n):
                sends.append(pltpu.make_async_remote_copy(
                    src_ref=src_of(i, peer), dst_ref=outs[i].at[me],
                    send_sem=send_sems.at[k - 1, i], recv_sem=recv_sems.at[k - 1, i],
                    device_id=(px, py, pc), device_id_type=pl.DeviceIdType.MESH))
                recvs.append(pltpu.make_async_remote_copy(
                    src_ref=src_of(i, peer), dst_ref=outs[i].at[peer],
                    send_sem=send_sems.at[k - 1, i], recv_sem=recv_sems.at[k - 1, i],
                    device_id=(px, py, pc), device_id_type=pl.DeviceIdType.MESH))
        for cp in sends:
            cp.start()
        for cp in recvs:
            cp.wait_recv()
        for cp in sends:
            cp.wait_send()
        for cp in local:
            cp.wait()

    hbm = pl.BlockSpec(memory_space=pltpu.HBM)
    return pl.pallas_call(
        body, name=name, out_shape=out_shapes,
        in_specs=[hbm] * n, out_specs=[hbm] * n,
        scratch_shapes=[pltpu.SemaphoreType.DMA((N_DEV - 1, n)), pltpu.SemaphoreType.DMA((N_DEV - 1, n)),
                        pltpu.SemaphoreType.DMA((n,))],
    )(*srcs)


def _cond_fwd(c_all, ada_w, ada_b_mine):
    ncol = ada_w.shape[-1]

    def body(c_ref, w_ref, b_ref, o_ref):
        c = c_ref[...]
        ca = (c * _sigmoid(c)).astype(BF16)
        o_ref[0] = _dot(ca, w_ref[0].astype(BF16)) + b_ref[0]

    return pl.pallas_call(
        body, name="cond_fwd", grid=(DEPTH,),
        out_shape=jax.ShapeDtypeStruct((DEPTH, N_DEV, ncol), F32),
        in_specs=[pl.BlockSpec((N_DEV, D_MODEL), lambda l: (0, 0)),
                  pl.BlockSpec((1, D_MODEL, ncol), lambda l: (l, 0, 0)),
                  pl.BlockSpec((1, 1, ncol), lambda l: (l, 0, 0))],
        out_specs=pl.BlockSpec((1, N_DEV, ncol), lambda l: (l, 0, 0)),
        compiler_params=_params("arbitrary"),
    )(c_all, ada_w, ada_b_mine)


def _cond_bwd(c_all_t, dcond_mine, dcond_all):
    ncol = dcond_mine.shape[-1]
    nall = dcond_all.shape[-1]

    def body(ct_ref, d_ref, da_ref, gw_ref, gb_ref):
        ct = ct_ref[...]
        ct = ct * _sigmoid(ct)
        d = d_ref[0]
        acc = ct[:, 0:1] * d[0:1, :]
        for b in range(1, N_DEV):
            acc = acc + ct[:, b:b + 1] * d[b:b + 1, :]
        gw_ref[0] = acc
        gb_ref[0] = _sum0(da_ref[0])

    return pl.pallas_call(
        body, name="cond_bwd", grid=(DEPTH,),
        out_shape=(jax.ShapeDtypeStruct((DEPTH, D_MODEL, ncol), F32), jax.ShapeDtypeStruct((DEPTH, 1, nall), F32)),
        in_specs=[pl.BlockSpec((D_MODEL, N_DEV), lambda l: (0, 0)),
                  pl.BlockSpec((1, N_DEV, ncol), lambda l: (l, 0, 0)),
                  pl.BlockSpec((1, N_DEV, nall), lambda l: (l, 0, 0))],
        out_specs=(pl.BlockSpec((1, D_MODEL, ncol), lambda l: (l, 0, 0)),
                   pl.BlockSpec((1, 1, nall), lambda l: (l, 0, 0))),
        compiler_params=_params("arbitrary"),
    )(c_all_t, dcond_mine, dcond_all)


def _modnorm(x, g, shift, scale):
    xhat, r = _rms(x)
    return (xhat * g) * (1.0 + scale) + shift, xhat, r


def _modnorm_bwd(xhat, r, g, scale, dh):
    n = xhat * g
    dn = dh * (1.0 + scale)
    dx = _rms_bwd(xhat, r, dn * g)
    return dx, _sum0(dh), _sum0(dh * n), _sum0(dn * xhat)


def _row_spec(rows):
    return pl.BlockSpec((rows, D_MODEL), lambda *_: (0, 0))


def _ffn_fwd(x, cond3, g, w_in_g, w_out_g):
    tm = TOKEN_TILE
    last = N_FF_CHUNK - 1

    def body(x_ref, cond_ref, g_ref, wa_ref, wb_ref, wo_ref, xo_ref, f_ref, h_scr, acc_scr):
        j = pl.program_id(1)

        @pl.when(j == 0)
        def _():
            h, _, _ = _modnorm(x_ref[...], g_ref[...], cond_ref[0:1, :], cond_ref[1:2, :])
            h_scr[...] = h.astype(BF16)
            acc_scr[...] = jnp.zeros_like(acc_scr)

        h = h_scr[...]
        a = _dot(h, wa_ref[0])
        b = _dot(h, wb_ref[0])
        act = (a * _sigmoid(a)) * b
        acc_scr[...] += _dot(act.astype(BF16), wo_ref[0])

        @pl.when(j == last)
        def _():
            f = acc_scr[...]
            f_ref[...] = f
            xo_ref[...] = x_ref[...] + (0.5 * cond_ref[2:3, :]) * f

    tok = pl.BlockSpec((tm, D_MODEL), lambda i, j: (i, 0))
    return pl.pallas_call(
        body, name="ffn_fwd", grid=(SEQ // tm, N_FF_CHUNK),
        out_shape=(jax.ShapeDtypeStruct((SEQ, D_MODEL), F32), jax.ShapeDtypeStruct((SEQ, D_MODEL), F32)),
        in_specs=[tok, _row_spec(3), _row_spec(1),
                  pl.BlockSpec((1, D_MODEL, FF_PIECE), lambda i, j: (j, 0, 0)),
                  pl.BlockSpec((1, D_MODEL, FF_PIECE), lambda i, j: (j + N_FF_CHUNK, 0, 0)),
                  pl.BlockSpec((1, FF_PIECE, D_MODEL), lambda i, j: (j, 0, 0))],
        out_specs=(tok, tok),
        scratch_shapes=[pltpu.VMEM((tm, D_MODEL), BF16), pltpu.VMEM((tm, D_MODEL), F32)],
        compiler_params=_params("arbitrary", "arbitrary"),
    )(x, cond3, g, w_in_g, w_in_g, w_out_g)


def _ffn_bwd(dy, x, f, cond3, g, w_in_g, w_out_g):
    tm = TOKEN_TILE
    last = N_FF_CHUNK - 1

    def body(dy_ref, x_ref, f_ref, cond_ref, g_ref, wa_ref, wb_ref, wo_ref,
             dx_ref, da_ref, db_ref, act_ref, h_ref, do_ref, part_ref, h_scr, do_scr, dh_scr):
        i, j = pl.program_id(0), pl.program_id(1)

        @pl.when(j == 0)
        def _():
            h, _, _ = _modnorm(x_ref[...], g_ref[...], cond_ref[0:1, :], cond_ref[1:2, :])
            hb = h.astype(BF16)
            h_scr[...] = hb
            h_ref[...] = hb
            do = ((0.5 * cond_ref[2:3, :]) * dy_ref[...]).astype(BF16)
            do_scr[...] = do
            do_ref[...] = do
            dh_scr[...] = jnp.zeros_like(dh_scr)

        @pl.when((i == 0) & (j == 0))
        def _():
            part_ref[...] = jnp.zeros_like(part_ref)

        h = h_scr[...]
        do = do_scr[...]
        a = _dot(h, wa_ref[0])
        b = _dot(h, wb_ref[0])
        dact = _dot_nt(do, wo_ref[0])
        sig = _sigmoid(a)
        s = a * sig
        da = (dact * b * (sig * (1.0 + a * (1.0 - sig)))).astype(BF16)
        db = (dact * s).astype(BF16)
        da_ref[0] = da
        db_ref[0] = db
        act_ref[0] = (s * b).astype(BF16)
        dh_scr[...] += _dot_nt(da, wa_ref[0]) + _dot_nt(db, wb_ref[0])

        @pl.when(j == last)
        def _():
            dyv = dy_ref[...]
            xhat, r = _rms(x_ref[...])
            dx, dshift, dscale, dg = _modnorm_bwd(xhat, r, g_ref[...], cond_ref[1:2, :], dh_scr[...])
            dx_ref[...] = dyv + dx
            part_ref[0:1, :] += dshift
            part_ref[1:2, :] += dscale
            part_ref[2:3, :] += _sum0(0.5 * dyv * f_ref[...])
            part_ref[3:4, :] += dg

    tok = pl.BlockSpec((tm, D_MODEL), lambda i, j: (i, 0))
    chunk = pl.BlockSpec((1, tm, FF_PIECE), lambda i, j: (j, i, 0))
    chunk_shape = jax.ShapeDtypeStruct((N_FF_CHUNK, SEQ, FF_PIECE), BF16)
    return pl.pallas_call(
        body, name="ffn_bwd", grid=(SEQ // tm, N_FF_CHUNK),
        out_shape=(jax.ShapeDtypeStruct((SEQ, D_MODEL), F32), chunk_shape, chunk_shape, chunk_shape,
                   jax.ShapeDtypeStruct((SEQ, D_MODEL), BF16), jax.ShapeDtypeStruct((SEQ, D_MODEL), BF16),
                   jax.ShapeDtypeStruct((SUBLANES, D_MODEL), F32)),
        in_specs=[tok, tok, tok, _row_spec(3), _row_spec(1),
                  pl.BlockSpec((1, D_MODEL, FF_PIECE), lambda i, j: (j, 0, 0)),
                  pl.BlockSpec((1, D_MODEL, FF_PIECE), lambda i, j: (j + N_FF_CHUNK, 0, 0)),
                  pl.BlockSpec((1, FF_PIECE, D_MODEL), lambda i, j: (j, 0, 0))],
        out_specs=(tok, chunk, chunk, chunk, tok, tok, _row_spec(SUBLANES)),
        scratch_shapes=[pltpu.VMEM((tm, D_MODEL), BF16), pltpu.VMEM((tm, D_MODEL), BF16),
                        pltpu.VMEM((tm, D_MODEL), F32)],
        compiler_params=_params("arbitrary", "arbitrary"),
    )(dy, x, f, cond3, g, w_in_g, w_in_g, w_out_g)


def _dw(lhs, rhs_a, rhs_b=None, name="dw"):
    pl_, s, m = lhs.shape
    pa, _, nn = rhs_a.shape
    pb = 0 if rhs_b is None else rhs_b.shape[0]
    p_out = max(pl_, pa + pb)
    tmm = 512 if m % 512 == 0 else m
    two = rhs_b is not None

    def body(*refs):
        if two:
            l_ref, a_ref, b_ref, o_ref = refs
        else:
            l_ref, a_ref, o_ref = refs
        p = pl.program_id(0)
        if two:
            @pl.when(p < pa)
            def _():
                o_ref[0] = _dot_tn(l_ref[0], a_ref[0]).astype(BF16)

            @pl.when(p >= pa)
            def _():
                o_ref[0] = _dot_tn(l_ref[0], b_ref[0]).astype(BF16)
        else:
            o_ref[0] = _dot_tn(l_ref[0], a_ref[0]).astype(BF16)

    in_specs = [pl.BlockSpec((1, s, tmm), (lambda p, q: (p, 0, q)) if pl_ > 1 else (lambda p, q: (0, 0, q)))]
    if two:
        in_specs.append(pl.BlockSpec((1, s, nn), lambda p, q: (jnp.minimum(p, pa - 1), 0, 0)))
        in_specs.append(pl.BlockSpec((1, s, nn), lambda p, q: (jnp.maximum(p - pa, 0), 0, 0)))
        args = (lhs, rhs_a, rhs_b)
    else:
        in_specs.append(pl.BlockSpec((1, s, nn), (lambda p, q: (p, 0, 0)) if pa > 1 else (lambda p, q: (0, 0, 0))))
        args = (lhs, rhs_a)
    return pl.pallas_call(
        body, name=name, grid=(p_out, m // tmm),
        out_shape=jax.ShapeDtypeStruct((p_out, m, nn), BF16),
        in_specs=in_specs, out_specs=pl.BlockSpec((1, tmm, nn), lambda p, q: (p, q, 0)),
        compiler_params=_params("arbitrary", "arbitrary"),
    )(*args)


def _mix_in_fwd(x, cond2, g, w):
    tm = TOKEN_TILE

    def body(x_ref, cond_ref, g_ref, w_ref, z_ref):
        h, _, _ = _modnorm(x_ref[...], g_ref[...], cond_ref[0:1, :], cond_ref[1:2, :])
        z_ref[...] = _dot(h.astype(BF16), w_ref[...])

    return pl.pallas_call(
        body, name="mix_in_fwd", grid=(SEQ // tm,),
        out_shape=jax.ShapeDtypeStruct((SEQ, P_IN), F32),
        in_specs=[pl.BlockSpec((tm, D_MODEL), lambda i: (i, 0)), _row_spec(2), _row_spec(1),
                  pl.BlockSpec((D_MODEL, P_IN), lambda i: (0, 0))],
        out_specs=pl.BlockSpec((tm, P_IN), lambda i: (i, 0)),
        compiler_params=_params("arbitrary"),
    )(x, cond2, g, w)


def _mix_in_bwd(dz, x, dy, cond2, g, w):
    tm = TOKEN_TILE

    def body(dz_ref, x_ref, dy_ref, cond_ref, g_ref, w_ref, dx_ref, h_ref, dzb_ref, part_ref):
        i = pl.program_id(0)

        @pl.when(i == 0)
        def _():
            part_ref[...] = jnp.zeros_like(part_ref)

        h, xhat, r = _modnorm(x_ref[...], g_ref[...], cond_ref[0:1, :], cond_ref[1:2, :])
        h_ref[...] = h.astype(BF16)
        dzb = dz_ref[...].astype(BF16)
        dzb_ref[...] = dzb
        dh = _dot_nt(dzb, w_ref[...])
        dx, dshift, dscale, dg = _modnorm_bwd(xhat, r, g_ref[...], cond_ref[1:2, :], dh)
        dx_ref[...] = dy_ref[...] + dx
        part_ref[0:1, :] += dshift
        part_ref[1:2, :] += dscale
        part_ref[2:3, :] += dg

    tok = pl.BlockSpec((tm, D_MODEL), lambda i: (i, 0))
    ztok = pl.BlockSpec((tm, P_IN), lambda i: (i, 0))
    return pl.pallas_call(
        body, name="mix_in_bwd", grid=(SEQ // tm,),
        out_shape=(jax.ShapeDtypeStruct((SEQ, D_MODEL), F32), jax.ShapeDtypeStruct((SEQ, D_MODEL), BF16),
                   jax.ShapeDtypeStruct((SEQ, P_IN), BF16), jax.ShapeDtypeStruct((SUBLANES, D_MODEL), F32)),
        in_specs=[ztok, tok, tok, _row_spec(2), _row_spec(1), pl.BlockSpec((D_MODEL, P_IN), lambda i: (0, 0))],
        out_specs=(tok, tok, ztok, _row_spec(SUBLANES)),
        compiler_params=_params("arbitrary"),
    )(dz, x, dy, cond2, g, w)


def _group_norm(ys, g_ref):
    out = []
    for k, y in enumerate(ys):
        yhat, r = _rms(y)
        out.append((yhat, r, g_ref[:, k * W_GRP:(k + 1) * W_GRP]))
    return out


def _mix_out_fwd(x, ys, g, gate, w):
    tm = TOKEN_TILE

    def body(x_ref, ya_ref, yb_ref, yc_ref, yd_ref, g_ref, gate_ref, w_ref, xo_ref):
        acc = None
        for k, (yhat, _, gk) in enumerate(_group_norm([r[...] for r in (ya_ref, yb_ref, yc_ref, yd_ref)], g_ref)):
            t = _dot((yhat * gk).astype(BF16), w_ref[k * W_GRP:(k + 1) * W_GRP, :])
            acc = t if acc is None else acc + t
        xo_ref[...] = x_ref[...] + gate_ref[...] * acc

    tok = pl.BlockSpec((tm, D_MODEL), lambda i: (i, 0))
    ytok = pl.BlockSpec((tm, W_GRP), lambda i: (i, 0))
    return pl.pallas_call(
        body, name="mix_out_fwd", grid=(SEQ // tm,),
        out_shape=jax.ShapeDtypeStruct((SEQ, D_MODEL), F32),
        in_specs=[tok, ytok, ytok, ytok, ytok, _row_spec(1), _row_spec(1),
                  pl.BlockSpec((D_MODEL, D_MODEL), lambda i: (0, 0))],
        out_specs=tok, compiler_params=_params("arbitrary"),
    )(x, *ys, g, gate, w)


def _mix_out_bwd(dy, ys, g, gate, w):
    tm = TOKEN_TILE

    def body(dy_ref, ya_ref, yb_ref, yc_ref, yd_ref, g_ref, gate_ref, w_ref,
             da_ref, db_ref, dc_ref, dd_ref, yn_ref, dmo_ref, part_ref):
        i = pl.program_id(0)

        @pl.when(i == 0)
        def _():
            part_ref[...] = jnp.zeros_like(part_ref)

        dyv = dy_ref[...]
        dmo = (gate_ref[...] * dyv).astype(BF16)
        dmo_ref[...] = dmo
        dyn = _dot_nt(dmo, w_ref[...])
        norms = _group_norm([r[...] for r in (ya_ref, yb_ref, yc_ref, yd_ref)], g_ref)
        mo = None
        for k, ((yhat, r, gk), o_ref) in enumerate(zip(norms, (da_ref, db_ref, dc_ref, dd_ref))):
            sl = slice(k * W_GRP, (k + 1) * W_GRP)
            ynk = (yhat * gk).astype(BF16)
            yn_ref[:, sl] = ynk
            t = _dot(ynk, w_ref[sl, :])
            mo = t if mo is None else mo + t
            dk = dyn[:, sl]
            o_ref[...] = _rms_bwd(yhat, r, dk * gk)
            part_ref[1:2, sl] += _sum0(dk * yhat)
        part_ref[0:1, :] += _sum0(dyv * mo)

    tok = pl.BlockSpec((tm, D_MODEL), lambda i: (i, 0))
    ytok = pl.BlockSpec((tm, W_GRP), lambda i: (i, 0))
    ysh = jax.ShapeDtypeStruct((SEQ, W_GRP), F32)
    return pl.pallas_call(
        body, name="mix_out_bwd", grid=(SEQ // tm,),
        out_shape=(ysh, ysh, ysh, ysh, jax.ShapeDtypeStruct((SEQ, D_MODEL), BF16),
                   jax.ShapeDtypeStruct((SEQ, D_MODEL), BF16), jax.ShapeDtypeStruct((SUBLANES, D_MODEL), F32)),
        in_specs=[tok, ytok, ytok, ytok, ytok, _row_spec(1), _row_spec(1),
                  pl.BlockSpec((D_MODEL, D_MODEL), lambda i: (0, 0))],
        out_specs=(ytok, ytok, ytok, ytok, tok, tok, _row_spec(SUBLANES)),
        compiler_params=_params("arbitrary"),
    )(dy, *ys, g, gate, w)


def _shift_down(v, k, rows):
    return jnp.where(rows >= k, pltpu.roll(v, k, axis=0), 0.0)


def _shift_up(v, k, rows):
    n = v.shape[0]
    return jnp.where(rows < n - k, pltpu.roll(v, n - k, axis=0), 0.0)


def _zslab(width, index):
    return pl.BlockSpec((SEQ, width), lambda *_: (0, index))


def _full(shape):
    return pl.BlockSpec(shape, lambda *_: (0,) * len(shape))


def _head_avg():
    r = lax.broadcasted_iota(jnp.int32, (W_GRP, W_GRP), 0) // SGU_HEAD_DIM
    c = lax.broadcasted_iota(jnp.int32, (W_GRP, W_GRP), 1) // SGU_HEAD_DIM
    return jnp.where(r == c, 1.0 / SGU_HEAD_DIM, 0.0).astype(F32)


def _sgu_norm(za):
    z = _gelu(za)
    u, v = z[:, :W_GRP], z[:, W_GRP:]
    avg = _head_avg()
    vc = v - _dot_hi(v, avg)
    rstd = lax.rsqrt(_dot_hi(vc * vc, avg) + EPS)
    return u, vc * rstd, rstd


def _sgu_masked_w(w_ref):
    t = lax.broadcasted_iota(jnp.int32, (CHUNK, CHUNK), 0)
    s = lax.broadcasted_iota(jnp.int32, (CHUNK, CHUNK), 1)
    tril = t >= s
    return [jnp.where(tril, w_ref[:, h * CHUNK:(h + 1) * CHUNK], 0.0).astype(BF16) for h in range(SGU_HEADS)]


def _head_of_lane():
    return lax.broadcasted_iota(jnp.int32, (CHUNK, W_GRP), 1) // SGU_HEAD_DIM


def _sgu_fwd(z, w_cat, bias):
    def body(z_ref, w_ref, b_ref, y_ref, vn_scr, u_scr):
        u, vn, _ = _sgu_norm(z_ref[...])
        vn_scr[...] = vn.astype(BF16)
        u_scr[...] = u
        ws = _sgu_masked_w(w_ref)
        head = _head_of_lane()
        bias_v = b_ref[...]

        def chunk(n, carry):
            rows = pl.ds(pl.multiple_of(n * CHUNK, CHUNK), CHUNK)
            vb = vn_scr[rows, :]
            mixed = bias_v
            for h in range(SGU_HEADS):
                mixed = mixed + jnp.where(head == h, _dot(ws[h], vb), 0.0)
            y_ref[rows, :] = u_scr[rows, :] * mixed
            return carry

        lax.fori_loop(0, SEQ // CHUNK, chunk, 0)

    return pl.pallas_call(
        body, name="sgu_fwd", grid=(1,),
        out_shape=jax.ShapeDtypeStruct((SEQ, W_GRP), F32),
        in_specs=[_zslab(2 * W_GRP, 0), _full((CHUNK, SGU_HEADS * CHUNK)), _full((CHUNK, W_GRP))],
        out_specs=_full((SEQ, W_GRP)),
        scratch_shapes=[pltpu.VMEM((SEQ, W_GRP), BF16), pltpu.VMEM((SEQ, W_GRP), F32)],
        compiler_params=_params("arbitrary"),
    )(z, w_cat, bias)


def _sgu_bwd(z, dy, w_cat, bias):
    def body(z_ref, dy_ref, w_ref, b_ref, dz_ref, dw_ref, db_ref, vn_scr, u_scr, dvn_scr, du_scr):
        za = z_ref[...]
        u, vn, rstd = _sgu_norm(za)
        vn_scr[...] = vn.astype(BF16)
        u_scr[...] = u
        ws = _sgu_masked_w(w_ref)
        head = _head_of_lane()
        bias_v = b_ref[...]

        def chunk(n, carry):
            dws, dbias = carry
            rows = pl.ds(pl.multiple_of(n * CHUNK, CHUNK), CHUNK)
            vb = vn_scr[rows, :]
            mixed = bias_v
            for h in range(SGU_HEADS):
                mixed = mixed + jnp.where(head == h, _dot(ws[h], vb), 0.0)
            dyc = dy_ref[rows, :]
            du_scr[rows, :] = dyc * mixed
            dmixed = dyc * u_scr[rows, :]
            dvn = jnp.zeros((CHUNK, W_GRP), F32)
            new_dws = []
            for h in range(SGU_HEADS):
                dm = jnp.where(head == h, dmixed, 0.0).astype(BF16)
                new_dws.append(dws[h] + _dot_nt(dm, vb))
                dvn = dvn + _dot_tn(ws[h], dm)
            dvn_scr[rows, :] = dvn
            return tuple(new_dws), dbias + dmixed

        zero_w = tuple(jnp.zeros((CHUNK, CHUNK), F32) for _ in range(SGU_HEADS))
        dws, dbias = lax.fori_loop(0, SEQ // CHUNK, chunk, (zero_w, jnp.zeros((CHUNK, W_GRP), F32)))
        t = lax.broadcasted_iota(jnp.int32, (CHUNK, CHUNK), 0)
        s = lax.broadcasted_iota(jnp.int32, (CHUNK, CHUNK), 1)
        for h in range(SGU_HEADS):
            dw_ref[:, h * CHUNK:(h + 1) * CHUNK] = jnp.where(t >= s, dws[h], 0.0)
        avg = _head_avg()
        db_ref[...] = _dot_hi(dbias, avg) * float(SGU_HEAD_DIM)
        dvn = dvn_scr[...]
        dv = rstd * (dvn - _dot_hi(dvn, avg) - vn * _dot_hi(dvn * vn, avg))
        gg = _gelu_grad(za)
        dz_ref[:, :W_GRP] = gg[:, :W_GRP] * du_scr[...]
        dz_ref[:, W_GRP:] = gg[:, W_GRP:] * dv

    return pl.pallas_call(
        body, name="sgu_bwd", grid=(1,),
        out_shape=(jax.ShapeDtypeStruct((SEQ, 2 * W_GRP), F32), jax.ShapeDtypeStruct((CHUNK, SGU_HEADS * CHUNK), F32),
                   jax.ShapeDtypeStruct((CHUNK, W_GRP), F32)),
        in_specs=[_zslab(2 * W_GRP, 0), _full((SEQ, W_GRP)), _full((CHUNK, SGU_HEADS * CHUNK)), _full((CHUNK, W_GRP))],
        out_specs=(_full((SEQ, 2 * W_GRP)), _full((CHUNK, SGU_HEADS * CHUNK)), _full((CHUNK, W_GRP))),
        scratch_shapes=[pltpu.VMEM((SEQ, W_GRP), BF16), pltpu.VMEM((SEQ, W_GRP), F32),
                        pltpu.VMEM((SEQ, W_GRP), F32), pltpu.VMEM((SEQ, W_GRP), F32)],
        compiler_params=_params("arbitrary"),
    )(z, dy, w_cat, bias)


def _pool_window_of_lane(shape):
    grp = lax.broadcasted_iota(jnp.int32, shape, 1) // POOL_GROUP_DIM
    win = jnp.full(shape, POOL_WINDOWS[0], jnp.int32)
    for k in range(1, len(POOL_WINDOWS)):
        win = jnp.where(grp == k, POOL_WINDOWS[k], win)
    return grp, win


def _pool_select(levels, grp):
    out = levels[0]
    for k in range(1, len(levels)):
        out = jnp.where(grp == k, levels[k], out)
    return out


def _pool_p(z):
    shape = z.shape
    rows = lax.broadcasted_iota(jnp.int32, shape, 0)
    grp, win = _pool_window_of_lane(shape)
    levels, s, k = [], z, 1
    for _ in POOL_WINDOWS:
        s = s + _shift_down(s, k, rows)
        levels.append(s)
        k *= 2
    inv = 1.0 / jnp.minimum(rows + 1, win).astype(F32)
    return _pool_select(levels, grp) * inv - z, inv, rows, grp


def _pool_fwd(z, w_bd, scale):
    def body(z_ref, w_ref, s_ref, y_ref):
        p, _, _, _ = _pool_p(z_ref[...])
        y_ref[...] = _dot(p.astype(BF16), w_ref[...]) * s_ref[...]

    return pl.pallas_call(
        body, name="pool_fwd", grid=(1,),
        out_shape=jax.ShapeDtypeStruct((SEQ, W_GRP), F32),
        in_specs=[_zslab(W_GRP, 2), _full((W_GRP, W_GRP)), _full((1, W_GRP))],
        out_specs=_full((SEQ, W_GRP)), compiler_params=_params("arbitrary"),
    )(z, w_bd, scale)


def _pool_bwd(z, dy, w_bd, scale):
    def body(z_ref, dy_ref, w_ref, s_ref, dz_ref, dw_ref, ds_ref):
        p, inv, rows, grp = _pool_p(z_ref[...])
        pb = p.astype(BF16)
        dyv = dy_ref[...]
        ds_ref[...] = _sum0(dyv * _dot(pb, w_ref[...]))
        dpre = (dyv * s_ref[...]).astype(BF16)
        dw_ref[...] = _dot_tn(pb, dpre)
        dp = _dot_nt(dpre, w_ref[...])
        q = dp * inv
        levels, s, k = [], q, 1
        for _ in POOL_WINDOWS:
            s = s + _shift_up(s, k, rows)
            levels.append(s)
            k *= 2
        dz_ref[...] = _pool_select(levels, grp) - dp

    return pl.pallas_call(
        body, name="pool_bwd", grid=(1,),
        out_shape=(jax.ShapeDtypeStruct((SEQ, W_GRP), F32), jax.ShapeDtypeStruct((W_GRP, W_GRP), F32),
                   jax.ShapeDtypeStruct((1, W_GRP), F32)),
        in_specs=[_zslab(W_GRP, 2), _full((SEQ, W_GRP)), _full((W_GRP, W_GRP)), _full((1, W_GRP))],
        out_specs=(_full((SEQ, W_GRP)), _full((W_GRP, W_GRP)), _full((1, W_GRP))),
        compiler_params=_params("arbitrary"),
    )(z, dy, w_bd, scale)


def _conv_fwd(z, w):
    def body(z_ref, w_ref, y_ref):
        zc = z_ref[...]
        bg, cg, xh = zc[:, :W_GRP], zc[:, W_GRP:2 * W_GRP], zc[:, 2 * W_GRP:]
        rows = lax.broadcasted_iota(jnp.int32, (SEQ, W_GRP), 0)
        y = cg * xh
        conv = w_ref[0:1, :] * _shift_down(y, 2, rows) + w_ref[1:2, :] * _shift_down(y, 1, rows) + w_ref[2:3, :] * y
        y_ref[...] = bg * conv

    return pl.pallas_call(
        body, name="conv_fwd", grid=(1,),
        out_shape=jax.ShapeDtypeStruct((SEQ, W_GRP), F32),
        in_specs=[_zslab(3 * W_GRP, 1), _full((3, W_GRP))],
        out_specs=_full((SEQ, W_GRP)), compiler_params=_params("arbitrary"),
    )(z, w)


def _conv_bwd(z, dy, w):
    def body(z_ref, dy_ref, w_ref, dz_ref, dw_ref):
        zc = z_ref[...]
        bg, cg, xh = zc[:, :W_GRP], zc[:, W_GRP:2 * W_GRP], zc[:, 2 * W_GRP:]
        rows = lax.broadcasted_iota(jnp.int32, (SEQ, W_GRP), 0)
        y = cg * xh
        y2, y1 = _shift_down(y, 2, rows), _shift_down(y, 1, rows)
        conv = w_ref[0:1, :] * y2 + w_ref[1:2, :] * y1 + w_ref[2:3, :] * y
        dyv = dy_ref[...]
        dconv = dyv * bg
        dw_ref[...] = jnp.zeros_like(dw_ref)
        dw_ref[0:1, :] = _sum0(dconv * y2)
        dw_ref[1:2, :] = _sum0(dconv * y1)
        dw_ref[2:3, :] = _sum0(dconv * y)
        dyy = (w_ref[0:1, :] * _shift_up(dconv, 2, rows) + w_ref[1:2, :] * _shift_up(dconv, 1, rows)
               + w_ref[2:3, :] * dconv)
        dz_ref[:, :W_GRP] = dyv * conv
        dz_ref[:, W_GRP:2 * W_GRP] = dyy * xh
        dz_ref[:, 2 * W_GRP:] = dyy * cg

    return pl.pallas_call(
        body, name="conv_bwd", grid=(1,),
        out_shape=(jax.ShapeDtypeStruct((SEQ, 3 * W_GRP), F32), jax.ShapeDtypeStruct((SUBLANES, W_GRP), F32)),
        in_specs=[_zslab(3 * W_GRP, 1), _full((SEQ, W_GRP)), _full((3, W_GRP))],
        out_specs=(_full((SEQ, 3 * W_GRP)), _full((SUBLANES, W_GRP))),
        compiler_params=_params("arbitrary"),
    )(z, dy, w)


def _s5_disc(lre, lim, ldt, br, bi):
    dt = jnp.exp(ldt)
    mag = jnp.exp(lre * dt)
    ang = lim * dt
    a_re, a_im = mag * jnp.cos(ang), mag * jnp.sin(ang)
    nr, ni = a_re - 1.0, a_im
    den = lre * lre + lim * lim
    k_re = (nr * lre + ni * lim) / den
    k_im = (ni * lre - nr * lim) / den
    return a_re, a_im, k_re * br - k_im * bi, k_re * bi + k_im * br


def _s5_prep_fwd(lre, lim, ldt, br, bi):
    def body(lre_ref, lim_ref, ldt_ref, br_ref, bi_ref, ar_ref, ai_ref, bbr_ref, bbi_ref):
        ar, ai, bbr, bbi = _s5_disc(lre_ref[...], lim_ref[...], ldt_ref[...], br_ref[...], bi_ref[...])
        ar_ref[...] = ar
        ai_ref[...] = ai
        bbr_ref[...] = bbr
        bbi_ref[...] = bbi

    return pl.pallas_call(
        body, name="s5_prep_fwd",
        out_shape=(jax.ShapeDtypeStruct(lre.shape, F32), jax.ShapeDtypeStruct(lre.shape, F32),
                   jax.ShapeDtypeStruct(br.shape, F32), jax.ShapeDtypeStruct(br.shape, F32)),
        compiler_params=_params(),
    )(lre, lim, ldt, br, bi)


def _s5_prep_bwd(lre, lim, ldt, br, bi, dar, dai, dbbr, dbbi):
    def body(lre_ref, lim_ref, ldt_ref, br_ref, bi_ref, dar_ref, dai_ref, dbbr_ref, dbbi_ref,
             o_lre, o_lim, o_ldt, o_br, o_bi):
        _, pull = jax.vjp(_s5_disc, lre_ref[...], lim_ref[...], ldt_ref[...], br_ref[...], bi_ref[...])
        g = pull((dar_ref[...], dai_ref[...], dbbr_ref[...], dbbi_ref[...]))
        for o, v in zip((o_lre, o_lim, o_ldt, o_br, o_bi), g):
            o[...] = v

    return pl.pallas_call(
        body, name="s5_prep_bwd",
        out_shape=tuple(jax.ShapeDtypeStruct(a.shape, F32) for a in (lre, lim, ldt, br, bi)),
        compiler_params=_params(),
    )(lre, lim, ldt, br, bi, dar, dai, dbbr, dbbi)


def _cmul(ar, ai, br, bi):
    return ar * br - ai * bi, ar * bi + ai * br


def _s5_tile_consts(ar, ai, reverse):
    if reverse:
        ai = -ai
    shape = (SUBLANES, LANES)
    row = lax.broadcasted_iota(jnp.int32, shape, 0)
    a1 = (jnp.broadcast_to(ar, shape), jnp.broadcast_to(ai, shape))
    a2 = _cmul(*a1, *a1)
    a4 = _cmul(*a2, *a2)
    a8 = _cmul(*a4, *a4)
    steps = []
    for s, (pr, pi) in ((1, a1), (2, a2), (4, a4)):
        keep = (row < SUBLANES - s) if reverse else (row >= s)
        steps.append((s, jnp.where(keep, pr, 0.0), jnp.where(keep, pi, 0.0)))
    e = (SUBLANES - row) if reverse else (row + 1)
    pr, pi = jnp.ones(shape, F32), jnp.zeros(shape, F32)
    for bit, (qr, qi) in ((1, a1), (2, a2), (4, a4), (8, a8)):
        nr, ni = _cmul(pr, pi, qr, qi)
        hit = (e & bit) != 0
        pr, pi = jnp.where(hit, nr, pr), jnp.where(hit, ni, pi)
    return steps, pr, pi


def _s5_tile(xr, xi, steps, reverse):
    for s, pr, pi in steps:
        sh = SUBLANES - s if reverse else s
        sr, si = pltpu.roll(xr, sh, axis=0), pltpu.roll(xi, sh, axis=0)
        xr, xi = xr + pr * sr - pi * si, xi + pr * si + pi * sr
    return xr, xi


N_TILES = SEQ // SUBLANES


def _s5_fwd(z, b_re, b_im, c_re, c_im, a_re, a_im, d, glu_w, glu_b):
    nblk = S5_LANES // LANES

    def body(u_ref, br_ref, bi_ref, cr_ref, ci_ref, ar_ref, ai_ref, d_ref, gw_ref, gb_ref,
             y_ref, y0_ref, xr_ref, xi_ref, ub_scr, acc_scr):
        jb = pl.program_id(0)

        @pl.when(jb == 0)
        def _():
            ub_scr[...] = u_ref[...].astype(BF16)
            acc_scr[...] = jnp.zeros_like(acc_scr)

        ub = ub_scr[...]
        xr_ref[...] = _dot(ub, br_ref[...])
        xi_ref[...] = _dot(ub, bi_ref[...])
        steps, pr, pi = _s5_tile_consts(ar_ref[...], ai_ref[...], False)

        def tile(t, carry):
            cr, ci = carry
            rows = pl.ds(pl.multiple_of(t * SUBLANES, SUBLANES), SUBLANES)
            xr, xi = _s5_tile(xr_ref[rows, :], xi_ref[rows, :], steps, False)
            xr, xi = xr + pr * cr - pi * ci, xi + pr * ci + pi * cr
            xr_ref[rows, :] = xr
            xi_ref[rows, :] = xi
            return xr[SUBLANES - 1:, :], xi[SUBLANES - 1:, :]

        zero = jnp.zeros((1, LANES), F32)
        lax.fori_loop(0, N_TILES, tile, (zero, zero), unroll=4)
        acc_scr[...] += (_dot(xr_ref[...].astype(BF16), cr_ref[...]) - _dot(xi_ref[...].astype(BF16), ci_ref[...]))

        @pl.when(jb == nblk - 1)
        def _():
            y0 = acc_scr[...] + d_ref[...] * u_ref[...]
            y0_ref[...] = y0
            y1 = _gelu(y0)
            y_ref[...] = y1 * _sigmoid(_dot(y1.astype(BF16), gw_ref[...]) + gb_ref[...])

    lane_blk = pl.BlockSpec((SEQ, LANES), lambda j: (0, j))
    return pl.pallas_call(
        body, name="s5_fwd", grid=(nblk,),
        out_shape=(jax.ShapeDtypeStruct((SEQ, W_GRP), F32), jax.ShapeDtypeStruct((SEQ, W_GRP), F32),
                   jax.ShapeDtypeStruct((SEQ, S5_LANES), F32), jax.ShapeDtypeStruct((SEQ, S5_LANES), F32)),
        in_specs=[_zslab(W_GRP, 6),
                  pl.BlockSpec((W_GRP, LANES), lambda j: (0, j)), pl.BlockSpec((W_GRP, LANES), lambda j: (0, j)),
                  pl.BlockSpec((LANES, W_GRP), lambda j: (j, 0)), pl.BlockSpec((LANES, W_GRP), lambda j: (j, 0)),
                  pl.BlockSpec((1, LANES), lambda j: (0, j)), pl.BlockSpec((1, LANES), lambda j: (0, j)),
                  _full((1, W_GRP)), _full((W_GRP, W_GRP)), _full((1, W_GRP))],
        out_specs=(_full((SEQ, W_GRP)), _full((SEQ, W_GRP)), lane_blk, lane_blk),
        scratch_shapes=[pltpu.VMEM((SEQ, W_GRP), BF16), pltpu.VMEM((SEQ, W_GRP), F32)],
        compiler_params=_params("arbitrary"),
    )(z, b_re, b_im, c_re, c_im, a_re, a_im, d, glu_w, glu_b)


def _s5_bwd(z, y0, dy, xr, xi, b_re, b_im, c_re, c_im, a_re, a_im, d, glu_w, glu_b):
    nblk = S5_LANES // LANES

    def body(u_ref, y0_ref, dy_ref, xr_ref, xi_ref, br_ref, bi_ref, cr_ref, ci_ref, ar_ref, ai_ref,
             d_ref, gw_ref, gb_ref,
             du_ref, dbr_ref, dbi_ref, dcr_ref, dci_ref, dar_ref, dai_ref, dd_ref, dgw_ref, dgb_ref,
             ub_scr, dy0_scr, du_scr, lr_scr, li_scr):
        jb = pl.program_id(0)

        @pl.when(jb == 0)
        def _():
            u = u_ref[...]
            y0v = y0_ref[...]
            y1 = _gelu(y0v)
            y1b = y1.astype(BF16)
            sg = _sigmoid(_dot(y1b, gw_ref[...]) + gb_ref[...])
            dyv = dy_ref[...]
            dpre = dyv * y1 * sg * (1.0 - sg)
            dpb = dpre.astype(BF16)
            dgw_ref[...] = _dot_tn(y1b, dpb)
            dgb_ref[...] = _sum0(dpre)
            dy1 = dyv * sg + _dot_nt(dpb, gw_ref[...])
            dy0 = dy1 * _gelu_grad(y0v)
            dd_ref[...] = _sum0(dy0 * u)
            du_scr[...] = dy0 * d_ref[...]
            dy0_scr[...] = dy0.astype(BF16)
            ub_scr[...] = u.astype(BF16)

        dy0b = dy0_scr[...]
        lr_scr[...] = _dot_nt(dy0b, cr_ref[...])
        li_scr[...] = -_dot_nt(dy0b, ci_ref[...])
        dcr_ref[...] = _dot_tn(xr_ref[...].astype(BF16), dy0b)
        dci_ref[...] = -_dot_tn(xi_ref[...].astype(BF16), dy0b)
        steps, pr, pi = _s5_tile_consts(ar_ref[...], ai_ref[...], True)
        row = lax.broadcasted_iota(jnp.int32, (SUBLANES, LANES), 0)

        def tile(k, carry):
            cr, ci, accr, acci = carry
            t = N_TILES - 1 - k
            rows = pl.ds(pl.multiple_of(t * SUBLANES, SUBLANES), SUBLANES)
            lr, li = _s5_tile(lr_scr[rows, :], li_scr[rows, :], steps, True)
            lr, li = lr + pr * cr - pi * ci, li + pr * ci + pi * cr
            lr_scr[rows, :] = lr
            li_scr[rows, :] = li
            prev = pl.ds(pl.multiple_of(jnp.maximum(t - 1, 0) * SUBLANES, SUBLANES), SUBLANES)
            live = jnp.where(t > 0, 1.0, 0.0)
            xpr = jnp.where(row == 0, pltpu.roll(xr_ref[prev, :], 1, axis=0) * live, pltpu.roll(xr_ref[rows, :], 1, axis=0))
            xpi = jnp.where(row == 0, pltpu.roll(xi_ref[prev, :], 1, axis=0) * live, pltpu.roll(xi_ref[rows, :], 1, axis=0))
            accr = accr + lr * xpr + li * xpi
            acci = acci + li * xpr - lr * xpi
            return lr[0:1, :], li[0:1, :], accr, acci

        zero = jnp.zeros((1, LANES), F32)
        zt = jnp.zeros((SUBLANES, LANES), F32)
        _, _, accr, acci = lax.fori_loop(0, N_TILES, tile, (zero, zero, zt, zt), unroll=4)
        dar_ref[...] = jnp.zeros_like(dar_ref)
        dai_ref[...] = jnp.zeros_like(dai_ref)
        dar_ref[0:1, :] = _sum0(accr)
        dai_ref[0:1, :] = _sum0(acci)
        lrb, lib = lr_scr[...].astype(BF16), li_scr[...].astype(BF16)
        ub = ub_scr[...]
        dbr_ref[...] = _dot_tn(ub, lrb)
        dbi_ref[...] = _dot_tn(ub, lib)
        du_scr[...] += _dot_nt(lrb, br_ref[...]) + _dot_nt(lib, bi_ref[...])

        @pl.when(jb == nblk - 1)
        def _():
            du_ref[...] = du_scr[...]

    lane_blk = pl.BlockSpec((SEQ, LANES), lambda j: (0, j))
    bspec = pl.BlockSpec((W_GRP, LANES), lambda j: (0, j))
    cspec = pl.BlockSpec((LANES, W_GRP), lambda j: (j, 0))
    aspec = pl.BlockSpec((1, LANES), lambda j: (0, j))
    a8spec = pl.BlockSpec((SUBLANES, LANES), lambda j: (0, j))
    sd = jax.ShapeDtypeStruct
    return pl.pallas_call(
        body, name="s5_bwd", grid=(nblk,),
        out_shape=(sd((SEQ, W_GRP), F32), sd((W_GRP, S5_LANES), F32), sd((W_GRP, S5_LANES), F32),
                   sd((S5_LANES, W_GRP), F32), sd((S5_LANES, W_GRP), F32),
                   sd((SUBLANES, S5_LANES), F32), sd((SUBLANES, S5_LANES), F32),
                   sd((1, W_GRP), F32), sd((W_GRP, W_GRP), F32), sd((1, W_GRP), F32)),
        in_specs=[_zslab(W_GRP, 6), _full((SEQ, W_GRP)), _full((SEQ, W_GRP)), lane_blk, lane_blk,
                  bspec, bspec, cspec, cspec, aspec, aspec,
                  _full((1, W_GRP)), _full((W_GRP, W_GRP)), _full((1, W_GRP))],
        out_specs=(_full((SEQ, W_GRP)), bspec, bspec, cspec, cspec, a8spec, a8spec,
                   _full((1, W_GRP)), _full((W_GRP, W_GRP)), _full((1, W_GRP))),
        scratch_shapes=[pltpu.VMEM((SEQ, W_GRP), BF16), pltpu.VMEM((SEQ, W_GRP), BF16), pltpu.VMEM((SEQ, W_GRP), F32),
                        pltpu.VMEM((SEQ, LANES), F32), pltpu.VMEM((SEQ, LANES), F32)],
        compiler_params=_params("arbitrary"),
    )(z, y0, dy, xr, xi, b_re, b_im, c_re, c_im, a_re, a_im, d, glu_w, glu_b)


def _head(x, g, target):
    tm = TOKEN_TILE
    n = SEQ // tm

    def body(x_ref, g_ref, t_ref, dx_ref, st_ref, acc_scr):
        i = pl.program_id(0)

        @pl.when(i == 0)
        def _():
            acc_scr[...] = jnp.zeros_like(acc_scr)

        xhat, r = _rms(x_ref[...])
        gv = g_ref[...]
        err = xhat * gv - t_ref[...]
        dyv = err * (1.0 / D_MODEL)
        dx_ref[...] = _rms_bwd(xhat, r, dyv * gv)
        acc_scr[0:1, :] += _sum0(err * err)
        acc_scr[1:2, :] += _sum0(dyv * xhat)

        @pl.when(i == n - 1)
        def _():
            st_ref[...] = acc_scr[...]
            tot = jnp.sum(acc_scr[0:1, :], axis=-1, keepdims=True) * (0.5 / D_MODEL)
            st_ref[0:1, :] = jnp.broadcast_to(tot, (1, D_MODEL))

    tok = pl.BlockSpec((tm, D_MODEL), lambda i: (i, 0))
    return pl.pallas_call(
        body, name="head", grid=(n,),
        out_shape=(jax.ShapeDtypeStruct((SEQ, D_MODEL), F32), jax.ShapeDtypeStruct((SUBLANES, D_MODEL), F32)),
        in_specs=[tok, _row_spec(1), tok], out_specs=(tok, _row_spec(SUBLANES)),
        scratch_shapes=[pltpu.VMEM((SUBLANES, D_MODEL), F32)],
        compiler_params=_params("arbitrary"),
    )(x, g, target)


def _adamw(w, gparts, m, v, name):
    r, c = w.shape
    npart = gparts.shape[0]
    tr = r
    for cand in (512, 256, 128, 64, 32, 16):
        if r % cand == 0 and r > cand:
            tr = cand
            break
    b1c = 1.0 - ADAM_B1 ** ADAM_STEP
    b2c = 1.0 - ADAM_B2 ** ADAM_STEP

    def body(w_ref, g_ref, m_ref, v_ref, go_ref, d_ref, mo_ref, vo_ref):
        g = g_ref[0].astype(F32)
        for k in range(1, npart):
            g = g + g_ref[k].astype(F32)
        mn = ADAM_B1 * m_ref[...] + (1.0 - ADAM_B1) * g
        vn = ADAM_B2 * v_ref[...] + (1.0 - ADAM_B2) * (g * g)
        m_hat = mn / b1c
        v_hat = vn / b2c
        go_ref[...] = g
        d_ref[...] = -ADAM_LR * (m_hat / (jnp.sqrt(v_hat) + ADAM_EPS) + ADAM_WD * w_ref[...])
        mo_ref[...] = mn
        vo_ref[...] = vn

    blk = pl.BlockSpec((tr, c), lambda i: (i, 0))
    sh = jax.ShapeDtypeStruct((r, c), F32)
    return pl.pallas_call(
        body, name=name, grid=(r // tr,),
        out_shape=(sh, sh, sh, sh),
        in_specs=[blk, pl.BlockSpec((npart, tr, c), lambda i: (0, i, 0)), blk, blk],
        out_specs=(blk, blk, blk, blk), compiler_params=_params("arbitrary"),
    )(w, gparts, m, v)


def _sum_parts(parts, name):
    n, r, c = parts.shape

    def body(p_ref, o_ref):
        acc = p_ref[0]
        for k in range(1, n):
            acc = acc + p_ref[k]
        o_ref[...] = acc

    return pl.pallas_call(
        body, name=name, out_shape=jax.ShapeDtypeStruct((r, c), F32), compiler_params=_params(),
    )(parts)


def _block_diag(blocks):
    g, a, b = blocks.shape
    eye = jnp.eye(g, dtype=blocks.dtype)
    return (blocks[:, :, None, :] * eye[:, None, :, None]).reshape(g * a, g * b)


def _diag_blocks(dense, g):
    a, b = dense.shape[0] // g, dense.shape[1] // g
    d4 = dense.reshape(g, a, g, b)
    eye = jnp.eye(g, dtype=dense.dtype)
    return jnp.sum(d4 * eye[:, None, :, None], axis=2)


def _pack(parts, cols):
    flat = jnp.concatenate([p.reshape(-1) for p in parts])
    unit = N_DEV * SUBLANES * cols
    total = -(-flat.shape[0] // unit) * unit
    flat = jnp.pad(flat, (0, total - flat.shape[0]))
    return flat.reshape(N_DEV, total // (N_DEV * cols), cols)


def _unpack(flat, shapes):
    out, pos = [], 0
    for s in shapes:
        n = math.prod(s)
        out.append(flat[pos:pos + n].reshape(s))
        pos += n
    return out


SMALL = ("norm1_g", "norm2_g", "sgu_w", "sgu_b", "pool_w", "pool_scale", "conv_w", "s5_lambda_re", "s5_lambda_im",
         "s5_b_re", "s5_b_im", "s5_c_re", "s5_c_im", "s5_d", "s5_log_dt", "s5_glu_w", "s5_glu_b", "mix_norm_g",
         "norm3_g", "final_norm_g")
BIG = ("ffn1_w_in", "ffn1_w_out", "w_mix_in", "w_mix_out", "ffn2_w_in", "ffn2_w_out")
WEIGHTS = ("ada_w", "ada_b", "norm1_g", "ffn1_w_in", "ffn1_w_out", "norm2_g", "w_mix_in", "sgu_w", "sgu_b", "pool_w",
           "pool_scale", "conv_w", "s5_lambda_re", "s5_lambda_im", "s5_b_re", "s5_b_im", "s5_c_re", "s5_c_im", "s5_d",
           "s5_log_dt", "s5_glu_w", "s5_glu_b", "mix_norm_g", "w_mix_out", "norm3_g", "ffn2_w_in", "ffn2_w_out",
           "final_norm_g")
PACK_COLS = 1024


def kernel(x, c, ada_w, ada_b, norm1_g, ffn1_w_in, ffn1_w_out, norm2_g, w_mix_in, sgu_w, sgu_b, pool_w, pool_scale, conv_w, s5_lambda_re, s5_lambda_im, s5_b_re, s5_b_im, s5_c_re, s5_c_im, s5_d, s5_log_dt, s5_glu_w, s5_glu_b, mix_norm_g, w_mix_out, norm3_g, ffn2_w_in, ffn2_w_out, final_norm_g, loss_target, m_ada_w, m_ada_b, m_norm1_g, m_ffn1_w_in, m_ffn1_w_out, m_norm2_g, m_w_mix_in, m_sgu_w, m_sgu_b, m_pool_w, m_pool_scale, m_conv_w, m_s5_lambda_re, m_s5_lambda_im, m_s5_b_re, m_s5_b_im, m_s5_c_re, m_s5_c_im, m_s5_d, m_s5_log_dt, m_s5_glu_w, m_s5_glu_b, m_mix_norm_g, m_w_mix_out, m_norm3_g, m_ffn2_w_in, m_ffn2_w_out, m_final_norm_g, v_ada_w, v_ada_b, v_norm1_g, v_ffn1_w_in, v_ffn1_w_out, v_norm2_g, v_w_mix_in, v_sgu_w, v_sgu_b, v_pool_w, v_pool_scale, v_conv_w, v_s5_lambda_re, v_s5_lambda_im, v_s5_b_re, v_s5_b_im, v_s5_c_re, v_s5_c_im, v_s5_d, v_s5_log_dt, v_s5_glu_w, v_s5_glu_b, v_mix_norm_g, v_w_mix_out, v_norm3_g, v_ffn2_w_in, v_ffn2_w_out, v_final_norm_g):
    args = dict(locals())
    W = {n: args[n] for n in WEIGHTS}
    M = {n: args["m_" + n] for n in WEIGHTS}
    V = {n: args["v_" + n] for n in WEIGHTS}
    me = _me()
    L = DEPTH
    x0 = x[0]
    target = loss_target[0]

    conv_cols = conv_w.shape[-1]
    glu_rows = s5_glu_w.shape[1]
    c_g, conv_g, glu_g = _exchange(
        [c.reshape(SUBLANES, LANES), conv_w.reshape(L * 3, conv_cols), s5_glu_w.reshape(L * glu_rows, W_GRP)],
        False, "gather_small")
    c_all = c_g.reshape(N_DEV, D_MODEL)
    conv_full = conv_g.reshape(N_DEV, L, 3, conv_cols).transpose(1, 2, 0, 3).reshape(L, 3, W_GRP)
    glu_full = glu_g.reshape(N_DEV, L, glu_rows, W_GRP).transpose(1, 0, 2, 3).reshape(L, W_GRP, W_GRP)

    ncol = ada_w.shape[-1]
    ada_b_mine = lax.dynamic_slice_in_dim(ada_b, me * ncol, ncol, axis=1).reshape(L, 1, ncol)
    cond_part = _cond_fwd(c_all, ada_w, ada_b_mine)
    (cond_g,) = _exchange([cond_part.reshape(L * N_DEV, ncol)], False, "gather_cond")
    cond_g = cond_g.reshape(N_DEV, L, N_DEV, ncol)
    cond_mine = lax.dynamic_index_in_dim(cond_g, me, axis=2, keepdims=False)
    cond = cond_mine.transpose(1, 0, 2).reshape(L, N_ADA, D_MODEL)

    lg = L * S5_GROUPS
    lre3 = s5_lambda_re.reshape(lg, S5_STATE, 1)
    lim3 = s5_lambda_im.reshape(lg, S5_STATE, 1)
    ldt3 = s5_log_dt.reshape(lg, 1, 1)
    br3 = s5_b_re.reshape(lg, S5_STATE, S5_GROUP_CH)
    bi3 = s5_b_im.reshape(lg, S5_STATE, S5_GROUP_CH)
    a_re3, a_im3, bb_re3, bb_im3 = _s5_prep_fwd(lre3, lim3, ldt3, br3, bi3)
    a_re = a_re3.reshape(L, 1, S5_LANES)
    a_im = a_im3.reshape(L, 1, S5_LANES)

    def b_mat(bb3, l):
        return _block_diag(bb3.reshape(L, S5_GROUPS, S5_STATE, S5_GROUP_CH)[l].transpose(0, 2, 1)).astype(BF16)

    def c_mat(cw, l):
        return _block_diag(cw[l].transpose(0, 2, 1)).astype(BF16)

    def gather_layer(l):
        srcs = [W[n][l].astype(BF16) for n in BIG]
        g = _exchange(srcs, False, "gather_weights")
        out = dict(zip(BIG, g))
        out["ffn1_w_out"] = out["ffn1_w_out"].reshape(N_FF_CHUNK, FF_PIECE, D_MODEL)
        out["ffn2_w_out"] = out["ffn2_w_out"].reshape(N_FF_CHUNK, FF_PIECE, D_MODEL)
        out["w_mix_in"] = out["w_mix_in"].transpose(1, 0, 2).reshape(D_MODEL, P_IN)
        out["w_mix_out"] = out["w_mix_out"].reshape(D_MODEL, D_MODEL)
        return out

    def mixer_consts(l):
        w_cat = sgu_w[l].transpose(1, 0, 2).reshape(CHUNK, SGU_HEADS * CHUNK)
        bias = jnp.repeat(sgu_b[l].T, SGU_HEAD_DIM, axis=1)
        return dict(
            w_cat=w_cat, bias=bias, pool_bd=_block_diag(pool_w[l]).astype(BF16), pool_scale=pool_scale[l][None],
            conv=conv_full[l], b_re=b_mat(bb_re3, l), b_im=b_mat(bb_im3, l), c_re=c_mat(s5_c_re, l),
            c_im=c_mat(s5_c_im, l), a_re=a_re[l], a_im=a_im[l], d=s5_d[l][None], glu_w=glu_full[l].astype(BF16),
            glu_b=s5_glu_b[l][None])

    saved = []
    xc = x0
    for l in range(L):
        wl = gather_layer(l)
        mc = mixer_consts(l)
        cl = cond[l]
        x_a = xc
        x_b, f1 = _ffn_fwd(x_a, cl[0:3], norm1_g[l][None], wl["ffn1_w_in"], wl["ffn1_w_out"])
        z = _mix_in_fwd(x_b, cl[3:5], norm2_g[l][None], wl["w_mix_in"])
        ya = _sgu_fwd(z, mc["w_cat"], mc["bias"])
        yb = _pool_fwd(z, mc["pool_bd"], mc["pool_scale"])
        yc = _conv_fwd(z, mc["conv"])
        yd, y0, sxr, sxi = _s5_fwd(z, mc["b_re"], mc["b_im"], mc["c_re"], mc["c_im"], mc["a_re"], mc["a_im"],
                                   mc["d"], mc["glu_w"], mc["glu_b"])
        ys = (ya, yb, yc, yd)
        x_c = _mix_out_fwd(x_b, ys, mix_norm_g[l][None], cl[5:6], wl["w_mix_out"])
        x_d, f2 = _ffn_fwd(x_c, cl[6:9], norm3_g[l][None], wl["ffn2_w_in"], wl["ffn2_w_out"])
        saved.append(dict(wl=wl, mc=mc, x_a=x_a, x_b=x_b, x_c=x_c, f1=f1, f2=f2, z=z, ys=ys, y0=y0, xr=sxr, xi=sxi))
        xc = x_d

    dx, stats = _head(xc, final_norm_g[None], target)
    loss = lax.psum(stats[0, 0], MESH_AXES)

    small_grads = {n: [None] * L for n in SMALL if n != "final_norm_g"}
    small_grads["final_norm_g"] = stats[1]
    big_recv = {n: [None] * L for n in BIG}
    dcond_rows = [None] * L
    d_are, d_aim, d_bbre, d_bbim = [None] * L, [None] * L, [None] * L, [None] * L
    for l in reversed(range(L)):
        sv = saved[l]
        wl, mc, cl = sv["wl"], sv["mc"], cond[l]
        dx, da, db, act, hb, dob, part3 = _ffn_bwd(dx, sv["x_c"], sv["f2"], cl[6:9], norm3_g[l][None],
                                                   wl["ffn2_w_in"], wl["ffn2_w_out"])
        g_ffn2_in = _dw(hb[None], da, db, name="dw_ffn_in")
        g_ffn2_out = _dw(act, dob[None], name="dw_ffn_out").reshape(N_DEV, D_FF // N_DEV, D_MODEL)
        dya, dyb, dyc, dyd, ynb, dmob, part_mo = _mix_out_bwd(dx, sv["ys"], mix_norm_g[l][None], cl[5:6],
                                                              wl["w_mix_out"])
        g_mix_out = _dw(ynb[None], dmob[None], name="dw_mix_out").reshape(N_DEV, D_MODEL // N_DEV, D_MODEL)
        z = sv["z"]
        dza, dw_cat, dbias = _sgu_bwd(z, dya, mc["w_cat"], mc["bias"])
        dzb, dpool_dense, dpool_scale = _pool_bwd(z, dyb, mc["pool_bd"], mc["pool_scale"])
        dzc, dconv8 = _conv_bwd(z, dyc, mc["conv"])
        (dzd, dbre_d, dbim_d, dcre_d, dcim_d, dar8, dai8, dd, dglu_w, dglu_b) = _s5_bwd(
            z, sv["y0"], dyd, sv["xr"], sv["xi"], mc["b_re"], mc["b_im"], mc["c_re"], mc["c_im"],
            mc["a_re"], mc["a_im"], mc["d"], mc["glu_w"], mc["glu_b"])
        dz = jnp.concatenate([dza, dzb, dzc, dzd], axis=1)
        dx, h2b, dzbf, part2 = _mix_in_bwd(dz, sv["x_b"], dx, cl[3:5], norm2_g[l][None], wl["w_mix_in"])
        g_mix_in = _dw(h2b[None], dzbf[None], name="dw_mix_in")[0]
        g_mix_in = g_mix_in.reshape(D_MODEL, N_DEV, P_IN // N_DEV).transpose(1, 0, 2)
        dx, da, db, act, hb, dob, part1 = _ffn_bwd(dx, sv["x_a"], sv["f1"], cl[0:3], norm1_g[l][None],
                                                   wl["ffn1_w_in"], wl["ffn1_w_out"])
        g_ffn1_in = _dw(hb[None], da, db, name="dw_ffn_in")
        g_ffn1_out = _dw(act, dob[None], name="dw_ffn_out").reshape(N_DEV, D_FF // N_DEV, D_MODEL)
        recv = _exchange([g_ffn1_in, g_ffn1_out, g_mix_in, g_mix_out, g_ffn2_in, g_ffn2_out], True, "scatter_grads")
        for n, r in zip(BIG, recv):
            big_recv[n][l] = r
        dcond_rows[l] = jnp.concatenate([part1[0:3], part2[0:2], part_mo[0:1], part3[0:3]], axis=0)
        sg = small_grads
        sg["norm1_g"][l] = part1[3]
        sg["norm2_g"][l] = part2[2]
        sg["norm3_g"][l] = part3[3]
        sg["mix_norm_g"][l] = part_mo[1]
        sg["sgu_w"][l] = dw_cat.reshape(CHUNK, SGU_HEADS, CHUNK).transpose(1, 0, 2)
        sg["sgu_b"][l] = dbias[:, ::SGU_HEAD_DIM].T
        sg["pool_w"][l] = _diag_blocks(dpool_dense, len(POOL_WINDOWS))
        sg["pool_scale"][l] = dpool_scale[0]
        sg["conv_w"][l] = dconv8[0:3]
        sg["s5_c_re"][l] = _diag_blocks(dcre_d, S5_GROUPS).transpose(0, 2, 1)
        sg["s5_c_im"][l] = _diag_blocks(dcim_d, S5_GROUPS).transpose(0, 2, 1)
        sg["s5_d"][l] = dd[0]
        sg["s5_glu_w"][l] = dglu_w
        sg["s5_glu_b"][l] = dglu_b[0]
        d_are[l], d_aim[l] = dar8[0], dai8[0]
        d_bbre[l] = _diag_blocks(dbre_d, S5_GROUPS).transpose(0, 2, 1)
        d_bbim[l] = _diag_blocks(dbim_d, S5_GROUPS).transpose(0, 2, 1)
    grad_x = dx

    g_lre, g_lim, g_ldt, g_br, g_bi = _s5_prep_bwd(
        lre3, lim3, ldt3, br3, bi3,
        jnp.stack(d_are).reshape(lg, S5_STATE, 1), jnp.stack(d_aim).reshape(lg, S5_STATE, 1),
        jnp.stack(d_bbre).reshape(lg, S5_STATE, S5_GROUP_CH), jnp.stack(d_bbim).reshape(lg, S5_STATE, S5_GROUP_CH))
    small = {n: (jnp.stack(v) if isinstance(v, list) and v[0] is not None else v) for n, v in small_grads.items()}
    small["s5_lambda_re"] = g_lre.reshape(s5_lambda_re.shape)
    small["s5_lambda_im"] = g_lim.reshape(s5_lambda_im.shape)
    small["s5_log_dt"] = g_ldt.reshape(s5_log_dt.shape)
    small["s5_b_re"] = g_br.reshape(s5_b_re.shape)
    small["s5_b_im"] = g_bi.reshape(s5_b_im.shape)

    small_shapes = [(L, 3, W_GRP) if n == "conv_w" else (L, W_GRP, W_GRP) if n == "s5_glu_w" else W[n].shape
                    for n in SMALL]
    packed = _pack([small[n].reshape(s) for n, s in zip(SMALL, small_shapes)], PACK_COLS)
    (pieces,) = _exchange([packed], True, "scatter_small")
    mine = _sum_parts(pieces, "sum_small")
    (summed,) = _exchange([mine], False, "gather_small_sums")
    small_sum = dict(zip(SMALL, _unpack(summed.reshape(-1), small_shapes)))
    small_sum["conv_w"] = lax.dynamic_slice_in_dim(small_sum["conv_w"], me * conv_cols, conv_cols, axis=2)
    small_sum["s5_glu_w"] = lax.dynamic_slice_in_dim(small_sum["s5_glu_w"], me * glu_rows, glu_rows, axis=1)

    dcond = jnp.stack(dcond_rows).reshape(L * N_ADA, D_MODEL)
    (dcond_g,) = _exchange([dcond], False, "gather_dcond")
    dcond_all = dcond_g.reshape(N_DEV, L, N_ADA * D_MODEL).transpose(1, 0, 2)
    dcond_mine = lax.dynamic_slice_in_dim(dcond_all, me * ncol, ncol, axis=2)
    g_ada_w, g_ada_b = _cond_bwd(c_all.T, dcond_mine, dcond_all)

    grads, deltas, new_m, new_v = {}, {}, {}, {}
    for n in BIG:
        w = W[n]
        rows, cols = w.shape[1], w.shape[2]
        parts = jnp.stack(big_recv[n], axis=1).reshape(N_DEV, L * rows, cols)
        out = _adamw(w.reshape(L * rows, cols), parts, M[n].reshape(L * rows, cols), V[n].reshape(L * rows, cols),
                     "adamw_" + n)
        grads[n], deltas[n], new_m[n], new_v[n] = (o.reshape(w.shape) for o in out)
    out = _adamw(ada_w.reshape(L * D_MODEL, ncol), g_ada_w.reshape(1, L * D_MODEL, ncol),
                 m_ada_w.reshape(L * D_MODEL, ncol), v_ada_w.reshape(L * D_MODEL, ncol), "adamw_ada_w")
    grads["ada_w"], deltas["ada_w"], new_m["ada_w"], new_v["ada_w"] = (o.reshape(ada_w.shape) for o in out)
    small_names = SMALL + ("ada_b",)
    small_g = dict(small_sum)
    small_g["ada_b"] = g_ada_b.reshape(ada_b.shape)
    shapes = [W[n].shape for n in small_names]
    pw = _pack([W[n] for n in small_names], PACK_COLS)
    rows = pw.shape[0] * pw.shape[1]
    out = _adamw(pw.reshape(rows, PACK_COLS),
                 _pack([small_g[n] for n in small_names], PACK_COLS).reshape(1, rows, PACK_COLS),
                 _pack([M[n] for n in small_names], PACK_COLS).reshape(rows, PACK_COLS),
                 _pack([V[n] for n in small_names], PACK_COLS).reshape(rows, PACK_COLS), "adamw_small")
    for store, o in zip((grads, deltas, new_m, new_v), out):
        store.update(zip(small_names, _unpack(o.reshape(-1), shapes)))

    return (loss, grad_x[None], *[grads[n] for n in WEIGHTS], *[deltas[n] for n in WEIGHTS],
            *[new_m[n] for n in WEIGHTS], *[new_v[n] for n in WEIGHTS])
```

```python
import functools
import math

import jax
import jax.numpy as jnp
from jax import lax
from jax.experimental import pallas as pl
from jax.experimental.pallas import tpu as pltpu

F32 = jnp.float32
BF16 = jnp.bfloat16

D_MODEL = 1024
SEQ = 2048
DEPTH = 4
N_DEV = 8
W_GRP = 256
CHUNK = 128
SGU_HEADS = 4
SGU_HEAD_DIM = 64
POOL_WINDOWS = (2, 4, 8, 16)
POOL_GROUP_DIM = 64
S5_GROUPS = 16
S5_GROUP_CH = 16
S5_STATE = 64
S5_LANES = S5_GROUPS * S5_STATE
P_IN = 1792
D_FF = 2816
FF_PIECE = 2 * D_FF // N_DEV
N_FF_CHUNK = D_FF // FF_PIECE
N_ADA = 9
EPS = 1e-6
ADAM_LR = 0.001
ADAM_B1 = 0.9
ADAM_B2 = 0.999
ADAM_EPS = 1e-08
ADAM_WD = 0.01
ADAM_STEP = 10

SUBLANES = 8
LANES = 128
VMEM_LIMIT = 56 * 1024 * 1024
TOKEN_TILE = 512
HIGHEST = lax.Precision.HIGHEST
MESH_AXES = ("x", "y", "c")

_GELU_C = math.sqrt(2.0 / math.pi)
_GELU_A = 0.044715


def _params(*sem):
    return pltpu.CompilerParams(dimension_semantics=tuple(sem) if sem else None, vmem_limit_bytes=VMEM_LIMIT)


def _dot(a, b):
    return jnp.dot(a, b, preferred_element_type=F32)


def _dot_nt(a, b):
    return lax.dot_general(a, b, (((1,), (1,)), ((), ())), preferred_element_type=F32)


def _dot_tn(a, b):
    return lax.dot_general(a, b, (((0,), (0,)), ((), ())), preferred_element_type=F32)


def _dot_hi(a, b):
    return jnp.dot(a, b, preferred_element_type=F32, precision=HIGHEST)


def _sigmoid(x):
    return 1.0 / (1.0 + jnp.exp(-x))


def _gelu(x):
    return 0.5 * x * (1.0 + jnp.tanh(_GELU_C * (x + _GELU_A * x * x * x)))


def _gelu_grad(x):
    t = jnp.tanh(_GELU_C * (x + _GELU_A * x * x * x))
    return 0.5 * (1.0 + t) + 0.5 * x * (1.0 - t * t) * (_GELU_C * (1.0 + 3.0 * _GELU_A * x * x))


def _rms(x):
    r = lax.rsqrt(jnp.mean(x * x, axis=-1, keepdims=True) + EPS)
    return x * r, r


def _rms_bwd(xhat, r, dxhat):
    return r * (dxhat - xhat * jnp.mean(dxhat * xhat, axis=-1, keepdims=True))


def _sum0(x):
    return jnp.sum(x, axis=0, keepdims=True)


def _me():
    return 4 * lax.axis_index("x") + 2 * lax.axis_index("y") + lax.axis_index("c")


def _exchange(srcs, scatter, name):
    n = len(srcs)
    out_shapes = []
    for s in srcs:
        piece = s.shape[1:] if scatter else s.shape
        out_shapes.append(jax.ShapeDtypeStruct((N_DEV,) + tuple(piece), s.dtype))

    def body(*refs):
        ins, outs = refs[:n], refs[n:2 * n]
        send_sems, recv_sems, local_sems = refs[2 * n:]
        x, y, c = lax.axis_index("x"), lax.axis_index("y"), lax.axis_index("c")
        me = 4 * x + 2 * y + c

        def src_of(i, dev):
            return ins[i].at[dev] if scatter else ins[i]

        local = [pltpu.make_async_copy(src_of(i, me), outs[i].at[me], local_sems.at[i]) for i in range(n)]
        for cp in local:
            cp.start()
        sends, recvs = [], []
        for k in range(1, N_DEV):
            px = 1 - x if (k >> 2) & 1 else x
            py = 1 - y if (k >> 1) & 1 else y
            pc = 1 - c if k & 1 else c
            peer = 4 * px + 2 * py + pc
            for i in range(n):
                sends.append(pltpu.make_async_remote_copy(
                    src_ref=src_of(i, peer), dst_ref=outs[i].at[me],
                    send_sem=send_sems.at[k - 1, i], recv_sem=recv_sems.at[k - 1, i],
                    device_id=(px, py, pc), device_id_type=pl.DeviceIdType.MESH))
                recvs.append(pltpu.make_async_remote_copy(
                    src_ref=src_of(i, peer), dst_ref=outs[i].at[peer],
                    send_sem=send_sems.at[k - 1, i], recv_sem=recv_sems.at[k - 1, i],
                    device_id=(px, py, pc), device_id_type=pl.DeviceIdType.MESH))
        for cp in sends:
            cp.start()
        for cp in recvs:
            cp.wait_recv()
        for cp in sends:
            cp.wait_send()
        for cp in local:
            cp.wait()

    hbm = pl.BlockSpec(memory_space=pltpu.HBM)
    return pl.pallas_call(
        body, name=name, out_shape=out_shapes,
        in_specs=[hbm] * n, out_specs=[hbm] * n,
        scratch_shapes=[pltpu.SemaphoreType.DMA((N_DEV - 1, n)), pltpu.SemaphoreType.DMA((N_DEV - 1, n)),
                        pltpu.SemaphoreType.DMA((n,))],
    )(*srcs)


def _peers():
    x, y, c = lax.axis_index("x"), lax.axis_index("y"), lax.axis_index("c")
    out = []
    for k in range(1, N_DEV):
        px = 1 - x if (k >> 2) & 1 else x
        py = 1 - y if (k >> 1) & 1 else y
        pc = 1 - c if k & 1 else c
        out.append((k, (px, py, pc), 4 * px + 2 * py + pc))
    return out


def _split_copies(ins, lands, send_sems, recv_sems, scatter, with_recvs):
    me = _me()
    sends, recvs = [], []
    for k, dev, peer in _peers():
        for i in range(len(ins)):
            src = ins[i].at[peer] if scatter else ins[i]
            slot = (k - 1) * len(ins) + i
            sems = dict(send_sem=send_sems.at[slot], recv_sem=recv_sems.at[slot],
                        device_id=dev, device_id_type=pl.DeviceIdType.MESH)
            sends.append(pltpu.make_async_remote_copy(src_ref=src, dst_ref=lands[i].at[me], **sems))
            if with_recvs:
                recvs.append(pltpu.make_async_remote_copy(src_ref=src, dst_ref=lands[i].at[peer], **sems))
    return sends, recvs


_HBM = pl.BlockSpec(memory_space=pltpu.HBM)
_SEM = pl.BlockSpec(memory_space=pltpu.SEMAPHORE)
_EFFECT = pltpu.SideEffectType.DATAFLOW_SIDE_EFFECTING


def _place_own(srcs, scatter, name):
    n = len(srcs)
    out_shapes = [jax.ShapeDtypeStruct((N_DEV,) + tuple(s.shape[1:] if scatter else s.shape), s.dtype) for s in srcs]

    def body(*refs):
        ins, outs, sems = refs[:n], refs[n:2 * n], refs[2 * n]
        me = _me()
        cps = [pltpu.make_async_copy(ins[i].at[me] if scatter else ins[i], outs[i].at[me], sems.at[i]) for i in range(n)]
        for cp in cps:
            cp.start()
        for cp in cps:
            cp.wait()

    return pl.pallas_call(
        body, name=name, out_shape=out_shapes, in_specs=[_HBM] * n, out_specs=[_HBM] * n,
        scratch_shapes=[pltpu.SemaphoreType.DMA((n,))],
    )(*srcs)


def _exchange_start(srcs, lands, after, scatter, name):
    n = len(srcs)

    def body(*refs):
        ins, land_in = refs[:n], refs[n:2 * n]
        send_sems, recv_sems = refs[2 * n + 1], refs[2 * n + 2]
        token = refs[-1]
        sends, _ = _split_copies(ins, land_in, send_sems, recv_sems, scatter, False)
        for cp in sends:
            cp.start()
        token[...] = jnp.zeros_like(token)

    sem = pltpu.SemaphoreType.DMA(((N_DEV - 1) * n,))
    out = pl.pallas_call(
        body, name=name,
        out_shape=(sem, sem, *[pltpu.HBM(s.shape, s.dtype) for s in srcs], *[pltpu.HBM(s.shape, s.dtype) for s in lands],
                   jax.ShapeDtypeStruct((SUBLANES, LANES), F32)),
        in_specs=[_HBM] * (2 * n) + [pl.BlockSpec(memory_space=pl.ANY)],
        out_specs=(_SEM, _SEM, *[_HBM] * (2 * n), pl.BlockSpec(memory_space=pltpu.VMEM)),
        input_output_aliases={i: 2 + i for i in range(2 * n)},
        compiler_params=pltpu.CompilerParams(has_side_effects=_EFFECT),
    )(*srcs, *lands, after)
    return (out[0], out[1], out[2:2 + n], out[2 + n:2 + 2 * n]), out[-1]


def _exchange_wait(handle, after, scatter, name):
    send_sems, recv_sems, srcs, lands = handle
    n = len(srcs)

    def body(*refs):
        ins, land_in = refs[:n], refs[n:2 * n]
        sends, recvs = _split_copies(ins, land_in, refs[2 * n], refs[2 * n + 1], scatter, True)
        for cp in sends:
            cp.wait_send()
        for cp in recvs:
            cp.wait_recv()

    out = pl.pallas_call(
        body, name=name,
        out_shape=(*[pltpu.HBM(s.shape, s.dtype) for s in srcs], *[pltpu.HBM(s.shape, s.dtype) for s in lands]),
        in_specs=[_HBM] * (2 * n) + [_SEM, _SEM, pl.BlockSpec(memory_space=pl.ANY)],
        out_specs=tuple([_HBM] * (2 * n)),
        input_output_aliases={i: i for i in range(2 * n)},
        compiler_params=pltpu.CompilerParams(has_side_effects=_EFFECT),
    )(*srcs, *lands, send_sems, recv_sems, after)
    return out[n:]


def _cond_fwd(c_all, ada_w, ada_b_mine):
    ncol = ada_w.shape[-1]

    def body(c_ref, w_ref, b_ref, o_ref):
        c = c_ref[...]
        ca = (c * _sigmoid(c)).astype(BF16)
        o_ref[0] = _dot(ca, w_ref[0].astype(BF16)) + b_ref[0]

    return pl.pallas_call(
        body, name="cond_fwd", grid=(DEPTH,),
        out_shape=jax.ShapeDtypeStruct((DEPTH, N_DEV, ncol), F32),
        in_specs=[pl.BlockSpec((N_DEV, D_MODEL), lambda l: (0, 0)),
                  pl.BlockSpec((1, D_MODEL, ncol), lambda l: (l, 0, 0)),
                  pl.BlockSpec((1, 1, ncol), lambda l: (l, 0, 0))],
        out_specs=pl.BlockSpec((1, N_DEV, ncol), lambda l: (l, 0, 0)),
        compiler_params=_params("arbitrary"),
    )(c_all, ada_w, ada_b_mine)


def _cond_bwd(c_all_t, dcond_mine, dcond_all):
    ncol = dcond_mine.shape[-1]
    nall = dcond_all.shape[-1]

    def body(ct_ref, d_ref, da_ref, gw_ref, gb_ref):
        ct = ct_ref[...]
        ct = ct * _sigmoid(ct)
        d = d_ref[0]
        acc = ct[:, 0:1] * d[0:1, :]
        for b in range(1, N_DEV):
            acc = acc + ct[:, b:b + 1] * d[b:b + 1, :]
        gw_ref[0] = acc
        gb_ref[0] = _sum0(da_ref[0])

    return pl.pallas_call(
        body, name="cond_bwd", grid=(DEPTH,),
        out_shape=(jax.ShapeDtypeStruct((DEPTH, D_MODEL, ncol), F32), jax.ShapeDtypeStruct((DEPTH, 1, nall), F32)),
        in_specs=[pl.BlockSpec((D_MODEL, N_DEV), lambda l: (0, 0)),
                  pl.BlockSpec((1, N_DEV, ncol), lambda l: (l, 0, 0)),
                  pl.BlockSpec((1, N_DEV, nall), lambda l: (l, 0, 0))],
        out_specs=(pl.BlockSpec((1, D_MODEL, ncol), lambda l: (l, 0, 0)),
                   pl.BlockSpec((1, 1, nall), lambda l: (l, 0, 0))),
        compiler_params=_params("arbitrary"),
    )(c_all_t, dcond_mine, dcond_all)


def _modnorm(x, g, shift, scale):
    xhat, r = _rms(x)
    return (xhat * g) * (1.0 + scale) + shift, xhat, r


def _modnorm_bwd(xhat, r, g, scale, dh):
    n = xhat * g
    dn = dh * (1.0 + scale)
    dx = _rms_bwd(xhat, r, dn * g)
    return dx, _sum0(dh), _sum0(dh * n), _sum0(dn * xhat)


def _row_spec(rows):
    return pl.BlockSpec((rows, D_MODEL), lambda *_: (0, 0))


def _ffn_fwd(x, cond3, g, w_in_g, w_out_g):
    tm = TOKEN_TILE
    last = N_FF_CHUNK - 1

    def body(x_ref, cond_ref, g_ref, wa_ref, wb_ref, wo_ref, xo_ref, f_ref, h_scr, acc_scr):
        j = pl.program_id(1)

        @pl.when(j == 0)
        def _():
            h, _, _ = _modnorm(x_ref[...], g_ref[...], cond_ref[0:1, :], cond_ref[1:2, :])
            h_scr[...] = h.astype(BF16)
            acc_scr[...] = jnp.zeros_like(acc_scr)

        h = h_scr[...]
        a = _dot(h, wa_ref[0])
        b = _dot(h, wb_ref[0])
        act = (a * _sigmoid(a)) * b
        acc_scr[...] += _dot(act.astype(BF16), wo_ref[0])

        @pl.when(j == last)
        def _():
            f = acc_scr[...]
            f_ref[...] = f
            xo_ref[...] = x_ref[...] + (0.5 * cond_ref[2:3, :]) * f

    tok = pl.BlockSpec((tm, D_MODEL), lambda i, j: (i, 0))
    return pl.pallas_call(
        body, name="ffn_fwd", grid=(SEQ // tm, N_FF_CHUNK),
        out_shape=(jax.ShapeDtypeStruct((SEQ, D_MODEL), F32), jax.ShapeDtypeStruct((SEQ, D_MODEL), F32)),
        in_specs=[tok, _row_spec(3), _row_spec(1),
                  pl.BlockSpec((1, D_MODEL, FF_PIECE), lambda i, j: (j, 0, 0)),
                  pl.BlockSpec((1, D_MODEL, FF_PIECE), lambda i, j: (j + N_FF_CHUNK, 0, 0)),
                  pl.BlockSpec((1, FF_PIECE, D_MODEL), lambda i, j: (j, 0, 0))],
        out_specs=(tok, tok),
        scratch_shapes=[pltpu.VMEM((tm, D_MODEL), BF16), pltpu.VMEM((tm, D_MODEL), F32)],
        compiler_params=_params("arbitrary", "arbitrary"),
    )(x, cond3, g, w_in_g, w_in_g, w_out_g)


def _ffn_bwd(dy, x, f, cond3, g, w_in_g, w_out_g):
    tm = TOKEN_TILE
    last = N_FF_CHUNK - 1

    def body(dy_ref, x_ref, f_ref, cond_ref, g_ref, wa_ref, wb_ref, wo_ref,
             dx_ref, da_ref, db_ref, act_ref, h_ref, do_ref, part_ref, h_scr, do_scr, dh_scr):
        i, j = pl.program_id(0), pl.program_id(1)

        @pl.when(j == 0)
        def _():
            h, _, _ = _modnorm(x_ref[...], g_ref[...], cond_ref[0:1, :], cond_ref[1:2, :])
            hb = h.astype(BF16)
            h_scr[...] = hb
            h_ref[...] = hb
            do = ((0.5 * cond_ref[2:3, :]) * dy_ref[...]).astype(BF16)
            do_scr[...] = do
            do_ref[...] = do
            dh_scr[...] = jnp.zeros_like(dh_scr)

        @pl.when((i == 0) & (j == 0))
        def _():
            part_ref[...] = jnp.zeros_like(part_ref)

        h = h_scr[...]
        do = do_scr[...]
        a = _dot(h, wa_ref[0])
        b = _dot(h, wb_ref[0])
        dact = _dot_nt(do, wo_ref[0])
        sig = _sigmoid(a)
        s = a * sig
        da = (dact * b * (sig * (1.0 + a * (1.0 - sig)))).astype(BF16)
        db = (dact * s).astype(BF16)
        da_ref[0] = da
        db_ref[0] = db
        act_ref[0] = (s * b).astype(BF16)
        dh_scr[...] += _dot_nt(da, wa_ref[0]) + _dot_nt(db, wb_ref[0])

        @pl.when(j == last)
        def _():
            dyv = dy_ref[...]
            xhat, r = _rms(x_ref[...])
            dx, dshift, dscale, dg = _modnorm_bwd(xhat, r, g_ref[...], cond_ref[1:2, :], dh_scr[...])
            dx_ref[...] = dyv + dx
            part_ref[0:1, :] += dshift
            part_ref[1:2, :] += dscale
            part_ref[2:3, :] += _sum0(0.5 * dyv * f_ref[...])
            part_ref[3:4, :] += dg

    tok = pl.BlockSpec((tm, D_MODEL), lambda i, j: (i, 0))
    chunk = pl.BlockSpec((1, tm, FF_PIECE), lambda i, j: (j, i, 0))
    chunk_shape = jax.ShapeDtypeStruct((N_FF_CHUNK, SEQ, FF_PIECE), BF16)
    return pl.pallas_call(
        body, name="ffn_bwd", grid=(SEQ // tm, N_FF_CHUNK),
        out_shape=(jax.ShapeDtypeStruct((SEQ, D_MODEL), F32), chunk_shape, chunk_shape, chunk_shape,
                   jax.ShapeDtypeStruct((SEQ, D_MODEL), BF16), jax.ShapeDtypeStruct((SEQ, D_MODEL), BF16),
                   jax.ShapeDtypeStruct((SUBLANES, D_MODEL), F32)),
        in_specs=[tok, tok, tok, _row_spec(3), _row_spec(1),
                  pl.BlockSpec((1, D_MODEL, FF_PIECE), lambda i, j: (j, 0, 0)),
                  pl.BlockSpec((1, D_MODEL, FF_PIECE), lambda i, j: (j + N_FF_CHUNK, 0, 0)),
                  pl.BlockSpec((1, FF_PIECE, D_MODEL), lambda i, j: (j, 0, 0))],
        out_specs=(tok, chunk, chunk, chunk, tok, tok, _row_spec(SUBLANES)),
        scratch_shapes=[pltpu.VMEM((tm, D_MODEL), BF16), pltpu.VMEM((tm, D_MODEL), BF16),
                        pltpu.VMEM((tm, D_MODEL), F32)],
        compiler_params=_params("arbitrary", "arbitrary"),
    )(dy, x, f, cond3, g, w_in_g, w_in_g, w_out_g)


def _dw(lhs, rhs_a, rhs_b=None, name="dw"):
    pl_, s, m = lhs.shape
    pa, _, nn = rhs_a.shape
    pb = 0 if rhs_b is None else rhs_b.shape[0]
    p_out = max(pl_, pa + pb)
    tmm = 512 if m % 512 == 0 else m
    two = rhs_b is not None

    def body(*refs):
        if two:
            l_ref, a_ref, b_ref, o_ref = refs
        else:
            l_ref, a_ref, o_ref = refs
        p = pl.program_id(0)
        if two:
            @pl.when(p < pa)
            def _():
                o_ref[0] = _dot_tn(l_ref[0], a_ref[0]).astype(BF16)

            @pl.when(p >= pa)
            def _():
                o_ref[0] = _dot_tn(l_ref[0], b_ref[0]).astype(BF16)
        else:
            o_ref[0] = _dot_tn(l_ref[0], a_ref[0]).astype(BF16)

    in_specs = [pl.BlockSpec((1, s, tmm), (lambda p, q: (p, 0, q)) if pl_ > 1 else (lambda p, q: (0, 0, q)))]
    if two:
        in_specs.append(pl.BlockSpec((1, s, nn), lambda p, q: (jnp.minimum(p, pa - 1), 0, 0)))
        in_specs.append(pl.BlockSpec((1, s, nn), lambda p, q: (jnp.maximum(p - pa, 0), 0, 0)))
        args = (lhs, rhs_a, rhs_b)
    else:
        in_specs.append(pl.BlockSpec((1, s, nn), (lambda p, q: (p, 0, 0)) if pa > 1 else (lambda p, q: (0, 0, 0))))
        args = (lhs, rhs_a)
    return pl.pallas_call(
        body, name=name, grid=(p_out, m // tmm),
        out_shape=jax.ShapeDtypeStruct((p_out, m, nn), BF16),
        in_specs=in_specs, out_specs=pl.BlockSpec((1, tmm, nn), lambda p, q: (p, q, 0)),
        compiler_params=_params("arbitrary", "arbitrary"),
    )(*args)


def _mix_in_fwd(x, cond2, g, w):
    tm = TOKEN_TILE

    def body(x_ref, cond_ref, g_ref, w_ref, z_ref):
        h, _, _ = _modnorm(x_ref[...], g_ref[...], cond_ref[0:1, :], cond_ref[1:2, :])
        z_ref[...] = _dot(h.astype(BF16), w_ref[...])

    return pl.pallas_call(
        body, name="mix_in_fwd", grid=(SEQ // tm,),
        out_shape=jax.ShapeDtypeStruct((SEQ, P_IN), F32),
        in_specs=[pl.BlockSpec((tm, D_MODEL), lambda i: (i, 0)), _row_spec(2), _row_spec(1),
                  pl.BlockSpec((D_MODEL, P_IN), lambda i: (0, 0))],
        out_specs=pl.BlockSpec((tm, P_IN), lambda i: (i, 0)),
        compiler_params=_params("arbitrary"),
    )(x, cond2, g, w)


MIX_SLABS = ((0, 2 * W_GRP), (2 * W_GRP, 3 * W_GRP), (3 * W_GRP, 6 * W_GRP), (6 * W_GRP, 7 * W_GRP))


def _mix_in_bwd(dzs, x, dy, cond2, g, w):
    tm = TOKEN_TILE

    def body(dza_ref, dzb_ref, dzc_ref, dzd_ref, x_ref, dy_ref, cond_ref, g_ref, w_ref, dx_ref, h_ref, dzo_ref, part_ref):
        i = pl.program_id(0)

        @pl.when(i == 0)
        def _():
            part_ref[...] = jnp.zeros_like(part_ref)

        h, xhat, r = _modnorm(x_ref[...], g_ref[...], cond_ref[0:1, :], cond_ref[1:2, :])
        h_ref[...] = h.astype(BF16)
        dh = None
        for (lo, hi), d_ref in zip(MIX_SLABS, (dza_ref, dzb_ref, dzc_ref, dzd_ref)):
            dzb = d_ref[...].astype(BF16)
            dzo_ref[:, lo:hi] = dzb
            t = _dot_nt(dzb, w_ref[:, lo:hi])
            dh = t if dh is None else dh + t
        dx, dshift, dscale, dg = _modnorm_bwd(xhat, r, g_ref[...], cond_ref[1:2, :], dh)
        dx_ref[...] = dy_ref[...] + dx
        part_ref[0:1, :] += dshift
        part_ref[1:2, :] += dscale
        part_ref[2:3, :] += dg

    tok = pl.BlockSpec((tm, D_MODEL), lambda i: (i, 0))
    ztok = pl.BlockSpec((tm, P_IN), lambda i: (i, 0))
    slabs = [pl.BlockSpec((tm, hi - lo), lambda i: (i, 0)) for lo, hi in MIX_SLABS]
    return pl.pallas_call(
        body, name="mix_in_bwd", grid=(SEQ // tm,),
        out_shape=(jax.ShapeDtypeStruct((SEQ, D_MODEL), F32), jax.ShapeDtypeStruct((SEQ, D_MODEL), BF16),
                   jax.ShapeDtypeStruct((SEQ, P_IN), BF16), jax.ShapeDtypeStruct((SUBLANES, D_MODEL), F32)),
        in_specs=[*slabs, tok, tok, _row_spec(2), _row_spec(1), pl.BlockSpec((D_MODEL, P_IN), lambda i: (0, 0))],
        out_specs=(tok, tok, ztok, _row_spec(SUBLANES)),
        compiler_params=_params("arbitrary"),
    )(*dzs, x, dy, cond2, g, w)


def _group_norm(ys, g_ref):
    out = []
    for k, y in enumerate(ys):
        yhat, r = _rms(y)
        out.append((yhat, r, g_ref[:, k * W_GRP:(k + 1) * W_GRP]))
    return out


def _mix_out_fwd(x, ys, g, gate, w):
    tm = TOKEN_TILE

    def body(x_ref, ya_ref, yb_ref, yc_ref, yd_ref, g_ref, gate_ref, w_ref, xo_ref):
        acc = None
        for k, (yhat, _, gk) in enumerate(_group_norm([r[...] for r in (ya_ref, yb_ref, yc_ref, yd_ref)], g_ref)):
            t = _dot((yhat * gk).astype(BF16), w_ref[k * W_GRP:(k + 1) * W_GRP, :])
            acc = t if acc is None else acc + t
        xo_ref[...] = x_ref[...] + gate_ref[...] * acc

    tok = pl.BlockSpec((tm, D_MODEL), lambda i: (i, 0))
    ytok = pl.BlockSpec((tm, W_GRP), lambda i: (i, 0))
    return pl.pallas_call(
        body, name="mix_out_fwd", grid=(SEQ // tm,),
        out_shape=jax.ShapeDtypeStruct((SEQ, D_MODEL), F32),
        in_specs=[tok, ytok, ytok, ytok, ytok, _row_spec(1), _row_spec(1),
                  pl.BlockSpec((D_MODEL, D_MODEL), lambda i: (0, 0))],
        out_specs=tok, compiler_params=_params("arbitrary"),
    )(x, *ys, g, gate, w)


def _mix_out_bwd(dy, ys, g, gate, w):
    tm = TOKEN_TILE

    def body(dy_ref, ya_ref, yb_ref, yc_ref, yd_ref, g_ref, gate_ref, w_ref,
             da_ref, db_ref, dc_ref, dd_ref, yn_ref, dmo_ref, part_ref):
        i = pl.program_id(0)

        @pl.when(i == 0)
        def _():
            part_ref[...] = jnp.zeros_like(part_ref)

        dyv = dy_ref[...]
        dmo = (gate_ref[...] * dyv).astype(BF16)
        dmo_ref[...] = dmo
        dyn = _dot_nt(dmo, w_ref[...])
        norms = _group_norm([r[...] for r in (ya_ref, yb_ref, yc_ref, yd_ref)], g_ref)
        mo = None
        for k, ((yhat, r, gk), o_ref) in enumerate(zip(norms, (da_ref, db_ref, dc_ref, dd_ref))):
            sl = slice(k * W_GRP, (k + 1) * W_GRP)
            ynk = (yhat * gk).astype(BF16)
            yn_ref[:, sl] = ynk
            t = _dot(ynk, w_ref[sl, :])
            mo = t if mo is None else mo + t
            dk = dyn[:, sl]
            o_ref[...] = _rms_bwd(yhat, r, dk * gk)
            part_ref[1:2, sl] += _sum0(dk * yhat)
        part_ref[0:1, :] += _sum0(dyv * mo)

    tok = pl.BlockSpec((tm, D_MODEL), lambda i: (i, 0))
    ytok = pl.BlockSpec((tm, W_GRP), lambda i: (i, 0))
    ysh = jax.ShapeDtypeStruct((SEQ, W_GRP), F32)
    return pl.pallas_call(
        body, name="mix_out_bwd", grid=(SEQ // tm,),
        out_shape=(ysh, ysh, ysh, ysh, jax.ShapeDtypeStruct((SEQ, D_MODEL), BF16),
                   jax.ShapeDtypeStruct((SEQ, D_MODEL), BF16), jax.ShapeDtypeStruct((SUBLANES, D_MODEL), F32)),
        in_specs=[tok, ytok, ytok, ytok, ytok, _row_spec(1), _row_spec(1),
                  pl.BlockSpec((D_MODEL, D_MODEL), lambda i: (0, 0))],
        out_specs=(ytok, ytok, ytok, ytok, tok, tok, _row_spec(SUBLANES)),
        compiler_params=_params("arbitrary"),
    )(dy, *ys, g, gate, w)


def _shift_down(v, k, rows):
    return jnp.where(rows >= k, pltpu.roll(v, k, axis=0), 0.0)


def _shift_up(v, k, rows):
    n = v.shape[0]
    return jnp.where(rows < n - k, pltpu.roll(v, n - k, axis=0), 0.0)


def _zslab(width, index):
    return pl.BlockSpec((SEQ, width), lambda *_: (0, index))


def _full(shape):
    return pl.BlockSpec(shape, lambda *_: (0,) * len(shape))


def _head_avg():
    r = lax.broadcasted_iota(jnp.int32, (W_GRP, W_GRP), 0) // SGU_HEAD_DIM
    c = lax.broadcasted_iota(jnp.int32, (W_GRP, W_GRP), 1) // SGU_HEAD_DIM
    return jnp.where(r == c, 1.0 / SGU_HEAD_DIM, 0.0).astype(F32)


def _sgu_norm(za):
    z = _gelu(za)
    u, v = z[:, :W_GRP], z[:, W_GRP:]
    avg = _head_avg()
    vc = v - _dot_hi(v, avg)
    rstd = lax.rsqrt(_dot_hi(vc * vc, avg) + EPS)
    return u, vc * rstd, rstd


def _sgu_masked_w(w_ref):
    t = lax.broadcasted_iota(jnp.int32, (CHUNK, CHUNK), 0)
    s = lax.broadcasted_iota(jnp.int32, (CHUNK, CHUNK), 1)
    tril = t >= s
    return [jnp.where(tril, w_ref[:, h * CHUNK:(h + 1) * CHUNK], 0.0).astype(BF16) for h in range(SGU_HEADS)]


def _head_of_lane():
    return lax.broadcasted_iota(jnp.int32, (CHUNK, W_GRP), 1) // SGU_HEAD_DIM


def _sgu_fwd(z, w_cat, bias):
    def body(z_ref, w_ref, b_ref, y_ref, vn_scr, u_scr):
        u, vn, _ = _sgu_norm(z_ref[...])
        vn_scr[...] = vn.astype(BF16)
        u_scr[...] = u
        ws = _sgu_masked_w(w_ref)
        head = _head_of_lane()
        bias_v = b_ref[...]

        def chunk(n, carry):
            rows = pl.ds(pl.multiple_of(n * CHUNK, CHUNK), CHUNK)
            vb = vn_scr[rows, :]
            mixed = bias_v
            for h in range(SGU_HEADS):
                mixed = mixed + jnp.where(head == h, _dot(ws[h], vb), 0.0)
            y_ref[rows, :] = u_scr[rows, :] * mixed
            return carry

        lax.fori_loop(0, SEQ // CHUNK, chunk, 0)

    return pl.pallas_call(
        body, name="sgu_fwd", grid=(1,),
        out_shape=jax.ShapeDtypeStruct((SEQ, W_GRP), F32),
        in_specs=[_zslab(2 * W_GRP, 0), _full((CHUNK, SGU_HEADS * CHUNK)), _full((CHUNK, W_GRP))],
        out_specs=_full((SEQ, W_GRP)),
        scratch_shapes=[pltpu.VMEM((SEQ, W_GRP), BF16), pltpu.VMEM((SEQ, W_GRP), F32)],
        compiler_params=_params("arbitrary"),
    )(z, w_cat, bias)


def _sgu_bwd(z, dy, w_cat, bias):
    def body(z_ref, dy_ref, w_ref, b_ref, dz_ref, dw_ref, db_ref, vn_scr, u_scr, dvn_scr, du_scr):
        za = z_ref[...]
        u, vn, rstd = _sgu_norm(za)
        vn_scr[...] = vn.astype(BF16)
        u_scr[...] = u
        ws = _sgu_masked_w(w_ref)
        head = _head_of_lane()
        bias_v = b_ref[...]

        def chunk(n, carry):
            dws, dbias = carry
            rows = pl.ds(pl.multiple_of(n * CHUNK, CHUNK), CHUNK)
            vb = vn_scr[rows, :]
            mixed = bias_v
            for h in range(SGU_HEADS):
                mixed = mixed + jnp.where(head == h, _dot(ws[h], vb), 0.0)
            dyc = dy_ref[rows, :]
            du_scr[rows, :] = dyc * mixed
            dmixed = dyc * u_scr[rows, :]
            dvn = jnp.zeros((CHUNK, W_GRP), F32)
            new_dws = []
            for h in range(SGU_HEADS):
                dm = jnp.where(head == h, dmixed, 0.0).astype(BF16)
                new_dws.append(dws[h] + _dot_nt(dm, vb))
                dvn = dvn + _dot_tn(ws[h], dm)
            dvn_scr[rows, :] = dvn
            return tuple(new_dws), dbias + dmixed

        zero_w = tuple(jnp.zeros((CHUNK, CHUNK), F32) for _ in range(SGU_HEADS))
        dws, dbias = lax.fori_loop(0, SEQ // CHUNK, chunk, (zero_w, jnp.zeros((CHUNK, W_GRP), F32)))
        t = lax.broadcasted_iota(jnp.int32, (CHUNK, CHUNK), 0)
        s = lax.broadcasted_iota(jnp.int32, (CHUNK, CHUNK), 1)
        for h in range(SGU_HEADS):
            dw_ref[:, h * CHUNK:(h + 1) * CHUNK] = jnp.where(t >= s, dws[h], 0.0)
        avg = _head_avg()
        db_ref[...] = _dot_hi(dbias, avg) * float(SGU_HEAD_DIM)
        dvn = dvn_scr[...]
        dv = rstd * (dvn - _dot_hi(dvn, avg) - vn * _dot_hi(dvn * vn, avg))
        gg = _gelu_grad(za)
        dz_ref[:, :W_GRP] = gg[:, :W_GRP] * du_scr[...]
        dz_ref[:, W_GRP:] = gg[:, W_GRP:] * dv

    return pl.pallas_call(
        body, name="sgu_bwd", grid=(1,),
        out_shape=(jax.ShapeDtypeStruct((SEQ, 2 * W_GRP), F32), jax.ShapeDtypeStruct((CHUNK, SGU_HEADS * CHUNK), F32),
                   jax.ShapeDtypeStruct((CHUNK, W_GRP), F32)),
        in_specs=[_zslab(2 * W_GRP, 0), _full((SEQ, W_GRP)), _full((CHUNK, SGU_HEADS * CHUNK)), _full((CHUNK, W_GRP))],
        out_specs=(_full((SEQ, 2 * W_GRP)), _full((CHUNK, SGU_HEADS * CHUNK)), _full((CHUNK, W_GRP))),
        scratch_shapes=[pltpu.VMEM((SEQ, W_GRP), BF16), pltpu.VMEM((SEQ, W_GRP), F32),
                        pltpu.VMEM((SEQ, W_GRP), F32), pltpu.VMEM((SEQ, W_GRP), F32)],
        compiler_params=_params("arbitrary"),
    )(z, dy, w_cat, bias)


def _pool_window_of_lane(shape):
    grp = lax.broadcasted_iota(jnp.int32, shape, 1) // POOL_GROUP_DIM
    win = jnp.full(shape, POOL_WINDOWS[0], jnp.int32)
    for k in range(1, len(POOL_WINDOWS)):
        win = jnp.where(grp == k, POOL_WINDOWS[k], win)
    return grp, win


def _pool_select(levels, grp):
    out = levels[0]
    for k in range(1, len(levels)):
        out = jnp.where(grp == k, levels[k], out)
    return out


def _pool_p(z):
    shape = z.shape
    rows = lax.broadcasted_iota(jnp.int32, shape, 0)
    grp, win = _pool_window_of_lane(shape)
    levels, s, k = [], z, 1
    for _ in POOL_WINDOWS:
        s = s + _shift_down(s, k, rows)
        levels.append(s)
        k *= 2
    inv = 1.0 / jnp.minimum(rows + 1, win).astype(F32)
    return _pool_select(levels, grp) * inv - z, inv, rows, grp


def _pool_fwd(z, w_bd, scale):
    def body(z_ref, w_ref, s_ref, y_ref):
        p, _, _, _ = _pool_p(z_ref[...])
        y_ref[...] = _dot(p.astype(BF16), w_ref[...]) * s_ref[...]

    return pl.pallas_call(
        body, name="pool_fwd", grid=(1,),
        out_shape=jax.ShapeDtypeStruct((SEQ, W_GRP), F32),
        in_specs=[_zslab(W_GRP, 2), _full((W_GRP, W_GRP)), _full((1, W_GRP))],
        out_specs=_full((SEQ, W_GRP)), compiler_params=_params("arbitrary"),
    )(z, w_bd, scale)


def _pool_bwd(z, dy, w_bd, scale):
    def body(z_ref, dy_ref, w_ref, s_ref, dz_ref, dw_ref, ds_ref):
        p, inv, rows, grp = _pool_p(z_ref[...])
        pb = p.astype(BF16)
        dyv = dy_ref[...]
        ds_ref[...] = _sum0(dyv * _dot(pb, w_ref[...]))
        dpre = (dyv * s_ref[...]).astype(BF16)
        dw_ref[...] = _dot_tn(pb, dpre)
        dp = _dot_nt(dpre, w_ref[...])
        q = dp * inv
        levels, s, k = [], q, 1
        for _ in POOL_WINDOWS:
            s = s + _shift_up(s, k, rows)
            levels.append(s)
            k *= 2
        dz_ref[...] = _pool_select(levels, grp) - dp

    return pl.pallas_call(
        body, name="pool_bwd", grid=(1,),
        out_shape=(jax.ShapeDtypeStruct((SEQ, W_GRP), F32), jax.ShapeDtypeStruct((W_GRP, W_GRP), F32),
                   jax.ShapeDtypeStruct((1, W_GRP), F32)),
        in_specs=[_zslab(W_GRP, 2), _full((SEQ, W_GRP)), _full((W_GRP, W_GRP)), _full((1, W_GRP))],
        out_specs=(_full((SEQ, W_GRP)), _full((W_GRP, W_GRP)), _full((1, W_GRP))),
        compiler_params=_params("arbitrary"),
    )(z, dy, w_bd, scale)


def _conv_fwd(z, w):
    def body(z_ref, w_ref, y_ref):
        zc = z_ref[...]
        bg, cg, xh = zc[:, :W_GRP], zc[:, W_GRP:2 * W_GRP], zc[:, 2 * W_GRP:]
        rows = lax.broadcasted_iota(jnp.int32, (SEQ, W_GRP), 0)
        y = cg * xh
        conv = w_ref[0:1, :] * _shift_down(y, 2, rows) + w_ref[1:2, :] * _shift_down(y, 1, rows) + w_ref[2:3, :] * y
        y_ref[...] = bg * conv

    return pl.pallas_call(
        body, name="conv_fwd", grid=(1,),
        out_shape=jax.ShapeDtypeStruct((SEQ, W_GRP), F32),
        in_specs=[_zslab(3 * W_GRP, 1), _full((3, W_GRP))],
        out_specs=_full((SEQ, W_GRP)), compiler_params=_params("arbitrary"),
    )(z, w)


def _conv_bwd(z, dy, w):
    def body(z_ref, dy_ref, w_ref, dz_ref, dw_ref):
        zc = z_ref[...]
        bg, cg, xh = zc[:, :W_GRP], zc[:, W_GRP:2 * W_GRP], zc[:, 2 * W_GRP:]
        rows = lax.broadcasted_iota(jnp.int32, (SEQ, W_GRP), 0)
        y = cg * xh
        y2, y1 = _shift_down(y, 2, rows), _shift_down(y, 1, rows)
        conv = w_ref[0:1, :] * y2 + w_ref[1:2, :] * y1 + w_ref[2:3, :] * y
        dyv = dy_ref[...]
        dconv = dyv * bg
        dw_ref[...] = jnp.zeros_like(dw_ref)
        dw_ref[0:1, :] = _sum0(dconv * y2)
        dw_ref[1:2, :] = _sum0(dconv * y1)
        dw_ref[2:3, :] = _sum0(dconv * y)
        dyy = (w_ref[0:1, :] * _shift_up(dconv, 2, rows) + w_ref[1:2, :] * _shift_up(dconv, 1, rows)
               + w_ref[2:3, :] * dconv)
        dz_ref[:, :W_GRP] = dyv * conv
        dz_ref[:, W_GRP:2 * W_GRP] = dyy * xh
        dz_ref[:, 2 * W_GRP:] = dyy * cg

    return pl.pallas_call(
        body, name="conv_bwd", grid=(1,),
        out_shape=(jax.ShapeDtypeStruct((SEQ, 3 * W_GRP), F32), jax.ShapeDtypeStruct((SUBLANES, W_GRP), F32)),
        in_specs=[_zslab(3 * W_GRP, 1), _full((SEQ, W_GRP)), _full((3, W_GRP))],
        out_specs=(_full((SEQ, 3 * W_GRP)), _full((SUBLANES, W_GRP))),
        compiler_params=_params("arbitrary"),
    )(z, dy, w)


def _s5_disc(lre, lim, ldt, br, bi):
    dt = jnp.exp(ldt)
    mag = jnp.exp(lre * dt)
    ang = lim * dt
    a_re, a_im = mag * jnp.cos(ang), mag * jnp.sin(ang)
    nr, ni = a_re - 1.0, a_im
    den = lre * lre + lim * lim
    k_re = (nr * lre + ni * lim) / den
    k_im = (ni * lre - nr * lim) / den
    return a_re, a_im, k_re * br - k_im * bi, k_re * bi + k_im * br


def _s5_prep_fwd(lre, lim, ldt, br, bi):
    def body(lre_ref, lim_ref, ldt_ref, br_ref, bi_ref, ar_ref, ai_ref, bbr_ref, bbi_ref):
        ar, ai, bbr, bbi = _s5_disc(lre_ref[...], lim_ref[...], ldt_ref[...], br_ref[...], bi_ref[...])
        ar_ref[...] = ar
        ai_ref[...] = ai
        bbr_ref[...] = bbr
        bbi_ref[...] = bbi

    return pl.pallas_call(
        body, name="s5_prep_fwd",
        out_shape=(jax.ShapeDtypeStruct(lre.shape, F32), jax.ShapeDtypeStruct(lre.shape, F32),
                   jax.ShapeDtypeStruct(br.shape, F32), jax.ShapeDtypeStruct(br.shape, F32)),
        compiler_params=_params(),
    )(lre, lim, ldt, br, bi)


def _s5_prep_bwd(lre, lim, ldt, br, bi, dar, dai, dbbr, dbbi):
    def body(lre_ref, lim_ref, ldt_ref, br_ref, bi_ref, dar_ref, dai_ref, dbbr_ref, dbbi_ref,
             o_lre, o_lim, o_ldt, o_br, o_bi):
        _, pull = jax.vjp(_s5_disc, lre_ref[...], lim_ref[...], ldt_ref[...], br_ref[...], bi_ref[...])
        g = pull((dar_ref[...], dai_ref[...], dbbr_ref[...], dbbi_ref[...]))
        for o, v in zip((o_lre, o_lim, o_ldt, o_br, o_bi), g):
            o[...] = v

    return pl.pallas_call(
        body, name="s5_prep_bwd",
        out_shape=tuple(jax.ShapeDtypeStruct(a.shape, F32) for a in (lre, lim, ldt, br, bi)),
        compiler_params=_params(),
    )(lre, lim, ldt, br, bi, dar, dai, dbbr, dbbi)


def _cmul(ar, ai, br, bi):
    return ar * br - ai * bi, ar * bi + ai * br


def _s5_tile_consts(ar, ai, reverse):
    if reverse:
        ai = -ai
    shape = (SUBLANES, LANES)
    row = lax.broadcasted_iota(jnp.int32, shape, 0)
    a1 = (jnp.broadcast_to(ar, shape), jnp.broadcast_to(ai, shape))
    a2 = _cmul(*a1, *a1)
    a4 = _cmul(*a2, *a2)
    a8 = _cmul(*a4, *a4)
    steps = []
    for s, (pr, pi) in ((1, a1), (2, a2), (4, a4)):
        keep = (row < SUBLANES - s) if reverse else (row >= s)
        steps.append((s, jnp.where(keep, pr, 0.0), jnp.where(keep, pi, 0.0)))
    e = (SUBLANES - row) if reverse else (row + 1)
    pr, pi = jnp.ones(shape, F32), jnp.zeros(shape, F32)
    for bit, (qr, qi) in ((1, a1), (2, a2), (4, a4), (8, a8)):
        nr, ni = _cmul(pr, pi, qr, qi)
        hit = (e & bit) != 0
        pr, pi = jnp.where(hit, nr, pr), jnp.where(hit, ni, pi)
    return steps, pr, pi


def _s5_tile(xr, xi, steps, reverse):
    for s, pr, pi in steps:
        sh = SUBLANES - s if reverse else s
        sr, si = pltpu.roll(xr, sh, axis=0), pltpu.roll(xi, sh, axis=0)
        xr, xi = xr + pr * sr - pi * si, xi + pr * si + pi * sr
    return xr, xi


N_TILES = SEQ // SUBLANES


def _s5_fwd(z, b_re, b_im, c_re, c_im, a_re, a_im, d, glu_w, glu_b):
    nblk = S5_LANES // LANES

    def body(u_ref, br_ref, bi_ref, cr_ref, ci_ref, ar_ref, ai_ref, d_ref, gw_ref, gb_ref,
             y_ref, y0_ref, xr_ref, xi_ref, ub_scr, acc_scr):
        jb = pl.program_id(0)

        @pl.when(jb == 0)
        def _():
            ub_scr[...] = u_ref[...].astype(BF16)
            acc_scr[...] = jnp.zeros_like(acc_scr)

        ub = ub_scr[...]
        xr_ref[...] = _dot(ub, br_ref[...])
        xi_ref[...] = _dot(ub, bi_ref[...])
        steps, pr, pi = _s5_tile_consts(ar_ref[...], ai_ref[...], False)

        def tile(t, carry):
            cr, ci = carry
            rows = pl.ds(pl.multiple_of(t * SUBLANES, SUBLANES), SUBLANES)
            xr, xi = _s5_tile(xr_ref[rows, :], xi_ref[rows, :], steps, False)
            xr, xi = xr + pr * cr - pi * ci, xi + pr * ci + pi * cr
            xr_ref[rows, :] = xr
            xi_ref[rows, :] = xi
            return xr[SUBLANES - 1:, :], xi[SUBLANES - 1:, :]

        zero = jnp.zeros((1, LANES), F32)
        lax.fori_loop(0, N_TILES, tile, (zero, zero), unroll=4)
        acc_scr[...] += (_dot(xr_ref[...].astype(BF16), cr_ref[...]) - _dot(xi_ref[...].astype(BF16), ci_ref[...]))

        @pl.when(jb == nblk - 1)
        def _():
            y0 = acc_scr[...] + d_ref[...] * u_ref[...]
            y0_ref[...] = y0
            y1 = _gelu(y0)
            y_ref[...] = y1 * _sigmoid(_dot(y1.astype(BF16), gw_ref[...]) + gb_ref[...])

    lane_blk = pl.BlockSpec((SEQ, LANES), lambda j: (0, j))
    return pl.pallas_call(
        body, name="s5_fwd", grid=(nblk,),
        out_shape=(jax.ShapeDtypeStruct((SEQ, W_GRP), F32), jax.ShapeDtypeStruct((SEQ, W_GRP), F32),
                   jax.ShapeDtypeStruct((SEQ, S5_LANES), F32), jax.ShapeDtypeStruct((SEQ, S5_LANES), F32)),
        in_specs=[_zslab(W_GRP, 6),
                  pl.BlockSpec((W_GRP, LANES), lambda j: (0, j)), pl.BlockSpec((W_GRP, LANES), lambda j: (0, j)),
                  pl.BlockSpec((LANES, W_GRP), lambda j: (j, 0)), pl.BlockSpec((LANES, W_GRP), lambda j: (j, 0)),
                  pl.BlockSpec((1, LANES), lambda j: (0, j)), pl.BlockSpec((1, LANES), lambda j: (0, j)),
                  _full((1, W_GRP)), _full((W_GRP, W_GRP)), _full((1, W_GRP))],
        out_specs=(_full((SEQ, W_GRP)), _full((SEQ, W_GRP)), lane_blk, lane_blk),
        scratch_shapes=[pltpu.VMEM((SEQ, W_GRP), BF16), pltpu.VMEM((SEQ, W_GRP), F32)],
        compiler_params=_params("arbitrary"),
    )(z, b_re, b_im, c_re, c_im, a_re, a_im, d, glu_w, glu_b)


def _s5_bwd(z, y0, dy, xr, xi, b_re, b_im, c_re, c_im, a_re, a_im, d, glu_w, glu_b):
    nblk = S5_LANES // LANES

    def body(u_ref, y0_ref, dy_ref, xr_ref, xi_ref, br_ref, bi_ref, cr_ref, ci_ref, ar_ref, ai_ref,
             d_ref, gw_ref, gb_ref,
             du_ref, dbr_ref, dbi_ref, dcr_ref, dci_ref, dar_ref, dai_ref, dd_ref, dgw_ref, dgb_ref,
             ub_scr, dy0_scr, du_scr, lr_scr, li_scr):
        jb = pl.program_id(0)

        @pl.when(jb == 0)
        def _():
            u = u_ref[...]
            y0v = y0_ref[...]
            y1 = _gelu(y0v)
            y1b = y1.astype(BF16)
            sg = _sigmoid(_dot(y1b, gw_ref[...]) + gb_ref[...])
            dyv = dy_ref[...]
            dpre = dyv * y1 * sg * (1.0 - sg)
            dpb = dpre.astype(BF16)
            dgw_ref[...] = _dot_tn(y1b, dpb)
            dgb_ref[...] = _sum0(dpre)
            dy1 = dyv * sg + _dot_nt(dpb, gw_ref[...])
            dy0 = dy1 * _gelu_grad(y0v)
            dd_ref[...] = _sum0(dy0 * u)
            du_scr[...] = dy0 * d_ref[...]
            dy0_scr[...] = dy0.astype(BF16)
            ub_scr[...] = u.astype(BF16)

        dy0b = dy0_scr[...]
        lr_scr[...] = _dot_nt(dy0b, cr_ref[...])
        li_scr[...] = -_dot_nt(dy0b, ci_ref[...])
        dcr_ref[...] = _dot_tn(xr_ref[...].astype(BF16), dy0b)
        dci_ref[...] = -_dot_tn(xi_ref[...].astype(BF16), dy0b)
        steps, pr, pi = _s5_tile_consts(ar_ref[...], ai_ref[...], True)
        row = lax.broadcasted_iota(jnp.int32, (SUBLANES, LANES), 0)

        def tile(k, carry):
            cr, ci, accr, acci = carry
            t = N_TILES - 1 - k
            rows = pl.ds(pl.multiple_of(t * SUBLANES, SUBLANES), SUBLANES)
            lr, li = _s5_tile(lr_scr[rows, :], li_scr[rows, :], steps, True)
            lr, li = lr + pr * cr - pi * ci, li + pr * ci + pi * cr
            lr_scr[rows, :] = lr
            li_scr[rows, :] = li
            prev = pl.ds(pl.multiple_of(jnp.maximum(t - 1, 0) * SUBLANES, SUBLANES), SUBLANES)
            live = jnp.where(t > 0, 1.0, 0.0)
            xpr = jnp.where(row == 0, pltpu.roll(xr_ref[prev, :], 1, axis=0) * live, pltpu.roll(xr_ref[rows, :], 1, axis=0))
            xpi = jnp.where(row == 0, pltpu.roll(xi_ref[prev, :], 1, axis=0) * live, pltpu.roll(xi_ref[rows, :], 1, axis=0))
            accr = accr + lr * xpr + li * xpi
            acci = acci + li * xpr - lr * xpi
            return lr[0:1, :], li[0:1, :], accr, acci

        zero = jnp.zeros((1, LANES), F32)
        zt = jnp.zeros((SUBLANES, LANES), F32)
        _, _, accr, acci = lax.fori_loop(0, N_TILES, tile, (zero, zero, zt, zt), unroll=4)
        dar_ref[...] = jnp.zeros_like(dar_ref)
        dai_ref[...] = jnp.zeros_like(dai_ref)
        dar_ref[0:1, :] = _sum0(accr)
        dai_ref[0:1, :] = _sum0(acci)
        lrb, lib = lr_scr[...].astype(BF16), li_scr[...].astype(BF16)
        ub = ub_scr[...]
        dbr_ref[...] = _dot_tn(ub, lrb)
        dbi_ref[...] = _dot_tn(ub, lib)
        du_scr[...] += _dot_nt(lrb, br_ref[...]) + _dot_nt(lib, bi_ref[...])

        @pl.when(jb == nblk - 1)
        def _():
            du_ref[...] = du_scr[...]

    lane_blk = pl.BlockSpec((SEQ, LANES), lambda j: (0, j))
    bspec = pl.BlockSpec((W_GRP, LANES), lambda j: (0, j))
    cspec = pl.BlockSpec((LANES, W_GRP), lambda j: (j, 0))
    aspec = pl.BlockSpec((1, LANES), lambda j: (0, j))
    a8spec = pl.BlockSpec((SUBLANES, LANES), lambda j: (0, j))
    sd = jax.ShapeDtypeStruct
    return pl.pallas_call(
        body, name="s5_bwd", grid=(nblk,),
        out_shape=(sd((SEQ, W_GRP), F32), sd((W_GRP, S5_LANES), F32), sd((W_GRP, S5_LANES), F32),
                   sd((S5_LANES, W_GRP), F32), sd((S5_LANES, W_GRP), F32),
                   sd((SUBLANES, S5_LANES), F32), sd((SUBLANES, S5_LANES), F32),
                   sd((1, W_GRP), F32), sd((W_GRP, W_GRP), F32), sd((1, W_GRP), F32)),
        in_specs=[_zslab(W_GRP, 6), _full((SEQ, W_GRP)), _full((SEQ, W_GRP)), lane_blk, lane_blk,
                  bspec, bspec, cspec, cspec, aspec, aspec,
                  _full((1, W_GRP)), _full((W_GRP, W_GRP)), _full((1, W_GRP))],
        out_specs=(_full((SEQ, W_GRP)), bspec, bspec, cspec, cspec, a8spec, a8spec,
                   _full((1, W_GRP)), _full((W_GRP, W_GRP)), _full((1, W_GRP))),
        scratch_shapes=[pltpu.VMEM((SEQ, W_GRP), BF16), pltpu.VMEM((SEQ, W_GRP), BF16), pltpu.VMEM((SEQ, W_GRP), F32),
                        pltpu.VMEM((SEQ, LANES), F32), pltpu.VMEM((SEQ, LANES), F32)],
        compiler_params=_params("arbitrary"),
    )(z, y0, dy, xr, xi, b_re, b_im, c_re, c_im, a_re, a_im, d, glu_w, glu_b)


def _head(x, g, target):
    tm = TOKEN_TILE
    n = SEQ // tm

    def body(x_ref, g_ref, t_ref, dx_ref, st_ref, acc_scr):
        i = pl.program_id(0)

        @pl.when(i == 0)
        def _():
            acc_scr[...] = jnp.zeros_like(acc_scr)

        xhat, r = _rms(x_ref[...])
        gv = g_ref[...]
        err = xhat * gv - t_ref[...]
        dyv = err * (1.0 / D_MODEL)
        dx_ref[...] = _rms_bwd(xhat, r, dyv * gv)
        acc_scr[0:1, :] += _sum0(err * err)
        acc_scr[1:2, :] += _sum0(dyv * xhat)

        @pl.when(i == n - 1)
        def _():
            st_ref[...] = acc_scr[...]
            tot = jnp.sum(acc_scr[0:1, :], axis=-1, keepdims=True) * (0.5 / D_MODEL)
            st_ref[0:1, :] = jnp.broadcast_to(tot, (1, D_MODEL))

    tok = pl.BlockSpec((tm, D_MODEL), lambda i: (i, 0))
    return pl.pallas_call(
        body, name="head", grid=(n,),
        out_shape=(jax.ShapeDtypeStruct((SEQ, D_MODEL), F32), jax.ShapeDtypeStruct((SUBLANES, D_MODEL), F32)),
        in_specs=[tok, _row_spec(1), tok], out_specs=(tok, _row_spec(SUBLANES)),
        scratch_shapes=[pltpu.VMEM((SUBLANES, D_MODEL), F32)],
        compiler_params=_params("arbitrary"),
    )(x, g, target)


def _adamw(w, gparts, m, v, name):
    r, c = w.shape
    npart = gparts.shape[0]
    tr = r
    for cand in (512, 256, 128, 64, 32, 16):
        if r % cand == 0 and r > cand:
            tr = cand
            break
    b1c = 1.0 - ADAM_B1 ** ADAM_STEP
    b2c = 1.0 - ADAM_B2 ** ADAM_STEP

    def body(w_ref, g_ref, m_ref, v_ref, go_ref, d_ref, mo_ref, vo_ref):
        g = g_ref[0].astype(F32)
        for k in range(1, npart):
            g = g + g_ref[k].astype(F32)
        mn = ADAM_B1 * m_ref[...] + (1.0 - ADAM_B1) * g
        vn = ADAM_B2 * v_ref[...] + (1.0 - ADAM_B2) * (g * g)
        m_hat = mn / b1c
        v_hat = vn / b2c
        go_ref[...] = g
        d_ref[...] = -ADAM_LR * (m_hat / (jnp.sqrt(v_hat) + ADAM_EPS) + ADAM_WD * w_ref[...])
        mo_ref[...] = mn
        vo_ref[...] = vn

    blk = pl.BlockSpec((tr, c), lambda i: (i, 0))
    sh = jax.ShapeDtypeStruct((r, c), F32)
    return pl.pallas_call(
        body, name=name, grid=(r // tr,),
        out_shape=(sh, sh, sh, sh),
        in_specs=[blk, pl.BlockSpec((npart, tr, c), lambda i: (0, i, 0)), blk, blk],
        out_specs=(blk, blk, blk, blk), compiler_params=_params("arbitrary"),
    )(w, gparts, m, v)


def _adamw_layer(l, w, gparts, m, v, prev, name):
    _, r, c = w.shape
    npart = gparts.shape[0]
    tr = r
    for cand in (512, 256, 128, 64, 32, 16):
        if r % cand == 0 and r > cand:
            tr = cand
            break
    b1c = 1.0 - ADAM_B1 ** ADAM_STEP
    b2c = 1.0 - ADAM_B2 ** ADAM_STEP
    nprev = 0 if prev is None else 4

    def body(*refs):
        w_ref, g_ref, m_ref, v_ref = refs[:4]
        go_ref, d_ref, mo_ref, vo_ref = refs[4 + nprev:]
        g = g_ref[0].astype(F32)
        for k in range(1, npart):
            g = g + g_ref[k].astype(F32)
        mn = ADAM_B1 * m_ref[0] + (1.0 - ADAM_B1) * g
        vn = ADAM_B2 * v_ref[0] + (1.0 - ADAM_B2) * (g * g)
        go_ref[0] = g
        d_ref[0] = -ADAM_LR * ((mn / b1c) / (jnp.sqrt(vn / b2c) + ADAM_EPS) + ADAM_WD * w_ref[0])
        mo_ref[0] = mn
        vo_ref[0] = vn

    blk = pl.BlockSpec((1, tr, c), lambda i: (l, i, 0))
    sh = jax.ShapeDtypeStruct(w.shape, F32)
    keep = [pl.BlockSpec(memory_space=pl.ANY)] * nprev
    return pl.pallas_call(
        body, name=name, grid=(r // tr,),
        out_shape=(sh, sh, sh, sh),
        in_specs=[blk, pl.BlockSpec((npart, tr, c), lambda i: (0, i, 0)), blk, blk, *keep],
        out_specs=(blk, blk, blk, blk),
        input_output_aliases={4 + k: k for k in range(nprev)},
        compiler_params=_params("arbitrary"),
    )(w, gparts, m, v, *(prev or ()))


def _sum_parts(parts, name):
    n, r, c = parts.shape

    def body(p_ref, o_ref):
        acc = p_ref[0]
        for k in range(1, n):
            acc = acc + p_ref[k]
        o_ref[...] = acc

    return pl.pallas_call(
        body, name=name, out_shape=jax.ShapeDtypeStruct((r, c), F32), compiler_params=_params(),
    )(parts)


def _block_diag(blocks):
    g, a, b = blocks.shape
    eye = jnp.eye(g, dtype=blocks.dtype)
    return (blocks[:, :, None, :] * eye[:, None, :, None]).reshape(g * a, g * b)


def _diag_blocks(dense, g):
    a, b = dense.shape[0] // g, dense.shape[1] // g
    d4 = dense.reshape(g, a, g, b)
    eye = jnp.eye(g, dtype=dense.dtype)
    return jnp.sum(d4 * eye[:, None, :, None], axis=2)


def _pack(parts, cols):
    flat = jnp.concatenate([p.reshape(-1) for p in parts])
    unit = N_DEV * SUBLANES * cols
    total = -(-flat.shape[0] // unit) * unit
    flat = jnp.pad(flat, (0, total - flat.shape[0]))
    return flat.reshape(N_DEV, total // (N_DEV * cols), cols)


def _unpack(flat, shapes):
    out, pos = [], 0
    for s in shapes:
        n = math.prod(s)
        out.append(flat[pos:pos + n].reshape(s))
        pos += n
    return out


SMALL = ("norm1_g", "norm2_g", "sgu_w", "sgu_b", "pool_w", "pool_scale", "conv_w", "s5_lambda_re", "s5_lambda_im",
         "s5_b_re", "s5_b_im", "s5_c_re", "s5_c_im", "s5_d", "s5_log_dt", "s5_glu_w", "s5_glu_b", "mix_norm_g",
         "norm3_g", "final_norm_g")
BIG = ("ffn1_w_in", "ffn1_w_out", "w_mix_in", "w_mix_out", "ffn2_w_in", "ffn2_w_out")
WEIGHTS = ("ada_w", "ada_b", "norm1_g", "ffn1_w_in", "ffn1_w_out", "norm2_g", "w_mix_in", "sgu_w", "sgu_b", "pool_w",
           "pool_scale", "conv_w", "s5_lambda_re", "s5_lambda_im", "s5_b_re", "s5_b_im", "s5_c_re", "s5_c_im", "s5_d",
           "s5_log_dt", "s5_glu_w", "s5_glu_b", "mix_norm_g", "w_mix_out", "norm3_g", "ffn2_w_in", "ffn2_w_out",
           "final_norm_g")
PACK_COLS = 1024


def kernel(x, c, ada_w, ada_b, norm1_g, ffn1_w_in, ffn1_w_out, norm2_g, w_mix_in, sgu_w, sgu_b, pool_w, pool_scale, conv_w, s5_lambda_re, s5_lambda_im, s5_b_re, s5_b_im, s5_c_re, s5_c_im, s5_d, s5_log_dt, s5_glu_w, s5_glu_b, mix_norm_g, w_mix_out, norm3_g, ffn2_w_in, ffn2_w_out, final_norm_g, loss_target, m_ada_w, m_ada_b, m_norm1_g, m_ffn1_w_in, m_ffn1_w_out, m_norm2_g, m_w_mix_in, m_sgu_w, m_sgu_b, m_pool_w, m_pool_scale, m_conv_w, m_s5_lambda_re, m_s5_lambda_im, m_s5_b_re, m_s5_b_im, m_s5_c_re, m_s5_c_im, m_s5_d, m_s5_log_dt, m_s5_glu_w, m_s5_glu_b, m_mix_norm_g, m_w_mix_out, m_norm3_g, m_ffn2_w_in, m_ffn2_w_out, m_final_norm_g, v_ada_w, v_ada_b, v_norm1_g, v_ffn1_w_in, v_ffn1_w_out, v_norm2_g, v_w_mix_in, v_sgu_w, v_sgu_b, v_pool_w, v_pool_scale, v_conv_w, v_s5_lambda_re, v_s5_lambda_im, v_s5_b_re, v_s5_b_im, v_s5_c_re, v_s5_c_im, v_s5_d, v_s5_log_dt, v_s5_glu_w, v_s5_glu_b, v_mix_norm_g, v_w_mix_out, v_norm3_g, v_ffn2_w_in, v_ffn2_w_out, v_final_norm_g):
    args = dict(locals())
    W = {n: args[n] for n in WEIGHTS}
    M = {n: args["m_" + n] for n in WEIGHTS}
    V = {n: args["v_" + n] for n in WEIGHTS}
    me = _me()
    L = DEPTH
    x0 = x[0]
    target = loss_target[0]

    conv_cols = conv_w.shape[-1]
    glu_rows = s5_glu_w.shape[1]
    c_g, conv_g, glu_g = _exchange(
        [c.reshape(SUBLANES, LANES), conv_w.reshape(L * 3, conv_cols), s5_glu_w.reshape(L * glu_rows, W_GRP)],
        False, "gather_small")
    c_all = c_g.reshape(N_DEV, D_MODEL)
    conv_full = conv_g.reshape(N_DEV, L, 3, conv_cols).transpose(1, 2, 0, 3).reshape(L, 3, W_GRP)
    glu_full = glu_g.reshape(N_DEV, L, glu_rows, W_GRP).transpose(1, 0, 2, 3).reshape(L, W_GRP, W_GRP)

    ncol = ada_w.shape[-1]
    ada_b_mine = lax.dynamic_slice_in_dim(ada_b, me * ncol, ncol, axis=1).reshape(L, 1, ncol)
    cond_part = _cond_fwd(c_all, ada_w, ada_b_mine)
    (cond_g,) = _exchange([cond_part.reshape(L * N_DEV, ncol)], False, "gather_cond")
    cond_g = cond_g.reshape(N_DEV, L, N_DEV, ncol)
    cond_mine = lax.dynamic_index_in_dim(cond_g, me, axis=2, keepdims=False)
    cond = cond_mine.transpose(1, 0, 2).reshape(L, N_ADA, D_MODEL)

    lg = L * S5_GROUPS
    lre3 = s5_lambda_re.reshape(lg, S5_STATE, 1)
    lim3 = s5_lambda_im.reshape(lg, S5_STATE, 1)
    ldt3 = s5_log_dt.reshape(lg, 1, 1)
    br3 = s5_b_re.reshape(lg, S5_STATE, S5_GROUP_CH)
    bi3 = s5_b_im.reshape(lg, S5_STATE, S5_GROUP_CH)
    a_re3, a_im3, bb_re3, bb_im3 = _s5_prep_fwd(lre3, lim3, ldt3, br3, bi3)
    a_re = a_re3.reshape(L, 1, S5_LANES)
    a_im = a_im3.reshape(L, 1, S5_LANES)

    def b_mat(bb3, l):
        return _block_diag(bb3.reshape(L, S5_GROUPS, S5_STATE, S5_GROUP_CH)[l].transpose(0, 2, 1)).astype(BF16)

    def c_mat(cw, l):
        return _block_diag(cw[l].transpose(0, 2, 1)).astype(BF16)

    def gather_start(l, after):
        srcs = [W[n][l].astype(BF16) for n in BIG]
        lands = _place_own(srcs, False, "gather_weights_own")
        return _exchange_start(srcs, lands, after, False, "gather_weights_start")

    def gather_finish(handle, after):
        g = _exchange_wait(handle, after, False, "gather_weights_wait")
        out = dict(zip(BIG, g))
        out["ffn1_w_out"] = out["ffn1_w_out"].reshape(N_FF_CHUNK, FF_PIECE, D_MODEL)
        out["ffn2_w_out"] = out["ffn2_w_out"].reshape(N_FF_CHUNK, FF_PIECE, D_MODEL)
        out["w_mix_in"] = out["w_mix_in"].transpose(1, 0, 2).reshape(D_MODEL, P_IN)
        out["w_mix_out"] = out["w_mix_out"].reshape(D_MODEL, D_MODEL)
        return out

    def mixer_consts(l):
        w_cat = sgu_w[l].transpose(1, 0, 2).reshape(CHUNK, SGU_HEADS * CHUNK)
        bias = jnp.repeat(sgu_b[l].T, SGU_HEAD_DIM, axis=1)
        return dict(
            w_cat=w_cat, bias=bias, pool_bd=_block_diag(pool_w[l]).astype(BF16), pool_scale=pool_scale[l][None],
            conv=conv_full[l], b_re=b_mat(bb_re3, l), b_im=b_mat(bb_im3, l), c_re=c_mat(s5_c_re, l),
            c_im=c_mat(s5_c_im, l), a_re=a_re[l], a_im=a_im[l], d=s5_d[l][None], glu_w=glu_full[l].astype(BF16),
            glu_b=s5_glu_b[l][None])

    saved = []
    xc = x0
    handle, _ = gather_start(0, x0)
    wl = gather_finish(handle, x0)
    for l in range(L):
        cl = cond[l]
        if l + 1 < L:
            handle, token = gather_start(l + 1, wl["ffn1_w_in"])
            cl = cl + token[0, 0]
        mc = mixer_consts(l)
        x_a = xc
        x_b, f1 = _ffn_fwd(x_a, cl[0:3], norm1_g[l][None], wl["ffn1_w_in"], wl["ffn1_w_out"])
        z = _mix_in_fwd(x_b, cl[3:5], norm2_g[l][None], wl["w_mix_in"])
        ya = _sgu_fwd(z, mc["w_cat"], mc["bias"])
        yb = _pool_fwd(z, mc["pool_bd"], mc["pool_scale"])
        yc = _conv_fwd(z, mc["conv"])
        yd, y0, sxr, sxi = _s5_fwd(z, mc["b_re"], mc["b_im"], mc["c_re"], mc["c_im"], mc["a_re"], mc["a_im"],
                                   mc["d"], mc["glu_w"], mc["glu_b"])
        ys = (ya, yb, yc, yd)
        x_c = _mix_out_fwd(x_b, ys, mix_norm_g[l][None], cl[5:6], wl["w_mix_out"])
        x_d, f2 = _ffn_fwd(x_c, cl[6:9], norm3_g[l][None], wl["ffn2_w_in"], wl["ffn2_w_out"])
        saved.append(dict(wl=wl, mc=mc, x_a=x_a, x_b=x_b, x_c=x_c, f1=f1, f2=f2, z=z, ys=ys, y0=y0, xr=sxr, xi=sxi))
        xc = x_d
        if l + 1 < L:
            wl = gather_finish(handle, x_d)

    dx, stats = _head(xc, final_norm_g[None], target)
    loss = lax.psum(stats[0, 0], MESH_AXES)

    small_grads = {n: [None] * L for n in SMALL if n != "final_norm_g"}
    small_grads["final_norm_g"] = stats[1]
    big_out = {n: None for n in BIG}
    pending = None

    def finish_scatter(pend, after):
        layer, hnd = pend
        recv = _exchange_wait(hnd, after, True, "scatter_grads_wait")
        for n, r in zip(BIG, recv):
            big_out[n] = _adamw_layer(layer, W[n], r, M[n], V[n], big_out[n], "adamw_" + n)
        return recv[0]

    dcond_rows = [None] * L
    d_are, d_aim, d_bbre, d_bbim = [None] * L, [None] * L, [None] * L, [None] * L
    for l in reversed(range(L)):
        sv = saved[l]
        wl, mc, cl = sv["wl"], sv["mc"], cond[l]
        dx, da, db, act, hb, dob, part3 = _ffn_bwd(dx, sv["x_c"], sv["f2"], cl[6:9], norm3_g[l][None],
                                                   wl["ffn2_w_in"], wl["ffn2_w_out"])
        g_ffn2_in = _dw(hb[None], da, db, name="dw_ffn_in")
        g_ffn2_out = _dw(act, dob[None], name="dw_ffn_out").reshape(N_DEV, D_FF // N_DEV, D_MODEL)
        dya, dyb, dyc, dyd, ynb, dmob, part_mo = _mix_out_bwd(dx, sv["ys"], mix_norm_g[l][None], cl[5:6],
                                                              wl["w_mix_out"])
        g_mix_out = _dw(ynb[None], dmob[None], name="dw_mix_out").reshape(N_DEV, D_MODEL // N_DEV, D_MODEL)
        z = sv["z"]
        dza, dw_cat, dbias = _sgu_bwd(z, dya, mc["w_cat"], mc["bias"])
        dzb, dpool_dense, dpool_scale = _pool_bwd(z, dyb, mc["pool_bd"], mc["pool_scale"])
        dzc, dconv8 = _conv_bwd(z, dyc, mc["conv"])
        (dzd, dbre_d, dbim_d, dcre_d, dcim_d, dar8, dai8, dd, dglu_w, dglu_b) = _s5_bwd(
            z, sv["y0"], dyd, sv["xr"], sv["xi"], mc["b_re"], mc["b_im"], mc["c_re"], mc["c_im"],
            mc["a_re"], mc["a_im"], mc["d"], mc["glu_w"], mc["glu_b"])
        dx, h2b, dzbf, part2 = _mix_in_bwd((dza, dzb, dzc, dzd), sv["x_b"], dx, cl[3:5], norm2_g[l][None],
                                           wl["w_mix_in"])
        g_mix_in = _dw(h2b[None], dzbf[None], name="dw_mix_in")[0]
        g_mix_in = g_mix_in.reshape(D_MODEL, N_DEV, P_IN // N_DEV).transpose(1, 0, 2)
        dx, da, db, act, hb, dob, part1 = _ffn_bwd(dx, sv["x_a"], sv["f1"], cl[0:3], norm1_g[l][None],
                                                   wl["ffn1_w_in"], wl["ffn1_w_out"])
        g_ffn1_in = _dw(hb[None], da, db, name="dw_ffn_in")
        g_ffn1_out = _dw(act, dob[None], name="dw_ffn_out").reshape(N_DEV, D_FF // N_DEV, D_MODEL)
        pieces = [g_ffn1_in, g_ffn1_out, g_mix_in, g_mix_out, g_ffn2_in, g_ffn2_out]
        last = finish_scatter(pending, g_ffn1_in) if pending is not None else g_ffn1_in
        lands = _place_own(pieces, True, "scatter_grads_own")
        handle, _ = _exchange_start(pieces, lands, last, True, "scatter_grads_start")
        pending = (l, handle)
        dcond_rows[l] = jnp.concatenate([part1[0:3], part2[0:2], part_mo[0:1], part3[0:3]], axis=0)
        sg = small_grads
        sg["norm1_g"][l] = part1[3]
        sg["norm2_g"][l] = part2[2]
        sg["norm3_g"][l] = part3[3]
        sg["mix_norm_g"][l] = part_mo[1]
        sg["sgu_w"][l] = dw_cat.reshape(CHUNK, SGU_HEADS, CHUNK).transpose(1, 0, 2)
        sg["sgu_b"][l] = dbias[:, ::SGU_HEAD_DIM].T
        sg["pool_w"][l] = _diag_blocks(dpool_dense, len(POOL_WINDOWS))
        sg["pool_scale"][l] = dpool_scale[0]
        sg["conv_w"][l] = dconv8[0:3]
        sg["s5_c_re"][l] = _diag_blocks(dcre_d, S5_GROUPS).transpose(0, 2, 1)
        sg["s5_c_im"][l] = _diag_blocks(dcim_d, S5_GROUPS).transpose(0, 2, 1)
        sg["s5_d"][l] = dd[0]
        sg["s5_glu_w"][l] = dglu_w
        sg["s5_glu_b"][l] = dglu_b[0]
        d_are[l], d_aim[l] = dar8[0], dai8[0]
        d_bbre[l] = _diag_blocks(dbre_d, S5_GROUPS).transpose(0, 2, 1)
        d_bbim[l] = _diag_blocks(dbim_d, S5_GROUPS).transpose(0, 2, 1)
    grad_x = dx

    g_lre, g_lim, g_ldt, g_br, g_bi = _s5_prep_bwd(
        lre3, lim3, ldt3, br3, bi3,
        jnp.stack(d_are).reshape(lg, S5_STATE, 1), jnp.stack(d_aim).reshape(lg, S5_STATE, 1),
        jnp.stack(d_bbre).reshape(lg, S5_STATE, S5_GROUP_CH), jnp.stack(d_bbim).reshape(lg, S5_STATE, S5_GROUP_CH))
    small = {n: (jnp.stack(v) if isinstance(v, list) and v[0] is not None else v) for n, v in small_grads.items()}
    small["s5_lambda_re"] = g_lre.reshape(s5_lambda_re.shape)
    small["s5_lambda_im"] = g_lim.reshape(s5_lambda_im.shape)
    small["s5_log_dt"] = g_ldt.reshape(s5_log_dt.shape)
    small["s5_b_re"] = g_br.reshape(s5_b_re.shape)
    small["s5_b_im"] = g_bi.reshape(s5_b_im.shape)

    small_shapes = [(L, 3, W_GRP) if n == "conv_w" else (L, W_GRP, W_GRP) if n == "s5_glu_w" else W[n].shape
                    for n in SMALL]
    packed = _pack([small[n].reshape(s) for n, s in zip(SMALL, small_shapes)], PACK_COLS)
    (pieces,) = _exchange([packed], True, "scatter_small")
    mine = _sum_parts(pieces, "sum_small")
    (summed,) = _exchange([mine], False, "gather_small_sums")
    small_sum = dict(zip(SMALL, _unpack(summed.reshape(-1), small_shapes)))
    small_sum["conv_w"] = lax.dynamic_slice_in_dim(small_sum["conv_w"], me * conv_cols, conv_cols, axis=2)
    small_sum["s5_glu_w"] = lax.dynamic_slice_in_dim(small_sum["s5_glu_w"], me * glu_rows, glu_rows, axis=1)

    dcond = jnp.stack(dcond_rows).reshape(L * N_ADA, D_MODEL)
    (dcond_g,) = _exchange([dcond], False, "gather_dcond")
    dcond_all = dcond_g.reshape(N_DEV, L, N_ADA * D_MODEL).transpose(1, 0, 2)
    dcond_mine = lax.dynamic_slice_in_dim(dcond_all, me * ncol, ncol, axis=2)
    g_ada_w, g_ada_b = _cond_bwd(c_all.T, dcond_mine, dcond_all)

    grads, deltas, new_m, new_v = {}, {}, {}, {}
    out = _adamw(ada_w.reshape(L * D_MODEL, ncol), g_ada_w.reshape(1, L * D_MODEL, ncol),
                 m_ada_w.reshape(L * D_MODEL, ncol), v_ada_w.reshape(L * D_MODEL, ncol), "adamw_ada_w")
    grads["ada_w"], deltas["ada_w"], new_m["ada_w"], new_v["ada_w"] = (o.reshape(ada_w.shape) for o in out)
    small_names = SMALL + ("ada_b",)
    small_g = dict(small_sum)
    small_g["ada_b"] = g_ada_b.reshape(ada_b.shape)
    shapes = [W[n].shape for n in small_names]
    pw = _pack([W[n] for n in small_names], PACK_COLS)
    rows = pw.shape[0] * pw.shape[1]
    out = _adamw(pw.reshape(rows, PACK_COLS),
                 _pack([small_g[n] for n in small_names], PACK_COLS).reshape(1, rows, PACK_COLS),
                 _pack([M[n] for n in small_names], PACK_COLS).reshape(rows, PACK_COLS),
                 _pack([V[n] for n in small_names], PACK_COLS).reshape(rows, PACK_COLS), "adamw_small")
    for store, o in zip((grads, deltas, new_m, new_v), out):
        store.update(zip(small_names, _unpack(o.reshape(-1), shapes)))
    finish_scatter(pending, out[0])
    for n in BIG:
        grads[n], deltas[n], new_m[n], new_v[n] = big_out[n]

    return (loss, grad_x[None], *[grads[n] for n in WEIGHTS], *[deltas[n] for n in WEIGHTS],
            *[new_m[n] for n in WEIGHTS], *[new_v[n] for n in WEIGHTS])
```

```python
import functools
import math

import jax
import jax.numpy as jnp
from jax import lax
from jax.experimental import pallas as pl
from jax.experimental.pallas import tpu as pltpu

F32 = jnp.float32
BF16 = jnp.bfloat16

D_MODEL = 1024
SEQ = 2048
DEPTH = 4
N_DEV = 8
W_GRP = 256
CHUNK = 128
SGU_HEADS = 4
SGU_HEAD_DIM = 64
POOL_WINDOWS = (2, 4, 8, 16)
POOL_GROUP_DIM = 64
S5_GROUPS = 16
S5_GROUP_CH = 16
S5_STATE = 64
S5_LANES = S5_GROUPS * S5_STATE
P_IN = 1792
D_FF = 2816
FF_PIECE = 2 * D_FF // N_DEV
N_FF_CHUNK = D_FF // FF_PIECE
N_ADA = 9
EPS = 1e-6
ADAM_LR = 0.001
ADAM_B1 = 0.9
ADAM_B2 = 0.999
ADAM_EPS = 1e-08
ADAM_WD = 0.01
ADAM_STEP = 10

SUBLANES = 8
LANES = 128
VMEM_LIMIT = 56 * 1024 * 1024
TOKEN_TILE = 512
HIGHEST = lax.Precision.HIGHEST
MESH_AXES = ("x", "y", "c")

_GELU_C = math.sqrt(2.0 / math.pi)
_GELU_A = 0.044715


def _params(*sem):
    return pltpu.CompilerParams(dimension_semantics=tuple(sem) if sem else None, vmem_limit_bytes=VMEM_LIMIT)


def _dot(a, b):
    return jnp.dot(a, b, preferred_element_type=F32)


def _dot_nt(a, b):
    return lax.dot_general(a, b, (((1,), (1,)), ((), ())), preferred_element_type=F32)


def _dot_tn(a, b):
    return lax.dot_general(a, b, (((0,), (0,)), ((), ())), preferred_element_type=F32)


def _dot_hi(a, b):
    return jnp.dot(a, b, preferred_element_type=F32, precision=HIGHEST)


def _sigmoid(x):
    return 1.0 / (1.0 + jnp.exp(-x))


def _gelu(x):
    return 0.5 * x * (1.0 + jnp.tanh(_GELU_C * (x + _GELU_A * x * x * x)))


def _gelu_grad(x):
    t = jnp.tanh(_GELU_C * (x + _GELU_A * x * x * x))
    return 0.5 * (1.0 + t) + 0.5 * x * (1.0 - t * t) * (_GELU_C * (1.0 + 3.0 * _GELU_A * x * x))


def _rms(x):
    r = lax.rsqrt(jnp.mean(x * x, axis=-1, keepdims=True) + EPS)
    return x * r, r


def _rms_bwd(xhat, r, dxhat):
    return r * (dxhat - xhat * jnp.mean(dxhat * xhat, axis=-1, keepdims=True))


def _sum0(x):
    return jnp.sum(x, axis=0, keepdims=True)


def _me():
    return 4 * lax.axis_index("x") + 2 * lax.axis_index("y") + lax.axis_index("c")


def _exchange(srcs, scatter, name):
    n = len(srcs)
    out_shapes = []
    for s in srcs:
        piece = s.shape[1:] if scatter else s.shape
        out_shapes.append(jax.ShapeDtypeStruct((N_DEV,) + tuple(piece), s.dtype))

    def body(*refs):
        ins, outs = refs[:n], refs[n:2 * n]
        send_sems, recv_sems, local_sems = refs[2 * n:]
        x, y, c = lax.axis_index("x"), lax.axis_index("y"), lax.axis_index("c")
        me = 4 * x + 2 * y + c

        def src_of(i, dev):
            return ins[i].at[dev] if scatter else ins[i]

        local = [pltpu.make_async_copy(src_of(i, me), outs[i].at[me], local_sems.at[i]) for i in range(n)]
        for cp in local:
            cp.start()
        sends, recvs = [], []
        for k in range(1, N_DEV):
            px = 1 - x if (k >> 2) & 1 else x
            py = 1 - y if (k >> 1) & 1 else y
            pc = 1 - c if k & 1 else c
            peer = 4 * px + 2 * py + pc
            for i in range(n):
                sends.append(pltpu.make_async_remote_copy(
                    src_ref=src_of(i, peer), dst_ref=outs[i].at[me],
                    send_sem=send_sems.at[k - 1, i], recv_sem=recv_sems.at[k - 1, i],
                    device_id=(px, py, pc), device_id_type=pl.DeviceIdType.MESH))
                recvs.append(pltpu.make_async_remote_copy(
                    src_ref=src_of(i, peer), dst_ref=outs[i].at[peer],
                    send_sem=send_sems.at[k - 1, i], recv_sem=recv_sems.at[k - 1, i],
                    device_id=(px, py, pc), device_id_type=pl.DeviceIdType.MESH))
        for cp in sends:
            cp.start()
        for cp in recvs:
            cp.wait_recv()
        for cp in sends:
            cp.wait_send()
        for cp in local:
            cp.wait()

    hbm = pl.BlockSpec(memory_space=pltpu.HBM)
    return pl.pallas_call(
        body, name=name, out_shape=out_shapes,
        in_specs=[hbm] * n, out_specs=[hbm] * n,
        scratch_shapes=[pltpu.SemaphoreType.DMA((N_DEV - 1, n)), pltpu.SemaphoreType.DMA((N_DEV - 1, n)),
                        pltpu.SemaphoreType.DMA((n,))],
    )(*srcs)


def _peers():
    x, y, c = lax.axis_index("x"), lax.axis_index("y"), lax.axis_index("c")
    out = []
    for k in range(1, N_DEV):
        px = 1 - x if (k >> 2) & 1 else x
        py = 1 - y if (k >> 1) & 1 else y
        pc = 1 - c if k & 1 else c
        out.append((k, (px, py, pc), 4 * px + 2 * py + pc))
    return out


def _split_copies(ins, lands, send_sems, recv_sems, scatter, with_recvs):
    me = _me()
    sends, recvs = [], []
    for k, dev, peer in _peers():
        for i in range(len(ins)):
            src = ins[i].at[peer] if scatter else ins[i]
            slot = (k - 1) * len(ins) + i
            sems = dict(send_sem=send_sems.at[slot], recv_sem=recv_sems.at[slot],
                        device_id=dev, device_id_type=pl.DeviceIdType.MESH)
            sends.append(pltpu.make_async_remote_copy(src_ref=src, dst_ref=lands[i].at[me], **sems))
            if with_recvs:
                recvs.append(pltpu.make_async_remote_copy(src_ref=src, dst_ref=lands[i].at[peer], **sems))
    return sends, recvs


_HBM = pl.BlockSpec(memory_space=pltpu.HBM)
_SEM = pl.BlockSpec(memory_space=pltpu.SEMAPHORE)
_EFFECT = pltpu.SideEffectType.DATAFLOW_SIDE_EFFECTING


def _place_own(srcs, scatter, name):
    n = len(srcs)
    halves = 2
    out_shapes, in_specs, out_specs = [], [], []
    for s in srcs:
        r, c = s.shape[-2:]
        out_shapes.append(jax.ShapeDtypeStruct((N_DEV, r, c), s.dtype))
        if scatter:
            in_specs.append(pl.BlockSpec((1, r // halves, c), lambda i, me: (me[0], i, 0)))
        else:
            in_specs.append(pl.BlockSpec((r // halves, c), lambda i, me: (i, 0)))
        out_specs.append(pl.BlockSpec((1, r // halves, c), lambda i, me: (me[0], i, 0)))

    def body(me_ref, *refs):
        for i in range(n):
            refs[n + i][0] = refs[i][0] if scatter else refs[i][...]

    return pl.pallas_call(
        body, name=name, out_shape=out_shapes,
        grid_spec=pltpu.PrefetchScalarGridSpec(num_scalar_prefetch=1, grid=(halves,), in_specs=in_specs,
                                               out_specs=out_specs),
        compiler_params=_params("arbitrary"),
    )(_me().reshape(1).astype(jnp.int32), *srcs)


def _exchange_start(srcs, lands, after, scatter, name):
    n = len(srcs)

    def body(*refs):
        ins, land_in = refs[:n], refs[n:2 * n]
        send_sems, recv_sems = refs[2 * n + 1], refs[2 * n + 2]
        token = refs[-1]
        sends, _ = _split_copies(ins, land_in, send_sems, recv_sems, scatter, False)
        for cp in sends:
            cp.start()
        token[...] = jnp.zeros_like(token)

    sem = pltpu.SemaphoreType.DMA(((N_DEV - 1) * n,))
    out = pl.pallas_call(
        body, name=name,
        out_shape=(sem, sem, *[pltpu.HBM(s.shape, s.dtype) for s in srcs], *[pltpu.HBM(s.shape, s.dtype) for s in lands],
                   jax.ShapeDtypeStruct((SUBLANES, LANES), F32)),
        in_specs=[_HBM] * (2 * n) + [pl.BlockSpec(memory_space=pl.ANY)],
        out_specs=(_SEM, _SEM, *[_HBM] * (2 * n), pl.BlockSpec(memory_space=pltpu.VMEM)),
        input_output_aliases={i: 2 + i for i in range(2 * n)},
        compiler_params=pltpu.CompilerParams(has_side_effects=_EFFECT),
    )(*srcs, *lands, after)
    return (out[0], out[1], out[2:2 + n], out[2 + n:2 + 2 * n]), out[-1]


def _exchange_wait(handle, after, scatter, name):
    send_sems, recv_sems, srcs, lands = handle
    n = len(srcs)

    def body(*refs):
        ins, land_in = refs[:n], refs[n:2 * n]
        sends, recvs = _split_copies(ins, land_in, refs[2 * n], refs[2 * n + 1], scatter, True)
        for cp in sends:
            cp.wait_send()
        for cp in recvs:
            cp.wait_recv()

    out = pl.pallas_call(
        body, name=name,
        out_shape=(*[pltpu.HBM(s.shape, s.dtype) for s in srcs], *[pltpu.HBM(s.shape, s.dtype) for s in lands]),
        in_specs=[_HBM] * (2 * n) + [_SEM, _SEM, pl.BlockSpec(memory_space=pl.ANY)],
        out_specs=tuple([_HBM] * (2 * n)),
        input_output_aliases={i: i for i in range(2 * n)},
        compiler_params=pltpu.CompilerParams(has_side_effects=_EFFECT),
    )(*srcs, *lands, send_sems, recv_sems, after)
    return out[n:]


def _cond_fwd(c_all, ada_w, ada_b_mine):
    ncol = ada_w.shape[-1]

    def body(c_ref, w_ref, b_ref, o_ref):
        c = c_ref[...]
        ca = (c * _sigmoid(c)).astype(BF16)
        o_ref[0] = _dot(ca, w_ref[0].astype(BF16)) + b_ref[0]

    return pl.pallas_call(
        body, name="cond_fwd", grid=(DEPTH,),
        out_shape=jax.ShapeDtypeStruct((DEPTH, N_DEV, ncol), F32),
        in_specs=[pl.BlockSpec((N_DEV, D_MODEL), lambda l: (0, 0)),
                  pl.BlockSpec((1, D_MODEL, ncol), lambda l: (l, 0, 0)),
                  pl.BlockSpec((1, 1, ncol), lambda l: (l, 0, 0))],
        out_specs=pl.BlockSpec((1, N_DEV, ncol), lambda l: (l, 0, 0)),
        compiler_params=_params("arbitrary"),
    )(c_all, ada_w, ada_b_mine)


def _cond_bwd(c_all_t, dcond_mine, dcond_all):
    ncol = dcond_mine.shape[-1]
    nall = dcond_all.shape[-1]

    def body(ct_ref, d_ref, da_ref, gw_ref, gb_ref):
        ct = ct_ref[...]
        ct = ct * _sigmoid(ct)
        d = d_ref[0]
        acc = ct[:, 0:1] * d[0:1, :]
        for b in range(1, N_DEV):
            acc = acc + ct[:, b:b + 1] * d[b:b + 1, :]
        gw_ref[0] = acc
        gb_ref[0] = _sum0(da_ref[0])

    return pl.pallas_call(
        body, name="cond_bwd", grid=(DEPTH,),
        out_shape=(jax.ShapeDtypeStruct((DEPTH, D_MODEL, ncol), F32), jax.ShapeDtypeStruct((DEPTH, 1, nall), F32)),
        in_specs=[pl.BlockSpec((D_MODEL, N_DEV), lambda l: (0, 0)),
                  pl.BlockSpec((1, N_DEV, ncol), lambda l: (l, 0, 0)),
                  pl.BlockSpec((1, N_DEV, nall), lambda l: (l, 0, 0))],
        out_specs=(pl.BlockSpec((1, D_MODEL, ncol), lambda l: (l, 0, 0)),
                   pl.BlockSpec((1, 1, nall), lambda l: (l, 0, 0))),
        compiler_params=_params("arbitrary"),
    )(c_all_t, dcond_mine, dcond_all)


def _modnorm(x, g, shift, scale):
    xhat, r = _rms(x)
    return (xhat * g) * (1.0 + scale) + shift, xhat, r


def _modnorm_bwd(xhat, r, g, scale, dh):
    n = xhat * g
    dn = dh * (1.0 + scale)
    dx = _rms_bwd(xhat, r, dn * g)
    return dx, _sum0(dh), _sum0(dh * n), _sum0(dn * xhat)


def _row_spec(rows):
    return pl.BlockSpec((rows, D_MODEL), lambda *_: (0, 0))


def _ffn_fwd(x, cond3, g, w_in_g, w_out_g):
    tm = TOKEN_TILE
    last = N_FF_CHUNK - 1

    def body(x_ref, cond_ref, g_ref, wa_ref, wb_ref, wo_ref, xo_ref, f_ref, h_scr, acc_scr):
        j = pl.program_id(1)

        @pl.when(j == 0)
        def _():
            h, _, _ = _modnorm(x_ref[...], g_ref[...], cond_ref[0:1, :], cond_ref[1:2, :])
            h_scr[...] = h.astype(BF16)
            acc_scr[...] = jnp.zeros_like(acc_scr)

        h = h_scr[...]
        a = _dot(h, wa_ref[0])
        b = _dot(h, wb_ref[0])
        act = (a * _sigmoid(a)) * b
        acc_scr[...] += _dot(act.astype(BF16), wo_ref[0])

        @pl.when(j == last)
        def _():
            f = acc_scr[...]
            f_ref[...] = f
            xo_ref[...] = x_ref[...] + (0.5 * cond_ref[2:3, :]) * f

    tok = pl.BlockSpec((tm, D_MODEL), lambda i, j: (i, 0))
    return pl.pallas_call(
        body, name="ffn_fwd", grid=(SEQ // tm, N_FF_CHUNK),
        out_shape=(jax.ShapeDtypeStruct((SEQ, D_MODEL), F32), jax.ShapeDtypeStruct((SEQ, D_MODEL), F32)),
        in_specs=[tok, _row_spec(3), _row_spec(1),
                  pl.BlockSpec((1, D_MODEL, FF_PIECE), lambda i, j: (j, 0, 0)),
                  pl.BlockSpec((1, D_MODEL, FF_PIECE), lambda i, j: (j + N_FF_CHUNK, 0, 0)),
                  pl.BlockSpec((1, FF_PIECE, D_MODEL), lambda i, j: (j, 0, 0))],
        out_specs=(tok, tok),
        scratch_shapes=[pltpu.VMEM((tm, D_MODEL), BF16), pltpu.VMEM((tm, D_MODEL), F32)],
        compiler_params=_params("arbitrary", "arbitrary"),
    )(x, cond3, g, w_in_g, w_in_g, w_out_g)


def _ffn_bwd(dy, x, f, cond3, g, w_in_g, w_out_g):
    tm = TOKEN_TILE
    last = N_FF_CHUNK - 1

    def body(dy_ref, x_ref, f_ref, cond_ref, g_ref, wa_ref, wb_ref, wo_ref,
             dx_ref, da_ref, db_ref, act_ref, h_ref, do_ref, part_ref, h_scr, do_scr, dh_scr):
        i, j = pl.program_id(0), pl.program_id(1)

        @pl.when(j == 0)
        def _():
            h, _, _ = _modnorm(x_ref[...], g_ref[...], cond_ref[0:1, :], cond_ref[1:2, :])
            hb = h.astype(BF16)
            h_scr[...] = hb
            h_ref[...] = hb
            do = ((0.5 * cond_ref[2:3, :]) * dy_ref[...]).astype(BF16)
            do_scr[...] = do
            do_ref[...] = do
            dh_scr[...] = jnp.zeros_like(dh_scr)

        @pl.when((i == 0) & (j == 0))
        def _():
            part_ref[...] = jnp.zeros_like(part_ref)

        h = h_scr[...]
        do = do_scr[...]
        a = _dot(h, wa_ref[0])
        b = _dot(h, wb_ref[0])
        dact = _dot_nt(do, wo_ref[0])
        sig = _sigmoid(a)
        s = a * sig
        da = (dact * b * (sig * (1.0 + a * (1.0 - sig)))).astype(BF16)
        db = (dact * s).astype(BF16)
        da_ref[0] = da
        db_ref[0] = db
        act_ref[0] = (s * b).astype(BF16)
        dh_scr[...] += _dot_nt(da, wa_ref[0]) + _dot_nt(db, wb_ref[0])

        @pl.when(j == last)
        def _():
            dyv = dy_ref[...]
            xhat, r = _rms(x_ref[...])
            dx, dshift, dscale, dg = _modnorm_bwd(xhat, r, g_ref[...], cond_ref[1:2, :], dh_scr[...])
            dx_ref[...] = dyv + dx
            part_ref[0:1, :] += dshift
            part_ref[1:2, :] += dscale
            part_ref[2:3, :] += _sum0(0.5 * dyv * f_ref[...])
            part_ref[3:4, :] += dg

    tok = pl.BlockSpec((tm, D_MODEL), lambda i, j: (i, 0))
    chunk = pl.BlockSpec((1, tm, FF_PIECE), lambda i, j: (j, i, 0))
    chunk_shape = jax.ShapeDtypeStruct((N_FF_CHUNK, SEQ, FF_PIECE), BF16)
    return pl.pallas_call(
        body, name="ffn_bwd", grid=(SEQ // tm, N_FF_CHUNK),
        out_shape=(jax.ShapeDtypeStruct((SEQ, D_MODEL), F32), chunk_shape, chunk_shape, chunk_shape,
                   jax.ShapeDtypeStruct((SEQ, D_MODEL), BF16), jax.ShapeDtypeStruct((SEQ, D_MODEL), BF16),
                   jax.ShapeDtypeStruct((SUBLANES, D_MODEL), F32)),
        in_specs=[tok, tok, tok, _row_spec(3), _row_spec(1),
                  pl.BlockSpec((1, D_MODEL, FF_PIECE), lambda i, j: (j, 0, 0)),
                  pl.BlockSpec((1, D_MODEL, FF_PIECE), lambda i, j: (j + N_FF_CHUNK, 0, 0)),
                  pl.BlockSpec((1, FF_PIECE, D_MODEL), lambda i, j: (j, 0, 0))],
        out_specs=(tok, chunk, chunk, chunk, tok, tok, _row_spec(SUBLANES)),
        scratch_shapes=[pltpu.VMEM((tm, D_MODEL), BF16), pltpu.VMEM((tm, D_MODEL), BF16),
                        pltpu.VMEM((tm, D_MODEL), F32)],
        compiler_params=_params("arbitrary", "arbitrary"),
    )(dy, x, f, cond3, g, w_in_g, w_in_g, w_out_g)


def _dw(lhs, rhs_a, rhs_b=None, name="dw"):
    pl_, s, m = lhs.shape
    pa, _, nn = rhs_a.shape
    pb = 0 if rhs_b is None else rhs_b.shape[0]
    p_out = max(pl_, pa + pb)
    tmm = 512 if m % 512 == 0 else m
    two = rhs_b is not None

    def body(*refs):
        if two:
            l_ref, a_ref, b_ref, o_ref = refs
        else:
            l_ref, a_ref, o_ref = refs
        p = pl.program_id(0)
        if two:
            @pl.when(p < pa)
            def _():
                o_ref[0] = _dot_tn(l_ref[0], a_ref[0]).astype(BF16)

            @pl.when(p >= pa)
            def _():
                o_ref[0] = _dot_tn(l_ref[0], b_ref[0]).astype(BF16)
        else:
            o_ref[0] = _dot_tn(l_ref[0], a_ref[0]).astype(BF16)

    in_specs = [pl.BlockSpec((1, s, tmm), (lambda p, q: (p, 0, q)) if pl_ > 1 else (lambda p, q: (0, 0, q)))]
    if two:
        in_specs.append(pl.BlockSpec((1, s, nn), lambda p, q: (jnp.minimum(p, pa - 1), 0, 0)))
        in_specs.append(pl.BlockSpec((1, s, nn), lambda p, q: (jnp.maximum(p - pa, 0), 0, 0)))
        args = (lhs, rhs_a, rhs_b)
    else:
        in_specs.append(pl.BlockSpec((1, s, nn), (lambda p, q: (p, 0, 0)) if pa > 1 else (lambda p, q: (0, 0, 0))))
        args = (lhs, rhs_a)
    return pl.pallas_call(
        body, name=name, grid=(p_out, m // tmm),
        out_shape=jax.ShapeDtypeStruct((p_out, m, nn), BF16),
        in_specs=in_specs, out_specs=pl.BlockSpec((1, tmm, nn), lambda p, q: (p, q, 0)),
        compiler_params=_params("arbitrary", "arbitrary"),
    )(*args)


def _mix_in_fwd(x, cond2, g, w):
    tm = TOKEN_TILE

    def body(x_ref, cond_ref, g_ref, w_ref, z_ref):
        h, _, _ = _modnorm(x_ref[...], g_ref[...], cond_ref[0:1, :], cond_ref[1:2, :])
        z_ref[...] = _dot(h.astype(BF16), w_ref[...])

    return pl.pallas_call(
        body, name="mix_in_fwd", grid=(SEQ // tm,),
        out_shape=jax.ShapeDtypeStruct((SEQ, P_IN), F32),
        in_specs=[pl.BlockSpec((tm, D_MODEL), lambda i: (i, 0)), _row_spec(2), _row_spec(1),
                  pl.BlockSpec((D_MODEL, P_IN), lambda i: (0, 0))],
        out_specs=pl.BlockSpec((tm, P_IN), lambda i: (i, 0)),
        compiler_params=_params("arbitrary"),
    )(x, cond2, g, w)


MIX_SLABS = ((0, 2 * W_GRP), (2 * W_GRP, 3 * W_GRP), (3 * W_GRP, 6 * W_GRP), (6 * W_GRP, 7 * W_GRP))


def _mix_in_bwd(dzs, x, dy, cond2, g, w):
    tm = TOKEN_TILE

    def body(dza_ref, dzb_ref, dzc_ref, dzd_ref, x_ref, dy_ref, cond_ref, g_ref, w_ref, dx_ref, h_ref, dzo_ref, part_ref):
        i = pl.program_id(0)

        @pl.when(i == 0)
        def _():
            part_ref[...] = jnp.zeros_like(part_ref)

        h, xhat, r = _modnorm(x_ref[...], g_ref[...], cond_ref[0:1, :], cond_ref[1:2, :])
        h_ref[...] = h.astype(BF16)
        dh = None
        for (lo, hi), d_ref in zip(MIX_SLABS, (dza_ref, dzb_ref, dzc_ref, dzd_ref)):
            dzb = d_ref[...].astype(BF16)
            dzo_ref[:, lo:hi] = dzb
            t = _dot_nt(dzb, w_ref[:, lo:hi])
            dh = t if dh is None else dh + t
        dx, dshift, dscale, dg = _modnorm_bwd(xhat, r, g_ref[...], cond_ref[1:2, :], dh)
        dx_ref[...] = dy_ref[...] + dx
        part_ref[0:1, :] += dshift
        part_ref[1:2, :] += dscale
        part_ref[2:3, :] += dg

    tok = pl.BlockSpec((tm, D_MODEL), lambda i: (i, 0))
    ztok = pl.BlockSpec((tm, P_IN), lambda i: (i, 0))
    slabs = [pl.BlockSpec((tm, hi - lo), lambda i: (i, 0)) for lo, hi in MIX_SLABS]
    return pl.pallas_call(
        body, name="mix_in_bwd", grid=(SEQ // tm,),
        out_shape=(jax.ShapeDtypeStruct((SEQ, D_MODEL), F32), jax.ShapeDtypeStruct((SEQ, D_MODEL), BF16),
                   jax.ShapeDtypeStruct((SEQ, P_IN), BF16), jax.ShapeDtypeStruct((SUBLANES, D_MODEL), F32)),
        in_specs=[*slabs, tok, tok, _row_spec(2), _row_spec(1), pl.BlockSpec((D_MODEL, P_IN), lambda i: (0, 0))],
        out_specs=(tok, tok, ztok, _row_spec(SUBLANES)),
        compiler_params=_params("arbitrary"),
    )(*dzs, x, dy, cond2, g, w)


def _group_norm(ys, g_ref):
    out = []
    for k, y in enumerate(ys):
        yhat, r = _rms(y)
        out.append((yhat, r, g_ref[:, k * W_GRP:(k + 1) * W_GRP]))
    return out


def _mix_out_fwd(x, ys, g, gate, w):
    tm = TOKEN_TILE

    def body(x_ref, ya_ref, yb_ref, yc_ref, yd_ref, g_ref, gate_ref, w_ref, xo_ref):
        acc = None
        for k, (yhat, _, gk) in enumerate(_group_norm([r[...] for r in (ya_ref, yb_ref, yc_ref, yd_ref)], g_ref)):
            t = _dot((yhat * gk).astype(BF16), w_ref[k * W_GRP:(k + 1) * W_GRP, :])
            acc = t if acc is None else acc + t
        xo_ref[...] = x_ref[...] + gate_ref[...] * acc

    tok = pl.BlockSpec((tm, D_MODEL), lambda i: (i, 0))
    ytok = pl.BlockSpec((tm, W_GRP), lambda i: (i, 0))
    return pl.pallas_call(
        body, name="mix_out_fwd", grid=(SEQ // tm,),
        out_shape=jax.ShapeDtypeStruct((SEQ, D_MODEL), F32),
        in_specs=[tok, ytok, ytok, ytok, ytok, _row_spec(1), _row_spec(1),
                  pl.BlockSpec((D_MODEL, D_MODEL), lambda i: (0, 0))],
        out_specs=tok, compiler_params=_params("arbitrary"),
    )(x, *ys, g, gate, w)


def _mix_out_bwd(dy, ys, g, gate, w):
    tm = TOKEN_TILE

    def body(dy_ref, ya_ref, yb_ref, yc_ref, yd_ref, g_ref, gate_ref, w_ref,
             da_ref, db_ref, dc_ref, dd_ref, yn_ref, dmo_ref, part_ref):
        i = pl.program_id(0)

        @pl.when(i == 0)
        def _():
            part_ref[...] = jnp.zeros_like(part_ref)

        dyv = dy_ref[...]
        dmo = (gate_ref[...] * dyv).astype(BF16)
        dmo_ref[...] = dmo
        dyn = _dot_nt(dmo, w_ref[...])
        norms = _group_norm([r[...] for r in (ya_ref, yb_ref, yc_ref, yd_ref)], g_ref)
        mo = None
        for k, ((yhat, r, gk), o_ref) in enumerate(zip(norms, (da_ref, db_ref, dc_ref, dd_ref))):
            sl = slice(k * W_GRP, (k + 1) * W_GRP)
            ynk = (yhat * gk).astype(BF16)
            yn_ref[:, sl] = ynk
            t = _dot(ynk, w_ref[sl, :])
            mo = t if mo is None else mo + t
            dk = dyn[:, sl]
            o_ref[...] = _rms_bwd(yhat, r, dk * gk)
            part_ref[1:2, sl] += _sum0(dk * yhat)
        part_ref[0:1, :] += _sum0(dyv * mo)

    tok = pl.BlockSpec((tm, D_MODEL), lambda i: (i, 0))
    ytok = pl.BlockSpec((tm, W_GRP), lambda i: (i, 0))
    ysh = jax.ShapeDtypeStruct((SEQ, W_GRP), F32)
    return pl.pallas_call(
        body, name="mix_out_bwd", grid=(SEQ // tm,),
        out_shape=(ysh, ysh, ysh, ysh, jax.ShapeDtypeStruct((SEQ, D_MODEL), BF16),
                   jax.ShapeDtypeStruct((SEQ, D_MODEL), BF16), jax.ShapeDtypeStruct((SUBLANES, D_MODEL), F32)),
        in_specs=[tok, ytok, ytok, ytok, ytok, _row_spec(1), _row_spec(1),
                  pl.BlockSpec((D_MODEL, D_MODEL), lambda i: (0, 0))],
        out_specs=(ytok, ytok, ytok, ytok, tok, tok, _row_spec(SUBLANES)),
        compiler_params=_params("arbitrary"),
    )(dy, *ys, g, gate, w)


def _shift_down(v, k, rows):
    return jnp.where(rows >= k, pltpu.roll(v, k, axis=0), 0.0)


def _shift_up(v, k, rows):
    n = v.shape[0]
    return jnp.where(rows < n - k, pltpu.roll(v, n - k, axis=0), 0.0)


def _zslab(width, index):
    return pl.BlockSpec((SEQ, width), lambda *_: (0, index))


def _full(shape):
    return pl.BlockSpec(shape, lambda *_: (0,) * len(shape))


def _head_avg():
    r = lax.broadcasted_iota(jnp.int32, (W_GRP, W_GRP), 0) // SGU_HEAD_DIM
    c = lax.broadcasted_iota(jnp.int32, (W_GRP, W_GRP), 1) // SGU_HEAD_DIM
    return jnp.where(r == c, 1.0 / SGU_HEAD_DIM, 0.0).astype(F32)


def _sgu_norm(za):
    z = _gelu(za)
    u, v = z[:, :W_GRP], z[:, W_GRP:]
    avg = _head_avg()
    vc = v - _dot_hi(v, avg)
    rstd = lax.rsqrt(_dot_hi(vc * vc, avg) + EPS)
    return u, vc * rstd, rstd


def _sgu_masked_w(w_ref):
    t = lax.broadcasted_iota(jnp.int32, (CHUNK, CHUNK), 0)
    s = lax.broadcasted_iota(jnp.int32, (CHUNK, CHUNK), 1)
    tril = t >= s
    return [jnp.where(tril, w_ref[:, h * CHUNK:(h + 1) * CHUNK], 0.0).astype(BF16) for h in range(SGU_HEADS)]


def _head_of_lane():
    return lax.broadcasted_iota(jnp.int32, (CHUNK, W_GRP), 1) // SGU_HEAD_DIM


def _sgu_fwd(z, w_cat, bias):
    def body(z_ref, w_ref, b_ref, y_ref, vn_scr, u_scr):
        u, vn, _ = _sgu_norm(z_ref[...])
        vn_scr[...] = vn.astype(BF16)
        u_scr[...] = u
        ws = _sgu_masked_w(w_ref)
        head = _head_of_lane()
        bias_v = b_ref[...]

        def chunk(n, carry):
            rows = pl.ds(pl.multiple_of(n * CHUNK, CHUNK), CHUNK)
            vb = vn_scr[rows, :]
            mixed = bias_v
            for h in range(SGU_HEADS):
                mixed = mixed + jnp.where(head == h, _dot(ws[h], vb), 0.0)
            y_ref[rows, :] = u_scr[rows, :] * mixed
            return carry

        lax.fori_loop(0, SEQ // CHUNK, chunk, 0)

    return pl.pallas_call(
        body, name="sgu_fwd", grid=(1,),
        out_shape=jax.ShapeDtypeStruct((SEQ, W_GRP), F32),
        in_specs=[_zslab(2 * W_GRP, 0), _full((CHUNK, SGU_HEADS * CHUNK)), _full((CHUNK, W_GRP))],
        out_specs=_full((SEQ, W_GRP)),
        scratch_shapes=[pltpu.VMEM((SEQ, W_GRP), BF16), pltpu.VMEM((SEQ, W_GRP), F32)],
        compiler_params=_params("arbitrary"),
    )(z, w_cat, bias)


def _sgu_bwd(z, dy, w_cat, bias):
    def body(z_ref, dy_ref, w_ref, b_ref, dz_ref, dw_ref, db_ref, vn_scr, u_scr, dvn_scr, du_scr):
        za = z_ref[...]
        u, vn, rstd = _sgu_norm(za)
        vn_scr[...] = vn.astype(BF16)
        u_scr[...] = u
        ws = _sgu_masked_w(w_ref)
        head = _head_of_lane()
        bias_v = b_ref[...]

        def chunk(n, carry):
            dws, dbias = carry
            rows = pl.ds(pl.multiple_of(n * CHUNK, CHUNK), CHUNK)
            vb = vn_scr[rows, :]
            mixed = bias_v
            for h in range(SGU_HEADS):
                mixed = mixed + jnp.where(head == h, _dot(ws[h], vb), 0.0)
            dyc = dy_ref[rows, :]
            du_scr[rows, :] = dyc * mixed
            dmixed = dyc * u_scr[rows, :]
            dvn = jnp.zeros((CHUNK, W_GRP), F32)
            new_dws = []
            for h in range(SGU_HEADS):
                dm = jnp.where(head == h, dmixed, 0.0).astype(BF16)
                new_dws.append(dws[h] + _dot_nt(dm, vb))
                dvn = dvn + _dot_tn(ws[h], dm)
            dvn_scr[rows, :] = dvn
            return tuple(new_dws), dbias + dmixed

        zero_w = tuple(jnp.zeros((CHUNK, CHUNK), F32) for _ in range(SGU_HEADS))
        dws, dbias = lax.fori_loop(0, SEQ // CHUNK, chunk, (zero_w, jnp.zeros((CHUNK, W_GRP), F32)))
        t = lax.broadcasted_iota(jnp.int32, (CHUNK, CHUNK), 0)
        s = lax.broadcasted_iota(jnp.int32, (CHUNK, CHUNK), 1)
        for h in range(SGU_HEADS):
            dw_ref[:, h * CHUNK:(h + 1) * CHUNK] = jnp.where(t >= s, dws[h], 0.0)
        avg = _head_avg()
        db_ref[...] = _dot_hi(dbias, avg) * float(SGU_HEAD_DIM)
        dvn = dvn_scr[...]
        dv = rstd * (dvn - _dot_hi(dvn, avg) - vn * _dot_hi(dvn * vn, avg))
        gg = _gelu_grad(za)
        dz_ref[:, :W_GRP] = gg[:, :W_GRP] * du_scr[...]
        dz_ref[:, W_GRP:] = gg[:, W_GRP:] * dv

    return pl.pallas_call(
        body, name="sgu_bwd", grid=(1,),
        out_shape=(jax.ShapeDtypeStruct((SEQ, 2 * W_GRP), F32), jax.ShapeDtypeStruct((CHUNK, SGU_HEADS * CHUNK), F32),
                   jax.ShapeDtypeStruct((CHUNK, W_GRP), F32)),
        in_specs=[_zslab(2 * W_GRP, 0), _full((SEQ, W_GRP)), _full((CHUNK, SGU_HEADS * CHUNK)), _full((CHUNK, W_GRP))],
        out_specs=(_full((SEQ, 2 * W_GRP)), _full((CHUNK, SGU_HEADS * CHUNK)), _full((CHUNK, W_GRP))),
        scratch_shapes=[pltpu.VMEM((SEQ, W_GRP), BF16), pltpu.VMEM((SEQ, W_GRP), F32),
                        pltpu.VMEM((SEQ, W_GRP), F32), pltpu.VMEM((SEQ, W_GRP), F32)],
        compiler_params=_params("arbitrary"),
    )(z, dy, w_cat, bias)


def _pool_window_of_lane(shape):
    grp = lax.broadcasted_iota(jnp.int32, shape, 1) // POOL_GROUP_DIM
    win = jnp.full(shape, POOL_WINDOWS[0], jnp.int32)
    for k in range(1, len(POOL_WINDOWS)):
        win = jnp.where(grp == k, POOL_WINDOWS[k], win)
    return grp, win


def _pool_select(levels, grp):
    out = levels[0]
    for k in range(1, len(levels)):
        out = jnp.where(grp == k, levels[k], out)
    return out


def _pool_p(z):
    shape = z.shape
    rows = lax.broadcasted_iota(jnp.int32, shape, 0)
    grp, win = _pool_window_of_lane(shape)
    levels, s, k = [], z, 1
    for _ in POOL_WINDOWS:
        s = s + _shift_down(s, k, rows)
        levels.append(s)
        k *= 2
    inv = 1.0 / jnp.minimum(rows + 1, win).astype(F32)
    return _pool_select(levels, grp) * inv - z, inv, rows, grp


def _pool_fwd(z, w_bd, scale):
    def body(z_ref, w_ref, s_ref, y_ref):
        p, _, _, _ = _pool_p(z_ref[...])
        y_ref[...] = _dot(p.astype(BF16), w_ref[...]) * s_ref[...]

    return pl.pallas_call(
        body, name="pool_fwd", grid=(1,),
        out_shape=jax.ShapeDtypeStruct((SEQ, W_GRP), F32),
        in_specs=[_zslab(W_GRP, 2), _full((W_GRP, W_GRP)), _full((1, W_GRP))],
        out_specs=_full((SEQ, W_GRP)), compiler_params=_params("arbitrary"),
    )(z, w_bd, scale)


def _pool_bwd(z, dy, w_bd, scale):
    def body(z_ref, dy_ref, w_ref, s_ref, dz_ref, dw_ref, ds_ref):
        p, inv, rows, grp = _pool_p(z_ref[...])
        pb = p.astype(BF16)
        dyv = dy_ref[...]
        ds_ref[...] = _sum0(dyv * _dot(pb, w_ref[...]))
        dpre = (dyv * s_ref[...]).astype(BF16)
        dw_ref[...] = _dot_tn(pb, dpre)
        dp = _dot_nt(dpre, w_ref[...])
        q = dp * inv
        levels, s, k = [], q, 1
        for _ in POOL_WINDOWS:
            s = s + _shift_up(s, k, rows)
            levels.append(s)
            k *= 2
        dz_ref[...] = _pool_select(levels, grp) - dp

    return pl.pallas_call(
        body, name="pool_bwd", grid=(1,),
        out_shape=(jax.ShapeDtypeStruct((SEQ, W_GRP), F32), jax.ShapeDtypeStruct((W_GRP, W_GRP), F32),
                   jax.ShapeDtypeStruct((1, W_GRP), F32)),
        in_specs=[_zslab(W_GRP, 2), _full((SEQ, W_GRP)), _full((W_GRP, W_GRP)), _full((1, W_GRP))],
        out_specs=(_full((SEQ, W_GRP)), _full((W_GRP, W_GRP)), _full((1, W_GRP))),
        compiler_params=_params("arbitrary"),
    )(z, dy, w_bd, scale)


def _conv_fwd(z, w):
    def body(z_ref, w_ref, y_ref):
        zc = z_ref[...]
        bg, cg, xh = zc[:, :W_GRP], zc[:, W_GRP:2 * W_GRP], zc[:, 2 * W_GRP:]
        rows = lax.broadcasted_iota(jnp.int32, (SEQ, W_GRP), 0)
        y = cg * xh
        conv = w_ref[0:1, :] * _shift_down(y, 2, rows) + w_ref[1:2, :] * _shift_down(y, 1, rows) + w_ref[2:3, :] * y
        y_ref[...] = bg * conv

    return pl.pallas_call(
        body, name="conv_fwd", grid=(1,),
        out_shape=jax.ShapeDtypeStruct((SEQ, W_GRP), F32),
        in_specs=[_zslab(3 * W_GRP, 1), _full((3, W_GRP))],
        out_specs=_full((SEQ, W_GRP)), compiler_params=_params("arbitrary"),
    )(z, w)


def _conv_bwd(z, dy, w):
    def body(z_ref, dy_ref, w_ref, dz_ref, dw_ref):
        zc = z_ref[...]
        bg, cg, xh = zc[:, :W_GRP], zc[:, W_GRP:2 * W_GRP], zc[:, 2 * W_GRP:]
        rows = lax.broadcasted_iota(jnp.int32, (SEQ, W_GRP), 0)
        y = cg * xh
        y2, y1 = _shift_down(y, 2, rows), _shift_down(y, 1, rows)
        conv = w_ref[0:1, :] * y2 + w_ref[1:2, :] * y1 + w_ref[2:3, :] * y
        dyv = dy_ref[...]
        dconv = dyv * bg
        dw_ref[...] = jnp.zeros_like(dw_ref)
        dw_ref[0:1, :] = _sum0(dconv * y2)
        dw_ref[1:2, :] = _sum0(dconv * y1)
        dw_ref[2:3, :] = _sum0(dconv * y)
        dyy = (w_ref[0:1, :] * _shift_up(dconv, 2, rows) + w_ref[1:2, :] * _shift_up(dconv, 1, rows)
               + w_ref[2:3, :] * dconv)
        dz_ref[:, :W_GRP] = dyv * conv
        dz_ref[:, W_GRP:2 * W_GRP] = dyy * xh
        dz_ref[:, 2 * W_GRP:] = dyy * cg

    return pl.pallas_call(
        body, name="conv_bwd", grid=(1,),
        out_shape=(jax.ShapeDtypeStruct((SEQ, 3 * W_GRP), F32), jax.ShapeDtypeStruct((SUBLANES, W_GRP), F32)),
        in_specs=[_zslab(3 * W_GRP, 1), _full((SEQ, W_GRP)), _full((3, W_GRP))],
        out_specs=(_full((SEQ, 3 * W_GRP)), _full((SUBLANES, W_GRP))),
        compiler_params=_params("arbitrary"),
    )(z, dy, w)


def _s5_disc(lre, lim, ldt, br, bi):
    dt = jnp.exp(ldt)
    mag = jnp.exp(lre * dt)
    ang = lim * dt
    a_re, a_im = mag * jnp.cos(ang), mag * jnp.sin(ang)
    nr, ni = a_re - 1.0, a_im
    den = lre * lre + lim * lim
    k_re = (nr * lre + ni * lim) / den
    k_im = (ni * lre - nr * lim) / den
    return a_re, a_im, k_re * br - k_im * bi, k_re * bi + k_im * br


def _s5_prep_fwd(lre, lim, ldt, br, bi):
    def body(lre_ref, lim_ref, ldt_ref, br_ref, bi_ref, ar_ref, ai_ref, bbr_ref, bbi_ref):
        ar, ai, bbr, bbi = _s5_disc(lre_ref[...], lim_ref[...], ldt_ref[...], br_ref[...], bi_ref[...])
        ar_ref[...] = ar
        ai_ref[...] = ai
        bbr_ref[...] = bbr
        bbi_ref[...] = bbi

    return pl.pallas_call(
        body, name="s5_prep_fwd",
        out_shape=(jax.ShapeDtypeStruct(lre.shape, F32), jax.ShapeDtypeStruct(lre.shape, F32),
                   jax.ShapeDtypeStruct(br.shape, F32), jax.ShapeDtypeStruct(br.shape, F32)),
        compiler_params=_params(),
    )(lre, lim, ldt, br, bi)


def _s5_prep_bwd(lre, lim, ldt, br, bi, dar, dai, dbbr, dbbi):
    def body(lre_ref, lim_ref, ldt_ref, br_ref, bi_ref, dar_ref, dai_ref, dbbr_ref, dbbi_ref,
             o_lre, o_lim, o_ldt, o_br, o_bi):
        _, pull = jax.vjp(_s5_disc, lre_ref[...], lim_ref[...], ldt_ref[...], br_ref[...], bi_ref[...])
        g = pull((dar_ref[...], dai_ref[...], dbbr_ref[...], dbbi_ref[...]))
        for o, v in zip((o_lre, o_lim, o_ldt, o_br, o_bi), g):
            o[...] = v

    return pl.pallas_call(
        body, name="s5_prep_bwd",
        out_shape=tuple(jax.ShapeDtypeStruct(a.shape, F32) for a in (lre, lim, ldt, br, bi)),
        compiler_params=_params(),
    )(lre, lim, ldt, br, bi, dar, dai, dbbr, dbbi)


def _cmul(ar, ai, br, bi):
    return ar * br - ai * bi, ar * bi + ai * br


def _s5_tile_consts(ar, ai, reverse):
    if reverse:
        ai = -ai
    shape = (SUBLANES, LANES)
    row = lax.broadcasted_iota(jnp.int32, shape, 0)
    a1 = (jnp.broadcast_to(ar, shape), jnp.broadcast_to(ai, shape))
    a2 = _cmul(*a1, *a1)
    a4 = _cmul(*a2, *a2)
    a8 = _cmul(*a4, *a4)
    steps = []
    for s, (pr, pi) in ((1, a1), (2, a2), (4, a4)):
        keep = (row < SUBLANES - s) if reverse else (row >= s)
        steps.append((s, jnp.where(keep, pr, 0.0), jnp.where(keep, pi, 0.0)))
    e = (SUBLANES - row) if reverse else (row + 1)
    pr, pi = jnp.ones(shape, F32), jnp.zeros(shape, F32)
    for bit, (qr, qi) in ((1, a1), (2, a2), (4, a4), (8, a8)):
        nr, ni = _cmul(pr, pi, qr, qi)
        hit = (e & bit) != 0
        pr, pi = jnp.where(hit, nr, pr), jnp.where(hit, ni, pi)
    return steps, pr, pi


def _s5_tile(xr, xi, steps, reverse):
    for s, pr, pi in steps:
        sh = SUBLANES - s if reverse else s
        sr, si = pltpu.roll(xr, sh, axis=0), pltpu.roll(xi, sh, axis=0)
        xr, xi = xr + pr * sr - pi * si, xi + pr * si + pi * sr
    return xr, xi


N_TILES = SEQ // SUBLANES


def _s5_fwd(z, b_re, b_im, c_re, c_im, a_re, a_im, d, glu_w, glu_b):
    nblk = S5_LANES // LANES

    def body(u_ref, br_ref, bi_ref, cr_ref, ci_ref, ar_ref, ai_ref, d_ref, gw_ref, gb_ref,
             y_ref, y0_ref, xr_ref, xi_ref, ub_scr, acc_scr):
        jb = pl.program_id(0)

        @pl.when(jb == 0)
        def _():
            ub_scr[...] = u_ref[...].astype(BF16)
            acc_scr[...] = jnp.zeros_like(acc_scr)

        ub = ub_scr[...]
        xr_ref[...] = _dot(ub, br_ref[...])
        xi_ref[...] = _dot(ub, bi_ref[...])
        steps, pr, pi = _s5_tile_consts(ar_ref[...], ai_ref[...], False)

        def tile(t, carry):
            cr, ci = carry
            rows = pl.ds(pl.multiple_of(t * SUBLANES, SUBLANES), SUBLANES)
            xr, xi = _s5_tile(xr_ref[rows, :], xi_ref[rows, :], steps, False)
            xr, xi = xr + pr * cr - pi * ci, xi + pr * ci + pi * cr
            xr_ref[rows, :] = xr
            xi_ref[rows, :] = xi
            return xr[SUBLANES - 1:, :], xi[SUBLANES - 1:, :]

        zero = jnp.zeros((1, LANES), F32)
        lax.fori_loop(0, N_TILES, tile, (zero, zero), unroll=4)
        acc_scr[...] += (_dot(xr_ref[...].astype(BF16), cr_ref[...]) - _dot(xi_ref[...].astype(BF16), ci_ref[...]))

        @pl.when(jb == nblk - 1)
        def _():
            y0 = acc_scr[...] + d_ref[...] * u_ref[...]
            y0_ref[...] = y0
            y1 = _gelu(y0)
            y_ref[...] = y1 * _sigmoid(_dot(y1.astype(BF16), gw_ref[...]) + gb_ref[...])

    lane_blk = pl.BlockSpec((SEQ, LANES), lambda j: (0, j))
    return pl.pallas_call(
        body, name="s5_fwd", grid=(nblk,),
        out_shape=(jax.ShapeDtypeStruct((SEQ, W_GRP), F32), jax.ShapeDtypeStruct((SEQ, W_GRP), F32),
                   jax.ShapeDtypeStruct((SEQ, S5_LANES), F32), jax.ShapeDtypeStruct((SEQ, S5_LANES), F32)),
        in_specs=[_zslab(W_GRP, 6),
                  pl.BlockSpec((W_GRP, LANES), lambda j: (0, j)), pl.BlockSpec((W_GRP, LANES), lambda j: (0, j)),
                  pl.BlockSpec((LANES, W_GRP), lambda j: (j, 0)), pl.BlockSpec((LANES, W_GRP), lambda j: (j, 0)),
                  pl.BlockSpec((1, LANES), lambda j: (0, j)), pl.BlockSpec((1, LANES), lambda j: (0, j)),
                  _full((1, W_GRP)), _full((W_GRP, W_GRP)), _full((1, W_GRP))],
        out_specs=(_full((SEQ, W_GRP)), _full((SEQ, W_GRP)), lane_blk, lane_blk),
        scratch_shapes=[pltpu.VMEM((SEQ, W_GRP), BF16), pltpu.VMEM((SEQ, W_GRP), F32)],
        compiler_params=_params("arbitrary"),
    )(z, b_re, b_im, c_re, c_im, a_re, a_im, d, glu_w, glu_b)


def _s5_bwd(z, y0, dy, xr, xi, b_re, b_im, c_re, c_im, a_re, a_im, d, glu_w, glu_b):
    nblk = S5_LANES // LANES

    def body(u_ref, y0_ref, dy_ref, xr_ref, xi_ref, br_ref, bi_ref, cr_ref, ci_ref, ar_ref, ai_ref,
             d_ref, gw_ref, gb_ref,
             du_ref, dbr_ref, dbi_ref, dcr_ref, dci_ref, dar_ref, dai_ref, dd_ref, dgw_ref, dgb_ref,
             ub_scr, dy0_scr, du_scr, lr_scr, li_scr):
        jb = pl.program_id(0)

        @pl.when(jb == 0)
        def _():
            u = u_ref[...]
            y0v = y0_ref[...]
            y1 = _gelu(y0v)
            y1b = y1.astype(BF16)
            sg = _sigmoid(_dot(y1b, gw_ref[...]) + gb_ref[...])
            dyv = dy_ref[...]
            dpre = dyv * y1 * sg * (1.0 - sg)
            dpb = dpre.astype(BF16)
            dgw_ref[...] = _dot_tn(y1b, dpb)
            dgb_ref[...] = _sum0(dpre)
            dy1 = dyv * sg + _dot_nt(dpb, gw_ref[...])
            dy0 = dy1 * _gelu_grad(y0v)
            dd_ref[...] = _sum0(dy0 * u)
            du_scr[...] = dy0 * d_ref[...]
            dy0_scr[...] = dy0.astype(BF16)
            ub_scr[...] = u.astype(BF16)

        dy0b = dy0_scr[...]
        lr_scr[...] = _dot_nt(dy0b, cr_ref[...])
        li_scr[...] = -_dot_nt(dy0b, ci_ref[...])
        dcr_ref[...] = _dot_tn(xr_ref[...].astype(BF16), dy0b)
        dci_ref[...] = -_dot_tn(xi_ref[...].astype(BF16), dy0b)
        steps, pr, pi = _s5_tile_consts(ar_ref[...], ai_ref[...], True)
        row = lax.broadcasted_iota(jnp.int32, (SUBLANES, LANES), 0)

        def tile(k, carry):
            cr, ci, accr, acci = carry
            t = N_TILES - 1 - k
            rows = pl.ds(pl.multiple_of(t * SUBLANES, SUBLANES), SUBLANES)
            lr, li = _s5_tile(lr_scr[rows, :], li_scr[rows, :], steps, True)
            lr, li = lr + pr * cr - pi * ci, li + pr * ci + pi * cr
            lr_scr[rows, :] = lr
            li_scr[rows, :] = li
            prev = pl.ds(pl.multiple_of(jnp.maximum(t - 1, 0) * SUBLANES, SUBLANES), SUBLANES)
            live = jnp.where(t > 0, 1.0, 0.0)
            xpr = jnp.where(row == 0, pltpu.roll(xr_ref[prev, :], 1, axis=0) * live, pltpu.roll(xr_ref[rows, :], 1, axis=0))
            xpi = jnp.where(row == 0, pltpu.roll(xi_ref[prev, :], 1, axis=0) * live, pltpu.roll(xi_ref[rows, :], 1, axis=0))
            accr = accr + lr * xpr + li * xpi
            acci = acci + li * xpr - lr * xpi
            return lr[0:1, :], li[0:1, :], accr, acci

        zero = jnp.zeros((1, LANES), F32)
        zt = jnp.zeros((SUBLANES, LANES), F32)
        _, _, accr, acci = lax.fori_loop(0, N_TILES, tile, (zero, zero, zt, zt), unroll=4)
        dar_ref[...] = jnp.zeros_like(dar_ref)
        dai_ref[...] = jnp.zeros_like(dai_ref)
        dar_ref[0:1, :] = _sum0(accr)
        dai_ref[0:1, :] = _sum0(acci)
        lrb, lib = lr_scr[...].astype(BF16), li_scr[...].astype(BF16)
        ub = ub_scr[...]
        dbr_ref[...] = _dot_tn(ub, lrb)
        dbi_ref[...] = _dot_tn(ub, lib)
        du_scr[...] += _dot_nt(lrb, br_ref[...]) + _dot_nt(lib, bi_ref[...])

        @pl.when(jb == nblk - 1)
        def _():
            du_ref[...] = du_scr[...]

    lane_blk = pl.BlockSpec((SEQ, LANES), lambda j: (0, j))
    bspec = pl.BlockSpec((W_GRP, LANES), lambda j: (0, j))
    cspec = pl.BlockSpec((LANES, W_GRP), lambda j: (j, 0))
    aspec = pl.BlockSpec((1, LANES), lambda j: (0, j))
    a8spec = pl.BlockSpec((SUBLANES, LANES), lambda j: (0, j))
    sd = jax.ShapeDtypeStruct
    return pl.pallas_call(
        body, name="s5_bwd", grid=(nblk,),
        out_shape=(sd((SEQ, W_GRP), F32), sd((W_GRP, S5_LANES), F32), sd((W_GRP, S5_LANES), F32),
                   sd((S5_LANES, W_GRP), F32), sd((S5_LANES, W_GRP), F32),
                   sd((SUBLANES, S5_LANES), F32), sd((SUBLANES, S5_LANES), F32),
                   sd((1, W_GRP), F32), sd((W_GRP, W_GRP), F32), sd((1, W_GRP), F32)),
        in_specs=[_zslab(W_GRP, 6), _full((SEQ, W_GRP)), _full((SEQ, W_GRP)), lane_blk, lane_blk,
                  bspec, bspec, cspec, cspec, aspec, aspec,
                  _full((1, W_GRP)), _full((W_GRP, W_GRP)), _full((1, W_GRP))],
        out_specs=(_full((SEQ, W_GRP)), bspec, bspec, cspec, cspec, a8spec, a8spec,
                   _full((1, W_GRP)), _full((W_GRP, W_GRP)), _full((1, W_GRP))),
        scratch_shapes=[pltpu.VMEM((SEQ, W_GRP), BF16), pltpu.VMEM((SEQ, W_GRP), BF16), pltpu.VMEM((SEQ, W_GRP), F32),
                        pltpu.VMEM((SEQ, LANES), F32), pltpu.VMEM((SEQ, LANES), F32)],
        compiler_params=_params("arbitrary"),
    )(z, y0, dy, xr, xi, b_re, b_im, c_re, c_im, a_re, a_im, d, glu_w, glu_b)


def _head(x, g, target):
    tm = TOKEN_TILE
    n = SEQ // tm

    def body(x_ref, g_ref, t_ref, dx_ref, st_ref, acc_scr):
        i = pl.program_id(0)

        @pl.when(i == 0)
        def _():
            acc_scr[...] = jnp.zeros_like(acc_scr)

        xhat, r = _rms(x_ref[...])
        gv = g_ref[...]
        err = xhat * gv - t_ref[...]
        dyv = err * (1.0 / D_MODEL)
        dx_ref[...] = _rms_bwd(xhat, r, dyv * gv)
        acc_scr[0:1, :] += _sum0(err * err)
        acc_scr[1:2, :] += _sum0(dyv * xhat)

        @pl.when(i == n - 1)
        def _():
            st_ref[...] = acc_scr[...]
            tot = jnp.sum(acc_scr[0:1, :], axis=-1, keepdims=True) * (0.5 / D_MODEL)
            st_ref[0:1, :] = jnp.broadcast_to(tot, (1, D_MODEL))

    tok = pl.BlockSpec((tm, D_MODEL), lambda i: (i, 0))
    return pl.pallas_call(
        body, name="head", grid=(n,),
        out_shape=(jax.ShapeDtypeStruct((SEQ, D_MODEL), F32), jax.ShapeDtypeStruct((SUBLANES, D_MODEL), F32)),
        in_specs=[tok, _row_spec(1), tok], out_specs=(tok, _row_spec(SUBLANES)),
        scratch_shapes=[pltpu.VMEM((SUBLANES, D_MODEL), F32)],
        compiler_params=_params("arbitrary"),
    )(x, g, target)


def _adamw(w, gparts, m, v, name):
    r, c = w.shape
    npart = gparts.shape[0]
    tr = r
    for cand in (512, 256, 128, 64, 32, 16):
        if r % cand == 0 and r > cand:
            tr = cand
            break
    b1c = 1.0 - ADAM_B1 ** ADAM_STEP
    b2c = 1.0 - ADAM_B2 ** ADAM_STEP

    def body(w_ref, g_ref, m_ref, v_ref, go_ref, d_ref, mo_ref, vo_ref):
        g = g_ref[0].astype(F32)
        for k in range(1, npart):
            g = g + g_ref[k].astype(F32)
        mn = ADAM_B1 * m_ref[...] + (1.0 - ADAM_B1) * g
        vn = ADAM_B2 * v_ref[...] + (1.0 - ADAM_B2) * (g * g)
        m_hat = mn / b1c
        v_hat = vn / b2c
        go_ref[...] = g
        d_ref[...] = -ADAM_LR * (m_hat / (jnp.sqrt(v_hat) + ADAM_EPS) + ADAM_WD * w_ref[...])
        mo_ref[...] = mn
        vo_ref[...] = vn

    blk = pl.BlockSpec((tr, c), lambda i: (i, 0))
    sh = jax.ShapeDtypeStruct((r, c), F32)
    return pl.pallas_call(
        body, name=name, grid=(r // tr,),
        out_shape=(sh, sh, sh, sh),
        in_specs=[blk, pl.BlockSpec((npart, tr, c), lambda i: (0, i, 0)), blk, blk],
        out_specs=(blk, blk, blk, blk), compiler_params=_params("arbitrary"),
    )(w, gparts, m, v)


def _adamw_layer(l, w, gparts, m, v, prev, name):
    _, r, c = w.shape
    npart = gparts.shape[0]
    tr = r
    for cand in (512, 256, 128, 64, 32, 16):
        if r % cand == 0 and r > cand:
            tr = cand
            break
    b1c = 1.0 - ADAM_B1 ** ADAM_STEP
    b2c = 1.0 - ADAM_B2 ** ADAM_STEP
    nprev = 0 if prev is None else 4

    def body(*refs):
        w_ref, g_ref, m_ref, v_ref = refs[:4]
        go_ref, d_ref, mo_ref, vo_ref = refs[4 + nprev:]
        g = g_ref[0].astype(F32)
        for k in range(1, npart):
            g = g + g_ref[k].astype(F32)
        mn = ADAM_B1 * m_ref[0] + (1.0 - ADAM_B1) * g
        vn = ADAM_B2 * v_ref[0] + (1.0 - ADAM_B2) * (g * g)
        go_ref[0] = g
        d_ref[0] = -ADAM_LR * ((mn / b1c) / (jnp.sqrt(vn / b2c) + ADAM_EPS) + ADAM_WD * w_ref[0])
        mo_ref[0] = mn
        vo_ref[0] = vn

    blk = pl.BlockSpec((1, tr, c), lambda i: (l, i, 0))
    sh = jax.ShapeDtypeStruct(w.shape, F32)
    keep = [pl.BlockSpec(memory_space=pl.ANY)] * nprev
    return pl.pallas_call(
        body, name=name, grid=(r // tr,),
        out_shape=(sh, sh, sh, sh),
        in_specs=[blk, pl.BlockSpec((npart, tr, c), lambda i: (0, i, 0)), blk, blk, *keep],
        out_specs=(blk, blk, blk, blk),
        input_output_aliases={4 + k: k for k in range(nprev)},
        compiler_params=_params("arbitrary"),
    )(w, gparts, m, v, *(prev or ()))


def _sum_parts(parts, name):
    n, r, c = parts.shape

    def body(p_ref, o_ref):
        acc = p_ref[0]
        for k in range(1, n):
            acc = acc + p_ref[k]
        o_ref[...] = acc

    return pl.pallas_call(
        body, name=name, out_shape=jax.ShapeDtypeStruct((r, c), F32), compiler_params=_params(),
    )(parts)


def _block_diag(blocks):
    g, a, b = blocks.shape
    eye = jnp.eye(g, dtype=blocks.dtype)
    return (blocks[:, :, None, :] * eye[:, None, :, None]).reshape(g * a, g * b)


def _diag_blocks(dense, g):
    a, b = dense.shape[0] // g, dense.shape[1] // g
    d4 = dense.reshape(g, a, g, b)
    eye = jnp.eye(g, dtype=dense.dtype)
    return jnp.sum(d4 * eye[:, None, :, None], axis=2)


def _pack(parts, cols):
    flat = jnp.concatenate([p.reshape(-1) for p in parts])
    unit = N_DEV * SUBLANES * cols
    total = -(-flat.shape[0] // unit) * unit
    flat = jnp.pad(flat, (0, total - flat.shape[0]))
    return flat.reshape(N_DEV, total // (N_DEV * cols), cols)


def _unpack(flat, shapes):
    out, pos = [], 0
    for s in shapes:
        n = math.prod(s)
        out.append(flat[pos:pos + n].reshape(s))
        pos += n
    return out


SMALL = ("norm1_g", "norm2_g", "sgu_w", "sgu_b", "pool_w", "pool_scale", "conv_w", "s5_lambda_re", "s5_lambda_im",
         "s5_b_re", "s5_b_im", "s5_c_re", "s5_c_im", "s5_d", "s5_log_dt", "s5_glu_w", "s5_glu_b", "mix_norm_g",
         "norm3_g", "final_norm_g")
BIG = ("ffn1_w_in", "ffn1_w_out", "w_mix_in", "w_mix_out", "ffn2_w_in", "ffn2_w_out")
WEIGHTS = ("ada_w", "ada_b", "norm1_g", "ffn1_w_in", "ffn1_w_out", "norm2_g", "w_mix_in", "sgu_w", "sgu_b", "pool_w",
           "pool_scale", "conv_w", "s5_lambda_re", "s5_lambda_im", "s5_b_re", "s5_b_im", "s5_c_re", "s5_c_im", "s5_d",
           "s5_log_dt", "s5_glu_w", "s5_glu_b", "mix_norm_g", "w_mix_out", "norm3_g", "ffn2_w_in", "ffn2_w_out",
           "final_norm_g")
PACK_COLS = 1024


def kernel(x, c, ada_w, ada_b, norm1_g, ffn1_w_in, ffn1_w_out, norm2_g, w_mix_in, sgu_w, sgu_b, pool_w, pool_scale, conv_w, s5_lambda_re, s5_lambda_im, s5_b_re, s5_b_im, s5_c_re, s5_c_im, s5_d, s5_log_dt, s5_glu_w, s5_glu_b, mix_norm_g, w_mix_out, norm3_g, ffn2_w_in, ffn2_w_out, final_norm_g, loss_target, m_ada_w, m_ada_b, m_norm1_g, m_ffn1_w_in, m_ffn1_w_out, m_norm2_g, m_w_mix_in, m_sgu_w, m_sgu_b, m_pool_w, m_pool_scale, m_conv_w, m_s5_lambda_re, m_s5_lambda_im, m_s5_b_re, m_s5_b_im, m_s5_c_re, m_s5_c_im, m_s5_d, m_s5_log_dt, m_s5_glu_w, m_s5_glu_b, m_mix_norm_g, m_w_mix_out, m_norm3_g, m_ffn2_w_in, m_ffn2_w_out, m_final_norm_g, v_ada_w, v_ada_b, v_norm1_g, v_ffn1_w_in, v_ffn1_w_out, v_norm2_g, v_w_mix_in, v_sgu_w, v_sgu_b, v_pool_w, v_pool_scale, v_conv_w, v_s5_lambda_re, v_s5_lambda_im, v_s5_b_re, v_s5_b_im, v_s5_c_re, v_s5_c_im, v_s5_d, v_s5_log_dt, v_s5_glu_w, v_s5_glu_b, v_mix_norm_g, v_w_mix_out, v_norm3_g, v_ffn2_w_in, v_ffn2_w_out, v_final_norm_g):
    args = dict(locals())
    W = {n: args[n] for n in WEIGHTS}
    M = {n: args["m_" + n] for n in WEIGHTS}
    V = {n: args["v_" + n] for n in WEIGHTS}
    me = _me()
    L = DEPTH
    x0 = x[0]
    target = loss_target[0]

    conv_cols = conv_w.shape[-1]
    glu_rows = s5_glu_w.shape[1]
    c_g, conv_g, glu_g = _exchange(
        [c.reshape(SUBLANES, LANES), conv_w.reshape(L * 3, conv_cols), s5_glu_w.reshape(L * glu_rows, W_GRP)],
        False, "gather_small")
    c_all = c_g.reshape(N_DEV, D_MODEL)
    conv_full = conv_g.reshape(N_DEV, L, 3, conv_cols).transpose(1, 2, 0, 3).reshape(L, 3, W_GRP)
    glu_full = glu_g.reshape(N_DEV, L, glu_rows, W_GRP).transpose(1, 0, 2, 3).reshape(L, W_GRP, W_GRP)

    ncol = ada_w.shape[-1]
    ada_b_mine = lax.dynamic_slice_in_dim(ada_b, me * ncol, ncol, axis=1).reshape(L, 1, ncol)
    cond_part = _cond_fwd(c_all, ada_w, ada_b_mine)
    (cond_g,) = _exchange([cond_part.reshape(L * N_DEV, ncol)], False, "gather_cond")
    cond_g = cond_g.reshape(N_DEV, L, N_DEV, ncol)
    cond_mine = lax.dynamic_index_in_dim(cond_g, me, axis=2, keepdims=False)
    cond = cond_mine.transpose(1, 0, 2).reshape(L, N_ADA, D_MODEL)

    lg = L * S5_GROUPS
    lre3 = s5_lambda_re.reshape(lg, S5_STATE, 1)
    lim3 = s5_lambda_im.reshape(lg, S5_STATE, 1)
    ldt3 = s5_log_dt.reshape(lg, 1, 1)
    br3 = s5_b_re.reshape(lg, S5_STATE, S5_GROUP_CH)
    bi3 = s5_b_im.reshape(lg, S5_STATE, S5_GROUP_CH)
    a_re3, a_im3, bb_re3, bb_im3 = _s5_prep_fwd(lre3, lim3, ldt3, br3, bi3)
    a_re = a_re3.reshape(L, 1, S5_LANES)
    a_im = a_im3.reshape(L, 1, S5_LANES)

    def b_mat(bb3, l):
        return _block_diag(bb3.reshape(L, S5_GROUPS, S5_STATE, S5_GROUP_CH)[l].transpose(0, 2, 1)).astype(BF16)

    def c_mat(cw, l):
        return _block_diag(cw[l].transpose(0, 2, 1)).astype(BF16)

    def gather_start(l, after):
        srcs = [W[n][l].astype(BF16) for n in BIG]
        lands = _place_own(srcs, False, "gather_weights_own")
        return _exchange_start(srcs, lands, after, False, "gather_weights_start")

    def gather_finish(handle, after):
        g = _exchange_wait(handle, after, False, "gather_weights_wait")
        out = dict(zip(BIG, g))
        out["ffn1_w_out"] = out["ffn1_w_out"].reshape(N_FF_CHUNK, FF_PIECE, D_MODEL)
        out["ffn2_w_out"] = out["ffn2_w_out"].reshape(N_FF_CHUNK, FF_PIECE, D_MODEL)
        out["w_mix_in"] = out["w_mix_in"].transpose(1, 0, 2).reshape(D_MODEL, P_IN)
        out["w_mix_out"] = out["w_mix_out"].reshape(D_MODEL, D_MODEL)
        return out

    def mixer_consts(l):
        w_cat = sgu_w[l].transpose(1, 0, 2).reshape(CHUNK, SGU_HEADS * CHUNK)
        bias = jnp.repeat(sgu_b[l].T, SGU_HEAD_DIM, axis=1)
        return dict(
            w_cat=w_cat, bias=bias, pool_bd=_block_diag(pool_w[l]).astype(BF16), pool_scale=pool_scale[l][None],
            conv=conv_full[l], b_re=b_mat(bb_re3, l), b_im=b_mat(bb_im3, l), c_re=c_mat(s5_c_re, l),
            c_im=c_mat(s5_c_im, l), a_re=a_re[l], a_im=a_im[l], d=s5_d[l][None], glu_w=glu_full[l].astype(BF16),
            glu_b=s5_glu_b[l][None])

    saved = []
    xc = x0
    handle, _ = gather_start(0, x0)
    wl = gather_finish(handle, x0)
    for l in range(L):
        cl = cond[l]
        if l + 1 < L:
            handle, token = gather_start(l + 1, wl["ffn1_w_in"])
            cl = cl + token[0, 0]
        mc = mixer_consts(l)
        x_a = xc
        x_b, f1 = _ffn_fwd(x_a, cl[0:3], norm1_g[l][None], wl["ffn1_w_in"], wl["ffn1_w_out"])
        z = _mix_in_fwd(x_b, cl[3:5], norm2_g[l][None], wl["w_mix_in"])
        ya = _sgu_fwd(z, mc["w_cat"], mc["bias"])
        yb = _pool_fwd(z, mc["pool_bd"], mc["pool_scale"])
        yc = _conv_fwd(z, mc["conv"])
        yd, y0, sxr, sxi = _s5_fwd(z, mc["b_re"], mc["b_im"], mc["c_re"], mc["c_im"], mc["a_re"], mc["a_im"],
                                   mc["d"], mc["glu_w"], mc["glu_b"])
        ys = (ya, yb, yc, yd)
        x_c = _mix_out_fwd(x_b, ys, mix_norm_g[l][None], cl[5:6], wl["w_mix_out"])
        x_d, f2 = _ffn_fwd(x_c, cl[6:9], norm3_g[l][None], wl["ffn2_w_in"], wl["ffn2_w_out"])
        saved.append(dict(wl=wl, mc=mc, x_a=x_a, x_b=x_b, x_c=x_c, f1=f1, f2=f2, z=z, ys=ys, y0=y0, xr=sxr, xi=sxi))
        xc = x_d
        if l + 1 < L:
            wl = gather_finish(handle, x_d)

    dx, stats = _head(xc, final_norm_g[None], target)
    loss = lax.psum(stats[0, 0], MESH_AXES)

    small_grads = {n: [None] * L for n in SMALL if n != "final_norm_g"}
    small_grads["final_norm_g"] = stats[1]
    big_out = {n: None for n in BIG}
    pending = None

    def finish_scatter(pend, after):
        layer, hnd = pend
        recv = _exchange_wait(hnd, after, True, "scatter_grads_wait")
        for n, r in zip(BIG, recv):
            big_out[n] = _adamw_layer(layer, W[n], r, M[n], V[n], big_out[n], "adamw_" + n)
        return recv[0]

    dcond_rows = [None] * L
    d_are, d_aim, d_bbre, d_bbim = [None] * L, [None] * L, [None] * L, [None] * L
    token = None
    for l in reversed(range(L)):
        sv = saved[l]
        wl, mc, cl = sv["wl"], sv["mc"], cond[l]
        if token is not None:
            cl = cl + token[0, 0]
        dx, da, db, act, hb, dob, part3 = _ffn_bwd(dx, sv["x_c"], sv["f2"], cl[6:9], norm3_g[l][None],
                                                   wl["ffn2_w_in"], wl["ffn2_w_out"])
        g_ffn2_in = _dw(hb[None], da, db, name="dw_ffn_in")
        g_ffn2_out = _dw(act, dob[None], name="dw_ffn_out").reshape(N_DEV, D_FF // N_DEV, D_MODEL)
        dya, dyb, dyc, dyd, ynb, dmob, part_mo = _mix_out_bwd(dx, sv["ys"], mix_norm_g[l][None], cl[5:6],
                                                              wl["w_mix_out"])
        g_mix_out = _dw(ynb[None], dmob[None], name="dw_mix_out").reshape(N_DEV, D_MODEL // N_DEV, D_MODEL)
        z = sv["z"]
        dza, dw_cat, dbias = _sgu_bwd(z, dya, mc["w_cat"], mc["bias"])
        dzb, dpool_dense, dpool_scale = _pool_bwd(z, dyb, mc["pool_bd"], mc["pool_scale"])
        dzc, dconv8 = _conv_bwd(z, dyc, mc["conv"])
        (dzd, dbre_d, dbim_d, dcre_d, dcim_d, dar8, dai8, dd, dglu_w, dglu_b) = _s5_bwd(
            z, sv["y0"], dyd, sv["xr"], sv["xi"], mc["b_re"], mc["b_im"], mc["c_re"], mc["c_im"],
            mc["a_re"], mc["a_im"], mc["d"], mc["glu_w"], mc["glu_b"])
        dx, h2b, dzbf, part2 = _mix_in_bwd((dza, dzb, dzc, dzd), sv["x_b"], dx, cl[3:5], norm2_g[l][None],
                                           wl["w_mix_in"])
        g_mix_in = _dw(h2b[None], dzbf[None], name="dw_mix_in")[0]
        g_mix_in = g_mix_in.reshape(D_MODEL, N_DEV, P_IN // N_DEV).transpose(1, 0, 2)
        dx, da, db, act, hb, dob, part1 = _ffn_bwd(dx, sv["x_a"], sv["f1"], cl[0:3], norm1_g[l][None],
                                                   wl["ffn1_w_in"], wl["ffn1_w_out"])
        g_ffn1_in = _dw(hb[None], da, db, name="dw_ffn_in")
        g_ffn1_out = _dw(act, dob[None], name="dw_ffn_out").reshape(N_DEV, D_FF // N_DEV, D_MODEL)
        pieces = [g_ffn1_in, g_ffn1_out, g_mix_in, g_mix_out, g_ffn2_in, g_ffn2_out]
        last = finish_scatter(pending, g_ffn1_in) if pending is not None else g_ffn1_in
        lands = _place_own(pieces, True, "scatter_grads_own")
        handle, token = _exchange_start(pieces, lands, last, True, "scatter_grads_start")
        pending = (l, handle)
        dcond_rows[l] = jnp.concatenate([part1[0:3], part2[0:2], part_mo[0:1], part3[0:3]], axis=0)
        sg = small_grads
        sg["norm1_g"][l] = part1[3]
        sg["norm2_g"][l] = part2[2]
        sg["norm3_g"][l] = part3[3]
        sg["mix_norm_g"][l] = part_mo[1]
        sg["sgu_w"][l] = dw_cat.reshape(CHUNK, SGU_HEADS, CHUNK).transpose(1, 0, 2)
        sg["sgu_b"][l] = dbias[:, ::SGU_HEAD_DIM].T
        sg["pool_w"][l] = _diag_blocks(dpool_dense, len(POOL_WINDOWS))
        sg["pool_scale"][l] = dpool_scale[0]
        sg["conv_w"][l] = dconv8[0:3]
        sg["s5_c_re"][l] = _diag_blocks(dcre_d, S5_GROUPS).transpose(0, 2, 1)
        sg["s5_c_im"][l] = _diag_blocks(dcim_d, S5_GROUPS).transpose(0, 2, 1)
        sg["s5_d"][l] = dd[0]
        sg["s5_glu_w"][l] = dglu_w
        sg["s5_glu_b"][l] = dglu_b[0]
        d_are[l], d_aim[l] = dar8[0], dai8[0]
        d_bbre[l] = _diag_blocks(dbre_d, S5_GROUPS).transpose(0, 2, 1)
        d_bbim[l] = _diag_blocks(dbim_d, S5_GROUPS).transpose(0, 2, 1)
    grad_x = dx

    g_lre, g_lim, g_ldt, g_br, g_bi = _s5_prep_bwd(
        lre3, lim3, ldt3, br3, bi3,
        jnp.stack(d_are).reshape(lg, S5_STATE, 1), jnp.stack(d_aim).reshape(lg, S5_STATE, 1),
        jnp.stack(d_bbre).reshape(lg, S5_STATE, S5_GROUP_CH), jnp.stack(d_bbim).reshape(lg, S5_STATE, S5_GROUP_CH))
    small = {n: (jnp.stack(v) if isinstance(v, list) and v[0] is not None else v) for n, v in small_grads.items()}
    small["s5_lambda_re"] = g_lre.reshape(s5_lambda_re.shape)
    small["s5_lambda_im"] = g_lim.reshape(s5_lambda_im.shape)
    small["s5_log_dt"] = g_ldt.reshape(s5_log_dt.shape)
    small["s5_b_re"] = g_br.reshape(s5_b_re.shape)
    small["s5_b_im"] = g_bi.reshape(s5_b_im.shape)

    small_shapes = [(L, 3, W_GRP) if n == "conv_w" else (L, W_GRP, W_GRP) if n == "s5_glu_w" else W[n].shape
                    for n in SMALL]
    packed = _pack([small[n].reshape(s) for n, s in zip(SMALL, small_shapes)], PACK_COLS)
    (pieces,) = _exchange([packed], True, "scatter_small")
    mine = _sum_parts(pieces, "sum_small")
    (summed,) = _exchange([mine], False, "gather_small_sums")
    small_sum = dict(zip(SMALL, _unpack(summed.reshape(-1), small_shapes)))
    small_sum["conv_w"] = lax.dynamic_slice_in_dim(small_sum["conv_w"], me * conv_cols, conv_cols, axis=2)
    small_sum["s5_glu_w"] = lax.dynamic_slice_in_dim(small_sum["s5_glu_w"], me * glu_rows, glu_rows, axis=1)

    dcond = jnp.stack(dcond_rows).reshape(L * N_ADA, D_MODEL)
    (dcond_g,) = _exchange([dcond], False, "gather_dcond")
    dcond_all = dcond_g.reshape(N_DEV, L, N_ADA * D_MODEL).transpose(1, 0, 2)
    dcond_mine = lax.dynamic_slice_in_dim(dcond_all, me * ncol, ncol, axis=2)
    g_ada_w, g_ada_b = _cond_bwd(c_all.T, dcond_mine, dcond_all)

    grads, deltas, new_m, new_v = {}, {}, {}, {}
    out = _adamw(ada_w.reshape(L * D_MODEL, ncol), g_ada_w.reshape(1, L * D_MODEL, ncol),
                 m_ada_w.reshape(L * D_MODEL, ncol), v_ada_w.reshape(L * D_MODEL, ncol), "adamw_ada_w")
    grads["ada_w"], deltas["ada_w"], new_m["ada_w"], new_v["ada_w"] = (o.reshape(ada_w.shape) for o in out)
    small_names = SMALL + ("ada_b",)
    small_g = dict(small_sum)
    small_g["ada_b"] = g_ada_b.reshape(ada_b.shape)
    shapes = [W[n].shape for n in small_names]
    pw = _pack([W[n] for n in small_names], PACK_COLS)
    rows = pw.shape[0] * pw.shape[1]
    out = _adamw(pw.reshape(rows, PACK_COLS),
                 _pack([small_g[n] for n in small_names], PACK_COLS).reshape(1, rows, PACK_COLS),
                 _pack([M[n] for n in small_names], PACK_COLS).reshape(rows, PACK_COLS),
                 _pack([V[n] for n in small_names], PACK_COLS).reshape(rows, PACK_COLS), "adamw_small")
    for store, o in zip((grads, deltas, new_m, new_v), out):
        store.update(zip(small_names, _unpack(o.reshape(-1), shapes)))
    finish_scatter(pending, out[0])
    for n in BIG:
        grads[n], deltas[n], new_m[n], new_v[n] = big_out[n]

    return (loss, grad_x[None], *[grads[n] for n in WEIGHTS], *[deltas[n] for n in WEIGHTS],
            *[new_m[n] for n in WEIGHTS], *[new_v[n] for n in WEIGHTS])
```

```python
import functools
import math

import jax
import jax.numpy as jnp
from jax import lax
from jax.experimental import pallas as pl
from jax.experimental.pallas import tpu as pltpu

F32 = jnp.float32
BF16 = jnp.bfloat16

D_MODEL = 1024
SEQ = 2048
DEPTH = 4
N_DEV = 8
W_GRP = 256
CHUNK = 128
SGU_HEADS = 4
SGU_HEAD_DIM = 64
POOL_WINDOWS = (2, 4, 8, 16)
POOL_GROUP_DIM = 64
S5_GROUPS = 16
S5_GROUP_CH = 16
S5_STATE = 64
S5_LANES = S5_GROUPS * S5_STATE
P_IN = 1792
D_FF = 2816
FF_PIECE = 2 * D_FF // N_DEV
N_FF_CHUNK = D_FF // FF_PIECE
N_ADA = 9
EPS = 1e-6
ADAM_LR = 0.001
ADAM_B1 = 0.9
ADAM_B2 = 0.999
ADAM_EPS = 1e-08
ADAM_WD = 0.01
ADAM_STEP = 10

SUBLANES = 8
LANES = 128
VMEM_LIMIT = 56 * 1024 * 1024
TOKEN_TILE = 512
HIGHEST = lax.Precision.HIGHEST
MESH_AXES = ("x", "y", "c")

_GELU_C = math.sqrt(2.0 / math.pi)
_GELU_A = 0.044715


def _params(*sem):
    return pltpu.CompilerParams(dimension_semantics=tuple(sem) if sem else None, vmem_limit_bytes=VMEM_LIMIT)


def _dot(a, b):
    return jnp.dot(a, b, preferred_element_type=F32)


def _dot_nt(a, b):
    return lax.dot_general(a, b, (((1,), (1,)), ((), ())), preferred_element_type=F32)


def _dot_tn(a, b):
    return lax.dot_general(a, b, (((0,), (0,)), ((), ())), preferred_element_type=F32)


def _dot_hi(a, b):
    return jnp.dot(a, b, preferred_element_type=F32, precision=HIGHEST)


def _sigmoid(x):
    return 1.0 / (1.0 + jnp.exp(-x))


def _gelu(x):
    return 0.5 * x * (1.0 + jnp.tanh(_GELU_C * (x + _GELU_A * x * x * x)))


def _gelu_grad(x):
    t = jnp.tanh(_GELU_C * (x + _GELU_A * x * x * x))
    return 0.5 * (1.0 + t) + 0.5 * x * (1.0 - t * t) * (_GELU_C * (1.0 + 3.0 * _GELU_A * x * x))


def _rms(x):
    r = lax.rsqrt(jnp.mean(x * x, axis=-1, keepdims=True) + EPS)
    return x * r, r


def _rms_bwd(xhat, r, dxhat):
    return r * (dxhat - xhat * jnp.mean(dxhat * xhat, axis=-1, keepdims=True))


def _sum0(x):
    return jnp.sum(x, axis=0, keepdims=True)


def _me():
    return 4 * lax.axis_index("x") + 2 * lax.axis_index("y") + lax.axis_index("c")


def _exchange(srcs, scatter, name):
    n = len(srcs)
    out_shapes = []
    for s in srcs:
        piece = s.shape[1:] if scatter else s.shape
        out_shapes.append(jax.ShapeDtypeStruct((N_DEV,) + tuple(piece), s.dtype))

    def body(*refs):
        ins, outs = refs[:n], refs[n:2 * n]
        send_sems, recv_sems, local_sems = refs[2 * n:]
        x, y, c = lax.axis_index("x"), lax.axis_index("y"), lax.axis_index("c")
        me = 4 * x + 2 * y + c

        def src_of(i, dev):
            return ins[i].at[dev] if scatter else ins[i]

        local = [pltpu.make_async_copy(src_of(i, me), outs[i].at[me], local_sems.at[i]) for i in range(n)]
        for cp in local:
            cp.start()
        sends, recvs = [], []
        for k in range(1, N_DEV):
            px = 1 - x if (k >> 2) & 1 else x
            py = 1 - y if (k >> 1) & 1 else y
            pc = 1 - c if k & 1 else c
            peer = 4 * px + 2 * py + pc
            for i in range(n):
                sends.append(pltpu.make_async_remote_copy(
                    src_ref=src_of(i, peer), dst_ref=outs[i].at[me],
                    send_sem=send_sems.at[k - 1, i], recv_sem=recv_sems.at[k - 1, i],
                    device_id=(px, py, pc), device_id_type=pl.DeviceIdType.MESH))
                recvs.append(pltpu.make_async_remote_copy(
                    src_ref=src_of(i, peer), dst_ref=outs[i].at[peer],
                    send_sem=send_sems.at[k - 1, i], recv_sem=recv_sems.at[k - 1, i],
                    device_id=(px, py, pc), device_id_type=pl.DeviceIdType.MESH))
        for cp in sends:
            cp.start()
        for cp in recvs:
            cp.wait_recv()
        for cp in sends:
            cp.wait_send()
        for cp in local:
            cp.wait()

    hbm = pl.BlockSpec(memory_space=pltpu.HBM)
    return pl.pallas_call(
        body, name=name, out_shape=out_shapes,
        in_specs=[hbm] * n, out_specs=[hbm] * n,
        scratch_shapes=[pltpu.SemaphoreType.DMA((N_DEV - 1, n)), pltpu.SemaphoreType.DMA((N_DEV - 1, n)),
                        pltpu.SemaphoreType.DMA((n,))],
    )(*srcs)


ALL_PEERS = tuple(range(1, N_DEV))
OTHER_CHIPS = (2, 4, 6)


def _peers(which):
    x, y, c = lax.axis_index("x"), lax.axis_index("y"), lax.axis_index("c")
    out = []
    for k in which:
        px = 1 - x if (k >> 2) & 1 else x
        py = 1 - y if (k >> 1) & 1 else y
        pc = 1 - c if k & 1 else c
        out.append(((px, py, pc), 4 * px + 2 * py + pc))
    return out


def _split_copies(ins, lands, send_sems, recv_sems, scatter, with_recvs, which):
    me = _me()
    sends, recvs = [], []
    for j, (dev, peer) in enumerate(_peers(which)):
        for i in range(len(ins)):
            src = ins[i].at[peer] if scatter else ins[i]
            slot = j * len(ins) + i
            sems = dict(send_sem=send_sems.at[slot], recv_sem=recv_sems.at[slot],
                        device_id=dev, device_id_type=pl.DeviceIdType.MESH)
            sends.append(pltpu.make_async_remote_copy(src_ref=src, dst_ref=lands[i].at[me], **sems))
            if with_recvs:
                recvs.append(pltpu.make_async_remote_copy(src_ref=src, dst_ref=lands[i].at[peer], **sems))
    return sends, recvs


_HBM = pl.BlockSpec(memory_space=pltpu.HBM)
_SEM = pl.BlockSpec(memory_space=pltpu.SEMAPHORE)
_EFFECT = pltpu.SideEffectType.DATAFLOW_SIDE_EFFECTING


def _place_own(srcs, scatter, name):
    n = len(srcs)
    halves = 2
    out_shapes, in_specs, out_specs = [], [], []
    for s in srcs:
        r, c = s.shape[-2:]
        out_shapes.append(jax.ShapeDtypeStruct((N_DEV, r, c), s.dtype))
        if scatter:
            in_specs.append(pl.BlockSpec((1, r // halves, c), lambda i, me: (me[0], i, 0)))
        else:
            in_specs.append(pl.BlockSpec((r // halves, c), lambda i, me: (i, 0)))
        out_specs.append(pl.BlockSpec((1, r // halves, c), lambda i, me: (me[0], i, 0)))

    def body(me_ref, *refs):
        for i in range(n):
            refs[n + i][0] = refs[i][0] if scatter else refs[i][...]

    return pl.pallas_call(
        body, name=name, out_shape=out_shapes,
        grid_spec=pltpu.PrefetchScalarGridSpec(num_scalar_prefetch=1, grid=(halves,), in_specs=in_specs,
                                               out_specs=out_specs),
        compiler_params=_params("arbitrary"),
    )(_me().reshape(1).astype(jnp.int32), *srcs)


def _exchange_start(srcs, lands, after, scatter, name, which=ALL_PEERS):
    n = len(srcs)

    def body(*refs):
        ins, land_in = refs[:n], refs[n:2 * n]
        send_sems, recv_sems = refs[2 * n + 1], refs[2 * n + 2]
        token = refs[-1]
        sends, _ = _split_copies(ins, land_in, send_sems, recv_sems, scatter, False, which)
        for cp in sends:
            cp.start()
        token[...] = jnp.zeros_like(token)

    sem = pltpu.SemaphoreType.DMA((len(which) * n,))
    out = pl.pallas_call(
        body, name=name,
        out_shape=(sem, sem, *[pltpu.HBM(s.shape, s.dtype) for s in srcs], *[pltpu.HBM(s.shape, s.dtype) for s in lands],
                   jax.ShapeDtypeStruct((SUBLANES, LANES), F32)),
        in_specs=[_HBM] * (2 * n) + [pl.BlockSpec(memory_space=pl.ANY)],
        out_specs=(_SEM, _SEM, *[_HBM] * (2 * n), pl.BlockSpec(memory_space=pltpu.VMEM)),
        input_output_aliases={i: 2 + i for i in range(2 * n)},
        compiler_params=pltpu.CompilerParams(has_side_effects=_EFFECT),
    )(*srcs, *lands, after)
    return (out[0], out[1], out[2:2 + n], out[2 + n:2 + 2 * n]), out[-1]


def _exchange_wait(handle, after, scatter, name, which=ALL_PEERS):
    send_sems, recv_sems, srcs, lands = handle
    n = len(srcs)

    def body(*refs):
        ins, land_in = refs[:n], refs[n:2 * n]
        sends, recvs = _split_copies(ins, land_in, refs[2 * n], refs[2 * n + 1], scatter, True, which)
        for cp in sends:
            cp.wait_send()
        for cp in recvs:
            cp.wait_recv()

    out = pl.pallas_call(
        body, name=name,
        out_shape=(*[pltpu.HBM(s.shape, s.dtype) for s in srcs], *[pltpu.HBM(s.shape, s.dtype) for s in lands]),
        in_specs=[_HBM] * (2 * n) + [_SEM, _SEM, pl.BlockSpec(memory_space=pl.ANY)],
        out_specs=tuple([_HBM] * (2 * n)),
        input_output_aliases={i: i for i in range(2 * n)},
        compiler_params=pltpu.CompilerParams(has_side_effects=_EFFECT),
    )(*srcs, *lands, send_sems, recv_sems, after)
    return out[n:]


def _gather_sibling(lands, name):
    n = len(lands)
    chips = ((0, 0), (0, 1), (1, 0), (1, 1))

    def body(*refs):
        ins, outs = refs[:n], refs[n:2 * n]
        send_sems, recv_sems = refs[2 * n], refs[2 * n + 1]
        x, y, c = lax.axis_index("x"), lax.axis_index("y"), lax.axis_index("c")
        sends, recvs = [], []
        for j, (bx, by) in enumerate(chips):
            chip = 4 * (1 - x if bx else x) + 2 * (1 - y if by else y)
            for i in range(n):
                sems = dict(send_sem=send_sems.at[j * n + i], recv_sem=recv_sems.at[j * n + i],
                            device_id=(x, y, 1 - c), device_id_type=pl.DeviceIdType.MESH)
                sends.append(pltpu.make_async_remote_copy(src_ref=ins[i].at[chip + c], dst_ref=outs[i].at[chip + c], **sems))
                recvs.append(pltpu.make_async_remote_copy(src_ref=ins[i].at[chip + c], dst_ref=outs[i].at[chip + 1 - c],
                                                          **sems))
        for cp in sends:
            cp.start()
        for cp in recvs:
            cp.wait_recv()
        for cp in sends:
            cp.wait_send()

    return pl.pallas_call(
        body, name=name, out_shape=[jax.ShapeDtypeStruct(a.shape, a.dtype) for a in lands],
        in_specs=[_HBM] * n, out_specs=[_HBM] * n,
        scratch_shapes=[pltpu.SemaphoreType.DMA((len(chips) * n,)), pltpu.SemaphoreType.DMA((len(chips) * n,))],
        input_output_aliases={i: i for i in range(n)},
    )(*lands)


def _cond_fwd(c_all, ada_w, ada_b_mine):
    ncol = ada_w.shape[-1]

    def body(c_ref, w_ref, b_ref, o_ref):
        c = c_ref[...]
        ca = (c * _sigmoid(c)).astype(BF16)
        o_ref[0] = _dot(ca, w_ref[0].astype(BF16)) + b_ref[0]

    return pl.pallas_call(
        body, name="cond_fwd", grid=(DEPTH,),
        out_shape=jax.ShapeDtypeStruct((DEPTH, N_DEV, ncol), F32),
        in_specs=[pl.BlockSpec((N_DEV, D_MODEL), lambda l: (0, 0)),
                  pl.BlockSpec((1, D_MODEL, ncol), lambda l: (l, 0, 0)),
                  pl.BlockSpec((1, 1, ncol), lambda l: (l, 0, 0))],
        out_specs=pl.BlockSpec((1, N_DEV, ncol), lambda l: (l, 0, 0)),
        compiler_params=_params("arbitrary"),
    )(c_all, ada_w, ada_b_mine)


def _cond_bwd(c_all_t, dcond_mine, dcond_all):
    ncol = dcond_mine.shape[-1]
    nall = dcond_all.shape[-1]

    def body(ct_ref, d_ref, da_ref, gw_ref, gb_ref):
        ct = ct_ref[...]
        ct = ct * _sigmoid(ct)
        d = d_ref[0]
        acc = ct[:, 0:1] * d[0:1, :]
        for b in range(1, N_DEV):
            acc = acc + ct[:, b:b + 1] * d[b:b + 1, :]
        gw_ref[0] = acc
        gb_ref[0] = _sum0(da_ref[0])

    return pl.pallas_call(
        body, name="cond_bwd", grid=(DEPTH,),
        out_shape=(jax.ShapeDtypeStruct((DEPTH, D_MODEL, ncol), F32), jax.ShapeDtypeStruct((DEPTH, 1, nall), F32)),
        in_specs=[pl.BlockSpec((D_MODEL, N_DEV), lambda l: (0, 0)),
                  pl.BlockSpec((1, N_DEV, ncol), lambda l: (l, 0, 0)),
                  pl.BlockSpec((1, N_DEV, nall), lambda l: (l, 0, 0))],
        out_specs=(pl.BlockSpec((1, D_MODEL, ncol), lambda l: (l, 0, 0)),
                   pl.BlockSpec((1, 1, nall), lambda l: (l, 0, 0))),
        compiler_params=_params("arbitrary"),
    )(c_all_t, dcond_mine, dcond_all)


def _modnorm(x, g, shift, scale):
    xhat, r = _rms(x)
    return (xhat * g) * (1.0 + scale) + shift, xhat, r


def _modnorm_bwd(xhat, r, g, scale, dh):
    n = xhat * g
    dn = dh * (1.0 + scale)
    dx = _rms_bwd(xhat, r, dn * g)
    return dx, _sum0(dh), _sum0(dh * n), _sum0(dn * xhat)


def _row_spec(rows):
    return pl.BlockSpec((rows, D_MODEL), lambda *_: (0, 0))


def _ffn_fwd(x, cond3, g, w_in_g, w_out_g):
    tm = TOKEN_TILE
    last = N_FF_CHUNK - 1

    def body(x_ref, cond_ref, g_ref, wa_ref, wb_ref, wo_ref, xo_ref, f_ref, h_scr, acc_scr):
        j = pl.program_id(1)

        @pl.when(j == 0)
        def _():
            h, _, _ = _modnorm(x_ref[...], g_ref[...], cond_ref[0:1, :], cond_ref[1:2, :])
            h_scr[...] = h.astype(BF16)
            acc_scr[...] = jnp.zeros_like(acc_scr)

        h = h_scr[...]
        a = _dot_nt(h, wa_ref[0])
        b = _dot_nt(h, wb_ref[0])
        act = (a * _sigmoid(a)) * b
        acc_scr[...] += _dot(act.astype(BF16), wo_ref[0])

        @pl.when(j == last)
        def _():
            f = acc_scr[...]
            f_ref[...] = f
            xo_ref[...] = x_ref[...] + (0.5 * cond_ref[2:3, :]) * f

    tok = pl.BlockSpec((tm, D_MODEL), lambda i, j: (i, 0))
    return pl.pallas_call(
        body, name="ffn_fwd", grid=(SEQ // tm, N_FF_CHUNK),
        out_shape=(jax.ShapeDtypeStruct((SEQ, D_MODEL), F32), jax.ShapeDtypeStruct((SEQ, D_MODEL), F32)),
        in_specs=[tok, _row_spec(3), _row_spec(1),
                  pl.BlockSpec((1, FF_PIECE, D_MODEL), lambda i, j: (j, 0, 0)),
                  pl.BlockSpec((1, FF_PIECE, D_MODEL), lambda i, j: (j + N_FF_CHUNK, 0, 0)),
                  pl.BlockSpec((1, FF_PIECE, D_MODEL), lambda i, j: (j, 0, 0))],
        out_specs=(tok, tok),
        scratch_shapes=[pltpu.VMEM((tm, D_MODEL), BF16), pltpu.VMEM((tm, D_MODEL), F32)],
        compiler_params=_params("arbitrary", "arbitrary"),
    )(x, cond3, g, w_in_g, w_in_g, w_out_g)


def _ffn_bwd(dy, x, f, cond3, g, w_in_g, w_out_g):
    tm = TOKEN_TILE
    last = N_FF_CHUNK - 1

    def body(dy_ref, x_ref, f_ref, cond_ref, g_ref, wa_ref, wb_ref, wo_ref,
             dx_ref, da_ref, db_ref, act_ref, h_ref, do_ref, part_ref, h_scr, do_scr, dh_scr):
        i, j = pl.program_id(0), pl.program_id(1)

        @pl.when(j == 0)
        def _():
            h, _, _ = _modnorm(x_ref[...], g_ref[...], cond_ref[0:1, :], cond_ref[1:2, :])
            hb = h.astype(BF16)
            h_scr[...] = hb
            h_ref[...] = hb
            do = ((0.5 * cond_ref[2:3, :]) * dy_ref[...]).astype(BF16)
            do_scr[...] = do
            do_ref[...] = do
            dh_scr[...] = jnp.zeros_like(dh_scr)

        @pl.when((i == 0) & (j == 0))
        def _():
            part_ref[...] = jnp.zeros_like(part_ref)

        h = h_scr[...]
        do = do_scr[...]
        a = _dot_nt(h, wa_ref[0])
        b = _dot_nt(h, wb_ref[0])
        dact = _dot_nt(do, wo_ref[0])
        sig = _sigmoid(a)
        s = a * sig
        da = (dact * b * (sig * (1.0 + a * (1.0 - sig)))).astype(BF16)
        db = (dact * s).astype(BF16)
        da_ref[0] = da
        db_ref[0] = db
        act_ref[0] = (s * b).astype(BF16)
        dh_scr[...] += _dot(da, wa_ref[0]) + _dot(db, wb_ref[0])

        @pl.when(j == last)
        def _():
            dyv = dy_ref[...]
            xhat, r = _rms(x_ref[...])
            dx, dshift, dscale, dg = _modnorm_bwd(xhat, r, g_ref[...], cond_ref[1:2, :], dh_scr[...])
            dx_ref[...] = dyv + dx
            part_ref[0:1, :] += dshift
            part_ref[1:2, :] += dscale
            part_ref[2:3, :] += _sum0(0.5 * dyv * f_ref[...])
            part_ref[3:4, :] += dg

    tok = pl.BlockSpec((tm, D_MODEL), lambda i, j: (i, 0))
    chunk = pl.BlockSpec((1, tm, FF_PIECE), lambda i, j: (j, i, 0))
    chunk_shape = jax.ShapeDtypeStruct((N_FF_CHUNK, SEQ, FF_PIECE), BF16)
    return pl.pallas_call(
        body, name="ffn_bwd", grid=(SEQ // tm, N_FF_CHUNK),
        out_shape=(jax.ShapeDtypeStruct((SEQ, D_MODEL), F32), chunk_shape, chunk_shape, chunk_shape,
                   jax.ShapeDtypeStruct((SEQ, D_MODEL), BF16), jax.ShapeDtypeStruct((SEQ, D_MODEL), BF16),
                   jax.ShapeDtypeStruct((SUBLANES, D_MODEL), F32)),
        in_specs=[tok, tok, tok, _row_spec(3), _row_spec(1),
                  pl.BlockSpec((1, FF_PIECE, D_MODEL), lambda i, j: (j, 0, 0)),
                  pl.BlockSpec((1, FF_PIECE, D_MODEL), lambda i, j: (j + N_FF_CHUNK, 0, 0)),
                  pl.BlockSpec((1, FF_PIECE, D_MODEL), lambda i, j: (j, 0, 0))],
        out_specs=(tok, chunk, chunk, chunk, tok, tok, _row_spec(SUBLANES)),
        scratch_shapes=[pltpu.VMEM((tm, D_MODEL), BF16), pltpu.VMEM((tm, D_MODEL), BF16),
                        pltpu.VMEM((tm, D_MODEL), F32)],
        compiler_params=_params("arbitrary", "arbitrary"),
    )(dy, x, f, cond3, g, w_in_g, w_in_g, w_out_g)


def _dw(lhs_a, rhs, lhs_b=None, name="dw"):
    pa, s, m = lhs_a.shape
    nn = rhs.shape[-1]
    pb = 0 if lhs_b is None else lhs_b.shape[0]
    two = lhs_b is not None

    def body(*refs):
        if two:
            a_ref, b_ref, r_ref, o_ref = refs
            p = pl.program_id(0)

            @pl.when(p < pa)
            def _():
                o_ref[0] = _dot_tn(a_ref[0], r_ref[...]).astype(BF16)

            @pl.when(p >= pa)
            def _():
                o_ref[0] = _dot_tn(b_ref[0], r_ref[...]).astype(BF16)
        else:
            a_ref, r_ref, o_ref = refs
            o_ref[0] = _dot_tn(a_ref[0], r_ref[...]).astype(BF16)

    if two:
        in_specs = [pl.BlockSpec((1, s, m), lambda p: (jnp.minimum(p, pa - 1), 0, 0)),
                    pl.BlockSpec((1, s, m), lambda p: (jnp.maximum(p - pa, 0), 0, 0))]
        args = (lhs_a, lhs_b, rhs)
    else:
        in_specs = [pl.BlockSpec((1, s, m), lambda p: (p, 0, 0))]
        args = (lhs_a, rhs)
    in_specs.append(pl.BlockSpec((s, nn), lambda p: (0, 0)))
    return pl.pallas_call(
        body, name=name, grid=(pa + pb,),
        out_shape=jax.ShapeDtypeStruct((pa + pb, m, nn), BF16),
        in_specs=in_specs, out_specs=pl.BlockSpec((1, m, nn), lambda p: (p, 0, 0)),
        compiler_params=_params("arbitrary"),
    )(*args)


def _mix_in_fwd(x, cond2, g, w):
    tm = TOKEN_TILE

    def body(x_ref, cond_ref, g_ref, w_ref, z_ref):
        h, _, _ = _modnorm(x_ref[...], g_ref[...], cond_ref[0:1, :], cond_ref[1:2, :])
        z_ref[...] = _dot_nt(h.astype(BF16), w_ref[...])

    return pl.pallas_call(
        body, name="mix_in_fwd", grid=(SEQ // tm,),
        out_shape=jax.ShapeDtypeStruct((SEQ, P_IN), F32),
        in_specs=[pl.BlockSpec((tm, D_MODEL), lambda i: (i, 0)), _row_spec(2), _row_spec(1),
                  pl.BlockSpec((P_IN, D_MODEL), lambda i: (0, 0))],
        out_specs=pl.BlockSpec((tm, P_IN), lambda i: (i, 0)),
        compiler_params=_params("arbitrary"),
    )(x, cond2, g, w)


MIX_SLABS = ((0, 2 * W_GRP), (2 * W_GRP, 3 * W_GRP), (3 * W_GRP, 6 * W_GRP), (6 * W_GRP, 7 * W_GRP))


def _mix_in_bwd(dzs, x, dy, cond2, g, w):
    tm = TOKEN_TILE

    def body(dza_ref, dzb_ref, dzc_ref, dzd_ref, x_ref, dy_ref, cond_ref, g_ref, w_ref, dx_ref, h_ref, dzo_ref, part_ref):
        i = pl.program_id(0)

        @pl.when(i == 0)
        def _():
            part_ref[...] = jnp.zeros_like(part_ref)

        h, xhat, r = _modnorm(x_ref[...], g_ref[...], cond_ref[0:1, :], cond_ref[1:2, :])
        h_ref[...] = h.astype(BF16)
        dh = None
        for (lo, hi), d_ref in zip(MIX_SLABS, (dza_ref, dzb_ref, dzc_ref, dzd_ref)):
            dzb = d_ref[...].astype(BF16)
            dzo_ref[:, lo:hi] = dzb
            t = _dot(dzb, w_ref[lo:hi, :])
            dh = t if dh is None else dh + t
        dx, dshift, dscale, dg = _modnorm_bwd(xhat, r, g_ref[...], cond_ref[1:2, :], dh)
        dx_ref[...] = dy_ref[...] + dx
        part_ref[0:1, :] += dshift
        part_ref[1:2, :] += dscale
        part_ref[2:3, :] += dg

    tok = pl.BlockSpec((tm, D_MODEL), lambda i: (i, 0))
    ztok = pl.BlockSpec((tm, P_IN), lambda i: (i, 0))
    slabs = [pl.BlockSpec((tm, hi - lo), lambda i: (i, 0)) for lo, hi in MIX_SLABS]
    return pl.pallas_call(
        body, name="mix_in_bwd", grid=(SEQ // tm,),
        out_shape=(jax.ShapeDtypeStruct((SEQ, D_MODEL), F32), jax.ShapeDtypeStruct((SEQ, D_MODEL), BF16),
                   jax.ShapeDtypeStruct((SEQ, P_IN), BF16), jax.ShapeDtypeStruct((SUBLANES, D_MODEL), F32)),
        in_specs=[*slabs, tok, tok, _row_spec(2), _row_spec(1), pl.BlockSpec((P_IN, D_MODEL), lambda i: (0, 0))],
        out_specs=(tok, tok, ztok, _row_spec(SUBLANES)),
        compiler_params=_params("arbitrary"),
    )(*dzs, x, dy, cond2, g, w)


def _group_norm(ys, g_ref):
    out = []
    for k, y in enumerate(ys):
        yhat, r = _rms(y)
        out.append((yhat, r, g_ref[:, k * W_GRP:(k + 1) * W_GRP]))
    return out


def _mix_out_fwd(x, ys, g, gate, w):
    tm = TOKEN_TILE

    def body(x_ref, ya_ref, yb_ref, yc_ref, yd_ref, g_ref, gate_ref, w_ref, xo_ref):
        acc = None
        for k, (yhat, _, gk) in enumerate(_group_norm([r[...] for r in (ya_ref, yb_ref, yc_ref, yd_ref)], g_ref)):
            t = _dot((yhat * gk).astype(BF16), w_ref[k * W_GRP:(k + 1) * W_GRP, :])
            acc = t if acc is None else acc + t
        xo_ref[...] = x_ref[...] + gate_ref[...] * acc

    tok = pl.BlockSpec((tm, D_MODEL), lambda i: (i, 0))
    ytok = pl.BlockSpec((tm, W_GRP), lambda i: (i, 0))
    return pl.pallas_call(
        body, name="mix_out_fwd", grid=(SEQ // tm,),
        out_shape=jax.ShapeDtypeStruct((SEQ, D_MODEL), F32),
        in_specs=[tok, ytok, ytok, ytok, ytok, _row_spec(1), _row_spec(1),
                  pl.BlockSpec((D_MODEL, D_MODEL), lambda i: (0, 0))],
        out_specs=tok, compiler_params=_params("arbitrary"),
    )(x, *ys, g, gate, w)


def _mix_out_bwd(dy, ys, g, gate, w):
    tm = TOKEN_TILE

    def body(dy_ref, ya_ref, yb_ref, yc_ref, yd_ref, g_ref, gate_ref, w_ref,
             da_ref, db_ref, dc_ref, dd_ref, yn_ref, dmo_ref, part_ref):
        i = pl.program_id(0)

        @pl.when(i == 0)
        def _():
            part_ref[...] = jnp.zeros_like(part_ref)

        dyv = dy_ref[...]
        dmo = (gate_ref[...] * dyv).astype(BF16)
        dmo_ref[...] = dmo
        dyn = _dot_nt(dmo, w_ref[...])
        norms = _group_norm([r[...] for r in (ya_ref, yb_ref, yc_ref, yd_ref)], g_ref)
        mo = None
        for k, ((yhat, r, gk), o_ref) in enumerate(zip(norms, (da_ref, db_ref, dc_ref, dd_ref))):
            sl = slice(k * W_GRP, (k + 1) * W_GRP)
            ynk = (yhat * gk).astype(BF16)
            yn_ref[:, sl] = ynk
            t = _dot(ynk, w_ref[sl, :])
            mo = t if mo is None else mo + t
            dk = dyn[:, sl]
            o_ref[...] = _rms_bwd(yhat, r, dk * gk)
            part_ref[1:2, sl] += _sum0(dk * yhat)
        part_ref[0:1, :] += _sum0(dyv * mo)

    tok = pl.BlockSpec((tm, D_MODEL), lambda i: (i, 0))
    ytok = pl.BlockSpec((tm, W_GRP), lambda i: (i, 0))
    ysh = jax.ShapeDtypeStruct((SEQ, W_GRP), F32)
    return pl.pallas_call(
        body, name="mix_out_bwd", grid=(SEQ // tm,),
        out_shape=(ysh, ysh, ysh, ysh, jax.ShapeDtypeStruct((SEQ, D_MODEL), BF16),
                   jax.ShapeDtypeStruct((SEQ, D_MODEL), BF16), jax.ShapeDtypeStruct((SUBLANES, D_MODEL), F32)),
        in_specs=[tok, ytok, ytok, ytok, ytok, _row_spec(1), _row_spec(1),
                  pl.BlockSpec((D_MODEL, D_MODEL), lambda i: (0, 0))],
        out_specs=(ytok, ytok, ytok, ytok, tok, tok, _row_spec(SUBLANES)),
        compiler_params=_params("arbitrary"),
    )(dy, *ys, g, gate, w)


def _shift_down(v, k, rows):
    return jnp.where(rows >= k, pltpu.roll(v, k, axis=0), 0.0)


def _shift_up(v, k, rows):
    n = v.shape[0]
    return jnp.where(rows < n - k, pltpu.roll(v, n - k, axis=0), 0.0)


def _zslab(width, index):
    return pl.BlockSpec((SEQ, width), lambda *_: (0, index))


def _full(shape):
    return pl.BlockSpec(shape, lambda *_: (0,) * len(shape))


def _head_avg():
    r = lax.broadcasted_iota(jnp.int32, (W_GRP, W_GRP), 0) // SGU_HEAD_DIM
    c = lax.broadcasted_iota(jnp.int32, (W_GRP, W_GRP), 1) // SGU_HEAD_DIM
    return jnp.where(r == c, 1.0 / SGU_HEAD_DIM, 0.0).astype(F32)


def _sgu_norm(za):
    z = _gelu(za)
    u, v = z[:, :W_GRP], z[:, W_GRP:]
    avg = _head_avg()
    vc = v - _dot_hi(v, avg)
    rstd = lax.rsqrt(_dot_hi(vc * vc, avg) + EPS)
    return u, vc * rstd, rstd


def _sgu_masked_w(w_ref):
    t = lax.broadcasted_iota(jnp.int32, (CHUNK, CHUNK), 0)
    s = lax.broadcasted_iota(jnp.int32, (CHUNK, CHUNK), 1)
    tril = t >= s
    return [jnp.where(tril, w_ref[:, h * CHUNK:(h + 1) * CHUNK], 0.0).astype(BF16) for h in range(SGU_HEADS)]


def _head_of_lane():
    return lax.broadcasted_iota(jnp.int32, (CHUNK, W_GRP), 1) // SGU_HEAD_DIM


def _sgu_fwd(z, w_cat, bias):
    def body(z_ref, w_ref, b_ref, y_ref, vn_scr, u_scr):
        u, vn, _ = _sgu_norm(z_ref[...])
        vn_scr[...] = vn.astype(BF16)
        u_scr[...] = u
        ws = _sgu_masked_w(w_ref)
        head = _head_of_lane()
        bias_v = b_ref[...]

        def chunk(n, carry):
            rows = pl.ds(pl.multiple_of(n * CHUNK, CHUNK), CHUNK)
            vb = vn_scr[rows, :]
            mixed = bias_v
            for h in range(SGU_HEADS):
                mixed = mixed + jnp.where(head == h, _dot(ws[h], vb), 0.0)
            y_ref[rows, :] = u_scr[rows, :] * mixed
            return carry

        lax.fori_loop(0, SEQ // CHUNK, chunk, 0)

    return pl.pallas_call(
        body, name="sgu_fwd", grid=(1,),
        out_shape=jax.ShapeDtypeStruct((SEQ, W_GRP), F32),
        in_specs=[_zslab(2 * W_GRP, 0), _full((CHUNK, SGU_HEADS * CHUNK)), _full((CHUNK, W_GRP))],
        out_specs=_full((SEQ, W_GRP)),
        scratch_shapes=[pltpu.VMEM((SEQ, W_GRP), BF16), pltpu.VMEM((SEQ, W_GRP), F32)],
        compiler_params=_params("arbitrary"),
    )(z, w_cat, bias)


def _sgu_bwd(z, dy, w_cat, bias):
    def body(z_ref, dy_ref, w_ref, b_ref, dz_ref, dw_ref, db_ref, vn_scr, u_scr, dvn_scr, du_scr):
        za = z_ref[...]
        u, vn, rstd = _sgu_norm(za)
        vn_scr[...] = vn.astype(BF16)
        u_scr[...] = u
        ws = _sgu_masked_w(w_ref)
        head = _head_of_lane()
        bias_v = b_ref[...]

        def chunk(n, carry):
            dws, dbias = carry
            rows = pl.ds(pl.multiple_of(n * CHUNK, CHUNK), CHUNK)
            vb = vn_scr[rows, :]
            mixed = bias_v
            for h in range(SGU_HEADS):
                mixed = mixed + jnp.where(head == h, _dot(ws[h], vb), 0.0)
            dyc = dy_ref[rows, :]
            du_scr[rows, :] = dyc * mixed
            dmixed = dyc * u_scr[rows, :]
            dvn = jnp.zeros((CHUNK, W_GRP), F32)
            new_dws = []
            for h in range(SGU_HEADS):
                dm = jnp.where(head == h, dmixed, 0.0).astype(BF16)
                new_dws.append(dws[h] + _dot_nt(dm, vb))
                dvn = dvn + _dot_tn(ws[h], dm)
            dvn_scr[rows, :] = dvn
            return tuple(new_dws), dbias + dmixed

        zero_w = tuple(jnp.zeros((CHUNK, CHUNK), F32) for _ in range(SGU_HEADS))
        dws, dbias = lax.fori_loop(0, SEQ // CHUNK, chunk, (zero_w, jnp.zeros((CHUNK, W_GRP), F32)))
        t = lax.broadcasted_iota(jnp.int32, (CHUNK, CHUNK), 0)
        s = lax.broadcasted_iota(jnp.int32, (CHUNK, CHUNK), 1)
        for h in range(SGU_HEADS):
            dw_ref[:, h * CHUNK:(h + 1) * CHUNK] = jnp.where(t >= s, dws[h], 0.0)
        avg = _head_avg()
        db_ref[...] = _dot_hi(dbias, avg) * float(SGU_HEAD_DIM)
        dvn = dvn_scr[...]
        dv = rstd * (dvn - _dot_hi(dvn, avg) - vn * _dot_hi(dvn * vn, avg))
        gg = _gelu_grad(za)
        dz_ref[:, :W_GRP] = gg[:, :W_GRP] * du_scr[...]
        dz_ref[:, W_GRP:] = gg[:, W_GRP:] * dv

    return pl.pallas_call(
        body, name="sgu_bwd", grid=(1,),
        out_shape=(jax.ShapeDtypeStruct((SEQ, 2 * W_GRP), F32), jax.ShapeDtypeStruct((CHUNK, SGU_HEADS * CHUNK), F32),
                   jax.ShapeDtypeStruct((CHUNK, W_GRP), F32)),
        in_specs=[_zslab(2 * W_GRP, 0), _full((SEQ, W_GRP)), _full((CHUNK, SGU_HEADS * CHUNK)), _full((CHUNK, W_GRP))],
        out_specs=(_full((SEQ, 2 * W_GRP)), _full((CHUNK, SGU_HEADS * CHUNK)), _full((CHUNK, W_GRP))),
        scratch_shapes=[pltpu.VMEM((SEQ, W_GRP), BF16), pltpu.VMEM((SEQ, W_GRP), F32),
                        pltpu.VMEM((SEQ, W_GRP), F32), pltpu.VMEM((SEQ, W_GRP), F32)],
        compiler_params=_params("arbitrary"),
    )(z, dy, w_cat, bias)


def _pool_window_of_lane(shape):
    grp = lax.broadcasted_iota(jnp.int32, shape, 1) // POOL_GROUP_DIM
    win = jnp.full(shape, POOL_WINDOWS[0], jnp.int32)
    for k in range(1, len(POOL_WINDOWS)):
        win = jnp.where(grp == k, POOL_WINDOWS[k], win)
    return grp, win


def _pool_select(levels, grp):
    out = levels[0]
    for k in range(1, len(levels)):
        out = jnp.where(grp == k, levels[k], out)
    return out


def _pool_p(z):
    shape = z.shape
    rows = lax.broadcasted_iota(jnp.int32, shape, 0)
    grp, win = _pool_window_of_lane(shape)
    levels, s, k = [], z, 1
    for _ in POOL_WINDOWS:
        s = s + _shift_down(s, k, rows)
        levels.append(s)
        k *= 2
    inv = 1.0 / jnp.minimum(rows + 1, win).astype(F32)
    return _pool_select(levels, grp) * inv - z, inv, rows, grp


def _pool_fwd(z, w_bd, scale):
    def body(z_ref, w_ref, s_ref, y_ref):
        p, _, _, _ = _pool_p(z_ref[...])
        y_ref[...] = _dot(p.astype(BF16), w_ref[...]) * s_ref[...]

    return pl.pallas_call(
        body, name="pool_fwd", grid=(1,),
        out_shape=jax.ShapeDtypeStruct((SEQ, W_GRP), F32),
        in_specs=[_zslab(W_GRP, 2), _full((W_GRP, W_GRP)), _full((1, W_GRP))],
        out_specs=_full((SEQ, W_GRP)), compiler_params=_params("arbitrary"),
    )(z, w_bd, scale)


def _pool_bwd(z, dy, w_bd, scale):
    def body(z_ref, dy_ref, w_ref, s_ref, dz_ref, dw_ref, ds_ref):
        p, inv, rows, grp = _pool_p(z_ref[...])
        pb = p.astype(BF16)
        dyv = dy_ref[...]
        ds_ref[...] = _sum0(dyv * _dot(pb, w_ref[...]))
        dpre = (dyv * s_ref[...]).astype(BF16)
        dw_ref[...] = _dot_tn(pb, dpre)
        dp = _dot_nt(dpre, w_ref[...])
        q = dp * inv
        levels, s, k = [], q, 1
        for _ in POOL_WINDOWS:
            s = s + _shift_up(s, k, rows)
            levels.append(s)
            k *= 2
        dz_ref[...] = _pool_select(levels, grp) - dp

    return pl.pallas_call(
        body, name="pool_bwd", grid=(1,),
        out_shape=(jax.ShapeDtypeStruct((SEQ, W_GRP), F32), jax.ShapeDtypeStruct((W_GRP, W_GRP), F32),
                   jax.ShapeDtypeStruct((1, W_GRP), F32)),
        in_specs=[_zslab(W_GRP, 2), _full((SEQ, W_GRP)), _full((W_GRP, W_GRP)), _full((1, W_GRP))],
        out_specs=(_full((SEQ, W_GRP)), _full((W_GRP, W_GRP)), _full((1, W_GRP))),
        compiler_params=_params("arbitrary"),
    )(z, dy, w_bd, scale)


def _conv_fwd(z, w):
    def body(z_ref, w_ref, y_ref):
        zc = z_ref[...]
        bg, cg, xh = zc[:, :W_GRP], zc[:, W_GRP:2 * W_GRP], zc[:, 2 * W_GRP:]
        rows = lax.broadcasted_iota(jnp.int32, (SEQ, W_GRP), 0)
        y = cg * xh
        conv = w_ref[0:1, :] * _shift_down(y, 2, rows) + w_ref[1:2, :] * _shift_down(y, 1, rows) + w_ref[2:3, :] * y
        y_ref[...] = bg * conv

    return pl.pallas_call(
        body, name="conv_fwd", grid=(1,),
        out_shape=jax.ShapeDtypeStruct((SEQ, W_GRP), F32),
        in_specs=[_zslab(3 * W_GRP, 1), _full((3, W_GRP))],
        out_specs=_full((SEQ, W_GRP)), compiler_params=_params("arbitrary"),
    )(z, w)


def _conv_bwd(z, dy, w):
    def body(z_ref, dy_ref, w_ref, dz_ref, dw_ref):
        zc = z_ref[...]
        bg, cg, xh = zc[:, :W_GRP], zc[:, W_GRP:2 * W_GRP], zc[:, 2 * W_GRP:]
        rows = lax.broadcasted_iota(jnp.int32, (SEQ, W_GRP), 0)
        y = cg * xh
        y2, y1 = _shift_down(y, 2, rows), _shift_down(y, 1, rows)
        conv = w_ref[0:1, :] * y2 + w_ref[1:2, :] * y1 + w_ref[2:3, :] * y
        dyv = dy_ref[...]
        dconv = dyv * bg
        dw_ref[...] = jnp.zeros_like(dw_ref)
        dw_ref[0:1, :] = _sum0(dconv * y2)
        dw_ref[1:2, :] = _sum0(dconv * y1)
        dw_ref[2:3, :] = _sum0(dconv * y)
        dyy = (w_ref[0:1, :] * _shift_up(dconv, 2, rows) + w_ref[1:2, :] * _shift_up(dconv, 1, rows)
               + w_ref[2:3, :] * dconv)
        dz_ref[:, :W_GRP] = dyv * conv
        dz_ref[:, W_GRP:2 * W_GRP] = dyy * xh
        dz_ref[:, 2 * W_GRP:] = dyy * cg

    return pl.pallas_call(
        body, name="conv_bwd", grid=(1,),
        out_shape=(jax.ShapeDtypeStruct((SEQ, 3 * W_GRP), F32), jax.ShapeDtypeStruct((SUBLANES, W_GRP), F32)),
        in_specs=[_zslab(3 * W_GRP, 1), _full((SEQ, W_GRP)), _full((3, W_GRP))],
        out_specs=(_full((SEQ, 3 * W_GRP)), _full((SUBLANES, W_GRP))),
        compiler_params=_params("arbitrary"),
    )(z, dy, w)


def _s5_disc(lre, lim, ldt, br, bi):
    dt = jnp.exp(ldt)
    mag = jnp.exp(lre * dt)
    ang = lim * dt
    a_re, a_im = mag * jnp.cos(ang), mag * jnp.sin(ang)
    nr, ni = a_re - 1.0, a_im
    den = lre * lre + lim * lim
    k_re = (nr * lre + ni * lim) / den
    k_im = (ni * lre - nr * lim) / den
    return a_re, a_im, k_re * br - k_im * bi, k_re * bi + k_im * br


def _s5_prep_fwd(lre, lim, ldt, br, bi):
    def body(lre_ref, lim_ref, ldt_ref, br_ref, bi_ref, ar_ref, ai_ref, bbr_ref, bbi_ref):
        ar, ai, bbr, bbi = _s5_disc(lre_ref[...], lim_ref[...], ldt_ref[...], br_ref[...], bi_ref[...])
        ar_ref[...] = ar
        ai_ref[...] = ai
        bbr_ref[...] = bbr
        bbi_ref[...] = bbi

    return pl.pallas_call(
        body, name="s5_prep_fwd",
        out_shape=(jax.ShapeDtypeStruct(lre.shape, F32), jax.ShapeDtypeStruct(lre.shape, F32),
                   jax.ShapeDtypeStruct(br.shape, F32), jax.ShapeDtypeStruct(br.shape, F32)),
        compiler_params=_params(),
    )(lre, lim, ldt, br, bi)


def _s5_prep_bwd(lre, lim, ldt, br, bi, dar, dai, dbbr, dbbi):
    def body(lre_ref, lim_ref, ldt_ref, br_ref, bi_ref, dar_ref, dai_ref, dbbr_ref, dbbi_ref,
             o_lre, o_lim, o_ldt, o_br, o_bi):
        _, pull = jax.vjp(_s5_disc, lre_ref[...], lim_ref[...], ldt_ref[...], br_ref[...], bi_ref[...])
        g = pull((dar_ref[...], dai_ref[...], dbbr_ref[...], dbbi_ref[...]))
        for o, v in zip((o_lre, o_lim, o_ldt, o_br, o_bi), g):
            o[...] = v

    return pl.pallas_call(
        body, name="s5_prep_bwd",
        out_shape=tuple(jax.ShapeDtypeStruct(a.shape, F32) for a in (lre, lim, ldt, br, bi)),
        compiler_params=_params(),
    )(lre, lim, ldt, br, bi, dar, dai, dbbr, dbbi)


def _cmul(ar, ai, br, bi):
    return ar * br - ai * bi, ar * bi + ai * br


def _s5_tile_consts(ar, ai, reverse):
    if reverse:
        ai = -ai
    shape = (SUBLANES, LANES)
    row = lax.broadcasted_iota(jnp.int32, shape, 0)
    a1 = (jnp.broadcast_to(ar, shape), jnp.broadcast_to(ai, shape))
    a2 = _cmul(*a1, *a1)
    a4 = _cmul(*a2, *a2)
    a8 = _cmul(*a4, *a4)
    steps = []
    for s, (pr, pi) in ((1, a1), (2, a2), (4, a4)):
        keep = (row < SUBLANES - s) if reverse else (row >= s)
        steps.append((s, jnp.where(keep, pr, 0.0), jnp.where(keep, pi, 0.0)))
    e = (SUBLANES - row) if reverse else (row + 1)
    pr, pi = jnp.ones(shape, F32), jnp.zeros(shape, F32)
    for bit, (qr, qi) in ((1, a1), (2, a2), (4, a4), (8, a8)):
        nr, ni = _cmul(pr, pi, qr, qi)
        hit = (e & bit) != 0
        pr, pi = jnp.where(hit, nr, pr), jnp.where(hit, ni, pi)
    return steps, pr, pi


def _s5_tile(xr, xi, steps, reverse):
    for s, pr, pi in steps:
        sh = SUBLANES - s if reverse else s
        sr, si = pltpu.roll(xr, sh, axis=0), pltpu.roll(xi, sh, axis=0)
        xr, xi = xr + pr * sr - pi * si, xi + pr * si + pi * sr
    return xr, xi


N_TILES = SEQ // SUBLANES


def _s5_fwd(z, b_re, b_im, c_re, c_im, a_re, a_im, d, glu_w, glu_b):
    nblk = S5_LANES // LANES

    def body(u_ref, br_ref, bi_ref, cr_ref, ci_ref, ar_ref, ai_ref, d_ref, gw_ref, gb_ref,
             y_ref, y0_ref, xr_ref, xi_ref, ub_scr, acc_scr):
        jb = pl.program_id(0)

        @pl.when(jb == 0)
        def _():
            ub_scr[...] = u_ref[...].astype(BF16)
            acc_scr[...] = jnp.zeros_like(acc_scr)

        ub = ub_scr[...]
        xr_ref[...] = _dot(ub, br_ref[...])
        xi_ref[...] = _dot(ub, bi_ref[...])
        steps, pr, pi = _s5_tile_consts(ar_ref[...], ai_ref[...], False)

        def tile(t, carry):
            cr, ci = carry
            rows = pl.ds(pl.multiple_of(t * SUBLANES, SUBLANES), SUBLANES)
            xr, xi = _s5_tile(xr_ref[rows, :], xi_ref[rows, :], steps, False)
            xr, xi = xr + pr * cr - pi * ci, xi + pr * ci + pi * cr
            xr_ref[rows, :] = xr
            xi_ref[rows, :] = xi
            return xr[SUBLANES - 1:, :], xi[SUBLANES - 1:, :]

        zero = jnp.zeros((1, LANES), F32)
        lax.fori_loop(0, N_TILES, tile, (zero, zero), unroll=4)
        acc_scr[...] += (_dot(xr_ref[...].astype(BF16), cr_ref[...]) - _dot(xi_ref[...].astype(BF16), ci_ref[...]))

        @pl.when(jb == nblk - 1)
        def _():
            y0 = acc_scr[...] + d_ref[...] * u_ref[...]
            y0_ref[...] = y0
            y1 = _gelu(y0)
            y_ref[...] = y1 * _sigmoid(_dot(y1.astype(BF16), gw_ref[...]) + gb_ref[...])

    lane_blk = pl.BlockSpec((SEQ, LANES), lambda j: (0, j))
    return pl.pallas_call(
        body, name="s5_fwd", grid=(nblk,),
        out_shape=(jax.ShapeDtypeStruct((SEQ, W_GRP), F32), jax.ShapeDtypeStruct((SEQ, W_GRP), F32),
                   jax.ShapeDtypeStruct((SEQ, S5_LANES), F32), jax.ShapeDtypeStruct((SEQ, S5_LANES), F32)),
        in_specs=[_zslab(W_GRP, 6),
                  pl.BlockSpec((W_GRP, LANES), lambda j: (0, j)), pl.BlockSpec((W_GRP, LANES), lambda j: (0, j)),
                  pl.BlockSpec((LANES, W_GRP), lambda j: (j, 0)), pl.BlockSpec((LANES, W_GRP), lambda j: (j, 0)),
                  pl.BlockSpec((1, LANES), lambda j: (0, j)), pl.BlockSpec((1, LANES), lambda j: (0, j)),
                  _full((1, W_GRP)), _full((W_GRP, W_GRP)), _full((1, W_GRP))],
        out_specs=(_full((SEQ, W_GRP)), _full((SEQ, W_GRP)), lane_blk, lane_blk),
        scratch_shapes=[pltpu.VMEM((SEQ, W_GRP), BF16), pltpu.VMEM((SEQ, W_GRP), F32)],
        compiler_params=_params("arbitrary"),
    )(z, b_re, b_im, c_re, c_im, a_re, a_im, d, glu_w, glu_b)


def _s5_bwd(z, y0, dy, xr, xi, b_re, b_im, c_re, c_im, a_re, a_im, d, glu_w, glu_b):
    nblk = S5_LANES // LANES

    def body(u_ref, y0_ref, dy_ref, xr_ref, xi_ref, br_ref, bi_ref, cr_ref, ci_ref, ar_ref, ai_ref,
             d_ref, gw_ref, gb_ref,
             du_ref, dbr_ref, dbi_ref, dcr_ref, dci_ref, dar_ref, dai_ref, dd_ref, dgw_ref, dgb_ref,
             ub_scr, dy0_scr, du_scr, lr_scr, li_scr):
        jb = pl.program_id(0)

        @pl.when(jb == 0)
        def _():
            u = u_ref[...]
            y0v = y0_ref[...]
            y1 = _gelu(y0v)
            y1b = y1.astype(BF16)
            sg = _sigmoid(_dot(y1b, gw_ref[...]) + gb_ref[...])
            dyv = dy_ref[...]
            dpre = dyv * y1 * sg * (1.0 - sg)
            dpb = dpre.astype(BF16)
            dgw_ref[...] = _dot_tn(y1b, dpb)
            dgb_ref[...] = _sum0(dpre)
            dy1 = dyv * sg + _dot_nt(dpb, gw_ref[...])
            dy0 = dy1 * _gelu_grad(y0v)
            dd_ref[...] = _sum0(dy0 * u)
            du_scr[...] = dy0 * d_ref[...]
            dy0_scr[...] = dy0.astype(BF16)
            ub_scr[...] = u.astype(BF16)

        dy0b = dy0_scr[...]
        lr_scr[...] = _dot_nt(dy0b, cr_ref[...])
        li_scr[...] = -_dot_nt(dy0b, ci_ref[...])
        dcr_ref[...] = _dot_tn(xr_ref[...].astype(BF16), dy0b)
        dci_ref[...] = -_dot_tn(xi_ref[...].astype(BF16), dy0b)
        steps, pr, pi = _s5_tile_consts(ar_ref[...], ai_ref[...], True)
        row = lax.broadcasted_iota(jnp.int32, (SUBLANES, LANES), 0)

        def tile(k, carry):
            cr, ci, accr, acci = carry
            t = N_TILES - 1 - k
            rows = pl.ds(pl.multiple_of(t * SUBLANES, SUBLANES), SUBLANES)
            lr, li = _s5_tile(lr_scr[rows, :], li_scr[rows, :], steps, True)
            lr, li = lr + pr * cr - pi * ci, li + pr * ci + pi * cr
            lr_scr[rows, :] = lr
            li_scr[rows, :] = li
            prev = pl.ds(pl.multiple_of(jnp.maximum(t - 1, 0) * SUBLANES, SUBLANES), SUBLANES)
            live = jnp.where(t > 0, 1.0, 0.0)
            xpr = jnp.where(row == 0, pltpu.roll(xr_ref[prev, :], 1, axis=0) * live, pltpu.roll(xr_ref[rows, :], 1, axis=0))
            xpi = jnp.where(row == 0, pltpu.roll(xi_ref[prev, :], 1, axis=0) * live, pltpu.roll(xi_ref[rows, :], 1, axis=0))
            accr = accr + lr * xpr + li * xpi
            acci = acci + li * xpr - lr * xpi
            return lr[0:1, :], li[0:1, :], accr, acci

        zero = jnp.zeros((1, LANES), F32)
        zt = jnp.zeros((SUBLANES, LANES), F32)
        _, _, accr, acci = lax.fori_loop(0, N_TILES, tile, (zero, zero, zt, zt), unroll=4)
        dar_ref[...] = jnp.zeros_like(dar_ref)
        dai_ref[...] = jnp.zeros_like(dai_ref)
        dar_ref[0:1, :] = _sum0(accr)
        dai_ref[0:1, :] = _sum0(acci)
        lrb, lib = lr_scr[...].astype(BF16), li_scr[...].astype(BF16)
        ub = ub_scr[...]
        dbr_ref[...] = _dot_tn(ub, lrb)
        dbi_ref[...] = _dot_tn(ub, lib)
        du_scr[...] += _dot_nt(lrb, br_ref[...]) + _dot_nt(lib, bi_ref[...])

        @pl.when(jb == nblk - 1)
        def _():
            du_ref[...] = du_scr[...]

    lane_blk = pl.BlockSpec((SEQ, LANES), lambda j: (0, j))
    bspec = pl.BlockSpec((W_GRP, LANES), lambda j: (0, j))
    cspec = pl.BlockSpec((LANES, W_GRP), lambda j: (j, 0))
    aspec = pl.BlockSpec((1, LANES), lambda j: (0, j))
    a8spec = pl.BlockSpec((SUBLANES, LANES), lambda j: (0, j))
    sd = jax.ShapeDtypeStruct
    return pl.pallas_call(
        body, name="s5_bwd", grid=(nblk,),
        out_shape=(sd((SEQ, W_GRP), F32), sd((W_GRP, S5_LANES), F32), sd((W_GRP, S5_LANES), F32),
                   sd((S5_LANES, W_GRP), F32), sd((S5_LANES, W_GRP), F32),
                   sd((SUBLANES, S5_LANES), F32), sd((SUBLANES, S5_LANES), F32),
                   sd((1, W_GRP), F32), sd((W_GRP, W_GRP), F32), sd((1, W_GRP), F32)),
        in_specs=[_zslab(W_GRP, 6), _full((SEQ, W_GRP)), _full((SEQ, W_GRP)), lane_blk, lane_blk,
                  bspec, bspec, cspec, cspec, aspec, aspec,
                  _full((1, W_GRP)), _full((W_GRP, W_GRP)), _full((1, W_GRP))],
        out_specs=(_full((SEQ, W_GRP)), bspec, bspec, cspec, cspec, a8spec, a8spec,
                   _full((1, W_GRP)), _full((W_GRP, W_GRP)), _full((1, W_GRP))),
        scratch_shapes=[pltpu.VMEM((SEQ, W_GRP), BF16), pltpu.VMEM((SEQ, W_GRP), BF16), pltpu.VMEM((SEQ, W_GRP), F32),
                        pltpu.VMEM((SEQ, LANES), F32), pltpu.VMEM((SEQ, LANES), F32)],
        compiler_params=_params("arbitrary"),
    )(z, y0, dy, xr, xi, b_re, b_im, c_re, c_im, a_re, a_im, d, glu_w, glu_b)


def _head(x, g, target):
    tm = TOKEN_TILE
    n = SEQ // tm

    def body(x_ref, g_ref, t_ref, dx_ref, st_ref, acc_scr):
        i = pl.program_id(0)

        @pl.when(i == 0)
        def _():
            acc_scr[...] = jnp.zeros_like(acc_scr)

        xhat, r = _rms(x_ref[...])
        gv = g_ref[...]
        err = xhat * gv - t_ref[...]
        dyv = err * (1.0 / D_MODEL)
        dx_ref[...] = _rms_bwd(xhat, r, dyv * gv)
        acc_scr[0:1, :] += _sum0(err * err)
        acc_scr[1:2, :] += _sum0(dyv * xhat)

        @pl.when(i == n - 1)
        def _():
            st_ref[...] = acc_scr[...]
            tot = jnp.sum(acc_scr[0:1, :], axis=-1, keepdims=True) * (0.5 / D_MODEL)
            st_ref[0:1, :] = jnp.broadcast_to(tot, (1, D_MODEL))

    tok = pl.BlockSpec((tm, D_MODEL), lambda i: (i, 0))
    return pl.pallas_call(
        body, name="head", grid=(n,),
        out_shape=(jax.ShapeDtypeStruct((SEQ, D_MODEL), F32), jax.ShapeDtypeStruct((SUBLANES, D_MODEL), F32)),
        in_specs=[tok, _row_spec(1), tok], out_specs=(tok, _row_spec(SUBLANES)),
        scratch_shapes=[pltpu.VMEM((SUBLANES, D_MODEL), F32)],
        compiler_params=_params("arbitrary"),
    )(x, g, target)


def _adamw(w, gparts, m, v, name):
    r, c = w.shape
    npart = gparts.shape[0]
    tr = r
    for cand in (512, 256, 128, 64, 32, 16):
        if r % cand == 0 and r > cand:
            tr = cand
            break
    b1c = 1.0 - ADAM_B1 ** ADAM_STEP
    b2c = 1.0 - ADAM_B2 ** ADAM_STEP

    def body(w_ref, g_ref, m_ref, v_ref, go_ref, d_ref, mo_ref, vo_ref):
        g = g_ref[0].astype(F32)
        for k in range(1, npart):
            g = g + g_ref[k].astype(F32)
        mn = ADAM_B1 * m_ref[...] + (1.0 - ADAM_B1) * g
        vn = ADAM_B2 * v_ref[...] + (1.0 - ADAM_B2) * (g * g)
        m_hat = mn / b1c
        v_hat = vn / b2c
        go_ref[...] = g
        d_ref[...] = -ADAM_LR * (m_hat / (jnp.sqrt(v_hat) + ADAM_EPS) + ADAM_WD * w_ref[...])
        mo_ref[...] = mn
        vo_ref[...] = vn

    blk = pl.BlockSpec((tr, c), lambda i: (i, 0))
    sh = jax.ShapeDtypeStruct((r, c), F32)
    return pl.pallas_call(
        body, name=name, grid=(r // tr,),
        out_shape=(sh, sh, sh, sh),
        in_specs=[blk, pl.BlockSpec((npart, tr, c), lambda i: (0, i, 0)), blk, blk],
        out_specs=(blk, blk, blk, blk), compiler_params=_params("arbitrary"),
    )(w, gparts, m, v)


def _adamw_layer(l, w, gparts, m, v, prev, name):
    _, r, c = w.shape
    npart = gparts.shape[0]
    tr = max(t for t in range(16, 513, 16) if r % t == 0)
    b1c = 1.0 - ADAM_B1 ** ADAM_STEP
    b2c = 1.0 - ADAM_B2 ** ADAM_STEP
    nprev = 0 if prev is None else 4

    def body(*refs):
        w_ref, g_ref, m_ref, v_ref = refs[:4]
        go_ref, d_ref, mo_ref, vo_ref = refs[4 + nprev:]
        g = g_ref[0].astype(F32)
        for k in range(1, npart):
            g = g + g_ref[k].astype(F32)
        mn = ADAM_B1 * m_ref[0] + (1.0 - ADAM_B1) * g
        vn = ADAM_B2 * v_ref[0] + (1.0 - ADAM_B2) * (g * g)
        go_ref[0] = g
        d_ref[0] = -ADAM_LR * ((mn / b1c) / (jnp.sqrt(vn / b2c) + ADAM_EPS) + ADAM_WD * w_ref[0])
        mo_ref[0] = mn
        vo_ref[0] = vn

    blk = pl.BlockSpec((1, tr, c), lambda i: (l, i, 0))
    sh = jax.ShapeDtypeStruct(w.shape, F32)
    keep = [pl.BlockSpec(memory_space=pl.ANY)] * nprev
    return pl.pallas_call(
        body, name=name, grid=(r // tr,),
        out_shape=(sh, sh, sh, sh),
        in_specs=[blk, pl.BlockSpec((npart, tr, c), lambda i: (0, i, 0)), blk, blk, *keep],
        out_specs=(blk, blk, blk, blk),
        input_output_aliases={4 + k: k for k in range(nprev)},
        compiler_params=_params("arbitrary"),
    )(w, gparts, m, v, *(prev or ()))


def _sum_parts(parts, name):
    n, r, c = parts.shape

    def body(p_ref, o_ref):
        acc = p_ref[0]
        for k in range(1, n):
            acc = acc + p_ref[k]
        o_ref[...] = acc

    return pl.pallas_call(
        body, name=name, out_shape=jax.ShapeDtypeStruct((r, c), F32), compiler_params=_params(),
    )(parts)


def _block_diag(blocks):
    g, a, b = blocks.shape
    eye = jnp.eye(g, dtype=blocks.dtype)
    return (blocks[:, :, None, :] * eye[:, None, :, None]).reshape(g * a, g * b)


def _diag_blocks(dense, g):
    a, b = dense.shape[0] // g, dense.shape[1] // g
    d4 = dense.reshape(g, a, g, b)
    eye = jnp.eye(g, dtype=dense.dtype)
    return jnp.sum(d4 * eye[:, None, :, None], axis=2)


def _pack(parts, cols):
    flat = jnp.concatenate([p.reshape(-1) for p in parts])
    unit = N_DEV * SUBLANES * cols
    total = -(-flat.shape[0] // unit) * unit
    flat = jnp.pad(flat, (0, total - flat.shape[0]))
    return flat.reshape(N_DEV, total // (N_DEV * cols), cols)


def _unpack(flat, shapes):
    out, pos = [], 0
    for s in shapes:
        n = math.prod(s)
        out.append(flat[pos:pos + n].reshape(s))
        pos += n
    return out


SMALL = ("norm1_g", "norm2_g", "sgu_w", "sgu_b", "pool_w", "pool_scale", "conv_w", "s5_lambda_re", "s5_lambda_im",
         "s5_b_re", "s5_b_im", "s5_c_re", "s5_c_im", "s5_d", "s5_log_dt", "s5_glu_w", "s5_glu_b", "mix_norm_g",
         "norm3_g", "final_norm_g")
BIG = ("ffn1_w_in", "ffn1_w_out", "w_mix_in", "w_mix_out", "ffn2_w_in", "ffn2_w_out")
TRANSPOSED = ("ffn1_w_in", "w_mix_in", "ffn2_w_in")
WEIGHTS = ("ada_w", "ada_b", "norm1_g", "ffn1_w_in", "ffn1_w_out", "norm2_g", "w_mix_in", "sgu_w", "sgu_b", "pool_w",
           "pool_scale", "conv_w", "s5_lambda_re", "s5_lambda_im", "s5_b_re", "s5_b_im", "s5_c_re", "s5_c_im", "s5_d",
           "s5_log_dt", "s5_glu_w", "s5_glu_b", "mix_norm_g", "w_mix_out", "norm3_g", "ffn2_w_in", "ffn2_w_out",
           "final_norm_g")
PACK_COLS = 1024


def kernel(x, c, ada_w, ada_b, norm1_g, ffn1_w_in, ffn1_w_out, norm2_g, w_mix_in, sgu_w, sgu_b, pool_w, pool_scale, conv_w, s5_lambda_re, s5_lambda_im, s5_b_re, s5_b_im, s5_c_re, s5_c_im, s5_d, s5_log_dt, s5_glu_w, s5_glu_b, mix_norm_g, w_mix_out, norm3_g, ffn2_w_in, ffn2_w_out, final_norm_g, loss_target, m_ada_w, m_ada_b, m_norm1_g, m_ffn1_w_in, m_ffn1_w_out, m_norm2_g, m_w_mix_in, m_sgu_w, m_sgu_b, m_pool_w, m_pool_scale, m_conv_w, m_s5_lambda_re, m_s5_lambda_im, m_s5_b_re, m_s5_b_im, m_s5_c_re, m_s5_c_im, m_s5_d, m_s5_log_dt, m_s5_glu_w, m_s5_glu_b, m_mix_norm_g, m_w_mix_out, m_norm3_g, m_ffn2_w_in, m_ffn2_w_out, m_final_norm_g, v_ada_w, v_ada_b, v_norm1_g, v_ffn1_w_in, v_ffn1_w_out, v_norm2_g, v_w_mix_in, v_sgu_w, v_sgu_b, v_pool_w, v_pool_scale, v_conv_w, v_s5_lambda_re, v_s5_lambda_im, v_s5_b_re, v_s5_b_im, v_s5_c_re, v_s5_c_im, v_s5_d, v_s5_log_dt, v_s5_glu_w, v_s5_glu_b, v_mix_norm_g, v_w_mix_out, v_norm3_g, v_ffn2_w_in, v_ffn2_w_out, v_final_norm_g):
    args = dict(locals())
    W = {n: args[n] for n in WEIGHTS}
    M = {n: args["m_" + n] for n in WEIGHTS}
    V = {n: args["v_" + n] for n in WEIGHTS}
    me = _me()
    L = DEPTH
    x0 = x[0]
    target = loss_target[0]

    conv_cols = conv_w.shape[-1]
    glu_rows = s5_glu_w.shape[1]
    c_g, conv_g, glu_g = _exchange(
        [c.reshape(SUBLANES, LANES), conv_w.reshape(L * 3, conv_cols), s5_glu_w.reshape(L * glu_rows, W_GRP)],
        False, "gather_small")
    c_all = c_g.reshape(N_DEV, D_MODEL)
    conv_full = conv_g.reshape(N_DEV, L, 3, conv_cols).transpose(1, 2, 0, 3).reshape(L, 3, W_GRP)
    glu_full = glu_g.reshape(N_DEV, L, glu_rows, W_GRP).transpose(1, 0, 2, 3).reshape(L, W_GRP, W_GRP)

    ncol = ada_w.shape[-1]
    ada_b_mine = lax.dynamic_slice_in_dim(ada_b, me * ncol, ncol, axis=1).reshape(L, 1, ncol)
    cond_part = _cond_fwd(c_all, ada_w, ada_b_mine)
    (cond_g,) = _exchange([cond_part.reshape(L * N_DEV, ncol)], False, "gather_cond")
    cond_g = cond_g.reshape(N_DEV, L, N_DEV, ncol)
    cond_mine = lax.dynamic_index_in_dim(cond_g, me, axis=2, keepdims=False)
    cond = cond_mine.transpose(1, 0, 2).reshape(L, N_ADA, D_MODEL)

    lg = L * S5_GROUPS
    lre3 = s5_lambda_re.reshape(lg, S5_STATE, 1)
    lim3 = s5_lambda_im.reshape(lg, S5_STATE, 1)
    ldt3 = s5_log_dt.reshape(lg, 1, 1)
    br3 = s5_b_re.reshape(lg, S5_STATE, S5_GROUP_CH)
    bi3 = s5_b_im.reshape(lg, S5_STATE, S5_GROUP_CH)
    a_re3, a_im3, bb_re3, bb_im3 = _s5_prep_fwd(lre3, lim3, ldt3, br3, bi3)
    a_re = a_re3.reshape(L, 1, S5_LANES)
    a_im = a_im3.reshape(L, 1, S5_LANES)

    def b_mat(bb3, l):
        return _block_diag(bb3.reshape(L, S5_GROUPS, S5_STATE, S5_GROUP_CH)[l].transpose(0, 2, 1)).astype(BF16)

    def c_mat(cw, l):
        return _block_diag(cw[l].transpose(0, 2, 1)).astype(BF16)

    for n in TRANSPOSED:
        W[n], M[n], V[n] = (a.transpose(0, 2, 1) for a in (W[n], M[n], V[n]))

    def gather_start(l, after):
        srcs = [W[n][l].astype(BF16) for n in BIG]
        lands = _place_own(srcs, False, "gather_weights_own")
        return _exchange_start(srcs, lands, after, False, "gather_weights_start", OTHER_CHIPS)

    def gather_finish(handle, after):
        g = _exchange_wait(handle, after, False, "gather_weights_wait", OTHER_CHIPS)
        out = dict(zip(BIG, _gather_sibling(g, "gather_weights_sibling")))
        out["ffn1_w_out"] = out["ffn1_w_out"].reshape(N_FF_CHUNK, FF_PIECE, D_MODEL)
        out["ffn2_w_out"] = out["ffn2_w_out"].reshape(N_FF_CHUNK, FF_PIECE, D_MODEL)
        out["w_mix_in"] = out["w_mix_in"].reshape(P_IN, D_MODEL)
        out["w_mix_out"] = out["w_mix_out"].reshape(D_MODEL, D_MODEL)
        return out

    def mixer_consts(l):
        w_cat = sgu_w[l].transpose(1, 0, 2).reshape(CHUNK, SGU_HEADS * CHUNK)
        bias = jnp.repeat(sgu_b[l].T, SGU_HEAD_DIM, axis=1)
        return dict(
            w_cat=w_cat, bias=bias, pool_bd=_block_diag(pool_w[l]).astype(BF16), pool_scale=pool_scale[l][None],
            conv=conv_full[l], b_re=b_mat(bb_re3, l), b_im=b_mat(bb_im3, l), c_re=c_mat(s5_c_re, l),
            c_im=c_mat(s5_c_im, l), a_re=a_re[l], a_im=a_im[l], d=s5_d[l][None], glu_w=glu_full[l].astype(BF16),
            glu_b=s5_glu_b[l][None])

    saved = []
    xc = x0
    handle, _ = gather_start(0, cond)
    wl = gather_finish(handle, x0)
    for l in range(L):
        cl = cond[l]
        if l + 1 < L:
            handle, token = gather_start(l + 1, wl["ffn1_w_in"])
            cl = cl + token[0, 0]
        mc = mixer_consts(l)
        x_a = xc
        x_b, f1 = _ffn_fwd(x_a, cl[0:3], norm1_g[l][None], wl["ffn1_w_in"], wl["ffn1_w_out"])
        z = _mix_in_fwd(x_b, cl[3:5], norm2_g[l][None], wl["w_mix_in"])
        ya = _sgu_fwd(z, mc["w_cat"], mc["bias"])
        yb = _pool_fwd(z, mc["pool_bd"], mc["pool_scale"])
        yc = _conv_fwd(z, mc["conv"])
        yd, y0, sxr, sxi = _s5_fwd(z, mc["b_re"], mc["b_im"], mc["c_re"], mc["c_im"], mc["a_re"], mc["a_im"],
                                   mc["d"], mc["glu_w"], mc["glu_b"])
        ys = (ya, yb, yc, yd)
        x_c = _mix_out_fwd(x_b, ys, mix_norm_g[l][None], cl[5:6], wl["w_mix_out"])
        x_d, f2 = _ffn_fwd(x_c, cl[6:9], norm3_g[l][None], wl["ffn2_w_in"], wl["ffn2_w_out"])
        saved.append(dict(wl=wl, mc=mc, x_a=x_a, x_b=x_b, x_c=x_c, f1=f1, f2=f2, z=z, ys=ys, y0=y0, xr=sxr, xi=sxi))
        xc = x_d
        if l + 1 < L:
            wl = gather_finish(handle, x_d)

    dx, stats = _head(xc, final_norm_g[None], target)
    loss = lax.psum(stats[0, 0], MESH_AXES)

    small_grads = {n: [None] * L for n in SMALL if n != "final_norm_g"}
    small_grads["final_norm_g"] = stats[1]
    big_out = {n: None for n in BIG}
    pending = None

    def finish_scatter(pend, after):
        layer, hnd = pend
        recv = _exchange_wait(hnd, after, True, "scatter_grads_wait")
        for n, r in zip(BIG, recv):
            big_out[n] = _adamw_layer(layer, W[n], r, M[n], V[n], big_out[n], "adamw_" + n)
        return recv[0]

    dcond_rows = [None] * L
    d_are, d_aim, d_bbre, d_bbim = [None] * L, [None] * L, [None] * L, [None] * L
    token = None
    for l in reversed(range(L)):
        sv = saved[l]
        wl, mc, cl = sv["wl"], sv["mc"], cond[l]
        if token is not None:
            cl = cl + token[0, 0]
        dx, da, db, act, hb, dob, part3 = _ffn_bwd(dx, sv["x_c"], sv["f2"], cl[6:9], norm3_g[l][None],
                                                   wl["ffn2_w_in"], wl["ffn2_w_out"])
        g_ffn2_in = _dw(da, hb, db, name="dw_ffn_in")
        g_ffn2_out = _dw(act, dob, name="dw_ffn_out").reshape(N_DEV, D_FF // N_DEV, D_MODEL)
        dya, dyb, dyc, dyd, ynb, dmob, part_mo = _mix_out_bwd(dx, sv["ys"], mix_norm_g[l][None], cl[5:6],
                                                              wl["w_mix_out"])
        g_mix_out = _dw(ynb[None], dmob, name="dw_mix_out").reshape(N_DEV, D_MODEL // N_DEV, D_MODEL)
        z = sv["z"]
        dza, dw_cat, dbias = _sgu_bwd(z, dya, mc["w_cat"], mc["bias"])
        dzb, dpool_dense, dpool_scale = _pool_bwd(z, dyb, mc["pool_bd"], mc["pool_scale"])
        dzc, dconv8 = _conv_bwd(z, dyc, mc["conv"])
        (dzd, dbre_d, dbim_d, dcre_d, dcim_d, dar8, dai8, dd, dglu_w, dglu_b) = _s5_bwd(
            z, sv["y0"], dyd, sv["xr"], sv["xi"], mc["b_re"], mc["b_im"], mc["c_re"], mc["c_im"],
            mc["a_re"], mc["a_im"], mc["d"], mc["glu_w"], mc["glu_b"])
        dx, h2b, dzbf, part2 = _mix_in_bwd((dza, dzb, dzc, dzd), sv["x_b"], dx, cl[3:5], norm2_g[l][None],
                                           wl["w_mix_in"])
        g_mix_in = _dw(dzbf[None], h2b, name="dw_mix_in").reshape(N_DEV, P_IN // N_DEV, D_MODEL)
        dx, da, db, act, hb, dob, part1 = _ffn_bwd(dx, sv["x_a"], sv["f1"], cl[0:3], norm1_g[l][None],
                                                   wl["ffn1_w_in"], wl["ffn1_w_out"])
        g_ffn1_in = _dw(da, hb, db, name="dw_ffn_in")
        g_ffn1_out = _dw(act, dob, name="dw_ffn_out").reshape(N_DEV, D_FF // N_DEV, D_MODEL)
        pieces = [g_ffn1_in, g_ffn1_out, g_mix_in, g_mix_out, g_ffn2_in, g_ffn2_out]
        last = finish_scatter(pending, g_ffn1_in) if pending is not None else g_ffn1_in
        lands = _place_own(pieces, True, "scatter_grads_own")
        if l > 0:
            handle, token = _exchange_start(pieces, lands, last, True, "scatter_grads_start")
            pending = (l, handle)
        else:
            first_layer = (pieces, lands, last)
        dcond_rows[l] = jnp.concatenate([part1[0:3], part2[0:2], part_mo[0:1], part3[0:3]], axis=0)
        sg = small_grads
        sg["norm1_g"][l] = part1[3]
        sg["norm2_g"][l] = part2[2]
        sg["norm3_g"][l] = part3[3]
        sg["mix_norm_g"][l] = part_mo[1]
        sg["sgu_w"][l] = dw_cat.reshape(CHUNK, SGU_HEADS, CHUNK).transpose(1, 0, 2)
        sg["sgu_b"][l] = dbias[:, ::SGU_HEAD_DIM].T
        sg["pool_w"][l] = _diag_blocks(dpool_dense, len(POOL_WINDOWS))
        sg["pool_scale"][l] = dpool_scale[0]
        sg["conv_w"][l] = dconv8[0:3]
        sg["s5_c_re"][l] = _diag_blocks(dcre_d, S5_GROUPS).transpose(0, 2, 1)
        sg["s5_c_im"][l] = _diag_blocks(dcim_d, S5_GROUPS).transpose(0, 2, 1)
        sg["s5_d"][l] = dd[0]
        sg["s5_glu_w"][l] = dglu_w
        sg["s5_glu_b"][l] = dglu_b[0]
        d_are[l], d_aim[l] = dar8[0], dai8[0]
        d_bbre[l] = _diag_blocks(dbre_d, S5_GROUPS).transpose(0, 2, 1)
        d_bbim[l] = _diag_blocks(dbim_d, S5_GROUPS).transpose(0, 2, 1)
    grad_x = dx

    g_lre, g_lim, g_ldt, g_br, g_bi = _s5_prep_bwd(
        lre3, lim3, ldt3, br3, bi3,
        jnp.stack(d_are).reshape(lg, S5_STATE, 1), jnp.stack(d_aim).reshape(lg, S5_STATE, 1),
        jnp.stack(d_bbre).reshape(lg, S5_STATE, S5_GROUP_CH), jnp.stack(d_bbim).reshape(lg, S5_STATE, S5_GROUP_CH))
    small = {n: (jnp.stack(v) if isinstance(v, list) and v[0] is not None else v) for n, v in small_grads.items()}
    small["s5_lambda_re"] = g_lre.reshape(s5_lambda_re.shape)
    small["s5_lambda_im"] = g_lim.reshape(s5_lambda_im.shape)
    small["s5_log_dt"] = g_ldt.reshape(s5_log_dt.shape)
    small["s5_b_re"] = g_br.reshape(s5_b_re.shape)
    small["s5_b_im"] = g_bi.reshape(s5_b_im.shape)

    small_shapes = [(L, 3, W_GRP) if n == "conv_w" else (L, W_GRP, W_GRP) if n == "s5_glu_w" else W[n].shape
                    for n in SMALL]
    packed = _pack([small[n].reshape(s) for n, s in zip(SMALL, small_shapes)], PACK_COLS)
    (pieces,) = _exchange([packed], True, "scatter_small")
    mine = _sum_parts(pieces, "sum_small")
    (summed,) = _exchange([mine], False, "gather_small_sums")
    small_sum = dict(zip(SMALL, _unpack(summed.reshape(-1), small_shapes)))
    small_sum["conv_w"] = lax.dynamic_slice_in_dim(small_sum["conv_w"], me * conv_cols, conv_cols, axis=2)
    small_sum["s5_glu_w"] = lax.dynamic_slice_in_dim(small_sum["s5_glu_w"], me * glu_rows, glu_rows, axis=1)

    dcond = jnp.stack(dcond_rows).reshape(L * N_ADA, D_MODEL)
    (dcond_g,) = _exchange([dcond], False, "gather_dcond")
    g_pieces, g_lands, g_last = first_layer
    behind = jnp.stack([summed[0, 0, 0], dcond_g[0, 0, 0], g_last[0, 0, 0].astype(F32)])
    handle, _ = _exchange_start(g_pieces, g_lands, behind, True, "scatter_grads_start")
    pending = (0, handle)
    dcond_all = dcond_g.reshape(N_DEV, L, N_ADA * D_MODEL).transpose(1, 0, 2)
    dcond_mine = lax.dynamic_slice_in_dim(dcond_all, me * ncol, ncol, axis=2)
    g_ada_w, g_ada_b = _cond_bwd(c_all.T, dcond_mine, dcond_all)

    grads, deltas, new_m, new_v = {}, {}, {}, {}
    out = _adamw(ada_w.reshape(L * D_MODEL, ncol), g_ada_w.reshape(1, L * D_MODEL, ncol),
                 m_ada_w.reshape(L * D_MODEL, ncol), v_ada_w.reshape(L * D_MODEL, ncol), "adamw_ada_w")
    grads["ada_w"], deltas["ada_w"], new_m["ada_w"], new_v["ada_w"] = (o.reshape(ada_w.shape) for o in out)
    small_names = SMALL + ("ada_b",)
    small_g = dict(small_sum)
    small_g["ada_b"] = g_ada_b.reshape(ada_b.shape)
    shapes = [W[n].shape for n in small_names]
    pw = _pack([W[n] for n in small_names], PACK_COLS)
    rows = pw.shape[0] * pw.shape[1]
    out = _adamw(pw.reshape(rows, PACK_COLS),
                 _pack([small_g[n] for n in small_names], PACK_COLS).reshape(1, rows, PACK_COLS),
                 _pack([M[n] for n in small_names], PACK_COLS).reshape(rows, PACK_COLS),
                 _pack([V[n] for n in small_names], PACK_COLS).reshape(rows, PACK_COLS), "adamw_small")
    for store, o in zip((grads, deltas, new_m, new_v), out):
        store.update(zip(small_names, _unpack(o.reshape(-1), shapes)))
    finish_scatter(pending, out[0])
    for n in BIG:
        res = big_out[n]
        if n in TRANSPOSED:
            res = tuple(r.transpose(0, 2, 1) for r in res)
        grads[n], deltas[n], new_m[n], new_v[n] = res

    return (loss, grad_x[None], *[grads[n] for n in WEIGHTS], *[deltas[n] for n in WEIGHTS],
            *[new_m[n] for n in WEIGHTS], *[new_v[n] for n in WEIGHTS])
```

```python
import functools
import math

import jax
import jax.numpy as jnp
from jax import lax
from jax.experimental import pallas as pl
from jax.experimental.pallas import tpu as pltpu

F32 = jnp.float32
BF16 = jnp.bfloat16

D_MODEL = 1024
SEQ = 2048
DEPTH = 4
N_DEV = 8
W_GRP = 256
CHUNK = 128
SGU_HEADS = 4
SGU_HEAD_DIM = 64
POOL_WINDOWS = (2, 4, 8, 16)
POOL_GROUP_DIM = 64
S5_GROUPS = 16
S5_GROUP_CH = 16
S5_STATE = 64
S5_LANES = S5_GROUPS * S5_STATE
P_IN = 1792
D_FF = 2816
FF_PIECE = 2 * D_FF // N_DEV
N_FF_CHUNK = D_FF // FF_PIECE
N_ADA = 9
EPS = 1e-6
ADAM_LR = 0.001
ADAM_B1 = 0.9
ADAM_B2 = 0.999
ADAM_EPS = 1e-08
ADAM_WD = 0.01
ADAM_STEP = 10

SUBLANES = 8
LANES = 128
VMEM_LIMIT = 56 * 1024 * 1024
TOKEN_TILE = 512
ROW_SUBTILE = 256
HIGHEST = lax.Precision.HIGHEST
MESH_AXES = ("x", "y", "c")

_GELU_C = math.sqrt(2.0 / math.pi)
_GELU_A = 0.044715


def _params(*sem):
    return pltpu.CompilerParams(dimension_semantics=tuple(sem) if sem else None, vmem_limit_bytes=VMEM_LIMIT)


def _dot(a, b):
    return jnp.dot(a, b, preferred_element_type=F32)


def _dot_nt(a, b):
    return lax.dot_general(a, b, (((1,), (1,)), ((), ())), preferred_element_type=F32)


def _dot_tn(a, b):
    return lax.dot_general(a, b, (((0,), (0,)), ((), ())), preferred_element_type=F32)


def _dot_hi(a, b):
    return jnp.dot(a, b, preferred_element_type=F32, precision=HIGHEST)


def _sigmoid(x):
    return 1.0 / (1.0 + jnp.exp(-x))


def _gelu(x):
    return 0.5 * x * (1.0 + jnp.tanh(_GELU_C * (x + _GELU_A * x * x * x)))


def _gelu_grad(x):
    t = jnp.tanh(_GELU_C * (x + _GELU_A * x * x * x))
    return 0.5 * (1.0 + t) + 0.5 * x * (1.0 - t * t) * (_GELU_C * (1.0 + 3.0 * _GELU_A * x * x))


def _rms(x):
    r = lax.rsqrt(jnp.mean(x * x, axis=-1, keepdims=True) + EPS)
    return x * r, r


def _rms_bwd(xhat, r, dxhat):
    return r * (dxhat - xhat * jnp.mean(dxhat * xhat, axis=-1, keepdims=True))


def _sum0(x):
    return jnp.sum(x, axis=0, keepdims=True)


def _me():
    return 4 * lax.axis_index("x") + 2 * lax.axis_index("y") + lax.axis_index("c")


def _exchange(srcs, scatter, name):
    n = len(srcs)
    out_shapes = []
    for s in srcs:
        piece = s.shape[1:] if scatter else s.shape
        out_shapes.append(jax.ShapeDtypeStruct((N_DEV,) + tuple(piece), s.dtype))

    def body(*refs):
        ins, outs = refs[:n], refs[n:2 * n]
        send_sems, recv_sems, local_sems = refs[2 * n:]
        x, y, c = lax.axis_index("x"), lax.axis_index("y"), lax.axis_index("c")
        me = 4 * x + 2 * y + c

        def src_of(i, dev):
            return ins[i].at[dev] if scatter else ins[i]

        local = [pltpu.make_async_copy(src_of(i, me), outs[i].at[me], local_sems.at[i]) for i in range(n)]
        for cp in local:
            cp.start()
        sends, recvs = [], []
        for k in range(1, N_DEV):
            px = 1 - x if (k >> 2) & 1 else x
            py = 1 - y if (k >> 1) & 1 else y
            pc = 1 - c if k & 1 else c
            peer = 4 * px + 2 * py + pc
            for i in range(n):
                sends.append(pltpu.make_async_remote_copy(
                    src_ref=src_of(i, peer), dst_ref=outs[i].at[me],
                    send_sem=send_sems.at[k - 1, i], recv_sem=recv_sems.at[k - 1, i],
                    device_id=(px, py, pc), device_id_type=pl.DeviceIdType.MESH))
                recvs.append(pltpu.make_async_remote_copy(
                    src_ref=src_of(i, peer), dst_ref=outs[i].at[peer],
                    send_sem=send_sems.at[k - 1, i], recv_sem=recv_sems.at[k - 1, i],
                    device_id=(px, py, pc), device_id_type=pl.DeviceIdType.MESH))
        for cp in sends:
            cp.start()
        for cp in recvs:
            cp.wait_recv()
        for cp in sends:
            cp.wait_send()
        for cp in local:
            cp.wait()

    hbm = pl.BlockSpec(memory_space=pltpu.HBM)
    return pl.pallas_call(
        body, name=name, out_shape=out_shapes,
        in_specs=[hbm] * n, out_specs=[hbm] * n,
        scratch_shapes=[pltpu.SemaphoreType.DMA((N_DEV - 1, n)), pltpu.SemaphoreType.DMA((N_DEV - 1, n)),
                        pltpu.SemaphoreType.DMA((n,))],
    )(*srcs)


ALL_PEERS = tuple(range(1, N_DEV))
OTHER_CHIPS = (2, 4, 6)


def _peers(which):
    x, y, c = lax.axis_index("x"), lax.axis_index("y"), lax.axis_index("c")
    out = []
    for k in which:
        px = 1 - x if (k >> 2) & 1 else x
        py = 1 - y if (k >> 1) & 1 else y
        pc = 1 - c if k & 1 else c
        out.append(((px, py, pc), 4 * px + 2 * py + pc))
    return out


def _split_copies(ins, lands, send_sems, recv_sems, scatter, with_recvs, which):
    me = _me()
    sends, recvs = [], []
    for j, (dev, peer) in enumerate(_peers(which)):
        for i in range(len(ins)):
            src = ins[i].at[peer] if scatter else ins[i]
            slot = j * len(ins) + i
            sems = dict(send_sem=send_sems.at[slot], recv_sem=recv_sems.at[slot],
                        device_id=dev, device_id_type=pl.DeviceIdType.MESH)
            sends.append(pltpu.make_async_remote_copy(src_ref=src, dst_ref=lands[i].at[me], **sems))
            if with_recvs:
                recvs.append(pltpu.make_async_remote_copy(src_ref=src, dst_ref=lands[i].at[peer], **sems))
    return sends, recvs


_HBM = pl.BlockSpec(memory_space=pltpu.HBM)
_SEM = pl.BlockSpec(memory_space=pltpu.SEMAPHORE)
_EFFECT = pltpu.SideEffectType.DATAFLOW_SIDE_EFFECTING


def _place_own(srcs, scatter, name):
    n = len(srcs)
    halves = 2
    out_shapes, in_specs, out_specs = [], [], []
    for s in srcs:
        r, c = s.shape[-2:]
        out_shapes.append(jax.ShapeDtypeStruct((N_DEV, r, c), s.dtype))
        if scatter:
            in_specs.append(pl.BlockSpec((1, r // halves, c), lambda i, me: (me[0], i, 0)))
        else:
            in_specs.append(pl.BlockSpec((r // halves, c), lambda i, me: (i, 0)))
        out_specs.append(pl.BlockSpec((1, r // halves, c), lambda i, me: (me[0], i, 0)))

    def body(me_ref, *refs):
        for i in range(n):
            refs[n + i][0] = refs[i][0] if scatter else refs[i][...]

    return pl.pallas_call(
        body, name=name, out_shape=out_shapes,
        grid_spec=pltpu.PrefetchScalarGridSpec(num_scalar_prefetch=1, grid=(halves,), in_specs=in_specs,
                                               out_specs=out_specs),
        compiler_params=_params("arbitrary"),
    )(_me().reshape(1).astype(jnp.int32), *srcs)


def _exchange_start(srcs, lands, after, scatter, name, which=ALL_PEERS):
    n = len(srcs)

    def body(*refs):
        ins, land_in = refs[:n], refs[n:2 * n]
        send_sems, recv_sems = refs[2 * n + 1], refs[2 * n + 2]
        token = refs[-1]
        sends, _ = _split_copies(ins, land_in, send_sems, recv_sems, scatter, False, which)
        for cp in sends:
            cp.start()
        token[...] = jnp.zeros_like(token)

    sem = pltpu.SemaphoreType.DMA((len(which) * n,))
    out = pl.pallas_call(
        body, name=name,
        out_shape=(sem, sem, *[pltpu.HBM(s.shape, s.dtype) for s in srcs], *[pltpu.HBM(s.shape, s.dtype) for s in lands],
                   jax.ShapeDtypeStruct((SUBLANES, LANES), F32)),
        in_specs=[_HBM] * (2 * n) + [pl.BlockSpec(memory_space=pl.ANY)],
        out_specs=(_SEM, _SEM, *[_HBM] * (2 * n), pl.BlockSpec(memory_space=pltpu.VMEM)),
        input_output_aliases={i: 2 + i for i in range(2 * n)},
        compiler_params=pltpu.CompilerParams(has_side_effects=_EFFECT),
    )(*srcs, *lands, after)
    return (out[0], out[1], out[2:2 + n], out[2 + n:2 + 2 * n]), out[-1]


def _exchange_wait(handle, after, scatter, name, which=ALL_PEERS):
    send_sems, recv_sems, srcs, lands = handle
    n = len(srcs)

    def body(*refs):
        ins, land_in = refs[:n], refs[n:2 * n]
        sends, recvs = _split_copies(ins, land_in, refs[2 * n], refs[2 * n + 1], scatter, True, which)
        for cp in sends:
            cp.wait_send()
        for cp in recvs:
            cp.wait_recv()

    out = pl.pallas_call(
        body, name=name,
        out_shape=(*[pltpu.HBM(s.shape, s.dtype) for s in srcs], *[pltpu.HBM(s.shape, s.dtype) for s in lands]),
        in_specs=[_HBM] * (2 * n) + [_SEM, _SEM, pl.BlockSpec(memory_space=pl.ANY)],
        out_specs=tuple([_HBM] * (2 * n)),
        input_output_aliases={i: i for i in range(2 * n)},
        compiler_params=pltpu.CompilerParams(has_side_effects=_EFFECT),
    )(*srcs, *lands, send_sems, recv_sems, after)
    return out[n:]


def _gather_sibling(lands, name):
    n = len(lands)
    chips = ((0, 0), (0, 1), (1, 0), (1, 1))

    def body(*refs):
        ins, outs = refs[:n], refs[n:2 * n]
        send_sems, recv_sems = refs[2 * n], refs[2 * n + 1]
        x, y, c = lax.axis_index("x"), lax.axis_index("y"), lax.axis_index("c")
        sends, recvs = [], []
        for j, (bx, by) in enumerate(chips):
            chip = 4 * (1 - x if bx else x) + 2 * (1 - y if by else y)
            for i in range(n):
                sems = dict(send_sem=send_sems.at[j * n + i], recv_sem=recv_sems.at[j * n + i],
                            device_id=(x, y, 1 - c), device_id_type=pl.DeviceIdType.MESH)
                sends.append(pltpu.make_async_remote_copy(src_ref=ins[i].at[chip + c], dst_ref=outs[i].at[chip + c], **sems))
                recvs.append(pltpu.make_async_remote_copy(src_ref=ins[i].at[chip + c], dst_ref=outs[i].at[chip + 1 - c],
                                                          **sems))
        for cp in sends:
            cp.start()
        for cp in recvs:
            cp.wait_recv()
        for cp in sends:
            cp.wait_send()

    return pl.pallas_call(
        body, name=name, out_shape=[jax.ShapeDtypeStruct(a.shape, a.dtype) for a in lands],
        in_specs=[_HBM] * n, out_specs=[_HBM] * n,
        scratch_shapes=[pltpu.SemaphoreType.DMA((len(chips) * n,)), pltpu.SemaphoreType.DMA((len(chips) * n,))],
        input_output_aliases={i: i for i in range(n)},
    )(*lands)


def _cond_fwd(c_all, ada_w, ada_b_mine):
    ncol = ada_w.shape[-1]

    def body(c_ref, w_ref, b_ref, o_ref):
        c = c_ref[...]
        ca = (c * _sigmoid(c)).astype(BF16)
        o_ref[0] = _dot(ca, w_ref[0].astype(BF16)) + b_ref[0]

    return pl.pallas_call(
        body, name="cond_fwd", grid=(DEPTH,),
        out_shape=jax.ShapeDtypeStruct((DEPTH, N_DEV, ncol), F32),
        in_specs=[pl.BlockSpec((N_DEV, D_MODEL), lambda l: (0, 0)),
                  pl.BlockSpec((1, D_MODEL, ncol), lambda l: (l, 0, 0)),
                  pl.BlockSpec((1, 1, ncol), lambda l: (l, 0, 0))],
        out_specs=pl.BlockSpec((1, N_DEV, ncol), lambda l: (l, 0, 0)),
        compiler_params=_params("arbitrary"),
    )(c_all, ada_w, ada_b_mine)


def _cond_bwd(c_all_t, dcond_mine, dcond_all):
    ncol = dcond_mine.shape[-1]
    nall = dcond_all.shape[-1]

    def body(ct_ref, d_ref, da_ref, gw_ref, gb_ref):
        ct = ct_ref[...]
        ct = ct * _sigmoid(ct)
        d = d_ref[0]
        acc = ct[:, 0:1] * d[0:1, :]
        for b in range(1, N_DEV):
            acc = acc + ct[:, b:b + 1] * d[b:b + 1, :]
        gw_ref[0] = acc
        gb_ref[0] = _sum0(da_ref[0])

    return pl.pallas_call(
        body, name="cond_bwd", grid=(DEPTH,),
        out_shape=(jax.ShapeDtypeStruct((DEPTH, D_MODEL, ncol), F32), jax.ShapeDtypeStruct((DEPTH, 1, nall), F32)),
        in_specs=[pl.BlockSpec((D_MODEL, N_DEV), lambda l: (0, 0)),
                  pl.BlockSpec((1, N_DEV, ncol), lambda l: (l, 0, 0)),
                  pl.BlockSpec((1, N_DEV, nall), lambda l: (l, 0, 0))],
        out_specs=(pl.BlockSpec((1, D_MODEL, ncol), lambda l: (l, 0, 0)),
                   pl.BlockSpec((1, 1, nall), lambda l: (l, 0, 0))),
        compiler_params=_params("arbitrary"),
    )(c_all_t, dcond_mine, dcond_all)


def _modnorm(x, g, shift, scale):
    xhat, r = _rms(x)
    return (xhat * g) * (1.0 + scale) + shift, xhat, r


def _modnorm_bwd(xhat, r, g, scale, dh):
    n = xhat * g
    dn = dh * (1.0 + scale)
    dx = _rms_bwd(xhat, r, dn * g)
    return dx, _sum0(dh), _sum0(dh * n), _sum0(dn * xhat)


def _row_spec(rows):
    return pl.BlockSpec((rows, D_MODEL), lambda *_: (0, 0))


def _ffn_fwd(x, cond3, g, w_in_g, w_out_g):
    tm = TOKEN_TILE
    last = N_FF_CHUNK - 1

    def body(x_ref, cond_ref, g_ref, wa_ref, wb_ref, wo_ref, xo_ref, f_ref, h_scr, acc_scr):
        j = pl.program_id(1)

        @pl.when(j == 0)
        def _():
            h, _, _ = _modnorm(x_ref[...], g_ref[...], cond_ref[0:1, :], cond_ref[1:2, :])
            h_scr[...] = h.astype(BF16)
            acc_scr[...] = jnp.zeros_like(acc_scr)

        wa, wb, wo = wa_ref[0], wb_ref[0], wo_ref[0]
        for r in range(tm // ROW_SUBTILE):
            rows = slice(r * ROW_SUBTILE, (r + 1) * ROW_SUBTILE)
            h = h_scr[rows, :]
            a = _dot_nt(h, wa)
            b = _dot_nt(h, wb)
            act = (a * _sigmoid(a)) * b
            acc_scr[rows, :] += _dot(act.astype(BF16), wo)

        @pl.when(j == last)
        def _():
            f = acc_scr[...]
            f_ref[...] = f
            xo_ref[...] = x_ref[...] + (0.5 * cond_ref[2:3, :]) * f

    tok = pl.BlockSpec((tm, D_MODEL), lambda i, j: (i, 0))
    return pl.pallas_call(
        body, name="ffn_fwd", grid=(SEQ // tm, N_FF_CHUNK),
        out_shape=(jax.ShapeDtypeStruct((SEQ, D_MODEL), F32), jax.ShapeDtypeStruct((SEQ, D_MODEL), F32)),
        in_specs=[tok, _row_spec(3), _row_spec(1),
                  pl.BlockSpec((1, FF_PIECE, D_MODEL), lambda i, j: (j, 0, 0)),
                  pl.BlockSpec((1, FF_PIECE, D_MODEL), lambda i, j: (j + N_FF_CHUNK, 0, 0)),
                  pl.BlockSpec((1, FF_PIECE, D_MODEL), lambda i, j: (j, 0, 0))],
        out_specs=(tok, tok),
        scratch_shapes=[pltpu.VMEM((tm, D_MODEL), BF16), pltpu.VMEM((tm, D_MODEL), F32)],
        compiler_params=_params("arbitrary", "arbitrary"),
    )(x, cond3, g, w_in_g, w_in_g, w_out_g)


def _ffn_bwd(dy, x, f, cond3, g, w_in_g, w_out_g):
    tm = TOKEN_TILE
    last = N_FF_CHUNK - 1

    def body(dy_ref, x_ref, f_ref, cond_ref, g_ref, wa_ref, wb_ref, wo_ref,
             dx_ref, da_ref, db_ref, act_ref, h_ref, do_ref, part_ref, h_scr, do_scr, dh_scr):
        i, j = pl.program_id(0), pl.program_id(1)

        @pl.when(j == 0)
        def _():
            h, _, _ = _modnorm(x_ref[...], g_ref[...], cond_ref[0:1, :], cond_ref[1:2, :])
            hb = h.astype(BF16)
            h_scr[...] = hb
            h_ref[...] = hb
            do = ((0.5 * cond_ref[2:3, :]) * dy_ref[...]).astype(BF16)
            do_scr[...] = do
            do_ref[...] = do
            dh_scr[...] = jnp.zeros_like(dh_scr)

        @pl.when((i == 0) & (j == 0))
        def _():
            part_ref[...] = jnp.zeros_like(part_ref)

        wa, wb, wo = wa_ref[0], wb_ref[0], wo_ref[0]
        for r in range(tm // ROW_SUBTILE):
            rows = slice(r * ROW_SUBTILE, (r + 1) * ROW_SUBTILE)
            h = h_scr[rows, :]
            do = do_scr[rows, :]
            a = _dot_nt(h, wa)
            b = _dot_nt(h, wb)
            dact = _dot_nt(do, wo)
            sig = _sigmoid(a)
            s = a * sig
            da = (dact * b * (sig * (1.0 + a * (1.0 - sig)))).astype(BF16)
            db = (dact * s).astype(BF16)
            da_ref[0, rows, :] = da
            db_ref[0, rows, :] = db
            act_ref[0, rows, :] = (s * b).astype(BF16)
            dh_scr[rows, :] += _dot(da, wa) + _dot(db, wb)

        @pl.when(j == last)
        def _():
            dyv = dy_ref[...]
            xhat, r = _rms(x_ref[...])
            dx, dshift, dscale, dg = _modnorm_bwd(xhat, r, g_ref[...], cond_ref[1:2, :], dh_scr[...])
            dx_ref[...] = dyv + dx
            part_ref[0:1, :] += dshift
            part_ref[1:2, :] += dscale
            part_ref[2:3, :] += _sum0(0.5 * dyv * f_ref[...])
            part_ref[3:4, :] += dg

    tok = pl.BlockSpec((tm, D_MODEL), lambda i, j: (i, 0))
    chunk = pl.BlockSpec((1, tm, FF_PIECE), lambda i, j: (j, i, 0))
    chunk_shape = jax.ShapeDtypeStruct((N_FF_CHUNK, SEQ, FF_PIECE), BF16)
    return pl.pallas_call(
        body, name="ffn_bwd", grid=(SEQ // tm, N_FF_CHUNK),
        out_shape=(jax.ShapeDtypeStruct((SEQ, D_MODEL), F32), chunk_shape, chunk_shape, chunk_shape,
                   jax.ShapeDtypeStruct((SEQ, D_MODEL), BF16), jax.ShapeDtypeStruct((SEQ, D_MODEL), BF16),
                   jax.ShapeDtypeStruct((SUBLANES, D_MODEL), F32)),
        in_specs=[tok, tok, tok, _row_spec(3), _row_spec(1),
                  pl.BlockSpec((1, FF_PIECE, D_MODEL), lambda i, j: (j, 0, 0)),
                  pl.BlockSpec((1, FF_PIECE, D_MODEL), lambda i, j: (j + N_FF_CHUNK, 0, 0)),
                  pl.BlockSpec((1, FF_PIECE, D_MODEL), lambda i, j: (j, 0, 0))],
        out_specs=(tok, chunk, chunk, chunk, tok, tok, _row_spec(SUBLANES)),
        scratch_shapes=[pltpu.VMEM((tm, D_MODEL), BF16), pltpu.VMEM((tm, D_MODEL), BF16),
                        pltpu.VMEM((tm, D_MODEL), F32)],
        compiler_params=_params("arbitrary", "arbitrary"),
    )(dy, x, f, cond3, g, w_in_g, w_in_g, w_out_g)


def _dw(lhs_a, rhs, lhs_b=None, name="dw"):
    pa, s, m = lhs_a.shape
    nn = rhs.shape[-1]
    pb = 0 if lhs_b is None else lhs_b.shape[0]
    two = lhs_b is not None

    def body(*refs):
        if two:
            a_ref, b_ref, r_ref, o_ref = refs
            p = pl.program_id(0)

            @pl.when(p < pa)
            def _():
                o_ref[0] = _dot_tn(a_ref[0], r_ref[...]).astype(BF16)

            @pl.when(p >= pa)
            def _():
                o_ref[0] = _dot_tn(b_ref[0], r_ref[...]).astype(BF16)
        else:
            a_ref, r_ref, o_ref = refs
            o_ref[0] = _dot_tn(a_ref[0], r_ref[...]).astype(BF16)

    if two:
        in_specs = [pl.BlockSpec((1, s, m), lambda p: (jnp.minimum(p, pa - 1), 0, 0)),
                    pl.BlockSpec((1, s, m), lambda p: (jnp.maximum(p - pa, 0), 0, 0))]
        args = (lhs_a, lhs_b, rhs)
    else:
        in_specs = [pl.BlockSpec((1, s, m), lambda p: (p, 0, 0))]
        args = (lhs_a, rhs)
    in_specs.append(pl.BlockSpec((s, nn), lambda p: (0, 0)))
    return pl.pallas_call(
        body, name=name, grid=(pa + pb,),
        out_shape=jax.ShapeDtypeStruct((pa + pb, m, nn), BF16),
        in_specs=in_specs, out_specs=pl.BlockSpec((1, m, nn), lambda p: (p, 0, 0)),
        compiler_params=_params("arbitrary"),
    )(*args)


def _mix_in_fwd(x, cond2, g, w):
    tm = TOKEN_TILE

    def body(x_ref, cond_ref, g_ref, w_ref, z_ref):
        h, _, _ = _modnorm(x_ref[...], g_ref[...], cond_ref[0:1, :], cond_ref[1:2, :])
        z_ref[...] = _dot_nt(h.astype(BF16), w_ref[...])

    return pl.pallas_call(
        body, name="mix_in_fwd", grid=(SEQ // tm,),
        out_shape=jax.ShapeDtypeStruct((SEQ, P_IN), F32),
        in_specs=[pl.BlockSpec((tm, D_MODEL), lambda i: (i, 0)), _row_spec(2), _row_spec(1),
                  pl.BlockSpec((P_IN, D_MODEL), lambda i: (0, 0))],
        out_specs=pl.BlockSpec((tm, P_IN), lambda i: (i, 0)),
        compiler_params=_params("arbitrary"),
    )(x, cond2, g, w)


MIX_SLABS = ((0, 2 * W_GRP), (2 * W_GRP, 3 * W_GRP), (3 * W_GRP, 6 * W_GRP), (6 * W_GRP, 7 * W_GRP))


def _mix_in_bwd(dzs, x, dy, cond2, g, w):
    tm = TOKEN_TILE

    def body(dza_ref, dzb_ref, dzc_ref, dzd_ref, x_ref, dy_ref, cond_ref, g_ref, w_ref, dx_ref, h_ref, dzo_ref, part_ref):
        i = pl.program_id(0)

        @pl.when(i == 0)
        def _():
            part_ref[...] = jnp.zeros_like(part_ref)

        h, xhat, r = _modnorm(x_ref[...], g_ref[...], cond_ref[0:1, :], cond_ref[1:2, :])
        h_ref[...] = h.astype(BF16)
        dh = None
        for (lo, hi), d_ref in zip(MIX_SLABS, (dza_ref, dzb_ref, dzc_ref, dzd_ref)):
            dzb = d_ref[...].astype(BF16)
            dzo_ref[:, lo:hi] = dzb
            t = _dot(dzb, w_ref[lo:hi, :])
            dh = t if dh is None else dh + t
        dx, dshift, dscale, dg = _modnorm_bwd(xhat, r, g_ref[...], cond_ref[1:2, :], dh)
        dx_ref[...] = dy_ref[...] + dx
        part_ref[0:1, :] += dshift
        part_ref[1:2, :] += dscale
        part_ref[2:3, :] += dg

    tok = pl.BlockSpec((tm, D_MODEL), lambda i: (i, 0))
    ztok = pl.BlockSpec((tm, P_IN), lambda i: (i, 0))
    slabs = [pl.BlockSpec((tm, hi - lo), lambda i: (i, 0)) for lo, hi in MIX_SLABS]
    return pl.pallas_call(
        body, name="mix_in_bwd", grid=(SEQ // tm,),
        out_shape=(jax.ShapeDtypeStruct((SEQ, D_MODEL), F32), jax.ShapeDtypeStruct((SEQ, D_MODEL), BF16),
                   jax.ShapeDtypeStruct((SEQ, P_IN), BF16), jax.ShapeDtypeStruct((SUBLANES, D_MODEL), F32)),
        in_specs=[*slabs, tok, tok, _row_spec(2), _row_spec(1), pl.BlockSpec((P_IN, D_MODEL), lambda i: (0, 0))],
        out_specs=(tok, tok, ztok, _row_spec(SUBLANES)),
        compiler_params=_params("arbitrary"),
    )(*dzs, x, dy, cond2, g, w)


def _group_norm(ys, g_ref):
    out = []
    for k, y in enumerate(ys):
        yhat, r = _rms(y)
        out.append((yhat, r, g_ref[:, k * W_GRP:(k + 1) * W_GRP]))
    return out


def _mix_out_fwd(x, ys, g, gate, w):
    tm = TOKEN_TILE

    def body(x_ref, ya_ref, yb_ref, yc_ref, yd_ref, g_ref, gate_ref, w_ref, xo_ref):
        acc = None
        for k, (yhat, _, gk) in enumerate(_group_norm([r[...] for r in (ya_ref, yb_ref, yc_ref, yd_ref)], g_ref)):
            t = _dot((yhat * gk).astype(BF16), w_ref[k * W_GRP:(k + 1) * W_GRP, :])
            acc = t if acc is None else acc + t
        xo_ref[...] = x_ref[...] + gate_ref[...] * acc

    tok = pl.BlockSpec((tm, D_MODEL), lambda i: (i, 0))
    ytok = pl.BlockSpec((tm, W_GRP), lambda i: (i, 0))
    return pl.pallas_call(
        body, name="mix_out_fwd", grid=(SEQ // tm,),
        out_shape=jax.ShapeDtypeStruct((SEQ, D_MODEL), F32),
        in_specs=[tok, ytok, ytok, ytok, ytok, _row_spec(1), _row_spec(1),
                  pl.BlockSpec((D_MODEL, D_MODEL), lambda i: (0, 0))],
        out_specs=tok, compiler_params=_params("arbitrary"),
    )(x, *ys, g, gate, w)


def _mix_out_bwd(dy, ys, g, gate, w):
    tm = TOKEN_TILE

    def body(dy_ref, ya_ref, yb_ref, yc_ref, yd_ref, g_ref, gate_ref, w_ref,
             da_ref, db_ref, dc_ref, dd_ref, yn_ref, dmo_ref, part_ref):
        i = pl.program_id(0)

        @pl.when(i == 0)
        def _():
            part_ref[...] = jnp.zeros_like(part_ref)

        dyv = dy_ref[...]
        dmo = (gate_ref[...] * dyv).astype(BF16)
        dmo_ref[...] = dmo
        dyn = _dot_nt(dmo, w_ref[...])
        norms = _group_norm([r[...] for r in (ya_ref, yb_ref, yc_ref, yd_ref)], g_ref)
        mo = None
        for k, ((yhat, r, gk), o_ref) in enumerate(zip(norms, (da_ref, db_ref, dc_ref, dd_ref))):
            sl = slice(k * W_GRP, (k + 1) * W_GRP)
            ynk = (yhat * gk).astype(BF16)
            yn_ref[:, sl] = ynk
            t = _dot(ynk, w_ref[sl, :])
            mo = t if mo is None else mo + t
            dk = dyn[:, sl]
            o_ref[...] = _rms_bwd(yhat, r, dk * gk)
            part_ref[1:2, sl] += _sum0(dk * yhat)
        part_ref[0:1, :] += _sum0(dyv * mo)

    tok = pl.BlockSpec((tm, D_MODEL), lambda i: (i, 0))
    ytok = pl.BlockSpec((tm, W_GRP), lambda i: (i, 0))
    ysh = jax.ShapeDtypeStruct((SEQ, W_GRP), F32)
    return pl.pallas_call(
        body, name="mix_out_bwd", grid=(SEQ // tm,),
        out_shape=(ysh, ysh, ysh, ysh, jax.ShapeDtypeStruct((SEQ, D_MODEL), BF16),
                   jax.ShapeDtypeStruct((SEQ, D_MODEL), BF16), jax.ShapeDtypeStruct((SUBLANES, D_MODEL), F32)),
        in_specs=[tok, ytok, ytok, ytok, ytok, _row_spec(1), _row_spec(1),
                  pl.BlockSpec((D_MODEL, D_MODEL), lambda i: (0, 0))],
        out_specs=(ytok, ytok, ytok, ytok, tok, tok, _row_spec(SUBLANES)),
        compiler_params=_params("arbitrary"),
    )(dy, *ys, g, gate, w)


def _shift_down(v, k, rows):
    return jnp.where(rows >= k, pltpu.roll(v, k, axis=0), 0.0)


def _shift_up(v, k, rows):
    n = v.shape[0]
    return jnp.where(rows < n - k, pltpu.roll(v, n - k, axis=0), 0.0)


def _zslab(width, index):
    return pl.BlockSpec((SEQ, width), lambda *_: (0, index))


def _full(shape):
    return pl.BlockSpec(shape, lambda *_: (0,) * len(shape))


def _head_avg():
    r = lax.broadcasted_iota(jnp.int32, (W_GRP, W_GRP), 0) // SGU_HEAD_DIM
    c = lax.broadcasted_iota(jnp.int32, (W_GRP, W_GRP), 1) // SGU_HEAD_DIM
    return jnp.where(r == c, 1.0 / SGU_HEAD_DIM, 0.0).astype(F32)


def _sgu_norm(za):
    z = _gelu(za)
    u, v = z[:, :W_GRP], z[:, W_GRP:]
    avg = _head_avg()
    vc = v - _dot_hi(v, avg)
    rstd = lax.rsqrt(_dot_hi(vc * vc, avg) + EPS)
    return u, vc * rstd, rstd


def _sgu_masked_w(w_ref):
    t = lax.broadcasted_iota(jnp.int32, (CHUNK, CHUNK), 0)
    s = lax.broadcasted_iota(jnp.int32, (CHUNK, CHUNK), 1)
    tril = t >= s
    return [jnp.where(tril, w_ref[:, h * CHUNK:(h + 1) * CHUNK], 0.0).astype(BF16) for h in range(SGU_HEADS)]


def _head_of_lane():
    return lax.broadcasted_iota(jnp.int32, (CHUNK, W_GRP), 1) // SGU_HEAD_DIM


def _sgu_fwd(z, w_cat, bias):
    def body(z_ref, w_ref, b_ref, y_ref, vn_scr, u_scr):
        u, vn, _ = _sgu_norm(z_ref[...])
        vn_scr[...] = vn.astype(BF16)
        u_scr[...] = u
        ws = _sgu_masked_w(w_ref)
        head = _head_of_lane()
        bias_v = b_ref[...]

        def chunk(n, carry):
            rows = pl.ds(pl.multiple_of(n * CHUNK, CHUNK), CHUNK)
            vb = vn_scr[rows, :]
            mixed = bias_v
            for h in range(SGU_HEADS):
                mixed = mixed + jnp.where(head == h, _dot(ws[h], vb), 0.0)
            y_ref[rows, :] = u_scr[rows, :] * mixed
            return carry

        lax.fori_loop(0, SEQ // CHUNK, chunk, 0)

    return pl.pallas_call(
        body, name="sgu_fwd", grid=(1,),
        out_shape=jax.ShapeDtypeStruct((SEQ, W_GRP), F32),
        in_specs=[_zslab(2 * W_GRP, 0), _full((CHUNK, SGU_HEADS * CHUNK)), _full((CHUNK, W_GRP))],
        out_specs=_full((SEQ, W_GRP)),
        scratch_shapes=[pltpu.VMEM((SEQ, W_GRP), BF16), pltpu.VMEM((SEQ, W_GRP), F32)],
        compiler_params=_params("arbitrary"),
    )(z, w_cat, bias)


def _sgu_bwd(z, dy, w_cat, bias):
    def body(z_ref, dy_ref, w_ref, b_ref, dz_ref, dw_ref, db_ref, vn_scr, u_scr, dvn_scr, du_scr):
        za = z_ref[...]
        u, vn, rstd = _sgu_norm(za)
        vn_scr[...] = vn.astype(BF16)
        u_scr[...] = u
        ws = _sgu_masked_w(w_ref)
        head = _head_of_lane()
        bias_v = b_ref[...]

        def chunk(n, carry):
            dws, dbias = carry
            rows = pl.ds(pl.multiple_of(n * CHUNK, CHUNK), CHUNK)
            vb = vn_scr[rows, :]
            mixed = bias_v
            for h in range(SGU_HEADS):
                mixed = mixed + jnp.where(head == h, _dot(ws[h], vb), 0.0)
            dyc = dy_ref[rows, :]
            du_scr[rows, :] = dyc * mixed
            dmixed = dyc * u_scr[rows, :]
            dvn = jnp.zeros((CHUNK, W_GRP), F32)
            new_dws = []
            for h in range(SGU_HEADS):
                dm = jnp.where(head == h, dmixed, 0.0).astype(BF16)
                new_dws.append(dws[h] + _dot_nt(dm, vb))
                dvn = dvn + _dot_tn(ws[h], dm)
            dvn_scr[rows, :] = dvn
            return tuple(new_dws), dbias + dmixed

        zero_w = tuple(jnp.zeros((CHUNK, CHUNK), F32) for _ in range(SGU_HEADS))
        dws, dbias = lax.fori_loop(0, SEQ // CHUNK, chunk, (zero_w, jnp.zeros((CHUNK, W_GRP), F32)))
        t = lax.broadcasted_iota(jnp.int32, (CHUNK, CHUNK), 0)
        s = lax.broadcasted_iota(jnp.int32, (CHUNK, CHUNK), 1)
        for h in range(SGU_HEADS):
            dw_ref[:, h * CHUNK:(h + 1) * CHUNK] = jnp.where(t >= s, dws[h], 0.0)
        avg = _head_avg()
        db_ref[...] = _dot_hi(dbias, avg) * float(SGU_HEAD_DIM)
        dvn = dvn_scr[...]
        dv = rstd * (dvn - _dot_hi(dvn, avg) - vn * _dot_hi(dvn * vn, avg))
        gg = _gelu_grad(za)
        dz_ref[:, :W_GRP] = gg[:, :W_GRP] * du_scr[...]
        dz_ref[:, W_GRP:] = gg[:, W_GRP:] * dv

    return pl.pallas_call(
        body, name="sgu_bwd", grid=(1,),
        out_shape=(jax.ShapeDtypeStruct((SEQ, 2 * W_GRP), F32), jax.ShapeDtypeStruct((CHUNK, SGU_HEADS * CHUNK), F32),
                   jax.ShapeDtypeStruct((CHUNK, W_GRP), F32)),
        in_specs=[_zslab(2 * W_GRP, 0), _full((SEQ, W_GRP)), _full((CHUNK, SGU_HEADS * CHUNK)), _full((CHUNK, W_GRP))],
        out_specs=(_full((SEQ, 2 * W_GRP)), _full((CHUNK, SGU_HEADS * CHUNK)), _full((CHUNK, W_GRP))),
        scratch_shapes=[pltpu.VMEM((SEQ, W_GRP), BF16), pltpu.VMEM((SEQ, W_GRP), F32),
                        pltpu.VMEM((SEQ, W_GRP), F32), pltpu.VMEM((SEQ, W_GRP), F32)],
        compiler_params=_params("arbitrary"),
    )(z, dy, w_cat, bias)


def _pool_window_of_lane(shape):
    grp = lax.broadcasted_iota(jnp.int32, shape, 1) // POOL_GROUP_DIM
    win = jnp.full(shape, POOL_WINDOWS[0], jnp.int32)
    for k in range(1, len(POOL_WINDOWS)):
        win = jnp.where(grp == k, POOL_WINDOWS[k], win)
    return grp, win


def _pool_select(levels, grp):
    out = levels[0]
    for k in range(1, len(levels)):
        out = jnp.where(grp == k, levels[k], out)
    return out


def _pool_p(z):
    shape = z.shape
    rows = lax.broadcasted_iota(jnp.int32, shape, 0)
    grp, win = _pool_window_of_lane(shape)
    levels, s, k = [], z, 1
    for _ in POOL_WINDOWS:
        s = s + _shift_down(s, k, rows)
        levels.append(s)
        k *= 2
    inv = 1.0 / jnp.minimum(rows + 1, win).astype(F32)
    return _pool_select(levels, grp) * inv - z, inv, rows, grp


def _pool_fwd(z, w_bd, scale):
    def body(z_ref, w_ref, s_ref, y_ref):
        p, _, _, _ = _pool_p(z_ref[...])
        y_ref[...] = _dot(p.astype(BF16), w_ref[...]) * s_ref[...]

    return pl.pallas_call(
        body, name="pool_fwd", grid=(1,),
        out_shape=jax.ShapeDtypeStruct((SEQ, W_GRP), F32),
        in_specs=[_zslab(W_GRP, 2), _full((W_GRP, W_GRP)), _full((1, W_GRP))],
        out_specs=_full((SEQ, W_GRP)), compiler_params=_params("arbitrary"),
    )(z, w_bd, scale)


def _pool_bwd(z, dy, w_bd, scale):
    def body(z_ref, dy_ref, w_ref, s_ref, dz_ref, dw_ref, ds_ref):
        p, inv, rows, grp = _pool_p(z_ref[...])
        pb = p.astype(BF16)
        dyv = dy_ref[...]
        ds_ref[...] = _sum0(dyv * _dot(pb, w_ref[...]))
        dpre = (dyv * s_ref[...]).astype(BF16)
        dw_ref[...] = _dot_tn(pb, dpre)
        dp = _dot_nt(dpre, w_ref[...])
        q = dp * inv
        levels, s, k = [], q, 1
        for _ in POOL_WINDOWS:
            s = s + _shift_up(s, k, rows)
            levels.append(s)
            k *= 2
        dz_ref[...] = _pool_select(levels, grp) - dp

    return pl.pallas_call(
        body, name="pool_bwd", grid=(1,),
        out_shape=(jax.ShapeDtypeStruct((SEQ, W_GRP), F32), jax.ShapeDtypeStruct((W_GRP, W_GRP), F32),
                   jax.ShapeDtypeStruct((1, W_GRP), F32)),
        in_specs=[_zslab(W_GRP, 2), _full((SEQ, W_GRP)), _full((W_GRP, W_GRP)), _full((1, W_GRP))],
        out_specs=(_full((SEQ, W_GRP)), _full((W_GRP, W_GRP)), _full((1, W_GRP))),
        compiler_params=_params("arbitrary"),
    )(z, dy, w_bd, scale)


def _conv_fwd(z, w):
    def body(z_ref, w_ref, y_ref):
        zc = z_ref[...]
        bg, cg, xh = zc[:, :W_GRP], zc[:, W_GRP:2 * W_GRP], zc[:, 2 * W_GRP:]
        rows = lax.broadcasted_iota(jnp.int32, (SEQ, W_GRP), 0)
        y = cg * xh
        conv = w_ref[0:1, :] * _shift_down(y, 2, rows) + w_ref[1:2, :] * _shift_down(y, 1, rows) + w_ref[2:3, :] * y
        y_ref[...] = bg * conv

    return pl.pallas_call(
        body, name="conv_fwd", grid=(1,),
        out_shape=jax.ShapeDtypeStruct((SEQ, W_GRP), F32),
        in_specs=[_zslab(3 * W_GRP, 1), _full((3, W_GRP))],
        out_specs=_full((SEQ, W_GRP)), compiler_params=_params("arbitrary"),
    )(z, w)


def _conv_bwd(z, dy, w):
    def body(z_ref, dy_ref, w_ref, dz_ref, dw_ref):
        zc = z_ref[...]
        bg, cg, xh = zc[:, :W_GRP], zc[:, W_GRP:2 * W_GRP], zc[:, 2 * W_GRP:]
        rows = lax.broadcasted_iota(jnp.int32, (SEQ, W_GRP), 0)
        y = cg * xh
        y2, y1 = _shift_down(y, 2, rows), _shift_down(y, 1, rows)
        conv = w_ref[0:1, :] * y2 + w_ref[1:2, :] * y1 + w_ref[2:3, :] * y
        dyv = dy_ref[...]
        dconv = dyv * bg
        dw_ref[...] = jnp.zeros_like(dw_ref)
        dw_ref[0:1, :] = _sum0(dconv * y2)
        dw_ref[1:2, :] = _sum0(dconv * y1)
        dw_ref[2:3, :] = _sum0(dconv * y)
        dyy = (w_ref[0:1, :] * _shift_up(dconv, 2, rows) + w_ref[1:2, :] * _shift_up(dconv, 1, rows)
               + w_ref[2:3, :] * dconv)
        dz_ref[:, :W_GRP] = dyv * conv
        dz_ref[:, W_GRP:2 * W_GRP] = dyy * xh
        dz_ref[:, 2 * W_GRP:] = dyy * cg

    return pl.pallas_call(
        body, name="conv_bwd", grid=(1,),
        out_shape=(jax.ShapeDtypeStruct((SEQ, 3 * W_GRP), F32), jax.ShapeDtypeStruct((SUBLANES, W_GRP), F32)),
        in_specs=[_zslab(3 * W_GRP, 1), _full((SEQ, W_GRP)), _full((3, W_GRP))],
        out_specs=(_full((SEQ, 3 * W_GRP)), _full((SUBLANES, W_GRP))),
        compiler_params=_params("arbitrary"),
    )(z, dy, w)


def _s5_disc(lre, lim, ldt, br, bi):
    dt = jnp.exp(ldt)
    mag = jnp.exp(lre * dt)
    ang = lim * dt
    a_re, a_im = mag * jnp.cos(ang), mag * jnp.sin(ang)
    nr, ni = a_re - 1.0, a_im
    den = lre * lre + lim * lim
    k_re = (nr * lre + ni * lim) / den
    k_im = (ni * lre - nr * lim) / den
    return a_re, a_im, k_re * br - k_im * bi, k_re * bi + k_im * br


def _s5_prep_fwd(lre, lim, ldt, br, bi):
    def body(lre_ref, lim_ref, ldt_ref, br_ref, bi_ref, ar_ref, ai_ref, bbr_ref, bbi_ref):
        ar, ai, bbr, bbi = _s5_disc(lre_ref[...], lim_ref[...], ldt_ref[...], br_ref[...], bi_ref[...])
        ar_ref[...] = ar
        ai_ref[...] = ai
        bbr_ref[...] = bbr
        bbi_ref[...] = bbi

    return pl.pallas_call(
        body, name="s5_prep_fwd",
        out_shape=(jax.ShapeDtypeStruct(lre.shape, F32), jax.ShapeDtypeStruct(lre.shape, F32),
                   jax.ShapeDtypeStruct(br.shape, F32), jax.ShapeDtypeStruct(br.shape, F32)),
        compiler_params=_params(),
    )(lre, lim, ldt, br, bi)


def _s5_prep_bwd(lre, lim, ldt, br, bi, dar, dai, dbbr, dbbi):
    def body(lre_ref, lim_ref, ldt_ref, br_ref, bi_ref, dar_ref, dai_ref, dbbr_ref, dbbi_ref,
             o_lre, o_lim, o_ldt, o_br, o_bi):
        _, pull = jax.vjp(_s5_disc, lre_ref[...], lim_ref[...], ldt_ref[...], br_ref[...], bi_ref[...])
        g = pull((dar_ref[...], dai_ref[...], dbbr_ref[...], dbbi_ref[...]))
        for o, v in zip((o_lre, o_lim, o_ldt, o_br, o_bi), g):
            o[...] = v

    return pl.pallas_call(
        body, name="s5_prep_bwd",
        out_shape=tuple(jax.ShapeDtypeStruct(a.shape, F32) for a in (lre, lim, ldt, br, bi)),
        compiler_params=_params(),
    )(lre, lim, ldt, br, bi, dar, dai, dbbr, dbbi)


def _cmul(ar, ai, br, bi):
    return ar * br - ai * bi, ar * bi + ai * br


def _s5_tile_consts(ar, ai, reverse):
    if reverse:
        ai = -ai
    shape = (SUBLANES, LANES)
    row = lax.broadcasted_iota(jnp.int32, shape, 0)
    a1 = (jnp.broadcast_to(ar, shape), jnp.broadcast_to(ai, shape))
    a2 = _cmul(*a1, *a1)
    a4 = _cmul(*a2, *a2)
    a8 = _cmul(*a4, *a4)
    steps = []
    for s, (pr, pi) in ((1, a1), (2, a2), (4, a4)):
        keep = (row < SUBLANES - s) if reverse else (row >= s)
        steps.append((s, jnp.where(keep, pr, 0.0), jnp.where(keep, pi, 0.0)))
    e = (SUBLANES - row) if reverse else (row + 1)
    pr, pi = jnp.ones(shape, F32), jnp.zeros(shape, F32)
    for bit, (qr, qi) in ((1, a1), (2, a2), (4, a4), (8, a8)):
        nr, ni = _cmul(pr, pi, qr, qi)
        hit = (e & bit) != 0
        pr, pi = jnp.where(hit, nr, pr), jnp.where(hit, ni, pi)
    return steps, pr, pi


def _s5_tile(xr, xi, steps, reverse):
    for s, pr, pi in steps:
        sh = SUBLANES - s if reverse else s
        sr, si = pltpu.roll(xr, sh, axis=0), pltpu.roll(xi, sh, axis=0)
        xr, xi = xr + pr * sr - pi * si, xi + pr * si + pi * sr
    return xr, xi


N_TILES = SEQ // SUBLANES


def _s5_fwd(z, b_re, b_im, c_re, c_im, a_re, a_im, d, glu_w, glu_b):
    nblk = S5_LANES // LANES

    def body(u_ref, br_ref, bi_ref, cr_ref, ci_ref, ar_ref, ai_ref, d_ref, gw_ref, gb_ref,
             y_ref, y0_ref, xr_ref, xi_ref, ub_scr, acc_scr):
        jb = pl.program_id(0)

        @pl.when(jb == 0)
        def _():
            ub_scr[...] = u_ref[...].astype(BF16)
            acc_scr[...] = jnp.zeros_like(acc_scr)

        ub = ub_scr[...]
        xr_ref[...] = _dot(ub, br_ref[...])
        xi_ref[...] = _dot(ub, bi_ref[...])
        steps, pr, pi = _s5_tile_consts(ar_ref[...], ai_ref[...], False)

        def tile(t, carry):
            cr, ci = carry
            rows = pl.ds(pl.multiple_of(t * SUBLANES, SUBLANES), SUBLANES)
            xr, xi = _s5_tile(xr_ref[rows, :], xi_ref[rows, :], steps, False)
            xr, xi = xr + pr * cr - pi * ci, xi + pr * ci + pi * cr
            xr_ref[rows, :] = xr
            xi_ref[rows, :] = xi
            return xr[SUBLANES - 1:, :], xi[SUBLANES - 1:, :]

        zero = jnp.zeros((1, LANES), F32)
        lax.fori_loop(0, N_TILES, tile, (zero, zero), unroll=4)
        acc_scr[...] += (_dot(xr_ref[...].astype(BF16), cr_ref[...]) - _dot(xi_ref[...].astype(BF16), ci_ref[...]))

        @pl.when(jb == nblk - 1)
        def _():
            y0 = acc_scr[...] + d_ref[...] * u_ref[...]
            y0_ref[...] = y0
            y1 = _gelu(y0)
            y_ref[...] = y1 * _sigmoid(_dot(y1.astype(BF16), gw_ref[...]) + gb_ref[...])

    lane_blk = pl.BlockSpec((SEQ, LANES), lambda j: (0, j))
    return pl.pallas_call(
        body, name="s5_fwd", grid=(nblk,),
        out_shape=(jax.ShapeDtypeStruct((SEQ, W_GRP), F32), jax.ShapeDtypeStruct((SEQ, W_GRP), F32),
                   jax.ShapeDtypeStruct((SEQ, S5_LANES), F32), jax.ShapeDtypeStruct((SEQ, S5_LANES), F32)),
        in_specs=[_zslab(W_GRP, 6),
                  pl.BlockSpec((W_GRP, LANES), lambda j: (0, j)), pl.BlockSpec((W_GRP, LANES), lambda j: (0, j)),
                  pl.BlockSpec((LANES, W_GRP), lambda j: (j, 0)), pl.BlockSpec((LANES, W_GRP), lambda j: (j, 0)),
                  pl.BlockSpec((1, LANES), lambda j: (0, j)), pl.BlockSpec((1, LANES), lambda j: (0, j)),
                  _full((1, W_GRP)), _full((W_GRP, W_GRP)), _full((1, W_GRP))],
        out_specs=(_full((SEQ, W_GRP)), _full((SEQ, W_GRP)), lane_blk, lane_blk),
        scratch_shapes=[pltpu.VMEM((SEQ, W_GRP), BF16), pltpu.VMEM((SEQ, W_GRP), F32)],
        compiler_params=_params("arbitrary"),
    )(z, b_re, b_im, c_re, c_im, a_re, a_im, d, glu_w, glu_b)


def _s5_bwd(z, y0, dy, xr, xi, b_re, b_im, c_re, c_im, a_re, a_im, d, glu_w, glu_b):
    nblk = S5_LANES // LANES

    def body(u_ref, y0_ref, dy_ref, xr_ref, xi_ref, br_ref, bi_ref, cr_ref, ci_ref, ar_ref, ai_ref,
             d_ref, gw_ref, gb_ref,
             du_ref, dbr_ref, dbi_ref, dcr_ref, dci_ref, dar_ref, dai_ref, dd_ref, dgw_ref, dgb_ref,
             ub_scr, dy0_scr, du_scr, lr_scr, li_scr):
        jb = pl.program_id(0)

        @pl.when(jb == 0)
        def _():
            u = u_ref[...]
            y0v = y0_ref[...]
            y1 = _gelu(y0v)
            y1b = y1.astype(BF16)
            sg = _sigmoid(_dot(y1b, gw_ref[...]) + gb_ref[...])
            dyv = dy_ref[...]
            dpre = dyv * y1 * sg * (1.0 - sg)
            dpb = dpre.astype(BF16)
            dgw_ref[...] = _dot_tn(y1b, dpb)
            dgb_ref[...] = _sum0(dpre)
            dy1 = dyv * sg + _dot_nt(dpb, gw_ref[...])
            dy0 = dy1 * _gelu_grad(y0v)
            dd_ref[...] = _sum0(dy0 * u)
            du_scr[...] = dy0 * d_ref[...]
            dy0_scr[...] = dy0.astype(BF16)
            ub_scr[...] = u.astype(BF16)

        dy0b = dy0_scr[...]
        lr_scr[...] = _dot_nt(dy0b, cr_ref[...])
        li_scr[...] = -_dot_nt(dy0b, ci_ref[...])
        dcr_ref[...] = _dot_tn(xr_ref[...].astype(BF16), dy0b)
        dci_ref[...] = -_dot_tn(xi_ref[...].astype(BF16), dy0b)
        steps, pr, pi = _s5_tile_consts(ar_ref[...], ai_ref[...], True)
        row = lax.broadcasted_iota(jnp.int32, (SUBLANES, LANES), 0)

        def tile(k, carry):
            cr, ci, accr, acci = carry
            t = N_TILES - 1 - k
            rows = pl.ds(pl.multiple_of(t * SUBLANES, SUBLANES), SUBLANES)
            lr, li = _s5_tile(lr_scr[rows, :], li_scr[rows, :], steps, True)
            lr, li = lr + pr * cr - pi * ci, li + pr * ci + pi * cr
            lr_scr[rows, :] = lr
            li_scr[rows, :] = li
            prev = pl.ds(pl.multiple_of(jnp.maximum(t - 1, 0) * SUBLANES, SUBLANES), SUBLANES)
            live = jnp.where(t > 0, 1.0, 0.0)
            xpr = jnp.where(row == 0, pltpu.roll(xr_ref[prev, :], 1, axis=0) * live, pltpu.roll(xr_ref[rows, :], 1, axis=0))
            xpi = jnp.where(row == 0, pltpu.roll(xi_ref[prev, :], 1, axis=0) * live, pltpu.roll(xi_ref[rows, :], 1, axis=0))
            accr = accr + lr * xpr + li * xpi
            acci = acci + li * xpr - lr * xpi
            return lr[0:1, :], li[0:1, :], accr, acci

        zero = jnp.zeros((1, LANES), F32)
        zt = jnp.zeros((SUBLANES, LANES), F32)
        _, _, accr, acci = lax.fori_loop(0, N_TILES, tile, (zero, zero, zt, zt), unroll=4)
        dar_ref[...] = jnp.zeros_like(dar_ref)
        dai_ref[...] = jnp.zeros_like(dai_ref)
        dar_ref[0:1, :] = _sum0(accr)
        dai_ref[0:1, :] = _sum0(acci)
        lrb, lib = lr_scr[...].astype(BF16), li_scr[...].astype(BF16)
        ub = ub_scr[...]
        dbr_ref[...] = _dot_tn(ub, lrb)
        dbi_ref[...] = _dot_tn(ub, lib)
        du_scr[...] += _dot_nt(lrb, br_ref[...]) + _dot_nt(lib, bi_ref[...])

        @pl.when(jb == nblk - 1)
        def _():
            du_ref[...] = du_scr[...]

    lane_blk = pl.BlockSpec((SEQ, LANES), lambda j: (0, j))
    bspec = pl.BlockSpec((W_GRP, LANES), lambda j: (0, j))
    cspec = pl.BlockSpec((LANES, W_GRP), lambda j: (j, 0))
    aspec = pl.BlockSpec((1, LANES), lambda j: (0, j))
    a8spec = pl.BlockSpec((SUBLANES, LANES), lambda j: (0, j))
    sd = jax.ShapeDtypeStruct
    return pl.pallas_call(
        body, name="s5_bwd", grid=(nblk,),
        out_shape=(sd((SEQ, W_GRP), F32), sd((W_GRP, S5_LANES), F32), sd((W_GRP, S5_LANES), F32),
                   sd((S5_LANES, W_GRP), F32), sd((S5_LANES, W_GRP), F32),
                   sd((SUBLANES, S5_LANES), F32), sd((SUBLANES, S5_LANES), F32),
                   sd((1, W_GRP), F32), sd((W_GRP, W_GRP), F32), sd((1, W_GRP), F32)),
        in_specs=[_zslab(W_GRP, 6), _full((SEQ, W_GRP)), _full((SEQ, W_GRP)), lane_blk, lane_blk,
                  bspec, bspec, cspec, cspec, aspec, aspec,
                  _full((1, W_GRP)), _full((W_GRP, W_GRP)), _full((1, W_GRP))],
        out_specs=(_full((SEQ, W_GRP)), bspec, bspec, cspec, cspec, a8spec, a8spec,
                   _full((1, W_GRP)), _full((W_GRP, W_GRP)), _full((1, W_GRP))),
        scratch_shapes=[pltpu.VMEM((SEQ, W_GRP), BF16), pltpu.VMEM((SEQ, W_GRP), BF16), pltpu.VMEM((SEQ, W_GRP), F32),
                        pltpu.VMEM((SEQ, LANES), F32), pltpu.VMEM((SEQ, LANES), F32)],
        compiler_params=_params("arbitrary"),
    )(z, y0, dy, xr, xi, b_re, b_im, c_re, c_im, a_re, a_im, d, glu_w, glu_b)


def _head(x, g, target):
    tm = TOKEN_TILE
    n = SEQ // tm

    def body(x_ref, g_ref, t_ref, dx_ref, st_ref, acc_scr):
        i = pl.program_id(0)

        @pl.when(i == 0)
        def _():
            acc_scr[...] = jnp.zeros_like(acc_scr)

        xhat, r = _rms(x_ref[...])
        gv = g_ref[...]
        err = xhat * gv - t_ref[...]
        dyv = err * (1.0 / D_MODEL)
        dx_ref[...] = _rms_bwd(xhat, r, dyv * gv)
        acc_scr[0:1, :] += _sum0(err * err)
        acc_scr[1:2, :] += _sum0(dyv * xhat)

        @pl.when(i == n - 1)
        def _():
            st_ref[...] = acc_scr[...]
            tot = jnp.sum(acc_scr[0:1, :], axis=-1, keepdims=True) * (0.5 / D_MODEL)
            st_ref[0:1, :] = jnp.broadcast_to(tot, (1, D_MODEL))

    tok = pl.BlockSpec((tm, D_MODEL), lambda i: (i, 0))
    return pl.pallas_call(
        body, name="head", grid=(n,),
        out_shape=(jax.ShapeDtypeStruct((SEQ, D_MODEL), F32), jax.ShapeDtypeStruct((SUBLANES, D_MODEL), F32)),
        in_specs=[tok, _row_spec(1), tok], out_specs=(tok, _row_spec(SUBLANES)),
        scratch_shapes=[pltpu.VMEM((SUBLANES, D_MODEL), F32)],
        compiler_params=_params("arbitrary"),
    )(x, g, target)


def _adamw(w, gparts, m, v, name):
    r, c = w.shape
    npart = gparts.shape[0]
    tr = r
    for cand in (512, 256, 128, 64, 32, 16):
        if r % cand == 0 and r > cand:
            tr = cand
            break
    b1c = 1.0 - ADAM_B1 ** ADAM_STEP
    b2c = 1.0 - ADAM_B2 ** ADAM_STEP

    def body(w_ref, g_ref, m_ref, v_ref, go_ref, d_ref, mo_ref, vo_ref):
        g = g_ref[0].astype(F32)
        for k in range(1, npart):
            g = g + g_ref[k].astype(F32)
        mn = ADAM_B1 * m_ref[...] + (1.0 - ADAM_B1) * g
        vn = ADAM_B2 * v_ref[...] + (1.0 - ADAM_B2) * (g * g)
        m_hat = mn / b1c
        v_hat = vn / b2c
        go_ref[...] = g
        d_ref[...] = -ADAM_LR * (m_hat / (jnp.sqrt(v_hat) + ADAM_EPS) + ADAM_WD * w_ref[...])
        mo_ref[...] = mn
        vo_ref[...] = vn

    blk = pl.BlockSpec((tr, c), lambda i: (i, 0))
    sh = jax.ShapeDtypeStruct((r, c), F32)
    return pl.pallas_call(
        body, name=name, grid=(r // tr,),
        out_shape=(sh, sh, sh, sh),
        in_specs=[blk, pl.BlockSpec((npart, tr, c), lambda i: (0, i, 0)), blk, blk],
        out_specs=(blk, blk, blk, blk), compiler_params=_params("arbitrary"),
    )(w, gparts, m, v)


def _adamw_layer(l, w, gparts, m, v, prev, name):
    _, r, c = w.shape
    npart = gparts.shape[0]
    tr = max(t for t in range(16, 513, 16) if r % t == 0)
    b1c = 1.0 - ADAM_B1 ** ADAM_STEP
    b2c = 1.0 - ADAM_B2 ** ADAM_STEP
    nprev = 0 if prev is None else 4

    def body(*refs):
        w_ref, g_ref, m_ref, v_ref = refs[:4]
        go_ref, d_ref, mo_ref, vo_ref = refs[4 + nprev:]
        g = g_ref[0].astype(F32)
        for k in range(1, npart):
            g = g + g_ref[k].astype(F32)
        mn = ADAM_B1 * m_ref[0] + (1.0 - ADAM_B1) * g
        vn = ADAM_B2 * v_ref[0] + (1.0 - ADAM_B2) * (g * g)
        go_ref[0] = g
        d_ref[0] = -ADAM_LR * ((mn / b1c) / (jnp.sqrt(vn / b2c) + ADAM_EPS) + ADAM_WD * w_ref[0])
        mo_ref[0] = mn
        vo_ref[0] = vn

    blk = pl.BlockSpec((1, tr, c), lambda i: (l, i, 0))
    sh = jax.ShapeDtypeStruct(w.shape, F32)
    keep = [pl.BlockSpec(memory_space=pl.ANY)] * nprev
    return pl.pallas_call(
        body, name=name, grid=(r // tr,),
        out_shape=(sh, sh, sh, sh),
        in_specs=[blk, pl.BlockSpec((npart, tr, c), lambda i: (0, i, 0)), blk, blk, *keep],
        out_specs=(blk, blk, blk, blk),
        input_output_aliases={4 + k: k for k in range(nprev)},
        compiler_params=_params("arbitrary"),
    )(w, gparts, m, v, *(prev or ()))


def _sum_parts(parts, name):
    n, r, c = parts.shape

    def body(p_ref, o_ref):
        acc = p_ref[0]
        for k in range(1, n):
            acc = acc + p_ref[k]
        o_ref[...] = acc

    return pl.pallas_call(
        body, name=name, out_shape=jax.ShapeDtypeStruct((r, c), F32), compiler_params=_params(),
    )(parts)


def _block_diag(blocks):
    g, a, b = blocks.shape
    eye = jnp.eye(g, dtype=blocks.dtype)
    return (blocks[:, :, None, :] * eye[:, None, :, None]).reshape(g * a, g * b)


def _diag_blocks(dense, g):
    a, b = dense.shape[0] // g, dense.shape[1] // g
    d4 = dense.reshape(g, a, g, b)
    eye = jnp.eye(g, dtype=dense.dtype)
    return jnp.sum(d4 * eye[:, None, :, None], axis=2)


def _pack(parts, cols):
    flat = jnp.concatenate([p.reshape(-1) for p in parts])
    unit = N_DEV * SUBLANES * cols
    total = -(-flat.shape[0] // unit) * unit
    flat = jnp.pad(flat, (0, total - flat.shape[0]))
    return flat.reshape(N_DEV, total // (N_DEV * cols), cols)


def _unpack(flat, shapes):
    out, pos = [], 0
    for s in shapes:
        n = math.prod(s)
        out.append(flat[pos:pos + n].reshape(s))
        pos += n
    return out


SMALL = ("norm1_g", "norm2_g", "sgu_w", "sgu_b", "pool_w", "pool_scale", "conv_w", "s5_lambda_re", "s5_lambda_im",
         "s5_b_re", "s5_b_im", "s5_c_re", "s5_c_im", "s5_d", "s5_log_dt", "s5_glu_w", "s5_glu_b", "mix_norm_g",
         "norm3_g", "final_norm_g")
BIG = ("ffn1_w_in", "ffn1_w_out", "w_mix_in", "w_mix_out", "ffn2_w_in", "ffn2_w_out")
TRANSPOSED = ("ffn1_w_in", "w_mix_in", "ffn2_w_in")
WEIGHTS = ("ada_w", "ada_b", "norm1_g", "ffn1_w_in", "ffn1_w_out", "norm2_g", "w_mix_in", "sgu_w", "sgu_b", "pool_w",
           "pool_scale", "conv_w", "s5_lambda_re", "s5_lambda_im", "s5_b_re", "s5_b_im", "s5_c_re", "s5_c_im", "s5_d",
           "s5_log_dt", "s5_glu_w", "s5_glu_b", "mix_norm_g", "w_mix_out", "norm3_g", "ffn2_w_in", "ffn2_w_out",
           "final_norm_g")
PACK_COLS = 1024


def kernel(x, c, ada_w, ada_b, norm1_g, ffn1_w_in, ffn1_w_out, norm2_g, w_mix_in, sgu_w, sgu_b, pool_w, pool_scale, conv_w, s5_lambda_re, s5_lambda_im, s5_b_re, s5_b_im, s5_c_re, s5_c_im, s5_d, s5_log_dt, s5_glu_w, s5_glu_b, mix_norm_g, w_mix_out, norm3_g, ffn2_w_in, ffn2_w_out, final_norm_g, loss_target, m_ada_w, m_ada_b, m_norm1_g, m_ffn1_w_in, m_ffn1_w_out, m_norm2_g, m_w_mix_in, m_sgu_w, m_sgu_b, m_pool_w, m_pool_scale, m_conv_w, m_s5_lambda_re, m_s5_lambda_im, m_s5_b_re, m_s5_b_im, m_s5_c_re, m_s5_c_im, m_s5_d, m_s5_log_dt, m_s5_glu_w, m_s5_glu_b, m_mix_norm_g, m_w_mix_out, m_norm3_g, m_ffn2_w_in, m_ffn2_w_out, m_final_norm_g, v_ada_w, v_ada_b, v_norm1_g, v_ffn1_w_in, v_ffn1_w_out, v_norm2_g, v_w_mix_in, v_sgu_w, v_sgu_b, v_pool_w, v_pool_scale, v_conv_w, v_s5_lambda_re, v_s5_lambda_im, v_s5_b_re, v_s5_b_im, v_s5_c_re, v_s5_c_im, v_s5_d, v_s5_log_dt, v_s5_glu_w, v_s5_glu_b, v_mix_norm_g, v_w_mix_out, v_norm3_g, v_ffn2_w_in, v_ffn2_w_out, v_final_norm_g):
    args = dict(locals())
    W = {n: args[n] for n in WEIGHTS}
    M = {n: args["m_" + n] for n in WEIGHTS}
    V = {n: args["v_" + n] for n in WEIGHTS}
    me = _me()
    L = DEPTH
    x0 = x[0]
    target = loss_target[0]

    conv_cols = conv_w.shape[-1]
    glu_rows = s5_glu_w.shape[1]
    c_g, conv_g, glu_g = _exchange(
        [c.reshape(SUBLANES, LANES), conv_w.reshape(L * 3, conv_cols), s5_glu_w.reshape(L * glu_rows, W_GRP)],
        False, "gather_small")
    c_all = c_g.reshape(N_DEV, D_MODEL)
    conv_full = conv_g.reshape(N_DEV, L, 3, conv_cols).transpose(1, 2, 0, 3).reshape(L, 3, W_GRP)
    glu_full = glu_g.reshape(N_DEV, L, glu_rows, W_GRP).transpose(1, 0, 2, 3).reshape(L, W_GRP, W_GRP)

    ncol = ada_w.shape[-1]
    ada_b_mine = lax.dynamic_slice_in_dim(ada_b, me * ncol, ncol, axis=1).reshape(L, 1, ncol)
    cond_part = _cond_fwd(c_all, ada_w, ada_b_mine)
    (cond_g,) = _exchange([cond_part.reshape(L * N_DEV, ncol)], False, "gather_cond")
    cond_g = cond_g.reshape(N_DEV, L, N_DEV, ncol)
    cond_mine = lax.dynamic_index_in_dim(cond_g, me, axis=2, keepdims=False)
    cond = cond_mine.transpose(1, 0, 2).reshape(L, N_ADA, D_MODEL)

    lg = L * S5_GROUPS
    lre3 = s5_lambda_re.reshape(lg, S5_STATE, 1)
    lim3 = s5_lambda_im.reshape(lg, S5_STATE, 1)
    ldt3 = s5_log_dt.reshape(lg, 1, 1)
    br3 = s5_b_re.reshape(lg, S5_STATE, S5_GROUP_CH)
    bi3 = s5_b_im.reshape(lg, S5_STATE, S5_GROUP_CH)
    a_re3, a_im3, bb_re3, bb_im3 = _s5_prep_fwd(lre3, lim3, ldt3, br3, bi3)
    a_re = a_re3.reshape(L, 1, S5_LANES)
    a_im = a_im3.reshape(L, 1, S5_LANES)

    def b_mat(bb3, l):
        return _block_diag(bb3.reshape(L, S5_GROUPS, S5_STATE, S5_GROUP_CH)[l].transpose(0, 2, 1)).astype(BF16)

    def c_mat(cw, l):
        return _block_diag(cw[l].transpose(0, 2, 1)).astype(BF16)

    for n in TRANSPOSED:
        W[n], M[n], V[n] = (a.transpose(0, 2, 1) for a in (W[n], M[n], V[n]))

    gathered_shape = {"ffn1_w_out": (N_FF_CHUNK, FF_PIECE, D_MODEL), "ffn2_w_out": (N_FF_CHUNK, FF_PIECE, D_MODEL),
                      "w_mix_in": (P_IN, D_MODEL), "w_mix_out": (D_MODEL, D_MODEL)}

    def gather_start(l, after, names=BIG):
        srcs = [W[n][l].astype(BF16) for n in names]
        lands = _place_own(srcs, False, "gather_weights_own")
        return _exchange_start(srcs, lands, after, False, "gather_weights_start", OTHER_CHIPS)

    def gather_finish(handle, after, names=BIG):
        g = _exchange_wait(handle, after, False, "gather_weights_wait", OTHER_CHIPS)
        out = dict(zip(names, _gather_sibling(g, "gather_weights_sibling")))
        return {n: (a.reshape(gathered_shape[n]) if n in gathered_shape else a) for n, a in out.items()}

    def mixer_consts(l):
        w_cat = sgu_w[l].transpose(1, 0, 2).reshape(CHUNK, SGU_HEADS * CHUNK)
        bias = jnp.repeat(sgu_b[l].T, SGU_HEAD_DIM, axis=1)
        return dict(
            w_cat=w_cat, bias=bias, pool_bd=_block_diag(pool_w[l]).astype(BF16), pool_scale=pool_scale[l][None],
            conv=conv_full[l], b_re=b_mat(bb_re3, l), b_im=b_mat(bb_im3, l), c_re=c_mat(s5_c_re, l),
            c_im=c_mat(s5_c_im, l), a_re=a_re[l], a_im=a_im[l], d=s5_d[l][None], glu_w=glu_full[l].astype(BF16),
            glu_b=s5_glu_b[l][None])

    saved = []
    xc = x0
    first_ffn = BIG[:2]
    handle, token = gather_start(0, cond, first_ffn)
    handle_rest, _ = gather_start(0, token, BIG[2:])
    wl = gather_finish(handle, x0, first_ffn)
    for l in range(L):
        cl = cond[l]
        if l + 1 < L:
            handle, token = gather_start(l + 1, wl["ffn1_w_in"])
            cl = cl + token[0, 0]
        mc = mixer_consts(l)
        x_a = xc
        x_b, f1 = _ffn_fwd(x_a, cl[0:3], norm1_g[l][None], wl["ffn1_w_in"], wl["ffn1_w_out"])
        if l == 0:
            wl = {**wl, **gather_finish(handle_rest, x_b, BIG[2:])}
        z = _mix_in_fwd(x_b, cl[3:5], norm2_g[l][None], wl["w_mix_in"])
        ya = _sgu_fwd(z, mc["w_cat"], mc["bias"])
        yb = _pool_fwd(z, mc["pool_bd"], mc["pool_scale"])
        yc = _conv_fwd(z, mc["conv"])
        yd, y0, sxr, sxi = _s5_fwd(z, mc["b_re"], mc["b_im"], mc["c_re"], mc["c_im"], mc["a_re"], mc["a_im"],
                                   mc["d"], mc["glu_w"], mc["glu_b"])
        ys = (ya, yb, yc, yd)
        x_c = _mix_out_fwd(x_b, ys, mix_norm_g[l][None], cl[5:6], wl["w_mix_out"])
        x_d, f2 = _ffn_fwd(x_c, cl[6:9], norm3_g[l][None], wl["ffn2_w_in"], wl["ffn2_w_out"])
        saved.append(dict(wl=wl, mc=mc, x_a=x_a, x_b=x_b, x_c=x_c, f1=f1, f2=f2, z=z, ys=ys, y0=y0, xr=sxr, xi=sxi))
        xc = x_d
        if l + 1 < L:
            wl = gather_finish(handle, x_d)

    dx, stats = _head(xc, final_norm_g[None], target)
    loss = lax.psum(stats[0, 0], MESH_AXES)

    small_grads = {n: [None] * L for n in SMALL if n != "final_norm_g"}
    small_grads["final_norm_g"] = stats[1]
    big_out = {n: None for n in BIG}
    pending = None

    def finish_scatter(pend, after, names=BIG):
        layer, hnd = pend
        recv = _exchange_wait(hnd, after, True, "scatter_grads_wait")
        for n, r in zip(names, recv):
            big_out[n] = _adamw_layer(layer, W[n], r, M[n], V[n], big_out[n], "adamw_" + n)
        return recv[0]

    dcond_rows = [None] * L
    d_are, d_aim, d_bbre, d_bbim = [None] * L, [None] * L, [None] * L, [None] * L
    token = None
    for l in reversed(range(L)):
        sv = saved[l]
        wl, mc, cl = sv["wl"], sv["mc"], cond[l]
        if token is not None:
            cl = cl + token[0, 0]
        dx, da, db, act, hb, dob, part3 = _ffn_bwd(dx, sv["x_c"], sv["f2"], cl[6:9], norm3_g[l][None],
                                                   wl["ffn2_w_in"], wl["ffn2_w_out"])
        g_ffn2_in = _dw(da, hb, db, name="dw_ffn_in")
        g_ffn2_out = _dw(act, dob, name="dw_ffn_out").reshape(N_DEV, D_FF // N_DEV, D_MODEL)
        dya, dyb, dyc, dyd, ynb, dmob, part_mo = _mix_out_bwd(dx, sv["ys"], mix_norm_g[l][None], cl[5:6],
                                                              wl["w_mix_out"])
        g_mix_out = _dw(ynb[None], dmob, name="dw_mix_out").reshape(N_DEV, D_MODEL // N_DEV, D_MODEL)
        z = sv["z"]
        dza, dw_cat, dbias = _sgu_bwd(z, dya, mc["w_cat"], mc["bias"])
        dzb, dpool_dense, dpool_scale = _pool_bwd(z, dyb, mc["pool_bd"], mc["pool_scale"])
        dzc, dconv8 = _conv_bwd(z, dyc, mc["conv"])
        (dzd, dbre_d, dbim_d, dcre_d, dcim_d, dar8, dai8, dd, dglu_w, dglu_b) = _s5_bwd(
            z, sv["y0"], dyd, sv["xr"], sv["xi"], mc["b_re"], mc["b_im"], mc["c_re"], mc["c_im"],
            mc["a_re"], mc["a_im"], mc["d"], mc["glu_w"], mc["glu_b"])
        dx, h2b, dzbf, part2 = _mix_in_bwd((dza, dzb, dzc, dzd), sv["x_b"], dx, cl[3:5], norm2_g[l][None],
                                           wl["w_mix_in"])
        g_mix_in = _dw(dzbf[None], h2b, name="dw_mix_in").reshape(N_DEV, P_IN // N_DEV, D_MODEL)
        cl1 = cl[0:3]
        if l == 0:
            early = [g_mix_in, g_mix_out, g_ffn2_in, g_ffn2_out]
            handle, token = _exchange_start(early, _place_own(early, True, "scatter_grads_own"), g_mix_in, True,
                                            "scatter_grads_start")
            pending_early = (0, handle)
            cl1 = cl1 + token[0, 0]
        dx, da, db, act, hb, dob, part1 = _ffn_bwd(dx, sv["x_a"], sv["f1"], cl1, norm1_g[l][None],
                                                   wl["ffn1_w_in"], wl["ffn1_w_out"])
        g_ffn1_in = _dw(da, hb, db, name="dw_ffn_in")
        g_ffn1_out = _dw(act, dob, name="dw_ffn_out").reshape(N_DEV, D_FF // N_DEV, D_MODEL)
        last = finish_scatter(pending, g_ffn1_in) if pending is not None else g_ffn1_in
        if l > 0:
            pieces = [g_ffn1_in, g_ffn1_out, g_mix_in, g_mix_out, g_ffn2_in, g_ffn2_out]
            lands = _place_own(pieces, True, "scatter_grads_own")
            handle, token = _exchange_start(pieces, lands, last, True, "scatter_grads_start")
            pending = (l, handle)
        else:
            pieces = [g_ffn1_in, g_ffn1_out]
            first_layer = (pieces, _place_own(pieces, True, "scatter_grads_own"), last)
        dcond_rows[l] = jnp.concatenate([part1[0:3], part2[0:2], part_mo[0:1], part3[0:3]], axis=0)
        sg = small_grads
        sg["norm1_g"][l] = part1[3]
        sg["norm2_g"][l] = part2[2]
        sg["norm3_g"][l] = part3[3]
        sg["mix_norm_g"][l] = part_mo[1]
        sg["sgu_w"][l] = dw_cat.reshape(CHUNK, SGU_HEADS, CHUNK).transpose(1, 0, 2)
        sg["sgu_b"][l] = dbias[:, ::SGU_HEAD_DIM].T
        sg["pool_w"][l] = _diag_blocks(dpool_dense, len(POOL_WINDOWS))
        sg["pool_scale"][l] = dpool_scale[0]
        sg["conv_w"][l] = dconv8[0:3]
        sg["s5_c_re"][l] = _diag_blocks(dcre_d, S5_GROUPS).transpose(0, 2, 1)
        sg["s5_c_im"][l] = _diag_blocks(dcim_d, S5_GROUPS).transpose(0, 2, 1)
        sg["s5_d"][l] = dd[0]
        sg["s5_glu_w"][l] = dglu_w
        sg["s5_glu_b"][l] = dglu_b[0]
        d_are[l], d_aim[l] = dar8[0], dai8[0]
        d_bbre[l] = _diag_blocks(dbre_d, S5_GROUPS).transpose(0, 2, 1)
        d_bbim[l] = _diag_blocks(dbim_d, S5_GROUPS).transpose(0, 2, 1)
    grad_x = dx

    g_lre, g_lim, g_ldt, g_br, g_bi = _s5_prep_bwd(
        lre3, lim3, ldt3, br3, bi3,
        jnp.stack(d_are).reshape(lg, S5_STATE, 1), jnp.stack(d_aim).reshape(lg, S5_STATE, 1),
        jnp.stack(d_bbre).reshape(lg, S5_STATE, S5_GROUP_CH), jnp.stack(d_bbim).reshape(lg, S5_STATE, S5_GROUP_CH))
    small = {n: (jnp.stack(v) if isinstance(v, list) and v[0] is not None else v) for n, v in small_grads.items()}
    small["s5_lambda_re"] = g_lre.reshape(s5_lambda_re.shape)
    small["s5_lambda_im"] = g_lim.reshape(s5_lambda_im.shape)
    small["s5_log_dt"] = g_ldt.reshape(s5_log_dt.shape)
    small["s5_b_re"] = g_br.reshape(s5_b_re.shape)
    small["s5_b_im"] = g_bi.reshape(s5_b_im.shape)

    small_shapes = [(L, 3, W_GRP) if n == "conv_w" else (L, W_GRP, W_GRP) if n == "s5_glu_w" else W[n].shape
                    for n in SMALL]
    packed = _pack([small[n].reshape(s) for n, s in zip(SMALL, small_shapes)], PACK_COLS)
    (pieces,) = _exchange([packed], True, "scatter_small")
    mine = _sum_parts(pieces, "sum_small")
    (summed,) = _exchange([mine], False, "gather_small_sums")
    small_sum = dict(zip(SMALL, _unpack(summed.reshape(-1), small_shapes)))
    small_sum["conv_w"] = lax.dynamic_slice_in_dim(small_sum["conv_w"], me * conv_cols, conv_cols, axis=2)
    small_sum["s5_glu_w"] = lax.dynamic_slice_in_dim(small_sum["s5_glu_w"], me * glu_rows, glu_rows, axis=1)

    dcond = jnp.stack(dcond_rows).reshape(L * N_ADA, D_MODEL)
    (dcond_g,) = _exchange([dcond], False, "gather_dcond")
    g_pieces, g_lands, g_last = first_layer
    behind = jnp.stack([summed[0, 0, 0], dcond_g[0, 0, 0], g_last[0, 0, 0].astype(F32)])
    handle, _ = _exchange_start(g_pieces, g_lands, behind, True, "scatter_grads_start")
    pending = (0, handle)
    dcond_all = dcond_g.reshape(N_DEV, L, N_ADA * D_MODEL).transpose(1, 0, 2)
    dcond_mine = lax.dynamic_slice_in_dim(dcond_all, me * ncol, ncol, axis=2)
    g_ada_w, g_ada_b = _cond_bwd(c_all.T, dcond_mine, dcond_all)

    grads, deltas, new_m, new_v = {}, {}, {}, {}
    out = _adamw(ada_w.reshape(L * D_MODEL, ncol), g_ada_w.reshape(1, L * D_MODEL, ncol),
                 m_ada_w.reshape(L * D_MODEL, ncol), v_ada_w.reshape(L * D_MODEL, ncol), "adamw_ada_w")
    grads["ada_w"], deltas["ada_w"], new_m["ada_w"], new_v["ada_w"] = (o.reshape(ada_w.shape) for o in out)
    small_names = SMALL + ("ada_b",)
    small_g = dict(small_sum)
    small_g["ada_b"] = g_ada_b.reshape(ada_b.shape)
    shapes = [W[n].shape for n in small_names]
    pw = _pack([W[n] for n in small_names], PACK_COLS)
    rows = pw.shape[0] * pw.shape[1]
    out = _adamw(pw.reshape(rows, PACK_COLS),
                 _pack([small_g[n] for n in small_names], PACK_COLS).reshape(1, rows, PACK_COLS),
                 _pack([M[n] for n in small_names], PACK_COLS).reshape(rows, PACK_COLS),
                 _pack([V[n] for n in small_names], PACK_COLS).reshape(rows, PACK_COLS), "adamw_small")
    for store, o in zip((grads, deltas, new_m, new_v), out):
        store.update(zip(small_names, _unpack(o.reshape(-1), shapes)))
    finish_scatter(pending_early, out[0], BIG[2:])
    finish_scatter(pending, out[0], BIG[:2])
    for n in BIG:
        res = big_out[n]
        if n in TRANSPOSED:
            res = tuple(r.transpose(0, 2, 1) for r in res)
        grads[n], deltas[n], new_m[n], new_v[n] = res

    return (loss, grad_x[None], *[grads[n] for n in WEIGHTS], *[deltas[n] for n in WEIGHTS],
            *[new_m[n] for n in WEIGHTS], *[new_v[n] for n in WEIGHTS])
```

```python
import functools
import math

import jax
import jax.numpy as jnp
from jax import lax
from jax.experimental import pallas as pl
from jax.experimental.pallas import tpu as pltpu

F32 = jnp.float32
BF16 = jnp.bfloat16

D_MODEL = 1024
SEQ = 2048
DEPTH = 4
N_DEV = 8
W_GRP = 256
CHUNK = 128
SGU_HEADS = 4
SGU_HEAD_DIM = 64
POOL_WINDOWS = (2, 4, 8, 16)
POOL_GROUP_DIM = 64
S5_GROUPS = 16
S5_GROUP_CH = 16
S5_STATE = 64
S5_LANES = S5_GROUPS * S5_STATE
S5_BLOCK = 256
P_IN = 1792
D_FF = 2816
FF_PIECE = 2 * D_FF // N_DEV
N_FF_CHUNK = D_FF // FF_PIECE
N_ADA = 9
EPS = 1e-6
ADAM_LR = 0.001
ADAM_B1 = 0.9
ADAM_B2 = 0.999
ADAM_EPS = 1e-08
ADAM_WD = 0.01
ADAM_STEP = 10

SUBLANES = 8
LANES = 128
VMEM_LIMIT = 56 * 1024 * 1024
TOKEN_TILE = 512
ROW_SUBTILE = 256
HIGHEST = lax.Precision.HIGHEST
MESH_AXES = ("x", "y", "c")

_GELU_C = math.sqrt(2.0 / math.pi)
_GELU_A = 0.044715


def _params(*sem):
    return pltpu.CompilerParams(dimension_semantics=tuple(sem) if sem else None, vmem_limit_bytes=VMEM_LIMIT)


def _dot(a, b):
    return jnp.dot(a, b, preferred_element_type=F32)


def _dot_nt(a, b):
    return lax.dot_general(a, b, (((1,), (1,)), ((), ())), preferred_element_type=F32)


def _dot_tn(a, b):
    return lax.dot_general(a, b, (((0,), (0,)), ((), ())), preferred_element_type=F32)


def _dot_hi(a, b):
    return jnp.dot(a, b, preferred_element_type=F32, precision=HIGHEST)


def _sigmoid(x):
    return 1.0 / (1.0 + jnp.exp(-x))


def _gelu(x):
    return 0.5 * x * (1.0 + jnp.tanh(_GELU_C * (x + _GELU_A * x * x * x)))


def _gelu_grad(x):
    t = jnp.tanh(_GELU_C * (x + _GELU_A * x * x * x))
    return 0.5 * (1.0 + t) + 0.5 * x * (1.0 - t * t) * (_GELU_C * (1.0 + 3.0 * _GELU_A * x * x))


def _rms(x):
    r = lax.rsqrt(jnp.mean(x * x, axis=-1, keepdims=True) + EPS)
    return x * r, r


def _rms_bwd(xhat, r, dxhat):
    return r * (dxhat - xhat * jnp.mean(dxhat * xhat, axis=-1, keepdims=True))


def _sum0(x):
    return jnp.sum(x, axis=0, keepdims=True)


def _me():
    return 4 * lax.axis_index("x") + 2 * lax.axis_index("y") + lax.axis_index("c")


def _exchange(srcs, scatter, name):
    n = len(srcs)
    out_shapes = []
    for s in srcs:
        piece = s.shape[1:] if scatter else s.shape
        out_shapes.append(jax.ShapeDtypeStruct((N_DEV,) + tuple(piece), s.dtype))

    def body(*refs):
        ins, outs = refs[:n], refs[n:2 * n]
        send_sems, recv_sems, local_sems = refs[2 * n:]
        x, y, c = lax.axis_index("x"), lax.axis_index("y"), lax.axis_index("c")
        me = 4 * x + 2 * y + c

        def src_of(i, dev):
            return ins[i].at[dev] if scatter else ins[i]

        local = [pltpu.make_async_copy(src_of(i, me), outs[i].at[me], local_sems.at[i]) for i in range(n)]
        for cp in local:
            cp.start()
        sends, recvs = [], []
        for k in range(1, N_DEV):
            px = 1 - x if (k >> 2) & 1 else x
            py = 1 - y if (k >> 1) & 1 else y
            pc = 1 - c if k & 1 else c
            peer = 4 * px + 2 * py + pc
            for i in range(n):
                sends.append(pltpu.make_async_remote_copy(
                    src_ref=src_of(i, peer), dst_ref=outs[i].at[me],
                    send_sem=send_sems.at[k - 1, i], recv_sem=recv_sems.at[k - 1, i],
                    device_id=(px, py, pc), device_id_type=pl.DeviceIdType.MESH))
                recvs.append(pltpu.make_async_remote_copy(
                    src_ref=src_of(i, peer), dst_ref=outs[i].at[peer],
                    send_sem=send_sems.at[k - 1, i], recv_sem=recv_sems.at[k - 1, i],
                    device_id=(px, py, pc), device_id_type=pl.DeviceIdType.MESH))
        for cp in sends:
            cp.start()
        for cp in recvs:
            cp.wait_recv()
        for cp in sends:
            cp.wait_send()
        for cp in local:
            cp.wait()

    hbm = pl.BlockSpec(memory_space=pltpu.HBM)
    return pl.pallas_call(
        body, name=name, out_shape=out_shapes,
        in_specs=[hbm] * n, out_specs=[hbm] * n,
        scratch_shapes=[pltpu.SemaphoreType.DMA((N_DEV - 1, n)), pltpu.SemaphoreType.DMA((N_DEV - 1, n)),
                        pltpu.SemaphoreType.DMA((n,))],
    )(*srcs)


ALL_PEERS = tuple(range(1, N_DEV))
OTHER_CHIPS = (2, 4, 6)


def _peers(which):
    x, y, c = lax.axis_index("x"), lax.axis_index("y"), lax.axis_index("c")
    out = []
    for k in which:
        px = 1 - x if (k >> 2) & 1 else x
        py = 1 - y if (k >> 1) & 1 else y
        pc = 1 - c if k & 1 else c
        out.append(((px, py, pc), 4 * px + 2 * py + pc))
    return out


def _split_copies(ins, lands, send_sems, recv_sems, scatter, with_recvs, which):
    me = _me()
    sends, recvs = [], []
    for j, (dev, peer) in enumerate(_peers(which)):
        for i in range(len(ins)):
            src = ins[i].at[peer] if scatter else ins[i]
            slot = j * len(ins) + i
            sems = dict(send_sem=send_sems.at[slot], recv_sem=recv_sems.at[slot],
                        device_id=dev, device_id_type=pl.DeviceIdType.MESH)
            sends.append(pltpu.make_async_remote_copy(src_ref=src, dst_ref=lands[i].at[me], **sems))
            if with_recvs:
                recvs.append(pltpu.make_async_remote_copy(src_ref=src, dst_ref=lands[i].at[peer], **sems))
    return sends, recvs


_HBM = pl.BlockSpec(memory_space=pltpu.HBM)
_SEM = pl.BlockSpec(memory_space=pltpu.SEMAPHORE)
_EFFECT = pltpu.SideEffectType.DATAFLOW_SIDE_EFFECTING


def _place_own(srcs, scatter, name):
    n = len(srcs)
    halves = 2
    out_shapes, in_specs, out_specs = [], [], []
    for s in srcs:
        r, c = s.shape[-2:]
        out_shapes.append(jax.ShapeDtypeStruct((N_DEV, r, c), s.dtype))
        if scatter:
            in_specs.append(pl.BlockSpec((1, r // halves, c), lambda i, me: (me[0], i, 0)))
        else:
            in_specs.append(pl.BlockSpec((r // halves, c), lambda i, me: (i, 0)))
        out_specs.append(pl.BlockSpec((1, r // halves, c), lambda i, me: (me[0], i, 0)))

    def body(me_ref, *refs):
        for i in range(n):
            refs[n + i][0] = refs[i][0] if scatter else refs[i][...]

    return pl.pallas_call(
        body, name=name, out_shape=out_shapes,
        grid_spec=pltpu.PrefetchScalarGridSpec(num_scalar_prefetch=1, grid=(halves,), in_specs=in_specs,
                                               out_specs=out_specs),
        compiler_params=_params("arbitrary"),
    )(_me().reshape(1).astype(jnp.int32), *srcs)


def _exchange_start(srcs, lands, after, scatter, name, which=ALL_PEERS):
    n = len(srcs)

    def body(*refs):
        ins, land_in = refs[:n], refs[n:2 * n]
        send_sems, recv_sems = refs[2 * n + 1], refs[2 * n + 2]
        token = refs[-1]
        sends, _ = _split_copies(ins, land_in, send_sems, recv_sems, scatter, False, which)
        for cp in sends:
            cp.start()
        token[...] = jnp.zeros_like(token)

    sem = pltpu.SemaphoreType.DMA((len(which) * n,))
    out = pl.pallas_call(
        body, name=name,
        out_shape=(sem, sem, *[pltpu.HBM(s.shape, s.dtype) for s in srcs], *[pltpu.HBM(s.shape, s.dtype) for s in lands],
                   jax.ShapeDtypeStruct((SUBLANES, LANES), F32)),
        in_specs=[_HBM] * (2 * n) + [pl.BlockSpec(memory_space=pl.ANY)],
        out_specs=(_SEM, _SEM, *[_HBM] * (2 * n), pl.BlockSpec(memory_space=pltpu.VMEM)),
        input_output_aliases={i: 2 + i for i in range(2 * n)},
        compiler_params=pltpu.CompilerParams(has_side_effects=_EFFECT),
    )(*srcs, *lands, after)
    return (out[0], out[1], out[2:2 + n], out[2 + n:2 + 2 * n]), out[-1]


def _exchange_wait(handle, after, scatter, name, which=ALL_PEERS):
    send_sems, recv_sems, srcs, lands = handle
    n = len(srcs)

    def body(*refs):
        ins, land_in = refs[:n], refs[n:2 * n]
        sends, recvs = _split_copies(ins, land_in, refs[2 * n], refs[2 * n + 1], scatter, True, which)
        for cp in sends:
            cp.wait_send()
        for cp in recvs:
            cp.wait_recv()

    out = pl.pallas_call(
        body, name=name,
        out_shape=(*[pltpu.HBM(s.shape, s.dtype) for s in srcs], *[pltpu.HBM(s.shape, s.dtype) for s in lands]),
        in_specs=[_HBM] * (2 * n) + [_SEM, _SEM, pl.BlockSpec(memory_space=pl.ANY)],
        out_specs=tuple([_HBM] * (2 * n)),
        input_output_aliases={i: i for i in range(2 * n)},
        compiler_params=pltpu.CompilerParams(has_side_effects=_EFFECT),
    )(*srcs, *lands, send_sems, recv_sems, after)
    return out[n:]


def _gather_sibling(lands, name):
    n = len(lands)
    chips = ((0, 0), (0, 1), (1, 0), (1, 1))

    def body(*refs):
        ins, outs = refs[:n], refs[n:2 * n]
        send_sems, recv_sems = refs[2 * n], refs[2 * n + 1]
        x, y, c = lax.axis_index("x"), lax.axis_index("y"), lax.axis_index("c")
        sends, recvs = [], []
        for j, (bx, by) in enumerate(chips):
            chip = 4 * (1 - x if bx else x) + 2 * (1 - y if by else y)
            for i in range(n):
                sems = dict(send_sem=send_sems.at[j * n + i], recv_sem=recv_sems.at[j * n + i],
                            device_id=(x, y, 1 - c), device_id_type=pl.DeviceIdType.MESH)
                sends.append(pltpu.make_async_remote_copy(src_ref=ins[i].at[chip + c], dst_ref=outs[i].at[chip + c], **sems))
                recvs.append(pltpu.make_async_remote_copy(src_ref=ins[i].at[chip + c], dst_ref=outs[i].at[chip + 1 - c],
                                                          **sems))
        for cp in sends:
            cp.start()
        for cp in recvs:
            cp.wait_recv()
        for cp in sends:
            cp.wait_send()

    return pl.pallas_call(
        body, name=name, out_shape=[jax.ShapeDtypeStruct(a.shape, a.dtype) for a in lands],
        in_specs=[_HBM] * n, out_specs=[_HBM] * n,
        scratch_shapes=[pltpu.SemaphoreType.DMA((len(chips) * n,)), pltpu.SemaphoreType.DMA((len(chips) * n,))],
        input_output_aliases={i: i for i in range(n)},
    )(*lands)


def _cond_fwd(c_all, ada_w, ada_b_mine):
    ncol = ada_w.shape[-1]

    def body(c_ref, w_ref, b_ref, o_ref):
        c = c_ref[...]
        ca = (c * _sigmoid(c)).astype(BF16)
        o_ref[0] = _dot(ca, w_ref[0].astype(BF16)) + b_ref[0]

    return pl.pallas_call(
        body, name="cond_fwd", grid=(DEPTH,),
        out_shape=jax.ShapeDtypeStruct((DEPTH, N_DEV, ncol), F32),
        in_specs=[pl.BlockSpec((N_DEV, D_MODEL), lambda l: (0, 0)),
                  pl.BlockSpec((1, D_MODEL, ncol), lambda l: (l, 0, 0)),
                  pl.BlockSpec((1, 1, ncol), lambda l: (l, 0, 0))],
        out_specs=pl.BlockSpec((1, N_DEV, ncol), lambda l: (l, 0, 0)),
        compiler_params=_params("arbitrary"),
    )(c_all, ada_w, ada_b_mine)


def _cond_bwd(c_all_t, dcond_mine, dcond_all):
    ncol = dcond_mine.shape[-1]
    nall = dcond_all.shape[-1]

    def body(ct_ref, d_ref, da_ref, gw_ref, gb_ref):
        ct = ct_ref[...]
        ct = ct * _sigmoid(ct)
        d = d_ref[0]
        acc = ct[:, 0:1] * d[0:1, :]
        for b in range(1, N_DEV):
            acc = acc + ct[:, b:b + 1] * d[b:b + 1, :]
        gw_ref[0] = acc
        gb_ref[0] = _sum0(da_ref[0])

    return pl.pallas_call(
        body, name="cond_bwd", grid=(DEPTH,),
        out_shape=(jax.ShapeDtypeStruct((DEPTH, D_MODEL, ncol), F32), jax.ShapeDtypeStruct((DEPTH, 1, nall), F32)),
        in_specs=[pl.BlockSpec((D_MODEL, N_DEV), lambda l: (0, 0)),
                  pl.BlockSpec((1, N_DEV, ncol), lambda l: (l, 0, 0)),
                  pl.BlockSpec((1, N_DEV, nall), lambda l: (l, 0, 0))],
        out_specs=(pl.BlockSpec((1, D_MODEL, ncol), lambda l: (l, 0, 0)),
                   pl.BlockSpec((1, 1, nall), lambda l: (l, 0, 0))),
        compiler_params=_params("arbitrary"),
    )(c_all_t, dcond_mine, dcond_all)


def _modnorm(x, g, shift, scale):
    xhat, r = _rms(x)
    return (xhat * g) * (1.0 + scale) + shift, xhat, r


def _modnorm_bwd(xhat, r, g, scale, dh):
    n = xhat * g
    dn = dh * (1.0 + scale)
    dx = _rms_bwd(xhat, r, dn * g)
    return dx, _sum0(dh), _sum0(dh * n), _sum0(dn * xhat)


def _row_spec(rows):
    return pl.BlockSpec((rows, D_MODEL), lambda *_: (0, 0))


def _ffn_fwd(x, cond3, g, w_in_g, w_out_g):
    tm = TOKEN_TILE
    last = N_FF_CHUNK - 1

    def body(x_ref, cond_ref, g_ref, wa_ref, wb_ref, wo_ref, xo_ref, f_ref, h_scr, acc_scr):
        j = pl.program_id(1)

        @pl.when(j == 0)
        def _():
            h, _, _ = _modnorm(x_ref[...], g_ref[...], cond_ref[0:1, :], cond_ref[1:2, :])
            h_scr[...] = h.astype(BF16)
            acc_scr[...] = jnp.zeros_like(acc_scr)

        wa, wb, wo = wa_ref[0], wb_ref[0], wo_ref[0]
        for r in range(tm // ROW_SUBTILE):
            rows = slice(r * ROW_SUBTILE, (r + 1) * ROW_SUBTILE)
            h = h_scr[rows, :]
            a = _dot_nt(h, wa)
            b = _dot_nt(h, wb)
            act = (a * _sigmoid(a)) * b
            acc_scr[rows, :] += _dot(act.astype(BF16), wo)

        @pl.when(j == last)
        def _():
            f = acc_scr[...]
            f_ref[...] = f
            xo_ref[...] = x_ref[...] + (0.5 * cond_ref[2:3, :]) * f

    tok = pl.BlockSpec((tm, D_MODEL), lambda i, j: (i, 0))
    return pl.pallas_call(
        body, name="ffn_fwd", grid=(SEQ // tm, N_FF_CHUNK),
        out_shape=(jax.ShapeDtypeStruct((SEQ, D_MODEL), F32), jax.ShapeDtypeStruct((SEQ, D_MODEL), F32)),
        in_specs=[tok, _row_spec(3), _row_spec(1),
                  pl.BlockSpec((1, FF_PIECE, D_MODEL), lambda i, j: (j, 0, 0)),
                  pl.BlockSpec((1, FF_PIECE, D_MODEL), lambda i, j: (j + N_FF_CHUNK, 0, 0)),
                  pl.BlockSpec((1, FF_PIECE, D_MODEL), lambda i, j: (j, 0, 0))],
        out_specs=(tok, tok),
        scratch_shapes=[pltpu.VMEM((tm, D_MODEL), BF16), pltpu.VMEM((tm, D_MODEL), F32)],
        compiler_params=_params("arbitrary", "arbitrary"),
    )(x, cond3, g, w_in_g, w_in_g, w_out_g)


def _ffn_bwd(dy, x, f, cond3, g, w_in_g, w_out_g):
    tm = TOKEN_TILE
    last = N_FF_CHUNK - 1

    def body(dy_ref, x_ref, f_ref, cond_ref, g_ref, wa_ref, wb_ref, wo_ref,
             dx_ref, da_ref, db_ref, act_ref, h_ref, do_ref, part_ref, h_scr, do_scr, dh_scr):
        i, j = pl.program_id(0), pl.program_id(1)

        @pl.when(j == 0)
        def _():
            h, _, _ = _modnorm(x_ref[...], g_ref[...], cond_ref[0:1, :], cond_ref[1:2, :])
            hb = h.astype(BF16)
            h_scr[...] = hb
            h_ref[...] = hb
            do = ((0.5 * cond_ref[2:3, :]) * dy_ref[...]).astype(BF16)
            do_scr[...] = do
            do_ref[...] = do
            dh_scr[...] = jnp.zeros_like(dh_scr)

        @pl.when((i == 0) & (j == 0))
        def _():
            part_ref[...] = jnp.zeros_like(part_ref)

        wa, wb, wo = wa_ref[0], wb_ref[0], wo_ref[0]
        for r in range(tm // ROW_SUBTILE):
            rows = slice(r * ROW_SUBTILE, (r + 1) * ROW_SUBTILE)
            h = h_scr[rows, :]
            do = do_scr[rows, :]
            a = _dot_nt(h, wa)
            b = _dot_nt(h, wb)
            dact = _dot_nt(do, wo)
            sig = _sigmoid(a)
            s = a * sig
            da = (dact * b * (sig * (1.0 + a * (1.0 - sig)))).astype(BF16)
            db = (dact * s).astype(BF16)
            da_ref[0, rows, :] = da
            db_ref[0, rows, :] = db
            act_ref[0, rows, :] = (s * b).astype(BF16)
            dh_scr[rows, :] += _dot(da, wa) + _dot(db, wb)

        @pl.when(j == last)
        def _():
            dyv = dy_ref[...]
            xhat, r = _rms(x_ref[...])
            dx, dshift, dscale, dg = _modnorm_bwd(xhat, r, g_ref[...], cond_ref[1:2, :], dh_scr[...])
            dx_ref[...] = dyv + dx
            part_ref[0:1, :] += dshift
            part_ref[1:2, :] += dscale
            part_ref[2:3, :] += _sum0(0.5 * dyv * f_ref[...])
            part_ref[3:4, :] += dg

    tok = pl.BlockSpec((tm, D_MODEL), lambda i, j: (i, 0))
    chunk = pl.BlockSpec((1, tm, FF_PIECE), lambda i, j: (j, i, 0))
    chunk_shape = jax.ShapeDtypeStruct((N_FF_CHUNK, SEQ, FF_PIECE), BF16)
    return pl.pallas_call(
        body, name="ffn_bwd", grid=(SEQ // tm, N_FF_CHUNK),
        out_shape=(jax.ShapeDtypeStruct((SEQ, D_MODEL), F32), chunk_shape, chunk_shape, chunk_shape,
                   jax.ShapeDtypeStruct((SEQ, D_MODEL), BF16), jax.ShapeDtypeStruct((SEQ, D_MODEL), BF16),
                   jax.ShapeDtypeStruct((SUBLANES, D_MODEL), F32)),
        in_specs=[tok, tok, tok, _row_spec(3), _row_spec(1),
                  pl.BlockSpec((1, FF_PIECE, D_MODEL), lambda i, j: (j, 0, 0)),
                  pl.BlockSpec((1, FF_PIECE, D_MODEL), lambda i, j: (j + N_FF_CHUNK, 0, 0)),
                  pl.BlockSpec((1, FF_PIECE, D_MODEL), lambda i, j: (j, 0, 0))],
        out_specs=(tok, chunk, chunk, chunk, tok, tok, _row_spec(SUBLANES)),
        scratch_shapes=[pltpu.VMEM((tm, D_MODEL), BF16), pltpu.VMEM((tm, D_MODEL), BF16),
                        pltpu.VMEM((tm, D_MODEL), F32)],
        compiler_params=_params("arbitrary", "arbitrary"),
    )(dy, x, f, cond3, g, w_in_g, w_in_g, w_out_g)


def _dw(lhs_a, rhs, lhs_b=None, name="dw"):
    pa, s, m = lhs_a.shape
    nn = rhs.shape[-1]
    pb = 0 if lhs_b is None else lhs_b.shape[0]
    two = lhs_b is not None

    def body(*refs):
        if two:
            a_ref, b_ref, r_ref, o_ref = refs
            p = pl.program_id(0)

            @pl.when(p < pa)
            def _():
                o_ref[0] = _dot_tn(a_ref[0], r_ref[...]).astype(BF16)

            @pl.when(p >= pa)
            def _():
                o_ref[0] = _dot_tn(b_ref[0], r_ref[...]).astype(BF16)
        else:
            a_ref, r_ref, o_ref = refs
            o_ref[0] = _dot_tn(a_ref[0], r_ref[...]).astype(BF16)

    if two:
        in_specs = [pl.BlockSpec((1, s, m), lambda p: (jnp.minimum(p, pa - 1), 0, 0)),
                    pl.BlockSpec((1, s, m), lambda p: (jnp.maximum(p - pa, 0), 0, 0))]
        args = (lhs_a, lhs_b, rhs)
    else:
        in_specs = [pl.BlockSpec((1, s, m), lambda p: (p, 0, 0))]
        args = (lhs_a, rhs)
    in_specs.append(pl.BlockSpec((s, nn), lambda p: (0, 0)))
    return pl.pallas_call(
        body, name=name, grid=(pa + pb,),
        out_shape=jax.ShapeDtypeStruct((pa + pb, m, nn), BF16),
        in_specs=in_specs, out_specs=pl.BlockSpec((1, m, nn), lambda p: (p, 0, 0)),
        compiler_params=_params("arbitrary"),
    )(*args)


def _mix_in_fwd(x, cond2, g, w):
    tm = TOKEN_TILE

    def body(x_ref, cond_ref, g_ref, w_ref, z_ref):
        h, _, _ = _modnorm(x_ref[...], g_ref[...], cond_ref[0:1, :], cond_ref[1:2, :])
        z_ref[...] = _dot_nt(h.astype(BF16), w_ref[...])

    return pl.pallas_call(
        body, name="mix_in_fwd", grid=(SEQ // tm,),
        out_shape=jax.ShapeDtypeStruct((SEQ, P_IN), F32),
        in_specs=[pl.BlockSpec((tm, D_MODEL), lambda i: (i, 0)), _row_spec(2), _row_spec(1),
                  pl.BlockSpec((P_IN, D_MODEL), lambda i: (0, 0))],
        out_specs=pl.BlockSpec((tm, P_IN), lambda i: (i, 0)),
        compiler_params=_params("arbitrary"),
    )(x, cond2, g, w)


MIX_SLABS = ((0, 2 * W_GRP), (2 * W_GRP, 3 * W_GRP), (3 * W_GRP, 6 * W_GRP), (6 * W_GRP, 7 * W_GRP))


def _mix_in_bwd(dzs, x, dy, cond2, g, w):
    tm = TOKEN_TILE

    def body(dza_ref, dzb_ref, dzc_ref, dzd_ref, x_ref, dy_ref, cond_ref, g_ref, w_ref, dx_ref, h_ref, dzo_ref, part_ref):
        i = pl.program_id(0)

        @pl.when(i == 0)
        def _():
            part_ref[...] = jnp.zeros_like(part_ref)

        h, xhat, r = _modnorm(x_ref[...], g_ref[...], cond_ref[0:1, :], cond_ref[1:2, :])
        h_ref[...] = h.astype(BF16)
        dh = None
        for (lo, hi), d_ref in zip(MIX_SLABS, (dza_ref, dzb_ref, dzc_ref, dzd_ref)):
            dzb = d_ref[...].astype(BF16)
            dzo_ref[:, lo:hi] = dzb
            t = _dot(dzb, w_ref[lo:hi, :])
            dh = t if dh is None else dh + t
        dx, dshift, dscale, dg = _modnorm_bwd(xhat, r, g_ref[...], cond_ref[1:2, :], dh)
        dx_ref[...] = dy_ref[...] + dx
        part_ref[0:1, :] += dshift
        part_ref[1:2, :] += dscale
        part_ref[2:3, :] += dg

    tok = pl.BlockSpec((tm, D_MODEL), lambda i: (i, 0))
    ztok = pl.BlockSpec((tm, P_IN), lambda i: (i, 0))
    slabs = [pl.BlockSpec((tm, hi - lo), lambda i: (i, 0)) for lo, hi in MIX_SLABS]
    return pl.pallas_call(
        body, name="mix_in_bwd", grid=(SEQ // tm,),
        out_shape=(jax.ShapeDtypeStruct((SEQ, D_MODEL), F32), jax.ShapeDtypeStruct((SEQ, D_MODEL), BF16),
                   jax.ShapeDtypeStruct((SEQ, P_IN), BF16), jax.ShapeDtypeStruct((SUBLANES, D_MODEL), F32)),
        in_specs=[*slabs, tok, tok, _row_spec(2), _row_spec(1), pl.BlockSpec((P_IN, D_MODEL), lambda i: (0, 0))],
        out_specs=(tok, tok, ztok, _row_spec(SUBLANES)),
        compiler_params=_params("arbitrary"),
    )(*dzs, x, dy, cond2, g, w)


def _group_norm(ys, g_ref):
    out = []
    for k, y in enumerate(ys):
        yhat, r = _rms(y)
        out.append((yhat, r, g_ref[:, k * W_GRP:(k + 1) * W_GRP]))
    return out


def _mix_out_fwd(x, ys, g, gate, w):
    tm = TOKEN_TILE

    def body(x_ref, ya_ref, yb_ref, yc_ref, yd_ref, g_ref, gate_ref, w_ref, xo_ref):
        acc = None
        for k, (yhat, _, gk) in enumerate(_group_norm([r[...] for r in (ya_ref, yb_ref, yc_ref, yd_ref)], g_ref)):
            t = _dot((yhat * gk).astype(BF16), w_ref[k * W_GRP:(k + 1) * W_GRP, :])
            acc = t if acc is None else acc + t
        xo_ref[...] = x_ref[...] + gate_ref[...] * acc

    tok = pl.BlockSpec((tm, D_MODEL), lambda i: (i, 0))
    ytok = pl.BlockSpec((tm, W_GRP), lambda i: (i, 0))
    return pl.pallas_call(
        body, name="mix_out_fwd", grid=(SEQ // tm,),
        out_shape=jax.ShapeDtypeStruct((SEQ, D_MODEL), F32),
        in_specs=[tok, ytok, ytok, ytok, ytok, _row_spec(1), _row_spec(1),
                  pl.BlockSpec((D_MODEL, D_MODEL), lambda i: (0, 0))],
        out_specs=tok, compiler_params=_params("arbitrary"),
    )(x, *ys, g, gate, w)


def _mix_out_bwd(dy, ys, g, gate, w):
    tm = TOKEN_TILE

    def body(dy_ref, ya_ref, yb_ref, yc_ref, yd_ref, g_ref, gate_ref, w_ref,
             da_ref, db_ref, dc_ref, dd_ref, yn_ref, dmo_ref, part_ref):
        i = pl.program_id(0)

        @pl.when(i == 0)
        def _():
            part_ref[...] = jnp.zeros_like(part_ref)

        dyv = dy_ref[...]
        dmo = (gate_ref[...] * dyv).astype(BF16)
        dmo_ref[...] = dmo
        dyn = _dot_nt(dmo, w_ref[...])
        norms = _group_norm([r[...] for r in (ya_ref, yb_ref, yc_ref, yd_ref)], g_ref)
        mo = None
        for k, ((yhat, r, gk), o_ref) in enumerate(zip(norms, (da_ref, db_ref, dc_ref, dd_ref))):
            sl = slice(k * W_GRP, (k + 1) * W_GRP)
            ynk = (yhat * gk).astype(BF16)
            yn_ref[:, sl] = ynk
            t = _dot(ynk, w_ref[sl, :])
            mo = t if mo is None else mo + t
            dk = dyn[:, sl]
            o_ref[...] = _rms_bwd(yhat, r, dk * gk)
            part_ref[1:2, sl] += _sum0(dk * yhat)
        part_ref[0:1, :] += _sum0(dyv * mo)

    tok = pl.BlockSpec((tm, D_MODEL), lambda i: (i, 0))
    ytok = pl.BlockSpec((tm, W_GRP), lambda i: (i, 0))
    ysh = jax.ShapeDtypeStruct((SEQ, W_GRP), F32)
    return pl.pallas_call(
        body, name="mix_out_bwd", grid=(SEQ // tm,),
        out_shape=(ysh, ysh, ysh, ysh, jax.ShapeDtypeStruct((SEQ, D_MODEL), BF16),
                   jax.ShapeDtypeStruct((SEQ, D_MODEL), BF16), jax.ShapeDtypeStruct((SUBLANES, D_MODEL), F32)),
        in_specs=[tok, ytok, ytok, ytok, ytok, _row_spec(1), _row_spec(1),
                  pl.BlockSpec((D_MODEL, D_MODEL), lambda i: (0, 0))],
        out_specs=(ytok, ytok, ytok, ytok, tok, tok, _row_spec(SUBLANES)),
        compiler_params=_params("arbitrary"),
    )(dy, *ys, g, gate, w)


def _shift_down(v, k, rows):
    return jnp.where(rows >= k, pltpu.roll(v, k, axis=0), 0.0)


def _shift_up(v, k, rows):
    n = v.shape[0]
    return jnp.where(rows < n - k, pltpu.roll(v, n - k, axis=0), 0.0)


def _zslab(width, index):
    return pl.BlockSpec((SEQ, width), lambda *_: (0, index))


def _full(shape):
    return pl.BlockSpec(shape, lambda *_: (0,) * len(shape))


def _head_avg():
    r = lax.broadcasted_iota(jnp.int32, (W_GRP, W_GRP), 0) // SGU_HEAD_DIM
    c = lax.broadcasted_iota(jnp.int32, (W_GRP, W_GRP), 1) // SGU_HEAD_DIM
    return jnp.where(r == c, 1.0 / SGU_HEAD_DIM, 0.0).astype(F32)


def _sgu_norm(za):
    z = _gelu(za)
    u, v = z[:, :W_GRP], z[:, W_GRP:]
    avg = _head_avg()
    vc = v - _dot_hi(v, avg)
    rstd = lax.rsqrt(_dot_hi(vc * vc, avg) + EPS)
    return u, vc * rstd, rstd


def _sgu_masked_w(w_ref):
    t = lax.broadcasted_iota(jnp.int32, (CHUNK, CHUNK), 0)
    s = lax.broadcasted_iota(jnp.int32, (CHUNK, CHUNK), 1)
    tril = t >= s
    return [jnp.where(tril, w_ref[:, h * CHUNK:(h + 1) * CHUNK], 0.0).astype(BF16) for h in range(SGU_HEADS)]


def _head_of_lane():
    return lax.broadcasted_iota(jnp.int32, (CHUNK, W_GRP), 1) // SGU_HEAD_DIM


def _sgu_fwd(z, w_cat, bias):
    def body(z_ref, w_ref, b_ref, y_ref, vn_scr, u_scr):
        u, vn, _ = _sgu_norm(z_ref[...])
        vn_scr[...] = vn.astype(BF16)
        u_scr[...] = u
        ws = _sgu_masked_w(w_ref)
        head = _head_of_lane()
        bias_v = b_ref[...]

        def chunk(n, carry):
            rows = pl.ds(pl.multiple_of(n * CHUNK, CHUNK), CHUNK)
            vb = vn_scr[rows, :]
            mixed = bias_v
            for h in range(SGU_HEADS):
                mixed = mixed + jnp.where(head == h, _dot(ws[h], vb), 0.0)
            y_ref[rows, :] = u_scr[rows, :] * mixed
            return carry

        lax.fori_loop(0, SEQ // CHUNK, chunk, 0)

    return pl.pallas_call(
        body, name="sgu_fwd", grid=(1,),
        out_shape=jax.ShapeDtypeStruct((SEQ, W_GRP), F32),
        in_specs=[_zslab(2 * W_GRP, 0), _full((CHUNK, SGU_HEADS * CHUNK)), _full((CHUNK, W_GRP))],
        out_specs=_full((SEQ, W_GRP)),
        scratch_shapes=[pltpu.VMEM((SEQ, W_GRP), BF16), pltpu.VMEM((SEQ, W_GRP), F32)],
        compiler_params=_params("arbitrary"),
    )(z, w_cat, bias)


def _sgu_bwd(z, dy, w_cat, bias):
    def body(z_ref, dy_ref, w_ref, b_ref, dz_ref, dw_ref, db_ref, vn_scr, u_scr, dvn_scr, du_scr):
        za = z_ref[...]
        u, vn, rstd = _sgu_norm(za)
        vn_scr[...] = vn.astype(BF16)
        u_scr[...] = u
        ws = _sgu_masked_w(w_ref)
        head = _head_of_lane()
        bias_v = b_ref[...]

        def chunk(n, carry):
            dws, dbias = carry
            rows = pl.ds(pl.multiple_of(n * CHUNK, CHUNK), CHUNK)
            vb = vn_scr[rows, :]
            mixed = bias_v
            for h in range(SGU_HEADS):
                mixed = mixed + jnp.where(head == h, _dot(ws[h], vb), 0.0)
            dyc = dy_ref[rows, :]
            du_scr[rows, :] = dyc * mixed
            dmixed = dyc * u_scr[rows, :]
            dvn = jnp.zeros((CHUNK, W_GRP), F32)
            new_dws = []
            for h in range(SGU_HEADS):
                dm = jnp.where(head == h, dmixed, 0.0).astype(BF16)
                new_dws.append(dws[h] + _dot_nt(dm, vb))
                dvn = dvn + _dot_tn(ws[h], dm)
            dvn_scr[rows, :] = dvn
            return tuple(new_dws), dbias + dmixed

        zero_w = tuple(jnp.zeros((CHUNK, CHUNK), F32) for _ in range(SGU_HEADS))
        dws, dbias = lax.fori_loop(0, SEQ // CHUNK, chunk, (zero_w, jnp.zeros((CHUNK, W_GRP), F32)))
        t = lax.broadcasted_iota(jnp.int32, (CHUNK, CHUNK), 0)
        s = lax.broadcasted_iota(jnp.int32, (CHUNK, CHUNK), 1)
        for h in range(SGU_HEADS):
            dw_ref[:, h * CHUNK:(h + 1) * CHUNK] = jnp.where(t >= s, dws[h], 0.0)
        avg = _head_avg()
        db_ref[...] = _dot_hi(dbias, avg) * float(SGU_HEAD_DIM)
        dvn = dvn_scr[...]
        dv = rstd * (dvn - _dot_hi(dvn, avg) - vn * _dot_hi(dvn * vn, avg))
        gg = _gelu_grad(za)
        dz_ref[:, :W_GRP] = gg[:, :W_GRP] * du_scr[...]
        dz_ref[:, W_GRP:] = gg[:, W_GRP:] * dv

    return pl.pallas_call(
        body, name="sgu_bwd", grid=(1,),
        out_shape=(jax.ShapeDtypeStruct((SEQ, 2 * W_GRP), F32), jax.ShapeDtypeStruct((CHUNK, SGU_HEADS * CHUNK), F32),
                   jax.ShapeDtypeStruct((CHUNK, W_GRP), F32)),
        in_specs=[_zslab(2 * W_GRP, 0), _full((SEQ, W_GRP)), _full((CHUNK, SGU_HEADS * CHUNK)), _full((CHUNK, W_GRP))],
        out_specs=(_full((SEQ, 2 * W_GRP)), _full((CHUNK, SGU_HEADS * CHUNK)), _full((CHUNK, W_GRP))),
        scratch_shapes=[pltpu.VMEM((SEQ, W_GRP), BF16), pltpu.VMEM((SEQ, W_GRP), F32),
                        pltpu.VMEM((SEQ, W_GRP), F32), pltpu.VMEM((SEQ, W_GRP), F32)],
        compiler_params=_params("arbitrary"),
    )(z, dy, w_cat, bias)


def _pool_window_of_lane(shape):
    grp = lax.broadcasted_iota(jnp.int32, shape, 1) // POOL_GROUP_DIM
    win = jnp.full(shape, POOL_WINDOWS[0], jnp.int32)
    for k in range(1, len(POOL_WINDOWS)):
        win = jnp.where(grp == k, POOL_WINDOWS[k], win)
    return grp, win


def _pool_select(levels, grp):
    out = levels[0]
    for k in range(1, len(levels)):
        out = jnp.where(grp == k, levels[k], out)
    return out


def _pool_p(z):
    shape = z.shape
    rows = lax.broadcasted_iota(jnp.int32, shape, 0)
    grp, win = _pool_window_of_lane(shape)
    levels, s, k = [], z, 1
    for _ in POOL_WINDOWS:
        s = s + _shift_down(s, k, rows)
        levels.append(s)
        k *= 2
    inv = 1.0 / jnp.minimum(rows + 1, win).astype(F32)
    return _pool_select(levels, grp) * inv - z, inv, rows, grp


def _pool_fwd(z, w_bd, scale):
    def body(z_ref, w_ref, s_ref, y_ref):
        p, _, _, _ = _pool_p(z_ref[...])
        y_ref[...] = _dot(p.astype(BF16), w_ref[...]) * s_ref[...]

    return pl.pallas_call(
        body, name="pool_fwd", grid=(1,),
        out_shape=jax.ShapeDtypeStruct((SEQ, W_GRP), F32),
        in_specs=[_zslab(W_GRP, 2), _full((W_GRP, W_GRP)), _full((1, W_GRP))],
        out_specs=_full((SEQ, W_GRP)), compiler_params=_params("arbitrary"),
    )(z, w_bd, scale)


def _pool_bwd(z, dy, w_bd, scale):
    def body(z_ref, dy_ref, w_ref, s_ref, dz_ref, dw_ref, ds_ref):
        p, inv, rows, grp = _pool_p(z_ref[...])
        pb = p.astype(BF16)
        dyv = dy_ref[...]
        ds_ref[...] = _sum0(dyv * _dot(pb, w_ref[...]))
        dpre = (dyv * s_ref[...]).astype(BF16)
        dw_ref[...] = _dot_tn(pb, dpre)
        dp = _dot_nt(dpre, w_ref[...])
        q = dp * inv
        levels, s, k = [], q, 1
        for _ in POOL_WINDOWS:
            s = s + _shift_up(s, k, rows)
            levels.append(s)
            k *= 2
        dz_ref[...] = _pool_select(levels, grp) - dp

    return pl.pallas_call(
        body, name="pool_bwd", grid=(1,),
        out_shape=(jax.ShapeDtypeStruct((SEQ, W_GRP), F32), jax.ShapeDtypeStruct((W_GRP, W_GRP), F32),
                   jax.ShapeDtypeStruct((1, W_GRP), F32)),
        in_specs=[_zslab(W_GRP, 2), _full((SEQ, W_GRP)), _full((W_GRP, W_GRP)), _full((1, W_GRP))],
        out_specs=(_full((SEQ, W_GRP)), _full((W_GRP, W_GRP)), _full((1, W_GRP))),
        compiler_params=_params("arbitrary"),
    )(z, dy, w_bd, scale)


def _conv_fwd(z, w):
    def body(z_ref, w_ref, y_ref):
        zc = z_ref[...]
        bg, cg, xh = zc[:, :W_GRP], zc[:, W_GRP:2 * W_GRP], zc[:, 2 * W_GRP:]
        rows = lax.broadcasted_iota(jnp.int32, (SEQ, W_GRP), 0)
        y = cg * xh
        conv = w_ref[0:1, :] * _shift_down(y, 2, rows) + w_ref[1:2, :] * _shift_down(y, 1, rows) + w_ref[2:3, :] * y
        y_ref[...] = bg * conv

    return pl.pallas_call(
        body, name="conv_fwd", grid=(1,),
        out_shape=jax.ShapeDtypeStruct((SEQ, W_GRP), F32),
        in_specs=[_zslab(3 * W_GRP, 1), _full((3, W_GRP))],
        out_specs=_full((SEQ, W_GRP)), compiler_params=_params("arbitrary"),
    )(z, w)


def _conv_bwd(z, dy, w):
    def body(z_ref, dy_ref, w_ref, dz_ref, dw_ref):
        zc = z_ref[...]
        bg, cg, xh = zc[:, :W_GRP], zc[:, W_GRP:2 * W_GRP], zc[:, 2 * W_GRP:]
        rows = lax.broadcasted_iota(jnp.int32, (SEQ, W_GRP), 0)
        y = cg * xh
        y2, y1 = _shift_down(y, 2, rows), _shift_down(y, 1, rows)
        conv = w_ref[0:1, :] * y2 + w_ref[1:2, :] * y1 + w_ref[2:3, :] * y
        dyv = dy_ref[...]
        dconv = dyv * bg
        dw_ref[...] = jnp.zeros_like(dw_ref)
        dw_ref[0:1, :] = _sum0(dconv * y2)
        dw_ref[1:2, :] = _sum0(dconv * y1)
        dw_ref[2:3, :] = _sum0(dconv * y)
        dyy = (w_ref[0:1, :] * _shift_up(dconv, 2, rows) + w_ref[1:2, :] * _shift_up(dconv, 1, rows)
               + w_ref[2:3, :] * dconv)
        dz_ref[:, :W_GRP] = dyv * conv
        dz_ref[:, W_GRP:2 * W_GRP] = dyy * xh
        dz_ref[:, 2 * W_GRP:] = dyy * cg

    return pl.pallas_call(
        body, name="conv_bwd", grid=(1,),
        out_shape=(jax.ShapeDtypeStruct((SEQ, 3 * W_GRP), F32), jax.ShapeDtypeStruct((SUBLANES, W_GRP), F32)),
        in_specs=[_zslab(3 * W_GRP, 1), _full((SEQ, W_GRP)), _full((3, W_GRP))],
        out_specs=(_full((SEQ, 3 * W_GRP)), _full((SUBLANES, W_GRP))),
        compiler_params=_params("arbitrary"),
    )(z, dy, w)


def _s5_disc(lre, lim, ldt, br, bi):
    dt = jnp.exp(ldt)
    mag = jnp.exp(lre * dt)
    ang = lim * dt
    a_re, a_im = mag * jnp.cos(ang), mag * jnp.sin(ang)
    nr, ni = a_re - 1.0, a_im
    den = lre * lre + lim * lim
    k_re = (nr * lre + ni * lim) / den
    k_im = (ni * lre - nr * lim) / den
    return a_re, a_im, k_re * br - k_im * bi, k_re * bi + k_im * br


def _s5_prep_fwd(lre, lim, ldt, br, bi):
    def body(lre_ref, lim_ref, ldt_ref, br_ref, bi_ref, ar_ref, ai_ref, bbr_ref, bbi_ref):
        ar, ai, bbr, bbi = _s5_disc(lre_ref[...], lim_ref[...], ldt_ref[...], br_ref[...], bi_ref[...])
        ar_ref[...] = ar
        ai_ref[...] = ai
        bbr_ref[...] = bbr
        bbi_ref[...] = bbi

    return pl.pallas_call(
        body, name="s5_prep_fwd",
        out_shape=(jax.ShapeDtypeStruct(lre.shape, F32), jax.ShapeDtypeStruct(lre.shape, F32),
                   jax.ShapeDtypeStruct(br.shape, F32), jax.ShapeDtypeStruct(br.shape, F32)),
        compiler_params=_params(),
    )(lre, lim, ldt, br, bi)


def _s5_prep_bwd(lre, lim, ldt, br, bi, dar, dai, dbbr, dbbi):
    def body(lre_ref, lim_ref, ldt_ref, br_ref, bi_ref, dar_ref, dai_ref, dbbr_ref, dbbi_ref,
             o_lre, o_lim, o_ldt, o_br, o_bi):
        _, pull = jax.vjp(_s5_disc, lre_ref[...], lim_ref[...], ldt_ref[...], br_ref[...], bi_ref[...])
        g = pull((dar_ref[...], dai_ref[...], dbbr_ref[...], dbbi_ref[...]))
        for o, v in zip((o_lre, o_lim, o_ldt, o_br, o_bi), g):
            o[...] = v

    return pl.pallas_call(
        body, name="s5_prep_bwd",
        out_shape=tuple(jax.ShapeDtypeStruct(a.shape, F32) for a in (lre, lim, ldt, br, bi)),
        compiler_params=_params(),
    )(lre, lim, ldt, br, bi, dar, dai, dbbr, dbbi)


def _cmul(ar, ai, br, bi):
    return ar * br - ai * bi, ar * bi + ai * br


def _s5_tile_consts(ar, ai, reverse):
    if reverse:
        ai = -ai
    shape = (SUBLANES, S5_BLOCK)
    row = lax.broadcasted_iota(jnp.int32, shape, 0)
    a1 = (jnp.broadcast_to(ar, shape), jnp.broadcast_to(ai, shape))
    a2 = _cmul(*a1, *a1)
    a4 = _cmul(*a2, *a2)
    a8 = _cmul(*a4, *a4)
    steps = []
    for s, (pr, pi) in ((1, a1), (2, a2), (4, a4)):
        keep = (row < SUBLANES - s) if reverse else (row >= s)
        steps.append((s, jnp.where(keep, pr, 0.0), jnp.where(keep, pi, 0.0)))
    e = (SUBLANES - row) if reverse else (row + 1)
    pr, pi = jnp.ones(shape, F32), jnp.zeros(shape, F32)
    for bit, (qr, qi) in ((1, a1), (2, a2), (4, a4), (8, a8)):
        nr, ni = _cmul(pr, pi, qr, qi)
        hit = (e & bit) != 0
        pr, pi = jnp.where(hit, nr, pr), jnp.where(hit, ni, pi)
    return steps, pr, pi


def _s5_tile(xr, xi, steps, reverse):
    for s, pr, pi in steps:
        sh = SUBLANES - s if reverse else s
        sr, si = pltpu.roll(xr, sh, axis=0), pltpu.roll(xi, sh, axis=0)
        xr, xi = xr + pr * sr - pi * si, xi + pr * si + pi * sr
    return xr, xi


N_TILES = SEQ // SUBLANES


def _s5_fwd(z, b_re, b_im, c_re, c_im, a_re, a_im, d, glu_w, glu_b):
    nblk = S5_LANES // S5_BLOCK

    def body(u_ref, br_ref, bi_ref, cr_ref, ci_ref, ar_ref, ai_ref, d_ref, gw_ref, gb_ref,
             y_ref, y0_ref, xr_ref, xi_ref, ub_scr, acc_scr):
        jb = pl.program_id(0)

        @pl.when(jb == 0)
        def _():
            ub_scr[...] = u_ref[...].astype(BF16)
            acc_scr[...] = jnp.zeros_like(acc_scr)

        ub = ub_scr[...]
        xr_ref[...] = _dot(ub, br_ref[...])
        xi_ref[...] = _dot(ub, bi_ref[...])
        steps, pr, pi = _s5_tile_consts(ar_ref[...], ai_ref[...], False)

        def tile(t, carry):
            cr, ci = carry
            rows = pl.ds(pl.multiple_of(t * SUBLANES, SUBLANES), SUBLANES)
            xr, xi = _s5_tile(xr_ref[rows, :], xi_ref[rows, :], steps, False)
            xr, xi = xr + pr * cr - pi * ci, xi + pr * ci + pi * cr
            xr_ref[rows, :] = xr
            xi_ref[rows, :] = xi
            return xr[SUBLANES - 1:, :], xi[SUBLANES - 1:, :]

        zero = jnp.zeros((1, S5_BLOCK), F32)
        lax.fori_loop(0, N_TILES, tile, (zero, zero), unroll=2)
        acc_scr[...] += (_dot(xr_ref[...].astype(BF16), cr_ref[...]) - _dot(xi_ref[...].astype(BF16), ci_ref[...]))

        @pl.when(jb == nblk - 1)
        def _():
            y0 = acc_scr[...] + d_ref[...] * u_ref[...]
            y0_ref[...] = y0
            y1 = _gelu(y0)
            y_ref[...] = y1 * _sigmoid(_dot(y1.astype(BF16), gw_ref[...]) + gb_ref[...])

    lane_blk = pl.BlockSpec((SEQ, S5_BLOCK), lambda j: (0, j))
    return pl.pallas_call(
        body, name="s5_fwd", grid=(nblk,),
        out_shape=(jax.ShapeDtypeStruct((SEQ, W_GRP), F32), jax.ShapeDtypeStruct((SEQ, W_GRP), F32),
                   jax.ShapeDtypeStruct((SEQ, S5_LANES), F32), jax.ShapeDtypeStruct((SEQ, S5_LANES), F32)),
        in_specs=[_zslab(W_GRP, 6),
                  pl.BlockSpec((W_GRP, S5_BLOCK), lambda j: (0, j)), pl.BlockSpec((W_GRP, S5_BLOCK), lambda j: (0, j)),
                  pl.BlockSpec((S5_BLOCK, W_GRP), lambda j: (j, 0)), pl.BlockSpec((S5_BLOCK, W_GRP), lambda j: (j, 0)),
                  pl.BlockSpec((1, S5_BLOCK), lambda j: (0, j)), pl.BlockSpec((1, S5_BLOCK), lambda j: (0, j)),
                  _full((1, W_GRP)), _full((W_GRP, W_GRP)), _full((1, W_GRP))],
        out_specs=(_full((SEQ, W_GRP)), _full((SEQ, W_GRP)), lane_blk, lane_blk),
        scratch_shapes=[pltpu.VMEM((SEQ, W_GRP), BF16), pltpu.VMEM((SEQ, W_GRP), F32)],
        compiler_params=_params("arbitrary"),
    )(z, b_re, b_im, c_re, c_im, a_re, a_im, d, glu_w, glu_b)


def _s5_bwd(z, y0, dy, xr, xi, b_re, b_im, c_re, c_im, a_re, a_im, d, glu_w, glu_b):
    nblk = S5_LANES // S5_BLOCK

    def body(u_ref, y0_ref, dy_ref, xr_ref, xi_ref, br_ref, bi_ref, cr_ref, ci_ref, ar_ref, ai_ref,
             d_ref, gw_ref, gb_ref,
             du_ref, dbr_ref, dbi_ref, dcr_ref, dci_ref, dar_ref, dai_ref, dd_ref, dgw_ref, dgb_ref,
             ub_scr, dy0_scr, du_scr, lr_scr, li_scr):
        jb = pl.program_id(0)

        @pl.when(jb == 0)
        def _():
            u = u_ref[...]
            y0v = y0_ref[...]
            y1 = _gelu(y0v)
            y1b = y1.astype(BF16)
            sg = _sigmoid(_dot(y1b, gw_ref[...]) + gb_ref[...])
            dyv = dy_ref[...]
            dpre = dyv * y1 * sg * (1.0 - sg)
            dpb = dpre.astype(BF16)
            dgw_ref[...] = _dot_tn(y1b, dpb)
            dgb_ref[...] = _sum0(dpre)
            dy1 = dyv * sg + _dot_nt(dpb, gw_ref[...])
            dy0 = dy1 * _gelu_grad(y0v)
            dd_ref[...] = _sum0(dy0 * u)
            du_scr[...] = dy0 * d_ref[...]
            dy0_scr[...] = dy0.astype(BF16)
            ub_scr[...] = u.astype(BF16)

        dy0b = dy0_scr[...]
        lr_scr[...] = _dot_nt(dy0b, cr_ref[...])
        li_scr[...] = -_dot_nt(dy0b, ci_ref[...])
        dcr_ref[...] = _dot_tn(xr_ref[...].astype(BF16), dy0b)
        dci_ref[...] = -_dot_tn(xi_ref[...].astype(BF16), dy0b)
        steps, pr, pi = _s5_tile_consts(ar_ref[...], ai_ref[...], True)
        row = lax.broadcasted_iota(jnp.int32, (SUBLANES, S5_BLOCK), 0)

        def tile(k, carry):
            cr, ci, accr, acci = carry
            t = N_TILES - 1 - k
            rows = pl.ds(pl.multiple_of(t * SUBLANES, SUBLANES), SUBLANES)
            lr, li = _s5_tile(lr_scr[rows, :], li_scr[rows, :], steps, True)
            lr, li = lr + pr * cr - pi * ci, li + pr * ci + pi * cr
            lr_scr[rows, :] = lr
            li_scr[rows, :] = li
            prev = pl.ds(pl.multiple_of(jnp.maximum(t - 1, 0) * SUBLANES, SUBLANES), SUBLANES)
            live = jnp.where(t > 0, 1.0, 0.0)
            xpr = jnp.where(row == 0, pltpu.roll(xr_ref[prev, :], 1, axis=0) * live, pltpu.roll(xr_ref[rows, :], 1, axis=0))
            xpi = jnp.where(row == 0, pltpu.roll(xi_ref[prev, :], 1, axis=0) * live, pltpu.roll(xi_ref[rows, :], 1, axis=0))
            accr = accr + lr * xpr + li * xpi
            acci = acci + li * xpr - lr * xpi
            return lr[0:1, :], li[0:1, :], accr, acci

        zero = jnp.zeros((1, S5_BLOCK), F32)
        zt = jnp.zeros((SUBLANES, S5_BLOCK), F32)
        _, _, accr, acci = lax.fori_loop(0, N_TILES, tile, (zero, zero, zt, zt), unroll=2)
        dar_ref[...] = jnp.zeros_like(dar_ref)
        dai_ref[...] = jnp.zeros_like(dai_ref)
        dar_ref[0:1, :] = _sum0(accr)
        dai_ref[0:1, :] = _sum0(acci)
        lrb, lib = lr_scr[...].astype(BF16), li_scr[...].astype(BF16)
        ub = ub_scr[...]
        dbr_ref[...] = _dot_tn(ub, lrb)
        dbi_ref[...] = _dot_tn(ub, lib)
        du_scr[...] += _dot_nt(lrb, br_ref[...]) + _dot_nt(lib, bi_ref[...])

        @pl.when(jb == nblk - 1)
        def _():
            du_ref[...] = du_scr[...]

    lane_blk = pl.BlockSpec((SEQ, S5_BLOCK), lambda j: (0, j))
    bspec = pl.BlockSpec((W_GRP, S5_BLOCK), lambda j: (0, j))
    cspec = pl.BlockSpec((S5_BLOCK, W_GRP), lambda j: (j, 0))
    aspec = pl.BlockSpec((1, S5_BLOCK), lambda j: (0, j))
    a8spec = pl.BlockSpec((SUBLANES, S5_BLOCK), lambda j: (0, j))
    sd = jax.ShapeDtypeStruct
    return pl.pallas_call(
        body, name="s5_bwd", grid=(nblk,),
        out_shape=(sd((SEQ, W_GRP), F32), sd((W_GRP, S5_LANES), F32), sd((W_GRP, S5_LANES), F32),
                   sd((S5_LANES, W_GRP), F32), sd((S5_LANES, W_GRP), F32),
                   sd((SUBLANES, S5_LANES), F32), sd((SUBLANES, S5_LANES), F32),
                   sd((1, W_GRP), F32), sd((W_GRP, W_GRP), F32), sd((1, W_GRP), F32)),
        in_specs=[_zslab(W_GRP, 6), _full((SEQ, W_GRP)), _full((SEQ, W_GRP)), lane_blk, lane_blk,
                  bspec, bspec, cspec, cspec, aspec, aspec,
                  _full((1, W_GRP)), _full((W_GRP, W_GRP)), _full((1, W_GRP))],
        out_specs=(_full((SEQ, W_GRP)), bspec, bspec, cspec, cspec, a8spec, a8spec,
                   _full((1, W_GRP)), _full((W_GRP, W_GRP)), _full((1, W_GRP))),
        scratch_shapes=[pltpu.VMEM((SEQ, W_GRP), BF16), pltpu.VMEM((SEQ, W_GRP), BF16), pltpu.VMEM((SEQ, W_GRP), F32),
                        pltpu.VMEM((SEQ, S5_BLOCK), F32), pltpu.VMEM((SEQ, S5_BLOCK), F32)],
        compiler_params=_params("arbitrary"),
    )(z, y0, dy, xr, xi, b_re, b_im, c_re, c_im, a_re, a_im, d, glu_w, glu_b)


def _head(x, g, target):
    tm = TOKEN_TILE
    n = SEQ // tm

    def body(x_ref, g_ref, t_ref, dx_ref, st_ref, acc_scr):
        i = pl.program_id(0)

        @pl.when(i == 0)
        def _():
            acc_scr[...] = jnp.zeros_like(acc_scr)

        xhat, r = _rms(x_ref[...])
        gv = g_ref[...]
        err = xhat * gv - t_ref[...]
        dyv = err * (1.0 / D_MODEL)
        dx_ref[...] = _rms_bwd(xhat, r, dyv * gv)
        acc_scr[0:1, :] += _sum0(err * err)
        acc_scr[1:2, :] += _sum0(dyv * xhat)

        @pl.when(i == n - 1)
        def _():
            st_ref[...] = acc_scr[...]
            tot = jnp.sum(acc_scr[0:1, :], axis=-1, keepdims=True) * (0.5 / D_MODEL)
            st_ref[0:1, :] = jnp.broadcast_to(tot, (1, D_MODEL))

    tok = pl.BlockSpec((tm, D_MODEL), lambda i: (i, 0))
    return pl.pallas_call(
        body, name="head", grid=(n,),
        out_shape=(jax.ShapeDtypeStruct((SEQ, D_MODEL), F32), jax.ShapeDtypeStruct((SUBLANES, D_MODEL), F32)),
        in_specs=[tok, _row_spec(1), tok], out_specs=(tok, _row_spec(SUBLANES)),
        scratch_shapes=[pltpu.VMEM((SUBLANES, D_MODEL), F32)],
        compiler_params=_params("arbitrary"),
    )(x, g, target)


def _adamw(w, gparts, m, v, name):
    r, c = w.shape
    npart = gparts.shape[0]
    tr = r
    for cand in (512, 256, 128, 64, 32, 16):
        if r % cand == 0 and r > cand:
            tr = cand
            break
    b1c = 1.0 - ADAM_B1 ** ADAM_STEP
    b2c = 1.0 - ADAM_B2 ** ADAM_STEP

    def body(w_ref, g_ref, m_ref, v_ref, go_ref, d_ref, mo_ref, vo_ref):
        g = g_ref[0].astype(F32)
        for k in range(1, npart):
            g = g + g_ref[k].astype(F32)
        mn = ADAM_B1 * m_ref[...] + (1.0 - ADAM_B1) * g
        vn = ADAM_B2 * v_ref[...] + (1.0 - ADAM_B2) * (g * g)
        m_hat = mn / b1c
        v_hat = vn / b2c
        go_ref[...] = g
        d_ref[...] = -ADAM_LR * (m_hat / (jnp.sqrt(v_hat) + ADAM_EPS) + ADAM_WD * w_ref[...])
        mo_ref[...] = mn
        vo_ref[...] = vn

    blk = pl.BlockSpec((tr, c), lambda i: (i, 0))
    sh = jax.ShapeDtypeStruct((r, c), F32)
    return pl.pallas_call(
        body, name=name, grid=(r // tr,),
        out_shape=(sh, sh, sh, sh),
        in_specs=[blk, pl.BlockSpec((npart, tr, c), lambda i: (0, i, 0)), blk, blk],
        out_specs=(blk, blk, blk, blk), compiler_params=_params("arbitrary"),
    )(w, gparts, m, v)


def _adamw_layer(l, w, gparts, m, v, prev, name):
    _, r, c = w.shape
    npart = gparts.shape[0]
    tr = max(t for t in range(16, 513, 16) if r % t == 0)
    b1c = 1.0 - ADAM_B1 ** ADAM_STEP
    b2c = 1.0 - ADAM_B2 ** ADAM_STEP
    nprev = 0 if prev is None else 4

    def body(*refs):
        w_ref, g_ref, m_ref, v_ref = refs[:4]
        go_ref, d_ref, mo_ref, vo_ref = refs[4 + nprev:]
        g = g_ref[0].astype(F32)
        for k in range(1, npart):
            g = g + g_ref[k].astype(F32)
        mn = ADAM_B1 * m_ref[0] + (1.0 - ADAM_B1) * g
        vn = ADAM_B2 * v_ref[0] + (1.0 - ADAM_B2) * (g * g)
        go_ref[0] = g
        d_ref[0] = -ADAM_LR * ((mn / b1c) / (jnp.sqrt(vn / b2c) + ADAM_EPS) + ADAM_WD * w_ref[0])
        mo_ref[0] = mn
        vo_ref[0] = vn

    blk = pl.BlockSpec((1, tr, c), lambda i: (l, i, 0))
    sh = jax.ShapeDtypeStruct(w.shape, F32)
    keep = [pl.BlockSpec(memory_space=pl.ANY)] * nprev
    return pl.pallas_call(
        body, name=name, grid=(r // tr,),
        out_shape=(sh, sh, sh, sh),
        in_specs=[blk, pl.BlockSpec((npart, tr, c), lambda i: (0, i, 0)), blk, blk, *keep],
        out_specs=(blk, blk, blk, blk),
        input_output_aliases={4 + k: k for k in range(nprev)},
        compiler_params=_params("arbitrary"),
    )(w, gparts, m, v, *(prev or ()))


def _sum_parts(parts, name):
    n, r, c = parts.shape

    def body(p_ref, o_ref):
        acc = p_ref[0]
        for k in range(1, n):
            acc = acc + p_ref[k]
        o_ref[...] = acc

    return pl.pallas_call(
        body, name=name, out_shape=jax.ShapeDtypeStruct((r, c), F32), compiler_params=_params(),
    )(parts)


def _block_diag(blocks):
    g, a, b = blocks.shape
    eye = jnp.eye(g, dtype=blocks.dtype)
    return (blocks[:, :, None, :] * eye[:, None, :, None]).reshape(g * a, g * b)


def _diag_blocks(dense, g):
    a, b = dense.shape[0] // g, dense.shape[1] // g
    d4 = dense.reshape(g, a, g, b)
    eye = jnp.eye(g, dtype=dense.dtype)
    return jnp.sum(d4 * eye[:, None, :, None], axis=2)


def _pack(parts, cols):
    flat = jnp.concatenate([p.reshape(-1) for p in parts])
    unit = N_DEV * SUBLANES * cols
    total = -(-flat.shape[0] // unit) * unit
    flat = jnp.pad(flat, (0, total - flat.shape[0]))
    return flat.reshape(N_DEV, total // (N_DEV * cols), cols)


def _unpack(flat, shapes):
    out, pos = [], 0
    for s in shapes:
        n = math.prod(s)
        out.append(flat[pos:pos + n].reshape(s))
        pos += n
    return out


SMALL = ("norm1_g", "norm2_g", "sgu_w", "sgu_b", "pool_w", "pool_scale", "conv_w", "s5_lambda_re", "s5_lambda_im",
         "s5_b_re", "s5_b_im", "s5_c_re", "s5_c_im", "s5_d", "s5_log_dt", "s5_glu_w", "s5_glu_b", "mix_norm_g",
         "norm3_g", "final_norm_g")
BIG = ("ffn1_w_in", "ffn1_w_out", "w_mix_in", "w_mix_out", "ffn2_w_in", "ffn2_w_out")
TRANSPOSED = ("ffn1_w_in", "w_mix_in", "ffn2_w_in")
WEIGHTS = ("ada_w", "ada_b", "norm1_g", "ffn1_w_in", "ffn1_w_out", "norm2_g", "w_mix_in", "sgu_w", "sgu_b", "pool_w",
           "pool_scale", "conv_w", "s5_lambda_re", "s5_lambda_im", "s5_b_re", "s5_b_im", "s5_c_re", "s5_c_im", "s5_d",
           "s5_log_dt", "s5_glu_w", "s5_glu_b", "mix_norm_g", "w_mix_out", "norm3_g", "ffn2_w_in", "ffn2_w_out",
           "final_norm_g")
PACK_COLS = 1024


def kernel(x, c, ada_w, ada_b, norm1_g, ffn1_w_in, ffn1_w_out, norm2_g, w_mix_in, sgu_w, sgu_b, pool_w, pool_scale, conv_w, s5_lambda_re, s5_lambda_im, s5_b_re, s5_b_im, s5_c_re, s5_c_im, s5_d, s5_log_dt, s5_glu_w, s5_glu_b, mix_norm_g, w_mix_out, norm3_g, ffn2_w_in, ffn2_w_out, final_norm_g, loss_target, m_ada_w, m_ada_b, m_norm1_g, m_ffn1_w_in, m_ffn1_w_out, m_norm2_g, m_w_mix_in, m_sgu_w, m_sgu_b, m_pool_w, m_pool_scale, m_conv_w, m_s5_lambda_re, m_s5_lambda_im, m_s5_b_re, m_s5_b_im, m_s5_c_re, m_s5_c_im, m_s5_d, m_s5_log_dt, m_s5_glu_w, m_s5_glu_b, m_mix_norm_g, m_w_mix_out, m_norm3_g, m_ffn2_w_in, m_ffn2_w_out, m_final_norm_g, v_ada_w, v_ada_b, v_norm1_g, v_ffn1_w_in, v_ffn1_w_out, v_norm2_g, v_w_mix_in, v_sgu_w, v_sgu_b, v_pool_w, v_pool_scale, v_conv_w, v_s5_lambda_re, v_s5_lambda_im, v_s5_b_re, v_s5_b_im, v_s5_c_re, v_s5_c_im, v_s5_d, v_s5_log_dt, v_s5_glu_w, v_s5_glu_b, v_mix_norm_g, v_w_mix_out, v_norm3_g, v_ffn2_w_in, v_ffn2_w_out, v_final_norm_g):
    args = dict(locals())
    W = {n: args[n] for n in WEIGHTS}
    M = {n: args["m_" + n] for n in WEIGHTS}
    V = {n: args["v_" + n] for n in WEIGHTS}
    me = _me()
    L = DEPTH
    x0 = x[0]
    target = loss_target[0]

    conv_cols = conv_w.shape[-1]
    glu_rows = s5_glu_w.shape[1]
    c_g, conv_g, glu_g = _exchange(
        [c.reshape(SUBLANES, LANES), conv_w.reshape(L * 3, conv_cols), s5_glu_w.reshape(L * glu_rows, W_GRP)],
        False, "gather_small")
    c_all = c_g.reshape(N_DEV, D_MODEL)
    conv_full = conv_g.reshape(N_DEV, L, 3, conv_cols).transpose(1, 2, 0, 3).reshape(L, 3, W_GRP)
    glu_full = glu_g.reshape(N_DEV, L, glu_rows, W_GRP).transpose(1, 0, 2, 3).reshape(L, W_GRP, W_GRP)

    ncol = ada_w.shape[-1]
    ada_b_mine = lax.dynamic_slice_in_dim(ada_b, me * ncol, ncol, axis=1).reshape(L, 1, ncol)
    cond_part = _cond_fwd(c_all, ada_w, ada_b_mine)
    (cond_g,) = _exchange([cond_part.reshape(L * N_DEV, ncol)], False, "gather_cond")
    cond_g = cond_g.reshape(N_DEV, L, N_DEV, ncol)
    cond_mine = lax.dynamic_index_in_dim(cond_g, me, axis=2, keepdims=False)
    cond = cond_mine.transpose(1, 0, 2).reshape(L, N_ADA, D_MODEL)

    lg = L * S5_GROUPS
    lre3 = s5_lambda_re.reshape(lg, S5_STATE, 1)
    lim3 = s5_lambda_im.reshape(lg, S5_STATE, 1)
    ldt3 = s5_log_dt.reshape(lg, 1, 1)
    br3 = s5_b_re.reshape(lg, S5_STATE, S5_GROUP_CH)
    bi3 = s5_b_im.reshape(lg, S5_STATE, S5_GROUP_CH)
    a_re3, a_im3, bb_re3, bb_im3 = _s5_prep_fwd(lre3, lim3, ldt3, br3, bi3)
    a_re = a_re3.reshape(L, 1, S5_LANES)
    a_im = a_im3.reshape(L, 1, S5_LANES)

    def b_mat(bb3, l):
        return _block_diag(bb3.reshape(L, S5_GROUPS, S5_STATE, S5_GROUP_CH)[l].transpose(0, 2, 1)).astype(BF16)

    def c_mat(cw, l):
        return _block_diag(cw[l].transpose(0, 2, 1)).astype(BF16)

    for n in TRANSPOSED:
        W[n], M[n], V[n] = (a.transpose(0, 2, 1) for a in (W[n], M[n], V[n]))

    gathered_shape = {"ffn1_w_out": (N_FF_CHUNK, FF_PIECE, D_MODEL), "ffn2_w_out": (N_FF_CHUNK, FF_PIECE, D_MODEL),
                      "w_mix_in": (P_IN, D_MODEL), "w_mix_out": (D_MODEL, D_MODEL)}

    def gather_start(l, after, names=BIG):
        srcs = [W[n][l].astype(BF16) for n in names]
        lands = _place_own(srcs, False, "gather_weights_own")
        return _exchange_start(srcs, lands, after, False, "gather_weights_start", OTHER_CHIPS)

    def gather_finish(handle, after, names=BIG):
        g = _exchange_wait(handle, after, False, "gather_weights_wait", OTHER_CHIPS)
        out = dict(zip(names, _gather_sibling(g, "gather_weights_sibling")))
        return {n: (a.reshape(gathered_shape[n]) if n in gathered_shape else a) for n, a in out.items()}

    def mixer_consts(l):
        w_cat = sgu_w[l].transpose(1, 0, 2).reshape(CHUNK, SGU_HEADS * CHUNK)
        bias = jnp.repeat(sgu_b[l].T, SGU_HEAD_DIM, axis=1)
        return dict(
            w_cat=w_cat, bias=bias, pool_bd=_block_diag(pool_w[l]).astype(BF16), pool_scale=pool_scale[l][None],
            conv=conv_full[l], b_re=b_mat(bb_re3, l), b_im=b_mat(bb_im3, l), c_re=c_mat(s5_c_re, l),
            c_im=c_mat(s5_c_im, l), a_re=a_re[l], a_im=a_im[l], d=s5_d[l][None], glu_w=glu_full[l].astype(BF16),
            glu_b=s5_glu_b[l][None])

    saved = []
    xc = x0
    first_ffn = BIG[:2]
    handle, token = gather_start(0, cond, first_ffn)
    handle_rest, _ = gather_start(0, token, BIG[2:])
    wl = gather_finish(handle, x0, first_ffn)
    for l in range(L):
        cl = cond[l]
        if 0 < l < L - 1:
            handle, token = gather_start(l + 1, wl["ffn1_w_in"])
            cl = cl + token[0, 0]
        mc = mixer_consts(l)
        x_a = xc
        x_b, f1 = _ffn_fwd(x_a, cl[0:3], norm1_g[l][None], wl["ffn1_w_in"], wl["ffn1_w_out"])
        if l == 0:
            wl = {**wl, **gather_finish(handle_rest, x_b, BIG[2:])}
            handle, token = gather_start(1, wl["w_mix_in"])
            cl = cl + token[0, 0]
        z = _mix_in_fwd(x_b, cl[3:5], norm2_g[l][None], wl["w_mix_in"])
        ya = _sgu_fwd(z, mc["w_cat"], mc["bias"])
        yb = _pool_fwd(z, mc["pool_bd"], mc["pool_scale"])
        yc = _conv_fwd(z, mc["conv"])
        yd, y0, sxr, sxi = _s5_fwd(z, mc["b_re"], mc["b_im"], mc["c_re"], mc["c_im"], mc["a_re"], mc["a_im"],
                                   mc["d"], mc["glu_w"], mc["glu_b"])
        ys = (ya, yb, yc, yd)
        x_c = _mix_out_fwd(x_b, ys, mix_norm_g[l][None], cl[5:6], wl["w_mix_out"])
        x_d, f2 = _ffn_fwd(x_c, cl[6:9], norm3_g[l][None], wl["ffn2_w_in"], wl["ffn2_w_out"])
        saved.append(dict(wl=wl, mc=mc, x_a=x_a, x_b=x_b, x_c=x_c, f1=f1, f2=f2, z=z, ys=ys, y0=y0, xr=sxr, xi=sxi))
        xc = x_d
        if l + 1 < L:
            wl = gather_finish(handle, x_d)

    dx, stats = _head(xc, final_norm_g[None], target)
    loss = lax.psum(stats[0, 0], MESH_AXES)

    small_grads = {n: [None] * L for n in SMALL if n != "final_norm_g"}
    small_grads["final_norm_g"] = stats[1]
    big_out = {n: None for n in BIG}
    pending = None

    def finish_scatter(pend, after, names=BIG):
        layer, hnd = pend
        recv = _exchange_wait(hnd, after, True, "scatter_grads_wait")
        for n, r in zip(names, recv):
            big_out[n] = _adamw_layer(layer, W[n], r, M[n], V[n], big_out[n], "adamw_" + n)
        return recv[0]

    dcond_rows = [None] * L
    d_are, d_aim, d_bbre, d_bbim = [None] * L, [None] * L, [None] * L, [None] * L
    token = None
    for l in reversed(range(L)):
        sv = saved[l]
        wl, mc, cl = sv["wl"], sv["mc"], cond[l]
        if token is not None:
            cl = cl + token[0, 0]
        dx, da, db, act, hb, dob, part3 = _ffn_bwd(dx, sv["x_c"], sv["f2"], cl[6:9], norm3_g[l][None],
                                                   wl["ffn2_w_in"], wl["ffn2_w_out"])
        g_ffn2_in = _dw(da, hb, db, name="dw_ffn_in")
        g_ffn2_out = _dw(act, dob, name="dw_ffn_out").reshape(N_DEV, D_FF // N_DEV, D_MODEL)
        dya, dyb, dyc, dyd, ynb, dmob, part_mo = _mix_out_bwd(dx, sv["ys"], mix_norm_g[l][None], cl[5:6],
                                                              wl["w_mix_out"])
        g_mix_out = _dw(ynb[None], dmob, name="dw_mix_out").reshape(N_DEV, D_MODEL // N_DEV, D_MODEL)
        z = sv["z"]
        dza, dw_cat, dbias = _sgu_bwd(z, dya, mc["w_cat"], mc["bias"])
        dzb, dpool_dense, dpool_scale = _pool_bwd(z, dyb, mc["pool_bd"], mc["pool_scale"])
        dzc, dconv8 = _conv_bwd(z, dyc, mc["conv"])
        (dzd, dbre_d, dbim_d, dcre_d, dcim_d, dar8, dai8, dd, dglu_w, dglu_b) = _s5_bwd(
            z, sv["y0"], dyd, sv["xr"], sv["xi"], mc["b_re"], mc["b_im"], mc["c_re"], mc["c_im"],
            mc["a_re"], mc["a_im"], mc["d"], mc["glu_w"], mc["glu_b"])
        dx, h2b, dzbf, part2 = _mix_in_bwd((dza, dzb, dzc, dzd), sv["x_b"], dx, cl[3:5], norm2_g[l][None],
                                           wl["w_mix_in"])
        g_mix_in = _dw(dzbf[None], h2b, name="dw_mix_in").reshape(N_DEV, P_IN // N_DEV, D_MODEL)
        cl1 = cl[0:3]
        if l == 0:
            early = [g_mix_in, g_mix_out, g_ffn2_in, g_ffn2_out]
            handle, token = _exchange_start(early, _place_own(early, True, "scatter_grads_own"), g_mix_in, True,
                                            "scatter_grads_start")
            pending_early = (0, handle)
            cl1 = cl1 + token[0, 0]
        dx, da, db, act, hb, dob, part1 = _ffn_bwd(dx, sv["x_a"], sv["f1"], cl1, norm1_g[l][None],
                                                   wl["ffn1_w_in"], wl["ffn1_w_out"])
        g_ffn1_in = _dw(da, hb, db, name="dw_ffn_in")
        g_ffn1_out = _dw(act, dob, name="dw_ffn_out").reshape(N_DEV, D_FF // N_DEV, D_MODEL)
        last = finish_scatter(pending, g_ffn1_in) if pending is not None else g_ffn1_in
        if l > 0:
            pieces = [g_ffn1_in, g_ffn1_out, g_mix_in, g_mix_out, g_ffn2_in, g_ffn2_out]
            lands = _place_own(pieces, True, "scatter_grads_own")
            handle, token = _exchange_start(pieces, lands, last, True, "scatter_grads_start")
            pending = (l, handle)
        else:
            pieces = [g_ffn1_in, g_ffn1_out]
            first_layer = (pieces, _place_own(pieces, True, "scatter_grads_own"), last)
        dcond_rows[l] = jnp.concatenate([part1[0:3], part2[0:2], part_mo[0:1], part3[0:3]], axis=0)
        sg = small_grads
        sg["norm1_g"][l] = part1[3]
        sg["norm2_g"][l] = part2[2]
        sg["norm3_g"][l] = part3[3]
        sg["mix_norm_g"][l] = part_mo[1]
        sg["sgu_w"][l] = dw_cat.reshape(CHUNK, SGU_HEADS, CHUNK).transpose(1, 0, 2)
        sg["sgu_b"][l] = dbias[:, ::SGU_HEAD_DIM].T
        sg["pool_w"][l] = _diag_blocks(dpool_dense, len(POOL_WINDOWS))
        sg["pool_scale"][l] = dpool_scale[0]
        sg["conv_w"][l] = dconv8[0:3]
        sg["s5_c_re"][l] = _diag_blocks(dcre_d, S5_GROUPS).transpose(0, 2, 1)
        sg["s5_c_im"][l] = _diag_blocks(dcim_d, S5_GROUPS).transpose(0, 2, 1)
        sg["s5_d"][l] = dd[0]
        sg["s5_glu_w"][l] = dglu_w
        sg["s5_glu_b"][l] = dglu_b[0]
        d_are[l], d_aim[l] = dar8[0], dai8[0]
        d_bbre[l] = _diag_blocks(dbre_d, S5_GROUPS).transpose(0, 2, 1)
        d_bbim[l] = _diag_blocks(dbim_d, S5_GROUPS).transpose(0, 2, 1)
    grad_x = dx

    g_lre, g_lim, g_ldt, g_br, g_bi = _s5_prep_bwd(
        lre3, lim3, ldt3, br3, bi3,
        jnp.stack(d_are).reshape(lg, S5_STATE, 1), jnp.stack(d_aim).reshape(lg, S5_STATE, 1),
        jnp.stack(d_bbre).reshape(lg, S5_STATE, S5_GROUP_CH), jnp.stack(d_bbim).reshape(lg, S5_STATE, S5_GROUP_CH))
    small = {n: (jnp.stack(v) if isinstance(v, list) and v[0] is not None else v) for n, v in small_grads.items()}
    small["s5_lambda_re"] = g_lre.reshape(s5_lambda_re.shape)
    small["s5_lambda_im"] = g_lim.reshape(s5_lambda_im.shape)
    small["s5_log_dt"] = g_ldt.reshape(s5_log_dt.shape)
    small["s5_b_re"] = g_br.reshape(s5_b_re.shape)
    small["s5_b_im"] = g_bi.reshape(s5_b_im.shape)

    small_shapes = [(L, 3, W_GRP) if n == "conv_w" else (L, W_GRP, W_GRP) if n == "s5_glu_w" else W[n].shape
                    for n in SMALL]
    packed = _pack([small[n].reshape(s) for n, s in zip(SMALL, small_shapes)], PACK_COLS)
    (pieces,) = _exchange([packed], True, "scatter_small")
    mine = _sum_parts(pieces, "sum_small")
    (summed,) = _exchange([mine], False, "gather_small_sums")
    small_sum = dict(zip(SMALL, _unpack(summed.reshape(-1), small_shapes)))
    small_sum["conv_w"] = lax.dynamic_slice_in_dim(small_sum["conv_w"], me * conv_cols, conv_cols, axis=2)
    small_sum["s5_glu_w"] = lax.dynamic_slice_in_dim(small_sum["s5_glu_w"], me * glu_rows, glu_rows, axis=1)

    dcond = jnp.stack(dcond_rows).reshape(L * N_ADA, D_MODEL)
    (dcond_g,) = _exchange([dcond], False, "gather_dcond")
    g_pieces, g_lands, g_last = first_layer
    behind = jnp.stack([summed[0, 0, 0], dcond_g[0, 0, 0], g_last[0, 0, 0].astype(F32)])
    handle, token = _exchange_start(g_pieces, g_lands, behind, True, "scatter_grads_start")
    pending = (0, handle)
    dcond_all = dcond_g.reshape(N_DEV, L, N_ADA * D_MODEL).transpose(1, 0, 2)
    dcond_mine = lax.dynamic_slice_in_dim(dcond_all, me * ncol, ncol, axis=2) + token[0, 0]
    g_ada_w, g_ada_b = _cond_bwd(c_all.T, dcond_mine, dcond_all)

    grads, deltas, new_m, new_v = {}, {}, {}, {}
    out = _adamw(ada_w.reshape(L * D_MODEL, ncol), g_ada_w.reshape(1, L * D_MODEL, ncol),
                 m_ada_w.reshape(L * D_MODEL, ncol), v_ada_w.reshape(L * D_MODEL, ncol), "adamw_ada_w")
    grads["ada_w"], deltas["ada_w"], new_m["ada_w"], new_v["ada_w"] = (o.reshape(ada_w.shape) for o in out)
    small_names = SMALL + ("ada_b",)
    small_g = dict(small_sum)
    small_g["ada_b"] = g_ada_b.reshape(ada_b.shape)
    shapes = [W[n].shape for n in small_names]
    pw = _pack([W[n] for n in small_names], PACK_COLS)
    rows = pw.shape[0] * pw.shape[1]
    out = _adamw(pw.reshape(rows, PACK_COLS),
                 _pack([small_g[n] for n in small_names], PACK_COLS).reshape(1, rows, PACK_COLS),
                 _pack([M[n] for n in small_names], PACK_COLS).reshape(rows, PACK_COLS),
                 _pack([V[n] for n in small_names], PACK_COLS).reshape(rows, PACK_COLS), "adamw_small")
    for store, o in zip((grads, deltas, new_m, new_v), out):
        store.update(zip(small_names, _unpack(o.reshape(-1), shapes)))
    finish_scatter(pending_early, out[0], BIG[2:])
    finish_scatter(pending, out[0], BIG[:2])
    for n in BIG:
        res = big_out[n]
        if n in TRANSPOSED:
            res = tuple(r.transpose(0, 2, 1) for r in res)
        grads[n], deltas[n], new_m[n], new_v[n] = res

    return (loss, grad_x[None], *[grads[n] for n in WEIGHTS], *[deltas[n] for n in WEIGHTS],
            *[new_m[n] for n in WEIGHTS], *[new_v[n] for n in WEIGHTS])
```

```python
import functools
import math

import jax
import jax.numpy as jnp
from jax import lax
from jax.experimental import pallas as pl
from jax.experimental.pallas import tpu as pltpu

F32 = jnp.float32
BF16 = jnp.bfloat16

D_MODEL = 1024
SEQ = 2048
DEPTH = 4
N_DEV = 8
W_GRP = 256
CHUNK = 128
SGU_HEADS = 4
SGU_HEAD_DIM = 64
POOL_WINDOWS = (2, 4, 8, 16)
POOL_GROUP_DIM = 64
S5_GROUPS = 16
S5_GROUP_CH = 16
S5_STATE = 64
S5_LANES = S5_GROUPS * S5_STATE
S5_BLOCK = 256
P_IN = 1792
D_FF = 2816
FF_PIECE = 2 * D_FF // N_DEV
N_FF_CHUNK = D_FF // FF_PIECE
N_ADA = 9
EPS = 1e-6
ADAM_LR = 0.001
ADAM_B1 = 0.9
ADAM_B2 = 0.999
ADAM_EPS = 1e-08
ADAM_WD = 0.01
ADAM_STEP = 10

SUBLANES = 8
LANES = 128
VMEM_LIMIT = 56 * 1024 * 1024
TOKEN_TILE = 512
ROW_SUBTILE = 256
HIGHEST = lax.Precision.HIGHEST
MESH_AXES = ("x", "y", "c")

_GELU_C = math.sqrt(2.0 / math.pi)
_GELU_A = 0.044715


def _params(*sem):
    return pltpu.CompilerParams(dimension_semantics=tuple(sem) if sem else None, vmem_limit_bytes=VMEM_LIMIT)


def _dot(a, b):
    return jnp.dot(a, b, preferred_element_type=F32)


def _dot_nt(a, b):
    return lax.dot_general(a, b, (((1,), (1,)), ((), ())), preferred_element_type=F32)


def _dot_tn(a, b):
    return lax.dot_general(a, b, (((0,), (0,)), ((), ())), preferred_element_type=F32)


def _dot_hi(a, b):
    return jnp.dot(a, b, preferred_element_type=F32, precision=HIGHEST)


def _sigmoid(x):
    return 1.0 / (1.0 + jnp.exp(-x))


def _gelu(x):
    return 0.5 * x * (1.0 + jnp.tanh(_GELU_C * (x + _GELU_A * x * x * x)))


def _gelu_grad(x):
    t = jnp.tanh(_GELU_C * (x + _GELU_A * x * x * x))
    return 0.5 * (1.0 + t) + 0.5 * x * (1.0 - t * t) * (_GELU_C * (1.0 + 3.0 * _GELU_A * x * x))


def _rms(x):
    r = lax.rsqrt(jnp.mean(x * x, axis=-1, keepdims=True) + EPS)
    return x * r, r


def _rms_bwd(xhat, r, dxhat):
    return r * (dxhat - xhat * jnp.mean(dxhat * xhat, axis=-1, keepdims=True))


def _sum0(x):
    return jnp.sum(x, axis=0, keepdims=True)


def _me():
    return 4 * lax.axis_index("x") + 2 * lax.axis_index("y") + lax.axis_index("c")


def _exchange(srcs, scatter, name):
    n = len(srcs)
    out_shapes = []
    for s in srcs:
        piece = s.shape[1:] if scatter else s.shape
        out_shapes.append(jax.ShapeDtypeStruct((N_DEV,) + tuple(piece), s.dtype))

    def body(*refs):
        ins, outs = refs[:n], refs[n:2 * n]
        send_sems, recv_sems, local_sems = refs[2 * n:]
        x, y, c = lax.axis_index("x"), lax.axis_index("y"), lax.axis_index("c")
        me = 4 * x + 2 * y + c

        def src_of(i, dev):
            return ins[i].at[dev] if scatter else ins[i]

        local = [pltpu.make_async_copy(src_of(i, me), outs[i].at[me], local_sems.at[i]) for i in range(n)]
        for cp in local:
            cp.start()
        sends, recvs = [], []
        for k in range(1, N_DEV):
            px = 1 - x if (k >> 2) & 1 else x
            py = 1 - y if (k >> 1) & 1 else y
            pc = 1 - c if k & 1 else c
            peer = 4 * px + 2 * py + pc
            for i in range(n):
                sends.append(pltpu.make_async_remote_copy(
                    src_ref=src_of(i, peer), dst_ref=outs[i].at[me],
                    send_sem=send_sems.at[k - 1, i], recv_sem=recv_sems.at[k - 1, i],
                    device_id=(px, py, pc), device_id_type=pl.DeviceIdType.MESH))
                recvs.append(pltpu.make_async_remote_copy(
                    src_ref=src_of(i, peer), dst_ref=outs[i].at[peer],
                    send_sem=send_sems.at[k - 1, i], recv_sem=recv_sems.at[k - 1, i],
                    device_id=(px, py, pc), device_id_type=pl.DeviceIdType.MESH))
        for cp in sends:
            cp.start()
        for cp in recvs:
            cp.wait_recv()
        for cp in sends:
            cp.wait_send()
        for cp in local:
            cp.wait()

    hbm = pl.BlockSpec(memory_space=pltpu.HBM)
    return pl.pallas_call(
        body, name=name, out_shape=out_shapes,
        in_specs=[hbm] * n, out_specs=[hbm] * n,
        scratch_shapes=[pltpu.SemaphoreType.DMA((N_DEV - 1, n)), pltpu.SemaphoreType.DMA((N_DEV - 1, n)),
                        pltpu.SemaphoreType.DMA((n,))],
    )(*srcs)


ALL_PEERS = tuple(range(1, N_DEV))
OTHER_CHIPS = (2, 4, 6)


def _peers(which):
    x, y, c = lax.axis_index("x"), lax.axis_index("y"), lax.axis_index("c")
    out = []
    for k in which:
        px = 1 - x if (k >> 2) & 1 else x
        py = 1 - y if (k >> 1) & 1 else y
        pc = 1 - c if k & 1 else c
        out.append(((px, py, pc), 4 * px + 2 * py + pc))
    return out


def _split_copies(ins, lands, send_sems, recv_sems, scatter, with_recvs, which):
    me = _me()
    sends, recvs = [], []
    for j, (dev, peer) in enumerate(_peers(which)):
        for i in range(len(ins)):
            src = ins[i].at[peer] if scatter else ins[i]
            slot = j * len(ins) + i
            sems = dict(send_sem=send_sems.at[slot], recv_sem=recv_sems.at[slot],
                        device_id=dev, device_id_type=pl.DeviceIdType.MESH)
            sends.append(pltpu.make_async_remote_copy(src_ref=src, dst_ref=lands[i].at[me], **sems))
            if with_recvs:
                recvs.append(pltpu.make_async_remote_copy(src_ref=src, dst_ref=lands[i].at[peer], **sems))
    return sends, recvs


_HBM = pl.BlockSpec(memory_space=pltpu.HBM)
_SEM = pl.BlockSpec(memory_space=pltpu.SEMAPHORE)
_EFFECT = pltpu.SideEffectType.DATAFLOW_SIDE_EFFECTING


def _place_own(srcs, scatter, name):
    n = len(srcs)
    halves = 2
    out_shapes, in_specs, out_specs = [], [], []
    for s in srcs:
        r, c = s.shape[-2:]
        out_shapes.append(jax.ShapeDtypeStruct((N_DEV, r, c), s.dtype))
        if scatter:
            in_specs.append(pl.BlockSpec((1, r // halves, c), lambda i, me: (me[0], i, 0)))
        else:
            in_specs.append(pl.BlockSpec((r // halves, c), lambda i, me: (i, 0)))
        out_specs.append(pl.BlockSpec((1, r // halves, c), lambda i, me: (me[0], i, 0)))

    def body(me_ref, *refs):
        for i in range(n):
            refs[n + i][0] = refs[i][0] if scatter else refs[i][...]

    return pl.pallas_call(
        body, name=name, out_shape=out_shapes,
        grid_spec=pltpu.PrefetchScalarGridSpec(num_scalar_prefetch=1, grid=(halves,), in_specs=in_specs,
                                               out_specs=out_specs),
        compiler_params=_params("arbitrary"),
    )(_me().reshape(1).astype(jnp.int32), *srcs)


def _exchange_start(srcs, lands, after, scatter, name, which=ALL_PEERS):
    n = len(srcs)

    def body(*refs):
        ins, land_in = refs[:n], refs[n:2 * n]
        send_sems, recv_sems = refs[2 * n + 1], refs[2 * n + 2]
        token = refs[-1]
        sends, _ = _split_copies(ins, land_in, send_sems, recv_sems, scatter, False, which)
        for cp in sends:
            cp.start()
        token[...] = jnp.zeros_like(token)

    sem = pltpu.SemaphoreType.DMA((len(which) * n,))
    out = pl.pallas_call(
        body, name=name,
        out_shape=(sem, sem, *[pltpu.HBM(s.shape, s.dtype) for s in srcs], *[pltpu.HBM(s.shape, s.dtype) for s in lands],
                   jax.ShapeDtypeStruct((SUBLANES, LANES), F32)),
        in_specs=[_HBM] * (2 * n) + [pl.BlockSpec(memory_space=pl.ANY)],
        out_specs=(_SEM, _SEM, *[_HBM] * (2 * n), pl.BlockSpec(memory_space=pltpu.VMEM)),
        input_output_aliases={i: 2 + i for i in range(2 * n)},
        compiler_params=pltpu.CompilerParams(has_side_effects=_EFFECT),
    )(*srcs, *lands, after)
    return (out[0], out[1], out[2:2 + n], out[2 + n:2 + 2 * n]), out[-1]


def _exchange_wait(handle, after, scatter, name, which=ALL_PEERS):
    send_sems, recv_sems, srcs, lands = handle
    n = len(srcs)
    after = list(after) if isinstance(after, (list, tuple)) else [after]

    def body(*refs):
        ins, land_in = refs[:n], refs[n:2 * n]
        sends, recvs = _split_copies(ins, land_in, refs[2 * n], refs[2 * n + 1], scatter, True, which)
        for cp in sends:
            cp.wait_send()
        for cp in recvs:
            cp.wait_recv()

    out = pl.pallas_call(
        body, name=name,
        out_shape=(*[pltpu.HBM(s.shape, s.dtype) for s in srcs], *[pltpu.HBM(s.shape, s.dtype) for s in lands]),
        in_specs=[_HBM] * (2 * n) + [_SEM, _SEM] + [pl.BlockSpec(memory_space=pl.ANY)] * len(after),
        out_specs=tuple([_HBM] * (2 * n)),
        input_output_aliases={i: i for i in range(2 * n)},
        compiler_params=pltpu.CompilerParams(has_side_effects=_EFFECT),
    )(*srcs, *lands, send_sems, recv_sems, *after)
    return out[n:]


def _gather_sibling(lands, name):
    n = len(lands)
    chips = ((0, 0), (0, 1), (1, 0), (1, 1))

    def body(*refs):
        ins, outs = refs[:n], refs[n:2 * n]
        send_sems, recv_sems = refs[2 * n], refs[2 * n + 1]
        x, y, c = lax.axis_index("x"), lax.axis_index("y"), lax.axis_index("c")
        sends, recvs = [], []
        for j, (bx, by) in enumerate(chips):
            chip = 4 * (1 - x if bx else x) + 2 * (1 - y if by else y)
            for i in range(n):
                sems = dict(send_sem=send_sems.at[j * n + i], recv_sem=recv_sems.at[j * n + i],
                            device_id=(x, y, 1 - c), device_id_type=pl.DeviceIdType.MESH)
                sends.append(pltpu.make_async_remote_copy(src_ref=ins[i].at[chip + c], dst_ref=outs[i].at[chip + c], **sems))
                recvs.append(pltpu.make_async_remote_copy(src_ref=ins[i].at[chip + c], dst_ref=outs[i].at[chip + 1 - c],
                                                          **sems))
        for cp in sends:
            cp.start()
        for cp in recvs:
            cp.wait_recv()
        for cp in sends:
            cp.wait_send()

    return pl.pallas_call(
        body, name=name, out_shape=[jax.ShapeDtypeStruct(a.shape, a.dtype) for a in lands],
        in_specs=[_HBM] * n, out_specs=[_HBM] * n,
        scratch_shapes=[pltpu.SemaphoreType.DMA((len(chips) * n,)), pltpu.SemaphoreType.DMA((len(chips) * n,))],
        input_output_aliases={i: i for i in range(n)},
    )(*lands)


def _cond_fwd(c_all, ada_w, ada_b_mine):
    ncol = ada_w.shape[-1]

    def body(c_ref, w_ref, b_ref, o_ref):
        c = c_ref[...]
        ca = (c * _sigmoid(c)).astype(BF16)
        o_ref[0] = _dot(ca, w_ref[0].astype(BF16)) + b_ref[0]

    return pl.pallas_call(
        body, name="cond_fwd", grid=(DEPTH,),
        out_shape=jax.ShapeDtypeStruct((DEPTH, N_DEV, ncol), F32),
        in_specs=[pl.BlockSpec((N_DEV, D_MODEL), lambda l: (0, 0)),
                  pl.BlockSpec((1, D_MODEL, ncol), lambda l: (l, 0, 0)),
                  pl.BlockSpec((1, 1, ncol), lambda l: (l, 0, 0))],
        out_specs=pl.BlockSpec((1, N_DEV, ncol), lambda l: (l, 0, 0)),
        compiler_params=_params("arbitrary"),
    )(c_all, ada_w, ada_b_mine)


def _cond_bwd(c_all_t, dcond_mine, dcond_all):
    ncol = dcond_mine.shape[-1]
    nall = dcond_all.shape[-1]

    def body(ct_ref, d_ref, da_ref, gw_ref, gb_ref):
        ct = ct_ref[...]
        ct = ct * _sigmoid(ct)
        d = d_ref[0]
        acc = ct[:, 0:1] * d[0:1, :]
        for b in range(1, N_DEV):
            acc = acc + ct[:, b:b + 1] * d[b:b + 1, :]
        gw_ref[0] = acc
        gb_ref[0] = _sum0(da_ref[0])

    return pl.pallas_call(
        body, name="cond_bwd", grid=(DEPTH,),
        out_shape=(jax.ShapeDtypeStruct((DEPTH, D_MODEL, ncol), F32), jax.ShapeDtypeStruct((DEPTH, 1, nall), F32)),
        in_specs=[pl.BlockSpec((D_MODEL, N_DEV), lambda l: (0, 0)),
                  pl.BlockSpec((1, N_DEV, ncol), lambda l: (l, 0, 0)),
                  pl.BlockSpec((1, N_DEV, nall), lambda l: (l, 0, 0))],
        out_specs=(pl.BlockSpec((1, D_MODEL, ncol), lambda l: (l, 0, 0)),
                   pl.BlockSpec((1, 1, nall), lambda l: (l, 0, 0))),
        compiler_params=_params("arbitrary"),
    )(c_all_t, dcond_mine, dcond_all)


def _modnorm(x, g, shift, scale):
    xhat, r = _rms(x)
    return (xhat * g) * (1.0 + scale) + shift, xhat, r


def _modnorm_bwd(xhat, r, g, scale, dh):
    n = xhat * g
    dn = dh * (1.0 + scale)
    dx = _rms_bwd(xhat, r, dn * g)
    return dx, _sum0(dh), _sum0(dh * n), _sum0(dn * xhat)


def _row_spec(rows):
    return pl.BlockSpec((rows, D_MODEL), lambda *_: (0, 0))


def _ffn_fwd(x, cond3, g, w_in_g, w_out_g):
    tm = TOKEN_TILE
    last = N_FF_CHUNK - 1

    def body(x_ref, cond_ref, g_ref, wa_ref, wb_ref, wo_ref, xo_ref, f_ref, h_scr, acc_scr):
        j = pl.program_id(1)

        @pl.when(j == 0)
        def _():
            h, _, _ = _modnorm(x_ref[...], g_ref[...], cond_ref[0:1, :], cond_ref[1:2, :])
            h_scr[...] = h.astype(BF16)
            acc_scr[...] = jnp.zeros_like(acc_scr)

        wa, wb, wo = wa_ref[0], wb_ref[0], wo_ref[0]
        for r in range(tm // ROW_SUBTILE):
            rows = slice(r * ROW_SUBTILE, (r + 1) * ROW_SUBTILE)
            h = h_scr[rows, :]
            a = _dot_nt(h, wa)
            b = _dot_nt(h, wb)
            act = (a * _sigmoid(a)) * b
            acc_scr[rows, :] += _dot(act.astype(BF16), wo)

        @pl.when(j == last)
        def _():
            f = acc_scr[...]
            f_ref[...] = f
            xo_ref[...] = x_ref[...] + (0.5 * cond_ref[2:3, :]) * f

    tok = pl.BlockSpec((tm, D_MODEL), lambda i, j: (i, 0))
    return pl.pallas_call(
        body, name="ffn_fwd", grid=(SEQ // tm, N_FF_CHUNK),
        out_shape=(jax.ShapeDtypeStruct((SEQ, D_MODEL), F32), jax.ShapeDtypeStruct((SEQ, D_MODEL), F32)),
        in_specs=[tok, _row_spec(3), _row_spec(1),
                  pl.BlockSpec((1, FF_PIECE, D_MODEL), lambda i, j: (j, 0, 0)),
                  pl.BlockSpec((1, FF_PIECE, D_MODEL), lambda i, j: (j + N_FF_CHUNK, 0, 0)),
                  pl.BlockSpec((1, FF_PIECE, D_MODEL), lambda i, j: (j, 0, 0))],
        out_specs=(tok, tok),
        scratch_shapes=[pltpu.VMEM((tm, D_MODEL), BF16), pltpu.VMEM((tm, D_MODEL), F32)],
        compiler_params=_params("arbitrary", "arbitrary"),
    )(x, cond3, g, w_in_g, w_in_g, w_out_g)


def _ffn_bwd(dy, x, f, cond3, g, w_in_g, w_out_g):
    tm = TOKEN_TILE
    last = N_FF_CHUNK - 1

    def body(dy_ref, x_ref, f_ref, cond_ref, g_ref, wa_ref, wb_ref, wo_ref,
             dx_ref, da_ref, db_ref, act_ref, h_ref, do_ref, part_ref, h_scr, do_scr, dh_scr):
        i, j = pl.program_id(0), pl.program_id(1)

        @pl.when(j == 0)
        def _():
            h, _, _ = _modnorm(x_ref[...], g_ref[...], cond_ref[0:1, :], cond_ref[1:2, :])
            hb = h.astype(BF16)
            h_scr[...] = hb
            h_ref[...] = hb
            do = ((0.5 * cond_ref[2:3, :]) * dy_ref[...]).astype(BF16)
            do_scr[...] = do
            do_ref[...] = do
            dh_scr[...] = jnp.zeros_like(dh_scr)

        @pl.when((i == 0) & (j == 0))
        def _():
            part_ref[...] = jnp.zeros_like(part_ref)

        wa, wb, wo = wa_ref[0], wb_ref[0], wo_ref[0]
        for r in range(tm // ROW_SUBTILE):
            rows = slice(r * ROW_SUBTILE, (r + 1) * ROW_SUBTILE)
            h = h_scr[rows, :]
            do = do_scr[rows, :]
            a = _dot_nt(h, wa)
            b = _dot_nt(h, wb)
            dact = _dot_nt(do, wo)
            sig = _sigmoid(a)
            s = a * sig
            da = (dact * b * (sig * (1.0 + a * (1.0 - sig)))).astype(BF16)
            db = (dact * s).astype(BF16)
            da_ref[0, rows, :] = da
            db_ref[0, rows, :] = db
            act_ref[0, rows, :] = (s * b).astype(BF16)
            dh_scr[rows, :] += _dot(da, wa) + _dot(db, wb)

        @pl.when(j == last)
        def _():
            dyv = dy_ref[...]
            xhat, r = _rms(x_ref[...])
            dx, dshift, dscale, dg = _modnorm_bwd(xhat, r, g_ref[...], cond_ref[1:2, :], dh_scr[...])
            dx_ref[...] = dyv + dx
            part_ref[0:1, :] += dshift
            part_ref[1:2, :] += dscale
            part_ref[2:3, :] += _sum0(0.5 * dyv * f_ref[...])
            part_ref[3:4, :] += dg

    tok = pl.BlockSpec((tm, D_MODEL), lambda i, j: (i, 0))
    chunk = pl.BlockSpec((1, tm, FF_PIECE), lambda i, j: (j, i, 0))
    chunk_shape = jax.ShapeDtypeStruct((N_FF_CHUNK, SEQ, FF_PIECE), BF16)
    return pl.pallas_call(
        body, name="ffn_bwd", grid=(SEQ // tm, N_FF_CHUNK),
        out_shape=(jax.ShapeDtypeStruct((SEQ, D_MODEL), F32), chunk_shape, chunk_shape, chunk_shape,
                   jax.ShapeDtypeStruct((SEQ, D_MODEL), BF16), jax.ShapeDtypeStruct((SEQ, D_MODEL), BF16),
                   jax.ShapeDtypeStruct((SUBLANES, D_MODEL), F32)),
        in_specs=[tok, tok, tok, _row_spec(3), _row_spec(1),
                  pl.BlockSpec((1, FF_PIECE, D_MODEL), lambda i, j: (j, 0, 0)),
                  pl.BlockSpec((1, FF_PIECE, D_MODEL), lambda i, j: (j + N_FF_CHUNK, 0, 0)),
                  pl.BlockSpec((1, FF_PIECE, D_MODEL), lambda i, j: (j, 0, 0))],
        out_specs=(tok, chunk, chunk, chunk, tok, tok, _row_spec(SUBLANES)),
        scratch_shapes=[pltpu.VMEM((tm, D_MODEL), BF16), pltpu.VMEM((tm, D_MODEL), BF16),
                        pltpu.VMEM((tm, D_MODEL), F32)],
        compiler_params=_params("arbitrary", "arbitrary"),
    )(dy, x, f, cond3, g, w_in_g, w_in_g, w_out_g)


def _dw(lhs_a, rhs, lhs_b=None, name="dw"):
    pa, s, m = lhs_a.shape
    nn = rhs.shape[-1]
    pb = 0 if lhs_b is None else lhs_b.shape[0]
    two = lhs_b is not None

    def body(*refs):
        if two:
            a_ref, b_ref, r_ref, o_ref = refs
            p = pl.program_id(0)

            @pl.when(p < pa)
            def _():
                o_ref[0] = _dot_tn(a_ref[0], r_ref[...]).astype(BF16)

            @pl.when(p >= pa)
            def _():
                o_ref[0] = _dot_tn(b_ref[0], r_ref[...]).astype(BF16)
        else:
            a_ref, r_ref, o_ref = refs
            o_ref[0] = _dot_tn(a_ref[0], r_ref[...]).astype(BF16)

    if two:
        in_specs = [pl.BlockSpec((1, s, m), lambda p: (jnp.minimum(p, pa - 1), 0, 0)),
                    pl.BlockSpec((1, s, m), lambda p: (jnp.maximum(p - pa, 0), 0, 0))]
        args = (lhs_a, lhs_b, rhs)
    else:
        in_specs = [pl.BlockSpec((1, s, m), lambda p: (p, 0, 0))]
        args = (lhs_a, rhs)
    in_specs.append(pl.BlockSpec((s, nn), lambda p: (0, 0)))
    return pl.pallas_call(
        body, name=name, grid=(pa + pb,),
        out_shape=jax.ShapeDtypeStruct((pa + pb, m, nn), BF16),
        in_specs=in_specs, out_specs=pl.BlockSpec((1, m, nn), lambda p: (p, 0, 0)),
        compiler_params=_params("arbitrary"),
    )(*args)


def _mix_in_fwd(x, cond2, g, w):
    tm = TOKEN_TILE

    def body(x_ref, cond_ref, g_ref, w_ref, z_ref):
        h, _, _ = _modnorm(x_ref[...], g_ref[...], cond_ref[0:1, :], cond_ref[1:2, :])
        z_ref[...] = _dot_nt(h.astype(BF16), w_ref[...])

    return pl.pallas_call(
        body, name="mix_in_fwd", grid=(SEQ // tm,),
        out_shape=jax.ShapeDtypeStruct((SEQ, P_IN), F32),
        in_specs=[pl.BlockSpec((tm, D_MODEL), lambda i: (i, 0)), _row_spec(2), _row_spec(1),
                  pl.BlockSpec((P_IN, D_MODEL), lambda i: (0, 0))],
        out_specs=pl.BlockSpec((tm, P_IN), lambda i: (i, 0)),
        compiler_params=_params("arbitrary"),
    )(x, cond2, g, w)


MIX_SLABS = ((0, 2 * W_GRP), (2 * W_GRP, 3 * W_GRP), (3 * W_GRP, 6 * W_GRP), (6 * W_GRP, 7 * W_GRP))


def _mix_in_bwd(dzs, x, dy, cond2, g, w):
    tm = TOKEN_TILE

    def body(dza_ref, dzb_ref, dzc_ref, dzd_ref, x_ref, dy_ref, cond_ref, g_ref, w_ref, dx_ref, h_ref, dzo_ref, part_ref):
        i = pl.program_id(0)

        @pl.when(i == 0)
        def _():
            part_ref[...] = jnp.zeros_like(part_ref)

        h, xhat, r = _modnorm(x_ref[...], g_ref[...], cond_ref[0:1, :], cond_ref[1:2, :])
        h_ref[...] = h.astype(BF16)
        dh = None
        for (lo, hi), d_ref in zip(MIX_SLABS, (dza_ref, dzb_ref, dzc_ref, dzd_ref)):
            dzb = d_ref[...].astype(BF16)
            dzo_ref[:, lo:hi] = dzb
            t = _dot(dzb, w_ref[lo:hi, :])
            dh = t if dh is None else dh + t
        dx, dshift, dscale, dg = _modnorm_bwd(xhat, r, g_ref[...], cond_ref[1:2, :], dh)
        dx_ref[...] = dy_ref[...] + dx
        part_ref[0:1, :] += dshift
        part_ref[1:2, :] += dscale
        part_ref[2:3, :] += dg

    tok = pl.BlockSpec((tm, D_MODEL), lambda i: (i, 0))
    ztok = pl.BlockSpec((tm, P_IN), lambda i: (i, 0))
    slabs = [pl.BlockSpec((tm, hi - lo), lambda i: (i, 0)) for lo, hi in MIX_SLABS]
    return pl.pallas_call(
        body, name="mix_in_bwd", grid=(SEQ // tm,),
        out_shape=(jax.ShapeDtypeStruct((SEQ, D_MODEL), F32), jax.ShapeDtypeStruct((SEQ, D_MODEL), BF16),
                   jax.ShapeDtypeStruct((SEQ, P_IN), BF16), jax.ShapeDtypeStruct((SUBLANES, D_MODEL), F32)),
        in_specs=[*slabs, tok, tok, _row_spec(2), _row_spec(1), pl.BlockSpec((P_IN, D_MODEL), lambda i: (0, 0))],
        out_specs=(tok, tok, ztok, _row_spec(SUBLANES)),
        compiler_params=_params("arbitrary"),
    )(*dzs, x, dy, cond2, g, w)


def _group_norm(ys, g_ref):
    out = []
    for k, y in enumerate(ys):
        yhat, r = _rms(y)
        out.append((yhat, r, g_ref[:, k * W_GRP:(k + 1) * W_GRP]))
    return out


def _mix_out_fwd(x, ys, g, gate, w):
    tm = TOKEN_TILE

    def body(x_ref, ya_ref, yb_ref, yc_ref, yd_ref, g_ref, gate_ref, w_ref, xo_ref):
        acc = None
        for k, (yhat, _, gk) in enumerate(_group_norm([r[...] for r in (ya_ref, yb_ref, yc_ref, yd_ref)], g_ref)):
            t = _dot((yhat * gk).astype(BF16), w_ref[k * W_GRP:(k + 1) * W_GRP, :])
            acc = t if acc is None else acc + t
        xo_ref[...] = x_ref[...] + gate_ref[...] * acc

    tok = pl.BlockSpec((tm, D_MODEL), lambda i: (i, 0))
    ytok = pl.BlockSpec((tm, W_GRP), lambda i: (i, 0))
    return pl.pallas_call(
        body, name="mix_out_fwd", grid=(SEQ // tm,),
        out_shape=jax.ShapeDtypeStruct((SEQ, D_MODEL), F32),
        in_specs=[tok, ytok, ytok, ytok, ytok, _row_spec(1), _row_spec(1),
                  pl.BlockSpec((D_MODEL, D_MODEL), lambda i: (0, 0))],
        out_specs=tok, compiler_params=_params("arbitrary"),
    )(x, *ys, g, gate, w)


def _mix_out_bwd(dy, ys, g, gate, w):
    tm = TOKEN_TILE

    def body(dy_ref, ya_ref, yb_ref, yc_ref, yd_ref, g_ref, gate_ref, w_ref,
             da_ref, db_ref, dc_ref, dd_ref, yn_ref, dmo_ref, part_ref):
        i = pl.program_id(0)

        @pl.when(i == 0)
        def _():
            part_ref[...] = jnp.zeros_like(part_ref)

        dyv = dy_ref[...]
        dmo = (gate_ref[...] * dyv).astype(BF16)
        dmo_ref[...] = dmo
        dyn = _dot_nt(dmo, w_ref[...])
        norms = _group_norm([r[...] for r in (ya_ref, yb_ref, yc_ref, yd_ref)], g_ref)
        mo = None
        for k, ((yhat, r, gk), o_ref) in enumerate(zip(norms, (da_ref, db_ref, dc_ref, dd_ref))):
            sl = slice(k * W_GRP, (k + 1) * W_GRP)
            ynk = (yhat * gk).astype(BF16)
            yn_ref[:, sl] = ynk
            t = _dot(ynk, w_ref[sl, :])
            mo = t if mo is None else mo + t
            dk = dyn[:, sl]
            o_ref[...] = _rms_bwd(yhat, r, dk * gk)
            part_ref[1:2, sl] += _sum0(dk * yhat)
        part_ref[0:1, :] += _sum0(dyv * mo)

    tok = pl.BlockSpec((tm, D_MODEL), lambda i: (i, 0))
    ytok = pl.BlockSpec((tm, W_GRP), lambda i: (i, 0))
    ysh = jax.ShapeDtypeStruct((SEQ, W_GRP), F32)
    return pl.pallas_call(
        body, name="mix_out_bwd", grid=(SEQ // tm,),
        out_shape=(ysh, ysh, ysh, ysh, jax.ShapeDtypeStruct((SEQ, D_MODEL), BF16),
                   jax.ShapeDtypeStruct((SEQ, D_MODEL), BF16), jax.ShapeDtypeStruct((SUBLANES, D_MODEL), F32)),
        in_specs=[tok, ytok, ytok, ytok, ytok, _row_spec(1), _row_spec(1),
                  pl.BlockSpec((D_MODEL, D_MODEL), lambda i: (0, 0))],
        out_specs=(ytok, ytok, ytok, ytok, tok, tok, _row_spec(SUBLANES)),
        compiler_params=_params("arbitrary"),
    )(dy, *ys, g, gate, w)


def _shift_down(v, k, rows):
    return jnp.where(rows >= k, pltpu.roll(v, k, axis=0), 0.0)


def _shift_up(v, k, rows):
    n = v.shape[0]
    return jnp.where(rows < n - k, pltpu.roll(v, n - k, axis=0), 0.0)


def _zslab(width, index):
    return pl.BlockSpec((SEQ, width), lambda *_: (0, index))


def _full(shape):
    return pl.BlockSpec(shape, lambda *_: (0,) * len(shape))


def _head_avg():
    r = lax.broadcasted_iota(jnp.int32, (W_GRP, W_GRP), 0) // SGU_HEAD_DIM
    c = lax.broadcasted_iota(jnp.int32, (W_GRP, W_GRP), 1) // SGU_HEAD_DIM
    return jnp.where(r == c, 1.0 / SGU_HEAD_DIM, 0.0).astype(F32)


def _sgu_norm(za):
    z = _gelu(za)
    u, v = z[:, :W_GRP], z[:, W_GRP:]
    avg = _head_avg()
    vc = v - _dot_hi(v, avg)
    rstd = lax.rsqrt(_dot_hi(vc * vc, avg) + EPS)
    return u, vc * rstd, rstd


def _sgu_masked_w(w_ref):
    t = lax.broadcasted_iota(jnp.int32, (CHUNK, CHUNK), 0)
    s = lax.broadcasted_iota(jnp.int32, (CHUNK, CHUNK), 1)
    tril = t >= s
    return [jnp.where(tril, w_ref[:, h * CHUNK:(h + 1) * CHUNK], 0.0).astype(BF16) for h in range(SGU_HEADS)]


def _head_of_lane():
    return lax.broadcasted_iota(jnp.int32, (CHUNK, W_GRP), 1) // SGU_HEAD_DIM


def _sgu_fwd(z, w_cat, bias):
    def body(z_ref, w_ref, b_ref, y_ref, vn_scr, u_scr):
        u, vn, _ = _sgu_norm(z_ref[...])
        vn_scr[...] = vn.astype(BF16)
        u_scr[...] = u
        ws = _sgu_masked_w(w_ref)
        head = _head_of_lane()
        bias_v = b_ref[...]

        def chunk(n, carry):
            rows = pl.ds(pl.multiple_of(n * CHUNK, CHUNK), CHUNK)
            vb = vn_scr[rows, :]
            mixed = bias_v
            for h in range(SGU_HEADS):
                mixed = mixed + jnp.where(head == h, _dot(ws[h], vb), 0.0)
            y_ref[rows, :] = u_scr[rows, :] * mixed
            return carry

        lax.fori_loop(0, SEQ // CHUNK, chunk, 0)

    return pl.pallas_call(
        body, name="sgu_fwd", grid=(1,),
        out_shape=jax.ShapeDtypeStruct((SEQ, W_GRP), F32),
        in_specs=[_zslab(2 * W_GRP, 0), _full((CHUNK, SGU_HEADS * CHUNK)), _full((CHUNK, W_GRP))],
        out_specs=_full((SEQ, W_GRP)),
        scratch_shapes=[pltpu.VMEM((SEQ, W_GRP), BF16), pltpu.VMEM((SEQ, W_GRP), F32)],
        compiler_params=_params("arbitrary"),
    )(z, w_cat, bias)


def _sgu_bwd(z, dy, w_cat, bias):
    def body(z_ref, dy_ref, w_ref, b_ref, dz_ref, dw_ref, db_ref, vn_scr, u_scr, dvn_scr, du_scr):
        za = z_ref[...]
        u, vn, rstd = _sgu_norm(za)
        vn_scr[...] = vn.astype(BF16)
        u_scr[...] = u
        ws = _sgu_masked_w(w_ref)
        head = _head_of_lane()
        bias_v = b_ref[...]

        def chunk(n, carry):
            dws, dbias = carry
            rows = pl.ds(pl.multiple_of(n * CHUNK, CHUNK), CHUNK)
            vb = vn_scr[rows, :]
            mixed = bias_v
            for h in range(SGU_HEADS):
                mixed = mixed + jnp.where(head == h, _dot(ws[h], vb), 0.0)
            dyc = dy_ref[rows, :]
            du_scr[rows, :] = dyc * mixed
            dmixed = dyc * u_scr[rows, :]
            dvn = jnp.zeros((CHUNK, W_GRP), F32)
            new_dws = []
            for h in range(SGU_HEADS):
                dm = jnp.where(head == h, dmixed, 0.0).astype(BF16)
                new_dws.append(dws[h] + _dot_nt(dm, vb))
                dvn = dvn + _dot_tn(ws[h], dm)
            dvn_scr[rows, :] = dvn
            return tuple(new_dws), dbias + dmixed

        zero_w = tuple(jnp.zeros((CHUNK, CHUNK), F32) for _ in range(SGU_HEADS))
        dws, dbias = lax.fori_loop(0, SEQ // CHUNK, chunk, (zero_w, jnp.zeros((CHUNK, W_GRP), F32)))
        t = lax.broadcasted_iota(jnp.int32, (CHUNK, CHUNK), 0)
        s = lax.broadcasted_iota(jnp.int32, (CHUNK, CHUNK), 1)
        for h in range(SGU_HEADS):
            dw_ref[:, h * CHUNK:(h + 1) * CHUNK] = jnp.where(t >= s, dws[h], 0.0)
        avg = _head_avg()
        db_ref[...] = _dot_hi(dbias, avg) * float(SGU_HEAD_DIM)
        dvn = dvn_scr[...]
        dv = rstd * (dvn - _dot_hi(dvn, avg) - vn * _dot_hi(dvn * vn, avg))
        gg = _gelu_grad(za)
        dz_ref[:, :W_GRP] = gg[:, :W_GRP] * du_scr[...]
        dz_ref[:, W_GRP:] = gg[:, W_GRP:] * dv

    return pl.pallas_call(
        body, name="sgu_bwd", grid=(1,),
        out_shape=(jax.ShapeDtypeStruct((SEQ, 2 * W_GRP), F32), jax.ShapeDtypeStruct((CHUNK, SGU_HEADS * CHUNK), F32),
                   jax.ShapeDtypeStruct((CHUNK, W_GRP), F32)),
        in_specs=[_zslab(2 * W_GRP, 0), _full((SEQ, W_GRP)), _full((CHUNK, SGU_HEADS * CHUNK)), _full((CHUNK, W_GRP))],
        out_specs=(_full((SEQ, 2 * W_GRP)), _full((CHUNK, SGU_HEADS * CHUNK)), _full((CHUNK, W_GRP))),
        scratch_shapes=[pltpu.VMEM((SEQ, W_GRP), BF16), pltpu.VMEM((SEQ, W_GRP), F32),
                        pltpu.VMEM((SEQ, W_GRP), F32), pltpu.VMEM((SEQ, W_GRP), F32)],
        compiler_params=_params("arbitrary"),
    )(z, dy, w_cat, bias)


def _pool_window_of_lane(shape):
    grp = lax.broadcasted_iota(jnp.int32, shape, 1) // POOL_GROUP_DIM
    win = jnp.full(shape, POOL_WINDOWS[0], jnp.int32)
    for k in range(1, len(POOL_WINDOWS)):
        win = jnp.where(grp == k, POOL_WINDOWS[k], win)
    return grp, win


def _pool_select(levels, grp):
    out = levels[0]
    for k in range(1, len(levels)):
        out = jnp.where(grp == k, levels[k], out)
    return out


def _pool_p(z):
    shape = z.shape
    rows = lax.broadcasted_iota(jnp.int32, shape, 0)
    grp, win = _pool_window_of_lane(shape)
    levels, s, k = [], z, 1
    for _ in POOL_WINDOWS:
        s = s + _shift_down(s, k, rows)
        levels.append(s)
        k *= 2
    inv = 1.0 / jnp.minimum(rows + 1, win).astype(F32)
    return _pool_select(levels, grp) * inv - z, inv, rows, grp


def _pool_fwd(z, w_bd, scale):
    def body(z_ref, w_ref, s_ref, y_ref):
        p, _, _, _ = _pool_p(z_ref[...])
        y_ref[...] = _dot(p.astype(BF16), w_ref[...]) * s_ref[...]

    return pl.pallas_call(
        body, name="pool_fwd", grid=(1,),
        out_shape=jax.ShapeDtypeStruct((SEQ, W_GRP), F32),
        in_specs=[_zslab(W_GRP, 2), _full((W_GRP, W_GRP)), _full((1, W_GRP))],
        out_specs=_full((SEQ, W_GRP)), compiler_params=_params("arbitrary"),
    )(z, w_bd, scale)


def _pool_bwd(z, dy, w_bd, scale):
    def body(z_ref, dy_ref, w_ref, s_ref, dz_ref, dw_ref, ds_ref):
        p, inv, rows, grp = _pool_p(z_ref[...])
        pb = p.astype(BF16)
        dyv = dy_ref[...]
        ds_ref[...] = _sum0(dyv * _dot(pb, w_ref[...]))
        dpre = (dyv * s_ref[...]).astype(BF16)
        dw_ref[...] = _dot_tn(pb, dpre)
        dp = _dot_nt(dpre, w_ref[...])
        q = dp * inv
        levels, s, k = [], q, 1
        for _ in POOL_WINDOWS:
            s = s + _shift_up(s, k, rows)
            levels.append(s)
            k *= 2
        dz_ref[...] = _pool_select(levels, grp) - dp

    return pl.pallas_call(
        body, name="pool_bwd", grid=(1,),
        out_shape=(jax.ShapeDtypeStruct((SEQ, W_GRP), F32), jax.ShapeDtypeStruct((W_GRP, W_GRP), F32),
                   jax.ShapeDtypeStruct((1, W_GRP), F32)),
        in_specs=[_zslab(W_GRP, 2), _full((SEQ, W_GRP)), _full((W_GRP, W_GRP)), _full((1, W_GRP))],
        out_specs=(_full((SEQ, W_GRP)), _full((W_GRP, W_GRP)), _full((1, W_GRP))),
        compiler_params=_params("arbitrary"),
    )(z, dy, w_bd, scale)


def _conv_fwd(z, w):
    def body(z_ref, w_ref, y_ref):
        zc = z_ref[...]
        bg, cg, xh = zc[:, :W_GRP], zc[:, W_GRP:2 * W_GRP], zc[:, 2 * W_GRP:]
        rows = lax.broadcasted_iota(jnp.int32, (SEQ, W_GRP), 0)
        y = cg * xh
        conv = w_ref[0:1, :] * _shift_down(y, 2, rows) + w_ref[1:2, :] * _shift_down(y, 1, rows) + w_ref[2:3, :] * y
        y_ref[...] = bg * conv

    return pl.pallas_call(
        body, name="conv_fwd", grid=(1,),
        out_shape=jax.ShapeDtypeStruct((SEQ, W_GRP), F32),
        in_specs=[_zslab(3 * W_GRP, 1), _full((3, W_GRP))],
        out_specs=_full((SEQ, W_GRP)), compiler_params=_params("arbitrary"),
    )(z, w)


def _conv_bwd(z, dy, w):
    def body(z_ref, dy_ref, w_ref, dz_ref, dw_ref):
        zc = z_ref[...]
        bg, cg, xh = zc[:, :W_GRP], zc[:, W_GRP:2 * W_GRP], zc[:, 2 * W_GRP:]
        rows = lax.broadcasted_iota(jnp.int32, (SEQ, W_GRP), 0)
        y = cg * xh
        y2, y1 = _shift_down(y, 2, rows), _shift_down(y, 1, rows)
        conv = w_ref[0:1, :] * y2 + w_ref[1:2, :] * y1 + w_ref[2:3, :] * y
        dyv = dy_ref[...]
        dconv = dyv * bg
        dw_ref[...] = jnp.zeros_like(dw_ref)
        dw_ref[0:1, :] = _sum0(dconv * y2)
        dw_ref[1:2, :] = _sum0(dconv * y1)
        dw_ref[2:3, :] = _sum0(dconv * y)
        dyy = (w_ref[0:1, :] * _shift_up(dconv, 2, rows) + w_ref[1:2, :] * _shift_up(dconv, 1, rows)
               + w_ref[2:3, :] * dconv)
        dz_ref[:, :W_GRP] = dyv * conv
        dz_ref[:, W_GRP:2 * W_GRP] = dyy * xh
        dz_ref[:, 2 * W_GRP:] = dyy * cg

    return pl.pallas_call(
        body, name="conv_bwd", grid=(1,),
        out_shape=(jax.ShapeDtypeStruct((SEQ, 3 * W_GRP), F32), jax.ShapeDtypeStruct((SUBLANES, W_GRP), F32)),
        in_specs=[_zslab(3 * W_GRP, 1), _full((SEQ, W_GRP)), _full((3, W_GRP))],
        out_specs=(_full((SEQ, 3 * W_GRP)), _full((SUBLANES, W_GRP))),
        compiler_params=_params("arbitrary"),
    )(z, dy, w)


def _s5_disc(lre, lim, ldt, br, bi):
    dt = jnp.exp(ldt)
    mag = jnp.exp(lre * dt)
    ang = lim * dt
    a_re, a_im = mag * jnp.cos(ang), mag * jnp.sin(ang)
    nr, ni = a_re - 1.0, a_im
    den = lre * lre + lim * lim
    k_re = (nr * lre + ni * lim) / den
    k_im = (ni * lre - nr * lim) / den
    return a_re, a_im, k_re * br - k_im * bi, k_re * bi + k_im * br


def _s5_prep_fwd(lre, lim, ldt, br, bi):
    def body(lre_ref, lim_ref, ldt_ref, br_ref, bi_ref, ar_ref, ai_ref, bbr_ref, bbi_ref):
        ar, ai, bbr, bbi = _s5_disc(lre_ref[...], lim_ref[...], ldt_ref[...], br_ref[...], bi_ref[...])
        ar_ref[...] = ar
        ai_ref[...] = ai
        bbr_ref[...] = bbr
        bbi_ref[...] = bbi

    return pl.pallas_call(
        body, name="s5_prep_fwd",
        out_shape=(jax.ShapeDtypeStruct(lre.shape, F32), jax.ShapeDtypeStruct(lre.shape, F32),
                   jax.ShapeDtypeStruct(br.shape, F32), jax.ShapeDtypeStruct(br.shape, F32)),
        compiler_params=_params(),
    )(lre, lim, ldt, br, bi)


def _s5_prep_bwd(lre, lim, ldt, br, bi, dar, dai, dbbr, dbbi):
    def body(lre_ref, lim_ref, ldt_ref, br_ref, bi_ref, dar_ref, dai_ref, dbbr_ref, dbbi_ref,
             o_lre, o_lim, o_ldt, o_br, o_bi):
        _, pull = jax.vjp(_s5_disc, lre_ref[...], lim_ref[...], ldt_ref[...], br_ref[...], bi_ref[...])
        g = pull((dar_ref[...], dai_ref[...], dbbr_ref[...], dbbi_ref[...]))
        for o, v in zip((o_lre, o_lim, o_ldt, o_br, o_bi), g):
            o[...] = v

    return pl.pallas_call(
        body, name="s5_prep_bwd",
        out_shape=tuple(jax.ShapeDtypeStruct(a.shape, F32) for a in (lre, lim, ldt, br, bi)),
        compiler_params=_params(),
    )(lre, lim, ldt, br, bi, dar, dai, dbbr, dbbi)


def _cmul(ar, ai, br, bi):
    return ar * br - ai * bi, ar * bi + ai * br


def _s5_tile_consts(ar, ai, reverse):
    if reverse:
        ai = -ai
    shape = (SUBLANES, S5_BLOCK)
    row = lax.broadcasted_iota(jnp.int32, shape, 0)
    a1 = (jnp.broadcast_to(ar, shape), jnp.broadcast_to(ai, shape))
    a2 = _cmul(*a1, *a1)
    a4 = _cmul(*a2, *a2)
    a8 = _cmul(*a4, *a4)
    steps = []
    for s, (pr, pi) in ((1, a1), (2, a2), (4, a4)):
        keep = (row < SUBLANES - s) if reverse else (row >= s)
        steps.append((s, jnp.where(keep, pr, 0.0), jnp.where(keep, pi, 0.0)))
    e = (SUBLANES - row) if reverse else (row + 1)
    pr, pi = jnp.ones(shape, F32), jnp.zeros(shape, F32)
    for bit, (qr, qi) in ((1, a1), (2, a2), (4, a4), (8, a8)):
        nr, ni = _cmul(pr, pi, qr, qi)
        hit = (e & bit) != 0
        pr, pi = jnp.where(hit, nr, pr), jnp.where(hit, ni, pi)
    return steps, pr, pi


def _s5_tile(xr, xi, steps, reverse):
    for s, pr, pi in steps:
        sh = SUBLANES - s if reverse else s
        sr, si = pltpu.roll(xr, sh, axis=0), pltpu.roll(xi, sh, axis=0)
        xr, xi = xr + pr * sr - pi * si, xi + pr * si + pi * sr
    return xr, xi


N_TILES = SEQ // SUBLANES


def _s5_fwd(z, b_re, b_im, c_re, c_im, a_re, a_im, d, glu_w, glu_b):
    nblk = S5_LANES // S5_BLOCK

    def body(u_ref, br_ref, bi_ref, cr_ref, ci_ref, ar_ref, ai_ref, d_ref, gw_ref, gb_ref,
             y_ref, y0_ref, xr_ref, xi_ref, ub_scr, acc_scr):
        jb = pl.program_id(0)

        @pl.when(jb == 0)
        def _():
            ub_scr[...] = u_ref[...].astype(BF16)
            acc_scr[...] = jnp.zeros_like(acc_scr)

        ub = ub_scr[...]
        xr_ref[...] = _dot(ub, br_ref[...])
        xi_ref[...] = _dot(ub, bi_ref[...])
        steps, pr, pi = _s5_tile_consts(ar_ref[...], ai_ref[...], False)

        def tile(t, carry):
            cr, ci = carry
            rows = pl.ds(pl.multiple_of(t * SUBLANES, SUBLANES), SUBLANES)
            xr, xi = _s5_tile(xr_ref[rows, :], xi_ref[rows, :], steps, False)
            xr, xi = xr + pr * cr - pi * ci, xi + pr * ci + pi * cr
            xr_ref[rows, :] = xr
            xi_ref[rows, :] = xi
            return xr[SUBLANES - 1:, :], xi[SUBLANES - 1:, :]

        zero = jnp.zeros((1, S5_BLOCK), F32)
        lax.fori_loop(0, N_TILES, tile, (zero, zero), unroll=2)
        acc_scr[...] += (_dot(xr_ref[...].astype(BF16), cr_ref[...]) - _dot(xi_ref[...].astype(BF16), ci_ref[...]))

        @pl.when(jb == nblk - 1)
        def _():
            y0 = acc_scr[...] + d_ref[...] * u_ref[...]
            y0_ref[...] = y0
            y1 = _gelu(y0)
            y_ref[...] = y1 * _sigmoid(_dot(y1.astype(BF16), gw_ref[...]) + gb_ref[...])

    lane_blk = pl.BlockSpec((SEQ, S5_BLOCK), lambda j: (0, j))
    return pl.pallas_call(
        body, name="s5_fwd", grid=(nblk,),
        out_shape=(jax.ShapeDtypeStruct((SEQ, W_GRP), F32), jax.ShapeDtypeStruct((SEQ, W_GRP), F32),
                   jax.ShapeDtypeStruct((SEQ, S5_LANES), F32), jax.ShapeDtypeStruct((SEQ, S5_LANES), F32)),
        in_specs=[_zslab(W_GRP, 6),
                  pl.BlockSpec((W_GRP, S5_BLOCK), lambda j: (0, j)), pl.BlockSpec((W_GRP, S5_BLOCK), lambda j: (0, j)),
                  pl.BlockSpec((S5_BLOCK, W_GRP), lambda j: (j, 0)), pl.BlockSpec((S5_BLOCK, W_GRP), lambda j: (j, 0)),
                  pl.BlockSpec((1, S5_BLOCK), lambda j: (0, j)), pl.BlockSpec((1, S5_BLOCK), lambda j: (0, j)),
                  _full((1, W_GRP)), _full((W_GRP, W_GRP)), _full((1, W_GRP))],
        out_specs=(_full((SEQ, W_GRP)), _full((SEQ, W_GRP)), lane_blk, lane_blk),
        scratch_shapes=[pltpu.VMEM((SEQ, W_GRP), BF16), pltpu.VMEM((SEQ, W_GRP), F32)],
        compiler_params=_params("arbitrary"),
    )(z, b_re, b_im, c_re, c_im, a_re, a_im, d, glu_w, glu_b)


def _s5_bwd(z, y0, dy, xr, xi, b_re, b_im, c_re, c_im, a_re, a_im, d, glu_w, glu_b):
    nblk = S5_LANES // S5_BLOCK

    def body(u_ref, y0_ref, dy_ref, xr_ref, xi_ref, br_ref, bi_ref, cr_ref, ci_ref, ar_ref, ai_ref,
             d_ref, gw_ref, gb_ref,
             du_ref, dbr_ref, dbi_ref, dcr_ref, dci_ref, dar_ref, dai_ref, dd_ref, dgw_ref, dgb_ref,
             ub_scr, dy0_scr, du_scr, lr_scr, li_scr):
        jb = pl.program_id(0)

        @pl.when(jb == 0)
        def _():
            u = u_ref[...]
            y0v = y0_ref[...]
            y1 = _gelu(y0v)
            y1b = y1.astype(BF16)
            sg = _sigmoid(_dot(y1b, gw_ref[...]) + gb_ref[...])
            dyv = dy_ref[...]
            dpre = dyv * y1 * sg * (1.0 - sg)
            dpb = dpre.astype(BF16)
            dgw_ref[...] = _dot_tn(y1b, dpb)
            dgb_ref[...] = _sum0(dpre)
            dy1 = dyv * sg + _dot_nt(dpb, gw_ref[...])
            dy0 = dy1 * _gelu_grad(y0v)
            dd_ref[...] = _sum0(dy0 * u)
            du_scr[...] = dy0 * d_ref[...]
            dy0_scr[...] = dy0.astype(BF16)
            ub_scr[...] = u.astype(BF16)

        dy0b = dy0_scr[...]
        lr_scr[...] = _dot_nt(dy0b, cr_ref[...])
        li_scr[...] = -_dot_nt(dy0b, ci_ref[...])
        dcr_ref[...] = _dot_tn(xr_ref[...].astype(BF16), dy0b)
        dci_ref[...] = -_dot_tn(xi_ref[...].astype(BF16), dy0b)
        steps, pr, pi = _s5_tile_consts(ar_ref[...], ai_ref[...], True)
        row = lax.broadcasted_iota(jnp.int32, (SUBLANES, S5_BLOCK), 0)

        def tile(k, carry):
            cr, ci, accr, acci = carry
            t = N_TILES - 1 - k
            rows = pl.ds(pl.multiple_of(t * SUBLANES, SUBLANES), SUBLANES)
            lr, li = _s5_tile(lr_scr[rows, :], li_scr[rows, :], steps, True)
            lr, li = lr + pr * cr - pi * ci, li + pr * ci + pi * cr
            lr_scr[rows, :] = lr
            li_scr[rows, :] = li
            prev = pl.ds(pl.multiple_of(jnp.maximum(t - 1, 0) * SUBLANES, SUBLANES), SUBLANES)
            live = jnp.where(t > 0, 1.0, 0.0)
            xpr = jnp.where(row == 0, pltpu.roll(xr_ref[prev, :], 1, axis=0) * live, pltpu.roll(xr_ref[rows, :], 1, axis=0))
            xpi = jnp.where(row == 0, pltpu.roll(xi_ref[prev, :], 1, axis=0) * live, pltpu.roll(xi_ref[rows, :], 1, axis=0))
            accr = accr + lr * xpr + li * xpi
            acci = acci + li * xpr - lr * xpi
            return lr[0:1, :], li[0:1, :], accr, acci

        zero = jnp.zeros((1, S5_BLOCK), F32)
        zt = jnp.zeros((SUBLANES, S5_BLOCK), F32)
        _, _, accr, acci = lax.fori_loop(0, N_TILES, tile, (zero, zero, zt, zt), unroll=2)
        dar_ref[...] = jnp.zeros_like(dar_ref)
        dai_ref[...] = jnp.zeros_like(dai_ref)
        dar_ref[0:1, :] = _sum0(accr)
        dai_ref[0:1, :] = _sum0(acci)
        lrb, lib = lr_scr[...].astype(BF16), li_scr[...].astype(BF16)
        ub = ub_scr[...]
        dbr_ref[...] = _dot_tn(ub, lrb)
        dbi_ref[...] = _dot_tn(ub, lib)
        du_scr[...] += _dot_nt(lrb, br_ref[...]) + _dot_nt(lib, bi_ref[...])

        @pl.when(jb == nblk - 1)
        def _():
            du_ref[...] = du_scr[...]

    lane_blk = pl.BlockSpec((SEQ, S5_BLOCK), lambda j: (0, j))
    bspec = pl.BlockSpec((W_GRP, S5_BLOCK), lambda j: (0, j))
    cspec = pl.BlockSpec((S5_BLOCK, W_GRP), lambda j: (j, 0))
    aspec = pl.BlockSpec((1, S5_BLOCK), lambda j: (0, j))
    a8spec = pl.BlockSpec((SUBLANES, S5_BLOCK), lambda j: (0, j))
    sd = jax.ShapeDtypeStruct
    return pl.pallas_call(
        body, name="s5_bwd", grid=(nblk,),
        out_shape=(sd((SEQ, W_GRP), F32), sd((W_GRP, S5_LANES), F32), sd((W_GRP, S5_LANES), F32),
                   sd((S5_LANES, W_GRP), F32), sd((S5_LANES, W_GRP), F32),
                   sd((SUBLANES, S5_LANES), F32), sd((SUBLANES, S5_LANES), F32),
                   sd((1, W_GRP), F32), sd((W_GRP, W_GRP), F32), sd((1, W_GRP), F32)),
        in_specs=[_zslab(W_GRP, 6), _full((SEQ, W_GRP)), _full((SEQ, W_GRP)), lane_blk, lane_blk,
                  bspec, bspec, cspec, cspec, aspec, aspec,
                  _full((1, W_GRP)), _full((W_GRP, W_GRP)), _full((1, W_GRP))],
        out_specs=(_full((SEQ, W_GRP)), bspec, bspec, cspec, cspec, a8spec, a8spec,
                   _full((1, W_GRP)), _full((W_GRP, W_GRP)), _full((1, W_GRP))),
        scratch_shapes=[pltpu.VMEM((SEQ, W_GRP), BF16), pltpu.VMEM((SEQ, W_GRP), BF16), pltpu.VMEM((SEQ, W_GRP), F32),
                        pltpu.VMEM((SEQ, S5_BLOCK), F32), pltpu.VMEM((SEQ, S5_BLOCK), F32)],
        compiler_params=_params("arbitrary"),
    )(z, y0, dy, xr, xi, b_re, b_im, c_re, c_im, a_re, a_im, d, glu_w, glu_b)


def _head(x, g, target):
    tm = TOKEN_TILE
    n = SEQ // tm

    def body(x_ref, g_ref, t_ref, dx_ref, st_ref, acc_scr):
        i = pl.program_id(0)

        @pl.when(i == 0)
        def _():
            acc_scr[...] = jnp.zeros_like(acc_scr)

        xhat, r = _rms(x_ref[...])
        gv = g_ref[...]
        err = xhat * gv - t_ref[...]
        dyv = err * (1.0 / D_MODEL)
        dx_ref[...] = _rms_bwd(xhat, r, dyv * gv)
        acc_scr[0:1, :] += _sum0(err * err)
        acc_scr[1:2, :] += _sum0(dyv * xhat)

        @pl.when(i == n - 1)
        def _():
            st_ref[...] = acc_scr[...]
            tot = jnp.sum(acc_scr[0:1, :], axis=-1, keepdims=True) * (0.5 / D_MODEL)
            st_ref[0:1, :] = jnp.broadcast_to(tot, (1, D_MODEL))

    tok = pl.BlockSpec((tm, D_MODEL), lambda i: (i, 0))
    return pl.pallas_call(
        body, name="head", grid=(n,),
        out_shape=(jax.ShapeDtypeStruct((SEQ, D_MODEL), F32), jax.ShapeDtypeStruct((SUBLANES, D_MODEL), F32)),
        in_specs=[tok, _row_spec(1), tok], out_specs=(tok, _row_spec(SUBLANES)),
        scratch_shapes=[pltpu.VMEM((SUBLANES, D_MODEL), F32)],
        compiler_params=_params("arbitrary"),
    )(x, g, target)


def _adamw(w, gparts, m, v, name):
    r, c = w.shape
    npart = gparts.shape[0]
    tr = r
    for cand in (512, 256, 128, 64, 32, 16):
        if r % cand == 0 and r > cand:
            tr = cand
            break
    b1c = 1.0 - ADAM_B1 ** ADAM_STEP
    b2c = 1.0 - ADAM_B2 ** ADAM_STEP

    def body(w_ref, g_ref, m_ref, v_ref, go_ref, d_ref, mo_ref, vo_ref):
        g = g_ref[0].astype(F32)
        for k in range(1, npart):
            g = g + g_ref[k].astype(F32)
        mn = ADAM_B1 * m_ref[...] + (1.0 - ADAM_B1) * g
        vn = ADAM_B2 * v_ref[...] + (1.0 - ADAM_B2) * (g * g)
        m_hat = mn / b1c
        v_hat = vn / b2c
        go_ref[...] = g
        d_ref[...] = -ADAM_LR * (m_hat / (jnp.sqrt(v_hat) + ADAM_EPS) + ADAM_WD * w_ref[...])
        mo_ref[...] = mn
        vo_ref[...] = vn

    blk = pl.BlockSpec((tr, c), lambda i: (i, 0))
    sh = jax.ShapeDtypeStruct((r, c), F32)
    return pl.pallas_call(
        body, name=name, grid=(r // tr,),
        out_shape=(sh, sh, sh, sh),
        in_specs=[blk, pl.BlockSpec((npart, tr, c), lambda i: (0, i, 0)), blk, blk],
        out_specs=(blk, blk, blk, blk), compiler_params=_params("arbitrary"),
    )(w, gparts, m, v)


def _adamw_layer(l, w, gparts, m, v, prev, name):
    _, r, c = w.shape
    npart = gparts.shape[0]
    tr = max(t for t in range(16, 513, 16) if r % t == 0)
    b1c = 1.0 - ADAM_B1 ** ADAM_STEP
    b2c = 1.0 - ADAM_B2 ** ADAM_STEP
    nprev = 0 if prev is None else 4

    def body(*refs):
        w_ref, g_ref, m_ref, v_ref = refs[:4]
        go_ref, d_ref, mo_ref, vo_ref = refs[4 + nprev:]
        g = g_ref[0].astype(F32)
        for k in range(1, npart):
            g = g + g_ref[k].astype(F32)
        mn = ADAM_B1 * m_ref[0] + (1.0 - ADAM_B1) * g
        vn = ADAM_B2 * v_ref[0] + (1.0 - ADAM_B2) * (g * g)
        go_ref[0] = g
        d_ref[0] = -ADAM_LR * ((mn / b1c) / (jnp.sqrt(vn / b2c) + ADAM_EPS) + ADAM_WD * w_ref[0])
        mo_ref[0] = mn
        vo_ref[0] = vn

    blk = pl.BlockSpec((1, tr, c), lambda i: (l, i, 0))
    sh = jax.ShapeDtypeStruct(w.shape, F32)
    keep = [pl.BlockSpec(memory_space=pl.ANY)] * nprev
    return pl.pallas_call(
        body, name=name, grid=(r // tr,),
        out_shape=(sh, sh, sh, sh),
        in_specs=[blk, pl.BlockSpec((npart, tr, c), lambda i: (0, i, 0)), blk, blk, *keep],
        out_specs=(blk, blk, blk, blk),
        input_output_aliases={4 + k: k for k in range(nprev)},
        compiler_params=_params("arbitrary"),
    )(w, gparts, m, v, *(prev or ()))


def _sum_parts(parts, name):
    n, r, c = parts.shape

    def body(p_ref, o_ref):
        acc = p_ref[0]
        for k in range(1, n):
            acc = acc + p_ref[k]
        o_ref[...] = acc

    return pl.pallas_call(
        body, name=name, out_shape=jax.ShapeDtypeStruct((r, c), F32), compiler_params=_params(),
    )(parts)


def _block_diag(blocks):
    g, a, b = blocks.shape
    eye = jnp.eye(g, dtype=blocks.dtype)
    return (blocks[:, :, None, :] * eye[:, None, :, None]).reshape(g * a, g * b)


def _diag_blocks(dense, g):
    a, b = dense.shape[0] // g, dense.shape[1] // g
    d4 = dense.reshape(g, a, g, b)
    eye = jnp.eye(g, dtype=dense.dtype)
    return jnp.sum(d4 * eye[:, None, :, None], axis=2)


def _pack(parts, cols):
    flat = jnp.concatenate([p.reshape(-1) for p in parts])
    unit = N_DEV * SUBLANES * cols
    total = -(-flat.shape[0] // unit) * unit
    flat = jnp.pad(flat, (0, total - flat.shape[0]))
    return flat.reshape(N_DEV, total // (N_DEV * cols), cols)


def _unpack(flat, shapes):
    out, pos = [], 0
    for s in shapes:
        n = math.prod(s)
        out.append(flat[pos:pos + n].reshape(s))
        pos += n
    return out


SMALL = ("norm1_g", "norm2_g", "sgu_w", "sgu_b", "pool_w", "pool_scale", "conv_w", "s5_lambda_re", "s5_lambda_im",
         "s5_b_re", "s5_b_im", "s5_c_re", "s5_c_im", "s5_d", "s5_log_dt", "s5_glu_w", "s5_glu_b", "mix_norm_g",
         "norm3_g", "final_norm_g")
BIG = ("ffn1_w_in", "ffn1_w_out", "w_mix_in", "w_mix_out", "ffn2_w_in", "ffn2_w_out")
TRANSPOSED = ("ffn1_w_in", "w_mix_in", "ffn2_w_in")
WEIGHTS = ("ada_w", "ada_b", "norm1_g", "ffn1_w_in", "ffn1_w_out", "norm2_g", "w_mix_in", "sgu_w", "sgu_b", "pool_w",
           "pool_scale", "conv_w", "s5_lambda_re", "s5_lambda_im", "s5_b_re", "s5_b_im", "s5_c_re", "s5_c_im", "s5_d",
           "s5_log_dt", "s5_glu_w", "s5_glu_b", "mix_norm_g", "w_mix_out", "norm3_g", "ffn2_w_in", "ffn2_w_out",
           "final_norm_g")
PACK_COLS = 1024


def kernel(x, c, ada_w, ada_b, norm1_g, ffn1_w_in, ffn1_w_out, norm2_g, w_mix_in, sgu_w, sgu_b, pool_w, pool_scale, conv_w, s5_lambda_re, s5_lambda_im, s5_b_re, s5_b_im, s5_c_re, s5_c_im, s5_d, s5_log_dt, s5_glu_w, s5_glu_b, mix_norm_g, w_mix_out, norm3_g, ffn2_w_in, ffn2_w_out, final_norm_g, loss_target, m_ada_w, m_ada_b, m_norm1_g, m_ffn1_w_in, m_ffn1_w_out, m_norm2_g, m_w_mix_in, m_sgu_w, m_sgu_b, m_pool_w, m_pool_scale, m_conv_w, m_s5_lambda_re, m_s5_lambda_im, m_s5_b_re, m_s5_b_im, m_s5_c_re, m_s5_c_im, m_s5_d, m_s5_log_dt, m_s5_glu_w, m_s5_glu_b, m_mix_norm_g, m_w_mix_out, m_norm3_g, m_ffn2_w_in, m_ffn2_w_out, m_final_norm_g, v_ada_w, v_ada_b, v_norm1_g, v_ffn1_w_in, v_ffn1_w_out, v_norm2_g, v_w_mix_in, v_sgu_w, v_sgu_b, v_pool_w, v_pool_scale, v_conv_w, v_s5_lambda_re, v_s5_lambda_im, v_s5_b_re, v_s5_b_im, v_s5_c_re, v_s5_c_im, v_s5_d, v_s5_log_dt, v_s5_glu_w, v_s5_glu_b, v_mix_norm_g, v_w_mix_out, v_norm3_g, v_ffn2_w_in, v_ffn2_w_out, v_final_norm_g):
    args = dict(locals())
    W = {n: args[n] for n in WEIGHTS}
    M = {n: args["m_" + n] for n in WEIGHTS}
    V = {n: args["v_" + n] for n in WEIGHTS}
    me = _me()
    L = DEPTH
    x0 = x[0]
    target = loss_target[0]

    conv_cols = conv_w.shape[-1]
    glu_rows = s5_glu_w.shape[1]
    c_g, conv_g, glu_g = _exchange(
        [c.reshape(SUBLANES, LANES), conv_w.reshape(L * 3, conv_cols), s5_glu_w.reshape(L * glu_rows, W_GRP)],
        False, "gather_small")
    c_all = c_g.reshape(N_DEV, D_MODEL)
    conv_full = conv_g.reshape(N_DEV, L, 3, conv_cols).transpose(1, 2, 0, 3).reshape(L, 3, W_GRP)
    glu_full = glu_g.reshape(N_DEV, L, glu_rows, W_GRP).transpose(1, 0, 2, 3).reshape(L, W_GRP, W_GRP)

    ncol = ada_w.shape[-1]
    ada_b_mine = lax.dynamic_slice_in_dim(ada_b, me * ncol, ncol, axis=1).reshape(L, 1, ncol)
    cond_part = _cond_fwd(c_all, ada_w, ada_b_mine)
    (cond_g,) = _exchange([cond_part.reshape(L * N_DEV, ncol)], False, "gather_cond")
    cond_g = cond_g.reshape(N_DEV, L, N_DEV, ncol)
    cond_mine = lax.dynamic_index_in_dim(cond_g, me, axis=2, keepdims=False)
    cond = cond_mine.transpose(1, 0, 2).reshape(L, N_ADA, D_MODEL)

    lg = L * S5_GROUPS
    lre3 = s5_lambda_re.reshape(lg, S5_STATE, 1)
    lim3 = s5_lambda_im.reshape(lg, S5_STATE, 1)
    ldt3 = s5_log_dt.reshape(lg, 1, 1)
    br3 = s5_b_re.reshape(lg, S5_STATE, S5_GROUP_CH)
    bi3 = s5_b_im.reshape(lg, S5_STATE, S5_GROUP_CH)

    def b_mat(bb3, l):
        return _block_diag(bb3.reshape(L, S5_GROUPS, S5_STATE, S5_GROUP_CH)[l].transpose(0, 2, 1)).astype(BF16)

    def c_mat(cw, l):
        return _block_diag(cw[l].transpose(0, 2, 1)).astype(BF16)

    for n in TRANSPOSED:
        W[n], M[n], V[n] = (a.transpose(0, 2, 1) for a in (W[n], M[n], V[n]))

    gathered_shape = {"ffn1_w_out": (N_FF_CHUNK, FF_PIECE, D_MODEL), "ffn2_w_out": (N_FF_CHUNK, FF_PIECE, D_MODEL),
                      "w_mix_in": (P_IN, D_MODEL), "w_mix_out": (D_MODEL, D_MODEL)}

    def gather_start(l, after, names=BIG):
        srcs = [W[n][l].astype(BF16) for n in names]
        lands = _place_own(srcs, False, "gather_weights_own")
        return _exchange_start(srcs, lands, after, False, "gather_weights_start", OTHER_CHIPS)

    def gather_finish(handle, after, names=BIG):
        g = _exchange_wait(handle, after, False, "gather_weights_wait", OTHER_CHIPS)
        out = dict(zip(names, _gather_sibling(g, "gather_weights_sibling")))
        return {n: (a.reshape(gathered_shape[n]) if n in gathered_shape else a) for n, a in out.items()}

    saved = []
    xc = x0
    first_ffn = BIG[:2]
    handle, token = gather_start(0, cond, first_ffn)
    handle_rest, token = gather_start(0, token, BIG[2:])

    zero = token[0, 0]
    a_re3, a_im3, bb_re3, bb_im3 = _s5_prep_fwd(lre3 + zero, lim3, ldt3, br3, bi3)
    a_re = a_re3.reshape(L, 1, S5_LANES)
    a_im = a_im3.reshape(L, 1, S5_LANES)
    sgu_w_t, pool_w_t, c_re_t, c_im_t = (a + zero for a in (sgu_w, pool_w, s5_c_re, s5_c_im))

    def mixer_consts(l):
        w_cat = sgu_w_t[l].transpose(1, 0, 2).reshape(CHUNK, SGU_HEADS * CHUNK)
        bias = jnp.repeat(sgu_b[l].T, SGU_HEAD_DIM, axis=1)
        return dict(
            w_cat=w_cat, bias=bias, pool_bd=_block_diag(pool_w_t[l]).astype(BF16), pool_scale=pool_scale[l][None],
            conv=conv_full[l], b_re=b_mat(bb_re3, l), b_im=b_mat(bb_im3, l), c_re=c_mat(c_re_t, l),
            c_im=c_mat(c_im_t, l), a_re=a_re[l], a_im=a_im[l], d=s5_d[l][None], glu_w=glu_full[l].astype(BF16),
            glu_b=s5_glu_b[l][None])

    mcs = [mixer_consts(l) for l in range(L)]
    small_names = SMALL + ("ada_b",)
    first_small = W[small_names[0]] + zero
    packed_w = _pack([first_small] + [W[n] for n in small_names[1:]], PACK_COLS)
    packed_m = _pack([M[n] + zero if n == small_names[0] else M[n] for n in small_names], PACK_COLS)
    packed_v = _pack([V[n] + zero if n == small_names[0] else V[n] for n in small_names], PACK_COLS)
    wl = gather_finish(handle, [x0] + [a for mc in mcs for a in mc.values()], first_ffn)
    for l in range(L):
        cl = cond[l]
        if 0 < l < L - 1:
            handle, token = gather_start(l + 1, wl["ffn1_w_in"])
            cl = cl + token[0, 0]
        mc = mcs[l]
        x_a = xc
        x_b, f1 = _ffn_fwd(x_a, cl[0:3], norm1_g[l][None], wl["ffn1_w_in"], wl["ffn1_w_out"])
        if l == 0:
            wl = {**wl, **gather_finish(handle_rest, [x_b, packed_w, packed_m, packed_v], BIG[2:])}
            handle, token = gather_start(1, wl["w_mix_in"])
            cl = cl + token[0, 0]
        z = _mix_in_fwd(x_b, cl[3:5], norm2_g[l][None], wl["w_mix_in"])
        ya = _sgu_fwd(z, mc["w_cat"], mc["bias"])
        yb = _pool_fwd(z, mc["pool_bd"], mc["pool_scale"])
        yc = _conv_fwd(z, mc["conv"])
        yd, y0, sxr, sxi = _s5_fwd(z, mc["b_re"], mc["b_im"], mc["c_re"], mc["c_im"], mc["a_re"], mc["a_im"],
                                   mc["d"], mc["glu_w"], mc["glu_b"])
        ys = (ya, yb, yc, yd)
        x_c = _mix_out_fwd(x_b, ys, mix_norm_g[l][None], cl[5:6], wl["w_mix_out"])
        x_d, f2 = _ffn_fwd(x_c, cl[6:9], norm3_g[l][None], wl["ffn2_w_in"], wl["ffn2_w_out"])
        saved.append(dict(wl=wl, mc=mc, x_a=x_a, x_b=x_b, x_c=x_c, f1=f1, f2=f2, z=z, ys=ys, y0=y0, xr=sxr, xi=sxi))
        xc = x_d
        if l + 1 < L:
            wl = gather_finish(handle, x_d)

    dx, stats = _head(xc, final_norm_g[None], target)
    loss = lax.psum(stats[0, 0], MESH_AXES)

    small_grads = {n: [None] * L for n in SMALL if n != "final_norm_g"}
    small_grads["final_norm_g"] = stats[1]
    big_out = {n: None for n in BIG}
    pending = None

    def finish_scatter(pend, after, names=BIG):
        layer, hnd = pend
        recv = _exchange_wait(hnd, after, True, "scatter_grads_wait")
        for n, r in zip(names, recv):
            big_out[n] = _adamw_layer(layer, W[n], r, M[n], V[n], big_out[n], "adamw_" + n)
        return recv[0]

    dcond_rows = [None] * L
    d_are, d_aim, d_bbre, d_bbim = [None] * L, [None] * L, [None] * L, [None] * L
    token = None
    for l in reversed(range(L)):
        sv = saved[l]
        wl, mc, cl = sv["wl"], sv["mc"], cond[l]
        if token is not None:
            cl = cl + token[0, 0]
        dx, da, db, act, hb, dob, part3 = _ffn_bwd(dx, sv["x_c"], sv["f2"], cl[6:9], norm3_g[l][None],
                                                   wl["ffn2_w_in"], wl["ffn2_w_out"])
        g_ffn2_in = _dw(da, hb, db, name="dw_ffn_in")
        g_ffn2_out = _dw(act, dob, name="dw_ffn_out").reshape(N_DEV, D_FF // N_DEV, D_MODEL)
        dya, dyb, dyc, dyd, ynb, dmob, part_mo = _mix_out_bwd(dx, sv["ys"], mix_norm_g[l][None], cl[5:6],
                                                              wl["w_mix_out"])
        g_mix_out = _dw(ynb[None], dmob, name="dw_mix_out").reshape(N_DEV, D_MODEL // N_DEV, D_MODEL)
        z = sv["z"]
        dza, dw_cat, dbias = _sgu_bwd(z, dya, mc["w_cat"], mc["bias"])
        dzb, dpool_dense, dpool_scale = _pool_bwd(z, dyb, mc["pool_bd"], mc["pool_scale"])
        dzc, dconv8 = _conv_bwd(z, dyc, mc["conv"])
        (dzd, dbre_d, dbim_d, dcre_d, dcim_d, dar8, dai8, dd, dglu_w, dglu_b) = _s5_bwd(
            z, sv["y0"], dyd, sv["xr"], sv["xi"], mc["b_re"], mc["b_im"], mc["c_re"], mc["c_im"],
            mc["a_re"], mc["a_im"], mc["d"], mc["glu_w"], mc["glu_b"])
        dx, h2b, dzbf, part2 = _mix_in_bwd((dza, dzb, dzc, dzd), sv["x_b"], dx, cl[3:5], norm2_g[l][None],
                                           wl["w_mix_in"])
        g_mix_in = _dw(dzbf[None], h2b, name="dw_mix_in").reshape(N_DEV, P_IN // N_DEV, D_MODEL)
        cl1 = cl[0:3]
        if l == 0:
            early = [g_mix_in, g_mix_out, g_ffn2_in, g_ffn2_out]
            handle, token = _exchange_start(early, _place_own(early, True, "scatter_grads_own"), g_mix_in, True,
                                            "scatter_grads_start")
            pending_early = (0, handle)
            cl1 = cl1 + token[0, 0]
        dx, da, db, act, hb, dob, part1 = _ffn_bwd(dx, sv["x_a"], sv["f1"], cl1, norm1_g[l][None],
                                                   wl["ffn1_w_in"], wl["ffn1_w_out"])
        g_ffn1_in = _dw(da, hb, db, name="dw_ffn_in")
        g_ffn1_out = _dw(act, dob, name="dw_ffn_out").reshape(N_DEV, D_FF // N_DEV, D_MODEL)
        last = finish_scatter(pending, g_ffn1_in) if pending is not None else g_ffn1_in
        if l > 0:
            pieces = [g_ffn1_in, g_ffn1_out, g_mix_in, g_mix_out, g_ffn2_in, g_ffn2_out]
            lands = _place_own(pieces, True, "scatter_grads_own")
            handle, token = _exchange_start(pieces, lands, last, True, "scatter_grads_start")
            pending = (l, handle)
        else:
            pieces = [g_ffn1_in, g_ffn1_out]
            first_layer = (pieces, _place_own(pieces, True, "scatter_grads_own"), last)
        dcond_rows[l] = jnp.concatenate([part1[0:3], part2[0:2], part_mo[0:1], part3[0:3]], axis=0)
        sg = small_grads
        sg["norm1_g"][l] = part1[3]
        sg["norm2_g"][l] = part2[2]
        sg["norm3_g"][l] = part3[3]
        sg["mix_norm_g"][l] = part_mo[1]
        sg["sgu_w"][l] = dw_cat.reshape(CHUNK, SGU_HEADS, CHUNK).transpose(1, 0, 2)
        sg["sgu_b"][l] = dbias[:, ::SGU_HEAD_DIM].T
        sg["pool_w"][l] = _diag_blocks(dpool_dense, len(POOL_WINDOWS))
        sg["pool_scale"][l] = dpool_scale[0]
        sg["conv_w"][l] = dconv8[0:3]
        sg["s5_c_re"][l] = _diag_blocks(dcre_d, S5_GROUPS).transpose(0, 2, 1)
        sg["s5_c_im"][l] = _diag_blocks(dcim_d, S5_GROUPS).transpose(0, 2, 1)
        sg["s5_d"][l] = dd[0]
        sg["s5_glu_w"][l] = dglu_w
        sg["s5_glu_b"][l] = dglu_b[0]
        d_are[l], d_aim[l] = dar8[0], dai8[0]
        d_bbre[l] = _diag_blocks(dbre_d, S5_GROUPS).transpose(0, 2, 1)
        d_bbim[l] = _diag_blocks(dbim_d, S5_GROUPS).transpose(0, 2, 1)
    grad_x = dx

    g_lre, g_lim, g_ldt, g_br, g_bi = _s5_prep_bwd(
        lre3, lim3, ldt3, br3, bi3,
        jnp.stack(d_are).reshape(lg, S5_STATE, 1), jnp.stack(d_aim).reshape(lg, S5_STATE, 1),
        jnp.stack(d_bbre).reshape(lg, S5_STATE, S5_GROUP_CH), jnp.stack(d_bbim).reshape(lg, S5_STATE, S5_GROUP_CH))
    small = {n: (jnp.stack(v) if isinstance(v, list) and v[0] is not None else v) for n, v in small_grads.items()}
    small["s5_lambda_re"] = g_lre.reshape(s5_lambda_re.shape)
    small["s5_lambda_im"] = g_lim.reshape(s5_lambda_im.shape)
    small["s5_log_dt"] = g_ldt.reshape(s5_log_dt.shape)
    small["s5_b_re"] = g_br.reshape(s5_b_re.shape)
    small["s5_b_im"] = g_bi.reshape(s5_b_im.shape)

    small_shapes = [(L, 3, W_GRP) if n == "conv_w" else (L, W_GRP, W_GRP) if n == "s5_glu_w" else W[n].shape
                    for n in SMALL]
    packed = _pack([small[n].reshape(s) for n, s in zip(SMALL, small_shapes)], PACK_COLS)
    (pieces,) = _exchange([packed], True, "scatter_small")
    mine = _sum_parts(pieces, "sum_small")
    dcond = jnp.stack(dcond_rows).reshape(L * N_ADA, D_MODEL)
    summed, dcond_g = _exchange([mine, dcond], False, "gather_small_sums")
    small_sum = dict(zip(SMALL, _unpack(summed.reshape(-1), small_shapes)))
    small_sum["conv_w"] = lax.dynamic_slice_in_dim(small_sum["conv_w"], me * conv_cols, conv_cols, axis=2)
    small_sum["s5_glu_w"] = lax.dynamic_slice_in_dim(small_sum["s5_glu_w"], me * glu_rows, glu_rows, axis=1)

    g_pieces, g_lands, g_last = first_layer
    behind = jnp.stack([summed[0, 0, 0], dcond_g[0, 0, 0], g_last[0, 0, 0].astype(F32)])
    handle, token = _exchange_start(g_pieces, g_lands, behind, True, "scatter_grads_start")
    pending = (0, handle)
    dcond_all = dcond_g.reshape(N_DEV, L, N_ADA * D_MODEL).transpose(1, 0, 2)
    dcond_mine = lax.dynamic_slice_in_dim(dcond_all, me * ncol, ncol, axis=2) + token[0, 0]
    g_ada_w, g_ada_b = _cond_bwd(c_all.T, dcond_mine, dcond_all)

    grads, deltas, new_m, new_v = {}, {}, {}, {}
    out = _adamw(ada_w.reshape(L * D_MODEL, ncol), g_ada_w.reshape(1, L * D_MODEL, ncol),
                 m_ada_w.reshape(L * D_MODEL, ncol), v_ada_w.reshape(L * D_MODEL, ncol), "adamw_ada_w")
    grads["ada_w"], deltas["ada_w"], new_m["ada_w"], new_v["ada_w"] = (o.reshape(ada_w.shape) for o in out)
    small_g = dict(small_sum)
    small_g["ada_b"] = g_ada_b.reshape(ada_b.shape)
    shapes = [W[n].shape for n in small_names]
    rows = packed_w.shape[0] * packed_w.shape[1]
    out = _adamw(packed_w.reshape(rows, PACK_COLS),
                 _pack([small_g[n] for n in small_names], PACK_COLS).reshape(1, rows, PACK_COLS),
                 packed_m.reshape(rows, PACK_COLS), packed_v.reshape(rows, PACK_COLS), "adamw_small")
    for store, o in zip((grads, deltas, new_m, new_v), out):
        store.update(zip(small_names, _unpack(o.reshape(-1), shapes)))
    finish_scatter(pending_early, out[0], BIG[2:])
    finish_scatter(pending, out[0], BIG[:2])
    for n in BIG:
        res = big_out[n]
        if n in TRANSPOSED:
            res = tuple(r.transpose(0, 2, 1) for r in res)
        grads[n], deltas[n], new_m[n], new_v[n] = res

    return (loss, grad_x[None], *[grads[n] for n in WEIGHTS], *[deltas[n] for n in WEIGHTS],
            *[new_m[n] for n in WEIGHTS], *[new_v[n] for n in WEIGHTS])
```

```python
import functools
import math

import jax
import jax.numpy as jnp
from jax import lax
from jax.experimental import pallas as pl
from jax.experimental.pallas import tpu as pltpu

F32 = jnp.float32
BF16 = jnp.bfloat16

D_MODEL = 1024
SEQ = 2048
DEPTH = 4
N_DEV = 8
W_GRP = 256
CHUNK = 128
SGU_HEADS = 4
SGU_HEAD_DIM = 64
POOL_WINDOWS = (2, 4, 8, 16)
POOL_GROUP_DIM = 64
S5_GROUPS = 16
S5_GROUP_CH = 16
S5_STATE = 64
S5_LANES = S5_GROUPS * S5_STATE
S5_BLOCK = 256
P_IN = 1792
D_FF = 2816
FF_PIECE = 2 * D_FF // N_DEV
N_FF_CHUNK = D_FF // FF_PIECE
N_ADA = 9
EPS = 1e-6
ADAM_LR = 0.001
ADAM_B1 = 0.9
ADAM_B2 = 0.999
ADAM_EPS = 1e-08
ADAM_WD = 0.01
ADAM_STEP = 10

SUBLANES = 8
LANES = 128
VMEM_LIMIT = 56 * 1024 * 1024
TOKEN_TILE = 512
ROW_SUBTILE = 256
HIGHEST = lax.Precision.HIGHEST
MESH_AXES = ("x", "y", "c")

_GELU_C = math.sqrt(2.0 / math.pi)
_GELU_A = 0.044715


def _params(*sem):
    return pltpu.CompilerParams(dimension_semantics=tuple(sem) if sem else None, vmem_limit_bytes=VMEM_LIMIT)


def _dot(a, b):
    return jnp.dot(a, b, preferred_element_type=F32)


def _dot_nt(a, b):
    return lax.dot_general(a, b, (((1,), (1,)), ((), ())), preferred_element_type=F32)


def _dot_tn(a, b):
    return lax.dot_general(a, b, (((0,), (0,)), ((), ())), preferred_element_type=F32)


def _dot_hi(a, b):
    return jnp.dot(a, b, preferred_element_type=F32, precision=HIGHEST)


def _sigmoid(x):
    return 1.0 / (1.0 + jnp.exp(-x))


def _gelu(x):
    return 0.5 * x * (1.0 + jnp.tanh(_GELU_C * (x + _GELU_A * x * x * x)))


def _gelu_grad(x):
    t = jnp.tanh(_GELU_C * (x + _GELU_A * x * x * x))
    return 0.5 * (1.0 + t) + 0.5 * x * (1.0 - t * t) * (_GELU_C * (1.0 + 3.0 * _GELU_A * x * x))


def _rms(x):
    r = lax.rsqrt(jnp.mean(x * x, axis=-1, keepdims=True) + EPS)
    return x * r, r


def _rms_bwd(xhat, r, dxhat):
    return r * (dxhat - xhat * jnp.mean(dxhat * xhat, axis=-1, keepdims=True))


def _sum0(x):
    return jnp.sum(x, axis=0, keepdims=True)


def _me():
    return 4 * lax.axis_index("x") + 2 * lax.axis_index("y") + lax.axis_index("c")


def _exchange(srcs, scatter, name):
    n = len(srcs)
    out_shapes = []
    for s in srcs:
        piece = s.shape[1:] if scatter else s.shape
        out_shapes.append(jax.ShapeDtypeStruct((N_DEV,) + tuple(piece), s.dtype))

    def body(*refs):
        ins, outs = refs[:n], refs[n:2 * n]
        send_sems, recv_sems, local_sems = refs[2 * n:]
        x, y, c = lax.axis_index("x"), lax.axis_index("y"), lax.axis_index("c")
        me = 4 * x + 2 * y + c

        def src_of(i, dev):
            return ins[i].at[dev] if scatter else ins[i]

        local = [pltpu.make_async_copy(src_of(i, me), outs[i].at[me], local_sems.at[i]) for i in range(n)]
        for cp in local:
            cp.start()
        sends, recvs = [], []
        for k in range(1, N_DEV):
            px = 1 - x if (k >> 2) & 1 else x
            py = 1 - y if (k >> 1) & 1 else y
            pc = 1 - c if k & 1 else c
            peer = 4 * px + 2 * py + pc
            for i in range(n):
                sends.append(pltpu.make_async_remote_copy(
                    src_ref=src_of(i, peer), dst_ref=outs[i].at[me],
                    send_sem=send_sems.at[k - 1, i], recv_sem=recv_sems.at[k - 1, i],
                    device_id=(px, py, pc), device_id_type=pl.DeviceIdType.MESH))
                recvs.append(pltpu.make_async_remote_copy(
                    src_ref=src_of(i, peer), dst_ref=outs[i].at[peer],
                    send_sem=send_sems.at[k - 1, i], recv_sem=recv_sems.at[k - 1, i],
                    device_id=(px, py, pc), device_id_type=pl.DeviceIdType.MESH))
        for cp in sends:
            cp.start()
        for cp in recvs:
            cp.wait_recv()
        for cp in sends:
            cp.wait_send()
        for cp in local:
            cp.wait()

    hbm = pl.BlockSpec(memory_space=pltpu.HBM)
    return pl.pallas_call(
        body, name=name, out_shape=out_shapes,
        in_specs=[hbm] * n, out_specs=[hbm] * n,
        scratch_shapes=[pltpu.SemaphoreType.DMA((N_DEV - 1, n)), pltpu.SemaphoreType.DMA((N_DEV - 1, n)),
                        pltpu.SemaphoreType.DMA((n,))],
    )(*srcs)


ALL_PEERS = tuple(range(1, N_DEV))
OTHER_CHIPS = (2, 4, 6)


def _peers(which):
    x, y, c = lax.axis_index("x"), lax.axis_index("y"), lax.axis_index("c")
    out = []
    for k in which:
        px = 1 - x if (k >> 2) & 1 else x
        py = 1 - y if (k >> 1) & 1 else y
        pc = 1 - c if k & 1 else c
        out.append(((px, py, pc), 4 * px + 2 * py + pc))
    return out


def _split_copies(ins, lands, send_sems, recv_sems, scatter, with_recvs, which):
    me = _me()
    sends, recvs = [], []
    for j, (dev, peer) in enumerate(_peers(which)):
        for i in range(len(ins)):
            src = ins[i].at[peer] if scatter else ins[i]
            slot = j * len(ins) + i
            sems = dict(send_sem=send_sems.at[slot], recv_sem=recv_sems.at[slot],
                        device_id=dev, device_id_type=pl.DeviceIdType.MESH)
            sends.append(pltpu.make_async_remote_copy(src_ref=src, dst_ref=lands[i].at[me], **sems))
            if with_recvs:
                recvs.append(pltpu.make_async_remote_copy(src_ref=src, dst_ref=lands[i].at[peer], **sems))
    return sends, recvs


_HBM = pl.BlockSpec(memory_space=pltpu.HBM)
_SEM = pl.BlockSpec(memory_space=pltpu.SEMAPHORE)
_EFFECT = pltpu.SideEffectType.DATAFLOW_SIDE_EFFECTING


def _place_own(srcs, scatter, name):
    n = len(srcs)
    halves = 2
    out_shapes, in_specs, out_specs = [], [], []
    for s in srcs:
        r, c = s.shape[-2:]
        out_shapes.append(jax.ShapeDtypeStruct((N_DEV, r, c), s.dtype))
        if scatter:
            in_specs.append(pl.BlockSpec((1, r // halves, c), lambda i, me: (me[0], i, 0)))
        else:
            in_specs.append(pl.BlockSpec((r // halves, c), lambda i, me: (i, 0)))
        out_specs.append(pl.BlockSpec((1, r // halves, c), lambda i, me: (me[0], i, 0)))

    def body(me_ref, *refs):
        for i in range(n):
            refs[n + i][0] = refs[i][0] if scatter else refs[i][...]

    return pl.pallas_call(
        body, name=name, out_shape=out_shapes,
        grid_spec=pltpu.PrefetchScalarGridSpec(num_scalar_prefetch=1, grid=(halves,), in_specs=in_specs,
                                               out_specs=out_specs),
        compiler_params=_params("arbitrary"),
    )(_me().reshape(1).astype(jnp.int32), *srcs)


def _exchange_start(srcs, lands, after, scatter, name, which=ALL_PEERS):
    n = len(srcs)

    def body(*refs):
        ins, land_in = refs[:n], refs[n:2 * n]
        send_sems, recv_sems = refs[2 * n + 1], refs[2 * n + 2]
        token = refs[-1]
        sends, _ = _split_copies(ins, land_in, send_sems, recv_sems, scatter, False, which)
        for cp in sends:
            cp.start()
        token[...] = jnp.zeros_like(token)

    sem = pltpu.SemaphoreType.DMA((len(which) * n,))
    out = pl.pallas_call(
        body, name=name,
        out_shape=(sem, sem, *[pltpu.HBM(s.shape, s.dtype) for s in srcs], *[pltpu.HBM(s.shape, s.dtype) for s in lands],
                   jax.ShapeDtypeStruct((SUBLANES, LANES), F32)),
        in_specs=[_HBM] * (2 * n) + [pl.BlockSpec(memory_space=pl.ANY)],
        out_specs=(_SEM, _SEM, *[_HBM] * (2 * n), pl.BlockSpec(memory_space=pltpu.VMEM)),
        input_output_aliases={i: 2 + i for i in range(2 * n)},
        compiler_params=pltpu.CompilerParams(has_side_effects=_EFFECT),
    )(*srcs, *lands, after)
    return (out[0], out[1], out[2:2 + n], out[2 + n:2 + 2 * n]), out[-1]


def _exchange_wait(handle, after, scatter, name, which=ALL_PEERS):
    send_sems, recv_sems, srcs, lands = handle
    n = len(srcs)
    after = list(after) if isinstance(after, (list, tuple)) else [after]

    def body(*refs):
        ins, land_in = refs[:n], refs[n:2 * n]
        sends, recvs = _split_copies(ins, land_in, refs[2 * n], refs[2 * n + 1], scatter, True, which)
        for cp in sends:
            cp.wait_send()
        for cp in recvs:
            cp.wait_recv()

    out = pl.pallas_call(
        body, name=name,
        out_shape=(*[pltpu.HBM(s.shape, s.dtype) for s in srcs], *[pltpu.HBM(s.shape, s.dtype) for s in lands]),
        in_specs=[_HBM] * (2 * n) + [_SEM, _SEM] + [pl.BlockSpec(memory_space=pl.ANY)] * len(after),
        out_specs=tuple([_HBM] * (2 * n)),
        input_output_aliases={i: i for i in range(2 * n)},
        compiler_params=pltpu.CompilerParams(has_side_effects=_EFFECT),
    )(*srcs, *lands, send_sems, recv_sems, *after)
    return out[n:]


def _gather_sibling(lands, name):
    n = len(lands)
    chips = ((0, 0), (0, 1), (1, 0), (1, 1))

    def body(*refs):
        ins, outs = refs[:n], refs[n:2 * n]
        send_sems, recv_sems = refs[2 * n], refs[2 * n + 1]
        x, y, c = lax.axis_index("x"), lax.axis_index("y"), lax.axis_index("c")
        sends, recvs = [], []
        for j, (bx, by) in enumerate(chips):
            chip = 4 * (1 - x if bx else x) + 2 * (1 - y if by else y)
            for i in range(n):
                sems = dict(send_sem=send_sems.at[j * n + i], recv_sem=recv_sems.at[j * n + i],
                            device_id=(x, y, 1 - c), device_id_type=pl.DeviceIdType.MESH)
                sends.append(pltpu.make_async_remote_copy(src_ref=ins[i].at[chip + c], dst_ref=outs[i].at[chip + c], **sems))
                recvs.append(pltpu.make_async_remote_copy(src_ref=ins[i].at[chip + c], dst_ref=outs[i].at[chip + 1 - c],
                                                          **sems))
        for cp in sends:
            cp.start()
        for cp in recvs:
            cp.wait_recv()
        for cp in sends:
            cp.wait_send()

    return pl.pallas_call(
        body, name=name, out_shape=[jax.ShapeDtypeStruct(a.shape, a.dtype) for a in lands],
        in_specs=[_HBM] * n, out_specs=[_HBM] * n,
        scratch_shapes=[pltpu.SemaphoreType.DMA((len(chips) * n,)), pltpu.SemaphoreType.DMA((len(chips) * n,))],
        input_output_aliases={i: i for i in range(n)},
    )(*lands)


def _cond_fwd(c_all, ada_w, ada_b_mine):
    ncol = ada_w.shape[-1]

    def body(c_ref, w_ref, b_ref, o_ref):
        c = c_ref[...]
        ca = (c * _sigmoid(c)).astype(BF16)
        o_ref[0] = _dot(ca, w_ref[0].astype(BF16)) + b_ref[0]

    return pl.pallas_call(
        body, name="cond_fwd", grid=(DEPTH,),
        out_shape=jax.ShapeDtypeStruct((DEPTH, N_DEV, ncol), F32),
        in_specs=[pl.BlockSpec((N_DEV, D_MODEL), lambda l: (0, 0)),
                  pl.BlockSpec((1, D_MODEL, ncol), lambda l: (l, 0, 0)),
                  pl.BlockSpec((1, 1, ncol), lambda l: (l, 0, 0))],
        out_specs=pl.BlockSpec((1, N_DEV, ncol), lambda l: (l, 0, 0)),
        compiler_params=_params("arbitrary"),
    )(c_all, ada_w, ada_b_mine)


def _cond_bwd(c_all_t, dcond_mine, dcond_all):
    ncol = dcond_mine.shape[-1]
    nall = dcond_all.shape[-1]

    def body(ct_ref, d_ref, da_ref, gw_ref, gb_ref):
        ct = ct_ref[...]
        ct = ct * _sigmoid(ct)
        d = d_ref[0]
        acc = ct[:, 0:1] * d[0:1, :]
        for b in range(1, N_DEV):
            acc = acc + ct[:, b:b + 1] * d[b:b + 1, :]
        gw_ref[0] = acc
        gb_ref[0] = _sum0(da_ref[0])

    return pl.pallas_call(
        body, name="cond_bwd", grid=(DEPTH,),
        out_shape=(jax.ShapeDtypeStruct((DEPTH, D_MODEL, ncol), F32), jax.ShapeDtypeStruct((DEPTH, 1, nall), F32)),
        in_specs=[pl.BlockSpec((D_MODEL, N_DEV), lambda l: (0, 0)),
                  pl.BlockSpec((1, N_DEV, ncol), lambda l: (l, 0, 0)),
                  pl.BlockSpec((1, N_DEV, nall), lambda l: (l, 0, 0))],
        out_specs=(pl.BlockSpec((1, D_MODEL, ncol), lambda l: (l, 0, 0)),
                   pl.BlockSpec((1, 1, nall), lambda l: (l, 0, 0))),
        compiler_params=_params("arbitrary"),
    )(c_all_t, dcond_mine, dcond_all)


def _modnorm(x, g, shift, scale):
    xhat, r = _rms(x)
    return (xhat * g) * (1.0 + scale) + shift, xhat, r


def _modnorm_bwd(xhat, r, g, scale, dh):
    n = xhat * g
    dn = dh * (1.0 + scale)
    dx = _rms_bwd(xhat, r, dn * g)
    return dx, _sum0(dh), _sum0(dh * n), _sum0(dn * xhat)


def _row_spec(rows):
    return pl.BlockSpec((rows, D_MODEL), lambda *_: (0, 0))


def _ffn_fwd(x, cond3, g, w_in_g, w_out_g):
    tm = TOKEN_TILE
    last = N_FF_CHUNK - 1

    def body(x_ref, cond_ref, g_ref, wa_ref, wb_ref, wo_ref, xo_ref, f_ref, a_ref, b_ref, h_ref, h_scr, acc_scr):
        j = pl.program_id(1)

        @pl.when(j == 0)
        def _():
            h, _, _ = _modnorm(x_ref[...], g_ref[...], cond_ref[0:1, :], cond_ref[1:2, :])
            hb = h.astype(BF16)
            h_scr[...] = hb
            h_ref[...] = hb
            acc_scr[...] = jnp.zeros_like(acc_scr)

        wa, wb, wo = wa_ref[0], wb_ref[0], wo_ref[0]
        for r in range(tm // ROW_SUBTILE):
            rows = slice(r * ROW_SUBTILE, (r + 1) * ROW_SUBTILE)
            h = h_scr[rows, :]
            a = _dot_nt(h, wa)
            b = _dot_nt(h, wb)
            a_ref[0, rows, :] = a.astype(BF16)
            b_ref[0, rows, :] = b.astype(BF16)
            act = (a * _sigmoid(a)) * b
            acc_scr[rows, :] += _dot(act.astype(BF16), wo)

        @pl.when(j == last)
        def _():
            f = acc_scr[...]
            f_ref[...] = f
            xo_ref[...] = x_ref[...] + (0.5 * cond_ref[2:3, :]) * f

    tok = pl.BlockSpec((tm, D_MODEL), lambda i, j: (i, 0))
    chunk = pl.BlockSpec((1, tm, FF_PIECE), lambda i, j: (j, i, 0))
    chunk_shape = jax.ShapeDtypeStruct((N_FF_CHUNK, SEQ, FF_PIECE), BF16)
    return pl.pallas_call(
        body, name="ffn_fwd", grid=(SEQ // tm, N_FF_CHUNK),
        out_shape=(jax.ShapeDtypeStruct((SEQ, D_MODEL), F32), jax.ShapeDtypeStruct((SEQ, D_MODEL), F32),
                   chunk_shape, chunk_shape, jax.ShapeDtypeStruct((SEQ, D_MODEL), BF16)),
        in_specs=[tok, _row_spec(3), _row_spec(1),
                  pl.BlockSpec((1, FF_PIECE, D_MODEL), lambda i, j: (j, 0, 0)),
                  pl.BlockSpec((1, FF_PIECE, D_MODEL), lambda i, j: (j + N_FF_CHUNK, 0, 0)),
                  pl.BlockSpec((1, FF_PIECE, D_MODEL), lambda i, j: (j, 0, 0))],
        out_specs=(tok, tok, chunk, chunk, tok),
        scratch_shapes=[pltpu.VMEM((tm, D_MODEL), BF16), pltpu.VMEM((tm, D_MODEL), F32)],
        compiler_params=_params("arbitrary", "arbitrary"),
    )(x, cond3, g, w_in_g, w_in_g, w_out_g)


def _ffn_bwd(dy, x, f, a_sv, b_sv, cond3, g, w_in_g, w_out_g):
    tm = TOKEN_TILE
    last = N_FF_CHUNK - 1

    def body(dy_ref, x_ref, f_ref, a_ref, b_ref, cond_ref, g_ref, wa_ref, wb_ref, wo_ref,
             dx_ref, da_ref, db_ref, act_ref, do_ref, part_ref, do_scr, dh_scr):
        i, j = pl.program_id(0), pl.program_id(1)

        @pl.when(j == 0)
        def _():
            do = ((0.5 * cond_ref[2:3, :]) * dy_ref[...]).astype(BF16)
            do_scr[...] = do
            do_ref[...] = do
            dh_scr[...] = jnp.zeros_like(dh_scr)

        @pl.when((i == 0) & (j == 0))
        def _():
            part_ref[...] = jnp.zeros_like(part_ref)

        wa, wb, wo = wa_ref[0], wb_ref[0], wo_ref[0]
        for r in range(tm // ROW_SUBTILE):
            rows = slice(r * ROW_SUBTILE, (r + 1) * ROW_SUBTILE)
            do = do_scr[rows, :]
            a = a_ref[0, rows, :].astype(F32)
            b = b_ref[0, rows, :].astype(F32)
            dact = _dot_nt(do, wo)
            sig = _sigmoid(a)
            s = a * sig
            da = (dact * b * (sig * (1.0 + a * (1.0 - sig)))).astype(BF16)
            db = (dact * s).astype(BF16)
            da_ref[0, rows, :] = da
            db_ref[0, rows, :] = db
            act_ref[0, rows, :] = (s * b).astype(BF16)
            dh_scr[rows, :] += _dot(da, wa) + _dot(db, wb)

        @pl.when(j == last)
        def _():
            dyv = dy_ref[...]
            xhat, r = _rms(x_ref[...])
            dx, dshift, dscale, dg = _modnorm_bwd(xhat, r, g_ref[...], cond_ref[1:2, :], dh_scr[...])
            dx_ref[...] = dyv + dx
            part_ref[0:1, :] += dshift
            part_ref[1:2, :] += dscale
            part_ref[2:3, :] += _sum0(0.5 * dyv * f_ref[...])
            part_ref[3:4, :] += dg

    tok = pl.BlockSpec((tm, D_MODEL), lambda i, j: (i, 0))
    chunk = pl.BlockSpec((1, tm, FF_PIECE), lambda i, j: (j, i, 0))
    chunk_shape = jax.ShapeDtypeStruct((N_FF_CHUNK, SEQ, FF_PIECE), BF16)
    return pl.pallas_call(
        body, name="ffn_bwd", grid=(SEQ // tm, N_FF_CHUNK),
        out_shape=(jax.ShapeDtypeStruct((SEQ, D_MODEL), F32), chunk_shape, chunk_shape, chunk_shape,
                   jax.ShapeDtypeStruct((SEQ, D_MODEL), BF16), jax.ShapeDtypeStruct((SUBLANES, D_MODEL), F32)),
        in_specs=[tok, tok, tok, chunk, chunk, _row_spec(3), _row_spec(1),
                  pl.BlockSpec((1, FF_PIECE, D_MODEL), lambda i, j: (j, 0, 0)),
                  pl.BlockSpec((1, FF_PIECE, D_MODEL), lambda i, j: (j + N_FF_CHUNK, 0, 0)),
                  pl.BlockSpec((1, FF_PIECE, D_MODEL), lambda i, j: (j, 0, 0))],
        out_specs=(tok, chunk, chunk, chunk, tok, _row_spec(SUBLANES)),
        scratch_shapes=[pltpu.VMEM((tm, D_MODEL), BF16), pltpu.VMEM((tm, D_MODEL), F32)],
        compiler_params=_params("arbitrary", "arbitrary"),
    )(dy, x, f, a_sv, b_sv, cond3, g, w_in_g, w_in_g, w_out_g)


def _dw(lhs_a, rhs, lhs_b=None, name="dw"):
    pa, s, m = lhs_a.shape
    nn = rhs.shape[-1]
    pb = 0 if lhs_b is None else lhs_b.shape[0]
    two = lhs_b is not None

    def body(*refs):
        if two:
            a_ref, b_ref, r_ref, o_ref = refs
            p = pl.program_id(0)

            @pl.when(p < pa)
            def _():
                o_ref[0] = _dot_tn(a_ref[0], r_ref[...]).astype(BF16)

            @pl.when(p >= pa)
            def _():
                o_ref[0] = _dot_tn(b_ref[0], r_ref[...]).astype(BF16)
        else:
            a_ref, r_ref, o_ref = refs
            o_ref[0] = _dot_tn(a_ref[0], r_ref[...]).astype(BF16)

    if two:
        in_specs = [pl.BlockSpec((1, s, m), lambda p: (jnp.minimum(p, pa - 1), 0, 0)),
                    pl.BlockSpec((1, s, m), lambda p: (jnp.maximum(p - pa, 0), 0, 0))]
        args = (lhs_a, lhs_b, rhs)
    else:
        in_specs = [pl.BlockSpec((1, s, m), lambda p: (p, 0, 0))]
        args = (lhs_a, rhs)
    in_specs.append(pl.BlockSpec((s, nn), lambda p: (0, 0)))
    return pl.pallas_call(
        body, name=name, grid=(pa + pb,),
        out_shape=jax.ShapeDtypeStruct((pa + pb, m, nn), BF16),
        in_specs=in_specs, out_specs=pl.BlockSpec((1, m, nn), lambda p: (p, 0, 0)),
        compiler_params=_params("arbitrary"),
    )(*args)


def _mix_in_fwd(x, cond2, g, w):
    tm = TOKEN_TILE

    def body(x_ref, cond_ref, g_ref, w_ref, z_ref):
        h, _, _ = _modnorm(x_ref[...], g_ref[...], cond_ref[0:1, :], cond_ref[1:2, :])
        z_ref[...] = _dot_nt(h.astype(BF16), w_ref[...])

    return pl.pallas_call(
        body, name="mix_in_fwd", grid=(SEQ // tm,),
        out_shape=jax.ShapeDtypeStruct((SEQ, P_IN), F32),
        in_specs=[pl.BlockSpec((tm, D_MODEL), lambda i: (i, 0)), _row_spec(2), _row_spec(1),
                  pl.BlockSpec((P_IN, D_MODEL), lambda i: (0, 0))],
        out_specs=pl.BlockSpec((tm, P_IN), lambda i: (i, 0)),
        compiler_params=_params("arbitrary"),
    )(x, cond2, g, w)


MIX_SLABS = ((0, 2 * W_GRP), (2 * W_GRP, 3 * W_GRP), (3 * W_GRP, 6 * W_GRP), (6 * W_GRP, 7 * W_GRP))


def _mix_in_bwd(dzs, x, dy, cond2, g, w):
    tm = TOKEN_TILE

    def body(dza_ref, dzb_ref, dzc_ref, dzd_ref, x_ref, dy_ref, cond_ref, g_ref, w_ref, dx_ref, h_ref, dzo_ref, part_ref):
        i = pl.program_id(0)

        @pl.when(i == 0)
        def _():
            part_ref[...] = jnp.zeros_like(part_ref)

        h, xhat, r = _modnorm(x_ref[...], g_ref[...], cond_ref[0:1, :], cond_ref[1:2, :])
        h_ref[...] = h.astype(BF16)
        dh = None
        for (lo, hi), d_ref in zip(MIX_SLABS, (dza_ref, dzb_ref, dzc_ref, dzd_ref)):
            dzb = d_ref[...].astype(BF16)
            dzo_ref[:, lo:hi] = dzb
            t = _dot(dzb, w_ref[lo:hi, :])
            dh = t if dh is None else dh + t
        dx, dshift, dscale, dg = _modnorm_bwd(xhat, r, g_ref[...], cond_ref[1:2, :], dh)
        dx_ref[...] = dy_ref[...] + dx
        part_ref[0:1, :] += dshift
        part_ref[1:2, :] += dscale
        part_ref[2:3, :] += dg

    tok = pl.BlockSpec((tm, D_MODEL), lambda i: (i, 0))
    ztok = pl.BlockSpec((tm, P_IN), lambda i: (i, 0))
    slabs = [pl.BlockSpec((tm, hi - lo), lambda i: (i, 0)) for lo, hi in MIX_SLABS]
    return pl.pallas_call(
        body, name="mix_in_bwd", grid=(SEQ // tm,),
        out_shape=(jax.ShapeDtypeStruct((SEQ, D_MODEL), F32), jax.ShapeDtypeStruct((SEQ, D_MODEL), BF16),
                   jax.ShapeDtypeStruct((SEQ, P_IN), BF16), jax.ShapeDtypeStruct((SUBLANES, D_MODEL), F32)),
        in_specs=[*slabs, tok, tok, _row_spec(2), _row_spec(1), pl.BlockSpec((P_IN, D_MODEL), lambda i: (0, 0))],
        out_specs=(tok, tok, ztok, _row_spec(SUBLANES)),
        compiler_params=_params("arbitrary"),
    )(*dzs, x, dy, cond2, g, w)


def _group_norm(ys, g_ref):
    out = []
    for k, y in enumerate(ys):
        yhat, r = _rms(y)
        out.append((yhat, r, g_ref[:, k * W_GRP:(k + 1) * W_GRP]))
    return out


def _mix_out_fwd(x, ys, g, gate, w):
    tm = TOKEN_TILE

    def body(x_ref, ya_ref, yb_ref, yc_ref, yd_ref, g_ref, gate_ref, w_ref, xo_ref):
        acc = None
        for k, (yhat, _, gk) in enumerate(_group_norm([r[...] for r in (ya_ref, yb_ref, yc_ref, yd_ref)], g_ref)):
            t = _dot((yhat * gk).astype(BF16), w_ref[k * W_GRP:(k + 1) * W_GRP, :])
            acc = t if acc is None else acc + t
        xo_ref[...] = x_ref[...] + gate_ref[...] * acc

    tok = pl.BlockSpec((tm, D_MODEL), lambda i: (i, 0))
    ytok = pl.BlockSpec((tm, W_GRP), lambda i: (i, 0))
    return pl.pallas_call(
        body, name="mix_out_fwd", grid=(SEQ // tm,),
        out_shape=jax.ShapeDtypeStruct((SEQ, D_MODEL), F32),
        in_specs=[tok, ytok, ytok, ytok, ytok, _row_spec(1), _row_spec(1),
                  pl.BlockSpec((D_MODEL, D_MODEL), lambda i: (0, 0))],
        out_specs=tok, compiler_params=_params("arbitrary"),
    )(x, *ys, g, gate, w)


def _mix_out_bwd(dy, ys, g, gate, w):
    tm = TOKEN_TILE

    def body(dy_ref, ya_ref, yb_ref, yc_ref, yd_ref, g_ref, gate_ref, w_ref,
             da_ref, db_ref, dc_ref, dd_ref, yn_ref, dmo_ref, part_ref):
        i = pl.program_id(0)

        @pl.when(i == 0)
        def _():
            part_ref[...] = jnp.zeros_like(part_ref)

        dyv = dy_ref[...]
        dmo = (gate_ref[...] * dyv).astype(BF16)
        dmo_ref[...] = dmo
        dyn = _dot_nt(dmo, w_ref[...])
        norms = _group_norm([r[...] for r in (ya_ref, yb_ref, yc_ref, yd_ref)], g_ref)
        mo = None
        for k, ((yhat, r, gk), o_ref) in enumerate(zip(norms, (da_ref, db_ref, dc_ref, dd_ref))):
            sl = slice(k * W_GRP, (k + 1) * W_GRP)
            ynk = (yhat * gk).astype(BF16)
            yn_ref[:, sl] = ynk
            t = _dot(ynk, w_ref[sl, :])
            mo = t if mo is None else mo + t
            dk = dyn[:, sl]
            o_ref[...] = _rms_bwd(yhat, r, dk * gk)
            part_ref[1:2, sl] += _sum0(dk * yhat)
        part_ref[0:1, :] += _sum0(dyv * mo)

    tok = pl.BlockSpec((tm, D_MODEL), lambda i: (i, 0))
    ytok = pl.BlockSpec((tm, W_GRP), lambda i: (i, 0))
    ysh = jax.ShapeDtypeStruct((SEQ, W_GRP), F32)
    return pl.pallas_call(
        body, name="mix_out_bwd", grid=(SEQ // tm,),
        out_shape=(ysh, ysh, ysh, ysh, jax.ShapeDtypeStruct((SEQ, D_MODEL), BF16),
                   jax.ShapeDtypeStruct((SEQ, D_MODEL), BF16), jax.ShapeDtypeStruct((SUBLANES, D_MODEL), F32)),
        in_specs=[tok, ytok, ytok, ytok, ytok, _row_spec(1), _row_spec(1),
                  pl.BlockSpec((D_MODEL, D_MODEL), lambda i: (0, 0))],
        out_specs=(ytok, ytok, ytok, ytok, tok, tok, _row_spec(SUBLANES)),
        compiler_params=_params("arbitrary"),
    )(dy, *ys, g, gate, w)


def _shift_down(v, k, rows):
    return jnp.where(rows >= k, pltpu.roll(v, k, axis=0), 0.0)


def _shift_up(v, k, rows):
    n = v.shape[0]
    return jnp.where(rows < n - k, pltpu.roll(v, n - k, axis=0), 0.0)


def _zslab(width, index):
    return pl.BlockSpec((SEQ, width), lambda *_: (0, index))


def _full(shape):
    return pl.BlockSpec(shape, lambda *_: (0,) * len(shape))


def _head_avg():
    r = lax.broadcasted_iota(jnp.int32, (W_GRP, W_GRP), 0) // SGU_HEAD_DIM
    c = lax.broadcasted_iota(jnp.int32, (W_GRP, W_GRP), 1) // SGU_HEAD_DIM
    return jnp.where(r == c, 1.0 / SGU_HEAD_DIM, 0.0).astype(F32)


def _sgu_norm(za):
    z = _gelu(za)
    u, v = z[:, :W_GRP], z[:, W_GRP:]
    avg = _head_avg()
    vc = v - _dot_hi(v, avg)
    rstd = lax.rsqrt(_dot_hi(vc * vc, avg) + EPS)
    return u, vc * rstd, rstd


def _sgu_masked_w(w_ref):
    t = lax.broadcasted_iota(jnp.int32, (CHUNK, CHUNK), 0)
    s = lax.broadcasted_iota(jnp.int32, (CHUNK, CHUNK), 1)
    tril = t >= s
    return [jnp.where(tril, w_ref[:, h * CHUNK:(h + 1) * CHUNK], 0.0).astype(BF16) for h in range(SGU_HEADS)]


def _head_of_lane():
    return lax.broadcasted_iota(jnp.int32, (CHUNK, W_GRP), 1) // SGU_HEAD_DIM


def _sgu_fwd(z, w_cat, bias):
    def body(z_ref, w_ref, b_ref, y_ref, vn_scr, u_scr):
        u, vn, _ = _sgu_norm(z_ref[...])
        vn_scr[...] = vn.astype(BF16)
        u_scr[...] = u
        ws = _sgu_masked_w(w_ref)
        head = _head_of_lane()
        bias_v = b_ref[...]

        def chunk(n, carry):
            rows = pl.ds(pl.multiple_of(n * CHUNK, CHUNK), CHUNK)
            vb = vn_scr[rows, :]
            mixed = bias_v
            for h in range(SGU_HEADS):
                mixed = mixed + jnp.where(head == h, _dot(ws[h], vb), 0.0)
            y_ref[rows, :] = u_scr[rows, :] * mixed
            return carry

        lax.fori_loop(0, SEQ // CHUNK, chunk, 0)

    return pl.pallas_call(
        body, name="sgu_fwd", grid=(1,),
        out_shape=jax.ShapeDtypeStruct((SEQ, W_GRP), F32),
        in_specs=[_zslab(2 * W_GRP, 0), _full((CHUNK, SGU_HEADS * CHUNK)), _full((CHUNK, W_GRP))],
        out_specs=_full((SEQ, W_GRP)),
        scratch_shapes=[pltpu.VMEM((SEQ, W_GRP), BF16), pltpu.VMEM((SEQ, W_GRP), F32)],
        compiler_params=_params("arbitrary"),
    )(z, w_cat, bias)


def _sgu_bwd(z, dy, w_cat, bias):
    def body(z_ref, dy_ref, w_ref, b_ref, dz_ref, dw_ref, db_ref, vn_scr, u_scr, dvn_scr, du_scr):
        za = z_ref[...]
        u, vn, rstd = _sgu_norm(za)
        vn_scr[...] = vn.astype(BF16)
        u_scr[...] = u
        ws = _sgu_masked_w(w_ref)
        head = _head_of_lane()
        bias_v = b_ref[...]

        def chunk(n, carry):
            dws, dbias = carry
            rows = pl.ds(pl.multiple_of(n * CHUNK, CHUNK), CHUNK)
            vb = vn_scr[rows, :]
            mixed = bias_v
            for h in range(SGU_HEADS):
                mixed = mixed + jnp.where(head == h, _dot(ws[h], vb), 0.0)
            dyc = dy_ref[rows, :]
            du_scr[rows, :] = dyc * mixed
            dmixed = dyc * u_scr[rows, :]
            dvn = jnp.zeros((CHUNK, W_GRP), F32)
            new_dws = []
            for h in range(SGU_HEADS):
                dm = jnp.where(head == h, dmixed, 0.0).astype(BF16)
                new_dws.append(dws[h] + _dot_nt(dm, vb))
                dvn = dvn + _dot_tn(ws[h], dm)
            dvn_scr[rows, :] = dvn
            return tuple(new_dws), dbias + dmixed

        zero_w = tuple(jnp.zeros((CHUNK, CHUNK), F32) for _ in range(SGU_HEADS))
        dws, dbias = lax.fori_loop(0, SEQ // CHUNK, chunk, (zero_w, jnp.zeros((CHUNK, W_GRP), F32)))
        t = lax.broadcasted_iota(jnp.int32, (CHUNK, CHUNK), 0)
        s = lax.broadcasted_iota(jnp.int32, (CHUNK, CHUNK), 1)
        for h in range(SGU_HEADS):
            dw_ref[:, h * CHUNK:(h + 1) * CHUNK] = jnp.where(t >= s, dws[h], 0.0)
        avg = _head_avg()
        db_ref[...] = _dot_hi(dbias, avg) * float(SGU_HEAD_DIM)
        dvn = dvn_scr[...]
        dv = rstd * (dvn - _dot_hi(dvn, avg) - vn * _dot_hi(dvn * vn, avg))
        gg = _gelu_grad(za)
        dz_ref[:, :W_GRP] = gg[:, :W_GRP] * du_scr[...]
        dz_ref[:, W_GRP:] = gg[:, W_GRP:] * dv

    return pl.pallas_call(
        body, name="sgu_bwd", grid=(1,),
        out_shape=(jax.ShapeDtypeStruct((SEQ, 2 * W_GRP), F32), jax.ShapeDtypeStruct((CHUNK, SGU_HEADS * CHUNK), F32),
                   jax.ShapeDtypeStruct((CHUNK, W_GRP), F32)),
        in_specs=[_zslab(2 * W_GRP, 0), _full((SEQ, W_GRP)), _full((CHUNK, SGU_HEADS * CHUNK)), _full((CHUNK, W_GRP))],
        out_specs=(_full((SEQ, 2 * W_GRP)), _full((CHUNK, SGU_HEADS * CHUNK)), _full((CHUNK, W_GRP))),
        scratch_shapes=[pltpu.VMEM((SEQ, W_GRP), BF16), pltpu.VMEM((SEQ, W_GRP), F32),
                        pltpu.VMEM((SEQ, W_GRP), F32), pltpu.VMEM((SEQ, W_GRP), F32)],
        compiler_params=_params("arbitrary"),
    )(z, dy, w_cat, bias)


def _pool_window_of_lane(shape):
    grp = lax.broadcasted_iota(jnp.int32, shape, 1) // POOL_GROUP_DIM
    win = jnp.full(shape, POOL_WINDOWS[0], jnp.int32)
    for k in range(1, len(POOL_WINDOWS)):
        win = jnp.where(grp == k, POOL_WINDOWS[k], win)
    return grp, win


def _pool_select(levels, grp):
    out = levels[0]
    for k in range(1, len(levels)):
        out = jnp.where(grp == k, levels[k], out)
    return out


def _pool_p(z):
    shape = z.shape
    rows = lax.broadcasted_iota(jnp.int32, shape, 0)
    grp, win = _pool_window_of_lane(shape)
    levels, s, k = [], z, 1
    for _ in POOL_WINDOWS:
        s = s + _shift_down(s, k, rows)
        levels.append(s)
        k *= 2
    inv = 1.0 / jnp.minimum(rows + 1, win).astype(F32)
    return _pool_select(levels, grp) * inv - z, inv, rows, grp


def _pool_fwd(z, w_bd, scale):
    def body(z_ref, w_ref, s_ref, y_ref):
        p, _, _, _ = _pool_p(z_ref[...])
        y_ref[...] = _dot(p.astype(BF16), w_ref[...]) * s_ref[...]

    return pl.pallas_call(
        body, name="pool_fwd", grid=(1,),
        out_shape=jax.ShapeDtypeStruct((SEQ, W_GRP), F32),
        in_specs=[_zslab(W_GRP, 2), _full((W_GRP, W_GRP)), _full((1, W_GRP))],
        out_specs=_full((SEQ, W_GRP)), compiler_params=_params("arbitrary"),
    )(z, w_bd, scale)


def _pool_bwd(z, dy, w_bd, scale):
    def body(z_ref, dy_ref, w_ref, s_ref, dz_ref, dw_ref, ds_ref):
        p, inv, rows, grp = _pool_p(z_ref[...])
        pb = p.astype(BF16)
        dyv = dy_ref[...]
        ds_ref[...] = _sum0(dyv * _dot(pb, w_ref[...]))
        dpre = (dyv * s_ref[...]).astype(BF16)
        dw_ref[...] = _dot_tn(pb, dpre)
        dp = _dot_nt(dpre, w_ref[...])
        q = dp * inv
        levels, s, k = [], q, 1
        for _ in POOL_WINDOWS:
            s = s + _shift_up(s, k, rows)
            levels.append(s)
            k *= 2
        dz_ref[...] = _pool_select(levels, grp) - dp

    return pl.pallas_call(
        body, name="pool_bwd", grid=(1,),
        out_shape=(jax.ShapeDtypeStruct((SEQ, W_GRP), F32), jax.ShapeDtypeStruct((W_GRP, W_GRP), F32),
                   jax.ShapeDtypeStruct((1, W_GRP), F32)),
        in_specs=[_zslab(W_GRP, 2), _full((SEQ, W_GRP)), _full((W_GRP, W_GRP)), _full((1, W_GRP))],
        out_specs=(_full((SEQ, W_GRP)), _full((W_GRP, W_GRP)), _full((1, W_GRP))),
        compiler_params=_params("arbitrary"),
    )(z, dy, w_bd, scale)


def _conv_fwd(z, w):
    def body(z_ref, w_ref, y_ref):
        zc = z_ref[...]
        bg, cg, xh = zc[:, :W_GRP], zc[:, W_GRP:2 * W_GRP], zc[:, 2 * W_GRP:]
        rows = lax.broadcasted_iota(jnp.int32, (SEQ, W_GRP), 0)
        y = cg * xh
        conv = w_ref[0:1, :] * _shift_down(y, 2, rows) + w_ref[1:2, :] * _shift_down(y, 1, rows) + w_ref[2:3, :] * y
        y_ref[...] = bg * conv

    return pl.pallas_call(
        body, name="conv_fwd", grid=(1,),
        out_shape=jax.ShapeDtypeStruct((SEQ, W_GRP), F32),
        in_specs=[_zslab(3 * W_GRP, 1), _full((3, W_GRP))],
        out_specs=_full((SEQ, W_GRP)), compiler_params=_params("arbitrary"),
    )(z, w)


def _conv_bwd(z, dy, w):
    def body(z_ref, dy_ref, w_ref, dz_ref, dw_ref):
        zc = z_ref[...]
        bg, cg, xh = zc[:, :W_GRP], zc[:, W_GRP:2 * W_GRP], zc[:, 2 * W_GRP:]
        rows = lax.broadcasted_iota(jnp.int32, (SEQ, W_GRP), 0)
        y = cg * xh
        y2, y1 = _shift_down(y, 2, rows), _shift_down(y, 1, rows)
        conv = w_ref[0:1, :] * y2 + w_ref[1:2, :] * y1 + w_ref[2:3, :] * y
        dyv = dy_ref[...]
        dconv = dyv * bg
        dw_ref[...] = jnp.zeros_like(dw_ref)
        dw_ref[0:1, :] = _sum0(dconv * y2)
        dw_ref[1:2, :] = _sum0(dconv * y1)
        dw_ref[2:3, :] = _sum0(dconv * y)
        dyy = (w_ref[0:1, :] * _shift_up(dconv, 2, rows) + w_ref[1:2, :] * _shift_up(dconv, 1, rows)
               + w_ref[2:3, :] * dconv)
        dz_ref[:, :W_GRP] = dyv * conv
        dz_ref[:, W_GRP:2 * W_GRP] = dyy * xh
        dz_ref[:, 2 * W_GRP:] = dyy * cg

    return pl.pallas_call(
        body, name="conv_bwd", grid=(1,),
        out_shape=(jax.ShapeDtypeStruct((SEQ, 3 * W_GRP), F32), jax.ShapeDtypeStruct((SUBLANES, W_GRP), F32)),
        in_specs=[_zslab(3 * W_GRP, 1), _full((SEQ, W_GRP)), _full((3, W_GRP))],
        out_specs=(_full((SEQ, 3 * W_GRP)), _full((SUBLANES, W_GRP))),
        compiler_params=_params("arbitrary"),
    )(z, dy, w)


def _s5_disc(lre, lim, ldt, br, bi):
    dt = jnp.exp(ldt)
    mag = jnp.exp(lre * dt)
    ang = lim * dt
    a_re, a_im = mag * jnp.cos(ang), mag * jnp.sin(ang)
    nr, ni = a_re - 1.0, a_im
    den = lre * lre + lim * lim
    k_re = (nr * lre + ni * lim) / den
    k_im = (ni * lre - nr * lim) / den
    return a_re, a_im, k_re * br - k_im * bi, k_re * bi + k_im * br


def _s5_prep_fwd(lre, lim, ldt, br, bi):
    def body(lre_ref, lim_ref, ldt_ref, br_ref, bi_ref, ar_ref, ai_ref, bbr_ref, bbi_ref):
        ar, ai, bbr, bbi = _s5_disc(lre_ref[...], lim_ref[...], ldt_ref[...], br_ref[...], bi_ref[...])
        ar_ref[...] = ar
        ai_ref[...] = ai
        bbr_ref[...] = bbr
        bbi_ref[...] = bbi

    return pl.pallas_call(
        body, name="s5_prep_fwd",
        out_shape=(jax.ShapeDtypeStruct(lre.shape, F32), jax.ShapeDtypeStruct(lre.shape, F32),
                   jax.ShapeDtypeStruct(br.shape, F32), jax.ShapeDtypeStruct(br.shape, F32)),
        compiler_params=_params(),
    )(lre, lim, ldt, br, bi)


def _s5_prep_bwd(lre, lim, ldt, br, bi, dar, dai, dbbr, dbbi):
    def body(lre_ref, lim_ref, ldt_ref, br_ref, bi_ref, dar_ref, dai_ref, dbbr_ref, dbbi_ref,
             o_lre, o_lim, o_ldt, o_br, o_bi):
        _, pull = jax.vjp(_s5_disc, lre_ref[...], lim_ref[...], ldt_ref[...], br_ref[...], bi_ref[...])
        g = pull((dar_ref[...], dai_ref[...], dbbr_ref[...], dbbi_ref[...]))
        for o, v in zip((o_lre, o_lim, o_ldt, o_br, o_bi), g):
            o[...] = v

    return pl.pallas_call(
        body, name="s5_prep_bwd",
        out_shape=tuple(jax.ShapeDtypeStruct(a.shape, F32) for a in (lre, lim, ldt, br, bi)),
        compiler_params=_params(),
    )(lre, lim, ldt, br, bi, dar, dai, dbbr, dbbi)


def _cmul(ar, ai, br, bi):
    return ar * br - ai * bi, ar * bi + ai * br


def _s5_tile_consts(ar, ai, reverse):
    if reverse:
        ai = -ai
    shape = (SUBLANES, S5_BLOCK)
    row = lax.broadcasted_iota(jnp.int32, shape, 0)
    a1 = (jnp.broadcast_to(ar, shape), jnp.broadcast_to(ai, shape))
    a2 = _cmul(*a1, *a1)
    a4 = _cmul(*a2, *a2)
    a8 = _cmul(*a4, *a4)
    steps = []
    for s, (pr, pi) in ((1, a1), (2, a2), (4, a4)):
        keep = (row < SUBLANES - s) if reverse else (row >= s)
        steps.append((s, jnp.where(keep, pr, 0.0), jnp.where(keep, pi, 0.0)))
    e = (SUBLANES - row) if reverse else (row + 1)
    pr, pi = jnp.ones(shape, F32), jnp.zeros(shape, F32)
    for bit, (qr, qi) in ((1, a1), (2, a2), (4, a4), (8, a8)):
        nr, ni = _cmul(pr, pi, qr, qi)
        hit = (e & bit) != 0
        pr, pi = jnp.where(hit, nr, pr), jnp.where(hit, ni, pi)
    return steps, pr, pi


def _s5_tile(xr, xi, steps, reverse):
    for s, pr, pi in steps:
        sh = SUBLANES - s if reverse else s
        sr, si = pltpu.roll(xr, sh, axis=0), pltpu.roll(xi, sh, axis=0)
        xr, xi = xr + pr * sr - pi * si, xi + pr * si + pi * sr
    return xr, xi


N_TILES = SEQ // SUBLANES


def _s5_fwd(z, b_re, b_im, c_re, c_im, a_re, a_im, d, glu_w, glu_b):
    nblk = S5_LANES // S5_BLOCK

    def body(u_ref, br_ref, bi_ref, cr_ref, ci_ref, ar_ref, ai_ref, d_ref, gw_ref, gb_ref,
             y_ref, y0_ref, xr_ref, xi_ref, ub_scr, acc_scr):
        jb = pl.program_id(0)

        @pl.when(jb == 0)
        def _():
            ub_scr[...] = u_ref[...].astype(BF16)
            acc_scr[...] = jnp.zeros_like(acc_scr)

        ub = ub_scr[...]
        xr_ref[...] = _dot(ub, br_ref[...])
        xi_ref[...] = _dot(ub, bi_ref[...])
        steps, pr, pi = _s5_tile_consts(ar_ref[...], ai_ref[...], False)

        def tile(t, carry):
            cr, ci = carry
            rows = pl.ds(pl.multiple_of(t * SUBLANES, SUBLANES), SUBLANES)
            xr, xi = _s5_tile(xr_ref[rows, :], xi_ref[rows, :], steps, False)
            xr, xi = xr + pr * cr - pi * ci, xi + pr * ci + pi * cr
            xr_ref[rows, :] = xr
            xi_ref[rows, :] = xi
            return xr[SUBLANES - 1:, :], xi[SUBLANES - 1:, :]

        zero = jnp.zeros((1, S5_BLOCK), F32)
        lax.fori_loop(0, N_TILES, tile, (zero, zero), unroll=2)
        acc_scr[...] += (_dot(xr_ref[...].astype(BF16), cr_ref[...]) - _dot(xi_ref[...].astype(BF16), ci_ref[...]))

        @pl.when(jb == nblk - 1)
        def _():
            y0 = acc_scr[...] + d_ref[...] * u_ref[...]
            y0_ref[...] = y0
            y1 = _gelu(y0)
            y_ref[...] = y1 * _sigmoid(_dot(y1.astype(BF16), gw_ref[...]) + gb_ref[...])

    lane_blk = pl.BlockSpec((SEQ, S5_BLOCK), lambda j: (0, j))
    return pl.pallas_call(
        body, name="s5_fwd", grid=(nblk,),
        out_shape=(jax.ShapeDtypeStruct((SEQ, W_GRP), F32), jax.ShapeDtypeStruct((SEQ, W_GRP), F32),
                   jax.ShapeDtypeStruct((SEQ, S5_LANES), F32), jax.ShapeDtypeStruct((SEQ, S5_LANES), F32)),
        in_specs=[_zslab(W_GRP, 6),
                  pl.BlockSpec((W_GRP, S5_BLOCK), lambda j: (0, j)), pl.BlockSpec((W_GRP, S5_BLOCK), lambda j: (0, j)),
                  pl.BlockSpec((S5_BLOCK, W_GRP), lambda j: (j, 0)), pl.BlockSpec((S5_BLOCK, W_GRP), lambda j: (j, 0)),
                  pl.BlockSpec((1, S5_BLOCK), lambda j: (0, j)), pl.BlockSpec((1, S5_BLOCK), lambda j: (0, j)),
                  _full((1, W_GRP)), _full((W_GRP, W_GRP)), _full((1, W_GRP))],
        out_specs=(_full((SEQ, W_GRP)), _full((SEQ, W_GRP)), lane_blk, lane_blk),
        scratch_shapes=[pltpu.VMEM((SEQ, W_GRP), BF16), pltpu.VMEM((SEQ, W_GRP), F32)],
        compiler_params=_params("arbitrary"),
    )(z, b_re, b_im, c_re, c_im, a_re, a_im, d, glu_w, glu_b)


def _s5_bwd(z, y0, dy, xr, xi, b_re, b_im, c_re, c_im, a_re, a_im, d, glu_w, glu_b):
    nblk = S5_LANES // S5_BLOCK

    def body(u_ref, y0_ref, dy_ref, xr_ref, xi_ref, br_ref, bi_ref, cr_ref, ci_ref, ar_ref, ai_ref,
             d_ref, gw_ref, gb_ref,
             du_ref, dbr_ref, dbi_ref, dcr_ref, dci_ref, dar_ref, dai_ref, dd_ref, dgw_ref, dgb_ref,
             ub_scr, dy0_scr, du_scr, lr_scr, li_scr):
        jb = pl.program_id(0)

        @pl.when(jb == 0)
        def _():
            u = u_ref[...]
            y0v = y0_ref[...]
            y1 = _gelu(y0v)
            y1b = y1.astype(BF16)
            sg = _sigmoid(_dot(y1b, gw_ref[...]) + gb_ref[...])
            dyv = dy_ref[...]
            dpre = dyv * y1 * sg * (1.0 - sg)
            dpb = dpre.astype(BF16)
            dgw_ref[...] = _dot_tn(y1b, dpb)
            dgb_ref[...] = _sum0(dpre)
            dy1 = dyv * sg + _dot_nt(dpb, gw_ref[...])
            dy0 = dy1 * _gelu_grad(y0v)
            dd_ref[...] = _sum0(dy0 * u)
            du_scr[...] = dy0 * d_ref[...]
            dy0_scr[...] = dy0.astype(BF16)
            ub_scr[...] = u.astype(BF16)

        dy0b = dy0_scr[...]
        lr_scr[...] = _dot_nt(dy0b, cr_ref[...])
        li_scr[...] = -_dot_nt(dy0b, ci_ref[...])
        dcr_ref[...] = _dot_tn(xr_ref[...].astype(BF16), dy0b)
        dci_ref[...] = -_dot_tn(xi_ref[...].astype(BF16), dy0b)
        steps, pr, pi = _s5_tile_consts(ar_ref[...], ai_ref[...], True)
        row = lax.broadcasted_iota(jnp.int32, (SUBLANES, S5_BLOCK), 0)

        def tile(k, carry):
            cr, ci, accr, acci = carry
            t = N_TILES - 1 - k
            rows = pl.ds(pl.multiple_of(t * SUBLANES, SUBLANES), SUBLANES)
            lr, li = _s5_tile(lr_scr[rows, :], li_scr[rows, :], steps, True)
            lr, li = lr + pr * cr - pi * ci, li + pr * ci + pi * cr
            lr_scr[rows, :] = lr
            li_scr[rows, :] = li
            prev = pl.ds(pl.multiple_of(jnp.maximum(t - 1, 0) * SUBLANES, SUBLANES), SUBLANES)
            live = jnp.where(t > 0, 1.0, 0.0)
            xpr = jnp.where(row == 0, pltpu.roll(xr_ref[prev, :], 1, axis=0) * live, pltpu.roll(xr_ref[rows, :], 1, axis=0))
            xpi = jnp.where(row == 0, pltpu.roll(xi_ref[prev, :], 1, axis=0) * live, pltpu.roll(xi_ref[rows, :], 1, axis=0))
            accr = accr + lr * xpr + li * xpi
            acci = acci + li * xpr - lr * xpi
            return lr[0:1, :], li[0:1, :], accr, acci

        zero = jnp.zeros((1, S5_BLOCK), F32)
        zt = jnp.zeros((SUBLANES, S5_BLOCK), F32)
        _, _, accr, acci = lax.fori_loop(0, N_TILES, tile, (zero, zero, zt, zt), unroll=2)
        dar_ref[...] = jnp.zeros_like(dar_ref)
        dai_ref[...] = jnp.zeros_like(dai_ref)
        dar_ref[0:1, :] = _sum0(accr)
        dai_ref[0:1, :] = _sum0(acci)
        lrb, lib = lr_scr[...].astype(BF16), li_scr[...].astype(BF16)
        ub = ub_scr[...]
        dbr_ref[...] = _dot_tn(ub, lrb)
        dbi_ref[...] = _dot_tn(ub, lib)
        du_scr[...] += _dot_nt(lrb, br_ref[...]) + _dot_nt(lib, bi_ref[...])

        @pl.when(jb == nblk - 1)
        def _():
            du_ref[...] = du_scr[...]

    lane_blk = pl.BlockSpec((SEQ, S5_BLOCK), lambda j: (0, j))
    bspec = pl.BlockSpec((W_GRP, S5_BLOCK), lambda j: (0, j))
    cspec = pl.BlockSpec((S5_BLOCK, W_GRP), lambda j: (j, 0))
    aspec = pl.BlockSpec((1, S5_BLOCK), lambda j: (0, j))
    a8spec = pl.BlockSpec((SUBLANES, S5_BLOCK), lambda j: (0, j))
    sd = jax.ShapeDtypeStruct
    return pl.pallas_call(
        body, name="s5_bwd", grid=(nblk,),
        out_shape=(sd((SEQ, W_GRP), F32), sd((W_GRP, S5_LANES), F32), sd((W_GRP, S5_LANES), F32),
                   sd((S5_LANES, W_GRP), F32), sd((S5_LANES, W_GRP), F32),
                   sd((SUBLANES, S5_LANES), F32), sd((SUBLANES, S5_LANES), F32),
                   sd((1, W_GRP), F32), sd((W_GRP, W_GRP), F32), sd((1, W_GRP), F32)),
        in_specs=[_zslab(W_GRP, 6), _full((SEQ, W_GRP)), _full((SEQ, W_GRP)), lane_blk, lane_blk,
                  bspec, bspec, cspec, cspec, aspec, aspec,
                  _full((1, W_GRP)), _full((W_GRP, W_GRP)), _full((1, W_GRP))],
        out_specs=(_full((SEQ, W_GRP)), bspec, bspec, cspec, cspec, a8spec, a8spec,
                   _full((1, W_GRP)), _full((W_GRP, W_GRP)), _full((1, W_GRP))),
        scratch_shapes=[pltpu.VMEM((SEQ, W_GRP), BF16), pltpu.VMEM((SEQ, W_GRP), BF16), pltpu.VMEM((SEQ, W_GRP), F32),
                        pltpu.VMEM((SEQ, S5_BLOCK), F32), pltpu.VMEM((SEQ, S5_BLOCK), F32)],
        compiler_params=_params("arbitrary"),
    )(z, y0, dy, xr, xi, b_re, b_im, c_re, c_im, a_re, a_im, d, glu_w, glu_b)


def _head(x, g, target):
    tm = TOKEN_TILE
    n = SEQ // tm

    def body(x_ref, g_ref, t_ref, dx_ref, st_ref, acc_scr):
        i = pl.program_id(0)

        @pl.when(i == 0)
        def _():
            acc_scr[...] = jnp.zeros_like(acc_scr)

        xhat, r = _rms(x_ref[...])
        gv = g_ref[...]
        err = xhat * gv - t_ref[...]
        dyv = err * (1.0 / D_MODEL)
        dx_ref[...] = _rms_bwd(xhat, r, dyv * gv)
        acc_scr[0:1, :] += _sum0(err * err)
        acc_scr[1:2, :] += _sum0(dyv * xhat)

        @pl.when(i == n - 1)
        def _():
            st_ref[...] = acc_scr[...]
            tot = jnp.sum(acc_scr[0:1, :], axis=-1, keepdims=True) * (0.5 / D_MODEL)
            st_ref[0:1, :] = jnp.broadcast_to(tot, (1, D_MODEL))

    tok = pl.BlockSpec((tm, D_MODEL), lambda i: (i, 0))
    return pl.pallas_call(
        body, name="head", grid=(n,),
        out_shape=(jax.ShapeDtypeStruct((SEQ, D_MODEL), F32), jax.ShapeDtypeStruct((SUBLANES, D_MODEL), F32)),
        in_specs=[tok, _row_spec(1), tok], out_specs=(tok, _row_spec(SUBLANES)),
        scratch_shapes=[pltpu.VMEM((SUBLANES, D_MODEL), F32)],
        compiler_params=_params("arbitrary"),
    )(x, g, target)


def _adamw(w, gparts, m, v, name):
    r, c = w.shape
    npart = gparts.shape[0]
    tr = r
    for cand in (512, 256, 128, 64, 32, 16):
        if r % cand == 0 and r > cand:
            tr = cand
            break
    b1c = 1.0 - ADAM_B1 ** ADAM_STEP
    b2c = 1.0 - ADAM_B2 ** ADAM_STEP

    def body(w_ref, g_ref, m_ref, v_ref, go_ref, d_ref, mo_ref, vo_ref):
        g = g_ref[0].astype(F32)
        for k in range(1, npart):
            g = g + g_ref[k].astype(F32)
        mn = ADAM_B1 * m_ref[...] + (1.0 - ADAM_B1) * g
        vn = ADAM_B2 * v_ref[...] + (1.0 - ADAM_B2) * (g * g)
        m_hat = mn / b1c
        v_hat = vn / b2c
        go_ref[...] = g
        d_ref[...] = -ADAM_LR * (m_hat / (jnp.sqrt(v_hat) + ADAM_EPS) + ADAM_WD * w_ref[...])
        mo_ref[...] = mn
        vo_ref[...] = vn

    blk = pl.BlockSpec((tr, c), lambda i: (i, 0))
    sh = jax.ShapeDtypeStruct((r, c), F32)
    return pl.pallas_call(
        body, name=name, grid=(r // tr,),
        out_shape=(sh, sh, sh, sh),
        in_specs=[blk, pl.BlockSpec((npart, tr, c), lambda i: (0, i, 0)), blk, blk],
        out_specs=(blk, blk, blk, blk), compiler_params=_params("arbitrary"),
    )(w, gparts, m, v)


def _adamw_layer(l, w, gparts, m, v, prev, name):
    _, r, c = w.shape
    npart = gparts.shape[0]
    tr = max(t for t in range(16, 513, 16) if r % t == 0)
    b1c = 1.0 - ADAM_B1 ** ADAM_STEP
    b2c = 1.0 - ADAM_B2 ** ADAM_STEP
    nprev = 0 if prev is None else 4

    def body(*refs):
        w_ref, g_ref, m_ref, v_ref = refs[:4]
        go_ref, d_ref, mo_ref, vo_ref = refs[4 + nprev:]
        g = g_ref[0].astype(F32)
        for k in range(1, npart):
            g = g + g_ref[k].astype(F32)
        mn = ADAM_B1 * m_ref[0] + (1.0 - ADAM_B1) * g
        vn = ADAM_B2 * v_ref[0] + (1.0 - ADAM_B2) * (g * g)
        go_ref[0] = g
        d_ref[0] = -ADAM_LR * ((mn / b1c) / (jnp.sqrt(vn / b2c) + ADAM_EPS) + ADAM_WD * w_ref[0])
        mo_ref[0] = mn
        vo_ref[0] = vn

    blk = pl.BlockSpec((1, tr, c), lambda i: (l, i, 0))
    sh = jax.ShapeDtypeStruct(w.shape, F32)
    keep = [pl.BlockSpec(memory_space=pl.ANY)] * nprev
    return pl.pallas_call(
        body, name=name, grid=(r // tr,),
        out_shape=(sh, sh, sh, sh),
        in_specs=[blk, pl.BlockSpec((npart, tr, c), lambda i: (0, i, 0)), blk, blk, *keep],
        out_specs=(blk, blk, blk, blk),
        input_output_aliases={4 + k: k for k in range(nprev)},
        compiler_params=_params("arbitrary"),
    )(w, gparts, m, v, *(prev or ()))


def _sum_parts(parts, name):
    n, r, c = parts.shape

    def body(p_ref, o_ref):
        acc = p_ref[0]
        for k in range(1, n):
            acc = acc + p_ref[k]
        o_ref[...] = acc

    return pl.pallas_call(
        body, name=name, out_shape=jax.ShapeDtypeStruct((r, c), F32), compiler_params=_params(),
    )(parts)


def _block_diag(blocks):
    g, a, b = blocks.shape
    eye = jnp.eye(g, dtype=blocks.dtype)
    return (blocks[:, :, None, :] * eye[:, None, :, None]).reshape(g * a, g * b)


def _diag_blocks(dense, g):
    a, b = dense.shape[0] // g, dense.shape[1] // g
    d4 = dense.reshape(g, a, g, b)
    eye = jnp.eye(g, dtype=dense.dtype)
    return jnp.sum(d4 * eye[:, None, :, None], axis=2)


def _pack(parts, cols):
    flat = jnp.concatenate([p.reshape(-1) for p in parts])
    unit = N_DEV * SUBLANES * cols
    total = -(-flat.shape[0] // unit) * unit
    flat = jnp.pad(flat, (0, total - flat.shape[0]))
    return flat.reshape(N_DEV, total // (N_DEV * cols), cols)


def _unpack(flat, shapes):
    out, pos = [], 0
    for s in shapes:
        n = math.prod(s)
        out.append(flat[pos:pos + n].reshape(s))
        pos += n
    return out


SMALL = ("norm1_g", "norm2_g", "sgu_w", "sgu_b", "pool_w", "pool_scale", "conv_w", "s5_lambda_re", "s5_lambda_im",
         "s5_b_re", "s5_b_im", "s5_c_re", "s5_c_im", "s5_d", "s5_log_dt", "s5_glu_w", "s5_glu_b", "mix_norm_g",
         "norm3_g", "final_norm_g")
BIG = ("ffn1_w_in", "ffn1_w_out", "w_mix_in", "w_mix_out", "ffn2_w_in", "ffn2_w_out")
TRANSPOSED = ("ffn1_w_in", "w_mix_in", "ffn2_w_in")
WEIGHTS = ("ada_w", "ada_b", "norm1_g", "ffn1_w_in", "ffn1_w_out", "norm2_g", "w_mix_in", "sgu_w", "sgu_b", "pool_w",
           "pool_scale", "conv_w", "s5_lambda_re", "s5_lambda_im", "s5_b_re", "s5_b_im", "s5_c_re", "s5_c_im", "s5_d",
           "s5_log_dt", "s5_glu_w", "s5_glu_b", "mix_norm_g", "w_mix_out", "norm3_g", "ffn2_w_in", "ffn2_w_out",
           "final_norm_g")
PACK_COLS = 1024


def kernel(x, c, ada_w, ada_b, norm1_g, ffn1_w_in, ffn1_w_out, norm2_g, w_mix_in, sgu_w, sgu_b, pool_w, pool_scale, conv_w, s5_lambda_re, s5_lambda_im, s5_b_re, s5_b_im, s5_c_re, s5_c_im, s5_d, s5_log_dt, s5_glu_w, s5_glu_b, mix_norm_g, w_mix_out, norm3_g, ffn2_w_in, ffn2_w_out, final_norm_g, loss_target, m_ada_w, m_ada_b, m_norm1_g, m_ffn1_w_in, m_ffn1_w_out, m_norm2_g, m_w_mix_in, m_sgu_w, m_sgu_b, m_pool_w, m_pool_scale, m_conv_w, m_s5_lambda_re, m_s5_lambda_im, m_s5_b_re, m_s5_b_im, m_s5_c_re, m_s5_c_im, m_s5_d, m_s5_log_dt, m_s5_glu_w, m_s5_glu_b, m_mix_norm_g, m_w_mix_out, m_norm3_g, m_ffn2_w_in, m_ffn2_w_out, m_final_norm_g, v_ada_w, v_ada_b, v_norm1_g, v_ffn1_w_in, v_ffn1_w_out, v_norm2_g, v_w_mix_in, v_sgu_w, v_sgu_b, v_pool_w, v_pool_scale, v_conv_w, v_s5_lambda_re, v_s5_lambda_im, v_s5_b_re, v_s5_b_im, v_s5_c_re, v_s5_c_im, v_s5_d, v_s5_log_dt, v_s5_glu_w, v_s5_glu_b, v_mix_norm_g, v_w_mix_out, v_norm3_g, v_ffn2_w_in, v_ffn2_w_out, v_final_norm_g):
    args = dict(locals())
    W = {n: args[n] for n in WEIGHTS}
    M = {n: args["m_" + n] for n in WEIGHTS}
    V = {n: args["v_" + n] for n in WEIGHTS}
    me = _me()
    L = DEPTH
    x0 = x[0]
    target = loss_target[0]

    conv_cols = conv_w.shape[-1]
    glu_rows = s5_glu_w.shape[1]
    c_g, conv_g, glu_g = _exchange(
        [c.reshape(SUBLANES, LANES), conv_w.reshape(L * 3, conv_cols), s5_glu_w.reshape(L * glu_rows, W_GRP)],
        False, "gather_small")
    c_all = c_g.reshape(N_DEV, D_MODEL)
    conv_full = conv_g.reshape(N_DEV, L, 3, conv_cols).transpose(1, 2, 0, 3).reshape(L, 3, W_GRP)
    glu_full = glu_g.reshape(N_DEV, L, glu_rows, W_GRP).transpose(1, 0, 2, 3).reshape(L, W_GRP, W_GRP)

    ncol = ada_w.shape[-1]
    ada_b_mine = lax.dynamic_slice_in_dim(ada_b, me * ncol, ncol, axis=1).reshape(L, 1, ncol)
    cond_part = _cond_fwd(c_all, ada_w, ada_b_mine)
    (cond_g,) = _exchange([cond_part.reshape(L * N_DEV, ncol)], False, "gather_cond")
    cond_g = cond_g.reshape(N_DEV, L, N_DEV, ncol)
    cond_mine = lax.dynamic_index_in_dim(cond_g, me, axis=2, keepdims=False)
    cond = cond_mine.transpose(1, 0, 2).reshape(L, N_ADA, D_MODEL)

    lg = L * S5_GROUPS
    lre3 = s5_lambda_re.reshape(lg, S5_STATE, 1)
    lim3 = s5_lambda_im.reshape(lg, S5_STATE, 1)
    ldt3 = s5_log_dt.reshape(lg, 1, 1)
    br3 = s5_b_re.reshape(lg, S5_STATE, S5_GROUP_CH)
    bi3 = s5_b_im.reshape(lg, S5_STATE, S5_GROUP_CH)

    def b_mat(bb3, l):
        return _block_diag(bb3.reshape(L, S5_GROUPS, S5_STATE, S5_GROUP_CH)[l].transpose(0, 2, 1)).astype(BF16)

    def c_mat(cw, l):
        return _block_diag(cw[l].transpose(0, 2, 1)).astype(BF16)

    for n in TRANSPOSED:
        W[n], M[n], V[n] = (a.transpose(0, 2, 1) for a in (W[n], M[n], V[n]))

    gathered_shape = {"ffn1_w_out": (N_FF_CHUNK, FF_PIECE, D_MODEL), "ffn2_w_out": (N_FF_CHUNK, FF_PIECE, D_MODEL),
                      "w_mix_in": (P_IN, D_MODEL), "w_mix_out": (D_MODEL, D_MODEL)}

    def gather_start(l, after, names=BIG):
        srcs = [W[n][l].astype(BF16) for n in names]
        lands = _place_own(srcs, False, "gather_weights_own")
        return _exchange_start(srcs, lands, after, False, "gather_weights_start", OTHER_CHIPS)

    def gather_finish(handle, after, names=BIG):
        g = _exchange_wait(handle, after, False, "gather_weights_wait", OTHER_CHIPS)
        out = dict(zip(names, _gather_sibling(g, "gather_weights_sibling")))
        return {n: (a.reshape(gathered_shape[n]) if n in gathered_shape else a) for n, a in out.items()}

    saved = []
    xc = x0
    first_ffn = BIG[:2]
    handle, token = gather_start(0, cond, first_ffn)
    handle_rest, token = gather_start(0, token, BIG[2:])

    zero = token[0, 0]
    a_re3, a_im3, bb_re3, bb_im3 = _s5_prep_fwd(lre3 + zero, lim3, ldt3, br3, bi3)
    a_re = a_re3.reshape(L, 1, S5_LANES)
    a_im = a_im3.reshape(L, 1, S5_LANES)
    sgu_w_t, pool_w_t, c_re_t, c_im_t = (a + zero for a in (sgu_w, pool_w, s5_c_re, s5_c_im))

    def mixer_consts(l):
        w_cat = sgu_w_t[l].transpose(1, 0, 2).reshape(CHUNK, SGU_HEADS * CHUNK)
        bias = jnp.repeat(sgu_b[l].T, SGU_HEAD_DIM, axis=1)
        return dict(
            w_cat=w_cat, bias=bias, pool_bd=_block_diag(pool_w_t[l]).astype(BF16), pool_scale=pool_scale[l][None],
            conv=conv_full[l], b_re=b_mat(bb_re3, l), b_im=b_mat(bb_im3, l), c_re=c_mat(c_re_t, l),
            c_im=c_mat(c_im_t, l), a_re=a_re[l], a_im=a_im[l], d=s5_d[l][None], glu_w=glu_full[l].astype(BF16),
            glu_b=s5_glu_b[l][None])

    mcs = [mixer_consts(l) for l in range(L)]
    small_names = SMALL + ("ada_b",)
    first_small = W[small_names[0]] + zero
    packed_w = _pack([first_small] + [W[n] for n in small_names[1:]], PACK_COLS)
    packed_m = _pack([M[n] + zero if n == small_names[0] else M[n] for n in small_names], PACK_COLS)
    packed_v = _pack([V[n] + zero if n == small_names[0] else V[n] for n in small_names], PACK_COLS)
    wl = gather_finish(handle, [x0] + [a for mc in mcs for a in mc.values()], first_ffn)
    for l in range(L):
        cl = cond[l]
        if 0 < l < L - 1:
            handle, token = gather_start(l + 1, wl["ffn1_w_in"])
            cl = cl + token[0, 0]
        mc = mcs[l]
        x_a = xc
        x_b, *ffn1_kept = _ffn_fwd(x_a, cl[0:3], norm1_g[l][None], wl["ffn1_w_in"], wl["ffn1_w_out"])
        if l == 0:
            wl = {**wl, **gather_finish(handle_rest, [x_b, packed_w, packed_m, packed_v], BIG[2:])}
            handle, token = gather_start(1, wl["w_mix_in"])
            cl = cl + token[0, 0]
        z = _mix_in_fwd(x_b, cl[3:5], norm2_g[l][None], wl["w_mix_in"])
        ya = _sgu_fwd(z, mc["w_cat"], mc["bias"])
        yb = _pool_fwd(z, mc["pool_bd"], mc["pool_scale"])
        yc = _conv_fwd(z, mc["conv"])
        yd, y0, sxr, sxi = _s5_fwd(z, mc["b_re"], mc["b_im"], mc["c_re"], mc["c_im"], mc["a_re"], mc["a_im"],
                                   mc["d"], mc["glu_w"], mc["glu_b"])
        ys = (ya, yb, yc, yd)
        x_c = _mix_out_fwd(x_b, ys, mix_norm_g[l][None], cl[5:6], wl["w_mix_out"])
        x_d, *ffn2_kept = _ffn_fwd(x_c, cl[6:9], norm3_g[l][None], wl["ffn2_w_in"], wl["ffn2_w_out"])
        saved.append(dict(wl=wl, mc=mc, x_a=x_a, x_b=x_b, x_c=x_c, ffn1=ffn1_kept, ffn2=ffn2_kept, z=z, ys=ys, y0=y0,
                          xr=sxr, xi=sxi))
        xc = x_d
        if l + 1 < L:
            wl = gather_finish(handle, x_d)

    dx, stats = _head(xc, final_norm_g[None], target)
    loss = lax.psum(stats[0, 0], MESH_AXES)

    small_grads = {n: [None] * L for n in SMALL if n != "final_norm_g"}
    small_grads["final_norm_g"] = stats[1]
    big_out = {n: None for n in BIG}
    pending = None

    def finish_scatter(pend, after, names=BIG):
        layer, hnd = pend
        recv = _exchange_wait(hnd, after, True, "scatter_grads_wait")
        for n, r in zip(names, recv):
            big_out[n] = _adamw_layer(layer, W[n], r, M[n], V[n], big_out[n], "adamw_" + n)
        return recv[0]

    dcond_rows = [None] * L
    d_are, d_aim, d_bbre, d_bbim = [None] * L, [None] * L, [None] * L, [None] * L
    token = None
    for l in reversed(range(L)):
        sv = saved[l]
        wl, mc, cl = sv["wl"], sv["mc"], cond[l]
        if token is not None:
            cl = cl + token[0, 0]
        f2, a2, b2, hb = sv["ffn2"]
        dx, da, db, act, dob, part3 = _ffn_bwd(dx, sv["x_c"], f2, a2, b2, cl[6:9], norm3_g[l][None],
                                               wl["ffn2_w_in"], wl["ffn2_w_out"])
        g_ffn2_in = _dw(da, hb, db, name="dw_ffn_in")
        g_ffn2_out = _dw(act, dob, name="dw_ffn_out").reshape(N_DEV, D_FF // N_DEV, D_MODEL)
        dya, dyb, dyc, dyd, ynb, dmob, part_mo = _mix_out_bwd(dx, sv["ys"], mix_norm_g[l][None], cl[5:6],
                                                              wl["w_mix_out"])
        g_mix_out = _dw(ynb[None], dmob, name="dw_mix_out").reshape(N_DEV, D_MODEL // N_DEV, D_MODEL)
        z = sv["z"]
        dza, dw_cat, dbias = _sgu_bwd(z, dya, mc["w_cat"], mc["bias"])
        dzb, dpool_dense, dpool_scale = _pool_bwd(z, dyb, mc["pool_bd"], mc["pool_scale"])
        dzc, dconv8 = _conv_bwd(z, dyc, mc["conv"])
        (dzd, dbre_d, dbim_d, dcre_d, dcim_d, dar8, dai8, dd, dglu_w, dglu_b) = _s5_bwd(
            z, sv["y0"], dyd, sv["xr"], sv["xi"], mc["b_re"], mc["b_im"], mc["c_re"], mc["c_im"],
            mc["a_re"], mc["a_im"], mc["d"], mc["glu_w"], mc["glu_b"])
        dx, h2b, dzbf, part2 = _mix_in_bwd((dza, dzb, dzc, dzd), sv["x_b"], dx, cl[3:5], norm2_g[l][None],
                                           wl["w_mix_in"])
        g_mix_in = _dw(dzbf[None], h2b, name="dw_mix_in").reshape(N_DEV, P_IN // N_DEV, D_MODEL)
        cl1 = cl[0:3]
        if l == 0:
            early = [g_mix_in, g_mix_out, g_ffn2_in, g_ffn2_out]
            handle, token = _exchange_start(early, _place_own(early, True, "scatter_grads_own"), g_mix_in, True,
                                            "scatter_grads_start")
            pending_early = (0, handle)
            cl1 = cl1 + token[0, 0]
        f1, a1, b1, hb = sv["ffn1"]
        dx, da, db, act, dob, part1 = _ffn_bwd(dx, sv["x_a"], f1, a1, b1, cl1, norm1_g[l][None],
                                               wl["ffn1_w_in"], wl["ffn1_w_out"])
        g_ffn1_in = _dw(da, hb, db, name="dw_ffn_in")
        g_ffn1_out = _dw(act, dob, name="dw_ffn_out").reshape(N_DEV, D_FF // N_DEV, D_MODEL)
        last = finish_scatter(pending, g_ffn1_in) if pending is not None else g_ffn1_in
        if l > 0:
            pieces = [g_ffn1_in, g_ffn1_out, g_mix_in, g_mix_out, g_ffn2_in, g_ffn2_out]
            lands = _place_own(pieces, True, "scatter_grads_own")
            handle, token = _exchange_start(pieces, lands, last, True, "scatter_grads_start")
            pending = (l, handle)
        else:
            pieces = [g_ffn1_in, g_ffn1_out]
            first_layer = (pieces, _place_own(pieces, True, "scatter_grads_own"), last)
        dcond_rows[l] = jnp.concatenate([part1[0:3], part2[0:2], part_mo[0:1], part3[0:3]], axis=0)
        sg = small_grads
        sg["norm1_g"][l] = part1[3]
        sg["norm2_g"][l] = part2[2]
        sg["norm3_g"][l] = part3[3]
        sg["mix_norm_g"][l] = part_mo[1]
        sg["sgu_w"][l] = dw_cat.reshape(CHUNK, SGU_HEADS, CHUNK).transpose(1, 0, 2)
        sg["sgu_b"][l] = dbias[:, ::SGU_HEAD_DIM].T
        sg["pool_w"][l] = _diag_blocks(dpool_dense, len(POOL_WINDOWS))
        sg["pool_scale"][l] = dpool_scale[0]
        sg["conv_w"][l] = dconv8[0:3]
        sg["s5_c_re"][l] = _diag_blocks(dcre_d, S5_GROUPS).transpose(0, 2, 1)
        sg["s5_c_im"][l] = _diag_blocks(dcim_d, S5_GROUPS).transpose(0, 2, 1)
        sg["s5_d"][l] = dd[0]
        sg["s5_glu_w"][l] = dglu_w
        sg["s5_glu_b"][l] = dglu_b[0]
        d_are[l], d_aim[l] = dar8[0], dai8[0]
        d_bbre[l] = _diag_blocks(dbre_d, S5_GROUPS).transpose(0, 2, 1)
        d_bbim[l] = _diag_blocks(dbim_d, S5_GROUPS).transpose(0, 2, 1)
    grad_x = dx

    g_lre, g_lim, g_ldt, g_br, g_bi = _s5_prep_bwd(
        lre3, lim3, ldt3, br3, bi3,
        jnp.stack(d_are).reshape(lg, S5_STATE, 1), jnp.stack(d_aim).reshape(lg, S5_STATE, 1),
        jnp.stack(d_bbre).reshape(lg, S5_STATE, S5_GROUP_CH), jnp.stack(d_bbim).reshape(lg, S5_STATE, S5_GROUP_CH))
    small = {n: (jnp.stack(v) if isinstance(v, list) and v[0] is not None else v) for n, v in small_grads.items()}
    small["s5_lambda_re"] = g_lre.reshape(s5_lambda_re.shape)
    small["s5_lambda_im"] = g_lim.reshape(s5_lambda_im.shape)
    small["s5_log_dt"] = g_ldt.reshape(s5_log_dt.shape)
    small["s5_b_re"] = g_br.reshape(s5_b_re.shape)
    small["s5_b_im"] = g_bi.reshape(s5_b_im.shape)

    small_shapes = [(L, 3, W_GRP) if n == "conv_w" else (L, W_GRP, W_GRP) if n == "s5_glu_w" else W[n].shape
                    for n in SMALL]
    packed = _pack([small[n].reshape(s) for n, s in zip(SMALL, small_shapes)], PACK_COLS)
    (pieces,) = _exchange([packed], True, "scatter_small")
    mine = _sum_parts(pieces, "sum_small")
    dcond = jnp.stack(dcond_rows).reshape(L * N_ADA, D_MODEL)
    summed, dcond_g = _exchange([mine, dcond], False, "gather_small_sums")
    small_sum = dict(zip(SMALL, _unpack(summed.reshape(-1), small_shapes)))
    small_sum["conv_w"] = lax.dynamic_slice_in_dim(small_sum["conv_w"], me * conv_cols, conv_cols, axis=2)
    small_sum["s5_glu_w"] = lax.dynamic_slice_in_dim(small_sum["s5_glu_w"], me * glu_rows, glu_rows, axis=1)

    g_pieces, g_lands, g_last = first_layer
    behind = jnp.stack([summed[0, 0, 0], dcond_g[0, 0, 0], g_last[0, 0, 0].astype(F32)])
    handle, token = _exchange_start(g_pieces, g_lands, behind, True, "scatter_grads_start")
    pending = (0, handle)
    dcond_all = dcond_g.reshape(N_DEV, L, N_ADA * D_MODEL).transpose(1, 0, 2)
    dcond_mine = lax.dynamic_slice_in_dim(dcond_all, me * ncol, ncol, axis=2) + token[0, 0]
    g_ada_w, g_ada_b = _cond_bwd(c_all.T, dcond_mine, dcond_all)

    grads, deltas, new_m, new_v = {}, {}, {}, {}
    out = _adamw(ada_w.reshape(L * D_MODEL, ncol), g_ada_w.reshape(1, L * D_MODEL, ncol),
                 m_ada_w.reshape(L * D_MODEL, ncol), v_ada_w.reshape(L * D_MODEL, ncol), "adamw_ada_w")
    grads["ada_w"], deltas["ada_w"], new_m["ada_w"], new_v["ada_w"] = (o.reshape(ada_w.shape) for o in out)
    small_g = dict(small_sum)
    small_g["ada_b"] = g_ada_b.reshape(ada_b.shape)
    shapes = [W[n].shape for n in small_names]
    rows = packed_w.shape[0] * packed_w.shape[1]
    out = _adamw(packed_w.reshape(rows, PACK_COLS),
                 _pack([small_g[n] for n in small_names], PACK_COLS).reshape(1, rows, PACK_COLS),
                 packed_m.reshape(rows, PACK_COLS), packed_v.reshape(rows, PACK_COLS), "adamw_small")
    for store, o in zip((grads, deltas, new_m, new_v), out):
        store.update(zip(small_names, _unpack(o.reshape(-1), shapes)))
    finish_scatter(pending_early, out[0], BIG[2:])
    finish_scatter(pending, out[0], BIG[:2])
    for n in BIG:
        res = big_out[n]
        if n in TRANSPOSED:
            res = tuple(r.transpose(0, 2, 1) for r in res)
        grads[n], deltas[n], new_m[n], new_v[n] = res

    return (loss, grad_x[None], *[grads[n] for n in WEIGHTS], *[deltas[n] for n in WEIGHTS],
            *[new_m[n] for n in WEIGHTS], *[new_v[n] for n in WEIGHTS])
```

```python
import functools
import math

import jax
import jax.numpy as jnp
from jax import lax
from jax.experimental import pallas as pl
from jax.experimental.pallas import tpu as pltpu

F32 = jnp.float32
BF16 = jnp.bfloat16

D_MODEL = 1024
SEQ = 2048
DEPTH = 4
N_DEV = 8
W_GRP = 256
CHUNK = 128
SGU_HEADS = 4
SGU_HEAD_DIM = 64
POOL_WINDOWS = (2, 4, 8, 16)
POOL_GROUP_DIM = 64
S5_GROUPS = 16
S5_GROUP_CH = 16
S5_STATE = 64
S5_LANES = S5_GROUPS * S5_STATE
S5_BLOCK = 256
P_IN = 1792
D_FF = 2816
FF_PIECE = 2 * D_FF // N_DEV
N_FF_CHUNK = D_FF // FF_PIECE
N_ADA = 9
EPS = 1e-6
ADAM_LR = 0.001
ADAM_B1 = 0.9
ADAM_B2 = 0.999
ADAM_EPS = 1e-08
ADAM_WD = 0.01
ADAM_STEP = 10

SUBLANES = 8
LANES = 128
VMEM_LIMIT = 56 * 1024 * 1024
TOKEN_TILE = 512
ROW_SUBTILE = 256
HIGHEST = lax.Precision.HIGHEST
MESH_AXES = ("x", "y", "c")

_GELU_C = math.sqrt(2.0 / math.pi)
_GELU_A = 0.044715


def _params(*sem):
    return pltpu.CompilerParams(dimension_semantics=tuple(sem) if sem else None, vmem_limit_bytes=VMEM_LIMIT)


def _dot(a, b):
    return jnp.dot(a, b, preferred_element_type=F32)


def _dot_nt(a, b):
    return lax.dot_general(a, b, (((1,), (1,)), ((), ())), preferred_element_type=F32)


def _dot_tn(a, b):
    return lax.dot_general(a, b, (((0,), (0,)), ((), ())), preferred_element_type=F32)


def _dot_hi(a, b):
    return jnp.dot(a, b, preferred_element_type=F32, precision=HIGHEST)


def _sigmoid(x):
    return 1.0 / (1.0 + jnp.exp(-x))


def _gelu(x):
    return 0.5 * x * (1.0 + jnp.tanh(_GELU_C * (x + _GELU_A * x * x * x)))


def _gelu_grad(x):
    t = jnp.tanh(_GELU_C * (x + _GELU_A * x * x * x))
    return 0.5 * (1.0 + t) + 0.5 * x * (1.0 - t * t) * (_GELU_C * (1.0 + 3.0 * _GELU_A * x * x))


def _rms(x):
    r = lax.rsqrt(jnp.mean(x * x, axis=-1, keepdims=True) + EPS)
    return x * r, r


def _rms_bwd(xhat, r, dxhat):
    return r * (dxhat - xhat * jnp.mean(dxhat * xhat, axis=-1, keepdims=True))


def _sum0(x):
    return jnp.sum(x, axis=0, keepdims=True)


def _me():
    return 4 * lax.axis_index("x") + 2 * lax.axis_index("y") + lax.axis_index("c")


def _exchange(srcs, scatter, name):
    n = len(srcs)
    out_shapes = []
    for s in srcs:
        piece = s.shape[1:] if scatter else s.shape
        out_shapes.append(jax.ShapeDtypeStruct((N_DEV,) + tuple(piece), s.dtype))

    def body(*refs):
        ins, outs = refs[:n], refs[n:2 * n]
        send_sems, recv_sems, local_sems = refs[2 * n:]
        x, y, c = lax.axis_index("x"), lax.axis_index("y"), lax.axis_index("c")
        me = 4 * x + 2 * y + c

        def src_of(i, dev):
            return ins[i].at[dev] if scatter else ins[i]

        local = [pltpu.make_async_copy(src_of(i, me), outs[i].at[me], local_sems.at[i]) for i in range(n)]
        for cp in local:
            cp.start()
        sends, recvs = [], []
        for k in range(1, N_DEV):
            px = 1 - x if (k >> 2) & 1 else x
            py = 1 - y if (k >> 1) & 1 else y
            pc = 1 - c if k & 1 else c
            peer = 4 * px + 2 * py + pc
            for i in range(n):
                sends.append(pltpu.make_async_remote_copy(
                    src_ref=src_of(i, peer), dst_ref=outs[i].at[me],
                    send_sem=send_sems.at[k - 1, i], recv_sem=recv_sems.at[k - 1, i],
                    device_id=(px, py, pc), device_id_type=pl.DeviceIdType.MESH))
                recvs.append(pltpu.make_async_remote_copy(
                    src_ref=src_of(i, peer), dst_ref=outs[i].at[peer],
                    send_sem=send_sems.at[k - 1, i], recv_sem=recv_sems.at[k - 1, i],
                    device_id=(px, py, pc), device_id_type=pl.DeviceIdType.MESH))
        for cp in sends:
            cp.start()
        for cp in recvs:
            cp.wait_recv()
        for cp in sends:
            cp.wait_send()
        for cp in local:
            cp.wait()

    hbm = pl.BlockSpec(memory_space=pltpu.HBM)
    return pl.pallas_call(
        body, name=name, out_shape=out_shapes,
        in_specs=[hbm] * n, out_specs=[hbm] * n,
        scratch_shapes=[pltpu.SemaphoreType.DMA((N_DEV - 1, n)), pltpu.SemaphoreType.DMA((N_DEV - 1, n)),
                        pltpu.SemaphoreType.DMA((n,))],
    )(*srcs)


ALL_PEERS = tuple(range(1, N_DEV))
OTHER_CHIPS = (2, 4, 6)


def _peers(which):
    x, y, c = lax.axis_index("x"), lax.axis_index("y"), lax.axis_index("c")
    out = []
    for k in which:
        px = 1 - x if (k >> 2) & 1 else x
        py = 1 - y if (k >> 1) & 1 else y
        pc = 1 - c if k & 1 else c
        out.append(((px, py, pc), 4 * px + 2 * py + pc))
    return out


SIBLING_SWAP = "sibling"
N_CHIPS = 4


def _swap_copies(ins, lands, send_sems, recv_sems, with_recvs):
    x, y, c = lax.axis_index("x"), lax.axis_index("y"), lax.axis_index("c")
    sends, recvs = [], []
    for q in range(N_CHIPS):
        for i in range(len(ins)):
            sems = dict(send_sem=send_sems.at[q * len(ins) + i], recv_sem=recv_sems.at[q * len(ins) + i],
                        device_id=(x, y, 1 - c), device_id_type=pl.DeviceIdType.MESH)
            theirs, mine = 2 * q + 1 - c, 2 * q + c
            sends.append(pltpu.make_async_remote_copy(src_ref=ins[i].at[theirs], dst_ref=lands[i].at[theirs], **sems))
            if with_recvs:
                recvs.append(pltpu.make_async_remote_copy(src_ref=ins[i].at[theirs], dst_ref=lands[i].at[mine], **sems))
    return sends, recvs


def _split_copies(ins, lands, send_sems, recv_sems, scatter, with_recvs, which):
    if which == SIBLING_SWAP:
        return _swap_copies(ins, lands, send_sems, recv_sems, with_recvs)
    me = _me()
    sends, recvs = [], []
    for j, (dev, peer) in enumerate(_peers(which)):
        for i in range(len(ins)):
            src = ins[i].at[peer] if scatter else ins[i]
            slot = j * len(ins) + i
            sems = dict(send_sem=send_sems.at[slot], recv_sem=recv_sems.at[slot],
                        device_id=dev, device_id_type=pl.DeviceIdType.MESH)
            sends.append(pltpu.make_async_remote_copy(src_ref=src, dst_ref=lands[i].at[me], **sems))
            if with_recvs:
                recvs.append(pltpu.make_async_remote_copy(src_ref=src, dst_ref=lands[i].at[peer], **sems))
    return sends, recvs


_HBM = pl.BlockSpec(memory_space=pltpu.HBM)
_SEM = pl.BlockSpec(memory_space=pltpu.SEMAPHORE)
_EFFECT = pltpu.SideEffectType.DATAFLOW_SIDE_EFFECTING


def _place_own(srcs, scatter, name):
    n = len(srcs)
    halves = 2
    out_shapes, in_specs, out_specs = [], [], []
    for s in srcs:
        r, c = s.shape[-2:]
        out_shapes.append(jax.ShapeDtypeStruct((N_DEV, r, c), s.dtype))
        if scatter:
            in_specs.append(pl.BlockSpec((1, r // halves, c), lambda i, me: (me[0], i, 0)))
        else:
            in_specs.append(pl.BlockSpec((r // halves, c), lambda i, me: (i, 0)))
        out_specs.append(pl.BlockSpec((1, r // halves, c), lambda i, me: (me[0], i, 0)))

    def body(me_ref, *refs):
        for i in range(n):
            refs[n + i][0] = refs[i][0] if scatter else refs[i][...]

    return pl.pallas_call(
        body, name=name, out_shape=out_shapes,
        grid_spec=pltpu.PrefetchScalarGridSpec(num_scalar_prefetch=1, grid=(halves,), in_specs=in_specs,
                                               out_specs=out_specs),
        compiler_params=_params("arbitrary"),
    )(_me().reshape(1).astype(jnp.int32), *srcs)


def _core_index():
    return lax.axis_index("c").reshape(1).astype(jnp.int32)


def _presum(pieces, received, name):
    n = len(pieces)
    halves = 2
    specs = []
    for s in pieces:
        _, r, c = s.shape
        specs.append(pl.BlockSpec((1, r // halves, c), lambda q, i, core: (2 * q + core[0], i, 0)))

    def body(core_ref, *refs):
        for i in range(n):
            refs[2 * n + i][...] = (refs[i][...].astype(F32) + refs[n + i][...].astype(F32)).astype(BF16)

    return pl.pallas_call(
        body, name=name, out_shape=[jax.ShapeDtypeStruct(s.shape, BF16) for s in pieces],
        grid_spec=pltpu.PrefetchScalarGridSpec(num_scalar_prefetch=1, grid=(N_CHIPS, halves), in_specs=specs + specs,
                                               out_specs=specs),
        compiler_params=_params("arbitrary", "arbitrary"),
    )(_core_index(), *pieces, *received)


def _exchange_start(srcs, lands, after, scatter, name, which=ALL_PEERS):
    n = len(srcs)

    def body(*refs):
        ins, land_in = refs[:n], refs[n:2 * n]
        send_sems, recv_sems = refs[2 * n + 1], refs[2 * n + 2]
        token = refs[-1]
        sends, _ = _split_copies(ins, land_in, send_sems, recv_sems, scatter, False, which)
        for cp in sends:
            cp.start()
        token[...] = jnp.zeros_like(token)

    sem = pltpu.SemaphoreType.DMA(((N_CHIPS if which == SIBLING_SWAP else len(which)) * n,))
    out = pl.pallas_call(
        body, name=name,
        out_shape=(sem, sem, *[pltpu.HBM(s.shape, s.dtype) for s in srcs], *[pltpu.HBM(s.shape, s.dtype) for s in lands],
                   jax.ShapeDtypeStruct((SUBLANES, LANES), F32)),
        in_specs=[_HBM] * (2 * n) + [pl.BlockSpec(memory_space=pl.ANY)],
        out_specs=(_SEM, _SEM, *[_HBM] * (2 * n), pl.BlockSpec(memory_space=pltpu.VMEM)),
        input_output_aliases={i: 2 + i for i in range(2 * n)},
        compiler_params=pltpu.CompilerParams(has_side_effects=_EFFECT),
    )(*srcs, *lands, after)
    return (out[0], out[1], out[2:2 + n], out[2 + n:2 + 2 * n]), out[-1]


def _exchange_wait(handle, after, scatter, name, which=ALL_PEERS, with_srcs=False):
    send_sems, recv_sems, srcs, lands = handle
    n = len(srcs)
    after = list(after) if isinstance(after, (list, tuple)) else [after]

    def body(*refs):
        ins, land_in = refs[:n], refs[n:2 * n]
        sends, recvs = _split_copies(ins, land_in, refs[2 * n], refs[2 * n + 1], scatter, True, which)
        for cp in sends:
            cp.wait_send()
        for cp in recvs:
            cp.wait_recv()

    out = pl.pallas_call(
        body, name=name,
        out_shape=(*[pltpu.HBM(s.shape, s.dtype) for s in srcs], *[pltpu.HBM(s.shape, s.dtype) for s in lands]),
        in_specs=[_HBM] * (2 * n) + [_SEM, _SEM] + [pl.BlockSpec(memory_space=pl.ANY)] * len(after),
        out_specs=tuple([_HBM] * (2 * n)),
        input_output_aliases={i: i for i in range(2 * n)},
        compiler_params=pltpu.CompilerParams(has_side_effects=_EFFECT),
    )(*srcs, *lands, send_sems, recv_sems, *after)
    return (out[:n], out[n:]) if with_srcs else out[n:]


def _gather_sibling(lands, name):
    n = len(lands)
    chips = ((0, 0), (0, 1), (1, 0), (1, 1))

    def body(*refs):
        ins, outs = refs[:n], refs[n:2 * n]
        send_sems, recv_sems = refs[2 * n], refs[2 * n + 1]
        x, y, c = lax.axis_index("x"), lax.axis_index("y"), lax.axis_index("c")
        sends, recvs = [], []
        for j, (bx, by) in enumerate(chips):
            chip = 4 * (1 - x if bx else x) + 2 * (1 - y if by else y)
            for i in range(n):
                sems = dict(send_sem=send_sems.at[j * n + i], recv_sem=recv_sems.at[j * n + i],
                            device_id=(x, y, 1 - c), device_id_type=pl.DeviceIdType.MESH)
                sends.append(pltpu.make_async_remote_copy(src_ref=ins[i].at[chip + c], dst_ref=outs[i].at[chip + c], **sems))
                recvs.append(pltpu.make_async_remote_copy(src_ref=ins[i].at[chip + c], dst_ref=outs[i].at[chip + 1 - c],
                                                          **sems))
        for cp in sends:
            cp.start()
        for cp in recvs:
            cp.wait_recv()
        for cp in sends:
            cp.wait_send()

    return pl.pallas_call(
        body, name=name, out_shape=[jax.ShapeDtypeStruct(a.shape, a.dtype) for a in lands],
        in_specs=[_HBM] * n, out_specs=[_HBM] * n,
        scratch_shapes=[pltpu.SemaphoreType.DMA((len(chips) * n,)), pltpu.SemaphoreType.DMA((len(chips) * n,))],
        input_output_aliases={i: i for i in range(n)},
    )(*lands)


def _cond_fwd(c_all, ada_w, ada_b_mine):
    ncol = ada_w.shape[-1]

    def body(c_ref, w_ref, b_ref, o_ref):
        c = c_ref[...]
        ca = (c * _sigmoid(c)).astype(BF16)
        o_ref[0] = _dot(ca, w_ref[0].astype(BF16)) + b_ref[0]

    return pl.pallas_call(
        body, name="cond_fwd", grid=(DEPTH,),
        out_shape=jax.ShapeDtypeStruct((DEPTH, N_DEV, ncol), F32),
        in_specs=[pl.BlockSpec((N_DEV, D_MODEL), lambda l: (0, 0)),
                  pl.BlockSpec((1, D_MODEL, ncol), lambda l: (l, 0, 0)),
                  pl.BlockSpec((1, 1, ncol), lambda l: (l, 0, 0))],
        out_specs=pl.BlockSpec((1, N_DEV, ncol), lambda l: (l, 0, 0)),
        compiler_params=_params("arbitrary"),
    )(c_all, ada_w, ada_b_mine)


def _cond_bwd(c_all_t, dcond_mine, dcond_all):
    ncol = dcond_mine.shape[-1]
    nall = dcond_all.shape[-1]

    def body(ct_ref, d_ref, da_ref, gw_ref, gb_ref):
        ct = ct_ref[...]
        ct = ct * _sigmoid(ct)
        d = d_ref[0]
        acc = ct[:, 0:1] * d[0:1, :]
        for b in range(1, N_DEV):
            acc = acc + ct[:, b:b + 1] * d[b:b + 1, :]
        gw_ref[0] = acc
        gb_ref[0] = _sum0(da_ref[0])

    return pl.pallas_call(
        body, name="cond_bwd", grid=(DEPTH,),
        out_shape=(jax.ShapeDtypeStruct((DEPTH, D_MODEL, ncol), F32), jax.ShapeDtypeStruct((DEPTH, 1, nall), F32)),
        in_specs=[pl.BlockSpec((D_MODEL, N_DEV), lambda l: (0, 0)),
                  pl.BlockSpec((1, N_DEV, ncol), lambda l: (l, 0, 0)),
                  pl.BlockSpec((1, N_DEV, nall), lambda l: (l, 0, 0))],
        out_specs=(pl.BlockSpec((1, D_MODEL, ncol), lambda l: (l, 0, 0)),
                   pl.BlockSpec((1, 1, nall), lambda l: (l, 0, 0))),
        compiler_params=_params("arbitrary"),
    )(c_all_t, dcond_mine, dcond_all)


def _modnorm(x, g, shift, scale):
    xhat, r = _rms(x)
    return (xhat * g) * (1.0 + scale) + shift, xhat, r


def _modnorm_bwd(xhat, r, g, scale, dh):
    n = xhat * g
    dn = dh * (1.0 + scale)
    dx = _rms_bwd(xhat, r, dn * g)
    return dx, _sum0(dh), _sum0(dh * n), _sum0(dn * xhat)


def _row_spec(rows):
    return pl.BlockSpec((rows, D_MODEL), lambda *_: (0, 0))


def _ffn_fwd(x, cond3, g, w_in_g, w_out_g):
    tm = TOKEN_TILE
    last = N_FF_CHUNK - 1

    def body(x_ref, cond_ref, g_ref, wa_ref, wb_ref, wo_ref, xo_ref, f_ref, a_ref, b_ref, h_ref, h_scr, acc_scr):
        j = pl.program_id(1)

        @pl.when(j == 0)
        def _():
            h, _, _ = _modnorm(x_ref[...], g_ref[...], cond_ref[0:1, :], cond_ref[1:2, :])
            hb = h.astype(BF16)
            h_scr[...] = hb
            h_ref[...] = hb
            acc_scr[...] = jnp.zeros_like(acc_scr)

        wa, wb, wo = wa_ref[0], wb_ref[0], wo_ref[0]
        for r in range(tm // ROW_SUBTILE):
            rows = slice(r * ROW_SUBTILE, (r + 1) * ROW_SUBTILE)
            h = h_scr[rows, :]
            a = _dot_nt(h, wa)
            b = _dot_nt(h, wb)
            a_ref[0, rows, :] = a.astype(BF16)
            b_ref[0, rows, :] = b.astype(BF16)
            act = (a * _sigmoid(a)) * b
            acc_scr[rows, :] += _dot(act.astype(BF16), wo)

        @pl.when(j == last)
        def _():
            f = acc_scr[...]
            f_ref[...] = f
            xo_ref[...] = x_ref[...] + (0.5 * cond_ref[2:3, :]) * f

    tok = pl.BlockSpec((tm, D_MODEL), lambda i, j: (i, 0))
    chunk = pl.BlockSpec((1, tm, FF_PIECE), lambda i, j: (j, i, 0))
    chunk_shape = jax.ShapeDtypeStruct((N_FF_CHUNK, SEQ, FF_PIECE), BF16)
    return pl.pallas_call(
        body, name="ffn_fwd", grid=(SEQ // tm, N_FF_CHUNK),
        out_shape=(jax.ShapeDtypeStruct((SEQ, D_MODEL), F32), jax.ShapeDtypeStruct((SEQ, D_MODEL), F32),
                   chunk_shape, chunk_shape, jax.ShapeDtypeStruct((SEQ, D_MODEL), BF16)),
        in_specs=[tok, _row_spec(3), _row_spec(1),
                  pl.BlockSpec((1, FF_PIECE, D_MODEL), lambda i, j: (j, 0, 0)),
                  pl.BlockSpec((1, FF_PIECE, D_MODEL), lambda i, j: (j + N_FF_CHUNK, 0, 0)),
                  pl.BlockSpec((1, FF_PIECE, D_MODEL), lambda i, j: (j, 0, 0))],
        out_specs=(tok, tok, chunk, chunk, tok),
        scratch_shapes=[pltpu.VMEM((tm, D_MODEL), BF16), pltpu.VMEM((tm, D_MODEL), F32)],
        compiler_params=_params("arbitrary", "arbitrary"),
    )(x, cond3, g, w_in_g, w_in_g, w_out_g)


def _ffn_bwd(dy, x, f, a_sv, b_sv, cond3, g, w_in_g, w_out_g):
    tm = TOKEN_TILE
    last = N_FF_CHUNK - 1

    def body(dy_ref, x_ref, f_ref, a_ref, b_ref, cond_ref, g_ref, wa_ref, wb_ref, wo_ref,
             dx_ref, da_ref, db_ref, act_ref, do_ref, part_ref, do_scr, dh_scr):
        i, j = pl.program_id(0), pl.program_id(1)

        @pl.when(j == 0)
        def _():
            do = ((0.5 * cond_ref[2:3, :]) * dy_ref[...]).astype(BF16)
            do_scr[...] = do
            do_ref[...] = do
            dh_scr[...] = jnp.zeros_like(dh_scr)

        @pl.when((i == 0) & (j == 0))
        def _():
            part_ref[...] = jnp.zeros_like(part_ref)

        wa, wb, wo = wa_ref[0], wb_ref[0], wo_ref[0]
        for r in range(tm // ROW_SUBTILE):
            rows = slice(r * ROW_SUBTILE, (r + 1) * ROW_SUBTILE)
            do = do_scr[rows, :]
            a = a_ref[0, rows, :].astype(F32)
            b = b_ref[0, rows, :].astype(F32)
            dact = _dot_nt(do, wo)
            sig = _sigmoid(a)
            s = a * sig
            da = (dact * b * (sig * (1.0 + a * (1.0 - sig)))).astype(BF16)
            db = (dact * s).astype(BF16)
            da_ref[0, rows, :] = da
            db_ref[0, rows, :] = db
            act_ref[0, rows, :] = (s * b).astype(BF16)
            dh_scr[rows, :] += _dot(da, wa) + _dot(db, wb)

        @pl.when(j == last)
        def _():
            dyv = dy_ref[...]
            xhat, r = _rms(x_ref[...])
            dx, dshift, dscale, dg = _modnorm_bwd(xhat, r, g_ref[...], cond_ref[1:2, :], dh_scr[...])
            dx_ref[...] = dyv + dx
            part_ref[0:1, :] += dshift
            part_ref[1:2, :] += dscale
            part_ref[2:3, :] += _sum0(0.5 * dyv * f_ref[...])
            part_ref[3:4, :] += dg

    tok = pl.BlockSpec((tm, D_MODEL), lambda i, j: (i, 0))
    chunk = pl.BlockSpec((1, tm, FF_PIECE), lambda i, j: (j, i, 0))
    chunk_shape = jax.ShapeDtypeStruct((N_FF_CHUNK, SEQ, FF_PIECE), BF16)
    return pl.pallas_call(
        body, name="ffn_bwd", grid=(SEQ // tm, N_FF_CHUNK),
        out_shape=(jax.ShapeDtypeStruct((SEQ, D_MODEL), F32), chunk_shape, chunk_shape, chunk_shape,
                   jax.ShapeDtypeStruct((SEQ, D_MODEL), BF16), jax.ShapeDtypeStruct((SUBLANES, D_MODEL), F32)),
        in_specs=[tok, tok, tok, chunk, chunk, _row_spec(3), _row_spec(1),
                  pl.BlockSpec((1, FF_PIECE, D_MODEL), lambda i, j: (j, 0, 0)),
                  pl.BlockSpec((1, FF_PIECE, D_MODEL), lambda i, j: (j + N_FF_CHUNK, 0, 0)),
                  pl.BlockSpec((1, FF_PIECE, D_MODEL), lambda i, j: (j, 0, 0))],
        out_specs=(tok, chunk, chunk, chunk, tok, _row_spec(SUBLANES)),
        scratch_shapes=[pltpu.VMEM((tm, D_MODEL), BF16), pltpu.VMEM((tm, D_MODEL), F32)],
        compiler_params=_params("arbitrary", "arbitrary"),
    )(dy, x, f, a_sv, b_sv, cond3, g, w_in_g, w_in_g, w_out_g)


def _dw(lhs_a, rhs, lhs_b=None, name="dw"):
    pa, s, m = lhs_a.shape
    nn = rhs.shape[-1]
    pb = 0 if lhs_b is None else lhs_b.shape[0]
    two = lhs_b is not None

    def body(*refs):
        if two:
            a_ref, b_ref, r_ref, o_ref = refs
            p = pl.program_id(0)

            @pl.when(p < pa)
            def _():
                o_ref[0] = _dot_tn(a_ref[0], r_ref[...]).astype(BF16)

            @pl.when(p >= pa)
            def _():
                o_ref[0] = _dot_tn(b_ref[0], r_ref[...]).astype(BF16)
        else:
            a_ref, r_ref, o_ref = refs
            o_ref[0] = _dot_tn(a_ref[0], r_ref[...]).astype(BF16)

    if two:
        in_specs = [pl.BlockSpec((1, s, m), lambda p: (jnp.minimum(p, pa - 1), 0, 0)),
                    pl.BlockSpec((1, s, m), lambda p: (jnp.maximum(p - pa, 0), 0, 0))]
        args = (lhs_a, lhs_b, rhs)
    else:
        in_specs = [pl.BlockSpec((1, s, m), lambda p: (p, 0, 0))]
        args = (lhs_a, rhs)
    in_specs.append(pl.BlockSpec((s, nn), lambda p: (0, 0)))
    return pl.pallas_call(
        body, name=name, grid=(pa + pb,),
        out_shape=jax.ShapeDtypeStruct((pa + pb, m, nn), BF16),
        in_specs=in_specs, out_specs=pl.BlockSpec((1, m, nn), lambda p: (p, 0, 0)),
        compiler_params=_params("arbitrary"),
    )(*args)


def _mix_in_fwd(x, cond2, g, w):
    tm = TOKEN_TILE

    def body(x_ref, cond_ref, g_ref, w_ref, z_ref):
        h, _, _ = _modnorm(x_ref[...], g_ref[...], cond_ref[0:1, :], cond_ref[1:2, :])
        z_ref[...] = _dot_nt(h.astype(BF16), w_ref[...])

    return pl.pallas_call(
        body, name="mix_in_fwd", grid=(SEQ // tm,),
        out_shape=jax.ShapeDtypeStruct((SEQ, P_IN), F32),
        in_specs=[pl.BlockSpec((tm, D_MODEL), lambda i: (i, 0)), _row_spec(2), _row_spec(1),
                  pl.BlockSpec((P_IN, D_MODEL), lambda i: (0, 0))],
        out_specs=pl.BlockSpec((tm, P_IN), lambda i: (i, 0)),
        compiler_params=_params("arbitrary"),
    )(x, cond2, g, w)


MIX_SLABS = ((0, 2 * W_GRP), (2 * W_GRP, 3 * W_GRP), (3 * W_GRP, 6 * W_GRP), (6 * W_GRP, 7 * W_GRP))


def _mix_in_bwd(dzs, x, dy, cond2, g, w):
    tm = TOKEN_TILE

    def body(dza_ref, dzb_ref, dzc_ref, dzd_ref, x_ref, dy_ref, cond_ref, g_ref, w_ref, dx_ref, h_ref, dzo_ref, part_ref):
        i = pl.program_id(0)

        @pl.when(i == 0)
        def _():
            part_ref[...] = jnp.zeros_like(part_ref)

        h, xhat, r = _modnorm(x_ref[...], g_ref[...], cond_ref[0:1, :], cond_ref[1:2, :])
        h_ref[...] = h.astype(BF16)
        dh = None
        for (lo, hi), d_ref in zip(MIX_SLABS, (dza_ref, dzb_ref, dzc_ref, dzd_ref)):
            dzb = d_ref[...].astype(BF16)
            dzo_ref[:, lo:hi] = dzb
            t = _dot(dzb, w_ref[lo:hi, :])
            dh = t if dh is None else dh + t
        dx, dshift, dscale, dg = _modnorm_bwd(xhat, r, g_ref[...], cond_ref[1:2, :], dh)
        dx_ref[...] = dy_ref[...] + dx
        part_ref[0:1, :] += dshift
        part_ref[1:2, :] += dscale
        part_ref[2:3, :] += dg

    tok = pl.BlockSpec((tm, D_MODEL), lambda i: (i, 0))
    ztok = pl.BlockSpec((tm, P_IN), lambda i: (i, 0))
    slabs = [pl.BlockSpec((tm, hi - lo), lambda i: (i, 0)) for lo, hi in MIX_SLABS]
    return pl.pallas_call(
        body, name="mix_in_bwd", grid=(SEQ // tm,),
        out_shape=(jax.ShapeDtypeStruct((SEQ, D_MODEL), F32), jax.ShapeDtypeStruct((SEQ, D_MODEL), BF16),
                   jax.ShapeDtypeStruct((SEQ, P_IN), BF16), jax.ShapeDtypeStruct((SUBLANES, D_MODEL), F32)),
        in_specs=[*slabs, tok, tok, _row_spec(2), _row_spec(1), pl.BlockSpec((P_IN, D_MODEL), lambda i: (0, 0))],
        out_specs=(tok, tok, ztok, _row_spec(SUBLANES)),
        compiler_params=_params("arbitrary"),
    )(*dzs, x, dy, cond2, g, w)


def _group_norm(ys, g_ref):
    out = []
    for k, y in enumerate(ys):
        yhat, r = _rms(y)
        out.append((yhat, r, g_ref[:, k * W_GRP:(k + 1) * W_GRP]))
    return out


def _mix_out_fwd(x, ys, g, gate, w):
    tm = TOKEN_TILE

    def body(x_ref, ya_ref, yb_ref, yc_ref, yd_ref, g_ref, gate_ref, w_ref, xo_ref):
        acc = None
        for k, (yhat, _, gk) in enumerate(_group_norm([r[...] for r in (ya_ref, yb_ref, yc_ref, yd_ref)], g_ref)):
            t = _dot((yhat * gk).astype(BF16), w_ref[k * W_GRP:(k + 1) * W_GRP, :])
            acc = t if acc is None else acc + t
        xo_ref[...] = x_ref[...] + gate_ref[...] * acc

    tok = pl.BlockSpec((tm, D_MODEL), lambda i: (i, 0))
    ytok = pl.BlockSpec((tm, W_GRP), lambda i: (i, 0))
    return pl.pallas_call(
        body, name="mix_out_fwd", grid=(SEQ // tm,),
        out_shape=jax.ShapeDtypeStruct((SEQ, D_MODEL), F32),
        in_specs=[tok, ytok, ytok, ytok, ytok, _row_spec(1), _row_spec(1),
                  pl.BlockSpec((D_MODEL, D_MODEL), lambda i: (0, 0))],
        out_specs=tok, compiler_params=_params("arbitrary"),
    )(x, *ys, g, gate, w)


def _mix_out_bwd(dy, ys, g, gate, w):
    tm = TOKEN_TILE

    def body(dy_ref, ya_ref, yb_ref, yc_ref, yd_ref, g_ref, gate_ref, w_ref,
             da_ref, db_ref, dc_ref, dd_ref, yn_ref, dmo_ref, part_ref):
        i = pl.program_id(0)

        @pl.when(i == 0)
        def _():
            part_ref[...] = jnp.zeros_like(part_ref)

        dyv = dy_ref[...]
        dmo = (gate_ref[...] * dyv).astype(BF16)
        dmo_ref[...] = dmo
        dyn = _dot_nt(dmo, w_ref[...])
        norms = _group_norm([r[...] for r in (ya_ref, yb_ref, yc_ref, yd_ref)], g_ref)
        mo = None
        for k, ((yhat, r, gk), o_ref) in enumerate(zip(norms, (da_ref, db_ref, dc_ref, dd_ref))):
            sl = slice(k * W_GRP, (k + 1) * W_GRP)
            ynk = (yhat * gk).astype(BF16)
            yn_ref[:, sl] = ynk
            t = _dot(ynk, w_ref[sl, :])
            mo = t if mo is None else mo + t
            dk = dyn[:, sl]
            o_ref[...] = _rms_bwd(yhat, r, dk * gk)
            part_ref[1:2, sl] += _sum0(dk * yhat)
        part_ref[0:1, :] += _sum0(dyv * mo)

    tok = pl.BlockSpec((tm, D_MODEL), lambda i: (i, 0))
    ytok = pl.BlockSpec((tm, W_GRP), lambda i: (i, 0))
    ysh = jax.ShapeDtypeStruct((SEQ, W_GRP), F32)
    return pl.pallas_call(
        body, name="mix_out_bwd", grid=(SEQ // tm,),
        out_shape=(ysh, ysh, ysh, ysh, jax.ShapeDtypeStruct((SEQ, D_MODEL), BF16),
                   jax.ShapeDtypeStruct((SEQ, D_MODEL), BF16), jax.ShapeDtypeStruct((SUBLANES, D_MODEL), F32)),
        in_specs=[tok, ytok, ytok, ytok, ytok, _row_spec(1), _row_spec(1),
                  pl.BlockSpec((D_MODEL, D_MODEL), lambda i: (0, 0))],
        out_specs=(ytok, ytok, ytok, ytok, tok, tok, _row_spec(SUBLANES)),
        compiler_params=_params("arbitrary"),
    )(dy, *ys, g, gate, w)


def _shift_down(v, k, rows):
    return jnp.where(rows >= k, pltpu.roll(v, k, axis=0), 0.0)


def _shift_up(v, k, rows):
    n = v.shape[0]
    return jnp.where(rows < n - k, pltpu.roll(v, n - k, axis=0), 0.0)


def _zslab(width, index):
    return pl.BlockSpec((SEQ, width), lambda *_: (0, index))


def _full(shape):
    return pl.BlockSpec(shape, lambda *_: (0,) * len(shape))


def _head_avg():
    r = lax.broadcasted_iota(jnp.int32, (W_GRP, W_GRP), 0) // SGU_HEAD_DIM
    c = lax.broadcasted_iota(jnp.int32, (W_GRP, W_GRP), 1) // SGU_HEAD_DIM
    return jnp.where(r == c, 1.0 / SGU_HEAD_DIM, 0.0).astype(F32)


def _sgu_norm(za):
    z = _gelu(za)
    u, v = z[:, :W_GRP], z[:, W_GRP:]
    avg = _head_avg()
    vc = v - _dot_hi(v, avg)
    rstd = lax.rsqrt(_dot_hi(vc * vc, avg) + EPS)
    return u, vc * rstd, rstd


def _sgu_masked_w(w_ref):
    t = lax.broadcasted_iota(jnp.int32, (CHUNK, CHUNK), 0)
    s = lax.broadcasted_iota(jnp.int32, (CHUNK, CHUNK), 1)
    tril = t >= s
    return [jnp.where(tril, w_ref[:, h * CHUNK:(h + 1) * CHUNK], 0.0).astype(BF16) for h in range(SGU_HEADS)]


def _head_of_lane():
    return lax.broadcasted_iota(jnp.int32, (CHUNK, W_GRP), 1) // SGU_HEAD_DIM


def _sgu_fwd(z, w_cat, bias):
    def body(z_ref, w_ref, b_ref, y_ref, vn_scr, u_scr):
        u, vn, _ = _sgu_norm(z_ref[...])
        vn_scr[...] = vn.astype(BF16)
        u_scr[...] = u
        ws = _sgu_masked_w(w_ref)
        head = _head_of_lane()
        bias_v = b_ref[...]

        def chunk(n, carry):
            rows = pl.ds(pl.multiple_of(n * CHUNK, CHUNK), CHUNK)
            vb = vn_scr[rows, :]
            mixed = bias_v
            for h in range(SGU_HEADS):
                mixed = mixed + jnp.where(head == h, _dot(ws[h], vb), 0.0)
            y_ref[rows, :] = u_scr[rows, :] * mixed
            return carry

        lax.fori_loop(0, SEQ // CHUNK, chunk, 0)

    return pl.pallas_call(
        body, name="sgu_fwd", grid=(1,),
        out_shape=jax.ShapeDtypeStruct((SEQ, W_GRP), F32),
        in_specs=[_zslab(2 * W_GRP, 0), _full((CHUNK, SGU_HEADS * CHUNK)), _full((CHUNK, W_GRP))],
        out_specs=_full((SEQ, W_GRP)),
        scratch_shapes=[pltpu.VMEM((SEQ, W_GRP), BF16), pltpu.VMEM((SEQ, W_GRP), F32)],
        compiler_params=_params("arbitrary"),
    )(z, w_cat, bias)


def _sgu_bwd(z, dy, w_cat, bias):
    def body(z_ref, dy_ref, w_ref, b_ref, dz_ref, dw_ref, db_ref, vn_scr, u_scr, dvn_scr, du_scr):
        za = z_ref[...]
        u, vn, rstd = _sgu_norm(za)
        vn_scr[...] = vn.astype(BF16)
        u_scr[...] = u
        ws = _sgu_masked_w(w_ref)
        head = _head_of_lane()
        bias_v = b_ref[...]

        def chunk(n, carry):
            dws, dbias = carry
            rows = pl.ds(pl.multiple_of(n * CHUNK, CHUNK), CHUNK)
            vb = vn_scr[rows, :]
            mixed = bias_v
            for h in range(SGU_HEADS):
                mixed = mixed + jnp.where(head == h, _dot(ws[h], vb), 0.0)
            dyc = dy_ref[rows, :]
            du_scr[rows, :] = dyc * mixed
            dmixed = dyc * u_scr[rows, :]
            dvn = jnp.zeros((CHUNK, W_GRP), F32)
            new_dws = []
            for h in range(SGU_HEADS):
                dm = jnp.where(head == h, dmixed, 0.0).astype(BF16)
                new_dws.append(dws[h] + _dot_nt(dm, vb))
                dvn = dvn + _dot_tn(ws[h], dm)
            dvn_scr[rows, :] = dvn
            return tuple(new_dws), dbias + dmixed

        zero_w = tuple(jnp.zeros((CHUNK, CHUNK), F32) for _ in range(SGU_HEADS))
        dws, dbias = lax.fori_loop(0, SEQ // CHUNK, chunk, (zero_w, jnp.zeros((CHUNK, W_GRP), F32)))
        t = lax.broadcasted_iota(jnp.int32, (CHUNK, CHUNK), 0)
        s = lax.broadcasted_iota(jnp.int32, (CHUNK, CHUNK), 1)
        for h in range(SGU_HEADS):
            dw_ref[:, h * CHUNK:(h + 1) * CHUNK] = jnp.where(t >= s, dws[h], 0.0)
        avg = _head_avg()
        db_ref[...] = _dot_hi(dbias, avg) * float(SGU_HEAD_DIM)
        dvn = dvn_scr[...]
        dv = rstd * (dvn - _dot_hi(dvn, avg) - vn * _dot_hi(dvn * vn, avg))
        gg = _gelu_grad(za)
        dz_ref[:, :W_GRP] = gg[:, :W_GRP] * du_scr[...]
        dz_ref[:, W_GRP:] = gg[:, W_GRP:] * dv

    return pl.pallas_call(
        body, name="sgu_bwd", grid=(1,),
        out_shape=(jax.ShapeDtypeStruct((SEQ, 2 * W_GRP), F32), jax.ShapeDtypeStruct((CHUNK, SGU_HEADS * CHUNK), F32),
                   jax.ShapeDtypeStruct((CHUNK, W_GRP), F32)),
        in_specs=[_zslab(2 * W_GRP, 0), _full((SEQ, W_GRP)), _full((CHUNK, SGU_HEADS * CHUNK)), _full((CHUNK, W_GRP))],
        out_specs=(_full((SEQ, 2 * W_GRP)), _full((CHUNK, SGU_HEADS * CHUNK)), _full((CHUNK, W_GRP))),
        scratch_shapes=[pltpu.VMEM((SEQ, W_GRP), BF16), pltpu.VMEM((SEQ, W_GRP), F32),
                        pltpu.VMEM((SEQ, W_GRP), F32), pltpu.VMEM((SEQ, W_GRP), F32)],
        compiler_params=_params("arbitrary"),
    )(z, dy, w_cat, bias)


def _pool_window_of_lane(shape):
    grp = lax.broadcasted_iota(jnp.int32, shape, 1) // POOL_GROUP_DIM
    win = jnp.full(shape, POOL_WINDOWS[0], jnp.int32)
    for k in range(1, len(POOL_WINDOWS)):
        win = jnp.where(grp == k, POOL_WINDOWS[k], win)
    return grp, win


def _pool_select(levels, grp):
    out = levels[0]
    for k in range(1, len(levels)):
        out = jnp.where(grp == k, levels[k], out)
    return out


def _pool_p(z):
    shape = z.shape
    rows = lax.broadcasted_iota(jnp.int32, shape, 0)
    grp, win = _pool_window_of_lane(shape)
    levels, s, k = [], z, 1
    for _ in POOL_WINDOWS:
        s = s + _shift_down(s, k, rows)
        levels.append(s)
        k *= 2
    inv = 1.0 / jnp.minimum(rows + 1, win).astype(F32)
    return _pool_select(levels, grp) * inv - z, inv, rows, grp


def _pool_fwd(z, w_bd, scale):
    def body(z_ref, w_ref, s_ref, y_ref):
        p, _, _, _ = _pool_p(z_ref[...])
        y_ref[...] = _dot(p.astype(BF16), w_ref[...]) * s_ref[...]

    return pl.pallas_call(
        body, name="pool_fwd", grid=(1,),
        out_shape=jax.ShapeDtypeStruct((SEQ, W_GRP), F32),
        in_specs=[_zslab(W_GRP, 2), _full((W_GRP, W_GRP)), _full((1, W_GRP))],
        out_specs=_full((SEQ, W_GRP)), compiler_params=_params("arbitrary"),
    )(z, w_bd, scale)


def _pool_bwd(z, dy, w_bd, scale):
    def body(z_ref, dy_ref, w_ref, s_ref, dz_ref, dw_ref, ds_ref):
        p, inv, rows, grp = _pool_p(z_ref[...])
        pb = p.astype(BF16)
        dyv = dy_ref[...]
        ds_ref[...] = _sum0(dyv * _dot(pb, w_ref[...]))
        dpre = (dyv * s_ref[...]).astype(BF16)
        dw_ref[...] = _dot_tn(pb, dpre)
        dp = _dot_nt(dpre, w_ref[...])
        q = dp * inv
        levels, s, k = [], q, 1
        for _ in POOL_WINDOWS:
            s = s + _shift_up(s, k, rows)
            levels.append(s)
            k *= 2
        dz_ref[...] = _pool_select(levels, grp) - dp

    return pl.pallas_call(
        body, name="pool_bwd", grid=(1,),
        out_shape=(jax.ShapeDtypeStruct((SEQ, W_GRP), F32), jax.ShapeDtypeStruct((W_GRP, W_GRP), F32),
                   jax.ShapeDtypeStruct((1, W_GRP), F32)),
        in_specs=[_zslab(W_GRP, 2), _full((SEQ, W_GRP)), _full((W_GRP, W_GRP)), _full((1, W_GRP))],
        out_specs=(_full((SEQ, W_GRP)), _full((W_GRP, W_GRP)), _full((1, W_GRP))),
        compiler_params=_params("arbitrary"),
    )(z, dy, w_bd, scale)


def _conv_fwd(z, w):
    def body(z_ref, w_ref, y_ref):
        zc = z_ref[...]
        bg, cg, xh = zc[:, :W_GRP], zc[:, W_GRP:2 * W_GRP], zc[:, 2 * W_GRP:]
        rows = lax.broadcasted_iota(jnp.int32, (SEQ, W_GRP), 0)
        y = cg * xh
        conv = w_ref[0:1, :] * _shift_down(y, 2, rows) + w_ref[1:2, :] * _shift_down(y, 1, rows) + w_ref[2:3, :] * y
        y_ref[...] = bg * conv

    return pl.pallas_call(
        body, name="conv_fwd", grid=(1,),
        out_shape=jax.ShapeDtypeStruct((SEQ, W_GRP), F32),
        in_specs=[_zslab(3 * W_GRP, 1), _full((3, W_GRP))],
        out_specs=_full((SEQ, W_GRP)), compiler_params=_params("arbitrary"),
    )(z, w)


def _conv_bwd(z, dy, w):
    def body(z_ref, dy_ref, w_ref, dz_ref, dw_ref):
        zc = z_ref[...]
        bg, cg, xh = zc[:, :W_GRP], zc[:, W_GRP:2 * W_GRP], zc[:, 2 * W_GRP:]
        rows = lax.broadcasted_iota(jnp.int32, (SEQ, W_GRP), 0)
        y = cg * xh
        y2, y1 = _shift_down(y, 2, rows), _shift_down(y, 1, rows)
        conv = w_ref[0:1, :] * y2 + w_ref[1:2, :] * y1 + w_ref[2:3, :] * y
        dyv = dy_ref[...]
        dconv = dyv * bg
        dw_ref[...] = jnp.zeros_like(dw_ref)
        dw_ref[0:1, :] = _sum0(dconv * y2)
        dw_ref[1:2, :] = _sum0(dconv * y1)
        dw_ref[2:3, :] = _sum0(dconv * y)
        dyy = (w_ref[0:1, :] * _shift_up(dconv, 2, rows) + w_ref[1:2, :] * _shift_up(dconv, 1, rows)
               + w_ref[2:3, :] * dconv)
        dz_ref[:, :W_GRP] = dyv * conv
        dz_ref[:, W_GRP:2 * W_GRP] = dyy * xh
        dz_ref[:, 2 * W_GRP:] = dyy * cg

    return pl.pallas_call(
        body, name="conv_bwd", grid=(1,),
        out_shape=(jax.ShapeDtypeStruct((SEQ, 3 * W_GRP), F32), jax.ShapeDtypeStruct((SUBLANES, W_GRP), F32)),
        in_specs=[_zslab(3 * W_GRP, 1), _full((SEQ, W_GRP)), _full((3, W_GRP))],
        out_specs=(_full((SEQ, 3 * W_GRP)), _full((SUBLANES, W_GRP))),
        compiler_params=_params("arbitrary"),
    )(z, dy, w)


def _s5_disc(lre, lim, ldt, br, bi):
    dt = jnp.exp(ldt)
    mag = jnp.exp(lre * dt)
    ang = lim * dt
    a_re, a_im = mag * jnp.cos(ang), mag * jnp.sin(ang)
    nr, ni = a_re - 1.0, a_im
    den = lre * lre + lim * lim
    k_re = (nr * lre + ni * lim) / den
    k_im = (ni * lre - nr * lim) / den
    return a_re, a_im, k_re * br - k_im * bi, k_re * bi + k_im * br


def _s5_prep_fwd(lre, lim, ldt, br, bi):
    def body(lre_ref, lim_ref, ldt_ref, br_ref, bi_ref, ar_ref, ai_ref, bbr_ref, bbi_ref):
        ar, ai, bbr, bbi = _s5_disc(lre_ref[...], lim_ref[...], ldt_ref[...], br_ref[...], bi_ref[...])
        ar_ref[...] = ar
        ai_ref[...] = ai
        bbr_ref[...] = bbr
        bbi_ref[...] = bbi

    return pl.pallas_call(
        body, name="s5_prep_fwd",
        out_shape=(jax.ShapeDtypeStruct(lre.shape, F32), jax.ShapeDtypeStruct(lre.shape, F32),
                   jax.ShapeDtypeStruct(br.shape, F32), jax.ShapeDtypeStruct(br.shape, F32)),
        compiler_params=_params(),
    )(lre, lim, ldt, br, bi)


def _s5_prep_bwd(lre, lim, ldt, br, bi, dar, dai, dbbr, dbbi):
    def body(lre_ref, lim_ref, ldt_ref, br_ref, bi_ref, dar_ref, dai_ref, dbbr_ref, dbbi_ref,
             o_lre, o_lim, o_ldt, o_br, o_bi):
        _, pull = jax.vjp(_s5_disc, lre_ref[...], lim_ref[...], ldt_ref[...], br_ref[...], bi_ref[...])
        g = pull((dar_ref[...], dai_ref[...], dbbr_ref[...], dbbi_ref[...]))
        for o, v in zip((o_lre, o_lim, o_ldt, o_br, o_bi), g):
            o[...] = v

    return pl.pallas_call(
        body, name="s5_prep_bwd",
        out_shape=tuple(jax.ShapeDtypeStruct(a.shape, F32) for a in (lre, lim, ldt, br, bi)),
        compiler_params=_params(),
    )(lre, lim, ldt, br, bi, dar, dai, dbbr, dbbi)


def _cmul(ar, ai, br, bi):
    return ar * br - ai * bi, ar * bi + ai * br


def _s5_tile_consts(ar, ai, reverse):
    if reverse:
        ai = -ai
    shape = (SUBLANES, S5_BLOCK)
    row = lax.broadcasted_iota(jnp.int32, shape, 0)
    a1 = (jnp.broadcast_to(ar, shape), jnp.broadcast_to(ai, shape))
    a2 = _cmul(*a1, *a1)
    a4 = _cmul(*a2, *a2)
    a8 = _cmul(*a4, *a4)
    steps = []
    for s, (pr, pi) in ((1, a1), (2, a2), (4, a4)):
        keep = (row < SUBLANES - s) if reverse else (row >= s)
        steps.append((s, jnp.where(keep, pr, 0.0), jnp.where(keep, pi, 0.0)))
    e = (SUBLANES - row) if reverse else (row + 1)
    pr, pi = jnp.ones(shape, F32), jnp.zeros(shape, F32)
    for bit, (qr, qi) in ((1, a1), (2, a2), (4, a4), (8, a8)):
        nr, ni = _cmul(pr, pi, qr, qi)
        hit = (e & bit) != 0
        pr, pi = jnp.where(hit, nr, pr), jnp.where(hit, ni, pi)
    return steps, pr, pi


def _s5_tile(xr, xi, steps, reverse):
    for s, pr, pi in steps:
        sh = SUBLANES - s if reverse else s
        sr, si = pltpu.roll(xr, sh, axis=0), pltpu.roll(xi, sh, axis=0)
        xr, xi = xr + pr * sr - pi * si, xi + pr * si + pi * sr
    return xr, xi


N_TILES = SEQ // SUBLANES


def _s5_fwd(z, b_re, b_im, c_re, c_im, a_re, a_im, d, glu_w, glu_b):
    nblk = S5_LANES // S5_BLOCK

    def body(u_ref, br_ref, bi_ref, cr_ref, ci_ref, ar_ref, ai_ref, d_ref, gw_ref, gb_ref,
             y_ref, y0_ref, xr_ref, xi_ref, ub_scr, acc_scr):
        jb = pl.program_id(0)

        @pl.when(jb == 0)
        def _():
            ub_scr[...] = u_ref[...].astype(BF16)
            acc_scr[...] = jnp.zeros_like(acc_scr)

        ub = ub_scr[...]
        xr_ref[...] = _dot(ub, br_ref[...])
        xi_ref[...] = _dot(ub, bi_ref[...])
        steps, pr, pi = _s5_tile_consts(ar_ref[...], ai_ref[...], False)

        def tile(t, carry):
            cr, ci = carry
            rows = pl.ds(pl.multiple_of(t * SUBLANES, SUBLANES), SUBLANES)
            xr, xi = _s5_tile(xr_ref[rows, :], xi_ref[rows, :], steps, False)
            xr, xi = xr + pr * cr - pi * ci, xi + pr * ci + pi * cr
            xr_ref[rows, :] = xr
            xi_ref[rows, :] = xi
            return xr[SUBLANES - 1:, :], xi[SUBLANES - 1:, :]

        zero = jnp.zeros((1, S5_BLOCK), F32)
        lax.fori_loop(0, N_TILES, tile, (zero, zero), unroll=2)
        acc_scr[...] += (_dot(xr_ref[...].astype(BF16), cr_ref[...]) - _dot(xi_ref[...].astype(BF16), ci_ref[...]))

        @pl.when(jb == nblk - 1)
        def _():
            y0 = acc_scr[...] + d_ref[...] * u_ref[...]
            y0_ref[...] = y0
            y1 = _gelu(y0)
            y_ref[...] = y1 * _sigmoid(_dot(y1.astype(BF16), gw_ref[...]) + gb_ref[...])

    lane_blk = pl.BlockSpec((SEQ, S5_BLOCK), lambda j: (0, j))
    return pl.pallas_call(
        body, name="s5_fwd", grid=(nblk,),
        out_shape=(jax.ShapeDtypeStruct((SEQ, W_GRP), F32), jax.ShapeDtypeStruct((SEQ, W_GRP), F32),
                   jax.ShapeDtypeStruct((SEQ, S5_LANES), F32), jax.ShapeDtypeStruct((SEQ, S5_LANES), F32)),
        in_specs=[_zslab(W_GRP, 6),
                  pl.BlockSpec((W_GRP, S5_BLOCK), lambda j: (0, j)), pl.BlockSpec((W_GRP, S5_BLOCK), lambda j: (0, j)),
                  pl.BlockSpec((S5_BLOCK, W_GRP), lambda j: (j, 0)), pl.BlockSpec((S5_BLOCK, W_GRP), lambda j: (j, 0)),
                  pl.BlockSpec((1, S5_BLOCK), lambda j: (0, j)), pl.BlockSpec((1, S5_BLOCK), lambda j: (0, j)),
                  _full((1, W_GRP)), _full((W_GRP, W_GRP)), _full((1, W_GRP))],
        out_specs=(_full((SEQ, W_GRP)), _full((SEQ, W_GRP)), lane_blk, lane_blk),
        scratch_shapes=[pltpu.VMEM((SEQ, W_GRP), BF16), pltpu.VMEM((SEQ, W_GRP), F32)],
        compiler_params=_params("arbitrary"),
    )(z, b_re, b_im, c_re, c_im, a_re, a_im, d, glu_w, glu_b)


def _s5_bwd(z, y0, dy, xr, xi, b_re, b_im, c_re, c_im, a_re, a_im, d, glu_w, glu_b):
    nblk = S5_LANES // S5_BLOCK

    def body(u_ref, y0_ref, dy_ref, xr_ref, xi_ref, br_ref, bi_ref, cr_ref, ci_ref, ar_ref, ai_ref,
             d_ref, gw_ref, gb_ref,
             du_ref, dbr_ref, dbi_ref, dcr_ref, dci_ref, dar_ref, dai_ref, dd_ref, dgw_ref, dgb_ref,
             ub_scr, dy0_scr, du_scr, lr_scr, li_scr):
        jb = pl.program_id(0)

        @pl.when(jb == 0)
        def _():
            u = u_ref[...]
            y0v = y0_ref[...]
            y1 = _gelu(y0v)
            y1b = y1.astype(BF16)
            sg = _sigmoid(_dot(y1b, gw_ref[...]) + gb_ref[...])
            dyv = dy_ref[...]
            dpre = dyv * y1 * sg * (1.0 - sg)
            dpb = dpre.astype(BF16)
            dgw_ref[...] = _dot_tn(y1b, dpb)
            dgb_ref[...] = _sum0(dpre)
            dy1 = dyv * sg + _dot_nt(dpb, gw_ref[...])
            dy0 = dy1 * _gelu_grad(y0v)
            dd_ref[...] = _sum0(dy0 * u)
            du_scr[...] = dy0 * d_ref[...]
            dy0_scr[...] = dy0.astype(BF16)
            ub_scr[...] = u.astype(BF16)

        dy0b = dy0_scr[...]
        lr_scr[...] = _dot_nt(dy0b, cr_ref[...])
        li_scr[...] = -_dot_nt(dy0b, ci_ref[...])
        dcr_ref[...] = _dot_tn(xr_ref[...].astype(BF16), dy0b)
        dci_ref[...] = -_dot_tn(xi_ref[...].astype(BF16), dy0b)
        steps, pr, pi = _s5_tile_consts(ar_ref[...], ai_ref[...], True)
        row = lax.broadcasted_iota(jnp.int32, (SUBLANES, S5_BLOCK), 0)

        def tile(k, carry):
            cr, ci, accr, acci = carry
            t = N_TILES - 1 - k
            rows = pl.ds(pl.multiple_of(t * SUBLANES, SUBLANES), SUBLANES)
            lr, li = _s5_tile(lr_scr[rows, :], li_scr[rows, :], steps, True)
            lr, li = lr + pr * cr - pi * ci, li + pr * ci + pi * cr
            lr_scr[rows, :] = lr
            li_scr[rows, :] = li
            prev = pl.ds(pl.multiple_of(jnp.maximum(t - 1, 0) * SUBLANES, SUBLANES), SUBLANES)
            live = jnp.where(t > 0, 1.0, 0.0)
            xpr = jnp.where(row == 0, pltpu.roll(xr_ref[prev, :], 1, axis=0) * live, pltpu.roll(xr_ref[rows, :], 1, axis=0))
            xpi = jnp.where(row == 0, pltpu.roll(xi_ref[prev, :], 1, axis=0) * live, pltpu.roll(xi_ref[rows, :], 1, axis=0))
            accr = accr + lr * xpr + li * xpi
            acci = acci + li * xpr - lr * xpi
            return lr[0:1, :], li[0:1, :], accr, acci

        zero = jnp.zeros((1, S5_BLOCK), F32)
        zt = jnp.zeros((SUBLANES, S5_BLOCK), F32)
        _, _, accr, acci = lax.fori_loop(0, N_TILES, tile, (zero, zero, zt, zt), unroll=2)
        dar_ref[...] = jnp.zeros_like(dar_ref)
        dai_ref[...] = jnp.zeros_like(dai_ref)
        dar_ref[0:1, :] = _sum0(accr)
        dai_ref[0:1, :] = _sum0(acci)
        lrb, lib = lr_scr[...].astype(BF16), li_scr[...].astype(BF16)
        ub = ub_scr[...]
        dbr_ref[...] = _dot_tn(ub, lrb)
        dbi_ref[...] = _dot_tn(ub, lib)
        du_scr[...] += _dot_nt(lrb, br_ref[...]) + _dot_nt(lib, bi_ref[...])

        @pl.when(jb == nblk - 1)
        def _():
            du_ref[...] = du_scr[...]

    lane_blk = pl.BlockSpec((SEQ, S5_BLOCK), lambda j: (0, j))
    bspec = pl.BlockSpec((W_GRP, S5_BLOCK), lambda j: (0, j))
    cspec = pl.BlockSpec((S5_BLOCK, W_GRP), lambda j: (j, 0))
    aspec = pl.BlockSpec((1, S5_BLOCK), lambda j: (0, j))
    a8spec = pl.BlockSpec((SUBLANES, S5_BLOCK), lambda j: (0, j))
    sd = jax.ShapeDtypeStruct
    return pl.pallas_call(
        body, name="s5_bwd", grid=(nblk,),
        out_shape=(sd((SEQ, W_GRP), F32), sd((W_GRP, S5_LANES), F32), sd((W_GRP, S5_LANES), F32),
                   sd((S5_LANES, W_GRP), F32), sd((S5_LANES, W_GRP), F32),
                   sd((SUBLANES, S5_LANES), F32), sd((SUBLANES, S5_LANES), F32),
                   sd((1, W_GRP), F32), sd((W_GRP, W_GRP), F32), sd((1, W_GRP), F32)),
        in_specs=[_zslab(W_GRP, 6), _full((SEQ, W_GRP)), _full((SEQ, W_GRP)), lane_blk, lane_blk,
                  bspec, bspec, cspec, cspec, aspec, aspec,
                  _full((1, W_GRP)), _full((W_GRP, W_GRP)), _full((1, W_GRP))],
        out_specs=(_full((SEQ, W_GRP)), bspec, bspec, cspec, cspec, a8spec, a8spec,
                   _full((1, W_GRP)), _full((W_GRP, W_GRP)), _full((1, W_GRP))),
        scratch_shapes=[pltpu.VMEM((SEQ, W_GRP), BF16), pltpu.VMEM((SEQ, W_GRP), BF16), pltpu.VMEM((SEQ, W_GRP), F32),
                        pltpu.VMEM((SEQ, S5_BLOCK), F32), pltpu.VMEM((SEQ, S5_BLOCK), F32)],
        compiler_params=_params("arbitrary"),
    )(z, y0, dy, xr, xi, b_re, b_im, c_re, c_im, a_re, a_im, d, glu_w, glu_b)


def _head(x, g, target):
    tm = TOKEN_TILE
    n = SEQ // tm

    def body(x_ref, g_ref, t_ref, dx_ref, st_ref, acc_scr):
        i = pl.program_id(0)

        @pl.when(i == 0)
        def _():
            acc_scr[...] = jnp.zeros_like(acc_scr)

        xhat, r = _rms(x_ref[...])
        gv = g_ref[...]
        err = xhat * gv - t_ref[...]
        dyv = err * (1.0 / D_MODEL)
        dx_ref[...] = _rms_bwd(xhat, r, dyv * gv)
        acc_scr[0:1, :] += _sum0(err * err)
        acc_scr[1:2, :] += _sum0(dyv * xhat)

        @pl.when(i == n - 1)
        def _():
            st_ref[...] = acc_scr[...]
            tot = jnp.sum(acc_scr[0:1, :], axis=-1, keepdims=True) * (0.5 / D_MODEL)
            st_ref[0:1, :] = jnp.broadcast_to(tot, (1, D_MODEL))

    tok = pl.BlockSpec((tm, D_MODEL), lambda i: (i, 0))
    return pl.pallas_call(
        body, name="head", grid=(n,),
        out_shape=(jax.ShapeDtypeStruct((SEQ, D_MODEL), F32), jax.ShapeDtypeStruct((SUBLANES, D_MODEL), F32)),
        in_specs=[tok, _row_spec(1), tok], out_specs=(tok, _row_spec(SUBLANES)),
        scratch_shapes=[pltpu.VMEM((SUBLANES, D_MODEL), F32)],
        compiler_params=_params("arbitrary"),
    )(x, g, target)


def _adamw(w, gparts, m, v, name):
    r, c = w.shape
    npart = gparts.shape[0]
    tr = r
    for cand in (512, 256, 128, 64, 32, 16):
        if r % cand == 0 and r > cand:
            tr = cand
            break
    b1c = 1.0 - ADAM_B1 ** ADAM_STEP
    b2c = 1.0 - ADAM_B2 ** ADAM_STEP

    def body(w_ref, g_ref, m_ref, v_ref, go_ref, d_ref, mo_ref, vo_ref):
        g = g_ref[0].astype(F32)
        for k in range(1, npart):
            g = g + g_ref[k].astype(F32)
        mn = ADAM_B1 * m_ref[...] + (1.0 - ADAM_B1) * g
        vn = ADAM_B2 * v_ref[...] + (1.0 - ADAM_B2) * (g * g)
        m_hat = mn / b1c
        v_hat = vn / b2c
        go_ref[...] = g
        d_ref[...] = -ADAM_LR * (m_hat / (jnp.sqrt(v_hat) + ADAM_EPS) + ADAM_WD * w_ref[...])
        mo_ref[...] = mn
        vo_ref[...] = vn

    blk = pl.BlockSpec((tr, c), lambda i: (i, 0))
    sh = jax.ShapeDtypeStruct((r, c), F32)
    return pl.pallas_call(
        body, name=name, grid=(r // tr,),
        out_shape=(sh, sh, sh, sh),
        in_specs=[blk, pl.BlockSpec((npart, tr, c), lambda i: (0, i, 0)), blk, blk],
        out_specs=(blk, blk, blk, blk), compiler_params=_params("arbitrary"),
    )(w, gparts, m, v)


def _adamw_layer(l, w, gparts, m, v, prev, name):
    _, r, c = w.shape
    tr = max(t for t in range(16, 513, 16) if r % t == 0)
    b1c = 1.0 - ADAM_B1 ** ADAM_STEP
    b2c = 1.0 - ADAM_B2 ** ADAM_STEP
    nprev = 0 if prev is None else 4

    def body(core_ref, *refs):
        w_ref, m_ref, v_ref = refs[:3]
        g_refs = refs[3:3 + N_CHIPS]
        go_ref, d_ref, mo_ref, vo_ref = refs[3 + N_CHIPS + nprev:]
        g = g_refs[0][0].astype(F32)
        for g_ref in g_refs[1:]:
            g = g + g_ref[0].astype(F32)
        mn = ADAM_B1 * m_ref[0] + (1.0 - ADAM_B1) * g
        vn = ADAM_B2 * v_ref[0] + (1.0 - ADAM_B2) * (g * g)
        go_ref[0] = g
        d_ref[0] = -ADAM_LR * ((mn / b1c) / (jnp.sqrt(vn / b2c) + ADAM_EPS) + ADAM_WD * w_ref[0])
        mo_ref[0] = mn
        vo_ref[0] = vn

    blk = pl.BlockSpec((1, tr, c), lambda i, core: (l, i, 0))
    slots = [pl.BlockSpec((1, tr, c), functools.partial(lambda i, core, q: (2 * q + core[0], i, 0), q=q))
             for q in range(N_CHIPS)]
    sh = jax.ShapeDtypeStruct(w.shape, F32)
    keep = [pl.BlockSpec(memory_space=pl.ANY)] * nprev
    return pl.pallas_call(
        body, name=name, out_shape=(sh, sh, sh, sh),
        grid_spec=pltpu.PrefetchScalarGridSpec(num_scalar_prefetch=1, grid=(r // tr,),
                                               in_specs=[blk, blk, blk, *slots, *keep], out_specs=(blk, blk, blk, blk)),
        input_output_aliases={4 + N_CHIPS + k: k for k in range(nprev)},
        compiler_params=_params("arbitrary"),
    )(_core_index(), w, m, v, *([gparts] * N_CHIPS), *(prev or ()))


def _sum_parts(parts, name):
    n, r, c = parts.shape

    def body(p_ref, o_ref):
        acc = p_ref[0]
        for k in range(1, n):
            acc = acc + p_ref[k]
        o_ref[...] = acc

    return pl.pallas_call(
        body, name=name, out_shape=jax.ShapeDtypeStruct((r, c), F32), compiler_params=_params(),
    )(parts)


def _block_diag(blocks):
    g, a, b = blocks.shape
    eye = jnp.eye(g, dtype=blocks.dtype)
    return (blocks[:, :, None, :] * eye[:, None, :, None]).reshape(g * a, g * b)


def _diag_blocks(dense, g):
    a, b = dense.shape[0] // g, dense.shape[1] // g
    d4 = dense.reshape(g, a, g, b)
    eye = jnp.eye(g, dtype=dense.dtype)
    return jnp.sum(d4 * eye[:, None, :, None], axis=2)


def _pack(parts, cols):
    flat = jnp.concatenate([p.reshape(-1) for p in parts])
    unit = N_DEV * SUBLANES * cols
    total = -(-flat.shape[0] // unit) * unit
    flat = jnp.pad(flat, (0, total - flat.shape[0]))
    return flat.reshape(N_DEV, total // (N_DEV * cols), cols)


def _unpack(flat, shapes):
    out, pos = [], 0
    for s in shapes:
        n = math.prod(s)
        out.append(flat[pos:pos + n].reshape(s))
        pos += n
    return out


SMALL = ("norm1_g", "norm2_g", "sgu_w", "sgu_b", "pool_w", "pool_scale", "conv_w", "s5_lambda_re", "s5_lambda_im",
         "s5_b_re", "s5_b_im", "s5_c_re", "s5_c_im", "s5_d", "s5_log_dt", "s5_glu_w", "s5_glu_b", "mix_norm_g",
         "norm3_g", "final_norm_g")
BIG = ("ffn1_w_in", "ffn1_w_out", "w_mix_in", "w_mix_out", "ffn2_w_in", "ffn2_w_out")
TRANSPOSED = ("ffn1_w_in", "w_mix_in", "ffn2_w_in")
WEIGHTS = ("ada_w", "ada_b", "norm1_g", "ffn1_w_in", "ffn1_w_out", "norm2_g", "w_mix_in", "sgu_w", "sgu_b", "pool_w",
           "pool_scale", "conv_w", "s5_lambda_re", "s5_lambda_im", "s5_b_re", "s5_b_im", "s5_c_re", "s5_c_im", "s5_d",
           "s5_log_dt", "s5_glu_w", "s5_glu_b", "mix_norm_g", "w_mix_out", "norm3_g", "ffn2_w_in", "ffn2_w_out",
           "final_norm_g")
PACK_COLS = 1024


def kernel(x, c, ada_w, ada_b, norm1_g, ffn1_w_in, ffn1_w_out, norm2_g, w_mix_in, sgu_w, sgu_b, pool_w, pool_scale, conv_w, s5_lambda_re, s5_lambda_im, s5_b_re, s5_b_im, s5_c_re, s5_c_im, s5_d, s5_log_dt, s5_glu_w, s5_glu_b, mix_norm_g, w_mix_out, norm3_g, ffn2_w_in, ffn2_w_out, final_norm_g, loss_target, m_ada_w, m_ada_b, m_norm1_g, m_ffn1_w_in, m_ffn1_w_out, m_norm2_g, m_w_mix_in, m_sgu_w, m_sgu_b, m_pool_w, m_pool_scale, m_conv_w, m_s5_lambda_re, m_s5_lambda_im, m_s5_b_re, m_s5_b_im, m_s5_c_re, m_s5_c_im, m_s5_d, m_s5_log_dt, m_s5_glu_w, m_s5_glu_b, m_mix_norm_g, m_w_mix_out, m_norm3_g, m_ffn2_w_in, m_ffn2_w_out, m_final_norm_g, v_ada_w, v_ada_b, v_norm1_g, v_ffn1_w_in, v_ffn1_w_out, v_norm2_g, v_w_mix_in, v_sgu_w, v_sgu_b, v_pool_w, v_pool_scale, v_conv_w, v_s5_lambda_re, v_s5_lambda_im, v_s5_b_re, v_s5_b_im, v_s5_c_re, v_s5_c_im, v_s5_d, v_s5_log_dt, v_s5_glu_w, v_s5_glu_b, v_mix_norm_g, v_w_mix_out, v_norm3_g, v_ffn2_w_in, v_ffn2_w_out, v_final_norm_g):
    args = dict(locals())
    W = {n: args[n] for n in WEIGHTS}
    M = {n: args["m_" + n] for n in WEIGHTS}
    V = {n: args["v_" + n] for n in WEIGHTS}
    me = _me()
    L = DEPTH
    x0 = x[0]
    target = loss_target[0]

    conv_cols = conv_w.shape[-1]
    glu_rows = s5_glu_w.shape[1]
    c_g, conv_g, glu_g = _exchange(
        [c.reshape(SUBLANES, LANES), conv_w.reshape(L * 3, conv_cols), s5_glu_w.reshape(L * glu_rows, W_GRP)],
        False, "gather_small")
    c_all = c_g.reshape(N_DEV, D_MODEL)
    conv_full = conv_g.reshape(N_DEV, L, 3, conv_cols).transpose(1, 2, 0, 3).reshape(L, 3, W_GRP)
    glu_full = glu_g.reshape(N_DEV, L, glu_rows, W_GRP).transpose(1, 0, 2, 3).reshape(L, W_GRP, W_GRP)

    ncol = ada_w.shape[-1]
    ada_b_mine = lax.dynamic_slice_in_dim(ada_b, me * ncol, ncol, axis=1).reshape(L, 1, ncol)
    cond_part = _cond_fwd(c_all, ada_w, ada_b_mine)
    (cond_g,) = _exchange([cond_part.reshape(L * N_DEV, ncol)], False, "gather_cond")
    cond_g = cond_g.reshape(N_DEV, L, N_DEV, ncol)
    cond_mine = lax.dynamic_index_in_dim(cond_g, me, axis=2, keepdims=False)
    cond = cond_mine.transpose(1, 0, 2).reshape(L, N_ADA, D_MODEL)

    lg = L * S5_GROUPS
    lre3 = s5_lambda_re.reshape(lg, S5_STATE, 1)
    lim3 = s5_lambda_im.reshape(lg, S5_STATE, 1)
    ldt3 = s5_log_dt.reshape(lg, 1, 1)
    br3 = s5_b_re.reshape(lg, S5_STATE, S5_GROUP_CH)
    bi3 = s5_b_im.reshape(lg, S5_STATE, S5_GROUP_CH)

    def b_mat(bb3, l):
        return _block_diag(bb3.reshape(L, S5_GROUPS, S5_STATE, S5_GROUP_CH)[l].transpose(0, 2, 1)).astype(BF16)

    def c_mat(cw, l):
        return _block_diag(cw[l].transpose(0, 2, 1)).astype(BF16)

    for n in TRANSPOSED:
        W[n], M[n], V[n] = (a.transpose(0, 2, 1) for a in (W[n], M[n], V[n]))

    gathered_shape = {"ffn1_w_out": (N_FF_CHUNK, FF_PIECE, D_MODEL), "ffn2_w_out": (N_FF_CHUNK, FF_PIECE, D_MODEL),
                      "w_mix_in": (P_IN, D_MODEL), "w_mix_out": (D_MODEL, D_MODEL)}

    def gather_start(l, after, names=BIG):
        srcs = [W[n][l].astype(BF16) for n in names]
        lands = _place_own(srcs, False, "gather_weights_own")
        return _exchange_start(srcs, lands, after, False, "gather_weights_start", OTHER_CHIPS)

    def gather_finish(handle, after, names=BIG):
        g = _exchange_wait(handle, after, False, "gather_weights_wait", OTHER_CHIPS)
        out = dict(zip(names, _gather_sibling(g, "gather_weights_sibling")))
        return {n: (a.reshape(gathered_shape[n]) if n in gathered_shape else a) for n, a in out.items()}

    saved = []
    xc = x0
    first_ffn = BIG[:2]
    handle, token = gather_start(0, cond, first_ffn)
    handle_rest, token = gather_start(0, token, BIG[2:])

    zero = token[0, 0]
    a_re3, a_im3, bb_re3, bb_im3 = _s5_prep_fwd(lre3 + zero, lim3, ldt3, br3, bi3)
    a_re = a_re3.reshape(L, 1, S5_LANES)
    a_im = a_im3.reshape(L, 1, S5_LANES)
    sgu_w_t, pool_w_t, c_re_t, c_im_t = (a + zero for a in (sgu_w, pool_w, s5_c_re, s5_c_im))

    def mixer_consts(l):
        w_cat = sgu_w_t[l].transpose(1, 0, 2).reshape(CHUNK, SGU_HEADS * CHUNK)
        bias = jnp.repeat(sgu_b[l].T, SGU_HEAD_DIM, axis=1)
        return dict(
            w_cat=w_cat, bias=bias, pool_bd=_block_diag(pool_w_t[l]).astype(BF16), pool_scale=pool_scale[l][None],
            conv=conv_full[l], b_re=b_mat(bb_re3, l), b_im=b_mat(bb_im3, l), c_re=c_mat(c_re_t, l),
            c_im=c_mat(c_im_t, l), a_re=a_re[l], a_im=a_im[l], d=s5_d[l][None], glu_w=glu_full[l].astype(BF16),
            glu_b=s5_glu_b[l][None])

    mcs = [mixer_consts(l) for l in range(L)]
    small_names = SMALL + ("ada_b",)
    first_small = W[small_names[0]] + zero
    packed_w = _pack([first_small] + [W[n] for n in small_names[1:]], PACK_COLS)
    packed_m = _pack([M[n] + zero if n == small_names[0] else M[n] for n in small_names], PACK_COLS)
    packed_v = _pack([V[n] + zero if n == small_names[0] else V[n] for n in small_names], PACK_COLS)
    wl = gather_finish(handle, [x0] + [a for mc in mcs for a in mc.values()], first_ffn)
    for l in range(L):
        cl = cond[l]
        if 0 < l < L - 1:
            handle, token = gather_start(l + 1, wl["ffn1_w_in"])
            cl = cl + token[0, 0]
        mc = mcs[l]
        x_a = xc
        x_b, *ffn1_kept = _ffn_fwd(x_a, cl[0:3], norm1_g[l][None], wl["ffn1_w_in"], wl["ffn1_w_out"])
        if l == 0:
            wl = {**wl, **gather_finish(handle_rest, [x_b, packed_w, packed_m, packed_v], BIG[2:])}
            handle, token = gather_start(1, wl["w_mix_in"])
            cl = cl + token[0, 0]
        z = _mix_in_fwd(x_b, cl[3:5], norm2_g[l][None], wl["w_mix_in"])
        ya = _sgu_fwd(z, mc["w_cat"], mc["bias"])
        yb = _pool_fwd(z, mc["pool_bd"], mc["pool_scale"])
        yc = _conv_fwd(z, mc["conv"])
        yd, y0, sxr, sxi = _s5_fwd(z, mc["b_re"], mc["b_im"], mc["c_re"], mc["c_im"], mc["a_re"], mc["a_im"],
                                   mc["d"], mc["glu_w"], mc["glu_b"])
        ys = (ya, yb, yc, yd)
        x_c = _mix_out_fwd(x_b, ys, mix_norm_g[l][None], cl[5:6], wl["w_mix_out"])
        x_d, *ffn2_kept = _ffn_fwd(x_c, cl[6:9], norm3_g[l][None], wl["ffn2_w_in"], wl["ffn2_w_out"])
        saved.append(dict(wl=wl, mc=mc, x_a=x_a, x_b=x_b, x_c=x_c, ffn1=ffn1_kept, ffn2=ffn2_kept, z=z, ys=ys, y0=y0,
                          xr=sxr, xi=sxi))
        xc = x_d
        if l + 1 < L:
            wl = gather_finish(handle, x_d)

    dx, stats = _head(xc, final_norm_g[None], target)
    loss = lax.psum(stats[0, 0], MESH_AXES)

    small_grads = {n: [None] * L for n in SMALL if n != "final_norm_g"}
    small_grads["final_norm_g"] = stats[1]
    big_out = {n: None for n in BIG}
    pending = None

    def swap_begin(pieces, after):
        bufs = _place_own(pieces, True, "scatter_grads_own")
        return _exchange_start(pieces, bufs, after, True, "swap_grads_start", SIBLING_SWAP)

    def scatter_begin(swap_handle, after):
        mine, theirs = _exchange_wait(swap_handle, after, True, "swap_grads_wait", SIBLING_SWAP, with_srcs=True)
        sums = _presum(mine, theirs, "presum_grads")
        lands = _place_own(sums, True, "scatter_grads_own")
        return _exchange_start(sums, lands, sums[0], True, "scatter_grads_start", OTHER_CHIPS)

    def finish_scatter(pend, after):
        layer, hnd = pend
        recv = _exchange_wait(hnd, after, True, "scatter_grads_wait", OTHER_CHIPS)
        for n, r in zip(BIG, recv):
            big_out[n] = _adamw_layer(layer, W[n], r, M[n], V[n], big_out[n], "adamw_" + n)
        return recv[0]

    dcond_rows = [None] * L
    d_are, d_aim, d_bbre, d_bbim = [None] * L, [None] * L, [None] * L, [None] * L
    swap = None
    for l in reversed(range(L)):
        sv = saved[l]
        wl, mc, cl = sv["wl"], sv["mc"], cond[l]
        if swap is not None:
            cl = cl + swap[1][0, 0]
        f2, a2, b2, hb = sv["ffn2"]
        dx, da, db, act, dob, part3 = _ffn_bwd(dx, sv["x_c"], f2, a2, b2, cl[6:9], norm3_g[l][None],
                                               wl["ffn2_w_in"], wl["ffn2_w_out"])
        if swap is not None:
            handle, token = scatter_begin(swap[0], dx)
            pending = (l + 1, handle)
            cl = cl + token[0, 0]
        g_ffn2_in = _dw(da, hb, db, name="dw_ffn_in")
        g_ffn2_out = _dw(act, dob, name="dw_ffn_out").reshape(N_DEV, D_FF // N_DEV, D_MODEL)
        dya, dyb, dyc, dyd, ynb, dmob, part_mo = _mix_out_bwd(dx, sv["ys"], mix_norm_g[l][None], cl[5:6],
                                                              wl["w_mix_out"])
        g_mix_out = _dw(ynb[None], dmob, name="dw_mix_out").reshape(N_DEV, D_MODEL // N_DEV, D_MODEL)
        z = sv["z"]
        dza, dw_cat, dbias = _sgu_bwd(z, dya, mc["w_cat"], mc["bias"])
        dzb, dpool_dense, dpool_scale = _pool_bwd(z, dyb, mc["pool_bd"], mc["pool_scale"])
        dzc, dconv8 = _conv_bwd(z, dyc, mc["conv"])
        (dzd, dbre_d, dbim_d, dcre_d, dcim_d, dar8, dai8, dd, dglu_w, dglu_b) = _s5_bwd(
            z, sv["y0"], dyd, sv["xr"], sv["xi"], mc["b_re"], mc["b_im"], mc["c_re"], mc["c_im"],
            mc["a_re"], mc["a_im"], mc["d"], mc["glu_w"], mc["glu_b"])
        dx, h2b, dzbf, part2 = _mix_in_bwd((dza, dzb, dzc, dzd), sv["x_b"], dx, cl[3:5], norm2_g[l][None],
                                           wl["w_mix_in"])
        g_mix_in = _dw(dzbf[None], h2b, name="dw_mix_in").reshape(N_DEV, P_IN // N_DEV, D_MODEL)
        f1, a1, b1, hb = sv["ffn1"]
        dx, da, db, act, dob, part1 = _ffn_bwd(dx, sv["x_a"], f1, a1, b1, cl[0:3], norm1_g[l][None],
                                               wl["ffn1_w_in"], wl["ffn1_w_out"])
        g_ffn1_in = _dw(da, hb, db, name="dw_ffn_in")
        g_ffn1_out = _dw(act, dob, name="dw_ffn_out").reshape(N_DEV, D_FF // N_DEV, D_MODEL)
        last = finish_scatter(pending, g_ffn1_in) if pending is not None else g_ffn1_in
        swap = swap_begin([g_ffn1_in, g_ffn1_out, g_mix_in, g_mix_out, g_ffn2_in, g_ffn2_out], last)
        dcond_rows[l] = jnp.concatenate([part1[0:3], part2[0:2], part_mo[0:1], part3[0:3]], axis=0)
        sg = small_grads
        sg["norm1_g"][l] = part1[3]
        sg["norm2_g"][l] = part2[2]
        sg["norm3_g"][l] = part3[3]
        sg["mix_norm_g"][l] = part_mo[1]
        sg["sgu_w"][l] = dw_cat.reshape(CHUNK, SGU_HEADS, CHUNK).transpose(1, 0, 2)
        sg["sgu_b"][l] = dbias[:, ::SGU_HEAD_DIM].T
        sg["pool_w"][l] = _diag_blocks(dpool_dense, len(POOL_WINDOWS))
        sg["pool_scale"][l] = dpool_scale[0]
        sg["conv_w"][l] = dconv8[0:3]
        sg["s5_c_re"][l] = _diag_blocks(dcre_d, S5_GROUPS).transpose(0, 2, 1)
        sg["s5_c_im"][l] = _diag_blocks(dcim_d, S5_GROUPS).transpose(0, 2, 1)
        sg["s5_d"][l] = dd[0]
        sg["s5_glu_w"][l] = dglu_w
        sg["s5_glu_b"][l] = dglu_b[0]
        d_are[l], d_aim[l] = dar8[0], dai8[0]
        d_bbre[l] = _diag_blocks(dbre_d, S5_GROUPS).transpose(0, 2, 1)
        d_bbim[l] = _diag_blocks(dbim_d, S5_GROUPS).transpose(0, 2, 1)
    grad_x = dx

    g_lre, g_lim, g_ldt, g_br, g_bi = _s5_prep_bwd(
        lre3, lim3, ldt3, br3, bi3,
        jnp.stack(d_are).reshape(lg, S5_STATE, 1), jnp.stack(d_aim).reshape(lg, S5_STATE, 1),
        jnp.stack(d_bbre).reshape(lg, S5_STATE, S5_GROUP_CH), jnp.stack(d_bbim).reshape(lg, S5_STATE, S5_GROUP_CH))
    small = {n: (jnp.stack(v) if isinstance(v, list) and v[0] is not None else v) for n, v in small_grads.items()}
    small["s5_lambda_re"] = g_lre.reshape(s5_lambda_re.shape)
    small["s5_lambda_im"] = g_lim.reshape(s5_lambda_im.shape)
    small["s5_log_dt"] = g_ldt.reshape(s5_log_dt.shape)
    small["s5_b_re"] = g_br.reshape(s5_b_re.shape)
    small["s5_b_im"] = g_bi.reshape(s5_b_im.shape)

    small_shapes = [(L, 3, W_GRP) if n == "conv_w" else (L, W_GRP, W_GRP) if n == "s5_glu_w" else W[n].shape
                    for n in SMALL]
    packed = _pack([small[n].reshape(s) for n, s in zip(SMALL, small_shapes)], PACK_COLS) + swap[1][0, 0]
    (pieces,) = _exchange([packed], True, "scatter_small")
    mine = _sum_parts(pieces, "sum_small")
    dcond = jnp.stack(dcond_rows).reshape(L * N_ADA, D_MODEL)
    summed, dcond_g = _exchange([mine, dcond], False, "gather_small_sums")
    small_sum = dict(zip(SMALL, _unpack(summed.reshape(-1), small_shapes)))
    small_sum["conv_w"] = lax.dynamic_slice_in_dim(small_sum["conv_w"], me * conv_cols, conv_cols, axis=2)
    small_sum["s5_glu_w"] = lax.dynamic_slice_in_dim(small_sum["s5_glu_w"], me * glu_rows, glu_rows, axis=1)

    handle, token = scatter_begin(swap[0], [summed, dcond_g])
    pending = (0, handle)
    dcond_all = dcond_g.reshape(N_DEV, L, N_ADA * D_MODEL).transpose(1, 0, 2)
    dcond_mine = lax.dynamic_slice_in_dim(dcond_all, me * ncol, ncol, axis=2) + token[0, 0]
    g_ada_w, g_ada_b = _cond_bwd(c_all.T, dcond_mine, dcond_all)

    grads, deltas, new_m, new_v = {}, {}, {}, {}
    out = _adamw(ada_w.reshape(L * D_MODEL, ncol), g_ada_w.reshape(1, L * D_MODEL, ncol),
                 m_ada_w.reshape(L * D_MODEL, ncol), v_ada_w.reshape(L * D_MODEL, ncol), "adamw_ada_w")
    grads["ada_w"], deltas["ada_w"], new_m["ada_w"], new_v["ada_w"] = (o.reshape(ada_w.shape) for o in out)
    small_g = dict(small_sum)
    small_g["ada_b"] = g_ada_b.reshape(ada_b.shape)
    shapes = [W[n].shape for n in small_names]
    rows = packed_w.shape[0] * packed_w.shape[1]
    out = _adamw(packed_w.reshape(rows, PACK_COLS),
                 _pack([small_g[n] for n in small_names], PACK_COLS).reshape(1, rows, PACK_COLS),
                 packed_m.reshape(rows, PACK_COLS), packed_v.reshape(rows, PACK_COLS), "adamw_small")
    for store, o in zip((grads, deltas, new_m, new_v), out):
        store.update(zip(small_names, _unpack(o.reshape(-1), shapes)))
    finish_scatter(pending, out[0])
    for n in BIG:
        res = big_out[n]
        if n in TRANSPOSED:
            res = tuple(r.transpose(0, 2, 1) for r in res)
        grads[n], deltas[n], new_m[n], new_v[n] = res

    return (loss, grad_x[None], *[grads[n] for n in WEIGHTS], *[deltas[n] for n in WEIGHTS],
            *[new_m[n] for n in WEIGHTS], *[new_v[n] for n in WEIGHTS])
```

```python
import functools
import math

import jax
import jax.numpy as jnp
from jax import lax
from jax.experimental import pallas as pl
from jax.experimental.pallas import tpu as pltpu

F32 = jnp.float32
BF16 = jnp.bfloat16

D_MODEL = 1024
SEQ = 2048
DEPTH = 4
N_DEV = 8
W_GRP = 256
CHUNK = 128
SGU_HEADS = 4
SGU_HEAD_DIM = 64
POOL_WINDOWS = (2, 4, 8, 16)
POOL_GROUP_DIM = 64
S5_GROUPS = 16
S5_GROUP_CH = 16
S5_STATE = 64
S5_LANES = S5_GROUPS * S5_STATE
S5_BLOCK = 256
P_IN = 1792
D_FF = 2816
FF_PIECE = 2 * D_FF // N_DEV
N_FF_CHUNK = D_FF // FF_PIECE
N_ADA = 9
EPS = 1e-6
ADAM_LR = 0.001
ADAM_B1 = 0.9
ADAM_B2 = 0.999
ADAM_EPS = 1e-08
ADAM_WD = 0.01
ADAM_STEP = 10

SUBLANES = 8
LANES = 128
VMEM_LIMIT = 56 * 1024 * 1024
TOKEN_TILE = 512
ROW_SUBTILE = 256
HIGHEST = lax.Precision.HIGHEST
MESH_AXES = ("x", "y", "c")

_GELU_C = math.sqrt(2.0 / math.pi)
_GELU_A = 0.044715


def _params(*sem):
    return pltpu.CompilerParams(dimension_semantics=tuple(sem) if sem else None, vmem_limit_bytes=VMEM_LIMIT)


def _dot(a, b):
    return jnp.dot(a, b, preferred_element_type=F32)


def _dot_nt(a, b):
    return lax.dot_general(a, b, (((1,), (1,)), ((), ())), preferred_element_type=F32)


def _dot_tn(a, b):
    return lax.dot_general(a, b, (((0,), (0,)), ((), ())), preferred_element_type=F32)


def _dot_hi(a, b):
    return jnp.dot(a, b, preferred_element_type=F32, precision=HIGHEST)


def _sigmoid(x):
    return 1.0 / (1.0 + jnp.exp(-x))


def _gelu(x):
    return 0.5 * x * (1.0 + jnp.tanh(_GELU_C * (x + _GELU_A * x * x * x)))


def _gelu_grad(x):
    t = jnp.tanh(_GELU_C * (x + _GELU_A * x * x * x))
    return 0.5 * (1.0 + t) + 0.5 * x * (1.0 - t * t) * (_GELU_C * (1.0 + 3.0 * _GELU_A * x * x))


def _rms(x):
    r = lax.rsqrt(jnp.mean(x * x, axis=-1, keepdims=True) + EPS)
    return x * r, r


def _rms_bwd(xhat, r, dxhat):
    return r * (dxhat - xhat * jnp.mean(dxhat * xhat, axis=-1, keepdims=True))


def _sum0(x):
    return jnp.sum(x, axis=0, keepdims=True)


def _me():
    return 4 * lax.axis_index("x") + 2 * lax.axis_index("y") + lax.axis_index("c")


def _exchange(srcs, scatter, name):
    n = len(srcs)
    out_shapes = []
    for s in srcs:
        piece = s.shape[1:] if scatter else s.shape
        out_shapes.append(jax.ShapeDtypeStruct((N_DEV,) + tuple(piece), s.dtype))

    def body(*refs):
        ins, outs = refs[:n], refs[n:2 * n]
        send_sems, recv_sems, local_sems = refs[2 * n:]
        x, y, c = lax.axis_index("x"), lax.axis_index("y"), lax.axis_index("c")
        me = 4 * x + 2 * y + c

        def src_of(i, dev):
            return ins[i].at[dev] if scatter else ins[i]

        local = [pltpu.make_async_copy(src_of(i, me), outs[i].at[me], local_sems.at[i]) for i in range(n)]
        for cp in local:
            cp.start()
        sends, recvs = [], []
        for k in range(1, N_DEV):
            px = 1 - x if (k >> 2) & 1 else x
            py = 1 - y if (k >> 1) & 1 else y
            pc = 1 - c if k & 1 else c
            peer = 4 * px + 2 * py + pc
            for i in range(n):
                sends.append(pltpu.make_async_remote_copy(
                    src_ref=src_of(i, peer), dst_ref=outs[i].at[me],
                    send_sem=send_sems.at[k - 1, i], recv_sem=recv_sems.at[k - 1, i],
                    device_id=(px, py, pc), device_id_type=pl.DeviceIdType.MESH))
                recvs.append(pltpu.make_async_remote_copy(
                    src_ref=src_of(i, peer), dst_ref=outs[i].at[peer],
                    send_sem=send_sems.at[k - 1, i], recv_sem=recv_sems.at[k - 1, i],
                    device_id=(px, py, pc), device_id_type=pl.DeviceIdType.MESH))
        for cp in sends:
            cp.start()
        for cp in recvs:
            cp.wait_recv()
        for cp in sends:
            cp.wait_send()
        for cp in local:
            cp.wait()

    hbm = pl.BlockSpec(memory_space=pltpu.HBM)
    return pl.pallas_call(
        body, name=name, out_shape=out_shapes,
        in_specs=[hbm] * n, out_specs=[hbm] * n,
        scratch_shapes=[pltpu.SemaphoreType.DMA((N_DEV - 1, n)), pltpu.SemaphoreType.DMA((N_DEV - 1, n)),
                        pltpu.SemaphoreType.DMA((n,))],
    )(*srcs)


ALL_PEERS = tuple(range(1, N_DEV))
OTHER_CHIPS = (2, 4, 6)


def _peers(which):
    x, y, c = lax.axis_index("x"), lax.axis_index("y"), lax.axis_index("c")
    out = []
    for k in which:
        px = 1 - x if (k >> 2) & 1 else x
        py = 1 - y if (k >> 1) & 1 else y
        pc = 1 - c if k & 1 else c
        out.append(((px, py, pc), 4 * px + 2 * py + pc))
    return out


SIBLING_SWAP = "sibling"
N_CHIPS = 4


def _swap_copies(ins, lands, send_sems, recv_sems, with_recvs):
    x, y, c = lax.axis_index("x"), lax.axis_index("y"), lax.axis_index("c")
    sends, recvs = [], []
    for q in range(N_CHIPS):
        for i in range(len(ins)):
            sems = dict(send_sem=send_sems.at[q * len(ins) + i], recv_sem=recv_sems.at[q * len(ins) + i],
                        device_id=(x, y, 1 - c), device_id_type=pl.DeviceIdType.MESH)
            theirs, mine = 2 * q + 1 - c, 2 * q + c
            sends.append(pltpu.make_async_remote_copy(src_ref=ins[i].at[theirs], dst_ref=lands[i].at[theirs], **sems))
            if with_recvs:
                recvs.append(pltpu.make_async_remote_copy(src_ref=ins[i].at[theirs], dst_ref=lands[i].at[mine], **sems))
    return sends, recvs


def _split_copies(ins, lands, send_sems, recv_sems, scatter, with_recvs, which):
    if which == SIBLING_SWAP:
        return _swap_copies(ins, lands, send_sems, recv_sems, with_recvs)
    me = _me()
    sends, recvs = [], []
    for j, (dev, peer) in enumerate(_peers(which)):
        for i in range(len(ins)):
            src = ins[i].at[peer] if scatter else ins[i]
            slot = j * len(ins) + i
            sems = dict(send_sem=send_sems.at[slot], recv_sem=recv_sems.at[slot],
                        device_id=dev, device_id_type=pl.DeviceIdType.MESH)
            sends.append(pltpu.make_async_remote_copy(src_ref=src, dst_ref=lands[i].at[me], **sems))
            if with_recvs:
                recvs.append(pltpu.make_async_remote_copy(src_ref=src, dst_ref=lands[i].at[peer], **sems))
    return sends, recvs


_HBM = pl.BlockSpec(memory_space=pltpu.HBM)
_SEM = pl.BlockSpec(memory_space=pltpu.SEMAPHORE)
_EFFECT = pltpu.SideEffectType.DATAFLOW_SIDE_EFFECTING


def _place_own(srcs, scatter, name):
    n = len(srcs)
    halves = 2
    out_shapes, in_specs, out_specs = [], [], []
    for s in srcs:
        r, c = s.shape[-2:]
        out_shapes.append(jax.ShapeDtypeStruct((N_DEV, r, c), s.dtype))
        if scatter:
            in_specs.append(pl.BlockSpec((1, r // halves, c), lambda i, me: (me[0], i, 0)))
        else:
            in_specs.append(pl.BlockSpec((r // halves, c), lambda i, me: (i, 0)))
        out_specs.append(pl.BlockSpec((1, r // halves, c), lambda i, me: (me[0], i, 0)))

    def body(me_ref, *refs):
        for i in range(n):
            refs[n + i][0] = refs[i][0] if scatter else refs[i][...]

    return pl.pallas_call(
        body, name=name, out_shape=out_shapes,
        grid_spec=pltpu.PrefetchScalarGridSpec(num_scalar_prefetch=1, grid=(halves,), in_specs=in_specs,
                                               out_specs=out_specs),
        compiler_params=_params("arbitrary"),
    )(_me().reshape(1).astype(jnp.int32), *srcs)


def _core_index():
    return lax.axis_index("c").reshape(1).astype(jnp.int32)


def _presum(pieces, received, name):
    n = len(pieces)
    halves = 2
    specs = []
    for s in pieces:
        _, r, c = s.shape
        specs.append(pl.BlockSpec((1, r // halves, c), lambda q, i, core: (2 * q + core[0], i, 0)))

    def body(core_ref, *refs):
        for i in range(n):
            refs[2 * n + i][...] = (refs[i][...].astype(F32) + refs[n + i][...].astype(F32)).astype(BF16)

    return pl.pallas_call(
        body, name=name, out_shape=[jax.ShapeDtypeStruct(s.shape, BF16) for s in pieces],
        grid_spec=pltpu.PrefetchScalarGridSpec(num_scalar_prefetch=1, grid=(N_CHIPS, halves), in_specs=specs + specs,
                                               out_specs=specs),
        compiler_params=_params("arbitrary", "arbitrary"),
    )(_core_index(), *pieces, *received)


def _exchange_start(srcs, lands, after, scatter, name, which=ALL_PEERS):
    n = len(srcs)

    def body(*refs):
        ins, land_in = refs[:n], refs[n:2 * n]
        send_sems, recv_sems = refs[2 * n + 1], refs[2 * n + 2]
        token = refs[-1]
        sends, _ = _split_copies(ins, land_in, send_sems, recv_sems, scatter, False, which)
        for cp in sends:
            cp.start()
        token[...] = jnp.zeros_like(token)

    sem = pltpu.SemaphoreType.DMA(((N_CHIPS if which == SIBLING_SWAP else len(which)) * n,))
    out = pl.pallas_call(
        body, name=name,
        out_shape=(sem, sem, *[pltpu.HBM(s.shape, s.dtype) for s in srcs], *[pltpu.HBM(s.shape, s.dtype) for s in lands],
                   jax.ShapeDtypeStruct((SUBLANES, LANES), F32)),
        in_specs=[_HBM] * (2 * n) + [pl.BlockSpec(memory_space=pl.ANY)],
        out_specs=(_SEM, _SEM, *[_HBM] * (2 * n), pl.BlockSpec(memory_space=pltpu.VMEM)),
        input_output_aliases={i: 2 + i for i in range(2 * n)},
        compiler_params=pltpu.CompilerParams(has_side_effects=_EFFECT),
    )(*srcs, *lands, after)
    return (out[0], out[1], out[2:2 + n], out[2 + n:2 + 2 * n]), out[-1]


def _exchange_wait(handle, after, scatter, name, which=ALL_PEERS, with_srcs=False):
    send_sems, recv_sems, srcs, lands = handle
    n = len(srcs)
    after = list(after) if isinstance(after, (list, tuple)) else [after]

    def body(*refs):
        ins, land_in = refs[:n], refs[n:2 * n]
        sends, recvs = _split_copies(ins, land_in, refs[2 * n], refs[2 * n + 1], scatter, True, which)
        for cp in sends:
            cp.wait_send()
        for cp in recvs:
            cp.wait_recv()

    out = pl.pallas_call(
        body, name=name,
        out_shape=(*[pltpu.HBM(s.shape, s.dtype) for s in srcs], *[pltpu.HBM(s.shape, s.dtype) for s in lands]),
        in_specs=[_HBM] * (2 * n) + [_SEM, _SEM] + [pl.BlockSpec(memory_space=pl.ANY)] * len(after),
        out_specs=tuple([_HBM] * (2 * n)),
        input_output_aliases={i: i for i in range(2 * n)},
        compiler_params=pltpu.CompilerParams(has_side_effects=_EFFECT),
    )(*srcs, *lands, send_sems, recv_sems, *after)
    return (out[:n], out[n:]) if with_srcs else out[n:]


def _gather_sibling(lands, name):
    n = len(lands)
    chips = ((0, 0), (0, 1), (1, 0), (1, 1))

    def body(*refs):
        ins, outs = refs[:n], refs[n:2 * n]
        send_sems, recv_sems = refs[2 * n], refs[2 * n + 1]
        x, y, c = lax.axis_index("x"), lax.axis_index("y"), lax.axis_index("c")
        sends, recvs = [], []
        for j, (bx, by) in enumerate(chips):
            chip = 4 * (1 - x if bx else x) + 2 * (1 - y if by else y)
            for i in range(n):
                sems = dict(send_sem=send_sems.at[j * n + i], recv_sem=recv_sems.at[j * n + i],
                            device_id=(x, y, 1 - c), device_id_type=pl.DeviceIdType.MESH)
                sends.append(pltpu.make_async_remote_copy(src_ref=ins[i].at[chip + c], dst_ref=outs[i].at[chip + c], **sems))
                recvs.append(pltpu.make_async_remote_copy(src_ref=ins[i].at[chip + c], dst_ref=outs[i].at[chip + 1 - c],
                                                          **sems))
        for cp in sends:
            cp.start()
        for cp in recvs:
            cp.wait_recv()
        for cp in sends:
            cp.wait_send()

    return pl.pallas_call(
        body, name=name, out_shape=[jax.ShapeDtypeStruct(a.shape, a.dtype) for a in lands],
        in_specs=[_HBM] * n, out_specs=[_HBM] * n,
        scratch_shapes=[pltpu.SemaphoreType.DMA((len(chips) * n,)), pltpu.SemaphoreType.DMA((len(chips) * n,))],
        input_output_aliases={i: i for i in range(n)},
    )(*lands)


def _cond_fwd(c_all, ada_w, ada_b_mine):
    ncol = ada_w.shape[-1]

    def body(c_ref, w_ref, b_ref, o_ref):
        c = c_ref[...]
        ca = (c * _sigmoid(c)).astype(BF16)
        o_ref[0] = _dot(ca, w_ref[0].astype(BF16)) + b_ref[0]

    return pl.pallas_call(
        body, name="cond_fwd", grid=(DEPTH,),
        out_shape=jax.ShapeDtypeStruct((DEPTH, N_DEV, ncol), F32),
        in_specs=[pl.BlockSpec((N_DEV, D_MODEL), lambda l: (0, 0)),
                  pl.BlockSpec((1, D_MODEL, ncol), lambda l: (l, 0, 0)),
                  pl.BlockSpec((1, 1, ncol), lambda l: (l, 0, 0))],
        out_specs=pl.BlockSpec((1, N_DEV, ncol), lambda l: (l, 0, 0)),
        compiler_params=_params("arbitrary"),
    )(c_all, ada_w, ada_b_mine)


def _cond_bwd(c_all_t, dcond_mine, dcond_all):
    ncol = dcond_mine.shape[-1]
    nall = dcond_all.shape[-1]

    def body(ct_ref, d_ref, da_ref, gw_ref, gb_ref):
        ct = ct_ref[...]
        ct = ct * _sigmoid(ct)
        d = d_ref[0]
        acc = ct[:, 0:1] * d[0:1, :]
        for b in range(1, N_DEV):
            acc = acc + ct[:, b:b + 1] * d[b:b + 1, :]
        gw_ref[0] = acc
        gb_ref[0] = _sum0(da_ref[0])

    return pl.pallas_call(
        body, name="cond_bwd", grid=(DEPTH,),
        out_shape=(jax.ShapeDtypeStruct((DEPTH, D_MODEL, ncol), F32), jax.ShapeDtypeStruct((DEPTH, 1, nall), F32)),
        in_specs=[pl.BlockSpec((D_MODEL, N_DEV), lambda l: (0, 0)),
                  pl.BlockSpec((1, N_DEV, ncol), lambda l: (l, 0, 0)),
                  pl.BlockSpec((1, N_DEV, nall), lambda l: (l, 0, 0))],
        out_specs=(pl.BlockSpec((1, D_MODEL, ncol), lambda l: (l, 0, 0)),
                   pl.BlockSpec((1, 1, nall), lambda l: (l, 0, 0))),
        compiler_params=_params("arbitrary"),
    )(c_all_t, dcond_mine, dcond_all)


def _modnorm(x, g, shift, scale):
    xhat, r = _rms(x)
    return (xhat * g) * (1.0 + scale) + shift, xhat, r


def _modnorm_bwd(xhat, r, g, scale, dh):
    n = xhat * g
    dn = dh * (1.0 + scale)
    dx = _rms_bwd(xhat, r, dn * g)
    return dx, _sum0(dh), _sum0(dh * n), _sum0(dn * xhat)


def _row_spec(rows):
    return pl.BlockSpec((rows, D_MODEL), lambda *_: (0, 0))


def _ffn_fwd(x, cond3, g, w_in_g, w_out_g):
    tm = TOKEN_TILE
    last = N_FF_CHUNK - 1

    def body(x_ref, cond_ref, g_ref, wa_ref, wb_ref, wo_ref, xo_ref, f_ref, a_ref, b_ref, h_ref, h_scr, acc_scr):
        j = pl.program_id(1)

        @pl.when(j == 0)
        def _():
            h, _, _ = _modnorm(x_ref[...], g_ref[...], cond_ref[0:1, :], cond_ref[1:2, :])
            hb = h.astype(BF16)
            h_scr[...] = hb
            h_ref[...] = hb
            acc_scr[...] = jnp.zeros_like(acc_scr)

        wa, wb, wo = wa_ref[0], wb_ref[0], wo_ref[0]
        for r in range(tm // ROW_SUBTILE):
            rows = slice(r * ROW_SUBTILE, (r + 1) * ROW_SUBTILE)
            h = h_scr[rows, :]
            a = _dot_nt(h, wa)
            b = _dot_nt(h, wb)
            a_ref[0, rows, :] = a.astype(BF16)
            b_ref[0, rows, :] = b.astype(BF16)
            act = (a * _sigmoid(a)) * b
            acc_scr[rows, :] += _dot(act.astype(BF16), wo)

        @pl.when(j == last)
        def _():
            f = acc_scr[...]
            f_ref[...] = f
            xo_ref[...] = x_ref[...] + (0.5 * cond_ref[2:3, :]) * f

    tok = pl.BlockSpec((tm, D_MODEL), lambda i, j: (i, 0))
    chunk = pl.BlockSpec((1, tm, FF_PIECE), lambda i, j: (j, i, 0))
    chunk_shape = jax.ShapeDtypeStruct((N_FF_CHUNK, SEQ, FF_PIECE), BF16)
    return pl.pallas_call(
        body, name="ffn_fwd", grid=(SEQ // tm, N_FF_CHUNK),
        out_shape=(jax.ShapeDtypeStruct((SEQ, D_MODEL), F32), jax.ShapeDtypeStruct((SEQ, D_MODEL), F32),
                   chunk_shape, chunk_shape, jax.ShapeDtypeStruct((SEQ, D_MODEL), BF16)),
        in_specs=[tok, _row_spec(3), _row_spec(1),
                  pl.BlockSpec((1, FF_PIECE, D_MODEL), lambda i, j: (j, 0, 0)),
                  pl.BlockSpec((1, FF_PIECE, D_MODEL), lambda i, j: (j + N_FF_CHUNK, 0, 0)),
                  pl.BlockSpec((1, FF_PIECE, D_MODEL), lambda i, j: (j, 0, 0))],
        out_specs=(tok, tok, chunk, chunk, tok),
        scratch_shapes=[pltpu.VMEM((tm, D_MODEL), BF16), pltpu.VMEM((tm, D_MODEL), F32)],
        compiler_params=_params("arbitrary", "arbitrary"),
    )(x, cond3, g, w_in_g, w_in_g, w_out_g)


def _ffn_bwd(dy, x, f, a_sv, b_sv, cond3, g, w_in_g, w_out_g):
    tm = TOKEN_TILE
    last = N_FF_CHUNK - 1

    def body(dy_ref, x_ref, f_ref, a_ref, b_ref, cond_ref, g_ref, wa_ref, wb_ref, wo_ref,
             dx_ref, da_ref, db_ref, act_ref, do_ref, part_ref, do_scr, dh_scr):
        i, j = pl.program_id(0), pl.program_id(1)

        @pl.when(j == 0)
        def _():
            do = ((0.5 * cond_ref[2:3, :]) * dy_ref[...]).astype(BF16)
            do_scr[...] = do
            do_ref[...] = do
            dh_scr[...] = jnp.zeros_like(dh_scr)

        @pl.when((i == 0) & (j == 0))
        def _():
            part_ref[...] = jnp.zeros_like(part_ref)

        wa, wb, wo = wa_ref[0], wb_ref[0], wo_ref[0]
        for r in range(tm // ROW_SUBTILE):
            rows = slice(r * ROW_SUBTILE, (r + 1) * ROW_SUBTILE)
            do = do_scr[rows, :]
            a = a_ref[0, rows, :].astype(F32)
            b = b_ref[0, rows, :].astype(F32)
            dact = _dot_nt(do, wo)
            sig = _sigmoid(a)
            s = a * sig
            da = (dact * b * (sig * (1.0 + a * (1.0 - sig)))).astype(BF16)
            db = (dact * s).astype(BF16)
            da_ref[0, rows, :] = da
            db_ref[0, rows, :] = db
            act_ref[0, rows, :] = (s * b).astype(BF16)
            dh_scr[rows, :] += _dot(da, wa) + _dot(db, wb)

        @pl.when(j == last)
        def _():
            dyv = dy_ref[...]
            xhat, r = _rms(x_ref[...])
            dx, dshift, dscale, dg = _modnorm_bwd(xhat, r, g_ref[...], cond_ref[1:2, :], dh_scr[...])
            dx_ref[...] = dyv + dx
            part_ref[0:1, :] += dshift
            part_ref[1:2, :] += dscale
            part_ref[2:3, :] += _sum0(0.5 * dyv * f_ref[...])
            part_ref[3:4, :] += dg

    tok = pl.BlockSpec((tm, D_MODEL), lambda i, j: (i, 0))
    chunk = pl.BlockSpec((1, tm, FF_PIECE), lambda i, j: (j, i, 0))
    chunk_shape = jax.ShapeDtypeStruct((N_FF_CHUNK, SEQ, FF_PIECE), BF16)
    return pl.pallas_call(
        body, name="ffn_bwd", grid=(SEQ // tm, N_FF_CHUNK),
        out_shape=(jax.ShapeDtypeStruct((SEQ, D_MODEL), F32), chunk_shape, chunk_shape, chunk_shape,
                   jax.ShapeDtypeStruct((SEQ, D_MODEL), BF16), jax.ShapeDtypeStruct((SUBLANES, D_MODEL), F32)),
        in_specs=[tok, tok, tok, chunk, chunk, _row_spec(3), _row_spec(1),
                  pl.BlockSpec((1, FF_PIECE, D_MODEL), lambda i, j: (j, 0, 0)),
                  pl.BlockSpec((1, FF_PIECE, D_MODEL), lambda i, j: (j + N_FF_CHUNK, 0, 0)),
                  pl.BlockSpec((1, FF_PIECE, D_MODEL), lambda i, j: (j, 0, 0))],
        out_specs=(tok, chunk, chunk, chunk, tok, _row_spec(SUBLANES)),
        scratch_shapes=[pltpu.VMEM((tm, D_MODEL), BF16), pltpu.VMEM((tm, D_MODEL), F32)],
        compiler_params=_params("arbitrary", "arbitrary"),
    )(dy, x, f, a_sv, b_sv, cond3, g, w_in_g, w_in_g, w_out_g)


def _dw(lhs_a, rhs, lhs_b=None, name="dw"):
    pa, s, m = lhs_a.shape
    nn = rhs.shape[-1]
    pb = 0 if lhs_b is None else lhs_b.shape[0]
    two = lhs_b is not None

    def body(*refs):
        if two:
            a_ref, b_ref, r_ref, o_ref = refs
            p = pl.program_id(0)

            @pl.when(p < pa)
            def _():
                o_ref[0] = _dot_tn(a_ref[0], r_ref[...]).astype(BF16)

            @pl.when(p >= pa)
            def _():
                o_ref[0] = _dot_tn(b_ref[0], r_ref[...]).astype(BF16)
        else:
            a_ref, r_ref, o_ref = refs
            o_ref[0] = _dot_tn(a_ref[0], r_ref[...]).astype(BF16)

    if two:
        in_specs = [pl.BlockSpec((1, s, m), lambda p: (jnp.minimum(p, pa - 1), 0, 0)),
                    pl.BlockSpec((1, s, m), lambda p: (jnp.maximum(p - pa, 0), 0, 0))]
        args = (lhs_a, lhs_b, rhs)
    else:
        in_specs = [pl.BlockSpec((1, s, m), lambda p: (p, 0, 0))]
        args = (lhs_a, rhs)
    in_specs.append(pl.BlockSpec((s, nn), lambda p: (0, 0)))
    return pl.pallas_call(
        body, name=name, grid=(pa + pb,),
        out_shape=jax.ShapeDtypeStruct((pa + pb, m, nn), BF16),
        in_specs=in_specs, out_specs=pl.BlockSpec((1, m, nn), lambda p: (p, 0, 0)),
        compiler_params=_params("arbitrary"),
    )(*args)


def _mix_in_fwd(x, cond2, g, w):
    tm = TOKEN_TILE

    def body(x_ref, cond_ref, g_ref, w_ref, z_ref):
        h, _, _ = _modnorm(x_ref[...], g_ref[...], cond_ref[0:1, :], cond_ref[1:2, :])
        z_ref[...] = _dot_nt(h.astype(BF16), w_ref[...])

    return pl.pallas_call(
        body, name="mix_in_fwd", grid=(SEQ // tm,),
        out_shape=jax.ShapeDtypeStruct((SEQ, P_IN), F32),
        in_specs=[pl.BlockSpec((tm, D_MODEL), lambda i: (i, 0)), _row_spec(2), _row_spec(1),
                  pl.BlockSpec((P_IN, D_MODEL), lambda i: (0, 0))],
        out_specs=pl.BlockSpec((tm, P_IN), lambda i: (i, 0)),
        compiler_params=_params("arbitrary"),
    )(x, cond2, g, w)


MIX_SLABS = ((0, 2 * W_GRP), (2 * W_GRP, 3 * W_GRP), (3 * W_GRP, 6 * W_GRP), (6 * W_GRP, 7 * W_GRP))


def _mix_in_bwd(dzs, x, dy, cond2, g, w):
    tm = TOKEN_TILE

    def body(dza_ref, dzb_ref, dzc_ref, dzd_ref, x_ref, dy_ref, cond_ref, g_ref, w_ref, dx_ref, h_ref, dzo_ref, part_ref):
        i = pl.program_id(0)

        @pl.when(i == 0)
        def _():
            part_ref[...] = jnp.zeros_like(part_ref)

        h, xhat, r = _modnorm(x_ref[...], g_ref[...], cond_ref[0:1, :], cond_ref[1:2, :])
        h_ref[...] = h.astype(BF16)
        dh = None
        for (lo, hi), d_ref in zip(MIX_SLABS, (dza_ref, dzb_ref, dzc_ref, dzd_ref)):
            dzb = d_ref[...].astype(BF16)
            dzo_ref[:, lo:hi] = dzb
            t = _dot(dzb, w_ref[lo:hi, :])
            dh = t if dh is None else dh + t
        dx, dshift, dscale, dg = _modnorm_bwd(xhat, r, g_ref[...], cond_ref[1:2, :], dh)
        dx_ref[...] = dy_ref[...] + dx
        part_ref[0:1, :] += dshift
        part_ref[1:2, :] += dscale
        part_ref[2:3, :] += dg

    tok = pl.BlockSpec((tm, D_MODEL), lambda i: (i, 0))
    ztok = pl.BlockSpec((tm, P_IN), lambda i: (i, 0))
    slabs = [pl.BlockSpec((tm, hi - lo), lambda i: (i, 0)) for lo, hi in MIX_SLABS]
    return pl.pallas_call(
        body, name="mix_in_bwd", grid=(SEQ // tm,),
        out_shape=(jax.ShapeDtypeStruct((SEQ, D_MODEL), F32), jax.ShapeDtypeStruct((SEQ, D_MODEL), BF16),
                   jax.ShapeDtypeStruct((SEQ, P_IN), BF16), jax.ShapeDtypeStruct((SUBLANES, D_MODEL), F32)),
        in_specs=[*slabs, tok, tok, _row_spec(2), _row_spec(1), pl.BlockSpec((P_IN, D_MODEL), lambda i: (0, 0))],
        out_specs=(tok, tok, ztok, _row_spec(SUBLANES)),
        compiler_params=_params("arbitrary"),
    )(*dzs, x, dy, cond2, g, w)


def _group_norm(ys, g_ref):
    out = []
    for k, y in enumerate(ys):
        yhat, r = _rms(y)
        out.append((yhat, r, g_ref[:, k * W_GRP:(k + 1) * W_GRP]))
    return out


def _mix_out_fwd(x, ys, g, gate, w):
    tm = TOKEN_TILE

    def body(x_ref, ya_ref, yb_ref, yc_ref, yd_ref, g_ref, gate_ref, w_ref, xo_ref):
        acc = None
        for k, (yhat, _, gk) in enumerate(_group_norm([r[...] for r in (ya_ref, yb_ref, yc_ref, yd_ref)], g_ref)):
            t = _dot((yhat * gk).astype(BF16), w_ref[k * W_GRP:(k + 1) * W_GRP, :])
            acc = t if acc is None else acc + t
        xo_ref[...] = x_ref[...] + gate_ref[...] * acc

    tok = pl.BlockSpec((tm, D_MODEL), lambda i: (i, 0))
    ytok = pl.BlockSpec((tm, W_GRP), lambda i: (i, 0))
    return pl.pallas_call(
        body, name="mix_out_fwd", grid=(SEQ // tm,),
        out_shape=jax.ShapeDtypeStruct((SEQ, D_MODEL), F32),
        in_specs=[tok, ytok, ytok, ytok, ytok, _row_spec(1), _row_spec(1),
                  pl.BlockSpec((D_MODEL, D_MODEL), lambda i: (0, 0))],
        out_specs=tok, compiler_params=_params("arbitrary"),
    )(x, *ys, g, gate, w)


def _mix_out_bwd(dy, ys, g, gate, w):
    tm = TOKEN_TILE

    def body(dy_ref, ya_ref, yb_ref, yc_ref, yd_ref, g_ref, gate_ref, w_ref,
             da_ref, db_ref, dc_ref, dd_ref, yn_ref, dmo_ref, part_ref):
        i = pl.program_id(0)

        @pl.when(i == 0)
        def _():
            part_ref[...] = jnp.zeros_like(part_ref)

        dyv = dy_ref[...]
        dmo = (gate_ref[...] * dyv).astype(BF16)
        dmo_ref[...] = dmo
        dyn = _dot_nt(dmo, w_ref[...])
        norms = _group_norm([r[...] for r in (ya_ref, yb_ref, yc_ref, yd_ref)], g_ref)
        mo = None
        for k, ((yhat, r, gk), o_ref) in enumerate(zip(norms, (da_ref, db_ref, dc_ref, dd_ref))):
            sl = slice(k * W_GRP, (k + 1) * W_GRP)
            ynk = (yhat * gk).astype(BF16)
            yn_ref[:, sl] = ynk
            t = _dot(ynk, w_ref[sl, :])
            mo = t if mo is None else mo + t
            dk = dyn[:, sl]
            o_ref[...] = _rms_bwd(yhat, r, dk * gk)
            part_ref[1:2, sl] += _sum0(dk * yhat)
        part_ref[0:1, :] += _sum0(dyv * mo)

    tok = pl.BlockSpec((tm, D_MODEL), lambda i: (i, 0))
    ytok = pl.BlockSpec((tm, W_GRP), lambda i: (i, 0))
    ysh = jax.ShapeDtypeStruct((SEQ, W_GRP), F32)
    return pl.pallas_call(
        body, name="mix_out_bwd", grid=(SEQ // tm,),
        out_shape=(ysh, ysh, ysh, ysh, jax.ShapeDtypeStruct((SEQ, D_MODEL), BF16),
                   jax.ShapeDtypeStruct((SEQ, D_MODEL), BF16), jax.ShapeDtypeStruct((SUBLANES, D_MODEL), F32)),
        in_specs=[tok, ytok, ytok, ytok, ytok, _row_spec(1), _row_spec(1),
                  pl.BlockSpec((D_MODEL, D_MODEL), lambda i: (0, 0))],
        out_specs=(ytok, ytok, ytok, ytok, tok, tok, _row_spec(SUBLANES)),
        compiler_params=_params("arbitrary"),
    )(dy, *ys, g, gate, w)


def _shift_down(v, k, rows):
    return jnp.where(rows >= k, pltpu.roll(v, k, axis=0), 0.0)


def _shift_up(v, k, rows):
    n = v.shape[0]
    return jnp.where(rows < n - k, pltpu.roll(v, n - k, axis=0), 0.0)


def _zslab(width, index):
    return pl.BlockSpec((SEQ, width), lambda *_: (0, index))


def _full(shape):
    return pl.BlockSpec(shape, lambda *_: (0,) * len(shape))


def _head_avg():
    r = lax.broadcasted_iota(jnp.int32, (W_GRP, W_GRP), 0) // SGU_HEAD_DIM
    c = lax.broadcasted_iota(jnp.int32, (W_GRP, W_GRP), 1) // SGU_HEAD_DIM
    return jnp.where(r == c, 1.0 / SGU_HEAD_DIM, 0.0).astype(F32)


def _sgu_norm(za):
    z = _gelu(za)
    u, v = z[:, :W_GRP], z[:, W_GRP:]
    avg = _head_avg()
    vc = v - _dot_hi(v, avg)
    rstd = lax.rsqrt(_dot_hi(vc * vc, avg) + EPS)
    return u, vc * rstd, rstd


def _sgu_masked_w(w_ref):
    t = lax.broadcasted_iota(jnp.int32, (CHUNK, CHUNK), 0)
    s = lax.broadcasted_iota(jnp.int32, (CHUNK, CHUNK), 1)
    tril = t >= s
    return [jnp.where(tril, w_ref[:, h * CHUNK:(h + 1) * CHUNK], 0.0).astype(BF16) for h in range(SGU_HEADS)]


def _head_of_lane():
    return lax.broadcasted_iota(jnp.int32, (CHUNK, W_GRP), 1) // SGU_HEAD_DIM


def _sgu_fwd(z, w_cat, bias):
    def body(z_ref, w_ref, b_ref, y_ref, vn_scr, u_scr):
        u, vn, _ = _sgu_norm(z_ref[...])
        vn_scr[...] = vn.astype(BF16)
        u_scr[...] = u
        ws = _sgu_masked_w(w_ref)
        head = _head_of_lane()
        bias_v = b_ref[...]

        def chunk(n, carry):
            rows = pl.ds(pl.multiple_of(n * CHUNK, CHUNK), CHUNK)
            vb = vn_scr[rows, :]
            mixed = bias_v
            for h in range(SGU_HEADS):
                mixed = mixed + jnp.where(head == h, _dot(ws[h], vb), 0.0)
            y_ref[rows, :] = u_scr[rows, :] * mixed
            return carry

        lax.fori_loop(0, SEQ // CHUNK, chunk, 0)

    return pl.pallas_call(
        body, name="sgu_fwd", grid=(1,),
        out_shape=jax.ShapeDtypeStruct((SEQ, W_GRP), F32),
        in_specs=[_zslab(2 * W_GRP, 0), _full((CHUNK, SGU_HEADS * CHUNK)), _full((CHUNK, W_GRP))],
        out_specs=_full((SEQ, W_GRP)),
        scratch_shapes=[pltpu.VMEM((SEQ, W_GRP), BF16), pltpu.VMEM((SEQ, W_GRP), F32)],
        compiler_params=_params("arbitrary"),
    )(z, w_cat, bias)


def _sgu_bwd(z, dy, w_cat, bias):
    def body(z_ref, dy_ref, w_ref, b_ref, dz_ref, dw_ref, db_ref, vn_scr, u_scr, dvn_scr, du_scr):
        za = z_ref[...]
        u, vn, rstd = _sgu_norm(za)
        vn_scr[...] = vn.astype(BF16)
        u_scr[...] = u
        ws = _sgu_masked_w(w_ref)
        head = _head_of_lane()
        bias_v = b_ref[...]

        def chunk(n, carry):
            dws, dbias = carry
            rows = pl.ds(pl.multiple_of(n * CHUNK, CHUNK), CHUNK)
            vb = vn_scr[rows, :]
            mixed = bias_v
            for h in range(SGU_HEADS):
                mixed = mixed + jnp.where(head == h, _dot(ws[h], vb), 0.0)
            dyc = dy_ref[rows, :]
            du_scr[rows, :] = dyc * mixed
            dmixed = dyc * u_scr[rows, :]
            dvn = jnp.zeros((CHUNK, W_GRP), F32)
            new_dws = []
            for h in range(SGU_HEADS):
                dm = jnp.where(head == h, dmixed, 0.0).astype(BF16)
                new_dws.append(dws[h] + _dot_nt(dm, vb))
                dvn = dvn + _dot_tn(ws[h], dm)
            dvn_scr[rows, :] = dvn
            return tuple(new_dws), dbias + dmixed

        zero_w = tuple(jnp.zeros((CHUNK, CHUNK), F32) for _ in range(SGU_HEADS))
        dws, dbias = lax.fori_loop(0, SEQ // CHUNK, chunk, (zero_w, jnp.zeros((CHUNK, W_GRP), F32)))
        t = lax.broadcasted_iota(jnp.int32, (CHUNK, CHUNK), 0)
        s = lax.broadcasted_iota(jnp.int32, (CHUNK, CHUNK), 1)
        for h in range(SGU_HEADS):
            dw_ref[:, h * CHUNK:(h + 1) * CHUNK] = jnp.where(t >= s, dws[h], 0.0)
        avg = _head_avg()
        db_ref[...] = _dot_hi(dbias, avg) * float(SGU_HEAD_DIM)
        dvn = dvn_scr[...]
        dv = rstd * (dvn - _dot_hi(dvn, avg) - vn * _dot_hi(dvn * vn, avg))
        gg = _gelu_grad(za)
        dz_ref[:, :W_GRP] = gg[:, :W_GRP] * du_scr[...]
        dz_ref[:, W_GRP:] = gg[:, W_GRP:] * dv

    return pl.pallas_call(
        body, name="sgu_bwd", grid=(1,),
        out_shape=(jax.ShapeDtypeStruct((SEQ, 2 * W_GRP), F32), jax.ShapeDtypeStruct((CHUNK, SGU_HEADS * CHUNK), F32),
                   jax.ShapeDtypeStruct((CHUNK, W_GRP), F32)),
        in_specs=[_zslab(2 * W_GRP, 0), _full((SEQ, W_GRP)), _full((CHUNK, SGU_HEADS * CHUNK)), _full((CHUNK, W_GRP))],
        out_specs=(_full((SEQ, 2 * W_GRP)), _full((CHUNK, SGU_HEADS * CHUNK)), _full((CHUNK, W_GRP))),
        scratch_shapes=[pltpu.VMEM((SEQ, W_GRP), BF16), pltpu.VMEM((SEQ, W_GRP), F32),
                        pltpu.VMEM((SEQ, W_GRP), F32), pltpu.VMEM((SEQ, W_GRP), F32)],
        compiler_params=_params("arbitrary"),
    )(z, dy, w_cat, bias)


def _pool_window_of_lane(shape):
    grp = lax.broadcasted_iota(jnp.int32, shape, 1) // POOL_GROUP_DIM
    win = jnp.full(shape, POOL_WINDOWS[0], jnp.int32)
    for k in range(1, len(POOL_WINDOWS)):
        win = jnp.where(grp == k, POOL_WINDOWS[k], win)
    return grp, win


def _pool_select(levels, grp):
    out = levels[0]
    for k in range(1, len(levels)):
        out = jnp.where(grp == k, levels[k], out)
    return out


def _pool_p(z):
    shape = z.shape
    rows = lax.broadcasted_iota(jnp.int32, shape, 0)
    grp, win = _pool_window_of_lane(shape)
    levels, s, k = [], z, 1
    for _ in POOL_WINDOWS:
        s = s + _shift_down(s, k, rows)
        levels.append(s)
        k *= 2
    inv = 1.0 / jnp.minimum(rows + 1, win).astype(F32)
    return _pool_select(levels, grp) * inv - z, inv, rows, grp


def _pool_fwd(z, w_bd, scale):
    def body(z_ref, w_ref, s_ref, y_ref):
        p, _, _, _ = _pool_p(z_ref[...])
        y_ref[...] = _dot(p.astype(BF16), w_ref[...]) * s_ref[...]

    return pl.pallas_call(
        body, name="pool_fwd", grid=(1,),
        out_shape=jax.ShapeDtypeStruct((SEQ, W_GRP), F32),
        in_specs=[_zslab(W_GRP, 2), _full((W_GRP, W_GRP)), _full((1, W_GRP))],
        out_specs=_full((SEQ, W_GRP)), compiler_params=_params("arbitrary"),
    )(z, w_bd, scale)


def _pool_bwd(z, dy, w_bd, scale):
    def body(z_ref, dy_ref, w_ref, s_ref, dz_ref, dw_ref, ds_ref):
        p, inv, rows, grp = _pool_p(z_ref[...])
        pb = p.astype(BF16)
        dyv = dy_ref[...]
        ds_ref[...] = _sum0(dyv * _dot(pb, w_ref[...]))
        dpre = (dyv * s_ref[...]).astype(BF16)
        dw_ref[...] = _dot_tn(pb, dpre)
        dp = _dot_nt(dpre, w_ref[...])
        q = dp * inv
        levels, s, k = [], q, 1
        for _ in POOL_WINDOWS:
            s = s + _shift_up(s, k, rows)
            levels.append(s)
            k *= 2
        dz_ref[...] = _pool_select(levels, grp) - dp

    return pl.pallas_call(
        body, name="pool_bwd", grid=(1,),
        out_shape=(jax.ShapeDtypeStruct((SEQ, W_GRP), F32), jax.ShapeDtypeStruct((W_GRP, W_GRP), F32),
                   jax.ShapeDtypeStruct((1, W_GRP), F32)),
        in_specs=[_zslab(W_GRP, 2), _full((SEQ, W_GRP)), _full((W_GRP, W_GRP)), _full((1, W_GRP))],
        out_specs=(_full((SEQ, W_GRP)), _full((W_GRP, W_GRP)), _full((1, W_GRP))),
        compiler_params=_params("arbitrary"),
    )(z, dy, w_bd, scale)


def _conv_fwd(z, w):
    def body(z_ref, w_ref, y_ref):
        zc = z_ref[...]
        bg, cg, xh = zc[:, :W_GRP], zc[:, W_GRP:2 * W_GRP], zc[:, 2 * W_GRP:]
        rows = lax.broadcasted_iota(jnp.int32, (SEQ, W_GRP), 0)
        y = cg * xh
        conv = w_ref[0:1, :] * _shift_down(y, 2, rows) + w_ref[1:2, :] * _shift_down(y, 1, rows) + w_ref[2:3, :] * y
        y_ref[...] = bg * conv

    return pl.pallas_call(
        body, name="conv_fwd", grid=(1,),
        out_shape=jax.ShapeDtypeStruct((SEQ, W_GRP), F32),
        in_specs=[_zslab(3 * W_GRP, 1), _full((3, W_GRP))],
        out_specs=_full((SEQ, W_GRP)), compiler_params=_params("arbitrary"),
    )(z, w)


def _conv_bwd(z, dy, w):
    def body(z_ref, dy_ref, w_ref, dz_ref, dw_ref):
        zc = z_ref[...]
        bg, cg, xh = zc[:, :W_GRP], zc[:, W_GRP:2 * W_GRP], zc[:, 2 * W_GRP:]
        rows = lax.broadcasted_iota(jnp.int32, (SEQ, W_GRP), 0)
        y = cg * xh
        y2, y1 = _shift_down(y, 2, rows), _shift_down(y, 1, rows)
        conv = w_ref[0:1, :] * y2 + w_ref[1:2, :] * y1 + w_ref[2:3, :] * y
        dyv = dy_ref[...]
        dconv = dyv * bg
        dw_ref[...] = jnp.zeros_like(dw_ref)
        dw_ref[0:1, :] = _sum0(dconv * y2)
        dw_ref[1:2, :] = _sum0(dconv * y1)
        dw_ref[2:3, :] = _sum0(dconv * y)
        dyy = (w_ref[0:1, :] * _shift_up(dconv, 2, rows) + w_ref[1:2, :] * _shift_up(dconv, 1, rows)
               + w_ref[2:3, :] * dconv)
        dz_ref[:, :W_GRP] = dyv * conv
        dz_ref[:, W_GRP:2 * W_GRP] = dyy * xh
        dz_ref[:, 2 * W_GRP:] = dyy * cg

    return pl.pallas_call(
        body, name="conv_bwd", grid=(1,),
        out_shape=(jax.ShapeDtypeStruct((SEQ, 3 * W_GRP), F32), jax.ShapeDtypeStruct((SUBLANES, W_GRP), F32)),
        in_specs=[_zslab(3 * W_GRP, 1), _full((SEQ, W_GRP)), _full((3, W_GRP))],
        out_specs=(_full((SEQ, 3 * W_GRP)), _full((SUBLANES, W_GRP))),
        compiler_params=_params("arbitrary"),
    )(z, dy, w)


def _s5_disc(lre, lim, ldt, br, bi):
    dt = jnp.exp(ldt)
    mag = jnp.exp(lre * dt)
    ang = lim * dt
    a_re, a_im = mag * jnp.cos(ang), mag * jnp.sin(ang)
    nr, ni = a_re - 1.0, a_im
    den = lre * lre + lim * lim
    k_re = (nr * lre + ni * lim) / den
    k_im = (ni * lre - nr * lim) / den
    return a_re, a_im, k_re * br - k_im * bi, k_re * bi + k_im * br


def _s5_prep_fwd(lre, lim, ldt, br, bi):
    def body(lre_ref, lim_ref, ldt_ref, br_ref, bi_ref, ar_ref, ai_ref, bbr_ref, bbi_ref):
        ar, ai, bbr, bbi = _s5_disc(lre_ref[...], lim_ref[...], ldt_ref[...], br_ref[...], bi_ref[...])
        ar_ref[...] = ar
        ai_ref[...] = ai
        bbr_ref[...] = bbr
        bbi_ref[...] = bbi

    return pl.pallas_call(
        body, name="s5_prep_fwd",
        out_shape=(jax.ShapeDtypeStruct(lre.shape, F32), jax.ShapeDtypeStruct(lre.shape, F32),
                   jax.ShapeDtypeStruct(br.shape, F32), jax.ShapeDtypeStruct(br.shape, F32)),
        compiler_params=_params(),
    )(lre, lim, ldt, br, bi)


def _s5_prep_bwd(lre, lim, ldt, br, bi, dar, dai, dbbr, dbbi):
    def body(lre_ref, lim_ref, ldt_ref, br_ref, bi_ref, dar_ref, dai_ref, dbbr_ref, dbbi_ref,
             o_lre, o_lim, o_ldt, o_br, o_bi):
        _, pull = jax.vjp(_s5_disc, lre_ref[...], lim_ref[...], ldt_ref[...], br_ref[...], bi_ref[...])
        g = pull((dar_ref[...], dai_ref[...], dbbr_ref[...], dbbi_ref[...]))
        for o, v in zip((o_lre, o_lim, o_ldt, o_br, o_bi), g):
            o[...] = v

    return pl.pallas_call(
        body, name="s5_prep_bwd",
        out_shape=tuple(jax.ShapeDtypeStruct(a.shape, F32) for a in (lre, lim, ldt, br, bi)),
        compiler_params=_params(),
    )(lre, lim, ldt, br, bi, dar, dai, dbbr, dbbi)


def _cmul(ar, ai, br, bi):
    return ar * br - ai * bi, ar * bi + ai * br


def _s5_tile_consts(ar, ai, reverse):
    if reverse:
        ai = -ai
    shape = (SUBLANES, S5_BLOCK)
    row = lax.broadcasted_iota(jnp.int32, shape, 0)
    a1 = (jnp.broadcast_to(ar, shape), jnp.broadcast_to(ai, shape))
    a2 = _cmul(*a1, *a1)
    a4 = _cmul(*a2, *a2)
    a8 = _cmul(*a4, *a4)
    steps = []
    for s, (pr, pi) in ((1, a1), (2, a2), (4, a4)):
        keep = (row < SUBLANES - s) if reverse else (row >= s)
        steps.append((s, jnp.where(keep, pr, 0.0), jnp.where(keep, pi, 0.0)))
    e = (SUBLANES - row) if reverse else (row + 1)
    pr, pi = jnp.ones(shape, F32), jnp.zeros(shape, F32)
    for bit, (qr, qi) in ((1, a1), (2, a2), (4, a4), (8, a8)):
        nr, ni = _cmul(pr, pi, qr, qi)
        hit = (e & bit) != 0
        pr, pi = jnp.where(hit, nr, pr), jnp.where(hit, ni, pi)
    return steps, pr, pi


def _s5_tile(xr, xi, steps, reverse):
    for s, pr, pi in steps:
        sh = SUBLANES - s if reverse else s
        sr, si = pltpu.roll(xr, sh, axis=0), pltpu.roll(xi, sh, axis=0)
        xr, xi = xr + pr * sr - pi * si, xi + pr * si + pi * sr
    return xr, xi


N_TILES = SEQ // SUBLANES


def _s5_fwd(z, b_re, b_im, c_re, c_im, a_re, a_im, d, glu_w, glu_b):
    nblk = S5_LANES // S5_BLOCK

    def body(u_ref, br_ref, bi_ref, cr_ref, ci_ref, ar_ref, ai_ref, d_ref, gw_ref, gb_ref,
             y_ref, y0_ref, xr_ref, xi_ref, ub_scr, acc_scr):
        jb = pl.program_id(0)

        @pl.when(jb == 0)
        def _():
            ub_scr[...] = u_ref[...].astype(BF16)
            acc_scr[...] = jnp.zeros_like(acc_scr)

        ub = ub_scr[...]
        xr_ref[...] = _dot(ub, br_ref[...])
        xi_ref[...] = _dot(ub, bi_ref[...])
        steps, pr, pi = _s5_tile_consts(ar_ref[...], ai_ref[...], False)

        def tile(t, carry):
            cr, ci = carry
            rows = pl.ds(pl.multiple_of(t * SUBLANES, SUBLANES), SUBLANES)
            xr, xi = _s5_tile(xr_ref[rows, :], xi_ref[rows, :], steps, False)
            xr, xi = xr + pr * cr - pi * ci, xi + pr * ci + pi * cr
            xr_ref[rows, :] = xr
            xi_ref[rows, :] = xi
            return xr[SUBLANES - 1:, :], xi[SUBLANES - 1:, :]

        zero = jnp.zeros((1, S5_BLOCK), F32)
        lax.fori_loop(0, N_TILES, tile, (zero, zero), unroll=2)
        acc_scr[...] += (_dot(xr_ref[...].astype(BF16), cr_ref[...]) - _dot(xi_ref[...].astype(BF16), ci_ref[...]))

        @pl.when(jb == nblk - 1)
        def _():
            y0 = acc_scr[...] + d_ref[...] * u_ref[...]
            y0_ref[...] = y0
            y1 = _gelu(y0)
            y_ref[...] = y1 * _sigmoid(_dot(y1.astype(BF16), gw_ref[...]) + gb_ref[...])

    lane_blk = pl.BlockSpec((SEQ, S5_BLOCK), lambda j: (0, j))
    return pl.pallas_call(
        body, name="s5_fwd", grid=(nblk,),
        out_shape=(jax.ShapeDtypeStruct((SEQ, W_GRP), F32), jax.ShapeDtypeStruct((SEQ, W_GRP), F32),
                   jax.ShapeDtypeStruct((SEQ, S5_LANES), F32), jax.ShapeDtypeStruct((SEQ, S5_LANES), F32)),
        in_specs=[_zslab(W_GRP, 6),
                  pl.BlockSpec((W_GRP, S5_BLOCK), lambda j: (0, j)), pl.BlockSpec((W_GRP, S5_BLOCK), lambda j: (0, j)),
                  pl.BlockSpec((S5_BLOCK, W_GRP), lambda j: (j, 0)), pl.BlockSpec((S5_BLOCK, W_GRP), lambda j: (j, 0)),
                  pl.BlockSpec((1, S5_BLOCK), lambda j: (0, j)), pl.BlockSpec((1, S5_BLOCK), lambda j: (0, j)),
                  _full((1, W_GRP)), _full((W_GRP, W_GRP)), _full((1, W_GRP))],
        out_specs=(_full((SEQ, W_GRP)), _full((SEQ, W_GRP)), lane_blk, lane_blk),
        scratch_shapes=[pltpu.VMEM((SEQ, W_GRP), BF16), pltpu.VMEM((SEQ, W_GRP), F32)],
        compiler_params=_params("arbitrary"),
    )(z, b_re, b_im, c_re, c_im, a_re, a_im, d, glu_w, glu_b)


def _s5_bwd(z, y0, dy, xr, xi, b_re, b_im, c_re, c_im, a_re, a_im, d, glu_w, glu_b):
    nblk = S5_LANES // S5_BLOCK

    def body(u_ref, y0_ref, dy_ref, xr_ref, xi_ref, br_ref, bi_ref, cr_ref, ci_ref, ar_ref, ai_ref,
             d_ref, gw_ref, gb_ref,
             du_ref, dbr_ref, dbi_ref, dcr_ref, dci_ref, dar_ref, dai_ref, dd_ref, dgw_ref, dgb_ref,
             ub_scr, dy0_scr, du_scr, lr_scr, li_scr):
        jb = pl.program_id(0)

        @pl.when(jb == 0)
        def _():
            u = u_ref[...]
            y0v = y0_ref[...]
            y1 = _gelu(y0v)
            y1b = y1.astype(BF16)
            sg = _sigmoid(_dot(y1b, gw_ref[...]) + gb_ref[...])
            dyv = dy_ref[...]
            dpre = dyv * y1 * sg * (1.0 - sg)
            dpb = dpre.astype(BF16)
            dgw_ref[...] = _dot_tn(y1b, dpb)
            dgb_ref[...] = _sum0(dpre)
            dy1 = dyv * sg + _dot_nt(dpb, gw_ref[...])
            dy0 = dy1 * _gelu_grad(y0v)
            dd_ref[...] = _sum0(dy0 * u)
            du_scr[...] = dy0 * d_ref[...]
            dy0_scr[...] = dy0.astype(BF16)
            ub_scr[...] = u.astype(BF16)

        dy0b = dy0_scr[...]
        lr_scr[...] = _dot_nt(dy0b, cr_ref[...])
        li_scr[...] = -_dot_nt(dy0b, ci_ref[...])
        dcr_ref[...] = _dot_tn(xr_ref[...].astype(BF16), dy0b)
        dci_ref[...] = -_dot_tn(xi_ref[...].astype(BF16), dy0b)
        steps, pr, pi = _s5_tile_consts(ar_ref[...], ai_ref[...], True)
        row = lax.broadcasted_iota(jnp.int32, (SUBLANES, S5_BLOCK), 0)

        def tile(k, carry):
            cr, ci, accr, acci = carry
            t = N_TILES - 1 - k
            rows = pl.ds(pl.multiple_of(t * SUBLANES, SUBLANES), SUBLANES)
            lr, li = _s5_tile(lr_scr[rows, :], li_scr[rows, :], steps, True)
            lr, li = lr + pr * cr - pi * ci, li + pr * ci + pi * cr
            lr_scr[rows, :] = lr
            li_scr[rows, :] = li
            prev = pl.ds(pl.multiple_of(jnp.maximum(t - 1, 0) * SUBLANES, SUBLANES), SUBLANES)
            live = jnp.where(t > 0, 1.0, 0.0)
            xpr = jnp.where(row == 0, pltpu.roll(xr_ref[prev, :], 1, axis=0) * live, pltpu.roll(xr_ref[rows, :], 1, axis=0))
            xpi = jnp.where(row == 0, pltpu.roll(xi_ref[prev, :], 1, axis=0) * live, pltpu.roll(xi_ref[rows, :], 1, axis=0))
            accr = accr + lr * xpr + li * xpi
            acci = acci + li * xpr - lr * xpi
            return lr[0:1, :], li[0:1, :], accr, acci

        zero = jnp.zeros((1, S5_BLOCK), F32)
        zt = jnp.zeros((SUBLANES, S5_BLOCK), F32)
        _, _, accr, acci = lax.fori_loop(0, N_TILES, tile, (zero, zero, zt, zt), unroll=2)
        dar_ref[...] = jnp.zeros_like(dar_ref)
        dai_ref[...] = jnp.zeros_like(dai_ref)
        dar_ref[0:1, :] = _sum0(accr)
        dai_ref[0:1, :] = _sum0(acci)
        lrb, lib = lr_scr[...].astype(BF16), li_scr[...].astype(BF16)
        ub = ub_scr[...]
        dbr_ref[...] = _dot_tn(ub, lrb)
        dbi_ref[...] = _dot_tn(ub, lib)
        du_scr[...] += _dot_nt(lrb, br_ref[...]) + _dot_nt(lib, bi_ref[...])

        @pl.when(jb == nblk - 1)
        def _():
            du_ref[...] = du_scr[...]

    lane_blk = pl.BlockSpec((SEQ, S5_BLOCK), lambda j: (0, j))
    bspec = pl.BlockSpec((W_GRP, S5_BLOCK), lambda j: (0, j))
    cspec = pl.BlockSpec((S5_BLOCK, W_GRP), lambda j: (j, 0))
    aspec = pl.BlockSpec((1, S5_BLOCK), lambda j: (0, j))
    a8spec = pl.BlockSpec((SUBLANES, S5_BLOCK), lambda j: (0, j))
    sd = jax.ShapeDtypeStruct
    return pl.pallas_call(
        body, name="s5_bwd", grid=(nblk,),
        out_shape=(sd((SEQ, W_GRP), F32), sd((W_GRP, S5_LANES), F32), sd((W_GRP, S5_LANES), F32),
                   sd((S5_LANES, W_GRP), F32), sd((S5_LANES, W_GRP), F32),
                   sd((SUBLANES, S5_LANES), F32), sd((SUBLANES, S5_LANES), F32),
                   sd((1, W_GRP), F32), sd((W_GRP, W_GRP), F32), sd((1, W_GRP), F32)),
        in_specs=[_zslab(W_GRP, 6), _full((SEQ, W_GRP)), _full((SEQ, W_GRP)), lane_blk, lane_blk,
                  bspec, bspec, cspec, cspec, aspec, aspec,
                  _full((1, W_GRP)), _full((W_GRP, W_GRP)), _full((1, W_GRP))],
        out_specs=(_full((SEQ, W_GRP)), bspec, bspec, cspec, cspec, a8spec, a8spec,
                   _full((1, W_GRP)), _full((W_GRP, W_GRP)), _full((1, W_GRP))),
        scratch_shapes=[pltpu.VMEM((SEQ, W_GRP), BF16), pltpu.VMEM((SEQ, W_GRP), BF16), pltpu.VMEM((SEQ, W_GRP), F32),
                        pltpu.VMEM((SEQ, S5_BLOCK), F32), pltpu.VMEM((SEQ, S5_BLOCK), F32)],
        compiler_params=_params("arbitrary"),
    )(z, y0, dy, xr, xi, b_re, b_im, c_re, c_im, a_re, a_im, d, glu_w, glu_b)


def _head(x, g, target):
    tm = TOKEN_TILE
    n = SEQ // tm

    def body(x_ref, g_ref, t_ref, dx_ref, st_ref, acc_scr):
        i = pl.program_id(0)

        @pl.when(i == 0)
        def _():
            acc_scr[...] = jnp.zeros_like(acc_scr)

        xhat, r = _rms(x_ref[...])
        gv = g_ref[...]
        err = xhat * gv - t_ref[...]
        dyv = err * (1.0 / D_MODEL)
        dx_ref[...] = _rms_bwd(xhat, r, dyv * gv)
        acc_scr[0:1, :] += _sum0(err * err)
        acc_scr[1:2, :] += _sum0(dyv * xhat)

        @pl.when(i == n - 1)
        def _():
            st_ref[...] = acc_scr[...]
            tot = jnp.sum(acc_scr[0:1, :], axis=-1, keepdims=True) * (0.5 / D_MODEL)
            st_ref[0:1, :] = jnp.broadcast_to(tot, (1, D_MODEL))

    tok = pl.BlockSpec((tm, D_MODEL), lambda i: (i, 0))
    return pl.pallas_call(
        body, name="head", grid=(n,),
        out_shape=(jax.ShapeDtypeStruct((SEQ, D_MODEL), F32), jax.ShapeDtypeStruct((SUBLANES, D_MODEL), F32)),
        in_specs=[tok, _row_spec(1), tok], out_specs=(tok, _row_spec(SUBLANES)),
        scratch_shapes=[pltpu.VMEM((SUBLANES, D_MODEL), F32)],
        compiler_params=_params("arbitrary"),
    )(x, g, target)


def _adamw(w, gparts, m, v, name):
    r, c = w.shape
    npart = gparts.shape[0]
    tr = r
    for cand in (512, 256, 128, 64, 32, 16):
        if r % cand == 0 and r > cand:
            tr = cand
            break
    b1c = 1.0 - ADAM_B1 ** ADAM_STEP
    b2c = 1.0 - ADAM_B2 ** ADAM_STEP

    def body(w_ref, g_ref, m_ref, v_ref, go_ref, d_ref, mo_ref, vo_ref):
        g = g_ref[0].astype(F32)
        for k in range(1, npart):
            g = g + g_ref[k].astype(F32)
        mn = ADAM_B1 * m_ref[...] + (1.0 - ADAM_B1) * g
        vn = ADAM_B2 * v_ref[...] + (1.0 - ADAM_B2) * (g * g)
        m_hat = mn / b1c
        v_hat = vn / b2c
        go_ref[...] = g
        d_ref[...] = -ADAM_LR * (m_hat / (jnp.sqrt(v_hat) + ADAM_EPS) + ADAM_WD * w_ref[...])
        mo_ref[...] = mn
        vo_ref[...] = vn

    blk = pl.BlockSpec((tr, c), lambda i: (i, 0))
    sh = jax.ShapeDtypeStruct((r, c), F32)
    return pl.pallas_call(
        body, name=name, grid=(r // tr,),
        out_shape=(sh, sh, sh, sh),
        in_specs=[blk, pl.BlockSpec((npart, tr, c), lambda i: (0, i, 0)), blk, blk],
        out_specs=(blk, blk, blk, blk), compiler_params=_params("arbitrary"),
    )(w, gparts, m, v)


def _adamw_layer(l, w, gparts, m, v, prev, name):
    _, r, c = w.shape
    tr = max(t for t in range(16, 513, 16) if r % t == 0)
    b1c = 1.0 - ADAM_B1 ** ADAM_STEP
    b2c = 1.0 - ADAM_B2 ** ADAM_STEP
    nprev = 0 if prev is None else 4

    def body(core_ref, *refs):
        w_ref, m_ref, v_ref = refs[:3]
        g_refs = refs[3:3 + N_CHIPS]
        go_ref, d_ref, mo_ref, vo_ref = refs[3 + N_CHIPS + nprev:]
        g = g_refs[0][0].astype(F32)
        for g_ref in g_refs[1:]:
            g = g + g_ref[0].astype(F32)
        mn = ADAM_B1 * m_ref[0] + (1.0 - ADAM_B1) * g
        vn = ADAM_B2 * v_ref[0] + (1.0 - ADAM_B2) * (g * g)
        go_ref[0] = g
        d_ref[0] = -ADAM_LR * ((mn / b1c) / (jnp.sqrt(vn / b2c) + ADAM_EPS) + ADAM_WD * w_ref[0])
        mo_ref[0] = mn
        vo_ref[0] = vn

    blk = pl.BlockSpec((1, tr, c), lambda i, core: (l, i, 0))
    slots = [pl.BlockSpec((1, tr, c), functools.partial(lambda i, core, q: (2 * q + core[0], i, 0), q=q))
             for q in range(N_CHIPS)]
    sh = jax.ShapeDtypeStruct(w.shape, F32)
    keep = [pl.BlockSpec(memory_space=pl.ANY)] * nprev
    return pl.pallas_call(
        body, name=name, out_shape=(sh, sh, sh, sh),
        grid_spec=pltpu.PrefetchScalarGridSpec(num_scalar_prefetch=1, grid=(r // tr,),
                                               in_specs=[blk, blk, blk, *slots, *keep], out_specs=(blk, blk, blk, blk)),
        input_output_aliases={4 + N_CHIPS + k: k for k in range(nprev)},
        compiler_params=_params("arbitrary"),
    )(_core_index(), w, m, v, *([gparts] * N_CHIPS), *(prev or ()))


def _sum_parts(parts, name):
    n, r, c = parts.shape

    def body(p_ref, o_ref):
        acc = p_ref[0]
        for k in range(1, n):
            acc = acc + p_ref[k]
        o_ref[...] = acc

    return pl.pallas_call(
        body, name=name, out_shape=jax.ShapeDtypeStruct((r, c), F32), compiler_params=_params(),
    )(parts)


def _block_diag(blocks):
    g, a, b = blocks.shape
    eye = jnp.eye(g, dtype=blocks.dtype)
    return (blocks[:, :, None, :] * eye[:, None, :, None]).reshape(g * a, g * b)


def _diag_blocks(dense, g):
    a, b = dense.shape[0] // g, dense.shape[1] // g
    d4 = dense.reshape(g, a, g, b)
    eye = jnp.eye(g, dtype=dense.dtype)
    return jnp.sum(d4 * eye[:, None, :, None], axis=2)


def _pack(parts, cols):
    flat = jnp.concatenate([p.reshape(-1) for p in parts])
    unit = N_DEV * SUBLANES * cols
    total = -(-flat.shape[0] // unit) * unit
    flat = jnp.pad(flat, (0, total - flat.shape[0]))
    return flat.reshape(N_DEV, total // (N_DEV * cols), cols)


def _unpack(flat, shapes):
    out, pos = [], 0
    for s in shapes:
        n = math.prod(s)
        out.append(flat[pos:pos + n].reshape(s))
        pos += n
    return out


SMALL = ("norm1_g", "norm2_g", "sgu_w", "sgu_b", "pool_w", "pool_scale", "conv_w", "s5_lambda_re", "s5_lambda_im",
         "s5_b_re", "s5_b_im", "s5_c_re", "s5_c_im", "s5_d", "s5_log_dt", "s5_glu_w", "s5_glu_b", "mix_norm_g",
         "norm3_g", "final_norm_g")
BIG = ("ffn1_w_in", "ffn1_w_out", "w_mix_in", "w_mix_out", "ffn2_w_in", "ffn2_w_out")
TRANSPOSED = ("ffn1_w_in", "w_mix_in", "ffn2_w_in")
WEIGHTS = ("ada_w", "ada_b", "norm1_g", "ffn1_w_in", "ffn1_w_out", "norm2_g", "w_mix_in", "sgu_w", "sgu_b", "pool_w",
           "pool_scale", "conv_w", "s5_lambda_re", "s5_lambda_im", "s5_b_re", "s5_b_im", "s5_c_re", "s5_c_im", "s5_d",
           "s5_log_dt", "s5_glu_w", "s5_glu_b", "mix_norm_g", "w_mix_out", "norm3_g", "ffn2_w_in", "ffn2_w_out",
           "final_norm_g")
PACK_COLS = 1024


def kernel(x, c, ada_w, ada_b, norm1_g, ffn1_w_in, ffn1_w_out, norm2_g, w_mix_in, sgu_w, sgu_b, pool_w, pool_scale, conv_w, s5_lambda_re, s5_lambda_im, s5_b_re, s5_b_im, s5_c_re, s5_c_im, s5_d, s5_log_dt, s5_glu_w, s5_glu_b, mix_norm_g, w_mix_out, norm3_g, ffn2_w_in, ffn2_w_out, final_norm_g, loss_target, m_ada_w, m_ada_b, m_norm1_g, m_ffn1_w_in, m_ffn1_w_out, m_norm2_g, m_w_mix_in, m_sgu_w, m_sgu_b, m_pool_w, m_pool_scale, m_conv_w, m_s5_lambda_re, m_s5_lambda_im, m_s5_b_re, m_s5_b_im, m_s5_c_re, m_s5_c_im, m_s5_d, m_s5_log_dt, m_s5_glu_w, m_s5_glu_b, m_mix_norm_g, m_w_mix_out, m_norm3_g, m_ffn2_w_in, m_ffn2_w_out, m_final_norm_g, v_ada_w, v_ada_b, v_norm1_g, v_ffn1_w_in, v_ffn1_w_out, v_norm2_g, v_w_mix_in, v_sgu_w, v_sgu_b, v_pool_w, v_pool_scale, v_conv_w, v_s5_lambda_re, v_s5_lambda_im, v_s5_b_re, v_s5_b_im, v_s5_c_re, v_s5_c_im, v_s5_d, v_s5_log_dt, v_s5_glu_w, v_s5_glu_b, v_mix_norm_g, v_w_mix_out, v_norm3_g, v_ffn2_w_in, v_ffn2_w_out, v_final_norm_g):
    args = dict(locals())
    W = {n: args[n] for n in WEIGHTS}
    M = {n: args["m_" + n] for n in WEIGHTS}
    V = {n: args["v_" + n] for n in WEIGHTS}
    me = _me()
    L = DEPTH
    x0 = x[0]
    target = loss_target[0]

    conv_cols = conv_w.shape[-1]
    glu_rows = s5_glu_w.shape[1]
    c_g, conv_g, glu_g = _exchange(
        [c.reshape(SUBLANES, LANES), conv_w.reshape(L * 3, conv_cols), s5_glu_w.reshape(L * glu_rows, W_GRP)],
        False, "gather_small")
    c_all = c_g.reshape(N_DEV, D_MODEL)
    conv_full = conv_g.reshape(N_DEV, L, 3, conv_cols).transpose(1, 2, 0, 3).reshape(L, 3, W_GRP)
    glu_full = glu_g.reshape(N_DEV, L, glu_rows, W_GRP).transpose(1, 0, 2, 3).reshape(L, W_GRP, W_GRP)

    ncol = ada_w.shape[-1]
    ada_b_mine = lax.dynamic_slice_in_dim(ada_b, me * ncol, ncol, axis=1).reshape(L, 1, ncol)
    cond_part = _cond_fwd(c_all, ada_w, ada_b_mine)
    (cond_g,) = _exchange([cond_part.reshape(L * N_DEV, ncol)], False, "gather_cond")
    cond_g = cond_g.reshape(N_DEV, L, N_DEV, ncol)
    cond_mine = lax.dynamic_index_in_dim(cond_g, me, axis=2, keepdims=False)
    cond = cond_mine.transpose(1, 0, 2).reshape(L, N_ADA, D_MODEL)

    lg = L * S5_GROUPS
    lre3 = s5_lambda_re.reshape(lg, S5_STATE, 1)
    lim3 = s5_lambda_im.reshape(lg, S5_STATE, 1)
    ldt3 = s5_log_dt.reshape(lg, 1, 1)
    br3 = s5_b_re.reshape(lg, S5_STATE, S5_GROUP_CH)
    bi3 = s5_b_im.reshape(lg, S5_STATE, S5_GROUP_CH)

    def b_mat(bb3, l):
        return _block_diag(bb3.reshape(L, S5_GROUPS, S5_STATE, S5_GROUP_CH)[l].transpose(0, 2, 1)).astype(BF16)

    def c_mat(cw, l):
        return _block_diag(cw[l].transpose(0, 2, 1)).astype(BF16)

    for n in TRANSPOSED:
        W[n], M[n], V[n] = (a.transpose(0, 2, 1) for a in (W[n], M[n], V[n]))

    gathered_shape = {"ffn1_w_out": (N_FF_CHUNK, FF_PIECE, D_MODEL), "ffn2_w_out": (N_FF_CHUNK, FF_PIECE, D_MODEL),
                      "w_mix_in": (P_IN, D_MODEL), "w_mix_out": (D_MODEL, D_MODEL)}

    def gather_start(l, after, names=BIG):
        srcs = [W[n][l].astype(BF16) for n in names]
        lands = _place_own(srcs, False, "gather_weights_own")
        return _exchange_start(srcs, lands, after, False, "gather_weights_start", OTHER_CHIPS)

    def gather_finish(handle, after, names=BIG):
        g = _exchange_wait(handle, after, False, "gather_weights_wait", OTHER_CHIPS)
        out = dict(zip(names, _gather_sibling(g, "gather_weights_sibling")))
        return {n: (a.reshape(gathered_shape[n]) if n in gathered_shape else a) for n, a in out.items()}

    saved = []
    xc = x0
    first_ffn = BIG[:2]
    handle, token = gather_start(0, cond, first_ffn)
    handle_rest, token = gather_start(0, token, BIG[2:])

    zero = token[0, 0]
    a_re3, a_im3, bb_re3, bb_im3 = _s5_prep_fwd(lre3 + zero, lim3, ldt3, br3, bi3)
    a_re = a_re3.reshape(L, 1, S5_LANES)
    a_im = a_im3.reshape(L, 1, S5_LANES)
    sgu_w_t, pool_w_t, c_re_t, c_im_t = (a + zero for a in (sgu_w, pool_w, s5_c_re, s5_c_im))

    def mixer_consts(l):
        w_cat = sgu_w_t[l].transpose(1, 0, 2).reshape(CHUNK, SGU_HEADS * CHUNK)
        bias = jnp.repeat(sgu_b[l].T, SGU_HEAD_DIM, axis=1)
        return dict(
            w_cat=w_cat, bias=bias, pool_bd=_block_diag(pool_w_t[l]).astype(BF16), pool_scale=pool_scale[l][None],
            conv=conv_full[l], b_re=b_mat(bb_re3, l), b_im=b_mat(bb_im3, l), c_re=c_mat(c_re_t, l),
            c_im=c_mat(c_im_t, l), a_re=a_re[l], a_im=a_im[l], d=s5_d[l][None], glu_w=glu_full[l].astype(BF16),
            glu_b=s5_glu_b[l][None])

    mcs = [mixer_consts(l) for l in range(L)]
    small_names = SMALL + ("ada_b",)
    first_small = W[small_names[0]] + zero
    packed_w = _pack([first_small] + [W[n] for n in small_names[1:]], PACK_COLS)
    packed_m = _pack([M[n] + zero if n == small_names[0] else M[n] for n in small_names], PACK_COLS)
    packed_v = _pack([V[n] + zero if n == small_names[0] else V[n] for n in small_names], PACK_COLS)
    wl = gather_finish(handle, [x0] + [a for mc in mcs for a in mc.values()], first_ffn)
    for l in range(L):
        cl = cond[l]
        if 0 < l < L - 1:
            handle, token = gather_start(l + 1, wl["ffn1_w_in"])
            cl = cl + token[0, 0]
        mc = mcs[l]
        x_a = xc
        x_b, *ffn1_kept = _ffn_fwd(x_a, cl[0:3], norm1_g[l][None], wl["ffn1_w_in"], wl["ffn1_w_out"])
        if l == 0:
            wl = {**wl, **gather_finish(handle_rest, [x_b, packed_w, packed_m, packed_v], BIG[2:])}
            handle, token = gather_start(1, wl["w_mix_in"])
            cl = cl + token[0, 0]
        z = _mix_in_fwd(x_b, cl[3:5], norm2_g[l][None], wl["w_mix_in"])
        ya = _sgu_fwd(z, mc["w_cat"], mc["bias"])
        yb = _pool_fwd(z, mc["pool_bd"], mc["pool_scale"])
        yc = _conv_fwd(z, mc["conv"])
        yd, y0, sxr, sxi = _s5_fwd(z, mc["b_re"], mc["b_im"], mc["c_re"], mc["c_im"], mc["a_re"], mc["a_im"],
                                   mc["d"], mc["glu_w"], mc["glu_b"])
        ys = (ya, yb, yc, yd)
        x_c = _mix_out_fwd(x_b, ys, mix_norm_g[l][None], cl[5:6], wl["w_mix_out"])
        x_d, *ffn2_kept = _ffn_fwd(x_c, cl[6:9], norm3_g[l][None], wl["ffn2_w_in"], wl["ffn2_w_out"])
        saved.append(dict(wl=wl, mc=mc, x_a=x_a, x_b=x_b, x_c=x_c, ffn1=ffn1_kept, ffn2=ffn2_kept, z=z, ys=ys, y0=y0,
                          xr=sxr, xi=sxi))
        xc = x_d
        if l + 1 < L:
            wl = gather_finish(handle, x_d)

    dx, stats = _head(xc, final_norm_g[None], target)
    loss = lax.psum(stats[0, 0], MESH_AXES)

    small_grads = {n: [None] * L for n in SMALL if n != "final_norm_g"}
    small_grads["final_norm_g"] = stats[1]
    big_out = {n: None for n in BIG}
    pending = None

    def swap_begin(pieces, after):
        bufs = _place_own(pieces, True, "scatter_grads_own")
        return _exchange_start(pieces, bufs, after, True, "swap_grads_start", SIBLING_SWAP)

    def scatter_begin(swap_handle, after):
        mine, theirs = _exchange_wait(swap_handle, after, True, "swap_grads_wait", SIBLING_SWAP, with_srcs=True)
        sums = _presum(mine, theirs, "presum_grads")
        lands = _place_own(sums, True, "scatter_grads_own")
        return _exchange_start(sums, lands, sums[0], True, "scatter_grads_start", OTHER_CHIPS)

    def finish_scatter(pend, after):
        layer, hnd = pend
        recv = _exchange_wait(hnd, after, True, "scatter_grads_wait", OTHER_CHIPS)
        for n, r in zip(BIG, recv):
            big_out[n] = _adamw_layer(layer, W[n], r, M[n], V[n], big_out[n], "adamw_" + n)
        return recv[0]

    dcond_rows = [None] * L
    d_are, d_aim, d_bbre, d_bbim = [None] * L, [None] * L, [None] * L, [None] * L
    swap = None
    for l in reversed(range(L)):
        sv = saved[l]
        wl, mc, cl = sv["wl"], sv["mc"], cond[l]
        if swap is not None:
            cl = cl + swap[1][0, 0]
        f2, a2, b2, hb = sv["ffn2"]
        dx, da, db, act, dob, part3 = _ffn_bwd(dx, sv["x_c"], f2, a2, b2, cl[6:9], norm3_g[l][None],
                                               wl["ffn2_w_in"], wl["ffn2_w_out"])
        if swap is not None:
            handle, token = scatter_begin(swap[0], dx)
            pending = (l + 1, handle)
            cl = cl + token[0, 0]
        g_ffn2_in = _dw(da, hb, db, name="dw_ffn_in")
        g_ffn2_out = _dw(act, dob, name="dw_ffn_out").reshape(N_DEV, D_FF // N_DEV, D_MODEL)
        dya, dyb, dyc, dyd, ynb, dmob, part_mo = _mix_out_bwd(dx, sv["ys"], mix_norm_g[l][None], cl[5:6],
                                                              wl["w_mix_out"])
        g_mix_out = _dw(ynb[None], dmob, name="dw_mix_out").reshape(N_DEV, D_MODEL // N_DEV, D_MODEL)
        z = sv["z"]
        dza, dw_cat, dbias = _sgu_bwd(z, dya, mc["w_cat"], mc["bias"])
        dzb, dpool_dense, dpool_scale = _pool_bwd(z, dyb, mc["pool_bd"], mc["pool_scale"])
        dzc, dconv8 = _conv_bwd(z, dyc, mc["conv"])
        (dzd, dbre_d, dbim_d, dcre_d, dcim_d, dar8, dai8, dd, dglu_w, dglu_b) = _s5_bwd(
            z, sv["y0"], dyd, sv["xr"], sv["xi"], mc["b_re"], mc["b_im"], mc["c_re"], mc["c_im"],
            mc["a_re"], mc["a_im"], mc["d"], mc["glu_w"], mc["glu_b"])
        dx, h2b, dzbf, part2 = _mix_in_bwd((dza, dzb, dzc, dzd), sv["x_b"], dx, cl[3:5], norm2_g[l][None],
                                           wl["w_mix_in"])
        g_mix_in = _dw(dzbf[None], h2b, name="dw_mix_in").reshape(N_DEV, P_IN // N_DEV, D_MODEL)
        f1, a1, b1, hb = sv["ffn1"]
        dx, da, db, act, dob, part1 = _ffn_bwd(dx, sv["x_a"], f1, a1, b1, cl[0:3], norm1_g[l][None],
                                               wl["ffn1_w_in"], wl["ffn1_w_out"])
        g_ffn1_in = _dw(da, hb, db, name="dw_ffn_in")
        g_ffn1_out = _dw(act, dob, name="dw_ffn_out").reshape(N_DEV, D_FF // N_DEV, D_MODEL)
        last = finish_scatter(pending, g_ffn1_in) if pending is not None else g_ffn1_in
        swap = swap_begin([g_ffn1_in, g_ffn1_out, g_mix_in, g_mix_out, g_ffn2_in, g_ffn2_out], last)
        dcond_rows[l] = jnp.concatenate([part1[0:3], part2[0:2], part_mo[0:1], part3[0:3]], axis=0)
        sg = small_grads
        sg["norm1_g"][l] = part1[3]
        sg["norm2_g"][l] = part2[2]
        sg["norm3_g"][l] = part3[3]
        sg["mix_norm_g"][l] = part_mo[1]
        sg["sgu_w"][l] = dw_cat.reshape(CHUNK, SGU_HEADS, CHUNK).transpose(1, 0, 2)
        sg["sgu_b"][l] = dbias[:, ::SGU_HEAD_DIM].T
        sg["pool_w"][l] = _diag_blocks(dpool_dense, len(POOL_WINDOWS))
        sg["pool_scale"][l] = dpool_scale[0]
        sg["conv_w"][l] = dconv8[0:3]
        sg["s5_c_re"][l] = _diag_blocks(dcre_d, S5_GROUPS).transpose(0, 2, 1)
        sg["s5_c_im"][l] = _diag_blocks(dcim_d, S5_GROUPS).transpose(0, 2, 1)
        sg["s5_d"][l] = dd[0]
        sg["s5_glu_w"][l] = dglu_w
        sg["s5_glu_b"][l] = dglu_b[0]
        d_are[l], d_aim[l] = dar8[0], dai8[0]
        d_bbre[l] = _diag_blocks(dbre_d, S5_GROUPS).transpose(0, 2, 1)
        d_bbim[l] = _diag_blocks(dbim_d, S5_GROUPS).transpose(0, 2, 1)
    grad_x = dx

    g_lre, g_lim, g_ldt, g_br, g_bi = _s5_prep_bwd(
        lre3, lim3, ldt3, br3, bi3,
        jnp.stack(d_are).reshape(lg, S5_STATE, 1), jnp.stack(d_aim).reshape(lg, S5_STATE, 1),
        jnp.stack(d_bbre).reshape(lg, S5_STATE, S5_GROUP_CH), jnp.stack(d_bbim).reshape(lg, S5_STATE, S5_GROUP_CH))
    small = {n: (jnp.stack(v) if isinstance(v, list) and v[0] is not None else v) for n, v in small_grads.items()}
    small["s5_lambda_re"] = g_lre.reshape(s5_lambda_re.shape)
    small["s5_lambda_im"] = g_lim.reshape(s5_lambda_im.shape)
    small["s5_log_dt"] = g_ldt.reshape(s5_log_dt.shape)
    small["s5_b_re"] = g_br.reshape(s5_b_re.shape)
    small["s5_b_im"] = g_bi.reshape(s5_b_im.shape)

    small_shapes = [(L, 3, W_GRP) if n == "conv_w" else (L, W_GRP, W_GRP) if n == "s5_glu_w" else W[n].shape
                    for n in SMALL]
    packed = _pack([small[n].reshape(s) for n, s in zip(SMALL, small_shapes)], PACK_COLS) + swap[1][0, 0]
    (pieces,) = _exchange([packed], True, "scatter_small")
    mine = _sum_parts(pieces, "sum_small")
    dcond = jnp.stack(dcond_rows).reshape(L * N_ADA, D_MODEL)
    summed, dcond_g = _exchange([mine, dcond], False, "gather_small_sums")
    small_sum = dict(zip(SMALL, _unpack(summed.reshape(-1), small_shapes)))
    small_sum["conv_w"] = lax.dynamic_slice_in_dim(small_sum["conv_w"], me * conv_cols, conv_cols, axis=2)
    small_sum["s5_glu_w"] = lax.dynamic_slice_in_dim(small_sum["s5_glu_w"], me * glu_rows, glu_rows, axis=1)

    handle, token = scatter_begin(swap[0], [summed, dcond_g])
    pending = (0, handle)
    dcond_all = dcond_g.reshape(N_DEV, L, N_ADA * D_MODEL).transpose(1, 0, 2)
    dcond_mine = lax.dynamic_slice_in_dim(dcond_all, me * ncol, ncol, axis=2) + token[0, 0]
    g_ada_w, g_ada_b = _cond_bwd(c_all.T, dcond_mine, dcond_all)

    grads, deltas, new_m, new_v = {}, {}, {}, {}
    out = _adamw(ada_w.reshape(L * D_MODEL, ncol), g_ada_w.reshape(1, L * D_MODEL, ncol),
                 m_ada_w.reshape(L * D_MODEL, ncol), v_ada_w.reshape(L * D_MODEL, ncol), "adamw_ada_w")
    grads["ada_w"], deltas["ada_w"], new_m["ada_w"], new_v["ada_w"] = (o.reshape(ada_w.shape) for o in out)
    small_g = dict(small_sum)
    small_g["ada_b"] = g_ada_b.reshape(ada_b.shape)
    shapes = [W[n].shape for n in small_names]
    rows = packed_w.shape[0] * packed_w.shape[1]
    out = _adamw(packed_w.reshape(rows, PACK_COLS),
                 _pack([small_g[n] for n in small_names], PACK_COLS).reshape(1, rows, PACK_COLS),
                 packed_m.reshape(rows, PACK_COLS), packed_v.reshape(rows, PACK_COLS), "adamw_small")
    for store, o in zip((grads, deltas, new_m, new_v), out):
        store.update(zip(small_names, _unpack(o.reshape(-1), shapes)))
    finish_scatter(pending, [out[0]] + [big_out[n][0] for n in BIG])
    for n in BIG:
        res = big_out[n]
        if n in TRANSPOSED:
            res = tuple(r.transpose(0, 2, 1) for r in res)
        grads[n], deltas[n], new_m[n], new_v[n] = res

    return (loss, grad_x[None], *[grads[n] for n in WEIGHTS], *[deltas[n] for n in WEIGHTS],
            *[new_m[n] for n in WEIGHTS], *[new_v[n] for n in WEIGHTS])
```

```python
import functools
import math

import jax
import jax.numpy as jnp
from jax import lax
from jax.experimental import pallas as pl
from jax.experimental.pallas import tpu as pltpu

F32 = jnp.float32
BF16 = jnp.bfloat16

D_MODEL = 1024
SEQ = 2048
DEPTH = 4
N_DEV = 8
W_GRP = 256
CHUNK = 128
SGU_HEADS = 4
SGU_HEAD_DIM = 64
POOL_WINDOWS = (2, 4, 8, 16)
POOL_GROUP_DIM = 64
S5_GROUPS = 16
S5_GROUP_CH = 16
S5_STATE = 64
S5_LANES = S5_GROUPS * S5_STATE
S5_BLOCK = 256
P_IN = 1792
D_FF = 2816
FF_PIECE = 2 * D_FF // N_DEV
N_FF_CHUNK = D_FF // FF_PIECE
N_ADA = 9
EPS = 1e-6
ADAM_LR = 0.001
ADAM_B1 = 0.9
ADAM_B2 = 0.999
ADAM_EPS = 1e-08
ADAM_WD = 0.01
ADAM_STEP = 10

SUBLANES = 8
LANES = 128
VMEM_LIMIT = 56 * 1024 * 1024
TOKEN_TILE = 512
ROW_SUBTILE = 256
HIGHEST = lax.Precision.HIGHEST
MESH_AXES = ("x", "y", "c")

_GELU_C = math.sqrt(2.0 / math.pi)
_GELU_A = 0.044715


def _params(*sem):
    return pltpu.CompilerParams(dimension_semantics=tuple(sem) if sem else None, vmem_limit_bytes=VMEM_LIMIT)


def _dot(a, b):
    return jnp.dot(a, b, preferred_element_type=F32)


def _dot_nt(a, b):
    return lax.dot_general(a, b, (((1,), (1,)), ((), ())), preferred_element_type=F32)


def _dot_tn(a, b):
    return lax.dot_general(a, b, (((0,), (0,)), ((), ())), preferred_element_type=F32)


def _dot_hi(a, b):
    return jnp.dot(a, b, preferred_element_type=F32, precision=HIGHEST)


def _sigmoid(x):
    return 1.0 / (1.0 + jnp.exp(-x))


def _gelu(x):
    return 0.5 * x * (1.0 + jnp.tanh(_GELU_C * (x + _GELU_A * x * x * x)))


def _gelu_grad(x):
    t = jnp.tanh(_GELU_C * (x + _GELU_A * x * x * x))
    return 0.5 * (1.0 + t) + 0.5 * x * (1.0 - t * t) * (_GELU_C * (1.0 + 3.0 * _GELU_A * x * x))


def _rms(x):
    r = lax.rsqrt(jnp.mean(x * x, axis=-1, keepdims=True) + EPS)
    return x * r, r


def _rms_bwd(xhat, r, dxhat):
    return r * (dxhat - xhat * jnp.mean(dxhat * xhat, axis=-1, keepdims=True))


def _sum0(x):
    return jnp.sum(x, axis=0, keepdims=True)


def _me():
    return 4 * lax.axis_index("x") + 2 * lax.axis_index("y") + lax.axis_index("c")


def _exchange(srcs, scatter, name):
    n = len(srcs)
    out_shapes = []
    for s in srcs:
        piece = s.shape[1:] if scatter else s.shape
        out_shapes.append(jax.ShapeDtypeStruct((N_DEV,) + tuple(piece), s.dtype))

    def body(*refs):
        ins, outs = refs[:n], refs[n:2 * n]
        send_sems, recv_sems, local_sems = refs[2 * n:]
        x, y, c = lax.axis_index("x"), lax.axis_index("y"), lax.axis_index("c")
        me = 4 * x + 2 * y + c

        def src_of(i, dev):
            return ins[i].at[dev] if scatter else ins[i]

        local = [pltpu.make_async_copy(src_of(i, me), outs[i].at[me], local_sems.at[i]) for i in range(n)]
        for cp in local:
            cp.start()
        sends, recvs = [], []
        for k in range(1, N_DEV):
            px = 1 - x if (k >> 2) & 1 else x
            py = 1 - y if (k >> 1) & 1 else y
            pc = 1 - c if k & 1 else c
            peer = 4 * px + 2 * py + pc
            for i in range(n):
                sends.append(pltpu.make_async_remote_copy(
                    src_ref=src_of(i, peer), dst_ref=outs[i].at[me],
                    send_sem=send_sems.at[k - 1, i], recv_sem=recv_sems.at[k - 1, i],
                    device_id=(px, py, pc), device_id_type=pl.DeviceIdType.MESH))
                recvs.append(pltpu.make_async_remote_copy(
                    src_ref=src_of(i, peer), dst_ref=outs[i].at[peer],
                    send_sem=send_sems.at[k - 1, i], recv_sem=recv_sems.at[k - 1, i],
                    device_id=(px, py, pc), device_id_type=pl.DeviceIdType.MESH))
        for cp in sends:
            cp.start()
        for cp in recvs:
            cp.wait_recv()
        for cp in sends:
            cp.wait_send()
        for cp in local:
            cp.wait()

    hbm = pl.BlockSpec(memory_space=pltpu.HBM)
    return pl.pallas_call(
        body, name=name, out_shape=out_shapes,
        in_specs=[hbm] * n, out_specs=[hbm] * n,
        scratch_shapes=[pltpu.SemaphoreType.DMA((N_DEV - 1, n)), pltpu.SemaphoreType.DMA((N_DEV - 1, n)),
                        pltpu.SemaphoreType.DMA((n,))],
    )(*srcs)


ALL_PEERS = tuple(range(1, N_DEV))
OTHER_CHIPS = (2, 4, 6)


def _peers(which):
    x, y, c = lax.axis_index("x"), lax.axis_index("y"), lax.axis_index("c")
    out = []
    for k in which:
        px = 1 - x if (k >> 2) & 1 else x
        py = 1 - y if (k >> 1) & 1 else y
        pc = 1 - c if k & 1 else c
        out.append(((px, py, pc), 4 * px + 2 * py + pc))
    return out


SIBLING_SWAP = "sibling"
N_CHIPS = 4


def _swap_copies(ins, lands, send_sems, recv_sems, with_recvs):
    x, y, c = lax.axis_index("x"), lax.axis_index("y"), lax.axis_index("c")
    sends, recvs = [], []
    for q in range(N_CHIPS):
        for i in range(len(ins)):
            sems = dict(send_sem=send_sems.at[q * len(ins) + i], recv_sem=recv_sems.at[q * len(ins) + i],
                        device_id=(x, y, 1 - c), device_id_type=pl.DeviceIdType.MESH)
            theirs, mine = 2 * q + 1 - c, 2 * q + c
            sends.append(pltpu.make_async_remote_copy(src_ref=ins[i].at[theirs], dst_ref=lands[i].at[theirs], **sems))
            if with_recvs:
                recvs.append(pltpu.make_async_remote_copy(src_ref=ins[i].at[theirs], dst_ref=lands[i].at[mine], **sems))
    return sends, recvs


def _split_copies(ins, lands, send_sems, recv_sems, scatter, with_recvs, which):
    if which == SIBLING_SWAP:
        return _swap_copies(ins, lands, send_sems, recv_sems, with_recvs)
    me = _me()
    sends, recvs = [], []
    for j, (dev, peer) in enumerate(_peers(which)):
        for i in range(len(ins)):
            src = ins[i].at[peer] if scatter else ins[i]
            slot = j * len(ins) + i
            sems = dict(send_sem=send_sems.at[slot], recv_sem=recv_sems.at[slot],
                        device_id=dev, device_id_type=pl.DeviceIdType.MESH)
            sends.append(pltpu.make_async_remote_copy(src_ref=src, dst_ref=lands[i].at[me], **sems))
            if with_recvs:
                recvs.append(pltpu.make_async_remote_copy(src_ref=src, dst_ref=lands[i].at[peer], **sems))
    return sends, recvs


_HBM = pl.BlockSpec(memory_space=pltpu.HBM)
_SEM = pl.BlockSpec(memory_space=pltpu.SEMAPHORE)
_EFFECT = pltpu.SideEffectType.DATAFLOW_SIDE_EFFECTING


def _place_own(srcs, scatter, name):
    n = len(srcs)
    halves = 2
    out_shapes, in_specs, out_specs = [], [], []
    for s in srcs:
        r, c = s.shape[-2:]
        out_shapes.append(jax.ShapeDtypeStruct((N_DEV, r, c), s.dtype))
        if scatter:
            in_specs.append(pl.BlockSpec((1, r // halves, c), lambda i, me: (me[0], i, 0)))
        else:
            in_specs.append(pl.BlockSpec((r // halves, c), lambda i, me: (i, 0)))
        out_specs.append(pl.BlockSpec((1, r // halves, c), lambda i, me: (me[0], i, 0)))

    def body(me_ref, *refs):
        for i in range(n):
            refs[n + i][0] = refs[i][0] if scatter else refs[i][...]

    return pl.pallas_call(
        body, name=name, out_shape=out_shapes,
        grid_spec=pltpu.PrefetchScalarGridSpec(num_scalar_prefetch=1, grid=(halves,), in_specs=in_specs,
                                               out_specs=out_specs),
        compiler_params=_params("arbitrary"),
    )(_me().reshape(1).astype(jnp.int32), *srcs)


def _core_index():
    return lax.axis_index("c").reshape(1).astype(jnp.int32)


def _presum(pieces, received, name):
    n = len(pieces)
    halves = 2
    specs = []
    for s in pieces:
        _, r, c = s.shape
        specs.append(pl.BlockSpec((1, r // halves, c), lambda q, i, core: (2 * q + core[0], i, 0)))

    def body(core_ref, *refs):
        for i in range(n):
            refs[2 * n + i][...] = (refs[i][...].astype(F32) + refs[n + i][...].astype(F32)).astype(BF16)

    return pl.pallas_call(
        body, name=name, out_shape=[jax.ShapeDtypeStruct(s.shape, BF16) for s in pieces],
        grid_spec=pltpu.PrefetchScalarGridSpec(num_scalar_prefetch=1, grid=(N_CHIPS, halves), in_specs=specs + specs,
                                               out_specs=specs),
        compiler_params=_params("arbitrary", "arbitrary"),
    )(_core_index(), *pieces, *received)


def _exchange_start(srcs, lands, after, scatter, name, which=ALL_PEERS):
    n = len(srcs)

    def body(*refs):
        ins, land_in = refs[:n], refs[n:2 * n]
        send_sems, recv_sems = refs[2 * n + 1], refs[2 * n + 2]
        token = refs[-1]
        sends, _ = _split_copies(ins, land_in, send_sems, recv_sems, scatter, False, which)
        for cp in sends:
            cp.start()
        token[...] = jnp.zeros_like(token)

    sem = pltpu.SemaphoreType.DMA(((N_CHIPS if which == SIBLING_SWAP else len(which)) * n,))
    out = pl.pallas_call(
        body, name=name,
        out_shape=(sem, sem, *[pltpu.HBM(s.shape, s.dtype) for s in srcs], *[pltpu.HBM(s.shape, s.dtype) for s in lands],
                   jax.ShapeDtypeStruct((SUBLANES, LANES), F32)),
        in_specs=[_HBM] * (2 * n) + [pl.BlockSpec(memory_space=pl.ANY)],
        out_specs=(_SEM, _SEM, *[_HBM] * (2 * n), pl.BlockSpec(memory_space=pltpu.VMEM)),
        input_output_aliases={i: 2 + i for i in range(2 * n)},
        compiler_params=pltpu.CompilerParams(has_side_effects=_EFFECT),
    )(*srcs, *lands, after)
    return (out[0], out[1], out[2:2 + n], out[2 + n:2 + 2 * n]), out[-1]


def _exchange_wait(handle, after, scatter, name, which=ALL_PEERS, with_srcs=False):
    send_sems, recv_sems, srcs, lands = handle
    n = len(srcs)
    after = list(after) if isinstance(after, (list, tuple)) else [after]

    def body(*refs):
        ins, land_in = refs[:n], refs[n:2 * n]
        sends, recvs = _split_copies(ins, land_in, refs[2 * n], refs[2 * n + 1], scatter, True, which)
        for cp in sends:
            cp.wait_send()
        for cp in recvs:
            cp.wait_recv()

    out = pl.pallas_call(
        body, name=name,
        out_shape=(*[pltpu.HBM(s.shape, s.dtype) for s in srcs], *[pltpu.HBM(s.shape, s.dtype) for s in lands]),
        in_specs=[_HBM] * (2 * n) + [_SEM, _SEM] + [pl.BlockSpec(memory_space=pl.ANY)] * len(after),
        out_specs=tuple([_HBM] * (2 * n)),
        input_output_aliases={i: i for i in range(2 * n)},
        compiler_params=pltpu.CompilerParams(has_side_effects=_EFFECT),
    )(*srcs, *lands, send_sems, recv_sems, *after)
    return (out[:n], out[n:]) if with_srcs else out[n:]


def _gather_sibling(lands, name):
    n = len(lands)
    chips = ((0, 0), (0, 1), (1, 0), (1, 1))

    def body(*refs):
        ins, outs = refs[:n], refs[n:2 * n]
        send_sems, recv_sems = refs[2 * n], refs[2 * n + 1]
        x, y, c = lax.axis_index("x"), lax.axis_index("y"), lax.axis_index("c")
        sends, recvs = [], []
        for j, (bx, by) in enumerate(chips):
            chip = 4 * (1 - x if bx else x) + 2 * (1 - y if by else y)
            for i in range(n):
                sems = dict(send_sem=send_sems.at[j * n + i], recv_sem=recv_sems.at[j * n + i],
                            device_id=(x, y, 1 - c), device_id_type=pl.DeviceIdType.MESH)
                sends.append(pltpu.make_async_remote_copy(src_ref=ins[i].at[chip + c], dst_ref=outs[i].at[chip + c], **sems))
                recvs.append(pltpu.make_async_remote_copy(src_ref=ins[i].at[chip + c], dst_ref=outs[i].at[chip + 1 - c],
                                                          **sems))
        for cp in sends:
            cp.start()
        for cp in recvs:
            cp.wait_recv()
        for cp in sends:
            cp.wait_send()

    return pl.pallas_call(
        body, name=name, out_shape=[jax.ShapeDtypeStruct(a.shape, a.dtype) for a in lands],
        in_specs=[_HBM] * n, out_specs=[_HBM] * n,
        scratch_shapes=[pltpu.SemaphoreType.DMA((len(chips) * n,)), pltpu.SemaphoreType.DMA((len(chips) * n,))],
        input_output_aliases={i: i for i in range(n)},
    )(*lands)


def _cond_fwd(c_all, ada_w, ada_b_mine):
    ncol = ada_w.shape[-1]

    def body(c_ref, w_ref, b_ref, o_ref):
        c = c_ref[...]
        ca = (c * _sigmoid(c)).astype(BF16)
        o_ref[0] = _dot(ca, w_ref[0].astype(BF16)) + b_ref[0]

    return pl.pallas_call(
        body, name="cond_fwd", grid=(DEPTH,),
        out_shape=jax.ShapeDtypeStruct((DEPTH, N_DEV, ncol), F32),
        in_specs=[pl.BlockSpec((N_DEV, D_MODEL), lambda l: (0, 0)),
                  pl.BlockSpec((1, D_MODEL, ncol), lambda l: (l, 0, 0)),
                  pl.BlockSpec((1, 1, ncol), lambda l: (l, 0, 0))],
        out_specs=pl.BlockSpec((1, N_DEV, ncol), lambda l: (l, 0, 0)),
        compiler_params=_params("arbitrary"),
    )(c_all, ada_w, ada_b_mine)


def _cond_bwd(c_all_t, dcond_mine, dcond_all):
    ncol = dcond_mine.shape[-1]
    nall = dcond_all.shape[-1]

    def body(ct_ref, d_ref, da_ref, gw_ref, gb_ref):
        ct = ct_ref[...]
        ct = ct * _sigmoid(ct)
        d = d_ref[0]
        acc = ct[:, 0:1] * d[0:1, :]
        for b in range(1, N_DEV):
            acc = acc + ct[:, b:b + 1] * d[b:b + 1, :]
        gw_ref[0] = acc
        gb_ref[0] = _sum0(da_ref[0])

    return pl.pallas_call(
        body, name="cond_bwd", grid=(DEPTH,),
        out_shape=(jax.ShapeDtypeStruct((DEPTH, D_MODEL, ncol), F32), jax.ShapeDtypeStruct((DEPTH, 1, nall), F32)),
        in_specs=[pl.BlockSpec((D_MODEL, N_DEV), lambda l: (0, 0)),
                  pl.BlockSpec((1, N_DEV, ncol), lambda l: (l, 0, 0)),
                  pl.BlockSpec((1, N_DEV, nall), lambda l: (l, 0, 0))],
        out_specs=(pl.BlockSpec((1, D_MODEL, ncol), lambda l: (l, 0, 0)),
                   pl.BlockSpec((1, 1, nall), lambda l: (l, 0, 0))),
        compiler_params=_params("arbitrary"),
    )(c_all_t, dcond_mine, dcond_all)


def _modnorm(x, g, shift, scale):
    xhat, r = _rms(x)
    return (xhat * g) * (1.0 + scale) + shift, xhat, r


def _modnorm_bwd(xhat, r, g, scale, dh):
    n = xhat * g
    dn = dh * (1.0 + scale)
    dx = _rms_bwd(xhat, r, dn * g)
    return dx, _sum0(dh), _sum0(dh * n), _sum0(dn * xhat)


def _row_spec(rows):
    return pl.BlockSpec((rows, D_MODEL), lambda *_: (0, 0))


def _ffn_fwd(x, cond3, g, w_in_g, w_out_g):
    tm = TOKEN_TILE
    last = N_FF_CHUNK - 1

    def body(x_ref, cond_ref, g_ref, wa_ref, wb_ref, wo_ref, xo_ref, f_ref, a_ref, b_ref, h_ref, h_scr, acc_scr):
        j = pl.program_id(1)

        @pl.when(j == 0)
        def _():
            h, _, _ = _modnorm(x_ref[...], g_ref[...], cond_ref[0:1, :], cond_ref[1:2, :])
            hb = h.astype(BF16)
            h_scr[...] = hb
            h_ref[...] = hb
            acc_scr[...] = jnp.zeros_like(acc_scr)

        wa, wb, wo = wa_ref[0], wb_ref[0], wo_ref[0]
        for r in range(tm // ROW_SUBTILE):
            rows = slice(r * ROW_SUBTILE, (r + 1) * ROW_SUBTILE)
            h = h_scr[rows, :]
            a = _dot_nt(h, wa)
            b = _dot_nt(h, wb)
            a_ref[0, rows, :] = a.astype(BF16)
            b_ref[0, rows, :] = b.astype(BF16)
            act = (a * _sigmoid(a)) * b
            acc_scr[rows, :] += _dot(act.astype(BF16), wo)

        @pl.when(j == last)
        def _():
            f = acc_scr[...]
            f_ref[...] = f
            xo_ref[...] = x_ref[...] + (0.5 * cond_ref[2:3, :]) * f

    tok = pl.BlockSpec((tm, D_MODEL), lambda i, j: (i, 0))
    chunk = pl.BlockSpec((1, tm, FF_PIECE), lambda i, j: (j, i, 0))
    chunk_shape = jax.ShapeDtypeStruct((N_FF_CHUNK, SEQ, FF_PIECE), BF16)
    return pl.pallas_call(
        body, name="ffn_fwd", grid=(SEQ // tm, N_FF_CHUNK),
        out_shape=(jax.ShapeDtypeStruct((SEQ, D_MODEL), F32), jax.ShapeDtypeStruct((SEQ, D_MODEL), F32),
                   chunk_shape, chunk_shape, jax.ShapeDtypeStruct((SEQ, D_MODEL), BF16)),
        in_specs=[tok, _row_spec(3), _row_spec(1),
                  pl.BlockSpec((1, FF_PIECE, D_MODEL), lambda i, j: (j, 0, 0)),
                  pl.BlockSpec((1, FF_PIECE, D_MODEL), lambda i, j: (j + N_FF_CHUNK, 0, 0)),
                  pl.BlockSpec((1, FF_PIECE, D_MODEL), lambda i, j: (j, 0, 0))],
        out_specs=(tok, tok, chunk, chunk, tok),
        scratch_shapes=[pltpu.VMEM((tm, D_MODEL), BF16), pltpu.VMEM((tm, D_MODEL), F32)],
        compiler_params=_params("arbitrary", "arbitrary"),
    )(x, cond3, g, w_in_g, w_in_g, w_out_g)


def _ffn_bwd(dy, x, f, a_sv, b_sv, cond3, g, w_in_g, w_out_g):
    tm = TOKEN_TILE
    last = N_FF_CHUNK - 1

    def body(dy_ref, x_ref, f_ref, a_ref, b_ref, cond_ref, g_ref, wa_ref, wb_ref, wo_ref,
             dx_ref, da_ref, db_ref, act_ref, do_ref, part_ref, do_scr, dh_scr):
        i, j = pl.program_id(0), pl.program_id(1)

        @pl.when(j == 0)
        def _():
            do = ((0.5 * cond_ref[2:3, :]) * dy_ref[...]).astype(BF16)
            do_scr[...] = do
            do_ref[...] = do
            dh_scr[...] = jnp.zeros_like(dh_scr)

        @pl.when((i == 0) & (j == 0))
        def _():
            part_ref[...] = jnp.zeros_like(part_ref)

        wa, wb, wo = wa_ref[0], wb_ref[0], wo_ref[0]
        for r in range(tm // ROW_SUBTILE):
            rows = slice(r * ROW_SUBTILE, (r + 1) * ROW_SUBTILE)
            do = do_scr[rows, :]
            a = a_ref[0, rows, :].astype(F32)
            b = b_ref[0, rows, :].astype(F32)
            dact = _dot_nt(do, wo)
            sig = _sigmoid(a)
            s = a * sig
            da = (dact * b * (sig * (1.0 + a * (1.0 - sig)))).astype(BF16)
            db = (dact * s).astype(BF16)
            da_ref[0, rows, :] = da
            db_ref[0, rows, :] = db
            act_ref[0, rows, :] = (s * b).astype(BF16)
            dh_scr[rows, :] += _dot(da, wa) + _dot(db, wb)

        @pl.when(j == last)
        def _():
            dyv = dy_ref[...]
            xhat, r = _rms(x_ref[...])
            dx, dshift, dscale, dg = _modnorm_bwd(xhat, r, g_ref[...], cond_ref[1:2, :], dh_scr[...])
            dx_ref[...] = dyv + dx
            part_ref[0:1, :] += dshift
            part_ref[1:2, :] += dscale
            part_ref[2:3, :] += _sum0(0.5 * dyv * f_ref[...])
            part_ref[3:4, :] += dg

    tok = pl.BlockSpec((tm, D_MODEL), lambda i, j: (i, 0))
    chunk = pl.BlockSpec((1, tm, FF_PIECE), lambda i, j: (j, i, 0))
    chunk_shape = jax.ShapeDtypeStruct((N_FF_CHUNK, SEQ, FF_PIECE), BF16)
    return pl.pallas_call(
        body, name="ffn_bwd", grid=(SEQ // tm, N_FF_CHUNK),
        out_shape=(jax.ShapeDtypeStruct((SEQ, D_MODEL), F32), chunk_shape, chunk_shape, chunk_shape,
                   jax.ShapeDtypeStruct((SEQ, D_MODEL), BF16), jax.ShapeDtypeStruct((SUBLANES, D_MODEL), F32)),
        in_specs=[tok, tok, tok, chunk, chunk, _row_spec(3), _row_spec(1),
                  pl.BlockSpec((1, FF_PIECE, D_MODEL), lambda i, j: (j, 0, 0)),
                  pl.BlockSpec((1, FF_PIECE, D_MODEL), lambda i, j: (j + N_FF_CHUNK, 0, 0)),
                  pl.BlockSpec((1, FF_PIECE, D_MODEL), lambda i, j: (j, 0, 0))],
        out_specs=(tok, chunk, chunk, chunk, tok, _row_spec(SUBLANES)),
        scratch_shapes=[pltpu.VMEM((tm, D_MODEL), BF16), pltpu.VMEM((tm, D_MODEL), F32)],
        compiler_params=_params("arbitrary", "arbitrary"),
    )(dy, x, f, a_sv, b_sv, cond3, g, w_in_g, w_in_g, w_out_g)


def _dw(lhs_a, rhs, lhs_b=None, name="dw"):
    pa, s, m = lhs_a.shape
    nn = rhs.shape[-1]
    pb = 0 if lhs_b is None else lhs_b.shape[0]
    two = lhs_b is not None

    def body(*refs):
        if two:
            a_ref, b_ref, r_ref, o_ref = refs
            p = pl.program_id(0)

            @pl.when(p < pa)
            def _():
                o_ref[0] = _dot_tn(a_ref[0], r_ref[...]).astype(BF16)

            @pl.when(p >= pa)
            def _():
                o_ref[0] = _dot_tn(b_ref[0], r_ref[...]).astype(BF16)
        else:
            a_ref, r_ref, o_ref = refs
            o_ref[0] = _dot_tn(a_ref[0], r_ref[...]).astype(BF16)

    if two:
        in_specs = [pl.BlockSpec((1, s, m), lambda p: (jnp.minimum(p, pa - 1), 0, 0)),
                    pl.BlockSpec((1, s, m), lambda p: (jnp.maximum(p - pa, 0), 0, 0))]
        args = (lhs_a, lhs_b, rhs)
    else:
        in_specs = [pl.BlockSpec((1, s, m), lambda p: (p, 0, 0))]
        args = (lhs_a, rhs)
    in_specs.append(pl.BlockSpec((s, nn), lambda p: (0, 0)))
    return pl.pallas_call(
        body, name=name, grid=(pa + pb,),
        out_shape=jax.ShapeDtypeStruct((pa + pb, m, nn), BF16),
        in_specs=in_specs, out_specs=pl.BlockSpec((1, m, nn), lambda p: (p, 0, 0)),
        compiler_params=_params("arbitrary"),
    )(*args)


def _mix_in_fwd(x, cond2, g, w):
    tm = TOKEN_TILE

    def body(x_ref, cond_ref, g_ref, w_ref, z_ref):
        h, _, _ = _modnorm(x_ref[...], g_ref[...], cond_ref[0:1, :], cond_ref[1:2, :])
        z_ref[...] = _dot_nt(h.astype(BF16), w_ref[...])

    return pl.pallas_call(
        body, name="mix_in_fwd", grid=(SEQ // tm,),
        out_shape=jax.ShapeDtypeStruct((SEQ, P_IN), F32),
        in_specs=[pl.BlockSpec((tm, D_MODEL), lambda i: (i, 0)), _row_spec(2), _row_spec(1),
                  pl.BlockSpec((P_IN, D_MODEL), lambda i: (0, 0))],
        out_specs=pl.BlockSpec((tm, P_IN), lambda i: (i, 0)),
        compiler_params=_params("arbitrary"),
    )(x, cond2, g, w)


MIX_SLABS = ((0, 2 * W_GRP), (2 * W_GRP, 3 * W_GRP), (3 * W_GRP, 6 * W_GRP), (6 * W_GRP, 7 * W_GRP))


def _mix_in_bwd(dzs, x, dy, cond2, g, w):
    tm = TOKEN_TILE

    def body(dza_ref, dzb_ref, dzc_ref, dzd_ref, x_ref, dy_ref, cond_ref, g_ref, w_ref, dx_ref, h_ref, dzo_ref, part_ref):
        i = pl.program_id(0)

        @pl.when(i == 0)
        def _():
            part_ref[...] = jnp.zeros_like(part_ref)

        h, xhat, r = _modnorm(x_ref[...], g_ref[...], cond_ref[0:1, :], cond_ref[1:2, :])
        h_ref[...] = h.astype(BF16)
        dh = None
        for (lo, hi), d_ref in zip(MIX_SLABS, (dza_ref, dzb_ref, dzc_ref, dzd_ref)):
            dzb = d_ref[...].astype(BF16)
            dzo_ref[:, lo:hi] = dzb
            t = _dot(dzb, w_ref[lo:hi, :])
            dh = t if dh is None else dh + t
        dx, dshift, dscale, dg = _modnorm_bwd(xhat, r, g_ref[...], cond_ref[1:2, :], dh)
        dx_ref[...] = dy_ref[...] + dx
        part_ref[0:1, :] += dshift
        part_ref[1:2, :] += dscale
        part_ref[2:3, :] += dg

    tok = pl.BlockSpec((tm, D_MODEL), lambda i: (i, 0))
    ztok = pl.BlockSpec((tm, P_IN), lambda i: (i, 0))
    slabs = [pl.BlockSpec((tm, hi - lo), lambda i: (i, 0)) for lo, hi in MIX_SLABS]
    return pl.pallas_call(
        body, name="mix_in_bwd", grid=(SEQ // tm,),
        out_shape=(jax.ShapeDtypeStruct((SEQ, D_MODEL), F32), jax.ShapeDtypeStruct((SEQ, D_MODEL), BF16),
                   jax.ShapeDtypeStruct((SEQ, P_IN), BF16), jax.ShapeDtypeStruct((SUBLANES, D_MODEL), F32)),
        in_specs=[*slabs, tok, tok, _row_spec(2), _row_spec(1), pl.BlockSpec((P_IN, D_MODEL), lambda i: (0, 0))],
        out_specs=(tok, tok, ztok, _row_spec(SUBLANES)),
        compiler_params=_params("arbitrary"),
    )(*dzs, x, dy, cond2, g, w)


def _group_norm(ys, g_ref):
    out = []
    for k, y in enumerate(ys):
        yhat, r = _rms(y)
        out.append((yhat, r, g_ref[:, k * W_GRP:(k + 1) * W_GRP]))
    return out


def _mix_out_fwd(x, ys, g, gate, w):
    tm = TOKEN_TILE

    def body(x_ref, ya_ref, yb_ref, yc_ref, yd_ref, g_ref, gate_ref, w_ref, xo_ref):
        acc = None
        for k, (yhat, _, gk) in enumerate(_group_norm([r[...] for r in (ya_ref, yb_ref, yc_ref, yd_ref)], g_ref)):
            t = _dot((yhat * gk).astype(BF16), w_ref[k * W_GRP:(k + 1) * W_GRP, :])
            acc = t if acc is None else acc + t
        xo_ref[...] = x_ref[...] + gate_ref[...] * acc

    tok = pl.BlockSpec((tm, D_MODEL), lambda i: (i, 0))
    ytok = pl.BlockSpec((tm, W_GRP), lambda i: (i, 0))
    return pl.pallas_call(
        body, name="mix_out_fwd", grid=(SEQ // tm,),
        out_shape=jax.ShapeDtypeStruct((SEQ, D_MODEL), F32),
        in_specs=[tok, ytok, ytok, ytok, ytok, _row_spec(1), _row_spec(1),
                  pl.BlockSpec((D_MODEL, D_MODEL), lambda i: (0, 0))],
        out_specs=tok, compiler_params=_params("arbitrary"),
    )(x, *ys, g, gate, w)


def _mix_out_bwd(dy, ys, g, gate, w):
    tm = TOKEN_TILE

    def body(dy_ref, ya_ref, yb_ref, yc_ref, yd_ref, g_ref, gate_ref, w_ref,
             da_ref, db_ref, dc_ref, dd_ref, yn_ref, dmo_ref, part_ref):
        i = pl.program_id(0)

        @pl.when(i == 0)
        def _():
            part_ref[...] = jnp.zeros_like(part_ref)

        dyv = dy_ref[...]
        dmo = (gate_ref[...] * dyv).astype(BF16)
        dmo_ref[...] = dmo
        dyn = _dot_nt(dmo, w_ref[...])
        norms = _group_norm([r[...] for r in (ya_ref, yb_ref, yc_ref, yd_ref)], g_ref)
        mo = None
        for k, ((yhat, r, gk), o_ref) in enumerate(zip(norms, (da_ref, db_ref, dc_ref, dd_ref))):
            sl = slice(k * W_GRP, (k + 1) * W_GRP)
            ynk = (yhat * gk).astype(BF16)
            yn_ref[:, sl] = ynk
            t = _dot(ynk, w_ref[sl, :])
            mo = t if mo is None else mo + t
            dk = dyn[:, sl]
            o_ref[...] = _rms_bwd(yhat, r, dk * gk)
            part_ref[1:2, sl] += _sum0(dk * yhat)
        part_ref[0:1, :] += _sum0(dyv * mo)

    tok = pl.BlockSpec((tm, D_MODEL), lambda i: (i, 0))
    ytok = pl.BlockSpec((tm, W_GRP), lambda i: (i, 0))
    ysh = jax.ShapeDtypeStruct((SEQ, W_GRP), F32)
    return pl.pallas_call(
        body, name="mix_out_bwd", grid=(SEQ // tm,),
        out_shape=(ysh, ysh, ysh, ysh, jax.ShapeDtypeStruct((SEQ, D_MODEL), BF16),
                   jax.ShapeDtypeStruct((SEQ, D_MODEL), BF16), jax.ShapeDtypeStruct((SUBLANES, D_MODEL), F32)),
        in_specs=[tok, ytok, ytok, ytok, ytok, _row_spec(1), _row_spec(1),
                  pl.BlockSpec((D_MODEL, D_MODEL), lambda i: (0, 0))],
        out_specs=(ytok, ytok, ytok, ytok, tok, tok, _row_spec(SUBLANES)),
        compiler_params=_params("arbitrary"),
    )(dy, *ys, g, gate, w)


def _shift_down(v, k, rows):
    return jnp.where(rows >= k, pltpu.roll(v, k, axis=0), 0.0)


def _shift_up(v, k, rows):
    n = v.shape[0]
    return jnp.where(rows < n - k, pltpu.roll(v, n - k, axis=0), 0.0)


def _zslab(width, index):
    return pl.BlockSpec((SEQ, width), lambda *_: (0, index))


def _full(shape):
    return pl.BlockSpec(shape, lambda *_: (0,) * len(shape))


def _head_avg():
    r = lax.broadcasted_iota(jnp.int32, (W_GRP, W_GRP), 0) // SGU_HEAD_DIM
    c = lax.broadcasted_iota(jnp.int32, (W_GRP, W_GRP), 1) // SGU_HEAD_DIM
    return jnp.where(r == c, 1.0 / SGU_HEAD_DIM, 0.0).astype(F32)


def _sgu_norm(za):
    z = _gelu(za)
    u, v = z[:, :W_GRP], z[:, W_GRP:]
    avg = _head_avg()
    vc = v - _dot_hi(v, avg)
    rstd = lax.rsqrt(_dot_hi(vc * vc, avg) + EPS)
    return u, vc * rstd, rstd


def _sgu_masked_w(w_ref):
    t = lax.broadcasted_iota(jnp.int32, (CHUNK, CHUNK), 0)
    s = lax.broadcasted_iota(jnp.int32, (CHUNK, CHUNK), 1)
    tril = t >= s
    return [jnp.where(tril, w_ref[:, h * CHUNK:(h + 1) * CHUNK], 0.0).astype(BF16) for h in range(SGU_HEADS)]


def _head_of_lane():
    return lax.broadcasted_iota(jnp.int32, (CHUNK, W_GRP), 1) // SGU_HEAD_DIM


def _sgu_fwd(z, w_cat, bias):
    def body(z_ref, w_ref, b_ref, y_ref, vn_scr, u_scr):
        u, vn, _ = _sgu_norm(z_ref[...])
        vn_scr[...] = vn.astype(BF16)
        u_scr[...] = u
        ws = _sgu_masked_w(w_ref)
        head = _head_of_lane()
        bias_v = b_ref[...]

        def chunk(n, carry):
            rows = pl.ds(pl.multiple_of(n * CHUNK, CHUNK), CHUNK)
            vb = vn_scr[rows, :]
            mixed = bias_v
            for h in range(SGU_HEADS):
                mixed = mixed + jnp.where(head == h, _dot(ws[h], vb), 0.0)
            y_ref[rows, :] = u_scr[rows, :] * mixed
            return carry

        lax.fori_loop(0, SEQ // CHUNK, chunk, 0)

    return pl.pallas_call(
        body, name="sgu_fwd", grid=(1,),
        out_shape=jax.ShapeDtypeStruct((SEQ, W_GRP), F32),
        in_specs=[_zslab(2 * W_GRP, 0), _full((CHUNK, SGU_HEADS * CHUNK)), _full((CHUNK, W_GRP))],
        out_specs=_full((SEQ, W_GRP)),
        scratch_shapes=[pltpu.VMEM((SEQ, W_GRP), BF16), pltpu.VMEM((SEQ, W_GRP), F32)],
        compiler_params=_params("arbitrary"),
    )(z, w_cat, bias)


def _sgu_bwd(z, dy, w_cat, bias):
    def body(z_ref, dy_ref, w_ref, b_ref, dz_ref, dw_ref, db_ref, vn_scr, u_scr, dvn_scr, du_scr):
        za = z_ref[...]
        u, vn, rstd = _sgu_norm(za)
        vn_scr[...] = vn.astype(BF16)
        u_scr[...] = u
        ws = _sgu_masked_w(w_ref)
        head = _head_of_lane()
        bias_v = b_ref[...]

        def chunk(n, carry):
            dws, dbias = carry
            rows = pl.ds(pl.multiple_of(n * CHUNK, CHUNK), CHUNK)
            vb = vn_scr[rows, :]
            mixed = bias_v
            for h in range(SGU_HEADS):
                mixed = mixed + jnp.where(head == h, _dot(ws[h], vb), 0.0)
            dyc = dy_ref[rows, :]
            du_scr[rows, :] = dyc * mixed
            dmixed = dyc * u_scr[rows, :]
            dvn = jnp.zeros((CHUNK, W_GRP), F32)
            new_dws = []
            for h in range(SGU_HEADS):
                dm = jnp.where(head == h, dmixed, 0.0).astype(BF16)
                new_dws.append(dws[h] + _dot_nt(dm, vb))
                dvn = dvn + _dot_tn(ws[h], dm)
            dvn_scr[rows, :] = dvn
            return tuple(new_dws), dbias + dmixed

        zero_w = tuple(jnp.zeros((CHUNK, CHUNK), F32) for _ in range(SGU_HEADS))
        dws, dbias = lax.fori_loop(0, SEQ // CHUNK, chunk, (zero_w, jnp.zeros((CHUNK, W_GRP), F32)))
        t = lax.broadcasted_iota(jnp.int32, (CHUNK, CHUNK), 0)
        s = lax.broadcasted_iota(jnp.int32, (CHUNK, CHUNK), 1)
        for h in range(SGU_HEADS):
            dw_ref[:, h * CHUNK:(h + 1) * CHUNK] = jnp.where(t >= s, dws[h], 0.0)
        avg = _head_avg()
        db_ref[...] = _dot_hi(dbias, avg) * float(SGU_HEAD_DIM)
        dvn = dvn_scr[...]
        dv = rstd * (dvn - _dot_hi(dvn, avg) - vn * _dot_hi(dvn * vn, avg))
        gg = _gelu_grad(za)
        dz_ref[:, :W_GRP] = gg[:, :W_GRP] * du_scr[...]
        dz_ref[:, W_GRP:] = gg[:, W_GRP:] * dv

    return pl.pallas_call(
        body, name="sgu_bwd", grid=(1,),
        out_shape=(jax.ShapeDtypeStruct((SEQ, 2 * W_GRP), F32), jax.ShapeDtypeStruct((CHUNK, SGU_HEADS * CHUNK), F32),
                   jax.ShapeDtypeStruct((CHUNK, W_GRP), F32)),
        in_specs=[_zslab(2 * W_GRP, 0), _full((SEQ, W_GRP)), _full((CHUNK, SGU_HEADS * CHUNK)), _full((CHUNK, W_GRP))],
        out_specs=(_full((SEQ, 2 * W_GRP)), _full((CHUNK, SGU_HEADS * CHUNK)), _full((CHUNK, W_GRP))),
        scratch_shapes=[pltpu.VMEM((SEQ, W_GRP), BF16), pltpu.VMEM((SEQ, W_GRP), F32),
                        pltpu.VMEM((SEQ, W_GRP), F32), pltpu.VMEM((SEQ, W_GRP), F32)],
        compiler_params=_params("arbitrary"),
    )(z, dy, w_cat, bias)


def _pool_window_of_lane(shape):
    grp = lax.broadcasted_iota(jnp.int32, shape, 1) // POOL_GROUP_DIM
    win = jnp.full(shape, POOL_WINDOWS[0], jnp.int32)
    for k in range(1, len(POOL_WINDOWS)):
        win = jnp.where(grp == k, POOL_WINDOWS[k], win)
    return grp, win


def _pool_select(levels, grp):
    out = levels[0]
    for k in range(1, len(levels)):
        out = jnp.where(grp == k, levels[k], out)
    return out


def _pool_p(z):
    shape = z.shape
    rows = lax.broadcasted_iota(jnp.int32, shape, 0)
    grp, win = _pool_window_of_lane(shape)
    levels, s, k = [], z, 1
    for _ in POOL_WINDOWS:
        s = s + _shift_down(s, k, rows)
        levels.append(s)
        k *= 2
    inv = 1.0 / jnp.minimum(rows + 1, win).astype(F32)
    return _pool_select(levels, grp) * inv - z, inv, rows, grp


def _pool_fwd(z, w_bd, scale):
    def body(z_ref, w_ref, s_ref, y_ref):
        p, _, _, _ = _pool_p(z_ref[...])
        y_ref[...] = _dot(p.astype(BF16), w_ref[...]) * s_ref[...]

    return pl.pallas_call(
        body, name="pool_fwd", grid=(1,),
        out_shape=jax.ShapeDtypeStruct((SEQ, W_GRP), F32),
        in_specs=[_zslab(W_GRP, 2), _full((W_GRP, W_GRP)), _full((1, W_GRP))],
        out_specs=_full((SEQ, W_GRP)), compiler_params=_params("arbitrary"),
    )(z, w_bd, scale)


def _pool_bwd(z, dy, w_bd, scale):
    def body(z_ref, dy_ref, w_ref, s_ref, dz_ref, dw_ref, ds_ref):
        p, inv, rows, grp = _pool_p(z_ref[...])
        pb = p.astype(BF16)
        dyv = dy_ref[...]
        ds_ref[...] = _sum0(dyv * _dot(pb, w_ref[...]))
        dpre = (dyv * s_ref[...]).astype(BF16)
        dw_ref[...] = _dot_tn(pb, dpre)
        dp = _dot_nt(dpre, w_ref[...])
        q = dp * inv
        levels, s, k = [], q, 1
        for _ in POOL_WINDOWS:
            s = s + _shift_up(s, k, rows)
            levels.append(s)
            k *= 2
        dz_ref[...] = _pool_select(levels, grp) - dp

    return pl.pallas_call(
        body, name="pool_bwd", grid=(1,),
        out_shape=(jax.ShapeDtypeStruct((SEQ, W_GRP), F32), jax.ShapeDtypeStruct((W_GRP, W_GRP), F32),
                   jax.ShapeDtypeStruct((1, W_GRP), F32)),
        in_specs=[_zslab(W_GRP, 2), _full((SEQ, W_GRP)), _full((W_GRP, W_GRP)), _full((1, W_GRP))],
        out_specs=(_full((SEQ, W_GRP)), _full((W_GRP, W_GRP)), _full((1, W_GRP))),
        compiler_params=_params("arbitrary"),
    )(z, dy, w_bd, scale)


def _conv_fwd(z, w):
    def body(z_ref, w_ref, y_ref):
        zc = z_ref[...]
        bg, cg, xh = zc[:, :W_GRP], zc[:, W_GRP:2 * W_GRP], zc[:, 2 * W_GRP:]
        rows = lax.broadcasted_iota(jnp.int32, (SEQ, W_GRP), 0)
        y = cg * xh
        conv = w_ref[0:1, :] * _shift_down(y, 2, rows) + w_ref[1:2, :] * _shift_down(y, 1, rows) + w_ref[2:3, :] * y
        y_ref[...] = bg * conv

    return pl.pallas_call(
        body, name="conv_fwd", grid=(1,),
        out_shape=jax.ShapeDtypeStruct((SEQ, W_GRP), F32),
        in_specs=[_zslab(3 * W_GRP, 1), _full((3, W_GRP))],
        out_specs=_full((SEQ, W_GRP)), compiler_params=_params("arbitrary"),
    )(z, w)


def _conv_bwd(z, dy, w):
    def body(z_ref, dy_ref, w_ref, dz_ref, dw_ref):
        zc = z_ref[...]
        bg, cg, xh = zc[:, :W_GRP], zc[:, W_GRP:2 * W_GRP], zc[:, 2 * W_GRP:]
        rows = lax.broadcasted_iota(jnp.int32, (SEQ, W_GRP), 0)
        y = cg * xh
        y2, y1 = _shift_down(y, 2, rows), _shift_down(y, 1, rows)
        conv = w_ref[0:1, :] * y2 + w_ref[1:2, :] * y1 + w_ref[2:3, :] * y
        dyv = dy_ref[...]
        dconv = dyv * bg
        dw_ref[...] = jnp.zeros_like(dw_ref)
        dw_ref[0:1, :] = _sum0(dconv * y2)
        dw_ref[1:2, :] = _sum0(dconv * y1)
        dw_ref[2:3, :] = _sum0(dconv * y)
        dyy = (w_ref[0:1, :] * _shift_up(dconv, 2, rows) + w_ref[1:2, :] * _shift_up(dconv, 1, rows)
               + w_ref[2:3, :] * dconv)
        dz_ref[:, :W_GRP] = dyv * conv
        dz_ref[:, W_GRP:2 * W_GRP] = dyy * xh
        dz_ref[:, 2 * W_GRP:] = dyy * cg

    return pl.pallas_call(
        body, name="conv_bwd", grid=(1,),
        out_shape=(jax.ShapeDtypeStruct((SEQ, 3 * W_GRP), F32), jax.ShapeDtypeStruct((SUBLANES, W_GRP), F32)),
        in_specs=[_zslab(3 * W_GRP, 1), _full((SEQ, W_GRP)), _full((3, W_GRP))],
        out_specs=(_full((SEQ, 3 * W_GRP)), _full((SUBLANES, W_GRP))),
        compiler_params=_params("arbitrary"),
    )(z, dy, w)


def _s5_disc(lre, lim, ldt, br, bi):
    dt = jnp.exp(ldt)
    mag = jnp.exp(lre * dt)
    ang = lim * dt
    a_re, a_im = mag * jnp.cos(ang), mag * jnp.sin(ang)
    nr, ni = a_re - 1.0, a_im
    den = lre * lre + lim * lim
    k_re = (nr * lre + ni * lim) / den
    k_im = (ni * lre - nr * lim) / den
    return a_re, a_im, k_re * br - k_im * bi, k_re * bi + k_im * br


def _s5_prep_fwd(lre, lim, ldt, br, bi):
    def body(lre_ref, lim_ref, ldt_ref, br_ref, bi_ref, ar_ref, ai_ref, bbr_ref, bbi_ref):
        ar, ai, bbr, bbi = _s5_disc(lre_ref[...], lim_ref[...], ldt_ref[...], br_ref[...], bi_ref[...])
        ar_ref[...] = ar
        ai_ref[...] = ai
        bbr_ref[...] = bbr
        bbi_ref[...] = bbi

    return pl.pallas_call(
        body, name="s5_prep_fwd",
        out_shape=(jax.ShapeDtypeStruct(lre.shape, F32), jax.ShapeDtypeStruct(lre.shape, F32),
                   jax.ShapeDtypeStruct(br.shape, F32), jax.ShapeDtypeStruct(br.shape, F32)),
        compiler_params=_params(),
    )(lre, lim, ldt, br, bi)


def _s5_prep_bwd(lre, lim, ldt, br, bi, dar, dai, dbbr, dbbi):
    def body(lre_ref, lim_ref, ldt_ref, br_ref, bi_ref, dar_ref, dai_ref, dbbr_ref, dbbi_ref,
             o_lre, o_lim, o_ldt, o_br, o_bi):
        _, pull = jax.vjp(_s5_disc, lre_ref[...], lim_ref[...], ldt_ref[...], br_ref[...], bi_ref[...])
        g = pull((dar_ref[...], dai_ref[...], dbbr_ref[...], dbbi_ref[...]))
        for o, v in zip((o_lre, o_lim, o_ldt, o_br, o_bi), g):
            o[...] = v

    return pl.pallas_call(
        body, name="s5_prep_bwd",
        out_shape=tuple(jax.ShapeDtypeStruct(a.shape, F32) for a in (lre, lim, ldt, br, bi)),
        compiler_params=_params(),
    )(lre, lim, ldt, br, bi, dar, dai, dbbr, dbbi)


def _cmul(ar, ai, br, bi):
    return ar * br - ai * bi, ar * bi + ai * br


def _s5_tile_consts(ar, ai, reverse):
    if reverse:
        ai = -ai
    shape = (SUBLANES, S5_BLOCK)
    row = lax.broadcasted_iota(jnp.int32, shape, 0)
    a1 = (jnp.broadcast_to(ar, shape), jnp.broadcast_to(ai, shape))
    a2 = _cmul(*a1, *a1)
    a4 = _cmul(*a2, *a2)
    a8 = _cmul(*a4, *a4)
    steps = []
    for s, (pr, pi) in ((1, a1), (2, a2), (4, a4)):
        keep = (row < SUBLANES - s) if reverse else (row >= s)
        steps.append((s, jnp.where(keep, pr, 0.0), jnp.where(keep, pi, 0.0)))
    e = (SUBLANES - row) if reverse else (row + 1)
    pr, pi = jnp.ones(shape, F32), jnp.zeros(shape, F32)
    for bit, (qr, qi) in ((1, a1), (2, a2), (4, a4), (8, a8)):
        nr, ni = _cmul(pr, pi, qr, qi)
        hit = (e & bit) != 0
        pr, pi = jnp.where(hit, nr, pr), jnp.where(hit, ni, pi)
    return steps, pr, pi


def _s5_tile(xr, xi, steps, reverse):
    for s, pr, pi in steps:
        sh = SUBLANES - s if reverse else s
        sr, si = pltpu.roll(xr, sh, axis=0), pltpu.roll(xi, sh, axis=0)
        xr, xi = xr + pr * sr - pi * si, xi + pr * si + pi * sr
    return xr, xi


N_TILES = SEQ // SUBLANES


def _s5_fwd(z, b_re, b_im, c_re, c_im, a_re, a_im, d, glu_w, glu_b):
    nblk = S5_LANES // S5_BLOCK

    def body(u_ref, br_ref, bi_ref, cr_ref, ci_ref, ar_ref, ai_ref, d_ref, gw_ref, gb_ref,
             y_ref, y0_ref, xr_ref, xi_ref, ub_scr, acc_scr):
        jb = pl.program_id(0)

        @pl.when(jb == 0)
        def _():
            ub_scr[...] = u_ref[...].astype(BF16)
            acc_scr[...] = jnp.zeros_like(acc_scr)

        ub = ub_scr[...]
        xr_ref[...] = _dot(ub, br_ref[...])
        xi_ref[...] = _dot(ub, bi_ref[...])
        steps, pr, pi = _s5_tile_consts(ar_ref[...], ai_ref[...], False)

        def tile(t, carry):
            cr, ci = carry
            rows = pl.ds(pl.multiple_of(t * SUBLANES, SUBLANES), SUBLANES)
            xr, xi = _s5_tile(xr_ref[rows, :], xi_ref[rows, :], steps, False)
            xr, xi = xr + pr * cr - pi * ci, xi + pr * ci + pi * cr
            xr_ref[rows, :] = xr
            xi_ref[rows, :] = xi
            return xr[SUBLANES - 1:, :], xi[SUBLANES - 1:, :]

        zero = jnp.zeros((1, S5_BLOCK), F32)
        lax.fori_loop(0, N_TILES, tile, (zero, zero), unroll=2)
        acc_scr[...] += (_dot(xr_ref[...].astype(BF16), cr_ref[...]) - _dot(xi_ref[...].astype(BF16), ci_ref[...]))

        @pl.when(jb == nblk - 1)
        def _():
            y0 = acc_scr[...] + d_ref[...] * u_ref[...]
            y0_ref[...] = y0
            y1 = _gelu(y0)
            y_ref[...] = y1 * _sigmoid(_dot(y1.astype(BF16), gw_ref[...]) + gb_ref[...])

    lane_blk = pl.BlockSpec((SEQ, S5_BLOCK), lambda j: (0, j))
    return pl.pallas_call(
        body, name="s5_fwd", grid=(nblk,),
        out_shape=(jax.ShapeDtypeStruct((SEQ, W_GRP), F32), jax.ShapeDtypeStruct((SEQ, W_GRP), F32),
                   jax.ShapeDtypeStruct((SEQ, S5_LANES), F32), jax.ShapeDtypeStruct((SEQ, S5_LANES), F32)),
        in_specs=[_zslab(W_GRP, 6),
                  pl.BlockSpec((W_GRP, S5_BLOCK), lambda j: (0, j)), pl.BlockSpec((W_GRP, S5_BLOCK), lambda j: (0, j)),
                  pl.BlockSpec((S5_BLOCK, W_GRP), lambda j: (j, 0)), pl.BlockSpec((S5_BLOCK, W_GRP), lambda j: (j, 0)),
                  pl.BlockSpec((1, S5_BLOCK), lambda j: (0, j)), pl.BlockSpec((1, S5_BLOCK), lambda j: (0, j)),
                  _full((1, W_GRP)), _full((W_GRP, W_GRP)), _full((1, W_GRP))],
        out_specs=(_full((SEQ, W_GRP)), _full((SEQ, W_GRP)), lane_blk, lane_blk),
        scratch_shapes=[pltpu.VMEM((SEQ, W_GRP), BF16), pltpu.VMEM((SEQ, W_GRP), F32)],
        compiler_params=_params("arbitrary"),
    )(z, b_re, b_im, c_re, c_im, a_re, a_im, d, glu_w, glu_b)


def _s5_bwd(z, y0, dy, xr, xi, b_re, b_im, c_re, c_im, a_re, a_im, d, glu_w, glu_b):
    nblk = S5_LANES // S5_BLOCK

    def body(u_ref, y0_ref, dy_ref, xr_ref, xi_ref, br_ref, bi_ref, cr_ref, ci_ref, ar_ref, ai_ref,
             d_ref, gw_ref, gb_ref,
             du_ref, dbr_ref, dbi_ref, dcr_ref, dci_ref, dar_ref, dai_ref, dd_ref, dgw_ref, dgb_ref,
             ub_scr, dy0_scr, du_scr, lr_scr, li_scr):
        jb = pl.program_id(0)

        @pl.when(jb == 0)
        def _():
            u = u_ref[...]
            y0v = y0_ref[...]
            y1 = _gelu(y0v)
            y1b = y1.astype(BF16)
            sg = _sigmoid(_dot(y1b, gw_ref[...]) + gb_ref[...])
            dyv = dy_ref[...]
            dpre = dyv * y1 * sg * (1.0 - sg)
            dpb = dpre.astype(BF16)
            dgw_ref[...] = _dot_tn(y1b, dpb)
            dgb_ref[...] = _sum0(dpre)
            dy1 = dyv * sg + _dot_nt(dpb, gw_ref[...])
            dy0 = dy1 * _gelu_grad(y0v)
            dd_ref[...] = _sum0(dy0 * u)
            du_scr[...] = dy0 * d_ref[...]
            dy0_scr[...] = dy0.astype(BF16)
            ub_scr[...] = u.astype(BF16)

        dy0b = dy0_scr[...]
        lr_scr[...] = _dot_nt(dy0b, cr_ref[...])
        li_scr[...] = -_dot_nt(dy0b, ci_ref[...])
        dcr_ref[...] = _dot_tn(xr_ref[...].astype(BF16), dy0b)
        dci_ref[...] = -_dot_tn(xi_ref[...].astype(BF16), dy0b)
        steps, pr, pi = _s5_tile_consts(ar_ref[...], ai_ref[...], True)
        row = lax.broadcasted_iota(jnp.int32, (SUBLANES, S5_BLOCK), 0)

        def tile(k, carry):
            cr, ci, accr, acci = carry
            t = N_TILES - 1 - k
            rows = pl.ds(pl.multiple_of(t * SUBLANES, SUBLANES), SUBLANES)
            lr, li = _s5_tile(lr_scr[rows, :], li_scr[rows, :], steps, True)
            lr, li = lr + pr * cr - pi * ci, li + pr * ci + pi * cr
            lr_scr[rows, :] = lr
            li_scr[rows, :] = li
            prev = pl.ds(pl.multiple_of(jnp.maximum(t - 1, 0) * SUBLANES, SUBLANES), SUBLANES)
            live = jnp.where(t > 0, 1.0, 0.0)
            xpr = jnp.where(row == 0, pltpu.roll(xr_ref[prev, :], 1, axis=0) * live, pltpu.roll(xr_ref[rows, :], 1, axis=0))
            xpi = jnp.where(row == 0, pltpu.roll(xi_ref[prev, :], 1, axis=0) * live, pltpu.roll(xi_ref[rows, :], 1, axis=0))
            accr = accr + lr * xpr + li * xpi
            acci = acci + li * xpr - lr * xpi
            return lr[0:1, :], li[0:1, :], accr, acci

        zero = jnp.zeros((1, S5_BLOCK), F32)
        zt = jnp.zeros((SUBLANES, S5_BLOCK), F32)
        _, _, accr, acci = lax.fori_loop(0, N_TILES, tile, (zero, zero, zt, zt), unroll=2)
        dar_ref[...] = jnp.zeros_like(dar_ref)
        dai_ref[...] = jnp.zeros_like(dai_ref)
        dar_ref[0:1, :] = _sum0(accr)
        dai_ref[0:1, :] = _sum0(acci)
        lrb, lib = lr_scr[...].astype(BF16), li_scr[...].astype(BF16)
        ub = ub_scr[...]
        dbr_ref[...] = _dot_tn(ub, lrb)
        dbi_ref[...] = _dot_tn(ub, lib)
        du_scr[...] += _dot_nt(lrb, br_ref[...]) + _dot_nt(lib, bi_ref[...])

        @pl.when(jb == nblk - 1)
        def _():
            du_ref[...] = du_scr[...]

    lane_blk = pl.BlockSpec((SEQ, S5_BLOCK), lambda j: (0, j))
    bspec = pl.BlockSpec((W_GRP, S5_BLOCK), lambda j: (0, j))
    cspec = pl.BlockSpec((S5_BLOCK, W_GRP), lambda j: (j, 0))
    aspec = pl.BlockSpec((1, S5_BLOCK), lambda j: (0, j))
    a8spec = pl.BlockSpec((SUBLANES, S5_BLOCK), lambda j: (0, j))
    sd = jax.ShapeDtypeStruct
    return pl.pallas_call(
        body, name="s5_bwd", grid=(nblk,),
        out_shape=(sd((SEQ, W_GRP), F32), sd((W_GRP, S5_LANES), F32), sd((W_GRP, S5_LANES), F32),
                   sd((S5_LANES, W_GRP), F32), sd((S5_LANES, W_GRP), F32),
                   sd((SUBLANES, S5_LANES), F32), sd((SUBLANES, S5_LANES), F32),
                   sd((1, W_GRP), F32), sd((W_GRP, W_GRP), F32), sd((1, W_GRP), F32)),
        in_specs=[_zslab(W_GRP, 6), _full((SEQ, W_GRP)), _full((SEQ, W_GRP)), lane_blk, lane_blk,
                  bspec, bspec, cspec, cspec, aspec, aspec,
                  _full((1, W_GRP)), _full((W_GRP, W_GRP)), _full((1, W_GRP))],
        out_specs=(_full((SEQ, W_GRP)), bspec, bspec, cspec, cspec, a8spec, a8spec,
                   _full((1, W_GRP)), _full((W_GRP, W_GRP)), _full((1, W_GRP))),
        scratch_shapes=[pltpu.VMEM((SEQ, W_GRP), BF16), pltpu.VMEM((SEQ, W_GRP), BF16), pltpu.VMEM((SEQ, W_GRP), F32),
                        pltpu.VMEM((SEQ, S5_BLOCK), F32), pltpu.VMEM((SEQ, S5_BLOCK), F32)],
        compiler_params=_params("arbitrary"),
    )(z, y0, dy, xr, xi, b_re, b_im, c_re, c_im, a_re, a_im, d, glu_w, glu_b)


def _head(x, g, target):
    tm = TOKEN_TILE
    n = SEQ // tm

    def body(x_ref, g_ref, t_ref, dx_ref, st_ref, acc_scr):
        i = pl.program_id(0)

        @pl.when(i == 0)
        def _():
            acc_scr[...] = jnp.zeros_like(acc_scr)

        xhat, r = _rms(x_ref[...])
        gv = g_ref[...]
        err = xhat * gv - t_ref[...]
        dyv = err * (1.0 / D_MODEL)
        dx_ref[...] = _rms_bwd(xhat, r, dyv * gv)
        acc_scr[0:1, :] += _sum0(err * err)
        acc_scr[1:2, :] += _sum0(dyv * xhat)

        @pl.when(i == n - 1)
        def _():
            st_ref[...] = acc_scr[...]
            tot = jnp.sum(acc_scr[0:1, :], axis=-1, keepdims=True) * (0.5 / D_MODEL)
            st_ref[0:1, :] = jnp.broadcast_to(tot, (1, D_MODEL))

    tok = pl.BlockSpec((tm, D_MODEL), lambda i: (i, 0))
    return pl.pallas_call(
        body, name="head", grid=(n,),
        out_shape=(jax.ShapeDtypeStruct((SEQ, D_MODEL), F32), jax.ShapeDtypeStruct((SUBLANES, D_MODEL), F32)),
        in_specs=[tok, _row_spec(1), tok], out_specs=(tok, _row_spec(SUBLANES)),
        scratch_shapes=[pltpu.VMEM((SUBLANES, D_MODEL), F32)],
        compiler_params=_params("arbitrary"),
    )(x, g, target)


def _adamw(w, gparts, m, v, name):
    r, c = w.shape
    npart = gparts.shape[0]
    tr = r
    for cand in (512, 256, 128, 64, 32, 16):
        if r % cand == 0 and r > cand:
            tr = cand
            break
    b1c = 1.0 - ADAM_B1 ** ADAM_STEP
    b2c = 1.0 - ADAM_B2 ** ADAM_STEP

    def body(w_ref, g_ref, m_ref, v_ref, go_ref, d_ref, mo_ref, vo_ref):
        g = g_ref[0].astype(F32)
        for k in range(1, npart):
            g = g + g_ref[k].astype(F32)
        mn = ADAM_B1 * m_ref[...] + (1.0 - ADAM_B1) * g
        vn = ADAM_B2 * v_ref[...] + (1.0 - ADAM_B2) * (g * g)
        m_hat = mn / b1c
        v_hat = vn / b2c
        go_ref[...] = g
        d_ref[...] = -ADAM_LR * (m_hat / (jnp.sqrt(v_hat) + ADAM_EPS) + ADAM_WD * w_ref[...])
        mo_ref[...] = mn
        vo_ref[...] = vn

    blk = pl.BlockSpec((tr, c), lambda i: (i, 0))
    sh = jax.ShapeDtypeStruct((r, c), F32)
    return pl.pallas_call(
        body, name=name, grid=(r // tr,),
        out_shape=(sh, sh, sh, sh),
        in_specs=[blk, pl.BlockSpec((npart, tr, c), lambda i: (0, i, 0)), blk, blk],
        out_specs=(blk, blk, blk, blk), compiler_params=_params("arbitrary"),
    )(w, gparts, m, v)


def _adamw_layer(l, w, gparts, m, v, prev, name):
    _, r, c = w.shape
    tr = max(t for t in range(16, 513, 16) if r % t == 0)
    b1c = 1.0 - ADAM_B1 ** ADAM_STEP
    b2c = 1.0 - ADAM_B2 ** ADAM_STEP
    nprev = 0 if prev is None else 4

    def body(core_ref, *refs):
        w_ref, m_ref, v_ref = refs[:3]
        g_refs = refs[3:3 + N_CHIPS]
        go_ref, d_ref, mo_ref, vo_ref = refs[3 + N_CHIPS + nprev:]
        g = g_refs[0][0].astype(F32)
        for g_ref in g_refs[1:]:
            g = g + g_ref[0].astype(F32)
        mn = ADAM_B1 * m_ref[0] + (1.0 - ADAM_B1) * g
        vn = ADAM_B2 * v_ref[0] + (1.0 - ADAM_B2) * (g * g)
        go_ref[0] = g
        d_ref[0] = -ADAM_LR * ((mn / b1c) / (jnp.sqrt(vn / b2c) + ADAM_EPS) + ADAM_WD * w_ref[0])
        mo_ref[0] = mn
        vo_ref[0] = vn

    blk = pl.BlockSpec((1, tr, c), lambda i, core: (l, i, 0))
    slots = [pl.BlockSpec((1, tr, c), functools.partial(lambda i, core, q: (2 * q + core[0], i, 0), q=q))
             for q in range(N_CHIPS)]
    sh = jax.ShapeDtypeStruct(w.shape, F32)
    keep = [pl.BlockSpec(memory_space=pl.ANY)] * nprev
    return pl.pallas_call(
        body, name=name, out_shape=(sh, sh, sh, sh),
        grid_spec=pltpu.PrefetchScalarGridSpec(num_scalar_prefetch=1, grid=(r // tr,),
                                               in_specs=[blk, blk, blk, *slots, *keep], out_specs=(blk, blk, blk, blk)),
        input_output_aliases={4 + N_CHIPS + k: k for k in range(nprev)},
        compiler_params=_params("arbitrary"),
    )(_core_index(), w, m, v, *([gparts] * N_CHIPS), *(prev or ()))


def _sum_parts(parts, name):
    n, r, c = parts.shape

    def body(p_ref, o_ref):
        acc = p_ref[0]
        for k in range(1, n):
            acc = acc + p_ref[k]
        o_ref[...] = acc

    return pl.pallas_call(
        body, name=name, out_shape=jax.ShapeDtypeStruct((r, c), F32), compiler_params=_params(),
    )(parts)


def _block_diag(blocks):
    g, a, b = blocks.shape
    eye = jnp.eye(g, dtype=blocks.dtype)
    return (blocks[:, :, None, :] * eye[:, None, :, None]).reshape(g * a, g * b)


def _diag_blocks(dense, g):
    a, b = dense.shape[0] // g, dense.shape[1] // g
    d4 = dense.reshape(g, a, g, b)
    eye = jnp.eye(g, dtype=dense.dtype)
    return jnp.sum(d4 * eye[:, None, :, None], axis=2)


def _pack(parts, cols):
    flat = jnp.concatenate([p.reshape(-1) for p in parts])
    unit = N_DEV * SUBLANES * cols
    total = -(-flat.shape[0] // unit) * unit
    flat = jnp.pad(flat, (0, total - flat.shape[0]))
    return flat.reshape(N_DEV, total // (N_DEV * cols), cols)


def _unpack(flat, shapes):
    out, pos = [], 0
    for s in shapes:
        n = math.prod(s)
        out.append(flat[pos:pos + n].reshape(s))
        pos += n
    return out


SMALL = ("norm1_g", "norm2_g", "sgu_w", "sgu_b", "pool_w", "pool_scale", "conv_w", "s5_lambda_re", "s5_lambda_im",
         "s5_b_re", "s5_b_im", "s5_c_re", "s5_c_im", "s5_d", "s5_log_dt", "s5_glu_w", "s5_glu_b", "mix_norm_g",
         "norm3_g", "final_norm_g")
BIG = ("ffn1_w_in", "ffn1_w_out", "w_mix_in", "w_mix_out", "ffn2_w_in", "ffn2_w_out")
TRANSPOSED = ("ffn1_w_in", "w_mix_in", "ffn2_w_in")
WEIGHTS = ("ada_w", "ada_b", "norm1_g", "ffn1_w_in", "ffn1_w_out", "norm2_g", "w_mix_in", "sgu_w", "sgu_b", "pool_w",
           "pool_scale", "conv_w", "s5_lambda_re", "s5_lambda_im", "s5_b_re", "s5_b_im", "s5_c_re", "s5_c_im", "s5_d",
           "s5_log_dt", "s5_glu_w", "s5_glu_b", "mix_norm_g", "w_mix_out", "norm3_g", "ffn2_w_in", "ffn2_w_out",
           "final_norm_g")
PACK_COLS = 1024


def kernel(x, c, ada_w, ada_b, norm1_g, ffn1_w_in, ffn1_w_out, norm2_g, w_mix_in, sgu_w, sgu_b, pool_w, pool_scale, conv_w, s5_lambda_re, s5_lambda_im, s5_b_re, s5_b_im, s5_c_re, s5_c_im, s5_d, s5_log_dt, s5_glu_w, s5_glu_b, mix_norm_g, w_mix_out, norm3_g, ffn2_w_in, ffn2_w_out, final_norm_g, loss_target, m_ada_w, m_ada_b, m_norm1_g, m_ffn1_w_in, m_ffn1_w_out, m_norm2_g, m_w_mix_in, m_sgu_w, m_sgu_b, m_pool_w, m_pool_scale, m_conv_w, m_s5_lambda_re, m_s5_lambda_im, m_s5_b_re, m_s5_b_im, m_s5_c_re, m_s5_c_im, m_s5_d, m_s5_log_dt, m_s5_glu_w, m_s5_glu_b, m_mix_norm_g, m_w_mix_out, m_norm3_g, m_ffn2_w_in, m_ffn2_w_out, m_final_norm_g, v_ada_w, v_ada_b, v_norm1_g, v_ffn1_w_in, v_ffn1_w_out, v_norm2_g, v_w_mix_in, v_sgu_w, v_sgu_b, v_pool_w, v_pool_scale, v_conv_w, v_s5_lambda_re, v_s5_lambda_im, v_s5_b_re, v_s5_b_im, v_s5_c_re, v_s5_c_im, v_s5_d, v_s5_log_dt, v_s5_glu_w, v_s5_glu_b, v_mix_norm_g, v_w_mix_out, v_norm3_g, v_ffn2_w_in, v_ffn2_w_out, v_final_norm_g):
    args = dict(locals())
    W = {n: args[n] for n in WEIGHTS}
    M = {n: args["m_" + n] for n in WEIGHTS}
    V = {n: args["v_" + n] for n in WEIGHTS}
    me = _me()
    L = DEPTH
    x0 = x[0]
    target = loss_target[0]

    conv_cols = conv_w.shape[-1]
    glu_rows = s5_glu_w.shape[1]
    c_g, conv_g, glu_g = _exchange(
        [c.reshape(SUBLANES, LANES), conv_w.reshape(L * 3, conv_cols), s5_glu_w.reshape(L * glu_rows, W_GRP)],
        False, "gather_small")
    c_all = c_g.reshape(N_DEV, D_MODEL)
    conv_full = conv_g.reshape(N_DEV, L, 3, conv_cols).transpose(1, 2, 0, 3).reshape(L, 3, W_GRP)
    glu_full = glu_g.reshape(N_DEV, L, glu_rows, W_GRP).transpose(1, 0, 2, 3).reshape(L, W_GRP, W_GRP)

    ncol = ada_w.shape[-1]
    ada_b_mine = lax.dynamic_slice_in_dim(ada_b, me * ncol, ncol, axis=1).reshape(L, 1, ncol)
    cond_part = _cond_fwd(c_all, ada_w, ada_b_mine)
    (cond_g,) = _exchange([cond_part.reshape(L * N_DEV, ncol)], False, "gather_cond")
    cond_g = cond_g.reshape(N_DEV, L, N_DEV, ncol)
    cond_mine = lax.dynamic_index_in_dim(cond_g, me, axis=2, keepdims=False)
    cond = cond_mine.transpose(1, 0, 2).reshape(L, N_ADA, D_MODEL)

    lg = L * S5_GROUPS
    lre3 = s5_lambda_re.reshape(lg, S5_STATE, 1)
    lim3 = s5_lambda_im.reshape(lg, S5_STATE, 1)
    ldt3 = s5_log_dt.reshape(lg, 1, 1)
    br3 = s5_b_re.reshape(lg, S5_STATE, S5_GROUP_CH)
    bi3 = s5_b_im.reshape(lg, S5_STATE, S5_GROUP_CH)

    def b_mat(bb3, l):
        return _block_diag(bb3.reshape(L, S5_GROUPS, S5_STATE, S5_GROUP_CH)[l].transpose(0, 2, 1)).astype(BF16)

    def c_mat(cw, l):
        return _block_diag(cw[l].transpose(0, 2, 1)).astype(BF16)

    for n in TRANSPOSED:
        W[n], M[n], V[n] = (a.transpose(0, 2, 1) for a in (W[n], M[n], V[n]))

    gathered_shape = {"ffn1_w_out": (N_FF_CHUNK, FF_PIECE, D_MODEL), "ffn2_w_out": (N_FF_CHUNK, FF_PIECE, D_MODEL),
                      "w_mix_in": (P_IN, D_MODEL), "w_mix_out": (D_MODEL, D_MODEL)}

    def gather_start(l, after, names=BIG):
        srcs = [W[n][l].astype(BF16) for n in names]
        lands = _place_own(srcs, False, "gather_weights_own")
        return _exchange_start(srcs, lands, after, False, "gather_weights_start", OTHER_CHIPS)

    def gather_finish(handle, after, names=BIG):
        g = _exchange_wait(handle, after, False, "gather_weights_wait", OTHER_CHIPS)
        out = dict(zip(names, _gather_sibling(g, "gather_weights_sibling")))
        return {n: (a.reshape(gathered_shape[n]) if n in gathered_shape else a) for n, a in out.items()}

    saved = []
    xc = x0
    first_ffn = BIG[:2]
    handle, token = gather_start(0, cond, first_ffn)
    handle_rest, token = gather_start(0, token, BIG[2:])

    zero = token[0, 0]
    a_re3, a_im3, bb_re3, bb_im3 = _s5_prep_fwd(lre3 + zero, lim3, ldt3, br3, bi3)
    a_re = a_re3.reshape(L, 1, S5_LANES)
    a_im = a_im3.reshape(L, 1, S5_LANES)
    sgu_w_t, pool_w_t, c_re_t, c_im_t = (a + zero for a in (sgu_w, pool_w, s5_c_re, s5_c_im))

    def mixer_consts(l):
        w_cat = sgu_w_t[l].transpose(1, 0, 2).reshape(CHUNK, SGU_HEADS * CHUNK)
        bias = jnp.repeat(sgu_b[l].T, SGU_HEAD_DIM, axis=1)
        return dict(
            w_cat=w_cat, bias=bias, pool_bd=_block_diag(pool_w_t[l]).astype(BF16), pool_scale=pool_scale[l][None],
            conv=conv_full[l], b_re=b_mat(bb_re3, l), b_im=b_mat(bb_im3, l), c_re=c_mat(c_re_t, l),
            c_im=c_mat(c_im_t, l), a_re=a_re[l], a_im=a_im[l], d=s5_d[l][None], glu_w=glu_full[l].astype(BF16),
            glu_b=s5_glu_b[l][None])

    mcs = [mixer_consts(l) for l in range(L)]
    small_names = SMALL + ("ada_b",)
    first_small = W[small_names[0]] + zero
    packed_w = _pack([first_small] + [W[n] for n in small_names[1:]], PACK_COLS)
    packed_m = _pack([M[n] + zero if n == small_names[0] else M[n] for n in small_names], PACK_COLS)
    packed_v = _pack([V[n] + zero if n == small_names[0] else V[n] for n in small_names], PACK_COLS)
    wl = gather_finish(handle, [x0] + [a for mc in mcs for a in mc.values()], first_ffn)
    for l in range(L):
        cl = cond[l]
        if 0 < l < L - 1:
            handle, token = gather_start(l + 1, wl["ffn1_w_in"])
            cl = cl + token[0, 0]
        mc = mcs[l]
        x_a = xc
        x_b, *ffn1_kept = _ffn_fwd(x_a, cl[0:3], norm1_g[l][None], wl["ffn1_w_in"], wl["ffn1_w_out"])
        if l == 0:
            wl = {**wl, **gather_finish(handle_rest, [x_b, packed_w, packed_m, packed_v], BIG[2:])}
            handle, token = gather_start(1, wl["w_mix_in"])
            cl = cl + token[0, 0]
        z = _mix_in_fwd(x_b, cl[3:5], norm2_g[l][None], wl["w_mix_in"])
        ya = _sgu_fwd(z, mc["w_cat"], mc["bias"])
        yb = _pool_fwd(z, mc["pool_bd"], mc["pool_scale"])
        yc = _conv_fwd(z, mc["conv"])
        yd, y0, sxr, sxi = _s5_fwd(z, mc["b_re"], mc["b_im"], mc["c_re"], mc["c_im"], mc["a_re"], mc["a_im"],
                                   mc["d"], mc["glu_w"], mc["glu_b"])
        ys = (ya, yb, yc, yd)
        x_c = _mix_out_fwd(x_b, ys, mix_norm_g[l][None], cl[5:6], wl["w_mix_out"])
        x_d, *ffn2_kept = _ffn_fwd(x_c, cl[6:9], norm3_g[l][None], wl["ffn2_w_in"], wl["ffn2_w_out"])
        saved.append(dict(wl=wl, mc=mc, x_a=x_a, x_b=x_b, x_c=x_c, ffn1=ffn1_kept, ffn2=ffn2_kept, z=z, ys=ys, y0=y0,
                          xr=sxr, xi=sxi))
        xc = x_d
        if l + 1 < L:
            wl = gather_finish(handle, x_d)

    dx, stats = _head(xc, final_norm_g[None], target)
    loss = lax.psum(stats[0, 0], MESH_AXES)

    small_grads = {n: [None] * L for n in SMALL if n != "final_norm_g"}
    small_grads["final_norm_g"] = stats[1]
    big_out = {n: None for n in BIG}
    pending = None

    def swap_begin(pieces, after):
        bufs = _place_own(pieces, True, "scatter_grads_own")
        return _exchange_start(pieces, bufs, after, True, "swap_grads_start", SIBLING_SWAP)

    def scatter_begin(swap_handle, after):
        mine, theirs = _exchange_wait(swap_handle, after, True, "swap_grads_wait", SIBLING_SWAP, with_srcs=True)
        sums = _presum(mine, theirs, "presum_grads")
        lands = _place_own(sums, True, "scatter_grads_own")
        return _exchange_start(sums, lands, sums[0], True, "scatter_grads_start", OTHER_CHIPS)

    def finish_scatter(pend, after):
        layer, hnd = pend
        recv = _exchange_wait(hnd, after, True, "scatter_grads_wait", OTHER_CHIPS)
        for n, r in zip(BIG, recv):
            big_out[n] = _adamw_layer(layer, W[n], r, M[n], V[n], big_out[n], "adamw_" + n)
        return recv[0]

    dcond_rows = [None] * L
    d_are, d_aim, d_bbre, d_bbim = [None] * L, [None] * L, [None] * L, [None] * L
    swap = None
    for l in reversed(range(L)):
        sv = saved[l]
        wl, mc, cl = sv["wl"], sv["mc"], cond[l]
        if swap is not None:
            cl = cl + swap[1][0, 0]
        f2, a2, b2, hb = sv["ffn2"]
        dx, da, db, act, dob, part3 = _ffn_bwd(dx, sv["x_c"], f2, a2, b2, cl[6:9], norm3_g[l][None],
                                               wl["ffn2_w_in"], wl["ffn2_w_out"])
        if swap is not None:
            handle, token = scatter_begin(swap[0], dx)
            pending = (l + 1, handle)
            cl = cl + token[0, 0]
        g_ffn2_in = _dw(da, hb, db, name="dw_ffn_in")
        g_ffn2_out = _dw(act, dob, name="dw_ffn_out").reshape(N_DEV, D_FF // N_DEV, D_MODEL)
        dya, dyb, dyc, dyd, ynb, dmob, part_mo = _mix_out_bwd(dx, sv["ys"], mix_norm_g[l][None], cl[5:6],
                                                              wl["w_mix_out"])
        g_mix_out = _dw(ynb[None], dmob, name="dw_mix_out").reshape(N_DEV, D_MODEL // N_DEV, D_MODEL)
        z = sv["z"]
        dza, dw_cat, dbias = _sgu_bwd(z, dya, mc["w_cat"], mc["bias"])
        dzb, dpool_dense, dpool_scale = _pool_bwd(z, dyb, mc["pool_bd"], mc["pool_scale"])
        dzc, dconv8 = _conv_bwd(z, dyc, mc["conv"])
        (dzd, dbre_d, dbim_d, dcre_d, dcim_d, dar8, dai8, dd, dglu_w, dglu_b) = _s5_bwd(
            z, sv["y0"], dyd, sv["xr"], sv["xi"], mc["b_re"], mc["b_im"], mc["c_re"], mc["c_im"],
            mc["a_re"], mc["a_im"], mc["d"], mc["glu_w"], mc["glu_b"])
        dx, h2b, dzbf, part2 = _mix_in_bwd((dza, dzb, dzc, dzd), sv["x_b"], dx, cl[3:5], norm2_g[l][None],
                                           wl["w_mix_in"])
        g_mix_in = _dw(dzbf[None], h2b, name="dw_mix_in").reshape(N_DEV, P_IN // N_DEV, D_MODEL)
        f1, a1, b1, hb = sv["ffn1"]
        dx, da, db, act, dob, part1 = _ffn_bwd(dx, sv["x_a"], f1, a1, b1, cl[0:3], norm1_g[l][None],
                                               wl["ffn1_w_in"], wl["ffn1_w_out"])
        g_ffn1_in = _dw(da, hb, db, name="dw_ffn_in")
        g_ffn1_out = _dw(act, dob, name="dw_ffn_out").reshape(N_DEV, D_FF // N_DEV, D_MODEL)
        last = finish_scatter(pending, g_ffn1_in) if pending is not None else g_ffn1_in
        swap = swap_begin([g_ffn1_in, g_ffn1_out, g_mix_in, g_mix_out, g_ffn2_in, g_ffn2_out], last)
        dcond_rows[l] = jnp.concatenate([part1[0:3], part2[0:2], part_mo[0:1], part3[0:3]], axis=0)
        sg = small_grads
        sg["norm1_g"][l] = part1[3]
        sg["norm2_g"][l] = part2[2]
        sg["norm3_g"][l] = part3[3]
        sg["mix_norm_g"][l] = part_mo[1]
        sg["sgu_w"][l] = dw_cat.reshape(CHUNK, SGU_HEADS, CHUNK).transpose(1, 0, 2)
        sg["sgu_b"][l] = dbias[:, ::SGU_HEAD_DIM].T
        sg["pool_w"][l] = _diag_blocks(dpool_dense, len(POOL_WINDOWS))
        sg["pool_scale"][l] = dpool_scale[0]
        sg["conv_w"][l] = dconv8[0:3]
        sg["s5_c_re"][l] = _diag_blocks(dcre_d, S5_GROUPS).transpose(0, 2, 1)
        sg["s5_c_im"][l] = _diag_blocks(dcim_d, S5_GROUPS).transpose(0, 2, 1)
        sg["s5_d"][l] = dd[0]
        sg["s5_glu_w"][l] = dglu_w
        sg["s5_glu_b"][l] = dglu_b[0]
        d_are[l], d_aim[l] = dar8[0], dai8[0]
        d_bbre[l] = _diag_blocks(dbre_d, S5_GROUPS).transpose(0, 2, 1)
        d_bbim[l] = _diag_blocks(dbim_d, S5_GROUPS).transpose(0, 2, 1)
    grad_x = dx

    g_lre, g_lim, g_ldt, g_br, g_bi = _s5_prep_bwd(
        lre3, lim3, ldt3, br3, bi3,
        jnp.stack(d_are).reshape(lg, S5_STATE, 1), jnp.stack(d_aim).reshape(lg, S5_STATE, 1),
        jnp.stack(d_bbre).reshape(lg, S5_STATE, S5_GROUP_CH), jnp.stack(d_bbim).reshape(lg, S5_STATE, S5_GROUP_CH))
    small = {n: (jnp.stack(v) if isinstance(v, list) and v[0] is not None else v) for n, v in small_grads.items()}
    small["s5_lambda_re"] = g_lre.reshape(s5_lambda_re.shape)
    small["s5_lambda_im"] = g_lim.reshape(s5_lambda_im.shape)
    small["s5_log_dt"] = g_ldt.reshape(s5_log_dt.shape)
    small["s5_b_re"] = g_br.reshape(s5_b_re.shape)
    small["s5_b_im"] = g_bi.reshape(s5_b_im.shape)

    small_shapes = [(L, 3, W_GRP) if n == "conv_w" else (L, W_GRP, W_GRP) if n == "s5_glu_w" else W[n].shape
                    for n in SMALL]
    packed = _pack([small[n].reshape(s) for n, s in zip(SMALL, small_shapes)], PACK_COLS) + swap[1][0, 0]
    (pieces,) = _exchange([packed], True, "scatter_small")
    mine = _sum_parts(pieces, "sum_small")
    dcond = jnp.stack(dcond_rows).reshape(L * N_ADA, D_MODEL)
    summed, dcond_g = _exchange([mine, dcond], False, "gather_small_sums")
    small_sum = dict(zip(SMALL, _unpack(summed.reshape(-1), small_shapes)))
    small_sum["conv_w"] = lax.dynamic_slice_in_dim(small_sum["conv_w"], me * conv_cols, conv_cols, axis=2)
    small_sum["s5_glu_w"] = lax.dynamic_slice_in_dim(small_sum["s5_glu_w"], me * glu_rows, glu_rows, axis=1)

    handle, token = scatter_begin(swap[0], [summed, dcond_g])
    pending = (0, handle)
    dcond_all = dcond_g.reshape(N_DEV, L, N_ADA * D_MODEL).transpose(1, 0, 2)
    dcond_mine = lax.dynamic_slice_in_dim(dcond_all, me * ncol, ncol, axis=2) + token[0, 0]
    g_ada_w, g_ada_b = _cond_bwd(c_all.T, dcond_mine, dcond_all)

    grads, deltas, new_m, new_v = {}, {}, {}, {}
    out = _adamw(ada_w.reshape(L * D_MODEL, ncol), g_ada_w.reshape(1, L * D_MODEL, ncol),
                 m_ada_w.reshape(L * D_MODEL, ncol), v_ada_w.reshape(L * D_MODEL, ncol), "adamw_ada_w")
    grads["ada_w"], deltas["ada_w"], new_m["ada_w"], new_v["ada_w"] = (o.reshape(ada_w.shape) for o in out)
    small_g = dict(small_sum)
    small_g["ada_b"] = g_ada_b.reshape(ada_b.shape)
    shapes = [W[n].shape for n in small_names]
    rows = packed_w.shape[0] * packed_w.shape[1]
    out = _adamw(packed_w.reshape(rows, PACK_COLS),
                 _pack([small_g[n] for n in small_names], PACK_COLS).reshape(1, rows, PACK_COLS),
                 packed_m.reshape(rows, PACK_COLS), packed_v.reshape(rows, PACK_COLS), "adamw_small")
    for store, o in zip((grads, deltas, new_m, new_v), out):
        store.update(zip(small_names, _unpack(o.reshape(-1), shapes)))
    done = [store[n] for store in (grads, deltas, new_m, new_v) for n in small_names + ("ada_w",)]
    finish_scatter(pending, done + [big_out[n][0] for n in BIG])
    for n in BIG:
        res = big_out[n]
        if n in TRANSPOSED:
            res = tuple(r.transpose(0, 2, 1) for r in res)
        grads[n], deltas[n], new_m[n], new_v[n] = res

    return (loss, grad_x[None], *[grads[n] for n in WEIGHTS], *[deltas[n] for n in WEIGHTS],
            *[new_m[n] for n in WEIGHTS], *[new_v[n] for n in WEIGHTS])
```

```python
import functools
import math

import jax
import jax.numpy as jnp
from jax import lax
from jax.experimental import pallas as pl
from jax.experimental.pallas import tpu as pltpu

F32 = jnp.float32
BF16 = jnp.bfloat16

D_MODEL = 1024
SEQ = 2048
DEPTH = 4
N_DEV = 8
W_GRP = 256
CHUNK = 128
SGU_HEADS = 4
SGU_HEAD_DIM = 64
POOL_WINDOWS = (2, 4, 8, 16)
POOL_GROUP_DIM = 64
S5_GROUPS = 16
S5_GROUP_CH = 16
S5_STATE = 64
S5_LANES = S5_GROUPS * S5_STATE
S5_BLOCK = 256
P_IN = 1792
D_FF = 2816
FF_PIECE = 2 * D_FF // N_DEV
N_FF_CHUNK = D_FF // FF_PIECE
N_ADA = 9
EPS = 1e-6
ADAM_LR = 0.001
ADAM_B1 = 0.9
ADAM_B2 = 0.999
ADAM_EPS = 1e-08
ADAM_WD = 0.01
ADAM_STEP = 10

SUBLANES = 8
LANES = 128
VMEM_LIMIT = 56 * 1024 * 1024
TOKEN_TILE = 512
ROW_SUBTILE = 256
HIGHEST = lax.Precision.HIGHEST
MESH_AXES = ("x", "y", "c")

_GELU_C = math.sqrt(2.0 / math.pi)
_GELU_A = 0.044715


def _params(*sem):
    return pltpu.CompilerParams(dimension_semantics=tuple(sem) if sem else None, vmem_limit_bytes=VMEM_LIMIT)


def _dot(a, b):
    return jnp.dot(a, b, preferred_element_type=F32)


def _dot_nt(a, b):
    return lax.dot_general(a, b, (((1,), (1,)), ((), ())), preferred_element_type=F32)


def _dot_tn(a, b):
    return lax.dot_general(a, b, (((0,), (0,)), ((), ())), preferred_element_type=F32)


def _dot_hi(a, b):
    return jnp.dot(a, b, preferred_element_type=F32, precision=HIGHEST)


def _sigmoid(x):
    return 1.0 / (1.0 + jnp.exp(-x))


def _gelu(x):
    return 0.5 * x * (1.0 + jnp.tanh(_GELU_C * (x + _GELU_A * x * x * x)))


def _gelu_grad(x):
    t = jnp.tanh(_GELU_C * (x + _GELU_A * x * x * x))
    return 0.5 * (1.0 + t) + 0.5 * x * (1.0 - t * t) * (_GELU_C * (1.0 + 3.0 * _GELU_A * x * x))


def _rms(x):
    r = lax.rsqrt(jnp.mean(x * x, axis=-1, keepdims=True) + EPS)
    return x * r, r


def _rms_bwd(xhat, r, dxhat):
    return r * (dxhat - xhat * jnp.mean(dxhat * xhat, axis=-1, keepdims=True))


def _sum0(x):
    return jnp.sum(x, axis=0, keepdims=True)


def _me():
    return 4 * lax.axis_index("x") + 2 * lax.axis_index("y") + lax.axis_index("c")


def _exchange(srcs, scatter, name):
    n = len(srcs)
    out_shapes = []
    for s in srcs:
        piece = s.shape[1:] if scatter else s.shape
        out_shapes.append(jax.ShapeDtypeStruct((N_DEV,) + tuple(piece), s.dtype))

    def body(*refs):
        ins, outs = refs[:n], refs[n:2 * n]
        send_sems, recv_sems, local_sems = refs[2 * n:]
        x, y, c = lax.axis_index("x"), lax.axis_index("y"), lax.axis_index("c")
        me = 4 * x + 2 * y + c

        def src_of(i, dev):
            return ins[i].at[dev] if scatter else ins[i]

        local = [pltpu.make_async_copy(src_of(i, me), outs[i].at[me], local_sems.at[i]) for i in range(n)]
        for cp in local:
            cp.start()
        sends, recvs = [], []
        for k in range(1, N_DEV):
            px = 1 - x if (k >> 2) & 1 else x
            py = 1 - y if (k >> 1) & 1 else y
            pc = 1 - c if k & 1 else c
            peer = 4 * px + 2 * py + pc
            for i in range(n):
                sends.append(pltpu.make_async_remote_copy(
                    src_ref=src_of(i, peer), dst_ref=outs[i].at[me],
                    send_sem=send_sems.at[k - 1, i], recv_sem=recv_sems.at[k - 1, i],
                    device_id=(px, py, pc), device_id_type=pl.DeviceIdType.MESH))
                recvs.append(pltpu.make_async_remote_copy(
                    src_ref=src_of(i, peer), dst_ref=outs[i].at[peer],
                    send_sem=send_sems.at[k - 1, i], recv_sem=recv_sems.at[k - 1, i],
                    device_id=(px, py, pc), device_id_type=pl.DeviceIdType.MESH))
        for cp in sends:
            cp.start()
        for cp in recvs:
            cp.wait_recv()
        for cp in sends:
            cp.wait_send()
        for cp in local:
            cp.wait()

    hbm = pl.BlockSpec(memory_space=pltpu.HBM)
    return pl.pallas_call(
        body, name=name, out_shape=out_shapes,
        in_specs=[hbm] * n, out_specs=[hbm] * n,
        scratch_shapes=[pltpu.SemaphoreType.DMA((N_DEV - 1, n)), pltpu.SemaphoreType.DMA((N_DEV - 1, n)),
                        pltpu.SemaphoreType.DMA((n,))],
    )(*srcs)


ALL_PEERS = tuple(range(1, N_DEV))
OTHER_CHIPS = (2, 4, 6)


def _peers(which):
    x, y, c = lax.axis_index("x"), lax.axis_index("y"), lax.axis_index("c")
    out = []
    for k in which:
        px = 1 - x if (k >> 2) & 1 else x
        py = 1 - y if (k >> 1) & 1 else y
        pc = 1 - c if k & 1 else c
        out.append(((px, py, pc), 4 * px + 2 * py + pc))
    return out


SIBLING_SWAP = "sibling"
N_CHIPS = 4


def _swap_copies(ins, lands, send_sems, recv_sems, with_recvs):
    x, y, c = lax.axis_index("x"), lax.axis_index("y"), lax.axis_index("c")
    sends, recvs = [], []
    for q in range(N_CHIPS):
        for i in range(len(ins)):
            sems = dict(send_sem=send_sems.at[q * len(ins) + i], recv_sem=recv_sems.at[q * len(ins) + i],
                        device_id=(x, y, 1 - c), device_id_type=pl.DeviceIdType.MESH)
            theirs, mine = 2 * q + 1 - c, 2 * q + c
            sends.append(pltpu.make_async_remote_copy(src_ref=ins[i].at[theirs], dst_ref=lands[i].at[theirs], **sems))
            if with_recvs:
                recvs.append(pltpu.make_async_remote_copy(src_ref=ins[i].at[theirs], dst_ref=lands[i].at[mine], **sems))
    return sends, recvs


def _split_copies(ins, lands, send_sems, recv_sems, scatter, with_recvs, which):
    if which == SIBLING_SWAP:
        return _swap_copies(ins, lands, send_sems, recv_sems, with_recvs)
    me = _me()
    sends, recvs = [], []
    for j, (dev, peer) in enumerate(_peers(which)):
        for i in range(len(ins)):
            src = ins[i].at[peer] if scatter else ins[i]
            slot = j * len(ins) + i
            sems = dict(send_sem=send_sems.at[slot], recv_sem=recv_sems.at[slot],
                        device_id=dev, device_id_type=pl.DeviceIdType.MESH)
            sends.append(pltpu.make_async_remote_copy(src_ref=src, dst_ref=lands[i].at[me], **sems))
            if with_recvs:
                recvs.append(pltpu.make_async_remote_copy(src_ref=src, dst_ref=lands[i].at[peer], **sems))
    return sends, recvs


_HBM = pl.BlockSpec(memory_space=pltpu.HBM)
_SEM = pl.BlockSpec(memory_space=pltpu.SEMAPHORE)
_EFFECT = pltpu.SideEffectType.DATAFLOW_SIDE_EFFECTING


def _place_own(srcs, scatter, name):
    n = len(srcs)
    halves = 2
    out_shapes, in_specs, out_specs = [], [], []
    for s in srcs:
        r, c = s.shape[-2:]
        out_shapes.append(jax.ShapeDtypeStruct((N_DEV, r, c), s.dtype))
        if scatter:
            in_specs.append(pl.BlockSpec((1, r // halves, c), lambda i, me: (me[0], i, 0)))
        else:
            in_specs.append(pl.BlockSpec((r // halves, c), lambda i, me: (i, 0)))
        out_specs.append(pl.BlockSpec((1, r // halves, c), lambda i, me: (me[0], i, 0)))

    def body(me_ref, *refs):
        for i in range(n):
            refs[n + i][0] = refs[i][0] if scatter else refs[i][...]

    return pl.pallas_call(
        body, name=name, out_shape=out_shapes,
        grid_spec=pltpu.PrefetchScalarGridSpec(num_scalar_prefetch=1, grid=(halves,), in_specs=in_specs,
                                               out_specs=out_specs),
        compiler_params=_params("arbitrary"),
    )(_me().reshape(1).astype(jnp.int32), *srcs)


def _core_index():
    return lax.axis_index("c").reshape(1).astype(jnp.int32)


def _presum(pieces, received, name):
    n = len(pieces)
    halves = 2
    specs = []
    for s in pieces:
        _, r, c = s.shape
        specs.append(pl.BlockSpec((1, r // halves, c), lambda q, i, core: (2 * q + core[0], i, 0)))

    def body(core_ref, *refs):
        for i in range(n):
            refs[2 * n + i][...] = (refs[i][...].astype(F32) + refs[n + i][...].astype(F32)).astype(BF16)

    return pl.pallas_call(
        body, name=name, out_shape=[jax.ShapeDtypeStruct(s.shape, BF16) for s in pieces],
        grid_spec=pltpu.PrefetchScalarGridSpec(num_scalar_prefetch=1, grid=(N_CHIPS, halves), in_specs=specs + specs,
                                               out_specs=specs),
        compiler_params=_params("arbitrary", "arbitrary"),
    )(_core_index(), *pieces, *received)


def _exchange_start(srcs, lands, after, scatter, name, which=ALL_PEERS):
    n = len(srcs)

    def body(*refs):
        ins, land_in = refs[:n], refs[n:2 * n]
        send_sems, recv_sems = refs[2 * n + 1], refs[2 * n + 2]
        token = refs[-1]
        sends, _ = _split_copies(ins, land_in, send_sems, recv_sems, scatter, False, which)
        for cp in sends:
            cp.start()
        token[...] = jnp.zeros_like(token)

    sem = pltpu.SemaphoreType.DMA(((N_CHIPS if which == SIBLING_SWAP else len(which)) * n,))
    out = pl.pallas_call(
        body, name=name,
        out_shape=(sem, sem, *[pltpu.HBM(s.shape, s.dtype) for s in srcs], *[pltpu.HBM(s.shape, s.dtype) for s in lands],
                   jax.ShapeDtypeStruct((SUBLANES, LANES), F32)),
        in_specs=[_HBM] * (2 * n) + [pl.BlockSpec(memory_space=pl.ANY)],
        out_specs=(_SEM, _SEM, *[_HBM] * (2 * n), pl.BlockSpec(memory_space=pltpu.VMEM)),
        input_output_aliases={i: 2 + i for i in range(2 * n)},
        compiler_params=pltpu.CompilerParams(has_side_effects=_EFFECT),
    )(*srcs, *lands, after)
    return (out[0], out[1], out[2:2 + n], out[2 + n:2 + 2 * n]), out[-1]


def _exchange_wait(handle, after, scatter, name, which=ALL_PEERS, with_srcs=False):
    send_sems, recv_sems, srcs, lands = handle
    n = len(srcs)
    after = list(after) if isinstance(after, (list, tuple)) else [after]

    def body(*refs):
        ins, land_in = refs[:n], refs[n:2 * n]
        sends, recvs = _split_copies(ins, land_in, refs[2 * n], refs[2 * n + 1], scatter, True, which)
        for cp in sends:
            cp.wait_send()
        for cp in recvs:
            cp.wait_recv()

    out = pl.pallas_call(
        body, name=name,
        out_shape=(*[pltpu.HBM(s.shape, s.dtype) for s in srcs], *[pltpu.HBM(s.shape, s.dtype) for s in lands]),
        in_specs=[_HBM] * (2 * n) + [_SEM, _SEM] + [pl.BlockSpec(memory_space=pl.ANY)] * len(after),
        out_specs=tuple([_HBM] * (2 * n)),
        input_output_aliases={i: i for i in range(2 * n)},
        compiler_params=pltpu.CompilerParams(has_side_effects=_EFFECT),
    )(*srcs, *lands, send_sems, recv_sems, *after)
    return (out[:n], out[n:]) if with_srcs else out[n:]


def _gather_sibling(lands, name):
    n = len(lands)
    chips = ((0, 0), (0, 1), (1, 0), (1, 1))

    def body(*refs):
        ins, outs = refs[:n], refs[n:2 * n]
        send_sems, recv_sems = refs[2 * n], refs[2 * n + 1]
        x, y, c = lax.axis_index("x"), lax.axis_index("y"), lax.axis_index("c")
        sends, recvs = [], []
        for j, (bx, by) in enumerate(chips):
            chip = 4 * (1 - x if bx else x) + 2 * (1 - y if by else y)
            for i in range(n):
                sems = dict(send_sem=send_sems.at[j * n + i], recv_sem=recv_sems.at[j * n + i],
                            device_id=(x, y, 1 - c), device_id_type=pl.DeviceIdType.MESH)
                sends.append(pltpu.make_async_remote_copy(src_ref=ins[i].at[chip + c], dst_ref=outs[i].at[chip + c], **sems))
                recvs.append(pltpu.make_async_remote_copy(src_ref=ins[i].at[chip + c], dst_ref=outs[i].at[chip + 1 - c],
                                                          **sems))
        for cp in sends:
            cp.start()
        for cp in recvs:
            cp.wait_recv()
        for cp in sends:
            cp.wait_send()

    return pl.pallas_call(
        body, name=name, out_shape=[jax.ShapeDtypeStruct(a.shape, a.dtype) for a in lands],
        in_specs=[_HBM] * n, out_specs=[_HBM] * n,
        scratch_shapes=[pltpu.SemaphoreType.DMA((len(chips) * n,)), pltpu.SemaphoreType.DMA((len(chips) * n,))],
        input_output_aliases={i: i for i in range(n)},
    )(*lands)


def _cond_fwd(c_all, ada_w, ada_b_mine):
    ncol = ada_w.shape[-1]

    def body(c_ref, w_ref, b_ref, o_ref):
        c = c_ref[...]
        ca = (c * _sigmoid(c)).astype(BF16)
        o_ref[0] = _dot(ca, w_ref[0].astype(BF16)) + b_ref[0]

    return pl.pallas_call(
        body, name="cond_fwd", grid=(DEPTH,),
        out_shape=jax.ShapeDtypeStruct((DEPTH, N_DEV, ncol), F32),
        in_specs=[pl.BlockSpec((N_DEV, D_MODEL), lambda l: (0, 0)),
                  pl.BlockSpec((1, D_MODEL, ncol), lambda l: (l, 0, 0)),
                  pl.BlockSpec((1, 1, ncol), lambda l: (l, 0, 0))],
        out_specs=pl.BlockSpec((1, N_DEV, ncol), lambda l: (l, 0, 0)),
        compiler_params=_params("arbitrary"),
    )(c_all, ada_w, ada_b_mine)


def _cond_bwd(c_all_t, dcond_mine, dcond_all):
    ncol = dcond_mine.shape[-1]
    nall = dcond_all.shape[-1]

    def body(ct_ref, d_ref, da_ref, gw_ref, gb_ref):
        ct = ct_ref[...]
        ct = ct * _sigmoid(ct)
        d = d_ref[0]
        acc = ct[:, 0:1] * d[0:1, :]
        for b in range(1, N_DEV):
            acc = acc + ct[:, b:b + 1] * d[b:b + 1, :]
        gw_ref[0] = acc
        gb_ref[0] = _sum0(da_ref[0])

    return pl.pallas_call(
        body, name="cond_bwd", grid=(DEPTH,),
        out_shape=(jax.ShapeDtypeStruct((DEPTH, D_MODEL, ncol), F32), jax.ShapeDtypeStruct((DEPTH, 1, nall), F32)),
        in_specs=[pl.BlockSpec((D_MODEL, N_DEV), lambda l: (0, 0)),
                  pl.BlockSpec((1, N_DEV, ncol), lambda l: (l, 0, 0)),
                  pl.BlockSpec((1, N_DEV, nall), lambda l: (l, 0, 0))],
        out_specs=(pl.BlockSpec((1, D_MODEL, ncol), lambda l: (l, 0, 0)),
                   pl.BlockSpec((1, 1, nall), lambda l: (l, 0, 0))),
        compiler_params=_params("arbitrary"),
    )(c_all_t, dcond_mine, dcond_all)


def _modnorm(x, g, shift, scale):
    xhat, r = _rms(x)
    return (xhat * g) * (1.0 + scale) + shift, xhat, r


def _modnorm_bwd(xhat, r, g, scale, dh):
    n = xhat * g
    dn = dh * (1.0 + scale)
    dx = _rms_bwd(xhat, r, dn * g)
    return dx, _sum0(dh), _sum0(dh * n), _sum0(dn * xhat)


def _row_spec(rows):
    return pl.BlockSpec((rows, D_MODEL), lambda *_: (0, 0))


def _ffn_fwd(x, cond3, g, w_in_g, w_out_g):
    tm = TOKEN_TILE
    last = N_FF_CHUNK - 1

    def body(x_ref, cond_ref, g_ref, wa_ref, wb_ref, wo_ref, xo_ref, f_ref, a_ref, b_ref, h_ref, h_scr, acc_scr):
        j = pl.program_id(1)

        @pl.when(j == 0)
        def _():
            h, _, _ = _modnorm(x_ref[...], g_ref[...], cond_ref[0:1, :], cond_ref[1:2, :])
            hb = h.astype(BF16)
            h_scr[...] = hb
            h_ref[...] = hb
            acc_scr[...] = jnp.zeros_like(acc_scr)

        wa, wb, wo = wa_ref[0], wb_ref[0], wo_ref[0]
        for r in range(tm // ROW_SUBTILE):
            rows = slice(r * ROW_SUBTILE, (r + 1) * ROW_SUBTILE)
            h = h_scr[rows, :]
            a = _dot_nt(h, wa)
            b = _dot_nt(h, wb)
            a_ref[0, rows, :] = a.astype(BF16)
            b_ref[0, rows, :] = b.astype(BF16)
            act = (a * _sigmoid(a)) * b
            acc_scr[rows, :] += _dot(act.astype(BF16), wo)

        @pl.when(j == last)
        def _():
            f = acc_scr[...]
            f_ref[...] = f
            xo_ref[...] = x_ref[...] + (0.5 * cond_ref[2:3, :]) * f

    tok = pl.BlockSpec((tm, D_MODEL), lambda i, j: (i, 0))
    chunk = pl.BlockSpec((1, tm, FF_PIECE), lambda i, j: (j, i, 0))
    chunk_shape = jax.ShapeDtypeStruct((N_FF_CHUNK, SEQ, FF_PIECE), BF16)
    return pl.pallas_call(
        body, name="ffn_fwd", grid=(SEQ // tm, N_FF_CHUNK),
        out_shape=(jax.ShapeDtypeStruct((SEQ, D_MODEL), F32), jax.ShapeDtypeStruct((SEQ, D_MODEL), F32),
                   chunk_shape, chunk_shape, jax.ShapeDtypeStruct((SEQ, D_MODEL), BF16)),
        in_specs=[tok, _row_spec(3), _row_spec(1),
                  pl.BlockSpec((1, FF_PIECE, D_MODEL), lambda i, j: (j, 0, 0)),
                  pl.BlockSpec((1, FF_PIECE, D_MODEL), lambda i, j: (j + N_FF_CHUNK, 0, 0)),
                  pl.BlockSpec((1, FF_PIECE, D_MODEL), lambda i, j: (j, 0, 0))],
        out_specs=(tok, tok, chunk, chunk, tok),
        scratch_shapes=[pltpu.VMEM((tm, D_MODEL), BF16), pltpu.VMEM((tm, D_MODEL), F32)],
        compiler_params=_params("arbitrary", "arbitrary"),
    )(x, cond3, g, w_in_g, w_in_g, w_out_g)


def _ffn_bwd(dy, x, f, a_sv, b_sv, cond3, g, w_in_g, w_out_g):
    tm = TOKEN_TILE
    last = N_FF_CHUNK - 1

    def body(dy_ref, x_ref, f_ref, a_ref, b_ref, cond_ref, g_ref, wa_ref, wb_ref, wo_ref,
             dx_ref, da_ref, db_ref, act_ref, do_ref, part_ref, do_scr, dh_scr):
        i, j = pl.program_id(0), pl.program_id(1)

        @pl.when(j == 0)
        def _():
            do = ((0.5 * cond_ref[2:3, :]) * dy_ref[...]).astype(BF16)
            do_scr[...] = do
            do_ref[...] = do
            dh_scr[...] = jnp.zeros_like(dh_scr)

        @pl.when((i == 0) & (j == 0))
        def _():
            part_ref[...] = jnp.zeros_like(part_ref)

        wa, wb, wo = wa_ref[0], wb_ref[0], wo_ref[0]
        for r in range(tm // ROW_SUBTILE):
            rows = slice(r * ROW_SUBTILE, (r + 1) * ROW_SUBTILE)
            do = do_scr[rows, :]
            a = a_ref[0, rows, :].astype(F32)
            b = b_ref[0, rows, :].astype(F32)
            dact = _dot_nt(do, wo)
            sig = _sigmoid(a)
            s = a * sig
            da = (dact * b * (sig * (1.0 + a * (1.0 - sig)))).astype(BF16)
            db = (dact * s).astype(BF16)
            da_ref[0, rows, :] = da
            db_ref[0, rows, :] = db
            act_ref[0, rows, :] = (s * b).astype(BF16)
            dh_scr[rows, :] += _dot(da, wa) + _dot(db, wb)

        @pl.when(j == last)
        def _():
            dyv = dy_ref[...]
            xhat, r = _rms(x_ref[...])
            dx, dshift, dscale, dg = _modnorm_bwd(xhat, r, g_ref[...], cond_ref[1:2, :], dh_scr[...])
            dx_ref[...] = dyv + dx
            part_ref[0:1, :] += dshift
            part_ref[1:2, :] += dscale
            part_ref[2:3, :] += _sum0(0.5 * dyv * f_ref[...])
            part_ref[3:4, :] += dg

    tok = pl.BlockSpec((tm, D_MODEL), lambda i, j: (i, 0))
    chunk = pl.BlockSpec((1, tm, FF_PIECE), lambda i, j: (j, i, 0))
    chunk_shape = jax.ShapeDtypeStruct((N_FF_CHUNK, SEQ, FF_PIECE), BF16)
    return pl.pallas_call(
        body, name="ffn_bwd", grid=(SEQ // tm, N_FF_CHUNK),
        out_shape=(jax.ShapeDtypeStruct((SEQ, D_MODEL), F32), chunk_shape, chunk_shape, chunk_shape,
                   jax.ShapeDtypeStruct((SEQ, D_MODEL), BF16), jax.ShapeDtypeStruct((SUBLANES, D_MODEL), F32)),
        in_specs=[tok, tok, tok, chunk, chunk, _row_spec(3), _row_spec(1),
                  pl.BlockSpec((1, FF_PIECE, D_MODEL), lambda i, j: (j, 0, 0)),
                  pl.BlockSpec((1, FF_PIECE, D_MODEL), lambda i, j: (j + N_FF_CHUNK, 0, 0)),
                  pl.BlockSpec((1, FF_PIECE, D_MODEL), lambda i, j: (j, 0, 0))],
        out_specs=(tok, chunk, chunk, chunk, tok, _row_spec(SUBLANES)),
        scratch_shapes=[pltpu.VMEM((tm, D_MODEL), BF16), pltpu.VMEM((tm, D_MODEL), F32)],
        compiler_params=_params("arbitrary", "arbitrary"),
    )(dy, x, f, a_sv, b_sv, cond3, g, w_in_g, w_in_g, w_out_g)


def _dw(lhs_a, rhs, lhs_b=None, name="dw"):
    pa, s, m = lhs_a.shape
    nn = rhs.shape[-1]
    pb = 0 if lhs_b is None else lhs_b.shape[0]
    two = lhs_b is not None

    def body(*refs):
        if two:
            a_ref, b_ref, r_ref, o_ref = refs
            p = pl.program_id(0)

            @pl.when(p < pa)
            def _():
                o_ref[0] = _dot_tn(a_ref[0], r_ref[...]).astype(BF16)

            @pl.when(p >= pa)
            def _():
                o_ref[0] = _dot_tn(b_ref[0], r_ref[...]).astype(BF16)
        else:
            a_ref, r_ref, o_ref = refs
            o_ref[0] = _dot_tn(a_ref[0], r_ref[...]).astype(BF16)

    if two:
        in_specs = [pl.BlockSpec((1, s, m), lambda p: (jnp.minimum(p, pa - 1), 0, 0)),
                    pl.BlockSpec((1, s, m), lambda p: (jnp.maximum(p - pa, 0), 0, 0))]
        args = (lhs_a, lhs_b, rhs)
    else:
        in_specs = [pl.BlockSpec((1, s, m), lambda p: (p, 0, 0))]
        args = (lhs_a, rhs)
    in_specs.append(pl.BlockSpec((s, nn), lambda p: (0, 0)))
    return pl.pallas_call(
        body, name=name, grid=(pa + pb,),
        out_shape=jax.ShapeDtypeStruct((pa + pb, m, nn), BF16),
        in_specs=in_specs, out_specs=pl.BlockSpec((1, m, nn), lambda p: (p, 0, 0)),
        compiler_params=_params("arbitrary"),
    )(*args)


def _mix_in_fwd(x, cond2, g, w):
    tm = TOKEN_TILE

    def body(x_ref, cond_ref, g_ref, w_ref, z_ref):
        h, _, _ = _modnorm(x_ref[...], g_ref[...], cond_ref[0:1, :], cond_ref[1:2, :])
        z_ref[...] = _dot_nt(h.astype(BF16), w_ref[...])

    return pl.pallas_call(
        body, name="mix_in_fwd", grid=(SEQ // tm,),
        out_shape=jax.ShapeDtypeStruct((SEQ, P_IN), F32),
        in_specs=[pl.BlockSpec((tm, D_MODEL), lambda i: (i, 0)), _row_spec(2), _row_spec(1),
                  pl.BlockSpec((P_IN, D_MODEL), lambda i: (0, 0))],
        out_specs=pl.BlockSpec((tm, P_IN), lambda i: (i, 0)),
        compiler_params=_params("arbitrary"),
    )(x, cond2, g, w)


MIX_SLABS = ((0, 2 * W_GRP), (2 * W_GRP, 3 * W_GRP), (3 * W_GRP, 6 * W_GRP), (6 * W_GRP, 7 * W_GRP))


def _mix_in_bwd(dzs, x, dy, cond2, g, w):
    tm = TOKEN_TILE

    def body(dza_ref, dzb_ref, dzc_ref, dzd_ref, x_ref, dy_ref, cond_ref, g_ref, w_ref, dx_ref, h_ref, dzo_ref, part_ref):
        i = pl.program_id(0)

        @pl.when(i == 0)
        def _():
            part_ref[...] = jnp.zeros_like(part_ref)

        h, xhat, r = _modnorm(x_ref[...], g_ref[...], cond_ref[0:1, :], cond_ref[1:2, :])
        h_ref[...] = h.astype(BF16)
        dh = None
        for (lo, hi), d_ref in zip(MIX_SLABS, (dza_ref, dzb_ref, dzc_ref, dzd_ref)):
            dzb = d_ref[...].astype(BF16)
            dzo_ref[:, lo:hi] = dzb
            t = _dot(dzb, w_ref[lo:hi, :])
            dh = t if dh is None else dh + t
        dx, dshift, dscale, dg = _modnorm_bwd(xhat, r, g_ref[...], cond_ref[1:2, :], dh)
        dx_ref[...] = dy_ref[...] + dx
        part_ref[0:1, :] += dshift
        part_ref[1:2, :] += dscale
        part_ref[2:3, :] += dg

    tok = pl.BlockSpec((tm, D_MODEL), lambda i: (i, 0))
    ztok = pl.BlockSpec((tm, P_IN), lambda i: (i, 0))
    slabs = [pl.BlockSpec((tm, hi - lo), lambda i: (i, 0)) for lo, hi in MIX_SLABS]
    return pl.pallas_call(
        body, name="mix_in_bwd", grid=(SEQ // tm,),
        out_shape=(jax.ShapeDtypeStruct((SEQ, D_MODEL), F32), jax.ShapeDtypeStruct((SEQ, D_MODEL), BF16),
                   jax.ShapeDtypeStruct((SEQ, P_IN), BF16), jax.ShapeDtypeStruct((SUBLANES, D_MODEL), F32)),
        in_specs=[*slabs, tok, tok, _row_spec(2), _row_spec(1), pl.BlockSpec((P_IN, D_MODEL), lambda i: (0, 0))],
        out_specs=(tok, tok, ztok, _row_spec(SUBLANES)),
        compiler_params=_params("arbitrary"),
    )(*dzs, x, dy, cond2, g, w)


def _group_norm(ys, g_ref):
    out = []
    for k, y in enumerate(ys):
        yhat, r = _rms(y)
        out.append((yhat, r, g_ref[:, k * W_GRP:(k + 1) * W_GRP]))
    return out


def _mix_out_fwd(x, ys, g, gate, w):
    tm = TOKEN_TILE

    def body(x_ref, ya_ref, yb_ref, yc_ref, yd_ref, g_ref, gate_ref, w_ref, xo_ref):
        acc = None
        for k, (yhat, _, gk) in enumerate(_group_norm([r[...] for r in (ya_ref, yb_ref, yc_ref, yd_ref)], g_ref)):
            t = _dot((yhat * gk).astype(BF16), w_ref[k * W_GRP:(k + 1) * W_GRP, :])
            acc = t if acc is None else acc + t
        xo_ref[...] = x_ref[...] + gate_ref[...] * acc

    tok = pl.BlockSpec((tm, D_MODEL), lambda i: (i, 0))
    ytok = pl.BlockSpec((tm, W_GRP), lambda i: (i, 0))
    return pl.pallas_call(
        body, name="mix_out_fwd", grid=(SEQ // tm,),
        out_shape=jax.ShapeDtypeStruct((SEQ, D_MODEL), F32),
        in_specs=[tok, ytok, ytok, ytok, ytok, _row_spec(1), _row_spec(1),
                  pl.BlockSpec((D_MODEL, D_MODEL), lambda i: (0, 0))],
        out_specs=tok, compiler_params=_params("arbitrary"),
    )(x, *ys, g, gate, w)


def _mix_out_bwd(dy, ys, g, gate, w):
    tm = TOKEN_TILE

    def body(dy_ref, ya_ref, yb_ref, yc_ref, yd_ref, g_ref, gate_ref, w_ref,
             da_ref, db_ref, dc_ref, dd_ref, yn_ref, dmo_ref, part_ref):
        i = pl.program_id(0)

        @pl.when(i == 0)
        def _():
            part_ref[...] = jnp.zeros_like(part_ref)

        dyv = dy_ref[...]
        dmo = (gate_ref[...] * dyv).astype(BF16)
        dmo_ref[...] = dmo
        dyn = _dot_nt(dmo, w_ref[...])
        norms = _group_norm([r[...] for r in (ya_ref, yb_ref, yc_ref, yd_ref)], g_ref)
        mo = None
        for k, ((yhat, r, gk), o_ref) in enumerate(zip(norms, (da_ref, db_ref, dc_ref, dd_ref))):
            sl = slice(k * W_GRP, (k + 1) * W_GRP)
            ynk = (yhat * gk).astype(BF16)
            yn_ref[:, sl] = ynk
            t = _dot(ynk, w_ref[sl, :])
            mo = t if mo is None else mo + t
            dk = dyn[:, sl]
            o_ref[...] = _rms_bwd(yhat, r, dk * gk)
            part_ref[1:2, sl] += _sum0(dk * yhat)
        part_ref[0:1, :] += _sum0(dyv * mo)

    tok = pl.BlockSpec((tm, D_MODEL), lambda i: (i, 0))
    ytok = pl.BlockSpec((tm, W_GRP), lambda i: (i, 0))
    ysh = jax.ShapeDtypeStruct((SEQ, W_GRP), F32)
    return pl.pallas_call(
        body, name="mix_out_bwd", grid=(SEQ // tm,),
        out_shape=(ysh, ysh, ysh, ysh, jax.ShapeDtypeStruct((SEQ, D_MODEL), BF16),
                   jax.ShapeDtypeStruct((SEQ, D_MODEL), BF16), jax.ShapeDtypeStruct((SUBLANES, D_MODEL), F32)),
        in_specs=[tok, ytok, ytok, ytok, ytok, _row_spec(1), _row_spec(1),
                  pl.BlockSpec((D_MODEL, D_MODEL), lambda i: (0, 0))],
        out_specs=(ytok, ytok, ytok, ytok, tok, tok, _row_spec(SUBLANES)),
        compiler_params=_params("arbitrary"),
    )(dy, *ys, g, gate, w)


def _shift_down(v, k, rows):
    return jnp.where(rows >= k, pltpu.roll(v, k, axis=0), 0.0)


def _shift_up(v, k, rows):
    n = v.shape[0]
    return jnp.where(rows < n - k, pltpu.roll(v, n - k, axis=0), 0.0)


def _zslab(width, index):
    return pl.BlockSpec((SEQ, width), lambda *_: (0, index))


def _full(shape):
    return pl.BlockSpec(shape, lambda *_: (0,) * len(shape))


def _head_avg():
    r = lax.broadcasted_iota(jnp.int32, (W_GRP, W_GRP), 0) // SGU_HEAD_DIM
    c = lax.broadcasted_iota(jnp.int32, (W_GRP, W_GRP), 1) // SGU_HEAD_DIM
    return jnp.where(r == c, 1.0 / SGU_HEAD_DIM, 0.0).astype(F32)


def _sgu_norm(za):
    z = _gelu(za)
    u, v = z[:, :W_GRP], z[:, W_GRP:]
    avg = _head_avg()
    vc = v - _dot_hi(v, avg)
    rstd = lax.rsqrt(_dot_hi(vc * vc, avg) + EPS)
    return u, vc * rstd, rstd


def _sgu_masked_w(w_ref):
    t = lax.broadcasted_iota(jnp.int32, (CHUNK, CHUNK), 0)
    s = lax.broadcasted_iota(jnp.int32, (CHUNK, CHUNK), 1)
    tril = t >= s
    return [jnp.where(tril, w_ref[:, h * CHUNK:(h + 1) * CHUNK], 0.0).astype(BF16) for h in range(SGU_HEADS)]


def _head_of_lane():
    return lax.broadcasted_iota(jnp.int32, (CHUNK, W_GRP), 1) // SGU_HEAD_DIM


def _sgu_fwd(z, w_cat, bias):
    def body(z_ref, w_ref, b_ref, y_ref, vn_scr, u_scr):
        u, vn, _ = _sgu_norm(z_ref[...])
        vn_scr[...] = vn.astype(BF16)
        u_scr[...] = u
        ws = _sgu_masked_w(w_ref)
        head = _head_of_lane()
        bias_v = b_ref[...]

        def chunk(n, carry):
            rows = pl.ds(pl.multiple_of(n * CHUNK, CHUNK), CHUNK)
            vb = vn_scr[rows, :]
            mixed = bias_v
            for h in range(SGU_HEADS):
                mixed = mixed + jnp.where(head == h, _dot(ws[h], vb), 0.0)
            y_ref[rows, :] = u_scr[rows, :] * mixed
            return carry

        lax.fori_loop(0, SEQ // CHUNK, chunk, 0)

    return pl.pallas_call(
        body, name="sgu_fwd", grid=(1,),
        out_shape=jax.ShapeDtypeStruct((SEQ, W_GRP), F32),
        in_specs=[_zslab(2 * W_GRP, 0), _full((CHUNK, SGU_HEADS * CHUNK)), _full((CHUNK, W_GRP))],
        out_specs=_full((SEQ, W_GRP)),
        scratch_shapes=[pltpu.VMEM((SEQ, W_GRP), BF16), pltpu.VMEM((SEQ, W_GRP), F32)],
        compiler_params=_params("arbitrary"),
    )(z, w_cat, bias)


def _sgu_bwd(z, dy, w_cat, bias):
    def body(z_ref, dy_ref, w_ref, b_ref, dz_ref, dw_ref, db_ref, vn_scr, u_scr, dvn_scr, du_scr):
        za = z_ref[...]
        u, vn, rstd = _sgu_norm(za)
        vn_scr[...] = vn.astype(BF16)
        u_scr[...] = u
        ws = _sgu_masked_w(w_ref)
        head = _head_of_lane()
        bias_v = b_ref[...]

        def chunk(n, carry):
            dws, dbias = carry
            rows = pl.ds(pl.multiple_of(n * CHUNK, CHUNK), CHUNK)
            vb = vn_scr[rows, :]
            mixed = bias_v
            for h in range(SGU_HEADS):
                mixed = mixed + jnp.where(head == h, _dot(ws[h], vb), 0.0)
            dyc = dy_ref[rows, :]
            du_scr[rows, :] = dyc * mixed
            dmixed = dyc * u_scr[rows, :]
            dvn = jnp.zeros((CHUNK, W_GRP), F32)
            new_dws = []
            for h in range(SGU_HEADS):
                dm = jnp.where(head == h, dmixed, 0.0).astype(BF16)
                new_dws.append(dws[h] + _dot_nt(dm, vb))
                dvn = dvn + _dot_tn(ws[h], dm)
            dvn_scr[rows, :] = dvn
            return tuple(new_dws), dbias + dmixed

        zero_w = tuple(jnp.zeros((CHUNK, CHUNK), F32) for _ in range(SGU_HEADS))
        dws, dbias = lax.fori_loop(0, SEQ // CHUNK, chunk, (zero_w, jnp.zeros((CHUNK, W_GRP), F32)))
        t = lax.broadcasted_iota(jnp.int32, (CHUNK, CHUNK), 0)
        s = lax.broadcasted_iota(jnp.int32, (CHUNK, CHUNK), 1)
        for h in range(SGU_HEADS):
            dw_ref[:, h * CHUNK:(h + 1) * CHUNK] = jnp.where(t >= s, dws[h], 0.0)
        avg = _head_avg()
        db_ref[...] = _dot_hi(dbias, avg) * float(SGU_HEAD_DIM)
        dvn = dvn_scr[...]
        dv = rstd * (dvn - _dot_hi(dvn, avg) - vn * _dot_hi(dvn * vn, avg))
        gg = _gelu_grad(za)
        dz_ref[:, :W_GRP] = gg[:, :W_GRP] * du_scr[...]
        dz_ref[:, W_GRP:] = gg[:, W_GRP:] * dv

    return pl.pallas_call(
        body, name="sgu_bwd", grid=(1,),
        out_shape=(jax.ShapeDtypeStruct((SEQ, 2 * W_GRP), F32), jax.ShapeDtypeStruct((CHUNK, SGU_HEADS * CHUNK), F32),
                   jax.ShapeDtypeStruct((CHUNK, W_GRP), F32)),
        in_specs=[_zslab(2 * W_GRP, 0), _full((SEQ, W_GRP)), _full((CHUNK, SGU_HEADS * CHUNK)), _full((CHUNK, W_GRP))],
        out_specs=(_full((SEQ, 2 * W_GRP)), _full((CHUNK, SGU_HEADS * CHUNK)), _full((CHUNK, W_GRP))),
        scratch_shapes=[pltpu.VMEM((SEQ, W_GRP), BF16), pltpu.VMEM((SEQ, W_GRP), F32),
                        pltpu.VMEM((SEQ, W_GRP), F32), pltpu.VMEM((SEQ, W_GRP), F32)],
        compiler_params=_params("arbitrary"),
    )(z, dy, w_cat, bias)


def _pool_window_of_lane(shape):
    grp = lax.broadcasted_iota(jnp.int32, shape, 1) // POOL_GROUP_DIM
    win = jnp.full(shape, POOL_WINDOWS[0], jnp.int32)
    for k in range(1, len(POOL_WINDOWS)):
        win = jnp.where(grp == k, POOL_WINDOWS[k], win)
    return grp, win


def _pool_select(levels, grp):
    out = levels[0]
    for k in range(1, len(levels)):
        out = jnp.where(grp == k, levels[k], out)
    return out


def _pool_p(z):
    shape = z.shape
    rows = lax.broadcasted_iota(jnp.int32, shape, 0)
    grp, win = _pool_window_of_lane(shape)
    levels, s, k = [], z, 1
    for _ in POOL_WINDOWS:
        s = s + _shift_down(s, k, rows)
        levels.append(s)
        k *= 2
    inv = 1.0 / jnp.minimum(rows + 1, win).astype(F32)
    return _pool_select(levels, grp) * inv - z, inv, rows, grp


def _pool_fwd(z, w_bd, scale):
    def body(z_ref, w_ref, s_ref, y_ref):
        p, _, _, _ = _pool_p(z_ref[...])
        y_ref[...] = _dot(p.astype(BF16), w_ref[...]) * s_ref[...]

    return pl.pallas_call(
        body, name="pool_fwd", grid=(1,),
        out_shape=jax.ShapeDtypeStruct((SEQ, W_GRP), F32),
        in_specs=[_zslab(W_GRP, 2), _full((W_GRP, W_GRP)), _full((1, W_GRP))],
        out_specs=_full((SEQ, W_GRP)), compiler_params=_params("arbitrary"),
    )(z, w_bd, scale)


def _pool_bwd(z, dy, w_bd, scale):
    def body(z_ref, dy_ref, w_ref, s_ref, dz_ref, dw_ref, ds_ref):
        p, inv, rows, grp = _pool_p(z_ref[...])
        pb = p.astype(BF16)
        dyv = dy_ref[...]
        ds_ref[...] = _sum0(dyv * _dot(pb, w_ref[...]))
        dpre = (dyv * s_ref[...]).astype(BF16)
        dw_ref[...] = _dot_tn(pb, dpre)
        dp = _dot_nt(dpre, w_ref[...])
        q = dp * inv
        levels, s, k = [], q, 1
        for _ in POOL_WINDOWS:
            s = s + _shift_up(s, k, rows)
            levels.append(s)
            k *= 2
        dz_ref[...] = _pool_select(levels, grp) - dp

    return pl.pallas_call(
        body, name="pool_bwd", grid=(1,),
        out_shape=(jax.ShapeDtypeStruct((SEQ, W_GRP), F32), jax.ShapeDtypeStruct((W_GRP, W_GRP), F32),
                   jax.ShapeDtypeStruct((1, W_GRP), F32)),
        in_specs=[_zslab(W_GRP, 2), _full((SEQ, W_GRP)), _full((W_GRP, W_GRP)), _full((1, W_GRP))],
        out_specs=(_full((SEQ, W_GRP)), _full((W_GRP, W_GRP)), _full((1, W_GRP))),
        compiler_params=_params("arbitrary"),
    )(z, dy, w_bd, scale)


def _conv_fwd(z, w):
    def body(z_ref, w_ref, y_ref):
        zc = z_ref[...]
        bg, cg, xh = zc[:, :W_GRP], zc[:, W_GRP:2 * W_GRP], zc[:, 2 * W_GRP:]
        rows = lax.broadcasted_iota(jnp.int32, (SEQ, W_GRP), 0)
        y = cg * xh
        conv = w_ref[0:1, :] * _shift_down(y, 2, rows) + w_ref[1:2, :] * _shift_down(y, 1, rows) + w_ref[2:3, :] * y
        y_ref[...] = bg * conv

    return pl.pallas_call(
        body, name="conv_fwd", grid=(1,),
        out_shape=jax.ShapeDtypeStruct((SEQ, W_GRP), F32),
        in_specs=[_zslab(3 * W_GRP, 1), _full((3, W_GRP))],
        out_specs=_full((SEQ, W_GRP)), compiler_params=_params("arbitrary"),
    )(z, w)


def _conv_bwd(z, dy, w):
    def body(z_ref, dy_ref, w_ref, dz_ref, dw_ref):
        zc = z_ref[...]
        bg, cg, xh = zc[:, :W_GRP], zc[:, W_GRP:2 * W_GRP], zc[:, 2 * W_GRP:]
        rows = lax.broadcasted_iota(jnp.int32, (SEQ, W_GRP), 0)
        y = cg * xh
        y2, y1 = _shift_down(y, 2, rows), _shift_down(y, 1, rows)
        conv = w_ref[0:1, :] * y2 + w_ref[1:2, :] * y1 + w_ref[2:3, :] * y
        dyv = dy_ref[...]
        dconv = dyv * bg
        dw_ref[...] = jnp.zeros_like(dw_ref)
        dw_ref[0:1, :] = _sum0(dconv * y2)
        dw_ref[1:2, :] = _sum0(dconv * y1)
        dw_ref[2:3, :] = _sum0(dconv * y)
        dyy = (w_ref[0:1, :] * _shift_up(dconv, 2, rows) + w_ref[1:2, :] * _shift_up(dconv, 1, rows)
               + w_ref[2:3, :] * dconv)
        dz_ref[:, :W_GRP] = dyv * conv
        dz_ref[:, W_GRP:2 * W_GRP] = dyy * xh
        dz_ref[:, 2 * W_GRP:] = dyy * cg

    return pl.pallas_call(
        body, name="conv_bwd", grid=(1,),
        out_shape=(jax.ShapeDtypeStruct((SEQ, 3 * W_GRP), F32), jax.ShapeDtypeStruct((SUBLANES, W_GRP), F32)),
        in_specs=[_zslab(3 * W_GRP, 1), _full((SEQ, W_GRP)), _full((3, W_GRP))],
        out_specs=(_full((SEQ, 3 * W_GRP)), _full((SUBLANES, W_GRP))),
        compiler_params=_params("arbitrary"),
    )(z, dy, w)


def _s5_disc(lre, lim, ldt, br, bi):
    dt = jnp.exp(ldt)
    mag = jnp.exp(lre * dt)
    ang = lim * dt
    a_re, a_im = mag * jnp.cos(ang), mag * jnp.sin(ang)
    nr, ni = a_re - 1.0, a_im
    den = lre * lre + lim * lim
    k_re = (nr * lre + ni * lim) / den
    k_im = (ni * lre - nr * lim) / den
    return a_re, a_im, k_re * br - k_im * bi, k_re * bi + k_im * br


def _s5_prep_fwd(lre, lim, ldt, br, bi):
    def body(lre_ref, lim_ref, ldt_ref, br_ref, bi_ref, ar_ref, ai_ref, bbr_ref, bbi_ref):
        ar, ai, bbr, bbi = _s5_disc(lre_ref[...], lim_ref[...], ldt_ref[...], br_ref[...], bi_ref[...])
        ar_ref[...] = ar
        ai_ref[...] = ai
        bbr_ref[...] = bbr
        bbi_ref[...] = bbi

    return pl.pallas_call(
        body, name="s5_prep_fwd",
        out_shape=(jax.ShapeDtypeStruct(lre.shape, F32), jax.ShapeDtypeStruct(lre.shape, F32),
                   jax.ShapeDtypeStruct(br.shape, F32), jax.ShapeDtypeStruct(br.shape, F32)),
        compiler_params=_params(),
    )(lre, lim, ldt, br, bi)


def _s5_prep_bwd(lre, lim, ldt, br, bi, dar, dai, dbbr, dbbi):
    def body(lre_ref, lim_ref, ldt_ref, br_ref, bi_ref, dar_ref, dai_ref, dbbr_ref, dbbi_ref,
             o_lre, o_lim, o_ldt, o_br, o_bi):
        _, pull = jax.vjp(_s5_disc, lre_ref[...], lim_ref[...], ldt_ref[...], br_ref[...], bi_ref[...])
        g = pull((dar_ref[...], dai_ref[...], dbbr_ref[...], dbbi_ref[...]))
        for o, v in zip((o_lre, o_lim, o_ldt, o_br, o_bi), g):
            o[...] = v

    return pl.pallas_call(
        body, name="s5_prep_bwd",
        out_shape=tuple(jax.ShapeDtypeStruct(a.shape, F32) for a in (lre, lim, ldt, br, bi)),
        compiler_params=_params(),
    )(lre, lim, ldt, br, bi, dar, dai, dbbr, dbbi)


def _cmul(ar, ai, br, bi):
    return ar * br - ai * bi, ar * bi + ai * br


def _s5_tile_consts(ar, ai, reverse):
    if reverse:
        ai = -ai
    shape = (SUBLANES, S5_BLOCK)
    row = lax.broadcasted_iota(jnp.int32, shape, 0)
    a1 = (jnp.broadcast_to(ar, shape), jnp.broadcast_to(ai, shape))
    a2 = _cmul(*a1, *a1)
    a4 = _cmul(*a2, *a2)
    a8 = _cmul(*a4, *a4)
    steps = []
    for s, (pr, pi) in ((1, a1), (2, a2), (4, a4)):
        keep = (row < SUBLANES - s) if reverse else (row >= s)
        steps.append((s, jnp.where(keep, pr, 0.0), jnp.where(keep, pi, 0.0)))
    e = (SUBLANES - row) if reverse else (row + 1)
    pr, pi = jnp.ones(shape, F32), jnp.zeros(shape, F32)
    for bit, (qr, qi) in ((1, a1), (2, a2), (4, a4), (8, a8)):
        nr, ni = _cmul(pr, pi, qr, qi)
        hit = (e & bit) != 0
        pr, pi = jnp.where(hit, nr, pr), jnp.where(hit, ni, pi)
    return steps, pr, pi


def _s5_tile(xr, xi, steps, reverse):
    for s, pr, pi in steps:
        sh = SUBLANES - s if reverse else s
        sr, si = pltpu.roll(xr, sh, axis=0), pltpu.roll(xi, sh, axis=0)
        xr, xi = xr + pr * sr - pi * si, xi + pr * si + pi * sr
    return xr, xi


N_TILES = SEQ // SUBLANES


def _s5_fwd(z, b_re, b_im, c_re, c_im, a_re, a_im, d, glu_w, glu_b):
    nblk = S5_LANES // S5_BLOCK

    def body(u_ref, br_ref, bi_ref, cr_ref, ci_ref, ar_ref, ai_ref, d_ref, gw_ref, gb_ref,
             y_ref, y0_ref, xr_ref, xi_ref, ub_scr, acc_scr):
        jb = pl.program_id(0)

        @pl.when(jb == 0)
        def _():
            ub_scr[...] = u_ref[...].astype(BF16)
            acc_scr[...] = jnp.zeros_like(acc_scr)

        ub = ub_scr[...]
        xr_ref[...] = _dot(ub, br_ref[...])
        xi_ref[...] = _dot(ub, bi_ref[...])
        steps, pr, pi = _s5_tile_consts(ar_ref[...], ai_ref[...], False)

        def tile(t, carry):
            cr, ci = carry
            rows = pl.ds(pl.multiple_of(t * SUBLANES, SUBLANES), SUBLANES)
            xr, xi = _s5_tile(xr_ref[rows, :], xi_ref[rows, :], steps, False)
            xr, xi = xr + pr * cr - pi * ci, xi + pr * ci + pi * cr
            xr_ref[rows, :] = xr
            xi_ref[rows, :] = xi
            return xr[SUBLANES - 1:, :], xi[SUBLANES - 1:, :]

        zero = jnp.zeros((1, S5_BLOCK), F32)
        lax.fori_loop(0, N_TILES, tile, (zero, zero), unroll=2)
        acc_scr[...] += (_dot(xr_ref[...].astype(BF16), cr_ref[...]) - _dot(xi_ref[...].astype(BF16), ci_ref[...]))

        @pl.when(jb == nblk - 1)
        def _():
            y0 = acc_scr[...] + d_ref[...] * u_ref[...]
            y0_ref[...] = y0
            y1 = _gelu(y0)
            y_ref[...] = y1 * _sigmoid(_dot(y1.astype(BF16), gw_ref[...]) + gb_ref[...])

    lane_blk = pl.BlockSpec((SEQ, S5_BLOCK), lambda j: (0, j))
    return pl.pallas_call(
        body, name="s5_fwd", grid=(nblk,),
        out_shape=(jax.ShapeDtypeStruct((SEQ, W_GRP), F32), jax.ShapeDtypeStruct((SEQ, W_GRP), F32),
                   jax.ShapeDtypeStruct((SEQ, S5_LANES), F32), jax.ShapeDtypeStruct((SEQ, S5_LANES), F32)),
        in_specs=[_zslab(W_GRP, 6),
                  pl.BlockSpec((W_GRP, S5_BLOCK), lambda j: (0, j)), pl.BlockSpec((W_GRP, S5_BLOCK), lambda j: (0, j)),
                  pl.BlockSpec((S5_BLOCK, W_GRP), lambda j: (j, 0)), pl.BlockSpec((S5_BLOCK, W_GRP), lambda j: (j, 0)),
                  pl.BlockSpec((1, S5_BLOCK), lambda j: (0, j)), pl.BlockSpec((1, S5_BLOCK), lambda j: (0, j)),
                  _full((1, W_GRP)), _full((W_GRP, W_GRP)), _full((1, W_GRP))],
        out_specs=(_full((SEQ, W_GRP)), _full((SEQ, W_GRP)), lane_blk, lane_blk),
        scratch_shapes=[pltpu.VMEM((SEQ, W_GRP), BF16), pltpu.VMEM((SEQ, W_GRP), F32)],
        compiler_params=_params("arbitrary"),
    )(z, b_re, b_im, c_re, c_im, a_re, a_im, d, glu_w, glu_b)


def _s5_bwd(z, y0, dy, xr, xi, b_re, b_im, c_re, c_im, a_re, a_im, d, glu_w, glu_b):
    nblk = S5_LANES // S5_BLOCK

    def body(u_ref, y0_ref, dy_ref, xr_ref, xi_ref, br_ref, bi_ref, cr_ref, ci_ref, ar_ref, ai_ref,
             d_ref, gw_ref, gb_ref,
             du_ref, dbr_ref, dbi_ref, dcr_ref, dci_ref, dar_ref, dai_ref, dd_ref, dgw_ref, dgb_ref,
             ub_scr, dy0_scr, du_scr, lr_scr, li_scr):
        jb = pl.program_id(0)

        @pl.when(jb == 0)
        def _():
            u = u_ref[...]
            y0v = y0_ref[...]
            y1 = _gelu(y0v)
            y1b = y1.astype(BF16)
            sg = _sigmoid(_dot(y1b, gw_ref[...]) + gb_ref[...])
            dyv = dy_ref[...]
            dpre = dyv * y1 * sg * (1.0 - sg)
            dpb = dpre.astype(BF16)
            dgw_ref[...] = _dot_tn(y1b, dpb)
            dgb_ref[...] = _sum0(dpre)
            dy1 = dyv * sg + _dot_nt(dpb, gw_ref[...])
            dy0 = dy1 * _gelu_grad(y0v)
            dd_ref[...] = _sum0(dy0 * u)
            du_scr[...] = dy0 * d_ref[...]
            dy0_scr[...] = dy0.astype(BF16)
            ub_scr[...] = u.astype(BF16)

        dy0b = dy0_scr[...]
        lr_scr[...] = _dot_nt(dy0b, cr_ref[...])
        li_scr[...] = -_dot_nt(dy0b, ci_ref[...])
        dcr_ref[...] = _dot_tn(xr_ref[...].astype(BF16), dy0b)
        dci_ref[...] = -_dot_tn(xi_ref[...].astype(BF16), dy0b)
        steps, pr, pi = _s5_tile_consts(ar_ref[...], ai_ref[...], True)
        row = lax.broadcasted_iota(jnp.int32, (SUBLANES, S5_BLOCK), 0)

        def tile(k, carry):
            cr, ci, accr, acci = carry
            t = N_TILES - 1 - k
            rows = pl.ds(pl.multiple_of(t * SUBLANES, SUBLANES), SUBLANES)
            lr, li = _s5_tile(lr_scr[rows, :], li_scr[rows, :], steps, True)
            lr, li = lr + pr * cr - pi * ci, li + pr * ci + pi * cr
            lr_scr[rows, :] = lr
            li_scr[rows, :] = li
            prev = pl.ds(pl.multiple_of(jnp.maximum(t - 1, 0) * SUBLANES, SUBLANES), SUBLANES)
            live = jnp.where(t > 0, 1.0, 0.0)
            xpr = jnp.where(row == 0, pltpu.roll(xr_ref[prev, :], 1, axis=0) * live, pltpu.roll(xr_ref[rows, :], 1, axis=0))
            xpi = jnp.where(row == 0, pltpu.roll(xi_ref[prev, :], 1, axis=0) * live, pltpu.roll(xi_ref[rows, :], 1, axis=0))
            accr = accr + lr * xpr + li * xpi
            acci = acci + li * xpr - lr * xpi
            return lr[0:1, :], li[0:1, :], accr, acci

        zero = jnp.zeros((1, S5_BLOCK), F32)
        zt = jnp.zeros((SUBLANES, S5_BLOCK), F32)
        _, _, accr, acci = lax.fori_loop(0, N_TILES, tile, (zero, zero, zt, zt), unroll=2)
        dar_ref[...] = jnp.zeros_like(dar_ref)
        dai_ref[...] = jnp.zeros_like(dai_ref)
        dar_ref[0:1, :] = _sum0(accr)
        dai_ref[0:1, :] = _sum0(acci)
        lrb, lib = lr_scr[...].astype(BF16), li_scr[...].astype(BF16)
        ub = ub_scr[...]
        dbr_ref[...] = _dot_tn(ub, lrb)
        dbi_ref[...] = _dot_tn(ub, lib)
        du_scr[...] += _dot_nt(lrb, br_ref[...]) + _dot_nt(lib, bi_ref[...])

        @pl.when(jb == nblk - 1)
        def _():
            du_ref[...] = du_scr[...]

    lane_blk = pl.BlockSpec((SEQ, S5_BLOCK), lambda j: (0, j))
    bspec = pl.BlockSpec((W_GRP, S5_BLOCK), lambda j: (0, j))
    cspec = pl.BlockSpec((S5_BLOCK, W_GRP), lambda j: (j, 0))
    aspec = pl.BlockSpec((1, S5_BLOCK), lambda j: (0, j))
    a8spec = pl.BlockSpec((SUBLANES, S5_BLOCK), lambda j: (0, j))
    sd = jax.ShapeDtypeStruct
    return pl.pallas_call(
        body, name="s5_bwd", grid=(nblk,),
        out_shape=(sd((SEQ, W_GRP), F32), sd((W_GRP, S5_LANES), F32), sd((W_GRP, S5_LANES), F32),
                   sd((S5_LANES, W_GRP), F32), sd((S5_LANES, W_GRP), F32),
                   sd((SUBLANES, S5_LANES), F32), sd((SUBLANES, S5_LANES), F32),
                   sd((1, W_GRP), F32), sd((W_GRP, W_GRP), F32), sd((1, W_GRP), F32)),
        in_specs=[_zslab(W_GRP, 6), _full((SEQ, W_GRP)), _full((SEQ, W_GRP)), lane_blk, lane_blk,
                  bspec, bspec, cspec, cspec, aspec, aspec,
                  _full((1, W_GRP)), _full((W_GRP, W_GRP)), _full((1, W_GRP))],
        out_specs=(_full((SEQ, W_GRP)), bspec, bspec, cspec, cspec, a8spec, a8spec,
                   _full((1, W_GRP)), _full((W_GRP, W_GRP)), _full((1, W_GRP))),
        scratch_shapes=[pltpu.VMEM((SEQ, W_GRP), BF16), pltpu.VMEM((SEQ, W_GRP), BF16), pltpu.VMEM((SEQ, W_GRP), F32),
                        pltpu.VMEM((SEQ, S5_BLOCK), F32), pltpu.VMEM((SEQ, S5_BLOCK), F32)],
        compiler_params=_params("arbitrary"),
    )(z, y0, dy, xr, xi, b_re, b_im, c_re, c_im, a_re, a_im, d, glu_w, glu_b)


def _head(x, g, target):
    tm = TOKEN_TILE
    n = SEQ // tm

    def body(x_ref, g_ref, t_ref, dx_ref, st_ref, acc_scr):
        i = pl.program_id(0)

        @pl.when(i == 0)
        def _():
            acc_scr[...] = jnp.zeros_like(acc_scr)

        xhat, r = _rms(x_ref[...])
        gv = g_ref[...]
        err = xhat * gv - t_ref[...]
        dyv = err * (1.0 / D_MODEL)
        dx_ref[...] = _rms_bwd(xhat, r, dyv * gv)
        acc_scr[0:1, :] += _sum0(err * err)
        acc_scr[1:2, :] += _sum0(dyv * xhat)

        @pl.when(i == n - 1)
        def _():
            st_ref[...] = acc_scr[...]
            tot = jnp.sum(acc_scr[0:1, :], axis=-1, keepdims=True) * (0.5 / D_MODEL)
            st_ref[0:1, :] = jnp.broadcast_to(tot, (1, D_MODEL))

    tok = pl.BlockSpec((tm, D_MODEL), lambda i: (i, 0))
    return pl.pallas_call(
        body, name="head", grid=(n,),
        out_shape=(jax.ShapeDtypeStruct((SEQ, D_MODEL), F32), jax.ShapeDtypeStruct((SUBLANES, D_MODEL), F32)),
        in_specs=[tok, _row_spec(1), tok], out_specs=(tok, _row_spec(SUBLANES)),
        scratch_shapes=[pltpu.VMEM((SUBLANES, D_MODEL), F32)],
        compiler_params=_params("arbitrary"),
    )(x, g, target)


def _adamw(w, gparts, m, v, name):
    r, c = w.shape
    npart = gparts.shape[0]
    tr = r
    for cand in (512, 256, 128, 64, 32, 16):
        if r % cand == 0 and r > cand:
            tr = cand
            break
    b1c = 1.0 - ADAM_B1 ** ADAM_STEP
    b2c = 1.0 - ADAM_B2 ** ADAM_STEP

    def body(w_ref, g_ref, m_ref, v_ref, go_ref, d_ref, mo_ref, vo_ref):
        g = g_ref[0].astype(F32)
        for k in range(1, npart):
            g = g + g_ref[k].astype(F32)
        mn = ADAM_B1 * m_ref[...] + (1.0 - ADAM_B1) * g
        vn = ADAM_B2 * v_ref[...] + (1.0 - ADAM_B2) * (g * g)
        m_hat = mn / b1c
        v_hat = vn / b2c
        go_ref[...] = g
        d_ref[...] = -ADAM_LR * (m_hat / (jnp.sqrt(v_hat) + ADAM_EPS) + ADAM_WD * w_ref[...])
        mo_ref[...] = mn
        vo_ref[...] = vn

    blk = pl.BlockSpec((tr, c), lambda i: (i, 0))
    sh = jax.ShapeDtypeStruct((r, c), F32)
    return pl.pallas_call(
        body, name=name, grid=(r // tr,),
        out_shape=(sh, sh, sh, sh),
        in_specs=[blk, pl.BlockSpec((npart, tr, c), lambda i: (0, i, 0)), blk, blk],
        out_specs=(blk, blk, blk, blk), compiler_params=_params("arbitrary"),
    )(w, gparts, m, v)


def _adamw_layer(l, w, gparts, m, v, prev, name):
    _, r, c = w.shape
    tr = max(t for t in range(16, 513, 16) if r % t == 0)
    b1c = 1.0 - ADAM_B1 ** ADAM_STEP
    b2c = 1.0 - ADAM_B2 ** ADAM_STEP
    nprev = 0 if prev is None else 4

    def body(core_ref, *refs):
        w_ref, m_ref, v_ref = refs[:3]
        g_refs = refs[3:3 + N_CHIPS]
        go_ref, d_ref, mo_ref, vo_ref = refs[3 + N_CHIPS + nprev:]
        g = g_refs[0][0].astype(F32)
        for g_ref in g_refs[1:]:
            g = g + g_ref[0].astype(F32)
        mn = ADAM_B1 * m_ref[0] + (1.0 - ADAM_B1) * g
        vn = ADAM_B2 * v_ref[0] + (1.0 - ADAM_B2) * (g * g)
        go_ref[0] = g
        d_ref[0] = -ADAM_LR * ((mn / b1c) / (jnp.sqrt(vn / b2c) + ADAM_EPS) + ADAM_WD * w_ref[0])
        mo_ref[0] = mn
        vo_ref[0] = vn

    blk = pl.BlockSpec((1, tr, c), lambda i, core: (l, i, 0))
    slots = [pl.BlockSpec((1, tr, c), functools.partial(lambda i, core, q: (2 * q + core[0], i, 0), q=q))
             for q in range(N_CHIPS)]
    sh = jax.ShapeDtypeStruct(w.shape, F32)
    keep = [pl.BlockSpec(memory_space=pl.ANY)] * nprev
    return pl.pallas_call(
        body, name=name, out_shape=(sh, sh, sh, sh),
        grid_spec=pltpu.PrefetchScalarGridSpec(num_scalar_prefetch=1, grid=(r // tr,),
                                               in_specs=[blk, blk, blk, *slots, *keep], out_specs=(blk, blk, blk, blk)),
        input_output_aliases={4 + N_CHIPS + k: k for k in range(nprev)},
        compiler_params=_params("arbitrary"),
    )(_core_index(), w, m, v, *([gparts] * N_CHIPS), *(prev or ()))


def _sum_parts(parts, name):
    n, r, c = parts.shape

    def body(p_ref, o_ref):
        acc = p_ref[0]
        for k in range(1, n):
            acc = acc + p_ref[k]
        o_ref[...] = acc

    return pl.pallas_call(
        body, name=name, out_shape=jax.ShapeDtypeStruct((r, c), F32), compiler_params=_params(),
    )(parts)


def _block_diag(blocks):
    g, a, b = blocks.shape
    eye = jnp.eye(g, dtype=blocks.dtype)
    return (blocks[:, :, None, :] * eye[:, None, :, None]).reshape(g * a, g * b)


def _diag_blocks(dense, g):
    a, b = dense.shape[0] // g, dense.shape[1] // g
    d4 = dense.reshape(g, a, g, b)
    eye = jnp.eye(g, dtype=dense.dtype)
    return jnp.sum(d4 * eye[:, None, :, None], axis=2)


def _pack(parts, cols):
    flat = jnp.concatenate([p.reshape(-1) for p in parts])
    unit = N_DEV * SUBLANES * cols
    total = -(-flat.shape[0] // unit) * unit
    flat = jnp.pad(flat, (0, total - flat.shape[0]))
    return flat.reshape(N_DEV, total // (N_DEV * cols), cols)


def _unpack(flat, shapes):
    out, pos = [], 0
    for s in shapes:
        n = math.prod(s)
        out.append(flat[pos:pos + n].reshape(s))
        pos += n
    return out


SMALL = ("norm1_g", "norm2_g", "sgu_w", "sgu_b", "pool_w", "pool_scale", "s5_lambda_re", "s5_lambda_im",
         "s5_b_re", "s5_b_im", "s5_c_re", "s5_c_im", "s5_d", "s5_log_dt", "s5_glu_b", "mix_norm_g",
         "norm3_g", "final_norm_g", "conv_w", "s5_glu_w")
N_SMALL_WHOLE = len(SMALL) - 2
BIG = ("ffn1_w_in", "ffn1_w_out", "w_mix_in", "w_mix_out", "ffn2_w_in", "ffn2_w_out")
TRANSPOSED = ("ffn1_w_in", "w_mix_in", "ffn2_w_in")
WEIGHTS = ("ada_w", "ada_b", "norm1_g", "ffn1_w_in", "ffn1_w_out", "norm2_g", "w_mix_in", "sgu_w", "sgu_b", "pool_w",
           "pool_scale", "conv_w", "s5_lambda_re", "s5_lambda_im", "s5_b_re", "s5_b_im", "s5_c_re", "s5_c_im", "s5_d",
           "s5_log_dt", "s5_glu_w", "s5_glu_b", "mix_norm_g", "w_mix_out", "norm3_g", "ffn2_w_in", "ffn2_w_out",
           "final_norm_g")
PACK_COLS = 1024


def kernel(x, c, ada_w, ada_b, norm1_g, ffn1_w_in, ffn1_w_out, norm2_g, w_mix_in, sgu_w, sgu_b, pool_w, pool_scale, conv_w, s5_lambda_re, s5_lambda_im, s5_b_re, s5_b_im, s5_c_re, s5_c_im, s5_d, s5_log_dt, s5_glu_w, s5_glu_b, mix_norm_g, w_mix_out, norm3_g, ffn2_w_in, ffn2_w_out, final_norm_g, loss_target, m_ada_w, m_ada_b, m_norm1_g, m_ffn1_w_in, m_ffn1_w_out, m_norm2_g, m_w_mix_in, m_sgu_w, m_sgu_b, m_pool_w, m_pool_scale, m_conv_w, m_s5_lambda_re, m_s5_lambda_im, m_s5_b_re, m_s5_b_im, m_s5_c_re, m_s5_c_im, m_s5_d, m_s5_log_dt, m_s5_glu_w, m_s5_glu_b, m_mix_norm_g, m_w_mix_out, m_norm3_g, m_ffn2_w_in, m_ffn2_w_out, m_final_norm_g, v_ada_w, v_ada_b, v_norm1_g, v_ffn1_w_in, v_ffn1_w_out, v_norm2_g, v_w_mix_in, v_sgu_w, v_sgu_b, v_pool_w, v_pool_scale, v_conv_w, v_s5_lambda_re, v_s5_lambda_im, v_s5_b_re, v_s5_b_im, v_s5_c_re, v_s5_c_im, v_s5_d, v_s5_log_dt, v_s5_glu_w, v_s5_glu_b, v_mix_norm_g, v_w_mix_out, v_norm3_g, v_ffn2_w_in, v_ffn2_w_out, v_final_norm_g):
    args = dict(locals())
    W = {n: args[n] for n in WEIGHTS}
    M = {n: args["m_" + n] for n in WEIGHTS}
    V = {n: args["v_" + n] for n in WEIGHTS}
    me = _me()
    L = DEPTH
    x0 = x[0]
    target = loss_target[0]

    conv_cols = conv_w.shape[-1]
    glu_rows = s5_glu_w.shape[1]
    c_g, conv_g, glu_g = _exchange(
        [c.reshape(SUBLANES, LANES), conv_w.reshape(L * 3, conv_cols), s5_glu_w.reshape(L * glu_rows, W_GRP)],
        False, "gather_small")
    c_all = c_g.reshape(N_DEV, D_MODEL)
    conv_full = conv_g.reshape(N_DEV, L, 3, conv_cols).transpose(1, 2, 0, 3).reshape(L, 3, W_GRP)
    glu_full = glu_g.reshape(N_DEV, L, glu_rows, W_GRP).transpose(1, 0, 2, 3).reshape(L, W_GRP, W_GRP)

    ncol = ada_w.shape[-1]
    ada_b_mine = lax.dynamic_slice_in_dim(ada_b, me * ncol, ncol, axis=1).reshape(L, 1, ncol)
    cond_part = _cond_fwd(c_all, ada_w, ada_b_mine)
    (cond_g,) = _exchange([cond_part.reshape(L * N_DEV, ncol)], False, "gather_cond")
    cond_g = cond_g.reshape(N_DEV, L, N_DEV, ncol)
    cond_mine = lax.dynamic_index_in_dim(cond_g, me, axis=2, keepdims=False)
    cond = cond_mine.transpose(1, 0, 2).reshape(L, N_ADA, D_MODEL)

    lg = L * S5_GROUPS
    lre3 = s5_lambda_re.reshape(lg, S5_STATE, 1)
    lim3 = s5_lambda_im.reshape(lg, S5_STATE, 1)
    ldt3 = s5_log_dt.reshape(lg, 1, 1)
    br3 = s5_b_re.reshape(lg, S5_STATE, S5_GROUP_CH)
    bi3 = s5_b_im.reshape(lg, S5_STATE, S5_GROUP_CH)

    def b_mat(bb3, l):
        return _block_diag(bb3.reshape(L, S5_GROUPS, S5_STATE, S5_GROUP_CH)[l].transpose(0, 2, 1)).astype(BF16)

    def c_mat(cw, l):
        return _block_diag(cw[l].transpose(0, 2, 1)).astype(BF16)

    for n in TRANSPOSED:
        W[n], M[n], V[n] = (a.transpose(0, 2, 1) for a in (W[n], M[n], V[n]))

    gathered_shape = {"ffn1_w_out": (N_FF_CHUNK, FF_PIECE, D_MODEL), "ffn2_w_out": (N_FF_CHUNK, FF_PIECE, D_MODEL),
                      "w_mix_in": (P_IN, D_MODEL), "w_mix_out": (D_MODEL, D_MODEL)}

    def gather_start(l, after, names=BIG):
        srcs = [W[n][l].astype(BF16) for n in names]
        lands = _place_own(srcs, False, "gather_weights_own")
        return _exchange_start(srcs, lands, after, False, "gather_weights_start", OTHER_CHIPS)

    def gather_finish(handle, after, names=BIG):
        g = _exchange_wait(handle, after, False, "gather_weights_wait", OTHER_CHIPS)
        out = dict(zip(names, _gather_sibling(g, "gather_weights_sibling")))
        return {n: (a.reshape(gathered_shape[n]) if n in gathered_shape else a) for n, a in out.items()}

    saved = []
    xc = x0
    first_ffn = BIG[:2]
    handle, token = gather_start(0, cond, first_ffn)
    handle_rest, token = gather_start(0, token, BIG[2:])

    zero = token[0, 0]
    a_re3, a_im3, bb_re3, bb_im3 = _s5_prep_fwd(lre3 + zero, lim3, ldt3, br3, bi3)
    a_re = a_re3.reshape(L, 1, S5_LANES)
    a_im = a_im3.reshape(L, 1, S5_LANES)
    sgu_w_t, pool_w_t, c_re_t, c_im_t = (a + zero for a in (sgu_w, pool_w, s5_c_re, s5_c_im))

    def mixer_consts(l):
        w_cat = sgu_w_t[l].transpose(1, 0, 2).reshape(CHUNK, SGU_HEADS * CHUNK)
        bias = jnp.repeat(sgu_b[l].T, SGU_HEAD_DIM, axis=1)
        return dict(
            w_cat=w_cat, bias=bias, pool_bd=_block_diag(pool_w_t[l]).astype(BF16), pool_scale=pool_scale[l][None],
            conv=conv_full[l], b_re=b_mat(bb_re3, l), b_im=b_mat(bb_im3, l), c_re=c_mat(c_re_t, l),
            c_im=c_mat(c_im_t, l), a_re=a_re[l], a_im=a_im[l], d=s5_d[l][None], glu_w=glu_full[l].astype(BF16),
            glu_b=s5_glu_b[l][None])

    mcs = [mixer_consts(l) for l in range(L)]
    small_names = SMALL + ("ada_b",)
    first_small = W[small_names[0]] + zero
    packed_w = _pack([first_small] + [W[n] for n in small_names[1:]], PACK_COLS)
    packed_m = _pack([M[n] + zero if n == small_names[0] else M[n] for n in small_names], PACK_COLS)
    packed_v = _pack([V[n] + zero if n == small_names[0] else V[n] for n in small_names], PACK_COLS)
    wl = gather_finish(handle, [x0] + [a for mc in mcs for a in mc.values()], first_ffn)
    for l in range(L):
        cl = cond[l]
        if 0 < l < L - 1:
            handle, token = gather_start(l + 1, wl["ffn1_w_in"])
            cl = cl + token[0, 0]
        mc = mcs[l]
        x_a = xc
        x_b, *ffn1_kept = _ffn_fwd(x_a, cl[0:3], norm1_g[l][None], wl["ffn1_w_in"], wl["ffn1_w_out"])
        if l == 0:
            wl = {**wl, **gather_finish(handle_rest, [x_b, packed_w, packed_m, packed_v], BIG[2:])}
            handle, token = gather_start(1, wl["w_mix_in"])
            cl = cl + token[0, 0]
        z = _mix_in_fwd(x_b, cl[3:5], norm2_g[l][None], wl["w_mix_in"])
        ya = _sgu_fwd(z, mc["w_cat"], mc["bias"])
        yb = _pool_fwd(z, mc["pool_bd"], mc["pool_scale"])
        yc = _conv_fwd(z, mc["conv"])
        yd, y0, sxr, sxi = _s5_fwd(z, mc["b_re"], mc["b_im"], mc["c_re"], mc["c_im"], mc["a_re"], mc["a_im"],
                                   mc["d"], mc["glu_w"], mc["glu_b"])
        ys = (ya, yb, yc, yd)
        x_c = _mix_out_fwd(x_b, ys, mix_norm_g[l][None], cl[5:6], wl["w_mix_out"])
        x_d, *ffn2_kept = _ffn_fwd(x_c, cl[6:9], norm3_g[l][None], wl["ffn2_w_in"], wl["ffn2_w_out"])
        saved.append(dict(wl=wl, mc=mc, x_a=x_a, x_b=x_b, x_c=x_c, ffn1=ffn1_kept, ffn2=ffn2_kept, z=z, ys=ys, y0=y0,
                          xr=sxr, xi=sxi))
        xc = x_d
        if l + 1 < L:
            wl = gather_finish(handle, x_d)

    dx, stats = _head(xc, final_norm_g[None], target)
    loss = lax.psum(stats[0, 0], MESH_AXES)

    small_grads = {n: [None] * L for n in SMALL if n != "final_norm_g"}
    small_grads["final_norm_g"] = stats[1]
    big_out = {n: None for n in BIG}
    pending = None

    def swap_begin(pieces, after):
        bufs = _place_own(pieces, True, "scatter_grads_own")
        return _exchange_start(pieces, bufs, after, True, "swap_grads_start", SIBLING_SWAP)

    def scatter_begin(swap_handle, after):
        mine, theirs = _exchange_wait(swap_handle, after, True, "swap_grads_wait", SIBLING_SWAP, with_srcs=True)
        sums = _presum(mine, theirs, "presum_grads")
        lands = _place_own(sums, True, "scatter_grads_own")
        return _exchange_start(sums, lands, theirs[0], True, "scatter_grads_start", OTHER_CHIPS)

    def finish_scatter(pend, after):
        layer, hnd = pend
        recv = _exchange_wait(hnd, after, True, "scatter_grads_wait", OTHER_CHIPS)
        for n, r in zip(BIG, recv):
            big_out[n] = _adamw_layer(layer, W[n], r, M[n], V[n], big_out[n], "adamw_" + n)
        return recv[0]

    dcond_rows = [None] * L
    d_are, d_aim, d_bbre, d_bbim = [None] * L, [None] * L, [None] * L, [None] * L
    swap = None
    for l in reversed(range(L)):
        sv = saved[l]
        wl, mc, cl = sv["wl"], sv["mc"], cond[l]
        if swap is not None:
            cl = cl + swap[1][0, 0]
        f2, a2, b2, hb = sv["ffn2"]
        dx, da, db, act, dob, part3 = _ffn_bwd(dx, sv["x_c"], f2, a2, b2, cl[6:9], norm3_g[l][None],
                                               wl["ffn2_w_in"], wl["ffn2_w_out"])
        if swap is not None:
            handle, token = scatter_begin(swap[0], dx)
            pending = (l + 1, handle)
            cl = cl + token[0, 0]
        g_ffn2_in = _dw(da, hb, db, name="dw_ffn_in")
        g_ffn2_out = _dw(act, dob, name="dw_ffn_out").reshape(N_DEV, D_FF // N_DEV, D_MODEL)
        dya, dyb, dyc, dyd, ynb, dmob, part_mo = _mix_out_bwd(dx, sv["ys"], mix_norm_g[l][None], cl[5:6],
                                                              wl["w_mix_out"])
        g_mix_out = _dw(ynb[None], dmob, name="dw_mix_out").reshape(N_DEV, D_MODEL // N_DEV, D_MODEL)
        z = sv["z"]
        dza, dw_cat, dbias = _sgu_bwd(z, dya, mc["w_cat"], mc["bias"])
        dzb, dpool_dense, dpool_scale = _pool_bwd(z, dyb, mc["pool_bd"], mc["pool_scale"])
        dzc, dconv8 = _conv_bwd(z, dyc, mc["conv"])
        (dzd, dbre_d, dbim_d, dcre_d, dcim_d, dar8, dai8, dd, dglu_w, dglu_b) = _s5_bwd(
            z, sv["y0"], dyd, sv["xr"], sv["xi"], mc["b_re"], mc["b_im"], mc["c_re"], mc["c_im"],
            mc["a_re"], mc["a_im"], mc["d"], mc["glu_w"], mc["glu_b"])
        dx, h2b, dzbf, part2 = _mix_in_bwd((dza, dzb, dzc, dzd), sv["x_b"], dx, cl[3:5], norm2_g[l][None],
                                           wl["w_mix_in"])
        g_mix_in = _dw(dzbf[None], h2b, name="dw_mix_in").reshape(N_DEV, P_IN // N_DEV, D_MODEL)
        f1, a1, b1, hb = sv["ffn1"]
        dx, da, db, act, dob, part1 = _ffn_bwd(dx, sv["x_a"], f1, a1, b1, cl[0:3], norm1_g[l][None],
                                               wl["ffn1_w_in"], wl["ffn1_w_out"])
        g_ffn1_in = _dw(da, hb, db, name="dw_ffn_in")
        g_ffn1_out = _dw(act, dob, name="dw_ffn_out").reshape(N_DEV, D_FF // N_DEV, D_MODEL)
        last = finish_scatter(pending, g_ffn1_in) if pending is not None else dx
        swap = swap_begin([g_ffn1_in, g_ffn1_out, g_mix_in, g_mix_out, g_ffn2_in, g_ffn2_out], last)
        dcond_rows[l] = jnp.concatenate([part1[0:3], part2[0:2], part_mo[0:1], part3[0:3]], axis=0)
        sg = small_grads
        sg["norm1_g"][l] = part1[3]
        sg["norm2_g"][l] = part2[2]
        sg["norm3_g"][l] = part3[3]
        sg["mix_norm_g"][l] = part_mo[1]
        sg["sgu_w"][l] = dw_cat.reshape(CHUNK, SGU_HEADS, CHUNK).transpose(1, 0, 2)
        sg["sgu_b"][l] = dbias[:, ::SGU_HEAD_DIM].T
        sg["pool_w"][l] = _diag_blocks(dpool_dense, len(POOL_WINDOWS))
        sg["pool_scale"][l] = dpool_scale[0]
        sg["conv_w"][l] = dconv8[0:3]
        sg["s5_c_re"][l] = _diag_blocks(dcre_d, S5_GROUPS).transpose(0, 2, 1)
        sg["s5_c_im"][l] = _diag_blocks(dcim_d, S5_GROUPS).transpose(0, 2, 1)
        sg["s5_d"][l] = dd[0]
        sg["s5_glu_w"][l] = dglu_w
        sg["s5_glu_b"][l] = dglu_b[0]
        d_are[l], d_aim[l] = dar8[0], dai8[0]
        d_bbre[l] = _diag_blocks(dbre_d, S5_GROUPS).transpose(0, 2, 1)
        d_bbim[l] = _diag_blocks(dbim_d, S5_GROUPS).transpose(0, 2, 1)
    grad_x = dx

    g_lre, g_lim, g_ldt, g_br, g_bi = _s5_prep_bwd(
        lre3, lim3, ldt3, br3, bi3,
        jnp.stack(d_are).reshape(lg, S5_STATE, 1), jnp.stack(d_aim).reshape(lg, S5_STATE, 1),
        jnp.stack(d_bbre).reshape(lg, S5_STATE, S5_GROUP_CH), jnp.stack(d_bbim).reshape(lg, S5_STATE, S5_GROUP_CH))
    small = {n: (jnp.stack(v) if isinstance(v, list) and v[0] is not None else v) for n, v in small_grads.items()}
    small["s5_lambda_re"] = g_lre.reshape(s5_lambda_re.shape)
    small["s5_lambda_im"] = g_lim.reshape(s5_lambda_im.shape)
    small["s5_log_dt"] = g_ldt.reshape(s5_log_dt.shape)
    small["s5_b_re"] = g_br.reshape(s5_b_re.shape)
    small["s5_b_im"] = g_bi.reshape(s5_b_im.shape)

    small_shapes = [(L, 3, W_GRP) if n == "conv_w" else (L, W_GRP, W_GRP) if n == "s5_glu_w" else W[n].shape
                    for n in SMALL]
    packed = _pack([small[n].reshape(s) for n, s in zip(SMALL, small_shapes)], PACK_COLS) + swap[1][0, 0]
    (pieces,) = _exchange([packed], True, "scatter_small")
    mine = _sum_parts(pieces, "sum_small")
    dcond = jnp.stack(dcond_rows).reshape(L * N_ADA, D_MODEL)
    summed, dcond_g = _exchange([mine, dcond], False, "gather_small_sums")
    summed_flat = summed.reshape(-1)
    n_whole = sum(math.prod(s) for s in small_shapes[:N_SMALL_WHOLE])
    conv_sum, glu_sum = _unpack(summed_flat[n_whole:], small_shapes[N_SMALL_WHOLE:])
    conv_sum = lax.dynamic_slice_in_dim(conv_sum, me * conv_cols, conv_cols, axis=2)
    glu_sum = lax.dynamic_slice_in_dim(glu_sum, me * glu_rows, glu_rows, axis=1)

    handle, token = scatter_begin(swap[0], [summed, dcond_g])
    pending = (0, handle)
    dcond_all = dcond_g.reshape(N_DEV, L, N_ADA * D_MODEL).transpose(1, 0, 2)
    dcond_mine = lax.dynamic_slice_in_dim(dcond_all, me * ncol, ncol, axis=2) + token[0, 0]
    g_ada_w, g_ada_b = _cond_bwd(c_all.T, dcond_mine, dcond_all)

    grads, deltas, new_m, new_v = {}, {}, {}, {}
    out = _adamw(ada_w.reshape(L * D_MODEL, ncol), g_ada_w.reshape(1, L * D_MODEL, ncol),
                 m_ada_w.reshape(L * D_MODEL, ncol), v_ada_w.reshape(L * D_MODEL, ncol), "adamw_ada_w")
    grads["ada_w"], deltas["ada_w"], new_m["ada_w"], new_v["ada_w"] = (o.reshape(ada_w.shape) for o in out)
    shapes = [W[n].shape for n in small_names]
    rows = packed_w.shape[0] * packed_w.shape[1]
    packed_g = _pack([summed_flat[:n_whole], conv_sum, glu_sum, g_ada_b], PACK_COLS)
    out = _adamw(packed_w.reshape(rows, PACK_COLS), packed_g.reshape(1, rows, PACK_COLS),
                 packed_m.reshape(rows, PACK_COLS), packed_v.reshape(rows, PACK_COLS), "adamw_small")
    for store, o in zip((grads, deltas, new_m, new_v), out):
        store.update(zip(small_names, _unpack(o.reshape(-1), shapes)))
    done = [store[n] for store in (grads, deltas, new_m, new_v) for n in small_names + ("ada_w",)]
    finish_scatter(pending, done + [big_out[n][0] for n in BIG])
    for n in BIG:
        res = big_out[n]
        if n in TRANSPOSED:
            res = tuple(r.transpose(0, 2, 1) for r in res)
        grads[n], deltas[n], new_m[n], new_v[n] = res

    return (loss, grad_x[None], *[grads[n] for n in WEIGHTS], *[deltas[n] for n in WEIGHTS],
            *[new_m[n] for n in WEIGHTS], *[new_v[n] for n in WEIGHTS])
```

```python
import functools
import math

import jax
import jax.numpy as jnp
from jax import lax
from jax.experimental import pallas as pl
from jax.experimental.pallas import tpu as pltpu

F32 = jnp.float32
BF16 = jnp.bfloat16

D_MODEL = 1024
SEQ = 2048
DEPTH = 4
N_DEV = 8
W_GRP = 256
CHUNK = 128
SGU_HEADS = 4
SGU_HEAD_DIM = 64
POOL_WINDOWS = (2, 4, 8, 16)
POOL_GROUP_DIM = 64
S5_GROUPS = 16
S5_GROUP_CH = 16
S5_STATE = 64
S5_LANES = S5_GROUPS * S5_STATE
S5_BLOCK = 256
P_IN = 1792
D_FF = 2816
FF_PIECE = 2 * D_FF // N_DEV
N_FF_CHUNK = D_FF // FF_PIECE
N_ADA = 9
EPS = 1e-6
ADAM_LR = 0.001
ADAM_B1 = 0.9
ADAM_B2 = 0.999
ADAM_EPS = 1e-08
ADAM_WD = 0.01
ADAM_STEP = 10

SUBLANES = 8
LANES = 128
VMEM_LIMIT = 56 * 1024 * 1024
TOKEN_TILE = 512
ROW_SUBTILE = 256
HIGHEST = lax.Precision.HIGHEST
MESH_AXES = ("x", "y", "c")

_GELU_C = math.sqrt(2.0 / math.pi)
_GELU_A = 0.044715


def _params(*sem):
    return pltpu.CompilerParams(dimension_semantics=tuple(sem) if sem else None, vmem_limit_bytes=VMEM_LIMIT)


def _dot(a, b):
    return jnp.dot(a, b, preferred_element_type=F32)


def _dot_nt(a, b):
    return lax.dot_general(a, b, (((1,), (1,)), ((), ())), preferred_element_type=F32)


def _dot_tn(a, b):
    return lax.dot_general(a, b, (((0,), (0,)), ((), ())), preferred_element_type=F32)


def _dot_hi(a, b):
    return jnp.dot(a, b, preferred_element_type=F32, precision=HIGHEST)


def _sigmoid(x):
    return 1.0 / (1.0 + jnp.exp(-x))


def _gelu(x):
    return 0.5 * x * (1.0 + jnp.tanh(_GELU_C * (x + _GELU_A * x * x * x)))


def _gelu_grad(x):
    t = jnp.tanh(_GELU_C * (x + _GELU_A * x * x * x))
    return 0.5 * (1.0 + t) + 0.5 * x * (1.0 - t * t) * (_GELU_C * (1.0 + 3.0 * _GELU_A * x * x))


def _rms(x):
    r = lax.rsqrt(jnp.mean(x * x, axis=-1, keepdims=True) + EPS)
    return x * r, r


def _rms_bwd(xhat, r, dxhat):
    return r * (dxhat - xhat * jnp.mean(dxhat * xhat, axis=-1, keepdims=True))


def _sum0(x):
    return jnp.sum(x, axis=0, keepdims=True)


def _me():
    return 4 * lax.axis_index("x") + 2 * lax.axis_index("y") + lax.axis_index("c")


def _exchange(srcs, scatter, name):
    n = len(srcs)
    out_shapes = []
    for s in srcs:
        piece = s.shape[1:] if scatter else s.shape
        out_shapes.append(jax.ShapeDtypeStruct((N_DEV,) + tuple(piece), s.dtype))

    def body(*refs):
        ins, outs = refs[:n], refs[n:2 * n]
        send_sems, recv_sems, local_sems = refs[2 * n:]
        x, y, c = lax.axis_index("x"), lax.axis_index("y"), lax.axis_index("c")
        me = 4 * x + 2 * y + c

        def src_of(i, dev):
            return ins[i].at[dev] if scatter else ins[i]

        local = [pltpu.make_async_copy(src_of(i, me), outs[i].at[me], local_sems.at[i]) for i in range(n)]
        for cp in local:
            cp.start()
        sends, recvs = [], []
        for k in range(1, N_DEV):
            px = 1 - x if (k >> 2) & 1 else x
            py = 1 - y if (k >> 1) & 1 else y
            pc = 1 - c if k & 1 else c
            peer = 4 * px + 2 * py + pc
            for i in range(n):
                sends.append(pltpu.make_async_remote_copy(
                    src_ref=src_of(i, peer), dst_ref=outs[i].at[me],
                    send_sem=send_sems.at[k - 1, i], recv_sem=recv_sems.at[k - 1, i],
                    device_id=(px, py, pc), device_id_type=pl.DeviceIdType.MESH))
                recvs.append(pltpu.make_async_remote_copy(
                    src_ref=src_of(i, peer), dst_ref=outs[i].at[peer],
                    send_sem=send_sems.at[k - 1, i], recv_sem=recv_sems.at[k - 1, i],
                    device_id=(px, py, pc), device_id_type=pl.DeviceIdType.MESH))
        for cp in sends:
            cp.start()
        for cp in recvs:
            cp.wait_recv()
        for cp in sends:
            cp.wait_send()
        for cp in local:
            cp.wait()

    hbm = pl.BlockSpec(memory_space=pltpu.HBM)
    return pl.pallas_call(
        body, name=name, out_shape=out_shapes,
        in_specs=[hbm] * n, out_specs=[hbm] * n,
        scratch_shapes=[pltpu.SemaphoreType.DMA((N_DEV - 1, n)), pltpu.SemaphoreType.DMA((N_DEV - 1, n)),
                        pltpu.SemaphoreType.DMA((n,))],
    )(*srcs)


ALL_PEERS = tuple(range(1, N_DEV))
OTHER_CHIPS = (2, 4, 6)
GATHER_PEERS = (1, 2, 4, 6)


def _peers(which):
    x, y, c = lax.axis_index("x"), lax.axis_index("y"), lax.axis_index("c")
    out = []
    for k in which:
        px = 1 - x if (k >> 2) & 1 else x
        py = 1 - y if (k >> 1) & 1 else y
        pc = 1 - c if k & 1 else c
        out.append(((px, py, pc), 4 * px + 2 * py + pc))
    return out


SIBLING_SWAP = "sibling"
N_CHIPS = 4


def _swap_copies(ins, lands, send_sems, recv_sems, with_recvs):
    x, y, c = lax.axis_index("x"), lax.axis_index("y"), lax.axis_index("c")
    sends, recvs = [], []
    for q in range(N_CHIPS):
        for i in range(len(ins)):
            sems = dict(send_sem=send_sems.at[q * len(ins) + i], recv_sem=recv_sems.at[q * len(ins) + i],
                        device_id=(x, y, 1 - c), device_id_type=pl.DeviceIdType.MESH)
            theirs, mine = 2 * q + 1 - c, 2 * q + c
            sends.append(pltpu.make_async_remote_copy(src_ref=ins[i].at[theirs], dst_ref=lands[i].at[theirs], **sems))
            if with_recvs:
                recvs.append(pltpu.make_async_remote_copy(src_ref=ins[i].at[theirs], dst_ref=lands[i].at[mine], **sems))
    return sends, recvs


def _split_copies(ins, lands, send_sems, recv_sems, scatter, with_recvs, which):
    if which == SIBLING_SWAP:
        return _swap_copies(ins, lands, send_sems, recv_sems, with_recvs)
    me = _me()
    sends, recvs = [], []
    for j, (dev, peer) in enumerate(_peers(which)):
        for i in range(len(ins)):
            src = ins[i].at[peer] if scatter else ins[i]
            slot = j * len(ins) + i
            sems = dict(send_sem=send_sems.at[slot], recv_sem=recv_sems.at[slot],
                        device_id=dev, device_id_type=pl.DeviceIdType.MESH)
            sends.append(pltpu.make_async_remote_copy(src_ref=src, dst_ref=lands[i].at[me], **sems))
            if with_recvs:
                recvs.append(pltpu.make_async_remote_copy(src_ref=src, dst_ref=lands[i].at[peer], **sems))
    return sends, recvs


_HBM = pl.BlockSpec(memory_space=pltpu.HBM)
_SEM = pl.BlockSpec(memory_space=pltpu.SEMAPHORE)
_EFFECT = pltpu.SideEffectType.DATAFLOW_SIDE_EFFECTING


def _place_own(srcs, scatter, name):
    n = len(srcs)
    halves = 2
    out_shapes, in_specs, out_specs = [], [], []
    for s in srcs:
        r, c = s.shape[-2:]
        out_shapes.append(jax.ShapeDtypeStruct((N_DEV, r, c), s.dtype))
        if scatter:
            in_specs.append(pl.BlockSpec((1, r // halves, c), lambda i, me: (me[0], i, 0)))
        else:
            in_specs.append(pl.BlockSpec((r // halves, c), lambda i, me: (i, 0)))
        out_specs.append(pl.BlockSpec((1, r // halves, c), lambda i, me: (me[0], i, 0)))

    def body(me_ref, *refs):
        for i in range(n):
            refs[n + i][0] = refs[i][0] if scatter else refs[i][...]

    return pl.pallas_call(
        body, name=name, out_shape=out_shapes,
        grid_spec=pltpu.PrefetchScalarGridSpec(num_scalar_prefetch=1, grid=(halves,), in_specs=in_specs,
                                               out_specs=out_specs),
        compiler_params=_params("arbitrary"),
    )(_me().reshape(1).astype(jnp.int32), *srcs)


def _core_index():
    return lax.axis_index("c").reshape(1).astype(jnp.int32)


def _presum(pieces, received, name):
    n = len(pieces)
    halves = 2
    specs = []
    for s in pieces:
        _, r, c = s.shape
        specs.append(pl.BlockSpec((1, r // halves, c), lambda q, i, core: (2 * q + core[0], i, 0)))

    def body(core_ref, *refs):
        for i in range(n):
            refs[2 * n + i][...] = (refs[i][...].astype(F32) + refs[n + i][...].astype(F32)).astype(BF16)

    return pl.pallas_call(
        body, name=name, out_shape=[jax.ShapeDtypeStruct(s.shape, BF16) for s in pieces],
        grid_spec=pltpu.PrefetchScalarGridSpec(num_scalar_prefetch=1, grid=(N_CHIPS, halves), in_specs=specs + specs,
                                               out_specs=specs),
        compiler_params=_params("arbitrary", "arbitrary"),
    )(_core_index(), *pieces, *received)


def _exchange_start(srcs, lands, after, scatter, name, which=ALL_PEERS):
    n = len(srcs)

    def body(*refs):
        ins, land_in = refs[:n], refs[n:2 * n]
        send_sems, recv_sems = refs[2 * n + 1], refs[2 * n + 2]
        token = refs[-1]
        sends, _ = _split_copies(ins, land_in, send_sems, recv_sems, scatter, False, which)
        for cp in sends:
            cp.start()
        token[...] = jnp.zeros_like(token)

    sem = pltpu.SemaphoreType.DMA(((N_CHIPS if which == SIBLING_SWAP else len(which)) * n,))
    out = pl.pallas_call(
        body, name=name,
        out_shape=(sem, sem, *[pltpu.HBM(s.shape, s.dtype) for s in srcs], *[pltpu.HBM(s.shape, s.dtype) for s in lands],
                   jax.ShapeDtypeStruct((SUBLANES, LANES), F32)),
        in_specs=[_HBM] * (2 * n) + [pl.BlockSpec(memory_space=pl.ANY)],
        out_specs=(_SEM, _SEM, *[_HBM] * (2 * n), pl.BlockSpec(memory_space=pltpu.VMEM)),
        input_output_aliases={i: 2 + i for i in range(2 * n)},
        compiler_params=pltpu.CompilerParams(has_side_effects=_EFFECT),
    )(*srcs, *lands, after)
    return (out[0], out[1], out[2:2 + n], out[2 + n:2 + 2 * n]), out[-1]


def _exchange_wait(handle, after, scatter, name, which=ALL_PEERS, with_srcs=False):
    send_sems, recv_sems, srcs, lands = handle
    n = len(srcs)
    after = list(after) if isinstance(after, (list, tuple)) else [after]

    def body(*refs):
        ins, land_in = refs[:n], refs[n:2 * n]
        sends, recvs = _split_copies(ins, land_in, refs[2 * n], refs[2 * n + 1], scatter, True, which)
        for cp in sends:
            cp.wait_send()
        for cp in recvs:
            cp.wait_recv()

    out = pl.pallas_call(
        body, name=name,
        out_shape=(*[pltpu.HBM(s.shape, s.dtype) for s in srcs], *[pltpu.HBM(s.shape, s.dtype) for s in lands]),
        in_specs=[_HBM] * (2 * n) + [_SEM, _SEM] + [pl.BlockSpec(memory_space=pl.ANY)] * len(after),
        out_specs=tuple([_HBM] * (2 * n)),
        input_output_aliases={i: i for i in range(2 * n)},
        compiler_params=pltpu.CompilerParams(has_side_effects=_EFFECT),
    )(*srcs, *lands, send_sems, recv_sems, *after)
    return (out[:n], out[n:]) if with_srcs else out[n:]


def _gather_sibling(lands, name):
    n = len(lands)
    chips = ((0, 1), (1, 0), (1, 1))

    def body(*refs):
        ins, outs = refs[:n], refs[n:2 * n]
        send_sems, recv_sems = refs[2 * n], refs[2 * n + 1]
        x, y, c = lax.axis_index("x"), lax.axis_index("y"), lax.axis_index("c")
        sends, recvs = [], []
        for j, (bx, by) in enumerate(chips):
            chip = 4 * (1 - x if bx else x) + 2 * (1 - y if by else y)
            for i in range(n):
                sems = dict(send_sem=send_sems.at[j * n + i], recv_sem=recv_sems.at[j * n + i],
                            device_id=(x, y, 1 - c), device_id_type=pl.DeviceIdType.MESH)
                sends.append(pltpu.make_async_remote_copy(src_ref=ins[i].at[chip + c], dst_ref=outs[i].at[chip + c], **sems))
                recvs.append(pltpu.make_async_remote_copy(src_ref=ins[i].at[chip + c], dst_ref=outs[i].at[chip + 1 - c],
                                                          **sems))
        for cp in sends:
            cp.start()
        for cp in recvs:
            cp.wait_recv()
        for cp in sends:
            cp.wait_send()

    return pl.pallas_call(
        body, name=name, out_shape=[jax.ShapeDtypeStruct(a.shape, a.dtype) for a in lands],
        in_specs=[_HBM] * n, out_specs=[_HBM] * n,
        scratch_shapes=[pltpu.SemaphoreType.DMA((len(chips) * n,)), pltpu.SemaphoreType.DMA((len(chips) * n,))],
        input_output_aliases={i: i for i in range(n)},
    )(*lands)


def _cond_fwd(c_all, ada_w, ada_b_mine):
    ncol = ada_w.shape[-1]

    def body(c_ref, w_ref, b_ref, o_ref):
        c = c_ref[...]
        ca = (c * _sigmoid(c)).astype(BF16)
        o_ref[0] = _dot(ca, w_ref[0].astype(BF16)) + b_ref[0]

    return pl.pallas_call(
        body, name="cond_fwd", grid=(DEPTH,),
        out_shape=jax.ShapeDtypeStruct((DEPTH, N_DEV, ncol), F32),
        in_specs=[pl.BlockSpec((N_DEV, D_MODEL), lambda l: (0, 0)),
                  pl.BlockSpec((1, D_MODEL, ncol), lambda l: (l, 0, 0)),
                  pl.BlockSpec((1, 1, ncol), lambda l: (l, 0, 0))],
        out_specs=pl.BlockSpec((1, N_DEV, ncol), lambda l: (l, 0, 0)),
        compiler_params=_params("arbitrary"),
    )(c_all, ada_w, ada_b_mine)


def _cond_bwd(c_all_t, dcond_mine, dcond_all):
    ncol = dcond_mine.shape[-1]
    nall = dcond_all.shape[-1]

    def body(ct_ref, d_ref, da_ref, gw_ref, gb_ref):
        ct = ct_ref[...]
        ct = ct * _sigmoid(ct)
        d = d_ref[0]
        acc = ct[:, 0:1] * d[0:1, :]
        for b in range(1, N_DEV):
            acc = acc + ct[:, b:b + 1] * d[b:b + 1, :]
        gw_ref[0] = acc
        gb_ref[0] = _sum0(da_ref[0])

    return pl.pallas_call(
        body, name="cond_bwd", grid=(DEPTH,),
        out_shape=(jax.ShapeDtypeStruct((DEPTH, D_MODEL, ncol), F32), jax.ShapeDtypeStruct((DEPTH, 1, nall), F32)),
        in_specs=[pl.BlockSpec((D_MODEL, N_DEV), lambda l: (0, 0)),
                  pl.BlockSpec((1, N_DEV, ncol), lambda l: (l, 0, 0)),
                  pl.BlockSpec((1, N_DEV, nall), lambda l: (l, 0, 0))],
        out_specs=(pl.BlockSpec((1, D_MODEL, ncol), lambda l: (l, 0, 0)),
                   pl.BlockSpec((1, 1, nall), lambda l: (l, 0, 0))),
        compiler_params=_params("arbitrary"),
    )(c_all_t, dcond_mine, dcond_all)


def _modnorm(x, g, shift, scale):
    xhat, r = _rms(x)
    return (xhat * g) * (1.0 + scale) + shift, xhat, r


def _modnorm_bwd(xhat, r, g, scale, dh):
    n = xhat * g
    dn = dh * (1.0 + scale)
    dx = _rms_bwd(xhat, r, dn * g)
    return dx, _sum0(dh), _sum0(dh * n), _sum0(dn * xhat)


def _row_spec(rows):
    return pl.BlockSpec((rows, D_MODEL), lambda *_: (0, 0))


def _ffn_fwd(x, cond3, g, w_in_g, w_out_g):
    tm = TOKEN_TILE
    last = N_FF_CHUNK - 1

    def body(x_ref, cond_ref, g_ref, wa_ref, wb_ref, wo_ref, xo_ref, f_ref, a_ref, b_ref, h_ref, h_scr, acc_scr):
        j = pl.program_id(1)

        @pl.when(j == 0)
        def _():
            h, _, _ = _modnorm(x_ref[...], g_ref[...], cond_ref[0:1, :], cond_ref[1:2, :])
            hb = h.astype(BF16)
            h_scr[...] = hb
            h_ref[...] = hb
            acc_scr[...] = jnp.zeros_like(acc_scr)

        wa, wb, wo = wa_ref[0], wb_ref[0], wo_ref[0]
        for r in range(tm // ROW_SUBTILE):
            rows = slice(r * ROW_SUBTILE, (r + 1) * ROW_SUBTILE)
            h = h_scr[rows, :]
            a = _dot_nt(h, wa)
            b = _dot_nt(h, wb)
            a_ref[0, rows, :] = a.astype(BF16)
            b_ref[0, rows, :] = b.astype(BF16)
            act = (a * _sigmoid(a)) * b
            acc_scr[rows, :] += _dot(act.astype(BF16), wo)

        @pl.when(j == last)
        def _():
            f = acc_scr[...]
            f_ref[...] = f
            xo_ref[...] = x_ref[...] + (0.5 * cond_ref[2:3, :]) * f

    tok = pl.BlockSpec((tm, D_MODEL), lambda i, j: (i, 0))
    chunk = pl.BlockSpec((1, tm, FF_PIECE), lambda i, j: (j, i, 0))
    chunk_shape = jax.ShapeDtypeStruct((N_FF_CHUNK, SEQ, FF_PIECE), BF16)
    return pl.pallas_call(
        body, name="ffn_fwd", grid=(SEQ // tm, N_FF_CHUNK),
        out_shape=(jax.ShapeDtypeStruct((SEQ, D_MODEL), F32), jax.ShapeDtypeStruct((SEQ, D_MODEL), F32),
                   chunk_shape, chunk_shape, jax.ShapeDtypeStruct((SEQ, D_MODEL), BF16)),
        in_specs=[tok, _row_spec(3), _row_spec(1),
                  pl.BlockSpec((1, FF_PIECE, D_MODEL), lambda i, j: (j, 0, 0)),
                  pl.BlockSpec((1, FF_PIECE, D_MODEL), lambda i, j: (j + N_FF_CHUNK, 0, 0)),
                  pl.BlockSpec((1, FF_PIECE, D_MODEL), lambda i, j: (j, 0, 0))],
        out_specs=(tok, tok, chunk, chunk, tok),
        scratch_shapes=[pltpu.VMEM((tm, D_MODEL), BF16), pltpu.VMEM((tm, D_MODEL), F32)],
        compiler_params=_params("arbitrary", "arbitrary"),
    )(x, cond3, g, w_in_g, w_in_g, w_out_g)


def _ffn_bwd(dy, x, f, a_sv, b_sv, cond3, g, w_in_g, w_out_g):
    tm = TOKEN_TILE
    last = N_FF_CHUNK - 1

    def body(dy_ref, x_ref, f_ref, a_ref, b_ref, cond_ref, g_ref, wa_ref, wb_ref, wo_ref,
             dx_ref, da_ref, db_ref, act_ref, do_ref, part_ref, do_scr, dh_scr):
        i, j = pl.program_id(0), pl.program_id(1)

        @pl.when(j == 0)
        def _():
            do = ((0.5 * cond_ref[2:3, :]) * dy_ref[...]).astype(BF16)
            do_scr[...] = do
            do_ref[...] = do
            dh_scr[...] = jnp.zeros_like(dh_scr)

        @pl.when((i == 0) & (j == 0))
        def _():
            part_ref[...] = jnp.zeros_like(part_ref)

        wa, wb, wo = wa_ref[0], wb_ref[0], wo_ref[0]
        for r in range(tm // ROW_SUBTILE):
            rows = slice(r * ROW_SUBTILE, (r + 1) * ROW_SUBTILE)
            do = do_scr[rows, :]
            a = a_ref[0, rows, :].astype(F32)
            b = b_ref[0, rows, :].astype(F32)
            dact = _dot_nt(do, wo)
            sig = _sigmoid(a)
            s = a * sig
            da = (dact * b * (sig * (1.0 + a * (1.0 - sig)))).astype(BF16)
            db = (dact * s).astype(BF16)
            da_ref[0, rows, :] = da
            db_ref[0, rows, :] = db
            act_ref[0, rows, :] = (s * b).astype(BF16)
            dh_scr[rows, :] += _dot(da, wa) + _dot(db, wb)

        @pl.when(j == last)
        def _():
            dyv = dy_ref[...]
            xhat, r = _rms(x_ref[...])
            dx, dshift, dscale, dg = _modnorm_bwd(xhat, r, g_ref[...], cond_ref[1:2, :], dh_scr[...])
            dx_ref[...] = dyv + dx
            part_ref[0:1, :] += dshift
            part_ref[1:2, :] += dscale
            part_ref[2:3, :] += _sum0(0.5 * dyv * f_ref[...])
            part_ref[3:4, :] += dg

    tok = pl.BlockSpec((tm, D_MODEL), lambda i, j: (i, 0))
    chunk = pl.BlockSpec((1, tm, FF_PIECE), lambda i, j: (j, i, 0))
    chunk_shape = jax.ShapeDtypeStruct((N_FF_CHUNK, SEQ, FF_PIECE), BF16)
    return pl.pallas_call(
        body, name="ffn_bwd", grid=(SEQ // tm, N_FF_CHUNK),
        out_shape=(jax.ShapeDtypeStruct((SEQ, D_MODEL), F32), chunk_shape, chunk_shape, chunk_shape,
                   jax.ShapeDtypeStruct((SEQ, D_MODEL), BF16), jax.ShapeDtypeStruct((SUBLANES, D_MODEL), F32)),
        in_specs=[tok, tok, tok, chunk, chunk, _row_spec(3), _row_spec(1),
                  pl.BlockSpec((1, FF_PIECE, D_MODEL), lambda i, j: (j, 0, 0)),
                  pl.BlockSpec((1, FF_PIECE, D_MODEL), lambda i, j: (j + N_FF_CHUNK, 0, 0)),
                  pl.BlockSpec((1, FF_PIECE, D_MODEL), lambda i, j: (j, 0, 0))],
        out_specs=(tok, chunk, chunk, chunk, tok, _row_spec(SUBLANES)),
        scratch_shapes=[pltpu.VMEM((tm, D_MODEL), BF16), pltpu.VMEM((tm, D_MODEL), F32)],
        compiler_params=_params("arbitrary", "arbitrary"),
    )(dy, x, f, a_sv, b_sv, cond3, g, w_in_g, w_in_g, w_out_g)


def _dw(lhs_a, rhs, lhs_b=None, name="dw"):
    pa, s, m = lhs_a.shape
    nn = rhs.shape[-1]
    pb = 0 if lhs_b is None else lhs_b.shape[0]
    two = lhs_b is not None

    def body(*refs):
        if two:
            a_ref, b_ref, r_ref, o_ref = refs
            p = pl.program_id(0)

            @pl.when(p < pa)
            def _():
                o_ref[0] = _dot_tn(a_ref[0], r_ref[...]).astype(BF16)

            @pl.when(p >= pa)
            def _():
                o_ref[0] = _dot_tn(b_ref[0], r_ref[...]).astype(BF16)
        else:
            a_ref, r_ref, o_ref = refs
            o_ref[0] = _dot_tn(a_ref[0], r_ref[...]).astype(BF16)

    if two:
        in_specs = [pl.BlockSpec((1, s, m), lambda p: (jnp.minimum(p, pa - 1), 0, 0)),
                    pl.BlockSpec((1, s, m), lambda p: (jnp.maximum(p - pa, 0), 0, 0))]
        args = (lhs_a, lhs_b, rhs)
    else:
        in_specs = [pl.BlockSpec((1, s, m), lambda p: (p, 0, 0))]
        args = (lhs_a, rhs)
    in_specs.append(pl.BlockSpec((s, nn), lambda p: (0, 0)))
    return pl.pallas_call(
        body, name=name, grid=(pa + pb,),
        out_shape=jax.ShapeDtypeStruct((pa + pb, m, nn), BF16),
        in_specs=in_specs, out_specs=pl.BlockSpec((1, m, nn), lambda p: (p, 0, 0)),
        compiler_params=_params("arbitrary"),
    )(*args)


def _mix_in_fwd(x, cond2, g, w):
    tm = TOKEN_TILE

    def body(x_ref, cond_ref, g_ref, w_ref, z_ref):
        h, _, _ = _modnorm(x_ref[...], g_ref[...], cond_ref[0:1, :], cond_ref[1:2, :])
        z_ref[...] = _dot_nt(h.astype(BF16), w_ref[...])

    return pl.pallas_call(
        body, name="mix_in_fwd", grid=(SEQ // tm,),
        out_shape=jax.ShapeDtypeStruct((SEQ, P_IN), F32),
        in_specs=[pl.BlockSpec((tm, D_MODEL), lambda i: (i, 0)), _row_spec(2), _row_spec(1),
                  pl.BlockSpec((P_IN, D_MODEL), lambda i: (0, 0))],
        out_specs=pl.BlockSpec((tm, P_IN), lambda i: (i, 0)),
        compiler_params=_params("arbitrary"),
    )(x, cond2, g, w)


MIX_SLABS = ((0, 2 * W_GRP), (2 * W_GRP, 3 * W_GRP), (3 * W_GRP, 6 * W_GRP), (6 * W_GRP, 7 * W_GRP))


def _mix_in_bwd(dzs, x, dy, cond2, g, w):
    tm = TOKEN_TILE

    def body(dza_ref, dzb_ref, dzc_ref, dzd_ref, x_ref, dy_ref, cond_ref, g_ref, w_ref, dx_ref, h_ref, dzo_ref, part_ref):
        i = pl.program_id(0)

        @pl.when(i == 0)
        def _():
            part_ref[...] = jnp.zeros_like(part_ref)

        h, xhat, r = _modnorm(x_ref[...], g_ref[...], cond_ref[0:1, :], cond_ref[1:2, :])
        h_ref[...] = h.astype(BF16)
        dh = None
        for (lo, hi), d_ref in zip(MIX_SLABS, (dza_ref, dzb_ref, dzc_ref, dzd_ref)):
            dzb = d_ref[...].astype(BF16)
            dzo_ref[:, lo:hi] = dzb
            t = _dot(dzb, w_ref[lo:hi, :])
            dh = t if dh is None else dh + t
        dx, dshift, dscale, dg = _modnorm_bwd(xhat, r, g_ref[...], cond_ref[1:2, :], dh)
        dx_ref[...] = dy_ref[...] + dx
        part_ref[0:1, :] += dshift
        part_ref[1:2, :] += dscale
        part_ref[2:3, :] += dg

    tok = pl.BlockSpec((tm, D_MODEL), lambda i: (i, 0))
    ztok = pl.BlockSpec((tm, P_IN), lambda i: (i, 0))
    slabs = [pl.BlockSpec((tm, hi - lo), lambda i: (i, 0)) for lo, hi in MIX_SLABS]
    return pl.pallas_call(
        body, name="mix_in_bwd", grid=(SEQ // tm,),
        out_shape=(jax.ShapeDtypeStruct((SEQ, D_MODEL), F32), jax.ShapeDtypeStruct((SEQ, D_MODEL), BF16),
                   jax.ShapeDtypeStruct((SEQ, P_IN), BF16), jax.ShapeDtypeStruct((SUBLANES, D_MODEL), F32)),
        in_specs=[*slabs, tok, tok, _row_spec(2), _row_spec(1), pl.BlockSpec((P_IN, D_MODEL), lambda i: (0, 0))],
        out_specs=(tok, tok, ztok, _row_spec(SUBLANES)),
        compiler_params=_params("arbitrary"),
    )(*dzs, x, dy, cond2, g, w)


def _group_norm(ys, g_ref):
    out = []
    for k, y in enumerate(ys):
        yhat, r = _rms(y)
        out.append((yhat, r, g_ref[:, k * W_GRP:(k + 1) * W_GRP]))
    return out


def _mix_out_fwd(x, ys, g, gate, w):
    tm = TOKEN_TILE

    def body(x_ref, ya_ref, yb_ref, yc_ref, yd_ref, g_ref, gate_ref, w_ref, xo_ref, mo_ref):
        acc = None
        for k, (yhat, _, gk) in enumerate(_group_norm([r[...] for r in (ya_ref, yb_ref, yc_ref, yd_ref)], g_ref)):
            t = _dot((yhat * gk).astype(BF16), w_ref[k * W_GRP:(k + 1) * W_GRP, :])
            acc = t if acc is None else acc + t
        mo_ref[...] = acc
        xo_ref[...] = x_ref[...] + gate_ref[...] * acc

    tok = pl.BlockSpec((tm, D_MODEL), lambda i: (i, 0))
    ytok = pl.BlockSpec((tm, W_GRP), lambda i: (i, 0))
    sh = jax.ShapeDtypeStruct((SEQ, D_MODEL), F32)
    return pl.pallas_call(
        body, name="mix_out_fwd", grid=(SEQ // tm,),
        out_shape=(sh, sh),
        in_specs=[tok, ytok, ytok, ytok, ytok, _row_spec(1), _row_spec(1),
                  pl.BlockSpec((D_MODEL, D_MODEL), lambda i: (0, 0))],
        out_specs=(tok, tok), compiler_params=_params("arbitrary"),
    )(x, *ys, g, gate, w)


def _mix_out_bwd(dy, mo, ys, g, gate, w):
    tm = TOKEN_TILE

    def body(dy_ref, mo_ref, ya_ref, yb_ref, yc_ref, yd_ref, g_ref, gate_ref, w_ref,
             da_ref, db_ref, dc_ref, dd_ref, yn_ref, dmo_ref, part_ref):
        i = pl.program_id(0)

        @pl.when(i == 0)
        def _():
            part_ref[...] = jnp.zeros_like(part_ref)

        dyv = dy_ref[...]
        dmo = (gate_ref[...] * dyv).astype(BF16)
        dmo_ref[...] = dmo
        dyn = _dot_nt(dmo, w_ref[...])
        norms = _group_norm([r[...] for r in (ya_ref, yb_ref, yc_ref, yd_ref)], g_ref)
        for k, ((yhat, r, gk), o_ref) in enumerate(zip(norms, (da_ref, db_ref, dc_ref, dd_ref))):
            sl = slice(k * W_GRP, (k + 1) * W_GRP)
            yn_ref[:, sl] = (yhat * gk).astype(BF16)
            dk = dyn[:, sl]
            o_ref[...] = _rms_bwd(yhat, r, dk * gk)
            part_ref[1:2, sl] += _sum0(dk * yhat)
        part_ref[0:1, :] += _sum0(dyv * mo_ref[...])

    tok = pl.BlockSpec((tm, D_MODEL), lambda i: (i, 0))
    ytok = pl.BlockSpec((tm, W_GRP), lambda i: (i, 0))
    ysh = jax.ShapeDtypeStruct((SEQ, W_GRP), F32)
    return pl.pallas_call(
        body, name="mix_out_bwd", grid=(SEQ // tm,),
        out_shape=(ysh, ysh, ysh, ysh, jax.ShapeDtypeStruct((SEQ, D_MODEL), BF16),
                   jax.ShapeDtypeStruct((SEQ, D_MODEL), BF16), jax.ShapeDtypeStruct((SUBLANES, D_MODEL), F32)),
        in_specs=[tok, tok, ytok, ytok, ytok, ytok, _row_spec(1), _row_spec(1),
                  pl.BlockSpec((D_MODEL, D_MODEL), lambda i: (0, 0))],
        out_specs=(ytok, ytok, ytok, ytok, tok, tok, _row_spec(SUBLANES)),
        compiler_params=_params("arbitrary"),
    )(dy, mo, *ys, g, gate, w)


def _shift_down(v, k, rows):
    return jnp.where(rows >= k, pltpu.roll(v, k, axis=0), 0.0)


def _shift_up(v, k, rows):
    n = v.shape[0]
    return jnp.where(rows < n - k, pltpu.roll(v, n - k, axis=0), 0.0)


def _zslab(width, index):
    return pl.BlockSpec((SEQ, width), lambda *_: (0, index))


def _full(shape):
    return pl.BlockSpec(shape, lambda *_: (0,) * len(shape))


def _head_avg():
    r = lax.broadcasted_iota(jnp.int32, (W_GRP, W_GRP), 0) // SGU_HEAD_DIM
    c = lax.broadcasted_iota(jnp.int32, (W_GRP, W_GRP), 1) // SGU_HEAD_DIM
    return jnp.where(r == c, 1.0 / SGU_HEAD_DIM, 0.0).astype(F32)


def _sgu_norm(za):
    z = _gelu(za)
    u, v = z[:, :W_GRP], z[:, W_GRP:]
    avg = _head_avg()
    vc = v - _dot_hi(v, avg)
    rstd = lax.rsqrt(_dot_hi(vc * vc, avg) + EPS)
    return u, vc * rstd, rstd


def _sgu_masked_w(w_ref):
    t = lax.broadcasted_iota(jnp.int32, (CHUNK, CHUNK), 0)
    s = lax.broadcasted_iota(jnp.int32, (CHUNK, CHUNK), 1)
    tril = t >= s
    return [jnp.where(tril, w_ref[:, h * CHUNK:(h + 1) * CHUNK], 0.0).astype(BF16) for h in range(SGU_HEADS)]


def _head_of_lane():
    return lax.broadcasted_iota(jnp.int32, (CHUNK, W_GRP), 1) // SGU_HEAD_DIM


def _sgu_fwd(z, w_cat, bias):
    def body(z_ref, w_ref, b_ref, y_ref, vn_scr, u_scr):
        u, vn, _ = _sgu_norm(z_ref[...])
        vn_scr[...] = vn.astype(BF16)
        u_scr[...] = u
        ws = _sgu_masked_w(w_ref)
        head = _head_of_lane()
        bias_v = b_ref[...]

        def chunk(n, carry):
            rows = pl.ds(pl.multiple_of(n * CHUNK, CHUNK), CHUNK)
            vb = vn_scr[rows, :]
            mixed = bias_v
            for h in range(SGU_HEADS):
                mixed = mixed + jnp.where(head == h, _dot(ws[h], vb), 0.0)
            y_ref[rows, :] = u_scr[rows, :] * mixed
            return carry

        lax.fori_loop(0, SEQ // CHUNK, chunk, 0)

    return pl.pallas_call(
        body, name="sgu_fwd", grid=(1,),
        out_shape=jax.ShapeDtypeStruct((SEQ, W_GRP), F32),
        in_specs=[_zslab(2 * W_GRP, 0), _full((CHUNK, SGU_HEADS * CHUNK)), _full((CHUNK, W_GRP))],
        out_specs=_full((SEQ, W_GRP)),
        scratch_shapes=[pltpu.VMEM((SEQ, W_GRP), BF16), pltpu.VMEM((SEQ, W_GRP), F32)],
        compiler_params=_params("arbitrary"),
    )(z, w_cat, bias)


def _sgu_bwd(z, dy, w_cat, bias):
    def body(z_ref, dy_ref, w_ref, b_ref, dz_ref, dw_ref, db_ref, vn_scr, u_scr, dvn_scr, du_scr):
        za = z_ref[...]
        u, vn, rstd = _sgu_norm(za)
        vn_scr[...] = vn.astype(BF16)
        u_scr[...] = u
        ws = _sgu_masked_w(w_ref)
        head = _head_of_lane()
        bias_v = b_ref[...]

        def chunk(n, carry):
            dws, dbias = carry
            rows = pl.ds(pl.multiple_of(n * CHUNK, CHUNK), CHUNK)
            vb = vn_scr[rows, :]
            mixed = bias_v
            for h in range(SGU_HEADS):
                mixed = mixed + jnp.where(head == h, _dot(ws[h], vb), 0.0)
            dyc = dy_ref[rows, :]
            du_scr[rows, :] = dyc * mixed
            dmixed = dyc * u_scr[rows, :]
            dvn = jnp.zeros((CHUNK, W_GRP), F32)
            new_dws = []
            for h in range(SGU_HEADS):
                dm = jnp.where(head == h, dmixed, 0.0).astype(BF16)
                new_dws.append(dws[h] + _dot_nt(dm, vb))
                dvn = dvn + _dot_tn(ws[h], dm)
            dvn_scr[rows, :] = dvn
            return tuple(new_dws), dbias + dmixed

        zero_w = tuple(jnp.zeros((CHUNK, CHUNK), F32) for _ in range(SGU_HEADS))
        dws, dbias = lax.fori_loop(0, SEQ // CHUNK, chunk, (zero_w, jnp.zeros((CHUNK, W_GRP), F32)))
        t = lax.broadcasted_iota(jnp.int32, (CHUNK, CHUNK), 0)
        s = lax.broadcasted_iota(jnp.int32, (CHUNK, CHUNK), 1)
        for h in range(SGU_HEADS):
            dw_ref[:, h * CHUNK:(h + 1) * CHUNK] = jnp.where(t >= s, dws[h], 0.0)
        avg = _head_avg()
        db_ref[...] = _dot_hi(dbias, avg) * float(SGU_HEAD_DIM)
        dvn = dvn_scr[...]
        dv = rstd * (dvn - _dot_hi(dvn, avg) - vn * _dot_hi(dvn * vn, avg))
        gg = _gelu_grad(za)
        dz_ref[:, :W_GRP] = gg[:, :W_GRP] * du_scr[...]
        dz_ref[:, W_GRP:] = gg[:, W_GRP:] * dv

    return pl.pallas_call(
        body, name="sgu_bwd", grid=(1,),
        out_shape=(jax.ShapeDtypeStruct((SEQ, 2 * W_GRP), F32), jax.ShapeDtypeStruct((CHUNK, SGU_HEADS * CHUNK), F32),
                   jax.ShapeDtypeStruct((CHUNK, W_GRP), F32)),
        in_specs=[_zslab(2 * W_GRP, 0), _full((SEQ, W_GRP)), _full((CHUNK, SGU_HEADS * CHUNK)), _full((CHUNK, W_GRP))],
        out_specs=(_full((SEQ, 2 * W_GRP)), _full((CHUNK, SGU_HEADS * CHUNK)), _full((CHUNK, W_GRP))),
        scratch_shapes=[pltpu.VMEM((SEQ, W_GRP), BF16), pltpu.VMEM((SEQ, W_GRP), F32),
                        pltpu.VMEM((SEQ, W_GRP), F32), pltpu.VMEM((SEQ, W_GRP), F32)],
        compiler_params=_params("arbitrary"),
    )(z, dy, w_cat, bias)


def _pool_window_of_lane(shape):
    grp = lax.broadcasted_iota(jnp.int32, shape, 1) // POOL_GROUP_DIM
    win = jnp.full(shape, POOL_WINDOWS[0], jnp.int32)
    for k in range(1, len(POOL_WINDOWS)):
        win = jnp.where(grp == k, POOL_WINDOWS[k], win)
    return grp, win


def _pool_select(levels, grp):
    out = levels[0]
    for k in range(1, len(levels)):
        out = jnp.where(grp == k, levels[k], out)
    return out


def _pool_p(z):
    shape = z.shape
    rows = lax.broadcasted_iota(jnp.int32, shape, 0)
    grp, win = _pool_window_of_lane(shape)
    levels, s, k = [], z, 1
    for _ in POOL_WINDOWS:
        s = s + _shift_down(s, k, rows)
        levels.append(s)
        k *= 2
    inv = 1.0 / jnp.minimum(rows + 1, win).astype(F32)
    return _pool_select(levels, grp) * inv - z, inv, rows, grp


def _pool_fwd(z, w_bd, scale):
    def body(z_ref, w_ref, s_ref, y_ref):
        p, _, _, _ = _pool_p(z_ref[...])
        y_ref[...] = _dot(p.astype(BF16), w_ref[...]) * s_ref[...]

    return pl.pallas_call(
        body, name="pool_fwd", grid=(1,),
        out_shape=jax.ShapeDtypeStruct((SEQ, W_GRP), F32),
        in_specs=[_zslab(W_GRP, 2), _full((W_GRP, W_GRP)), _full((1, W_GRP))],
        out_specs=_full((SEQ, W_GRP)), compiler_params=_params("arbitrary"),
    )(z, w_bd, scale)


def _pool_bwd(z, dy, w_bd, scale):
    def body(z_ref, dy_ref, w_ref, s_ref, dz_ref, dw_ref, ds_ref):
        p, inv, rows, grp = _pool_p(z_ref[...])
        pb = p.astype(BF16)
        dyv = dy_ref[...]
        ds_ref[...] = _sum0(dyv * _dot(pb, w_ref[...]))
        dpre = (dyv * s_ref[...]).astype(BF16)
        dw_ref[...] = _dot_tn(pb, dpre)
        dp = _dot_nt(dpre, w_ref[...])
        q = dp * inv
        levels, s, k = [], q, 1
        for _ in POOL_WINDOWS:
            s = s + _shift_up(s, k, rows)
            levels.append(s)
            k *= 2
        dz_ref[...] = _pool_select(levels, grp) - dp

    return pl.pallas_call(
        body, name="pool_bwd", grid=(1,),
        out_shape=(jax.ShapeDtypeStruct((SEQ, W_GRP), F32), jax.ShapeDtypeStruct((W_GRP, W_GRP), F32),
                   jax.ShapeDtypeStruct((1, W_GRP), F32)),
        in_specs=[_zslab(W_GRP, 2), _full((SEQ, W_GRP)), _full((W_GRP, W_GRP)), _full((1, W_GRP))],
        out_specs=(_full((SEQ, W_GRP)), _full((W_GRP, W_GRP)), _full((1, W_GRP))),
        compiler_params=_params("arbitrary"),
    )(z, dy, w_bd, scale)


def _conv_fwd(z, w):
    def body(z_ref, w_ref, y_ref):
        zc = z_ref[...]
        bg, cg, xh = zc[:, :W_GRP], zc[:, W_GRP:2 * W_GRP], zc[:, 2 * W_GRP:]
        rows = lax.broadcasted_iota(jnp.int32, (SEQ, W_GRP), 0)
        y = cg * xh
        conv = w_ref[0:1, :] * _shift_down(y, 2, rows) + w_ref[1:2, :] * _shift_down(y, 1, rows) + w_ref[2:3, :] * y
        y_ref[...] = bg * conv

    return pl.pallas_call(
        body, name="conv_fwd", grid=(1,),
        out_shape=jax.ShapeDtypeStruct((SEQ, W_GRP), F32),
        in_specs=[_zslab(3 * W_GRP, 1), _full((3, W_GRP))],
        out_specs=_full((SEQ, W_GRP)), compiler_params=_params("arbitrary"),
    )(z, w)


def _conv_bwd(z, dy, w):
    def body(z_ref, dy_ref, w_ref, dz_ref, dw_ref):
        zc = z_ref[...]
        bg, cg, xh = zc[:, :W_GRP], zc[:, W_GRP:2 * W_GRP], zc[:, 2 * W_GRP:]
        rows = lax.broadcasted_iota(jnp.int32, (SEQ, W_GRP), 0)
        y = cg * xh
        y2, y1 = _shift_down(y, 2, rows), _shift_down(y, 1, rows)
        conv = w_ref[0:1, :] * y2 + w_ref[1:2, :] * y1 + w_ref[2:3, :] * y
        dyv = dy_ref[...]
        dconv = dyv * bg
        dw_ref[...] = jnp.zeros_like(dw_ref)
        dw_ref[0:1, :] = _sum0(dconv * y2)
        dw_ref[1:2, :] = _sum0(dconv * y1)
        dw_ref[2:3, :] = _sum0(dconv * y)
        dyy = (w_ref[0:1, :] * _shift_up(dconv, 2, rows) + w_ref[1:2, :] * _shift_up(dconv, 1, rows)
               + w_ref[2:3, :] * dconv)
        dz_ref[:, :W_GRP] = dyv * conv
        dz_ref[:, W_GRP:2 * W_GRP] = dyy * xh
        dz_ref[:, 2 * W_GRP:] = dyy * cg

    return pl.pallas_call(
        body, name="conv_bwd", grid=(1,),
        out_shape=(jax.ShapeDtypeStruct((SEQ, 3 * W_GRP), F32), jax.ShapeDtypeStruct((SUBLANES, W_GRP), F32)),
        in_specs=[_zslab(3 * W_GRP, 1), _full((SEQ, W_GRP)), _full((3, W_GRP))],
        out_specs=(_full((SEQ, 3 * W_GRP)), _full((SUBLANES, W_GRP))),
        compiler_params=_params("arbitrary"),
    )(z, dy, w)


def _s5_disc(lre, lim, ldt, br, bi):
    dt = jnp.exp(ldt)
    mag = jnp.exp(lre * dt)
    ang = lim * dt
    a_re, a_im = mag * jnp.cos(ang), mag * jnp.sin(ang)
    nr, ni = a_re - 1.0, a_im
    den = lre * lre + lim * lim
    k_re = (nr * lre + ni * lim) / den
    k_im = (ni * lre - nr * lim) / den
    return a_re, a_im, k_re * br - k_im * bi, k_re * bi + k_im * br


def _s5_prep_fwd(lre, lim, ldt, br, bi):
    def body(lre_ref, lim_ref, ldt_ref, br_ref, bi_ref, ar_ref, ai_ref, bbr_ref, bbi_ref):
        ar, ai, bbr, bbi = _s5_disc(lre_ref[...], lim_ref[...], ldt_ref[...], br_ref[...], bi_ref[...])
        ar_ref[...] = ar
        ai_ref[...] = ai
        bbr_ref[...] = bbr
        bbi_ref[...] = bbi

    return pl.pallas_call(
        body, name="s5_prep_fwd",
        out_shape=(jax.ShapeDtypeStruct(lre.shape, F32), jax.ShapeDtypeStruct(lre.shape, F32),
                   jax.ShapeDtypeStruct(br.shape, F32), jax.ShapeDtypeStruct(br.shape, F32)),
        compiler_params=_params(),
    )(lre, lim, ldt, br, bi)


def _s5_prep_bwd(lre, lim, ldt, br, bi, dar, dai, dbbr, dbbi):
    def body(lre_ref, lim_ref, ldt_ref, br_ref, bi_ref, dar_ref, dai_ref, dbbr_ref, dbbi_ref,
             o_lre, o_lim, o_ldt, o_br, o_bi):
        _, pull = jax.vjp(_s5_disc, lre_ref[...], lim_ref[...], ldt_ref[...], br_ref[...], bi_ref[...])
        g = pull((dar_ref[...], dai_ref[...], dbbr_ref[...], dbbi_ref[...]))
        for o, v in zip((o_lre, o_lim, o_ldt, o_br, o_bi), g):
            o[...] = v

    return pl.pallas_call(
        body, name="s5_prep_bwd",
        out_shape=tuple(jax.ShapeDtypeStruct(a.shape, F32) for a in (lre, lim, ldt, br, bi)),
        compiler_params=_params(),
    )(lre, lim, ldt, br, bi, dar, dai, dbbr, dbbi)


def _cmul(ar, ai, br, bi):
    return ar * br - ai * bi, ar * bi + ai * br


def _s5_tile_consts(ar, ai, reverse):
    if reverse:
        ai = -ai
    shape = (SUBLANES, S5_BLOCK)
    row = lax.broadcasted_iota(jnp.int32, shape, 0)
    a1 = (jnp.broadcast_to(ar, shape), jnp.broadcast_to(ai, shape))
    a2 = _cmul(*a1, *a1)
    a4 = _cmul(*a2, *a2)
    a8 = _cmul(*a4, *a4)
    steps = []
    for s, (pr, pi) in ((1, a1), (2, a2), (4, a4)):
        keep = (row < SUBLANES - s) if reverse else (row >= s)
        steps.append((s, jnp.where(keep, pr, 0.0), jnp.where(keep, pi, 0.0)))
    e = (SUBLANES - row) if reverse else (row + 1)
    pr, pi = jnp.ones(shape, F32), jnp.zeros(shape, F32)
    for bit, (qr, qi) in ((1, a1), (2, a2), (4, a4), (8, a8)):
        nr, ni = _cmul(pr, pi, qr, qi)
        hit = (e & bit) != 0
        pr, pi = jnp.where(hit, nr, pr), jnp.where(hit, ni, pi)
    return steps, pr, pi


def _s5_tile(xr, xi, steps, reverse):
    for s, pr, pi in steps:
        sh = SUBLANES - s if reverse else s
        sr, si = pltpu.roll(xr, sh, axis=0), pltpu.roll(xi, sh, axis=0)
        xr, xi = xr + pr * sr - pi * si, xi + pr * si + pi * sr
    return xr, xi


N_TILES = SEQ // SUBLANES


def _s5_fwd(z, b_re, b_im, c_re, c_im, a_re, a_im, d, glu_w, glu_b):
    nblk = S5_LANES // S5_BLOCK

    def body(u_ref, br_ref, bi_ref, cr_ref, ci_ref, ar_ref, ai_ref, d_ref, gw_ref, gb_ref,
             y_ref, y0_ref, xr_ref, xi_ref, ub_scr, acc_scr):
        jb = pl.program_id(0)

        @pl.when(jb == 0)
        def _():
            ub_scr[...] = u_ref[...].astype(BF16)
            acc_scr[...] = jnp.zeros_like(acc_scr)

        ub = ub_scr[...]
        xr_ref[...] = _dot(ub, br_ref[...])
        xi_ref[...] = _dot(ub, bi_ref[...])
        steps, pr, pi = _s5_tile_consts(ar_ref[...], ai_ref[...], False)

        def tile(t, carry):
            cr, ci = carry
            rows = pl.ds(pl.multiple_of(t * SUBLANES, SUBLANES), SUBLANES)
            xr, xi = _s5_tile(xr_ref[rows, :], xi_ref[rows, :], steps, False)
            xr, xi = xr + pr * cr - pi * ci, xi + pr * ci + pi * cr
            xr_ref[rows, :] = xr
            xi_ref[rows, :] = xi
            return xr[SUBLANES - 1:, :], xi[SUBLANES - 1:, :]

        zero = jnp.zeros((1, S5_BLOCK), F32)
        lax.fori_loop(0, N_TILES, tile, (zero, zero), unroll=2)
        acc_scr[...] += (_dot(xr_ref[...].astype(BF16), cr_ref[...]) - _dot(xi_ref[...].astype(BF16), ci_ref[...]))

        @pl.when(jb == nblk - 1)
        def _():
            y0 = acc_scr[...] + d_ref[...] * u_ref[...]
            y0_ref[...] = y0
            y1 = _gelu(y0)
            y_ref[...] = y1 * _sigmoid(_dot(y1.astype(BF16), gw_ref[...]) + gb_ref[...])

    lane_blk = pl.BlockSpec((SEQ, S5_BLOCK), lambda j: (0, j))
    return pl.pallas_call(
        body, name="s5_fwd", grid=(nblk,),
        out_shape=(jax.ShapeDtypeStruct((SEQ, W_GRP), F32), jax.ShapeDtypeStruct((SEQ, W_GRP), F32),
                   jax.ShapeDtypeStruct((SEQ, S5_LANES), F32), jax.ShapeDtypeStruct((SEQ, S5_LANES), F32)),
        in_specs=[_zslab(W_GRP, 6),
                  pl.BlockSpec((W_GRP, S5_BLOCK), lambda j: (0, j)), pl.BlockSpec((W_GRP, S5_BLOCK), lambda j: (0, j)),
                  pl.BlockSpec((S5_BLOCK, W_GRP), lambda j: (j, 0)), pl.BlockSpec((S5_BLOCK, W_GRP), lambda j: (j, 0)),
                  pl.BlockSpec((1, S5_BLOCK), lambda j: (0, j)), pl.BlockSpec((1, S5_BLOCK), lambda j: (0, j)),
                  _full((1, W_GRP)), _full((W_GRP, W_GRP)), _full((1, W_GRP))],
        out_specs=(_full((SEQ, W_GRP)), _full((SEQ, W_GRP)), lane_blk, lane_blk),
        scratch_shapes=[pltpu.VMEM((SEQ, W_GRP), BF16), pltpu.VMEM((SEQ, W_GRP), F32)],
        compiler_params=_params("arbitrary"),
    )(z, b_re, b_im, c_re, c_im, a_re, a_im, d, glu_w, glu_b)


def _s5_bwd(z, y0, dy, xr, xi, b_re, b_im, c_re, c_im, a_re, a_im, d, glu_w, glu_b):
    nblk = S5_LANES // S5_BLOCK

    def body(u_ref, y0_ref, dy_ref, xr_ref, xi_ref, br_ref, bi_ref, cr_ref, ci_ref, ar_ref, ai_ref,
             d_ref, gw_ref, gb_ref,
             du_ref, dbr_ref, dbi_ref, dcr_ref, dci_ref, dar_ref, dai_ref, dd_ref, dgw_ref, dgb_ref,
             ub_scr, dy0_scr, du_scr, lr_scr, li_scr):
        jb = pl.program_id(0)

        @pl.when(jb == 0)
        def _():
            u = u_ref[...]
            y0v = y0_ref[...]
            y1 = _gelu(y0v)
            y1b = y1.astype(BF16)
            sg = _sigmoid(_dot(y1b, gw_ref[...]) + gb_ref[...])
            dyv = dy_ref[...]
            dpre = dyv * y1 * sg * (1.0 - sg)
            dpb = dpre.astype(BF16)
            dgw_ref[...] = _dot_tn(y1b, dpb)
            dgb_ref[...] = _sum0(dpre)
            dy1 = dyv * sg + _dot_nt(dpb, gw_ref[...])
            dy0 = dy1 * _gelu_grad(y0v)
            dd_ref[...] = _sum0(dy0 * u)
            du_scr[...] = dy0 * d_ref[...]
            dy0_scr[...] = dy0.astype(BF16)
            ub_scr[...] = u.astype(BF16)

        dy0b = dy0_scr[...]
        lr_scr[...] = _dot_nt(dy0b, cr_ref[...])
        li_scr[...] = -_dot_nt(dy0b, ci_ref[...])
        dcr_ref[...] = _dot_tn(xr_ref[...].astype(BF16), dy0b)
        dci_ref[...] = -_dot_tn(xi_ref[...].astype(BF16), dy0b)
        steps, pr, pi = _s5_tile_consts(ar_ref[...], ai_ref[...], True)
        row = lax.broadcasted_iota(jnp.int32, (SUBLANES, S5_BLOCK), 0)

        def tile(k, carry):
            cr, ci, accr, acci = carry
            t = N_TILES - 1 - k
            rows = pl.ds(pl.multiple_of(t * SUBLANES, SUBLANES), SUBLANES)
            lr, li = _s5_tile(lr_scr[rows, :], li_scr[rows, :], steps, True)
            lr, li = lr + pr * cr - pi * ci, li + pr * ci + pi * cr
            lr_scr[rows, :] = lr
            li_scr[rows, :] = li
            prev = pl.ds(pl.multiple_of(jnp.maximum(t - 1, 0) * SUBLANES, SUBLANES), SUBLANES)
            live = jnp.where(t > 0, 1.0, 0.0)
            xpr = jnp.where(row == 0, pltpu.roll(xr_ref[prev, :], 1, axis=0) * live, pltpu.roll(xr_ref[rows, :], 1, axis=0))
            xpi = jnp.where(row == 0, pltpu.roll(xi_ref[prev, :], 1, axis=0) * live, pltpu.roll(xi_ref[rows, :], 1, axis=0))
            accr = accr + lr * xpr + li * xpi
            acci = acci + li * xpr - lr * xpi
            return lr[0:1, :], li[0:1, :], accr, acci

        zero = jnp.zeros((1, S5_BLOCK), F32)
        zt = jnp.zeros((SUBLANES, S5_BLOCK), F32)
        _, _, accr, acci = lax.fori_loop(0, N_TILES, tile, (zero, zero, zt, zt), unroll=2)
        dar_ref[...] = jnp.zeros_like(dar_ref)
        dai_ref[...] = jnp.zeros_like(dai_ref)
        dar_ref[0:1, :] = _sum0(accr)
        dai_ref[0:1, :] = _sum0(acci)
        lrb, lib = lr_scr[...].astype(BF16), li_scr[...].astype(BF16)
        ub = ub_scr[...]
        dbr_ref[...] = _dot_tn(ub, lrb)
        dbi_ref[...] = _dot_tn(ub, lib)
        du_scr[...] += _dot_nt(lrb, br_ref[...]) + _dot_nt(lib, bi_ref[...])

        @pl.when(jb == nblk - 1)
        def _():
            du_ref[...] = du_scr[...]

    lane_blk = pl.BlockSpec((SEQ, S5_BLOCK), lambda j: (0, j))
    bspec = pl.BlockSpec((W_GRP, S5_BLOCK), lambda j: (0, j))
    cspec = pl.BlockSpec((S5_BLOCK, W_GRP), lambda j: (j, 0))
    aspec = pl.BlockSpec((1, S5_BLOCK), lambda j: (0, j))
    a8spec = pl.BlockSpec((SUBLANES, S5_BLOCK), lambda j: (0, j))
    sd = jax.ShapeDtypeStruct
    return pl.pallas_call(
        body, name="s5_bwd", grid=(nblk,),
        out_shape=(sd((SEQ, W_GRP), F32), sd((W_GRP, S5_LANES), F32), sd((W_GRP, S5_LANES), F32),
                   sd((S5_LANES, W_GRP), F32), sd((S5_LANES, W_GRP), F32),
                   sd((SUBLANES, S5_LANES), F32), sd((SUBLANES, S5_LANES), F32),
                   sd((1, W_GRP), F32), sd((W_GRP, W_GRP), F32), sd((1, W_GRP), F32)),
        in_specs=[_zslab(W_GRP, 6), _full((SEQ, W_GRP)), _full((SEQ, W_GRP)), lane_blk, lane_blk,
                  bspec, bspec, cspec, cspec, aspec, aspec,
                  _full((1, W_GRP)), _full((W_GRP, W_GRP)), _full((1, W_GRP))],
        out_specs=(_full((SEQ, W_GRP)), bspec, bspec, cspec, cspec, a8spec, a8spec,
                   _full((1, W_GRP)), _full((W_GRP, W_GRP)), _full((1, W_GRP))),
        scratch_shapes=[pltpu.VMEM((SEQ, W_GRP), BF16), pltpu.VMEM((SEQ, W_GRP), BF16), pltpu.VMEM((SEQ, W_GRP), F32),
                        pltpu.VMEM((SEQ, S5_BLOCK), F32), pltpu.VMEM((SEQ, S5_BLOCK), F32)],
        compiler_params=_params("arbitrary"),
    )(z, y0, dy, xr, xi, b_re, b_im, c_re, c_im, a_re, a_im, d, glu_w, glu_b)


def _head(x, g, target):
    tm = TOKEN_TILE
    n = SEQ // tm

    def body(x_ref, g_ref, t_ref, dx_ref, st_ref, acc_scr):
        i = pl.program_id(0)

        @pl.when(i == 0)
        def _():
            acc_scr[...] = jnp.zeros_like(acc_scr)

        xhat, r = _rms(x_ref[...])
        gv = g_ref[...]
        err = xhat * gv - t_ref[...]
        dyv = err * (1.0 / D_MODEL)
        dx_ref[...] = _rms_bwd(xhat, r, dyv * gv)
        acc_scr[0:1, :] += _sum0(err * err)
        acc_scr[1:2, :] += _sum0(dyv * xhat)

        @pl.when(i == n - 1)
        def _():
            st_ref[...] = acc_scr[...]
            tot = jnp.sum(acc_scr[0:1, :], axis=-1, keepdims=True) * (0.5 / D_MODEL)
            st_ref[0:1, :] = jnp.broadcast_to(tot, (1, D_MODEL))

    tok = pl.BlockSpec((tm, D_MODEL), lambda i: (i, 0))
    return pl.pallas_call(
        body, name="head", grid=(n,),
        out_shape=(jax.ShapeDtypeStruct((SEQ, D_MODEL), F32), jax.ShapeDtypeStruct((SUBLANES, D_MODEL), F32)),
        in_specs=[tok, _row_spec(1), tok], out_specs=(tok, _row_spec(SUBLANES)),
        scratch_shapes=[pltpu.VMEM((SUBLANES, D_MODEL), F32)],
        compiler_params=_params("arbitrary"),
    )(x, g, target)


def _adamw(w, gparts, m, v, name):
    r, c = w.shape
    npart = gparts.shape[0]
    tr = r
    for cand in (512, 256, 128, 64, 32, 16):
        if r % cand == 0 and r > cand:
            tr = cand
            break
    b1c = 1.0 - ADAM_B1 ** ADAM_STEP
    b2c = 1.0 - ADAM_B2 ** ADAM_STEP

    def body(w_ref, g_ref, m_ref, v_ref, go_ref, d_ref, mo_ref, vo_ref):
        g = g_ref[0].astype(F32)
        for k in range(1, npart):
            g = g + g_ref[k].astype(F32)
        mn = ADAM_B1 * m_ref[...] + (1.0 - ADAM_B1) * g
        vn = ADAM_B2 * v_ref[...] + (1.0 - ADAM_B2) * (g * g)
        m_hat = mn / b1c
        v_hat = vn / b2c
        go_ref[...] = g
        d_ref[...] = -ADAM_LR * (m_hat / (jnp.sqrt(v_hat) + ADAM_EPS) + ADAM_WD * w_ref[...])
        mo_ref[...] = mn
        vo_ref[...] = vn

    blk = pl.BlockSpec((tr, c), lambda i: (i, 0))
    sh = jax.ShapeDtypeStruct((r, c), F32)
    return pl.pallas_call(
        body, name=name, grid=(r // tr,),
        out_shape=(sh, sh, sh, sh),
        in_specs=[blk, pl.BlockSpec((npart, tr, c), lambda i: (0, i, 0)), blk, blk],
        out_specs=(blk, blk, blk, blk), compiler_params=_params("arbitrary"),
    )(w, gparts, m, v)


def _adamw_layer(l, w, gparts, m, v, prev, name):
    _, r, c = w.shape
    tr = max(t for t in range(16, 513, 16) if r % t == 0)
    b1c = 1.0 - ADAM_B1 ** ADAM_STEP
    b2c = 1.0 - ADAM_B2 ** ADAM_STEP
    nprev = 0 if prev is None else 4

    def body(core_ref, *refs):
        w_ref, m_ref, v_ref = refs[:3]
        g_refs = refs[3:3 + N_CHIPS]
        go_ref, d_ref, mo_ref, vo_ref = refs[3 + N_CHIPS + nprev:]
        g = g_refs[0][0].astype(F32)
        for g_ref in g_refs[1:]:
            g = g + g_ref[0].astype(F32)
        mn = ADAM_B1 * m_ref[0] + (1.0 - ADAM_B1) * g
        vn = ADAM_B2 * v_ref[0] + (1.0 - ADAM_B2) * (g * g)
        go_ref[0] = g
        d_ref[0] = -ADAM_LR * ((mn / b1c) / (jnp.sqrt(vn / b2c) + ADAM_EPS) + ADAM_WD * w_ref[0])
        mo_ref[0] = mn
        vo_ref[0] = vn

    blk = pl.BlockSpec((1, tr, c), lambda i, core: (l, i, 0))
    slots = [pl.BlockSpec((1, tr, c), functools.partial(lambda i, core, q: (2 * q + core[0], i, 0), q=q))
             for q in range(N_CHIPS)]
    sh = jax.ShapeDtypeStruct(w.shape, F32)
    keep = [pl.BlockSpec(memory_space=pl.ANY)] * nprev
    return pl.pallas_call(
        body, name=name, out_shape=(sh, sh, sh, sh),
        grid_spec=pltpu.PrefetchScalarGridSpec(num_scalar_prefetch=1, grid=(r // tr,),
                                               in_specs=[blk, blk, blk, *slots, *keep], out_specs=(blk, blk, blk, blk)),
        input_output_aliases={4 + N_CHIPS + k: k for k in range(nprev)},
        compiler_params=_params("arbitrary"),
    )(_core_index(), w, m, v, *([gparts] * N_CHIPS), *(prev or ()))


def _sum_parts(parts, name):
    n, r, c = parts.shape

    def body(p_ref, o_ref):
        acc = p_ref[0]
        for k in range(1, n):
            acc = acc + p_ref[k]
        o_ref[...] = acc

    return pl.pallas_call(
        body, name=name, out_shape=jax.ShapeDtypeStruct((r, c), F32), compiler_params=_params(),
    )(parts)


def _block_diag(blocks):
    g, a, b = blocks.shape
    eye = jnp.eye(g, dtype=blocks.dtype)
    return (blocks[:, :, None, :] * eye[:, None, :, None]).reshape(g * a, g * b)


def _diag_blocks(dense, g):
    a, b = dense.shape[0] // g, dense.shape[1] // g
    d4 = dense.reshape(g, a, g, b)
    eye = jnp.eye(g, dtype=dense.dtype)
    return jnp.sum(d4 * eye[:, None, :, None], axis=2)


def _pack(parts, cols):
    flat = jnp.concatenate([p.reshape(-1) for p in parts])
    unit = N_DEV * SUBLANES * cols
    total = -(-flat.shape[0] // unit) * unit
    flat = jnp.pad(flat, (0, total - flat.shape[0]))
    return flat.reshape(N_DEV, total // (N_DEV * cols), cols)


def _unpack(flat, shapes):
    out, pos = [], 0
    for s in shapes:
        n = math.prod(s)
        out.append(flat[pos:pos + n].reshape(s))
        pos += n
    return out


SMALL = ("norm1_g", "norm2_g", "sgu_w", "sgu_b", "pool_w", "pool_scale", "s5_lambda_re", "s5_lambda_im",
         "s5_b_re", "s5_b_im", "s5_c_re", "s5_c_im", "s5_d", "s5_log_dt", "s5_glu_b", "mix_norm_g",
         "norm3_g", "final_norm_g", "conv_w", "s5_glu_w")
N_SMALL_WHOLE = len(SMALL) - 2
BIG = ("ffn1_w_in", "ffn1_w_out", "w_mix_in", "w_mix_out", "ffn2_w_in", "ffn2_w_out")
TRANSPOSED = ("ffn1_w_in", "w_mix_in", "ffn2_w_in")
WEIGHTS = ("ada_w", "ada_b", "norm1_g", "ffn1_w_in", "ffn1_w_out", "norm2_g", "w_mix_in", "sgu_w", "sgu_b", "pool_w",
           "pool_scale", "conv_w", "s5_lambda_re", "s5_lambda_im", "s5_b_re", "s5_b_im", "s5_c_re", "s5_c_im", "s5_d",
           "s5_log_dt", "s5_glu_w", "s5_glu_b", "mix_norm_g", "w_mix_out", "norm3_g", "ffn2_w_in", "ffn2_w_out",
           "final_norm_g")
PACK_COLS = 1024


def kernel(x, c, ada_w, ada_b, norm1_g, ffn1_w_in, ffn1_w_out, norm2_g, w_mix_in, sgu_w, sgu_b, pool_w, pool_scale, conv_w, s5_lambda_re, s5_lambda_im, s5_b_re, s5_b_im, s5_c_re, s5_c_im, s5_d, s5_log_dt, s5_glu_w, s5_glu_b, mix_norm_g, w_mix_out, norm3_g, ffn2_w_in, ffn2_w_out, final_norm_g, loss_target, m_ada_w, m_ada_b, m_norm1_g, m_ffn1_w_in, m_ffn1_w_out, m_norm2_g, m_w_mix_in, m_sgu_w, m_sgu_b, m_pool_w, m_pool_scale, m_conv_w, m_s5_lambda_re, m_s5_lambda_im, m_s5_b_re, m_s5_b_im, m_s5_c_re, m_s5_c_im, m_s5_d, m_s5_log_dt, m_s5_glu_w, m_s5_glu_b, m_mix_norm_g, m_w_mix_out, m_norm3_g, m_ffn2_w_in, m_ffn2_w_out, m_final_norm_g, v_ada_w, v_ada_b, v_norm1_g, v_ffn1_w_in, v_ffn1_w_out, v_norm2_g, v_w_mix_in, v_sgu_w, v_sgu_b, v_pool_w, v_pool_scale, v_conv_w, v_s5_lambda_re, v_s5_lambda_im, v_s5_b_re, v_s5_b_im, v_s5_c_re, v_s5_c_im, v_s5_d, v_s5_log_dt, v_s5_glu_w, v_s5_glu_b, v_mix_norm_g, v_w_mix_out, v_norm3_g, v_ffn2_w_in, v_ffn2_w_out, v_final_norm_g):
    args = dict(locals())
    W = {n: args[n] for n in WEIGHTS}
    M = {n: args["m_" + n] for n in WEIGHTS}
    V = {n: args["v_" + n] for n in WEIGHTS}
    me = _me()
    L = DEPTH
    x0 = x[0]
    target = loss_target[0]

    conv_cols = conv_w.shape[-1]
    glu_rows = s5_glu_w.shape[1]
    c_g, conv_g, glu_g = _exchange(
        [c.reshape(SUBLANES, LANES), conv_w.reshape(L * 3, conv_cols), s5_glu_w.reshape(L * glu_rows, W_GRP)],
        False, "gather_small")
    c_all = c_g.reshape(N_DEV, D_MODEL)
    conv_full = conv_g.reshape(N_DEV, L, 3, conv_cols).transpose(1, 2, 0, 3).reshape(L, 3, W_GRP)
    glu_full = glu_g.reshape(N_DEV, L, glu_rows, W_GRP).transpose(1, 0, 2, 3).reshape(L, W_GRP, W_GRP)

    ncol = ada_w.shape[-1]
    ada_b_mine = lax.dynamic_slice_in_dim(ada_b, me * ncol, ncol, axis=1).reshape(L, 1, ncol)
    cond_part = _cond_fwd(c_all, ada_w, ada_b_mine)
    (cond_g,) = _exchange([cond_part.reshape(L * N_DEV, ncol)], False, "gather_cond")
    cond_g = cond_g.reshape(N_DEV, L, N_DEV, ncol)
    cond_mine = lax.dynamic_index_in_dim(cond_g, me, axis=2, keepdims=False)
    cond = cond_mine.transpose(1, 0, 2).reshape(L, N_ADA, D_MODEL)

    lg = L * S5_GROUPS
    lre3 = s5_lambda_re.reshape(lg, S5_STATE, 1)
    lim3 = s5_lambda_im.reshape(lg, S5_STATE, 1)
    ldt3 = s5_log_dt.reshape(lg, 1, 1)
    br3 = s5_b_re.reshape(lg, S5_STATE, S5_GROUP_CH)
    bi3 = s5_b_im.reshape(lg, S5_STATE, S5_GROUP_CH)

    def b_mat(bb3, l):
        return _block_diag(bb3.reshape(L, S5_GROUPS, S5_STATE, S5_GROUP_CH)[l].transpose(0, 2, 1)).astype(BF16)

    def c_mat(cw, l):
        return _block_diag(cw[l].transpose(0, 2, 1)).astype(BF16)

    for n in TRANSPOSED:
        W[n], M[n], V[n] = (a.transpose(0, 2, 1) for a in (W[n], M[n], V[n]))

    gathered_shape = {"ffn1_w_out": (N_FF_CHUNK, FF_PIECE, D_MODEL), "ffn2_w_out": (N_FF_CHUNK, FF_PIECE, D_MODEL),
                      "w_mix_in": (P_IN, D_MODEL), "w_mix_out": (D_MODEL, D_MODEL)}

    def gather_start(l, after, names=BIG):
        srcs = [W[n][l].astype(BF16) for n in names]
        lands = _place_own(srcs, False, "gather_weights_own")
        return _exchange_start(srcs, lands, after, False, "gather_weights_start", GATHER_PEERS)

    def gather_finish(handle, after, names=BIG):
        g = _exchange_wait(handle, after, False, "gather_weights_wait", GATHER_PEERS)
        out = dict(zip(names, _gather_sibling(g, "gather_weights_sibling")))
        return {n: (a.reshape(gathered_shape[n]) if n in gathered_shape else a) for n, a in out.items()}

    saved = []
    xc = x0
    first_ffn = BIG[:2]
    handle, token = gather_start(0, cond, first_ffn)
    handle_rest, token = gather_start(0, token, BIG[2:])

    zero = token[0, 0]
    a_re3, a_im3, bb_re3, bb_im3 = _s5_prep_fwd(lre3 + zero, lim3, ldt3, br3, bi3)
    a_re = a_re3.reshape(L, 1, S5_LANES)
    a_im = a_im3.reshape(L, 1, S5_LANES)
    sgu_w_t, pool_w_t, c_re_t, c_im_t = (a + zero for a in (sgu_w, pool_w, s5_c_re, s5_c_im))

    def mixer_consts(l):
        w_cat = sgu_w_t[l].transpose(1, 0, 2).reshape(CHUNK, SGU_HEADS * CHUNK)
        bias = jnp.repeat(sgu_b[l].T, SGU_HEAD_DIM, axis=1)
        return dict(
            w_cat=w_cat, bias=bias, pool_bd=_block_diag(pool_w_t[l]).astype(BF16), pool_scale=pool_scale[l][None],
            conv=conv_full[l], b_re=b_mat(bb_re3, l), b_im=b_mat(bb_im3, l), c_re=c_mat(c_re_t, l),
            c_im=c_mat(c_im_t, l), a_re=a_re[l], a_im=a_im[l], d=s5_d[l][None], glu_w=glu_full[l].astype(BF16),
            glu_b=s5_glu_b[l][None])

    mcs = [mixer_consts(l) for l in range(L)]
    small_names = SMALL + ("ada_b",)
    first_small = W[small_names[0]] + zero
    packed_w = _pack([first_small] + [W[n] for n in small_names[1:]], PACK_COLS)
    packed_m = _pack([M[n] + zero if n == small_names[0] else M[n] for n in small_names], PACK_COLS)
    packed_v = _pack([V[n] + zero if n == small_names[0] else V[n] for n in small_names], PACK_COLS)
    wl = gather_finish(handle, [x0] + [a for mc in mcs for a in mc.values()], first_ffn)
    for l in range(L):
        cl = cond[l]
        if 0 < l < L - 1:
            handle, token = gather_start(l + 1, wl["ffn1_w_in"])
            cl = cl + token[0, 0]
        mc = mcs[l]
        x_a = xc
        x_b, *ffn1_kept = _ffn_fwd(x_a, cl[0:3], norm1_g[l][None], wl["ffn1_w_in"], wl["ffn1_w_out"])
        if l == 0:
            wl = {**wl, **gather_finish(handle_rest, [x_b, packed_w, packed_m, packed_v], BIG[2:])}
            handle, token = gather_start(1, wl["w_mix_in"])
            cl = cl + token[0, 0]
        z = _mix_in_fwd(x_b, cl[3:5], norm2_g[l][None], wl["w_mix_in"])
        ya = _sgu_fwd(z, mc["w_cat"], mc["bias"])
        yb = _pool_fwd(z, mc["pool_bd"], mc["pool_scale"])
        yc = _conv_fwd(z, mc["conv"])
        yd, y0, sxr, sxi = _s5_fwd(z, mc["b_re"], mc["b_im"], mc["c_re"], mc["c_im"], mc["a_re"], mc["a_im"],
                                   mc["d"], mc["glu_w"], mc["glu_b"])
        ys = (ya, yb, yc, yd)
        x_c, mo = _mix_out_fwd(x_b, ys, mix_norm_g[l][None], cl[5:6], wl["w_mix_out"])
        x_d, *ffn2_kept = _ffn_fwd(x_c, cl[6:9], norm3_g[l][None], wl["ffn2_w_in"], wl["ffn2_w_out"])
        saved.append(dict(wl=wl, mc=mc, x_a=x_a, x_b=x_b, x_c=x_c, ffn1=ffn1_kept, ffn2=ffn2_kept, z=z, ys=ys, y0=y0,
                          xr=sxr, xi=sxi, mo=mo))
        xc = x_d
        if l + 1 < L:
            wl = gather_finish(handle, x_d)

    dx, stats = _head(xc, final_norm_g[None], target)
    loss = lax.psum(stats[0, 0], MESH_AXES)

    small_grads = {n: [None] * L for n in SMALL if n != "final_norm_g"}
    small_grads["final_norm_g"] = stats[1]
    big_out = {n: None for n in BIG}
    pending = None

    def swap_begin(pieces, after):
        bufs = _place_own(pieces, True, "scatter_grads_own")
        return _exchange_start(pieces, bufs, after, True, "swap_grads_start", SIBLING_SWAP)

    def scatter_begin(swap_handle, after):
        mine, theirs = _exchange_wait(swap_handle, after, True, "swap_grads_wait", SIBLING_SWAP, with_srcs=True)
        sums = _presum(mine, theirs, "presum_grads")
        lands = _place_own(sums, True, "scatter_grads_own")
        return _exchange_start(sums, lands, theirs[0], True, "scatter_grads_start", OTHER_CHIPS)

    def finish_scatter(pend, after):
        layer, hnd = pend
        recv = _exchange_wait(hnd, after, True, "scatter_grads_wait", OTHER_CHIPS)
        for n, r in zip(BIG, recv):
            big_out[n] = _adamw_layer(layer, W[n], r, M[n], V[n], big_out[n], "adamw_" + n)
        return recv[0]

    dcond_rows = [None] * L
    d_are, d_aim, d_bbre, d_bbim = [None] * L, [None] * L, [None] * L, [None] * L
    swap = None
    for l in reversed(range(L)):
        sv = saved[l]
        wl, mc, cl = sv["wl"], sv["mc"], cond[l]
        if swap is not None:
            cl = cl + swap[1][0, 0]
        f2, a2, b2, hb = sv["ffn2"]
        dx, da, db, act, dob, part3 = _ffn_bwd(dx, sv["x_c"], f2, a2, b2, cl[6:9], norm3_g[l][None],
                                               wl["ffn2_w_in"], wl["ffn2_w_out"])
        if swap is not None:
            handle, token = scatter_begin(swap[0], dx)
            pending = (l + 1, handle)
            cl = cl + token[0, 0]
        g_ffn2_in = _dw(da, hb, db, name="dw_ffn_in")
        g_ffn2_out = _dw(act, dob, name="dw_ffn_out").reshape(N_DEV, D_FF // N_DEV, D_MODEL)
        dya, dyb, dyc, dyd, ynb, dmob, part_mo = _mix_out_bwd(dx, sv["mo"], sv["ys"], mix_norm_g[l][None], cl[5:6],
                                                              wl["w_mix_out"])
        g_mix_out = _dw(ynb[None], dmob, name="dw_mix_out").reshape(N_DEV, D_MODEL // N_DEV, D_MODEL)
        z = sv["z"]
        dza, dw_cat, dbias = _sgu_bwd(z, dya, mc["w_cat"], mc["bias"])
        dzb, dpool_dense, dpool_scale = _pool_bwd(z, dyb, mc["pool_bd"], mc["pool_scale"])
        dzc, dconv8 = _conv_bwd(z, dyc, mc["conv"])
        (dzd, dbre_d, dbim_d, dcre_d, dcim_d, dar8, dai8, dd, dglu_w, dglu_b) = _s5_bwd(
            z, sv["y0"], dyd, sv["xr"], sv["xi"], mc["b_re"], mc["b_im"], mc["c_re"], mc["c_im"],
            mc["a_re"], mc["a_im"], mc["d"], mc["glu_w"], mc["glu_b"])
        dx, h2b, dzbf, part2 = _mix_in_bwd((dza, dzb, dzc, dzd), sv["x_b"], dx, cl[3:5], norm2_g[l][None],
                                           wl["w_mix_in"])
        g_mix_in = _dw(dzbf[None], h2b, name="dw_mix_in").reshape(N_DEV, P_IN // N_DEV, D_MODEL)
        f1, a1, b1, hb = sv["ffn1"]
        dx, da, db, act, dob, part1 = _ffn_bwd(dx, sv["x_a"], f1, a1, b1, cl[0:3], norm1_g[l][None],
                                               wl["ffn1_w_in"], wl["ffn1_w_out"])
        g_ffn1_in = _dw(da, hb, db, name="dw_ffn_in")
        g_ffn1_out = _dw(act, dob, name="dw_ffn_out").reshape(N_DEV, D_FF // N_DEV, D_MODEL)
        last = finish_scatter(pending, g_ffn1_in) if pending is not None else dx
        swap = swap_begin([g_ffn1_in, g_ffn1_out, g_mix_in, g_mix_out, g_ffn2_in, g_ffn2_out], last)
        dcond_rows[l] = jnp.concatenate([part1[0:3], part2[0:2], part_mo[0:1], part3[0:3]], axis=0)
        sg = small_grads
        sg["norm1_g"][l] = part1[3]
        sg["norm2_g"][l] = part2[2]
        sg["norm3_g"][l] = part3[3]
        sg["mix_norm_g"][l] = part_mo[1]
        sg["sgu_w"][l] = dw_cat.reshape(CHUNK, SGU_HEADS, CHUNK).transpose(1, 0, 2)
        sg["sgu_b"][l] = dbias[:, ::SGU_HEAD_DIM].T
        sg["pool_w"][l] = _diag_blocks(dpool_dense, len(POOL_WINDOWS))
        sg["pool_scale"][l] = dpool_scale[0]
        sg["conv_w"][l] = dconv8[0:3]
        sg["s5_c_re"][l] = _diag_blocks(dcre_d, S5_GROUPS).transpose(0, 2, 1)
        sg["s5_c_im"][l] = _diag_blocks(dcim_d, S5_GROUPS).transpose(0, 2, 1)
        sg["s5_d"][l] = dd[0]
        sg["s5_glu_w"][l] = dglu_w
        sg["s5_glu_b"][l] = dglu_b[0]
        d_are[l], d_aim[l] = dar8[0], dai8[0]
        d_bbre[l] = _diag_blocks(dbre_d, S5_GROUPS).transpose(0, 2, 1)
        d_bbim[l] = _diag_blocks(dbim_d, S5_GROUPS).transpose(0, 2, 1)
    grad_x = dx

    g_lre, g_lim, g_ldt, g_br, g_bi = _s5_prep_bwd(
        lre3, lim3, ldt3, br3, bi3,
        jnp.stack(d_are).reshape(lg, S5_STATE, 1), jnp.stack(d_aim).reshape(lg, S5_STATE, 1),
        jnp.stack(d_bbre).reshape(lg, S5_STATE, S5_GROUP_CH), jnp.stack(d_bbim).reshape(lg, S5_STATE, S5_GROUP_CH))
    small = {n: (jnp.stack(v) if isinstance(v, list) and v[0] is not None else v) for n, v in small_grads.items()}
    small["s5_lambda_re"] = g_lre.reshape(s5_lambda_re.shape)
    small["s5_lambda_im"] = g_lim.reshape(s5_lambda_im.shape)
    small["s5_log_dt"] = g_ldt.reshape(s5_log_dt.shape)
    small["s5_b_re"] = g_br.reshape(s5_b_re.shape)
    small["s5_b_im"] = g_bi.reshape(s5_b_im.shape)

    small_shapes = [(L, 3, W_GRP) if n == "conv_w" else (L, W_GRP, W_GRP) if n == "s5_glu_w" else W[n].shape
                    for n in SMALL]
    packed = _pack([small[n].reshape(s) for n, s in zip(SMALL, small_shapes)], PACK_COLS) + swap[1][0, 0]
    (pieces,) = _exchange([packed], True, "scatter_small")
    mine = _sum_parts(pieces, "sum_small")
    dcond = jnp.stack(dcond_rows).reshape(L * N_ADA, D_MODEL)
    summed, dcond_g = _exchange([mine, dcond], False, "gather_small_sums")
    summed_flat = summed.reshape(-1)
    n_whole = sum(math.prod(s) for s in small_shapes[:N_SMALL_WHOLE])
    conv_sum, glu_sum = _unpack(summed_flat[n_whole:], small_shapes[N_SMALL_WHOLE:])
    conv_sum = lax.dynamic_slice_in_dim(conv_sum, me * conv_cols, conv_cols, axis=2)
    glu_sum = lax.dynamic_slice_in_dim(glu_sum, me * glu_rows, glu_rows, axis=1)

    handle, token = scatter_begin(swap[0], [summed, dcond_g])
    pending = (0, handle)
    dcond_all = dcond_g.reshape(N_DEV, L, N_ADA * D_MODEL).transpose(1, 0, 2)
    dcond_mine = lax.dynamic_slice_in_dim(dcond_all, me * ncol, ncol, axis=2) + token[0, 0]
    g_ada_w, g_ada_b = _cond_bwd(c_all.T, dcond_mine, dcond_all)

    grads, deltas, new_m, new_v = {}, {}, {}, {}
    out = _adamw(ada_w.reshape(L * D_MODEL, ncol), g_ada_w.reshape(1, L * D_MODEL, ncol),
                 m_ada_w.reshape(L * D_MODEL, ncol), v_ada_w.reshape(L * D_MODEL, ncol), "adamw_ada_w")
    grads["ada_w"], deltas["ada_w"], new_m["ada_w"], new_v["ada_w"] = (o.reshape(ada_w.shape) for o in out)
    shapes = [W[n].shape for n in small_names]
    rows = packed_w.shape[0] * packed_w.shape[1]
    packed_g = _pack([summed_flat[:n_whole], conv_sum, glu_sum, g_ada_b], PACK_COLS)
    out = _adamw(packed_w.reshape(rows, PACK_COLS), packed_g.reshape(1, rows, PACK_COLS),
                 packed_m.reshape(rows, PACK_COLS), packed_v.reshape(rows, PACK_COLS), "adamw_small")
    for store, o in zip((grads, deltas, new_m, new_v), out):
        store.update(zip(small_names, _unpack(o.reshape(-1), shapes)))
    done = [store[n] for store in (grads, deltas, new_m, new_v) for n in small_names + ("ada_w",)]
    finish_scatter(pending, done + [big_out[n][0] for n in BIG])
    for n in BIG:
        res = big_out[n]
        if n in TRANSPOSED:
            res = tuple(r.transpose(0, 2, 1) for r in res)
        grads[n], deltas[n], new_m[n], new_v[n] = res

    return (loss, grad_x[None], *[grads[n] for n in WEIGHTS], *[deltas[n] for n in WEIGHTS],
            *[new_m[n] for n in WEIGHTS], *[new_v[n] for n in WEIGHTS])
```

```python
import functools
import math

import jax
import jax.numpy as jnp
from jax import lax
from jax.experimental import pallas as pl
from jax.experimental.pallas import tpu as pltpu

F32 = jnp.float32
BF16 = jnp.bfloat16

D_MODEL = 1024
SEQ = 2048
DEPTH = 4
N_DEV = 8
W_GRP = 256
CHUNK = 128
SGU_HEADS = 4
SGU_HEAD_DIM = 64
POOL_WINDOWS = (2, 4, 8, 16)
POOL_GROUP_DIM = 64
S5_GROUPS = 16
S5_GROUP_CH = 16
S5_STATE = 64
S5_LANES = S5_GROUPS * S5_STATE
S5_BLOCK = 512
P_IN = 1792
D_FF = 2816
FF_PIECE = 2 * D_FF // N_DEV
N_FF_CHUNK = D_FF // FF_PIECE
N_ADA = 9
EPS = 1e-6
ADAM_LR = 0.001
ADAM_B1 = 0.9
ADAM_B2 = 0.999
ADAM_EPS = 1e-08
ADAM_WD = 0.01
ADAM_STEP = 10

SUBLANES = 8
LANES = 128
VMEM_LIMIT = 56 * 1024 * 1024
TOKEN_TILE = 512
ROW_SUBTILE = 256
HIGHEST = lax.Precision.HIGHEST
MESH_AXES = ("x", "y", "c")

_GELU_C = math.sqrt(2.0 / math.pi)
_GELU_A = 0.044715


def _params(*sem):
    return pltpu.CompilerParams(dimension_semantics=tuple(sem) if sem else None, vmem_limit_bytes=VMEM_LIMIT)


def _dot(a, b):
    return jnp.dot(a, b, preferred_element_type=F32)


def _dot_nt(a, b):
    return lax.dot_general(a, b, (((1,), (1,)), ((), ())), preferred_element_type=F32)


def _dot_tn(a, b):
    return lax.dot_general(a, b, (((0,), (0,)), ((), ())), preferred_element_type=F32)


def _dot_hi(a, b):
    return jnp.dot(a, b, preferred_element_type=F32, precision=HIGHEST)


def _sigmoid(x):
    return 1.0 / (1.0 + jnp.exp(-x))


def _gelu(x):
    return 0.5 * x * (1.0 + jnp.tanh(_GELU_C * (x + _GELU_A * x * x * x)))


def _gelu_grad(x):
    t = jnp.tanh(_GELU_C * (x + _GELU_A * x * x * x))
    return 0.5 * (1.0 + t) + 0.5 * x * (1.0 - t * t) * (_GELU_C * (1.0 + 3.0 * _GELU_A * x * x))


def _rms(x):
    r = lax.rsqrt(jnp.mean(x * x, axis=-1, keepdims=True) + EPS)
    return x * r, r


def _rms_bwd(xhat, r, dxhat):
    return r * (dxhat - xhat * jnp.mean(dxhat * xhat, axis=-1, keepdims=True))


def _sum0(x):
    return jnp.sum(x, axis=0, keepdims=True)


def _me():
    return 4 * lax.axis_index("x") + 2 * lax.axis_index("y") + lax.axis_index("c")


def _exchange(srcs, scatter, name):
    n = len(srcs)
    out_shapes = []
    for s in srcs:
        piece = s.shape[1:] if scatter else s.shape
        out_shapes.append(jax.ShapeDtypeStruct((N_DEV,) + tuple(piece), s.dtype))

    def body(*refs):
        ins, outs = refs[:n], refs[n:2 * n]
        send_sems, recv_sems, local_sems = refs[2 * n:]
        x, y, c = lax.axis_index("x"), lax.axis_index("y"), lax.axis_index("c")
        me = 4 * x + 2 * y + c

        def src_of(i, dev):
            return ins[i].at[dev] if scatter else ins[i]

        local = [pltpu.make_async_copy(src_of(i, me), outs[i].at[me], local_sems.at[i]) for i in range(n)]
        for cp in local:
            cp.start()
        sends, recvs = [], []
        for k in range(1, N_DEV):
            px = 1 - x if (k >> 2) & 1 else x
            py = 1 - y if (k >> 1) & 1 else y
            pc = 1 - c if k & 1 else c
            peer = 4 * px + 2 * py + pc
            for i in range(n):
                sends.append(pltpu.make_async_remote_copy(
                    src_ref=src_of(i, peer), dst_ref=outs[i].at[me],
                    send_sem=send_sems.at[k - 1, i], recv_sem=recv_sems.at[k - 1, i],
                    device_id=(px, py, pc), device_id_type=pl.DeviceIdType.MESH))
                recvs.append(pltpu.make_async_remote_copy(
                    src_ref=src_of(i, peer), dst_ref=outs[i].at[peer],
                    send_sem=send_sems.at[k - 1, i], recv_sem=recv_sems.at[k - 1, i],
                    device_id=(px, py, pc), device_id_type=pl.DeviceIdType.MESH))
        for cp in sends:
            cp.start()
        for cp in recvs:
            cp.wait_recv()
        for cp in sends:
            cp.wait_send()
        for cp in local:
            cp.wait()

    hbm = pl.BlockSpec(memory_space=pltpu.HBM)
    return pl.pallas_call(
        body, name=name, out_shape=out_shapes,
        in_specs=[hbm] * n, out_specs=[hbm] * n,
        scratch_shapes=[pltpu.SemaphoreType.DMA((N_DEV - 1, n)), pltpu.SemaphoreType.DMA((N_DEV - 1, n)),
                        pltpu.SemaphoreType.DMA((n,))],
    )(*srcs)


ALL_PEERS = tuple(range(1, N_DEV))
OTHER_CHIPS = (2, 4, 6)
GATHER_PEERS = (1, 2, 4, 6)


def _peers(which):
    x, y, c = lax.axis_index("x"), lax.axis_index("y"), lax.axis_index("c")
    out = []
    for k in which:
        px = 1 - x if (k >> 2) & 1 else x
        py = 1 - y if (k >> 1) & 1 else y
        pc = 1 - c if k & 1 else c
        out.append(((px, py, pc), 4 * px + 2 * py + pc))
    return out


SIBLING_SWAP = "sibling"
N_CHIPS = 4


def _swap_copies(ins, lands, send_sems, recv_sems, with_recvs):
    x, y, c = lax.axis_index("x"), lax.axis_index("y"), lax.axis_index("c")
    sends, recvs = [], []
    for q in range(N_CHIPS):
        for i in range(len(ins)):
            sems = dict(send_sem=send_sems.at[q * len(ins) + i], recv_sem=recv_sems.at[q * len(ins) + i],
                        device_id=(x, y, 1 - c), device_id_type=pl.DeviceIdType.MESH)
            theirs, mine = 2 * q + 1 - c, 2 * q + c
            sends.append(pltpu.make_async_remote_copy(src_ref=ins[i].at[theirs], dst_ref=lands[i].at[theirs], **sems))
            if with_recvs:
                recvs.append(pltpu.make_async_remote_copy(src_ref=ins[i].at[theirs], dst_ref=lands[i].at[mine], **sems))
    return sends, recvs


def _split_copies(ins, lands, send_sems, recv_sems, scatter, with_recvs, which):
    if which == SIBLING_SWAP:
        return _swap_copies(ins, lands, send_sems, recv_sems, with_recvs)
    me = _me()
    sends, recvs = [], []
    for j, (dev, peer) in enumerate(_peers(which)):
        for i in range(len(ins)):
            src = ins[i].at[peer] if scatter else ins[i]
            slot = j * len(ins) + i
            sems = dict(send_sem=send_sems.at[slot], recv_sem=recv_sems.at[slot],
                        device_id=dev, device_id_type=pl.DeviceIdType.MESH)
            sends.append(pltpu.make_async_remote_copy(src_ref=src, dst_ref=lands[i].at[me], **sems))
            if with_recvs:
                recvs.append(pltpu.make_async_remote_copy(src_ref=src, dst_ref=lands[i].at[peer], **sems))
    return sends, recvs


_HBM = pl.BlockSpec(memory_space=pltpu.HBM)
_SEM = pl.BlockSpec(memory_space=pltpu.SEMAPHORE)
_EFFECT = pltpu.SideEffectType.DATAFLOW_SIDE_EFFECTING


def _place_own(srcs, scatter, name):
    n = len(srcs)
    halves = 2
    out_shapes, in_specs, out_specs = [], [], []
    for s in srcs:
        r, c = s.shape[-2:]
        out_shapes.append(jax.ShapeDtypeStruct((N_DEV, r, c), s.dtype))
        if scatter:
            in_specs.append(pl.BlockSpec((1, r // halves, c), lambda i, me: (me[0], i, 0)))
        else:
            in_specs.append(pl.BlockSpec((r // halves, c), lambda i, me: (i, 0)))
        out_specs.append(pl.BlockSpec((1, r // halves, c), lambda i, me: (me[0], i, 0)))

    def body(me_ref, *refs):
        for i in range(n):
            refs[n + i][0] = refs[i][0] if scatter else refs[i][...]

    return pl.pallas_call(
        body, name=name, out_shape=out_shapes,
        grid_spec=pltpu.PrefetchScalarGridSpec(num_scalar_prefetch=1, grid=(halves,), in_specs=in_specs,
                                               out_specs=out_specs),
        compiler_params=_params("arbitrary"),
    )(_me().reshape(1).astype(jnp.int32), *srcs)


def _core_index():
    return lax.axis_index("c").reshape(1).astype(jnp.int32)


def _presum(pieces, received, name):
    n = len(pieces)
    halves = 2
    specs = []
    for s in pieces:
        _, r, c = s.shape
        specs.append(pl.BlockSpec((1, r // halves, c), lambda q, i, core: (2 * q + core[0], i, 0)))

    def body(core_ref, *refs):
        for i in range(n):
            refs[2 * n + i][...] = (refs[i][...].astype(F32) + refs[n + i][...].astype(F32)).astype(BF16)

    return pl.pallas_call(
        body, name=name, out_shape=[jax.ShapeDtypeStruct(s.shape, BF16) for s in pieces],
        grid_spec=pltpu.PrefetchScalarGridSpec(num_scalar_prefetch=1, grid=(N_CHIPS, halves), in_specs=specs + specs,
                                               out_specs=specs),
        compiler_params=_params("arbitrary", "arbitrary"),
    )(_core_index(), *pieces, *received)


def _exchange_start(srcs, lands, after, scatter, name, which=ALL_PEERS):
    n = len(srcs)

    def body(*refs):
        ins, land_in = refs[:n], refs[n:2 * n]
        send_sems, recv_sems = refs[2 * n + 1], refs[2 * n + 2]
        token = refs[-1]
        sends, _ = _split_copies(ins, land_in, send_sems, recv_sems, scatter, False, which)
        for cp in sends:
            cp.start()
        token[...] = jnp.zeros_like(token)

    sem = pltpu.SemaphoreType.DMA(((N_CHIPS if which == SIBLING_SWAP else len(which)) * n,))
    out = pl.pallas_call(
        body, name=name,
        out_shape=(sem, sem, *[pltpu.HBM(s.shape, s.dtype) for s in srcs], *[pltpu.HBM(s.shape, s.dtype) for s in lands],
                   jax.ShapeDtypeStruct((SUBLANES, LANES), F32)),
        in_specs=[_HBM] * (2 * n) + [pl.BlockSpec(memory_space=pl.ANY)],
        out_specs=(_SEM, _SEM, *[_HBM] * (2 * n), pl.BlockSpec(memory_space=pltpu.VMEM)),
        input_output_aliases={i: 2 + i for i in range(2 * n)},
        compiler_params=pltpu.CompilerParams(has_side_effects=_EFFECT),
    )(*srcs, *lands, after)
    return (out[0], out[1], out[2:2 + n], out[2 + n:2 + 2 * n]), out[-1]


def _exchange_wait(handle, after, scatter, name, which=ALL_PEERS, with_srcs=False):
    send_sems, recv_sems, srcs, lands = handle
    n = len(srcs)
    after = list(after) if isinstance(after, (list, tuple)) else [after]

    def body(*refs):
        ins, land_in = refs[:n], refs[n:2 * n]
        sends, recvs = _split_copies(ins, land_in, refs[2 * n], refs[2 * n + 1], scatter, True, which)
        for cp in sends:
            cp.wait_send()
        for cp in recvs:
            cp.wait_recv()

    out = pl.pallas_call(
        body, name=name,
        out_shape=(*[pltpu.HBM(s.shape, s.dtype) for s in srcs], *[pltpu.HBM(s.shape, s.dtype) for s in lands]),
        in_specs=[_HBM] * (2 * n) + [_SEM, _SEM] + [pl.BlockSpec(memory_space=pl.ANY)] * len(after),
        out_specs=tuple([_HBM] * (2 * n)),
        input_output_aliases={i: i for i in range(2 * n)},
        compiler_params=pltpu.CompilerParams(has_side_effects=_EFFECT),
    )(*srcs, *lands, send_sems, recv_sems, *after)
    return (out[:n], out[n:]) if with_srcs else out[n:]


def _gather_sibling(lands, name):
    n = len(lands)
    chips = ((0, 1), (1, 0), (1, 1))

    def body(*refs):
        ins, outs = refs[:n], refs[n:2 * n]
        send_sems, recv_sems = refs[2 * n], refs[2 * n + 1]
        x, y, c = lax.axis_index("x"), lax.axis_index("y"), lax.axis_index("c")
        sends, recvs = [], []
        for j, (bx, by) in enumerate(chips):
            chip = 4 * (1 - x if bx else x) + 2 * (1 - y if by else y)
            for i in range(n):
                sems = dict(send_sem=send_sems.at[j * n + i], recv_sem=recv_sems.at[j * n + i],
                            device_id=(x, y, 1 - c), device_id_type=pl.DeviceIdType.MESH)
                sends.append(pltpu.make_async_remote_copy(src_ref=ins[i].at[chip + c], dst_ref=outs[i].at[chip + c], **sems))
                recvs.append(pltpu.make_async_remote_copy(src_ref=ins[i].at[chip + c], dst_ref=outs[i].at[chip + 1 - c],
                                                          **sems))
        for cp in sends:
            cp.start()
        for cp in recvs:
            cp.wait_recv()
        for cp in sends:
            cp.wait_send()

    return pl.pallas_call(
        body, name=name, out_shape=[jax.ShapeDtypeStruct(a.shape, a.dtype) for a in lands],
        in_specs=[_HBM] * n, out_specs=[_HBM] * n,
        scratch_shapes=[pltpu.SemaphoreType.DMA((len(chips) * n,)), pltpu.SemaphoreType.DMA((len(chips) * n,))],
        input_output_aliases={i: i for i in range(n)},
    )(*lands)


def _cond_fwd(c_all, ada_w, ada_b_mine):
    ncol = ada_w.shape[-1]

    def body(c_ref, w_ref, b_ref, o_ref):
        c = c_ref[...]
        ca = (c * _sigmoid(c)).astype(BF16)
        o_ref[0] = _dot(ca, w_ref[0].astype(BF16)) + b_ref[0]

    return pl.pallas_call(
        body, name="cond_fwd", grid=(DEPTH,),
        out_shape=jax.ShapeDtypeStruct((DEPTH, N_DEV, ncol), F32),
        in_specs=[pl.BlockSpec((N_DEV, D_MODEL), lambda l: (0, 0)),
                  pl.BlockSpec((1, D_MODEL, ncol), lambda l: (l, 0, 0)),
                  pl.BlockSpec((1, 1, ncol), lambda l: (l, 0, 0))],
        out_specs=pl.BlockSpec((1, N_DEV, ncol), lambda l: (l, 0, 0)),
        compiler_params=_params("arbitrary"),
    )(c_all, ada_w, ada_b_mine)


def _cond_bwd(c_all_t, dcond_mine, dcond_all):
    ncol = dcond_mine.shape[-1]
    nall = dcond_all.shape[-1]

    def body(ct_ref, d_ref, da_ref, gw_ref, gb_ref):
        ct = ct_ref[...]
        ct = ct * _sigmoid(ct)
        d = d_ref[0]
        acc = ct[:, 0:1] * d[0:1, :]
        for b in range(1, N_DEV):
            acc = acc + ct[:, b:b + 1] * d[b:b + 1, :]
        gw_ref[0] = acc
        gb_ref[0] = _sum0(da_ref[0])

    return pl.pallas_call(
        body, name="cond_bwd", grid=(DEPTH,),
        out_shape=(jax.ShapeDtypeStruct((DEPTH, D_MODEL, ncol), F32), jax.ShapeDtypeStruct((DEPTH, 1, nall), F32)),
        in_specs=[pl.BlockSpec((D_MODEL, N_DEV), lambda l: (0, 0)),
                  pl.BlockSpec((1, N_DEV, ncol), lambda l: (l, 0, 0)),
                  pl.BlockSpec((1, N_DEV, nall), lambda l: (l, 0, 0))],
        out_specs=(pl.BlockSpec((1, D_MODEL, ncol), lambda l: (l, 0, 0)),
                   pl.BlockSpec((1, 1, nall), lambda l: (l, 0, 0))),
        compiler_params=_params("arbitrary"),
    )(c_all_t, dcond_mine, dcond_all)


def _modnorm(x, g, shift, scale):
    xhat, r = _rms(x)
    return (xhat * g) * (1.0 + scale) + shift, xhat, r


def _modnorm_bwd(xhat, r, g, scale, dh):
    n = xhat * g
    dn = dh * (1.0 + scale)
    dx = _rms_bwd(xhat, r, dn * g)
    return dx, _sum0(dh), _sum0(dh * n), _sum0(dn * xhat)


def _row_spec(rows):
    return pl.BlockSpec((rows, D_MODEL), lambda *_: (0, 0))


def _ffn_fwd(x, cond3, g, w_in_g, w_out_g):
    tm = TOKEN_TILE
    last = N_FF_CHUNK - 1

    def body(x_ref, cond_ref, g_ref, wa_ref, wb_ref, wo_ref, xo_ref, f_ref, a_ref, b_ref, h_ref, h_scr, acc_scr):
        j = pl.program_id(1)

        @pl.when(j == 0)
        def _():
            h, _, _ = _modnorm(x_ref[...], g_ref[...], cond_ref[0:1, :], cond_ref[1:2, :])
            hb = h.astype(BF16)
            h_scr[...] = hb
            h_ref[...] = hb
            acc_scr[...] = jnp.zeros_like(acc_scr)

        wa, wb, wo = wa_ref[0], wb_ref[0], wo_ref[0]
        for r in range(tm // ROW_SUBTILE):
            rows = slice(r * ROW_SUBTILE, (r + 1) * ROW_SUBTILE)
            h = h_scr[rows, :]
            a = _dot_nt(h, wa)
            b = _dot_nt(h, wb)
            a_ref[0, rows, :] = a.astype(BF16)
            b_ref[0, rows, :] = b.astype(BF16)
            act = (a * _sigmoid(a)) * b
            acc_scr[rows, :] += _dot(act.astype(BF16), wo)

        @pl.when(j == last)
        def _():
            f = acc_scr[...]
            f_ref[...] = f
            xo_ref[...] = x_ref[...] + (0.5 * cond_ref[2:3, :]) * f

    tok = pl.BlockSpec((tm, D_MODEL), lambda i, j: (i, 0))
    chunk = pl.BlockSpec((1, tm, FF_PIECE), lambda i, j: (j, i, 0))
    chunk_shape = jax.ShapeDtypeStruct((N_FF_CHUNK, SEQ, FF_PIECE), BF16)
    return pl.pallas_call(
        body, name="ffn_fwd", grid=(SEQ // tm, N_FF_CHUNK),
        out_shape=(jax.ShapeDtypeStruct((SEQ, D_MODEL), F32), jax.ShapeDtypeStruct((SEQ, D_MODEL), F32),
                   chunk_shape, chunk_shape, jax.ShapeDtypeStruct((SEQ, D_MODEL), BF16)),
        in_specs=[tok, _row_spec(3), _row_spec(1),
                  pl.BlockSpec((1, FF_PIECE, D_MODEL), lambda i, j: (j, 0, 0)),
                  pl.BlockSpec((1, FF_PIECE, D_MODEL), lambda i, j: (j + N_FF_CHUNK, 0, 0)),
                  pl.BlockSpec((1, FF_PIECE, D_MODEL), lambda i, j: (j, 0, 0))],
        out_specs=(tok, tok, chunk, chunk, tok),
        scratch_shapes=[pltpu.VMEM((tm, D_MODEL), BF16), pltpu.VMEM((tm, D_MODEL), F32)],
        compiler_params=_params("arbitrary", "arbitrary"),
    )(x, cond3, g, w_in_g, w_in_g, w_out_g)


def _ffn_bwd(dy, x, f, a_sv, b_sv, cond3, g, w_in_g, w_out_g):
    tm = TOKEN_TILE
    last = N_FF_CHUNK - 1

    def body(dy_ref, x_ref, f_ref, a_ref, b_ref, cond_ref, g_ref, wa_ref, wb_ref, wo_ref,
             dx_ref, da_ref, db_ref, act_ref, do_ref, part_ref, do_scr, dh_scr):
        i, j = pl.program_id(0), pl.program_id(1)

        @pl.when(j == 0)
        def _():
            do = ((0.5 * cond_ref[2:3, :]) * dy_ref[...]).astype(BF16)
            do_scr[...] = do
            do_ref[...] = do
            dh_scr[...] = jnp.zeros_like(dh_scr)

        @pl.when((i == 0) & (j == 0))
        def _():
            part_ref[...] = jnp.zeros_like(part_ref)

        wa, wb, wo = wa_ref[0], wb_ref[0], wo_ref[0]
        for r in range(tm // ROW_SUBTILE):
            rows = slice(r * ROW_SUBTILE, (r + 1) * ROW_SUBTILE)
            do = do_scr[rows, :]
            a = a_ref[0, rows, :].astype(F32)
            b = b_ref[0, rows, :].astype(F32)
            dact = _dot_nt(do, wo)
            sig = _sigmoid(a)
            s = a * sig
            da = (dact * b * (sig * (1.0 + a * (1.0 - sig)))).astype(BF16)
            db = (dact * s).astype(BF16)
            da_ref[0, rows, :] = da
            db_ref[0, rows, :] = db
            act_ref[0, rows, :] = (s * b).astype(BF16)
            dh_scr[rows, :] += _dot(da, wa) + _dot(db, wb)

        @pl.when(j == last)
        def _():
            dyv = dy_ref[...]
            xhat, r = _rms(x_ref[...])
            dx, dshift, dscale, dg = _modnorm_bwd(xhat, r, g_ref[...], cond_ref[1:2, :], dh_scr[...])
            dx_ref[...] = dyv + dx
            part_ref[0:1, :] += dshift
            part_ref[1:2, :] += dscale
            part_ref[2:3, :] += _sum0(0.5 * dyv * f_ref[...])
            part_ref[3:4, :] += dg

    tok = pl.BlockSpec((tm, D_MODEL), lambda i, j: (i, 0))
    chunk = pl.BlockSpec((1, tm, FF_PIECE), lambda i, j: (j, i, 0))
    chunk_shape = jax.ShapeDtypeStruct((N_FF_CHUNK, SEQ, FF_PIECE), BF16)
    return pl.pallas_call(
        body, name="ffn_bwd", grid=(SEQ // tm, N_FF_CHUNK),
        out_shape=(jax.ShapeDtypeStruct((SEQ, D_MODEL), F32), chunk_shape, chunk_shape, chunk_shape,
                   jax.ShapeDtypeStruct((SEQ, D_MODEL), BF16), jax.ShapeDtypeStruct((SUBLANES, D_MODEL), F32)),
        in_specs=[tok, tok, tok, chunk, chunk, _row_spec(3), _row_spec(1),
                  pl.BlockSpec((1, FF_PIECE, D_MODEL), lambda i, j: (j, 0, 0)),
                  pl.BlockSpec((1, FF_PIECE, D_MODEL), lambda i, j: (j + N_FF_CHUNK, 0, 0)),
                  pl.BlockSpec((1, FF_PIECE, D_MODEL), lambda i, j: (j, 0, 0))],
        out_specs=(tok, chunk, chunk, chunk, tok, _row_spec(SUBLANES)),
        scratch_shapes=[pltpu.VMEM((tm, D_MODEL), BF16), pltpu.VMEM((tm, D_MODEL), F32)],
        compiler_params=_params("arbitrary", "arbitrary"),
    )(dy, x, f, a_sv, b_sv, cond3, g, w_in_g, w_in_g, w_out_g)


def _dw(lhs_a, rhs, lhs_b=None, name="dw"):
    pa, s, m = lhs_a.shape
    nn = rhs.shape[-1]
    pb = 0 if lhs_b is None else lhs_b.shape[0]
    two = lhs_b is not None

    def body(*refs):
        if two:
            a_ref, b_ref, r_ref, o_ref = refs
            p = pl.program_id(0)

            @pl.when(p < pa)
            def _():
                o_ref[0] = _dot_tn(a_ref[0], r_ref[...]).astype(BF16)

            @pl.when(p >= pa)
            def _():
                o_ref[0] = _dot_tn(b_ref[0], r_ref[...]).astype(BF16)
        else:
            a_ref, r_ref, o_ref = refs
            o_ref[0] = _dot_tn(a_ref[0], r_ref[...]).astype(BF16)

    if two:
        in_specs = [pl.BlockSpec((1, s, m), lambda p: (jnp.minimum(p, pa - 1), 0, 0)),
                    pl.BlockSpec((1, s, m), lambda p: (jnp.maximum(p - pa, 0), 0, 0))]
        args = (lhs_a, lhs_b, rhs)
    else:
        in_specs = [pl.BlockSpec((1, s, m), lambda p: (p, 0, 0))]
        args = (lhs_a, rhs)
    in_specs.append(pl.BlockSpec((s, nn), lambda p: (0, 0)))
    return pl.pallas_call(
        body, name=name, grid=(pa + pb,),
        out_shape=jax.ShapeDtypeStruct((pa + pb, m, nn), BF16),
        in_specs=in_specs, out_specs=pl.BlockSpec((1, m, nn), lambda p: (p, 0, 0)),
        compiler_params=_params("arbitrary"),
    )(*args)


def _mix_in_fwd(x, cond2, g, w):
    tm = TOKEN_TILE

    def body(x_ref, cond_ref, g_ref, w_ref, z_ref):
        h, _, _ = _modnorm(x_ref[...], g_ref[...], cond_ref[0:1, :], cond_ref[1:2, :])
        z_ref[...] = _dot_nt(h.astype(BF16), w_ref[...])

    return pl.pallas_call(
        body, name="mix_in_fwd", grid=(SEQ // tm,),
        out_shape=jax.ShapeDtypeStruct((SEQ, P_IN), F32),
        in_specs=[pl.BlockSpec((tm, D_MODEL), lambda i: (i, 0)), _row_spec(2), _row_spec(1),
                  pl.BlockSpec((P_IN, D_MODEL), lambda i: (0, 0))],
        out_specs=pl.BlockSpec((tm, P_IN), lambda i: (i, 0)),
        compiler_params=_params("arbitrary"),
    )(x, cond2, g, w)


MIX_SLABS = ((0, 2 * W_GRP), (2 * W_GRP, 3 * W_GRP), (3 * W_GRP, 6 * W_GRP), (6 * W_GRP, 7 * W_GRP))


def _mix_in_bwd(dzs, x, dy, cond2, g, w):
    tm = TOKEN_TILE

    def body(dza_ref, dzb_ref, dzc_ref, dzd_ref, x_ref, dy_ref, cond_ref, g_ref, w_ref, dx_ref, h_ref, dzo_ref, part_ref):
        i = pl.program_id(0)

        @pl.when(i == 0)
        def _():
            part_ref[...] = jnp.zeros_like(part_ref)

        h, xhat, r = _modnorm(x_ref[...], g_ref[...], cond_ref[0:1, :], cond_ref[1:2, :])
        h_ref[...] = h.astype(BF16)
        dh = None
        for (lo, hi), d_ref in zip(MIX_SLABS, (dza_ref, dzb_ref, dzc_ref, dzd_ref)):
            dzb = d_ref[...].astype(BF16)
            dzo_ref[:, lo:hi] = dzb
            t = _dot(dzb, w_ref[lo:hi, :])
            dh = t if dh is None else dh + t
        dx, dshift, dscale, dg = _modnorm_bwd(xhat, r, g_ref[...], cond_ref[1:2, :], dh)
        dx_ref[...] = dy_ref[...] + dx
        part_ref[0:1, :] += dshift
        part_ref[1:2, :] += dscale
        part_ref[2:3, :] += dg

    tok = pl.BlockSpec((tm, D_MODEL), lambda i: (i, 0))
    ztok = pl.BlockSpec((tm, P_IN), lambda i: (i, 0))
    slabs = [pl.BlockSpec((tm, hi - lo), lambda i: (i, 0)) for lo, hi in MIX_SLABS]
    return pl.pallas_call(
        body, name="mix_in_bwd", grid=(SEQ // tm,),
        out_shape=(jax.ShapeDtypeStruct((SEQ, D_MODEL), F32), jax.ShapeDtypeStruct((SEQ, D_MODEL), BF16),
                   jax.ShapeDtypeStruct((SEQ, P_IN), BF16), jax.ShapeDtypeStruct((SUBLANES, D_MODEL), F32)),
        in_specs=[*slabs, tok, tok, _row_spec(2), _row_spec(1), pl.BlockSpec((P_IN, D_MODEL), lambda i: (0, 0))],
        out_specs=(tok, tok, ztok, _row_spec(SUBLANES)),
        compiler_params=_params("arbitrary"),
    )(*dzs, x, dy, cond2, g, w)


def _group_norm(ys, g_ref):
    out = []
    for k, y in enumerate(ys):
        yhat, r = _rms(y)
        out.append((yhat, r, g_ref[:, k * W_GRP:(k + 1) * W_GRP]))
    return out


def _mix_out_fwd(x, ys, g, gate, w):
    tm = TOKEN_TILE

    def body(x_ref, ya_ref, yb_ref, yc_ref, yd_ref, g_ref, gate_ref, w_ref, xo_ref, mo_ref):
        acc = None
        for k, (yhat, _, gk) in enumerate(_group_norm([r[...] for r in (ya_ref, yb_ref, yc_ref, yd_ref)], g_ref)):
            t = _dot((yhat * gk).astype(BF16), w_ref[k * W_GRP:(k + 1) * W_GRP, :])
            acc = t if acc is None else acc + t
        mo_ref[...] = acc
        xo_ref[...] = x_ref[...] + gate_ref[...] * acc

    tok = pl.BlockSpec((tm, D_MODEL), lambda i: (i, 0))
    ytok = pl.BlockSpec((tm, W_GRP), lambda i: (i, 0))
    sh = jax.ShapeDtypeStruct((SEQ, D_MODEL), F32)
    return pl.pallas_call(
        body, name="mix_out_fwd", grid=(SEQ // tm,),
        out_shape=(sh, sh),
        in_specs=[tok, ytok, ytok, ytok, ytok, _row_spec(1), _row_spec(1),
                  pl.BlockSpec((D_MODEL, D_MODEL), lambda i: (0, 0))],
        out_specs=(tok, tok), compiler_params=_params("arbitrary"),
    )(x, *ys, g, gate, w)


def _mix_out_bwd(dy, mo, ys, g, gate, w):
    tm = TOKEN_TILE

    def body(dy_ref, mo_ref, ya_ref, yb_ref, yc_ref, yd_ref, g_ref, gate_ref, w_ref,
             da_ref, db_ref, dc_ref, dd_ref, yn_ref, dmo_ref, part_ref):
        i = pl.program_id(0)

        @pl.when(i == 0)
        def _():
            part_ref[...] = jnp.zeros_like(part_ref)

        dyv = dy_ref[...]
        dmo = (gate_ref[...] * dyv).astype(BF16)
        dmo_ref[...] = dmo
        dyn = _dot_nt(dmo, w_ref[...])
        norms = _group_norm([r[...] for r in (ya_ref, yb_ref, yc_ref, yd_ref)], g_ref)
        for k, ((yhat, r, gk), o_ref) in enumerate(zip(norms, (da_ref, db_ref, dc_ref, dd_ref))):
            sl = slice(k * W_GRP, (k + 1) * W_GRP)
            yn_ref[:, sl] = (yhat * gk).astype(BF16)
            dk = dyn[:, sl]
            o_ref[...] = _rms_bwd(yhat, r, dk * gk)
            part_ref[1:2, sl] += _sum0(dk * yhat)
        part_ref[0:1, :] += _sum0(dyv * mo_ref[...])

    tok = pl.BlockSpec((tm, D_MODEL), lambda i: (i, 0))
    ytok = pl.BlockSpec((tm, W_GRP), lambda i: (i, 0))
    ysh = jax.ShapeDtypeStruct((SEQ, W_GRP), F32)
    return pl.pallas_call(
        body, name="mix_out_bwd", grid=(SEQ // tm,),
        out_shape=(ysh, ysh, ysh, ysh, jax.ShapeDtypeStruct((SEQ, D_MODEL), BF16),
                   jax.ShapeDtypeStruct((SEQ, D_MODEL), BF16), jax.ShapeDtypeStruct((SUBLANES, D_MODEL), F32)),
        in_specs=[tok, tok, ytok, ytok, ytok, ytok, _row_spec(1), _row_spec(1),
                  pl.BlockSpec((D_MODEL, D_MODEL), lambda i: (0, 0))],
        out_specs=(ytok, ytok, ytok, ytok, tok, tok, _row_spec(SUBLANES)),
        compiler_params=_params("arbitrary"),
    )(dy, mo, *ys, g, gate, w)


def _shift_down(v, k, rows):
    return jnp.where(rows >= k, pltpu.roll(v, k, axis=0), 0.0)


def _shift_up(v, k, rows):
    n = v.shape[0]
    return jnp.where(rows < n - k, pltpu.roll(v, n - k, axis=0), 0.0)


def _zslab(width, index):
    return pl.BlockSpec((SEQ, width), lambda *_: (0, index))


def _full(shape):
    return pl.BlockSpec(shape, lambda *_: (0,) * len(shape))


def _head_avg():
    r = lax.broadcasted_iota(jnp.int32, (W_GRP, W_GRP), 0) // SGU_HEAD_DIM
    c = lax.broadcasted_iota(jnp.int32, (W_GRP, W_GRP), 1) // SGU_HEAD_DIM
    return jnp.where(r == c, 1.0 / SGU_HEAD_DIM, 0.0).astype(F32)


def _sgu_norm(za):
    z = _gelu(za)
    u, v = z[:, :W_GRP], z[:, W_GRP:]
    avg = _head_avg()
    vc = v - _dot_hi(v, avg)
    rstd = lax.rsqrt(_dot_hi(vc * vc, avg) + EPS)
    return u, vc * rstd, rstd


def _sgu_masked_w(w_ref):
    t = lax.broadcasted_iota(jnp.int32, (CHUNK, CHUNK), 0)
    s = lax.broadcasted_iota(jnp.int32, (CHUNK, CHUNK), 1)
    tril = t >= s
    return [jnp.where(tril, w_ref[:, h * CHUNK:(h + 1) * CHUNK], 0.0).astype(BF16) for h in range(SGU_HEADS)]


def _head_of_lane():
    return lax.broadcasted_iota(jnp.int32, (CHUNK, W_GRP), 1) // SGU_HEAD_DIM


def _sgu_fwd(z, w_cat, bias):
    def body(z_ref, w_ref, b_ref, y_ref, vn_scr, u_scr):
        u, vn, _ = _sgu_norm(z_ref[...])
        vn_scr[...] = vn.astype(BF16)
        u_scr[...] = u
        ws = _sgu_masked_w(w_ref)
        head = _head_of_lane()
        bias_v = b_ref[...]

        def chunk(n, carry):
            rows = pl.ds(pl.multiple_of(n * CHUNK, CHUNK), CHUNK)
            vb = vn_scr[rows, :]
            mixed = bias_v
            for h in range(SGU_HEADS):
                mixed = mixed + jnp.where(head == h, _dot(ws[h], vb), 0.0)
            y_ref[rows, :] = u_scr[rows, :] * mixed
            return carry

        lax.fori_loop(0, SEQ // CHUNK, chunk, 0)

    return pl.pallas_call(
        body, name="sgu_fwd", grid=(1,),
        out_shape=jax.ShapeDtypeStruct((SEQ, W_GRP), F32),
        in_specs=[_zslab(2 * W_GRP, 0), _full((CHUNK, SGU_HEADS * CHUNK)), _full((CHUNK, W_GRP))],
        out_specs=_full((SEQ, W_GRP)),
        scratch_shapes=[pltpu.VMEM((SEQ, W_GRP), BF16), pltpu.VMEM((SEQ, W_GRP), F32)],
        compiler_params=_params("arbitrary"),
    )(z, w_cat, bias)


def _sgu_bwd(z, dy, w_cat, bias):
    def body(z_ref, dy_ref, w_ref, b_ref, dz_ref, dw_ref, db_ref, vn_scr, u_scr, dvn_scr, du_scr):
        za = z_ref[...]
        u, vn, rstd = _sgu_norm(za)
        vn_scr[...] = vn.astype(BF16)
        u_scr[...] = u
        ws = _sgu_masked_w(w_ref)
        head = _head_of_lane()
        bias_v = b_ref[...]

        def chunk(n, carry):
            dws, dbias = carry
            rows = pl.ds(pl.multiple_of(n * CHUNK, CHUNK), CHUNK)
            vb = vn_scr[rows, :]
            mixed = bias_v
            for h in range(SGU_HEADS):
                mixed = mixed + jnp.where(head == h, _dot(ws[h], vb), 0.0)
            dyc = dy_ref[rows, :]
            du_scr[rows, :] = dyc * mixed
            dmixed = dyc * u_scr[rows, :]
            dvn = jnp.zeros((CHUNK, W_GRP), F32)
            new_dws = []
            for h in range(SGU_HEADS):
                dm = jnp.where(head == h, dmixed, 0.0).astype(BF16)
                new_dws.append(dws[h] + _dot_nt(dm, vb))
                dvn = dvn + _dot_tn(ws[h], dm)
            dvn_scr[rows, :] = dvn
            return tuple(new_dws), dbias + dmixed

        zero_w = tuple(jnp.zeros((CHUNK, CHUNK), F32) for _ in range(SGU_HEADS))
        dws, dbias = lax.fori_loop(0, SEQ // CHUNK, chunk, (zero_w, jnp.zeros((CHUNK, W_GRP), F32)))
        t = lax.broadcasted_iota(jnp.int32, (CHUNK, CHUNK), 0)
        s = lax.broadcasted_iota(jnp.int32, (CHUNK, CHUNK), 1)
        for h in range(SGU_HEADS):
            dw_ref[:, h * CHUNK:(h + 1) * CHUNK] = jnp.where(t >= s, dws[h], 0.0)
        avg = _head_avg()
        db_ref[...] = _dot_hi(dbias, avg) * float(SGU_HEAD_DIM)
        dvn = dvn_scr[...]
        dv = rstd * (dvn - _dot_hi(dvn, avg) - vn * _dot_hi(dvn * vn, avg))
        gg = _gelu_grad(za)
        dz_ref[:, :W_GRP] = gg[:, :W_GRP] * du_scr[...]
        dz_ref[:, W_GRP:] = gg[:, W_GRP:] * dv

    return pl.pallas_call(
        body, name="sgu_bwd", grid=(1,),
        out_shape=(jax.ShapeDtypeStruct((SEQ, 2 * W_GRP), F32), jax.ShapeDtypeStruct((CHUNK, SGU_HEADS * CHUNK), F32),
                   jax.ShapeDtypeStruct((CHUNK, W_GRP), F32)),
        in_specs=[_zslab(2 * W_GRP, 0), _full((SEQ, W_GRP)), _full((CHUNK, SGU_HEADS * CHUNK)), _full((CHUNK, W_GRP))],
        out_specs=(_full((SEQ, 2 * W_GRP)), _full((CHUNK, SGU_HEADS * CHUNK)), _full((CHUNK, W_GRP))),
        scratch_shapes=[pltpu.VMEM((SEQ, W_GRP), BF16), pltpu.VMEM((SEQ, W_GRP), F32),
                        pltpu.VMEM((SEQ, W_GRP), F32), pltpu.VMEM((SEQ, W_GRP), F32)],
        compiler_params=_params("arbitrary"),
    )(z, dy, w_cat, bias)


def _pool_window_of_lane(shape):
    grp = lax.broadcasted_iota(jnp.int32, shape, 1) // POOL_GROUP_DIM
    win = jnp.full(shape, POOL_WINDOWS[0], jnp.int32)
    for k in range(1, len(POOL_WINDOWS)):
        win = jnp.where(grp == k, POOL_WINDOWS[k], win)
    return grp, win


def _pool_select(levels, grp):
    out = levels[0]
    for k in range(1, len(levels)):
        out = jnp.where(grp == k, levels[k], out)
    return out


def _pool_p(z):
    shape = z.shape
    rows = lax.broadcasted_iota(jnp.int32, shape, 0)
    grp, win = _pool_window_of_lane(shape)
    levels, s, k = [], z, 1
    for _ in POOL_WINDOWS:
        s = s + _shift_down(s, k, rows)
        levels.append(s)
        k *= 2
    inv = 1.0 / jnp.minimum(rows + 1, win).astype(F32)
    return _pool_select(levels, grp) * inv - z, inv, rows, grp


def _pool_fwd(z, w_bd, scale):
    def body(z_ref, w_ref, s_ref, y_ref):
        p, _, _, _ = _pool_p(z_ref[...])
        y_ref[...] = _dot(p.astype(BF16), w_ref[...]) * s_ref[...]

    return pl.pallas_call(
        body, name="pool_fwd", grid=(1,),
        out_shape=jax.ShapeDtypeStruct((SEQ, W_GRP), F32),
        in_specs=[_zslab(W_GRP, 2), _full((W_GRP, W_GRP)), _full((1, W_GRP))],
        out_specs=_full((SEQ, W_GRP)), compiler_params=_params("arbitrary"),
    )(z, w_bd, scale)


def _pool_bwd(z, dy, w_bd, scale):
    def body(z_ref, dy_ref, w_ref, s_ref, dz_ref, dw_ref, ds_ref):
        p, inv, rows, grp = _pool_p(z_ref[...])
        pb = p.astype(BF16)
        dyv = dy_ref[...]
        ds_ref[...] = _sum0(dyv * _dot(pb, w_ref[...]))
        dpre = (dyv * s_ref[...]).astype(BF16)
        dw_ref[...] = _dot_tn(pb, dpre)
        dp = _dot_nt(dpre, w_ref[...])
        q = dp * inv
        levels, s, k = [], q, 1
        for _ in POOL_WINDOWS:
            s = s + _shift_up(s, k, rows)
            levels.append(s)
            k *= 2
        dz_ref[...] = _pool_select(levels, grp) - dp

    return pl.pallas_call(
        body, name="pool_bwd", grid=(1,),
        out_shape=(jax.ShapeDtypeStruct((SEQ, W_GRP), F32), jax.ShapeDtypeStruct((W_GRP, W_GRP), F32),
                   jax.ShapeDtypeStruct((1, W_GRP), F32)),
        in_specs=[_zslab(W_GRP, 2), _full((SEQ, W_GRP)), _full((W_GRP, W_GRP)), _full((1, W_GRP))],
        out_specs=(_full((SEQ, W_GRP)), _full((W_GRP, W_GRP)), _full((1, W_GRP))),
        compiler_params=_params("arbitrary"),
    )(z, dy, w_bd, scale)


def _conv_fwd(z, w):
    def body(z_ref, w_ref, y_ref):
        zc = z_ref[...]
        bg, cg, xh = zc[:, :W_GRP], zc[:, W_GRP:2 * W_GRP], zc[:, 2 * W_GRP:]
        rows = lax.broadcasted_iota(jnp.int32, (SEQ, W_GRP), 0)
        y = cg * xh
        conv = w_ref[0:1, :] * _shift_down(y, 2, rows) + w_ref[1:2, :] * _shift_down(y, 1, rows) + w_ref[2:3, :] * y
        y_ref[...] = bg * conv

    return pl.pallas_call(
        body, name="conv_fwd", grid=(1,),
        out_shape=jax.ShapeDtypeStruct((SEQ, W_GRP), F32),
        in_specs=[_zslab(3 * W_GRP, 1), _full((3, W_GRP))],
        out_specs=_full((SEQ, W_GRP)), compiler_params=_params("arbitrary"),
    )(z, w)


def _conv_bwd(z, dy, w):
    def body(z_ref, dy_ref, w_ref, dz_ref, dw_ref):
        zc = z_ref[...]
        bg, cg, xh = zc[:, :W_GRP], zc[:, W_GRP:2 * W_GRP], zc[:, 2 * W_GRP:]
        rows = lax.broadcasted_iota(jnp.int32, (SEQ, W_GRP), 0)
        y = cg * xh
        y2, y1 = _shift_down(y, 2, rows), _shift_down(y, 1, rows)
        conv = w_ref[0:1, :] * y2 + w_ref[1:2, :] * y1 + w_ref[2:3, :] * y
        dyv = dy_ref[...]
        dconv = dyv * bg
        dw_ref[...] = jnp.zeros_like(dw_ref)
        dw_ref[0:1, :] = _sum0(dconv * y2)
        dw_ref[1:2, :] = _sum0(dconv * y1)
        dw_ref[2:3, :] = _sum0(dconv * y)
        dyy = (w_ref[0:1, :] * _shift_up(dconv, 2, rows) + w_ref[1:2, :] * _shift_up(dconv, 1, rows)
               + w_ref[2:3, :] * dconv)
        dz_ref[:, :W_GRP] = dyv * conv
        dz_ref[:, W_GRP:2 * W_GRP] = dyy * xh
        dz_ref[:, 2 * W_GRP:] = dyy * cg

    return pl.pallas_call(
        body, name="conv_bwd", grid=(1,),
        out_shape=(jax.ShapeDtypeStruct((SEQ, 3 * W_GRP), F32), jax.ShapeDtypeStruct((SUBLANES, W_GRP), F32)),
        in_specs=[_zslab(3 * W_GRP, 1), _full((SEQ, W_GRP)), _full((3, W_GRP))],
        out_specs=(_full((SEQ, 3 * W_GRP)), _full((SUBLANES, W_GRP))),
        compiler_params=_params("arbitrary"),
    )(z, dy, w)


def _s5_disc(lre, lim, ldt, br, bi):
    dt = jnp.exp(ldt)
    mag = jnp.exp(lre * dt)
    ang = lim * dt
    a_re, a_im = mag * jnp.cos(ang), mag * jnp.sin(ang)
    nr, ni = a_re - 1.0, a_im
    den = lre * lre + lim * lim
    k_re = (nr * lre + ni * lim) / den
    k_im = (ni * lre - nr * lim) / den
    return a_re, a_im, k_re * br - k_im * bi, k_re * bi + k_im * br


def _s5_prep_fwd(lre, lim, ldt, br, bi):
    def body(lre_ref, lim_ref, ldt_ref, br_ref, bi_ref, ar_ref, ai_ref, bbr_ref, bbi_ref):
        ar, ai, bbr, bbi = _s5_disc(lre_ref[...], lim_ref[...], ldt_ref[...], br_ref[...], bi_ref[...])
        ar_ref[...] = ar
        ai_ref[...] = ai
        bbr_ref[...] = bbr
        bbi_ref[...] = bbi

    return pl.pallas_call(
        body, name="s5_prep_fwd",
        out_shape=(jax.ShapeDtypeStruct(lre.shape, F32), jax.ShapeDtypeStruct(lre.shape, F32),
                   jax.ShapeDtypeStruct(br.shape, F32), jax.ShapeDtypeStruct(br.shape, F32)),
        compiler_params=_params(),
    )(lre, lim, ldt, br, bi)


def _s5_prep_bwd(lre, lim, ldt, br, bi, dar, dai, dbbr, dbbi):
    def body(lre_ref, lim_ref, ldt_ref, br_ref, bi_ref, dar_ref, dai_ref, dbbr_ref, dbbi_ref,
             o_lre, o_lim, o_ldt, o_br, o_bi):
        _, pull = jax.vjp(_s5_disc, lre_ref[...], lim_ref[...], ldt_ref[...], br_ref[...], bi_ref[...])
        g = pull((dar_ref[...], dai_ref[...], dbbr_ref[...], dbbi_ref[...]))
        for o, v in zip((o_lre, o_lim, o_ldt, o_br, o_bi), g):
            o[...] = v

    return pl.pallas_call(
        body, name="s5_prep_bwd",
        out_shape=tuple(jax.ShapeDtypeStruct(a.shape, F32) for a in (lre, lim, ldt, br, bi)),
        compiler_params=_params(),
    )(lre, lim, ldt, br, bi, dar, dai, dbbr, dbbi)


def _cmul(ar, ai, br, bi):
    return ar * br - ai * bi, ar * bi + ai * br


def _s5_tile_consts(ar, ai, reverse):
    if reverse:
        ai = -ai
    shape = (SUBLANES, S5_BLOCK)
    row = lax.broadcasted_iota(jnp.int32, shape, 0)
    a1 = (jnp.broadcast_to(ar, shape), jnp.broadcast_to(ai, shape))
    a2 = _cmul(*a1, *a1)
    a4 = _cmul(*a2, *a2)
    a8 = _cmul(*a4, *a4)
    steps = []
    for s, (pr, pi) in ((1, a1), (2, a2), (4, a4)):
        keep = (row < SUBLANES - s) if reverse else (row >= s)
        steps.append((s, jnp.where(keep, pr, 0.0), jnp.where(keep, pi, 0.0)))
    e = (SUBLANES - row) if reverse else (row + 1)
    pr, pi = jnp.ones(shape, F32), jnp.zeros(shape, F32)
    for bit, (qr, qi) in ((1, a1), (2, a2), (4, a4), (8, a8)):
        nr, ni = _cmul(pr, pi, qr, qi)
        hit = (e & bit) != 0
        pr, pi = jnp.where(hit, nr, pr), jnp.where(hit, ni, pi)
    return steps, pr, pi


def _s5_tile(xr, xi, steps, reverse):
    for s, pr, pi in steps:
        sh = SUBLANES - s if reverse else s
        sr, si = pltpu.roll(xr, sh, axis=0), pltpu.roll(xi, sh, axis=0)
        xr, xi = xr + pr * sr - pi * si, xi + pr * si + pi * sr
    return xr, xi


N_TILES = SEQ // SUBLANES


def _s5_fwd(z, b_re, b_im, c_re, c_im, a_re, a_im, d, glu_w, glu_b):
    nblk = S5_LANES // S5_BLOCK

    def body(u_ref, br_ref, bi_ref, cr_ref, ci_ref, ar_ref, ai_ref, d_ref, gw_ref, gb_ref,
             y_ref, y0_ref, xr_ref, xi_ref, ub_scr, acc_scr):
        jb = pl.program_id(0)

        @pl.when(jb == 0)
        def _():
            ub_scr[...] = u_ref[...].astype(BF16)
            acc_scr[...] = jnp.zeros_like(acc_scr)

        ub = ub_scr[...]
        xr_ref[...] = _dot(ub, br_ref[...])
        xi_ref[...] = _dot(ub, bi_ref[...])
        steps, pr, pi = _s5_tile_consts(ar_ref[...], ai_ref[...], False)

        def tile(t, carry):
            cr, ci = carry
            rows = pl.ds(pl.multiple_of(t * SUBLANES, SUBLANES), SUBLANES)
            xr, xi = _s5_tile(xr_ref[rows, :], xi_ref[rows, :], steps, False)
            xr, xi = xr + pr * cr - pi * ci, xi + pr * ci + pi * cr
            xr_ref[rows, :] = xr
            xi_ref[rows, :] = xi
            return xr[SUBLANES - 1:, :], xi[SUBLANES - 1:, :]

        zero = jnp.zeros((1, S5_BLOCK), F32)
        lax.fori_loop(0, N_TILES, tile, (zero, zero), unroll=2)
        acc_scr[...] += (_dot(xr_ref[...].astype(BF16), cr_ref[...]) - _dot(xi_ref[...].astype(BF16), ci_ref[...]))

        @pl.when(jb == nblk - 1)
        def _():
            y0 = acc_scr[...] + d_ref[...] * u_ref[...]
            y0_ref[...] = y0
            y1 = _gelu(y0)
            y_ref[...] = y1 * _sigmoid(_dot(y1.astype(BF16), gw_ref[...]) + gb_ref[...])

    lane_blk = pl.BlockSpec((SEQ, S5_BLOCK), lambda j: (0, j))
    return pl.pallas_call(
        body, name="s5_fwd", grid=(nblk,),
        out_shape=(jax.ShapeDtypeStruct((SEQ, W_GRP), F32), jax.ShapeDtypeStruct((SEQ, W_GRP), F32),
                   jax.ShapeDtypeStruct((SEQ, S5_LANES), F32), jax.ShapeDtypeStruct((SEQ, S5_LANES), F32)),
        in_specs=[_zslab(W_GRP, 6),
                  pl.BlockSpec((W_GRP, S5_BLOCK), lambda j: (0, j)), pl.BlockSpec((W_GRP, S5_BLOCK), lambda j: (0, j)),
                  pl.BlockSpec((S5_BLOCK, W_GRP), lambda j: (j, 0)), pl.BlockSpec((S5_BLOCK, W_GRP), lambda j: (j, 0)),
                  pl.BlockSpec((1, S5_BLOCK), lambda j: (0, j)), pl.BlockSpec((1, S5_BLOCK), lambda j: (0, j)),
                  _full((1, W_GRP)), _full((W_GRP, W_GRP)), _full((1, W_GRP))],
        out_specs=(_full((SEQ, W_GRP)), _full((SEQ, W_GRP)), lane_blk, lane_blk),
        scratch_shapes=[pltpu.VMEM((SEQ, W_GRP), BF16), pltpu.VMEM((SEQ, W_GRP), F32)],
        compiler_params=_params("arbitrary"),
    )(z, b_re, b_im, c_re, c_im, a_re, a_im, d, glu_w, glu_b)


def _s5_bwd(z, y0, dy, xr, xi, b_re, b_im, c_re, c_im, a_re, a_im, d, glu_w, glu_b):
    nblk = S5_LANES // S5_BLOCK

    def body(u_ref, y0_ref, dy_ref, xr_ref, xi_ref, br_ref, bi_ref, cr_ref, ci_ref, ar_ref, ai_ref,
             d_ref, gw_ref, gb_ref,
             du_ref, dbr_ref, dbi_ref, dcr_ref, dci_ref, dar_ref, dai_ref, dd_ref, dgw_ref, dgb_ref,
             ub_scr, dy0_scr, du_scr, lr_scr, li_scr):
        jb = pl.program_id(0)

        @pl.when(jb == 0)
        def _():
            u = u_ref[...]
            y0v = y0_ref[...]
            y1 = _gelu(y0v)
            y1b = y1.astype(BF16)
            sg = _sigmoid(_dot(y1b, gw_ref[...]) + gb_ref[...])
            dyv = dy_ref[...]
            dpre = dyv * y1 * sg * (1.0 - sg)
            dpb = dpre.astype(BF16)
            dgw_ref[...] = _dot_tn(y1b, dpb)
            dgb_ref[...] = _sum0(dpre)
            dy1 = dyv * sg + _dot_nt(dpb, gw_ref[...])
            dy0 = dy1 * _gelu_grad(y0v)
            dd_ref[...] = _sum0(dy0 * u)
            du_scr[...] = dy0 * d_ref[...]
            dy0_scr[...] = dy0.astype(BF16)
            ub_scr[...] = u.astype(BF16)

        dy0b = dy0_scr[...]
        lr_scr[...] = _dot_nt(dy0b, cr_ref[...])
        li_scr[...] = -_dot_nt(dy0b, ci_ref[...])
        dcr_ref[...] = _dot_tn(xr_ref[...].astype(BF16), dy0b)
        dci_ref[...] = -_dot_tn(xi_ref[...].astype(BF16), dy0b)
        steps, pr, pi = _s5_tile_consts(ar_ref[...], ai_ref[...], True)
        row = lax.broadcasted_iota(jnp.int32, (SUBLANES, S5_BLOCK), 0)

        def tile(k, carry):
            cr, ci, accr, acci = carry
            t = N_TILES - 1 - k
            rows = pl.ds(pl.multiple_of(t * SUBLANES, SUBLANES), SUBLANES)
            lr, li = _s5_tile(lr_scr[rows, :], li_scr[rows, :], steps, True)
            lr, li = lr + pr * cr - pi * ci, li + pr * ci + pi * cr
            lr_scr[rows, :] = lr
            li_scr[rows, :] = li
            prev = pl.ds(pl.multiple_of(jnp.maximum(t - 1, 0) * SUBLANES, SUBLANES), SUBLANES)
            live = jnp.where(t > 0, 1.0, 0.0)
            xpr = jnp.where(row == 0, pltpu.roll(xr_ref[prev, :], 1, axis=0) * live, pltpu.roll(xr_ref[rows, :], 1, axis=0))
            xpi = jnp.where(row == 0, pltpu.roll(xi_ref[prev, :], 1, axis=0) * live, pltpu.roll(xi_ref[rows, :], 1, axis=0))
            accr = accr + lr * xpr + li * xpi
            acci = acci + li * xpr - lr * xpi
            return lr[0:1, :], li[0:1, :], accr, acci

        zero = jnp.zeros((1, S5_BLOCK), F32)
        zt = jnp.zeros((SUBLANES, S5_BLOCK), F32)
        _, _, accr, acci = lax.fori_loop(0, N_TILES, tile, (zero, zero, zt, zt), unroll=2)
        dar_ref[...] = jnp.zeros_like(dar_ref)
        dai_ref[...] = jnp.zeros_like(dai_ref)
        dar_ref[0:1, :] = _sum0(accr)
        dai_ref[0:1, :] = _sum0(acci)
        lrb, lib = lr_scr[...].astype(BF16), li_scr[...].astype(BF16)
        ub = ub_scr[...]
        dbr_ref[...] = _dot_tn(ub, lrb)
        dbi_ref[...] = _dot_tn(ub, lib)
        du_scr[...] += _dot_nt(lrb, br_ref[...]) + _dot_nt(lib, bi_ref[...])

        @pl.when(jb == nblk - 1)
        def _():
            du_ref[...] = du_scr[...]

    lane_blk = pl.BlockSpec((SEQ, S5_BLOCK), lambda j: (0, j))
    bspec = pl.BlockSpec((W_GRP, S5_BLOCK), lambda j: (0, j))
    cspec = pl.BlockSpec((S5_BLOCK, W_GRP), lambda j: (j, 0))
    aspec = pl.BlockSpec((1, S5_BLOCK), lambda j: (0, j))
    a8spec = pl.BlockSpec((SUBLANES, S5_BLOCK), lambda j: (0, j))
    sd = jax.ShapeDtypeStruct
    return pl.pallas_call(
        body, name="s5_bwd", grid=(nblk,),
        out_shape=(sd((SEQ, W_GRP), F32), sd((W_GRP, S5_LANES), F32), sd((W_GRP, S5_LANES), F32),
                   sd((S5_LANES, W_GRP), F32), sd((S5_LANES, W_GRP), F32),
                   sd((SUBLANES, S5_LANES), F32), sd((SUBLANES, S5_LANES), F32),
                   sd((1, W_GRP), F32), sd((W_GRP, W_GRP), F32), sd((1, W_GRP), F32)),
        in_specs=[_zslab(W_GRP, 6), _full((SEQ, W_GRP)), _full((SEQ, W_GRP)), lane_blk, lane_blk,
                  bspec, bspec, cspec, cspec, aspec, aspec,
                  _full((1, W_GRP)), _full((W_GRP, W_GRP)), _full((1, W_GRP))],
        out_specs=(_full((SEQ, W_GRP)), bspec, bspec, cspec, cspec, a8spec, a8spec,
                   _full((1, W_GRP)), _full((W_GRP, W_GRP)), _full((1, W_GRP))),
        scratch_shapes=[pltpu.VMEM((SEQ, W_GRP), BF16), pltpu.VMEM((SEQ, W_GRP), BF16), pltpu.VMEM((SEQ, W_GRP), F32),
                        pltpu.VMEM((SEQ, S5_BLOCK), F32), pltpu.VMEM((SEQ, S5_BLOCK), F32)],
        compiler_params=_params("arbitrary"),
    )(z, y0, dy, xr, xi, b_re, b_im, c_re, c_im, a_re, a_im, d, glu_w, glu_b)


def _head(x, g, target):
    tm = TOKEN_TILE
    n = SEQ // tm

    def body(x_ref, g_ref, t_ref, dx_ref, st_ref, acc_scr):
        i = pl.program_id(0)

        @pl.when(i == 0)
        def _():
            acc_scr[...] = jnp.zeros_like(acc_scr)

        xhat, r = _rms(x_ref[...])
        gv = g_ref[...]
        err = xhat * gv - t_ref[...]
        dyv = err * (1.0 / D_MODEL)
        dx_ref[...] = _rms_bwd(xhat, r, dyv * gv)
        acc_scr[0:1, :] += _sum0(err * err)
        acc_scr[1:2, :] += _sum0(dyv * xhat)

        @pl.when(i == n - 1)
        def _():
            st_ref[...] = acc_scr[...]
            tot = jnp.sum(acc_scr[0:1, :], axis=-1, keepdims=True) * (0.5 / D_MODEL)
            st_ref[0:1, :] = jnp.broadcast_to(tot, (1, D_MODEL))

    tok = pl.BlockSpec((tm, D_MODEL), lambda i: (i, 0))
    return pl.pallas_call(
        body, name="head", grid=(n,),
        out_shape=(jax.ShapeDtypeStruct((SEQ, D_MODEL), F32), jax.ShapeDtypeStruct((SUBLANES, D_MODEL), F32)),
        in_specs=[tok, _row_spec(1), tok], out_specs=(tok, _row_spec(SUBLANES)),
        scratch_shapes=[pltpu.VMEM((SUBLANES, D_MODEL), F32)],
        compiler_params=_params("arbitrary"),
    )(x, g, target)


def _adamw(w, gparts, m, v, name):
    r, c = w.shape
    npart = gparts.shape[0]
    tr = r
    for cand in (512, 256, 128, 64, 32, 16):
        if r % cand == 0 and r > cand:
            tr = cand
            break
    b1c = 1.0 - ADAM_B1 ** ADAM_STEP
    b2c = 1.0 - ADAM_B2 ** ADAM_STEP

    def body(w_ref, g_ref, m_ref, v_ref, go_ref, d_ref, mo_ref, vo_ref):
        g = g_ref[0].astype(F32)
        for k in range(1, npart):
            g = g + g_ref[k].astype(F32)
        mn = ADAM_B1 * m_ref[...] + (1.0 - ADAM_B1) * g
        vn = ADAM_B2 * v_ref[...] + (1.0 - ADAM_B2) * (g * g)
        m_hat = mn / b1c
        v_hat = vn / b2c
        go_ref[...] = g
        d_ref[...] = -ADAM_LR * (m_hat / (jnp.sqrt(v_hat) + ADAM_EPS) + ADAM_WD * w_ref[...])
        mo_ref[...] = mn
        vo_ref[...] = vn

    blk = pl.BlockSpec((tr, c), lambda i: (i, 0))
    sh = jax.ShapeDtypeStruct((r, c), F32)
    return pl.pallas_call(
        body, name=name, grid=(r // tr,),
        out_shape=(sh, sh, sh, sh),
        in_specs=[blk, pl.BlockSpec((npart, tr, c), lambda i: (0, i, 0)), blk, blk],
        out_specs=(blk, blk, blk, blk), compiler_params=_params("arbitrary"),
    )(w, gparts, m, v)


def _adamw_layer(l, w, gparts, m, v, prev, name):
    _, r, c = w.shape
    tr = max(t for t in range(16, 513, 16) if r % t == 0)
    b1c = 1.0 - ADAM_B1 ** ADAM_STEP
    b2c = 1.0 - ADAM_B2 ** ADAM_STEP
    nprev = 0 if prev is None else 4

    def body(core_ref, *refs):
        w_ref, m_ref, v_ref = refs[:3]
        g_refs = refs[3:3 + N_CHIPS]
        go_ref, d_ref, mo_ref, vo_ref = refs[3 + N_CHIPS + nprev:]
        g = g_refs[0][0].astype(F32)
        for g_ref in g_refs[1:]:
            g = g + g_ref[0].astype(F32)
        mn = ADAM_B1 * m_ref[0] + (1.0 - ADAM_B1) * g
        vn = ADAM_B2 * v_ref[0] + (1.0 - ADAM_B2) * (g * g)
        go_ref[0] = g
        d_ref[0] = -ADAM_LR * ((mn / b1c) / (jnp.sqrt(vn / b2c) + ADAM_EPS) + ADAM_WD * w_ref[0])
        mo_ref[0] = mn
        vo_ref[0] = vn

    blk = pl.BlockSpec((1, tr, c), lambda i, core: (l, i, 0))
    slots = [pl.BlockSpec((1, tr, c), functools.partial(lambda i, core, q: (2 * q + core[0], i, 0), q=q))
             for q in range(N_CHIPS)]
    sh = jax.ShapeDtypeStruct(w.shape, F32)
    keep = [pl.BlockSpec(memory_space=pl.ANY)] * nprev
    return pl.pallas_call(
        body, name=name, out_shape=(sh, sh, sh, sh),
        grid_spec=pltpu.PrefetchScalarGridSpec(num_scalar_prefetch=1, grid=(r // tr,),
                                               in_specs=[blk, blk, blk, *slots, *keep], out_specs=(blk, blk, blk, blk)),
        input_output_aliases={4 + N_CHIPS + k: k for k in range(nprev)},
        compiler_params=_params("arbitrary"),
    )(_core_index(), w, m, v, *([gparts] * N_CHIPS), *(prev or ()))


def _sum_parts(parts, name):
    n, r, c = parts.shape

    def body(p_ref, o_ref):
        acc = p_ref[0]
        for k in range(1, n):
            acc = acc + p_ref[k]
        o_ref[...] = acc

    return pl.pallas_call(
        body, name=name, out_shape=jax.ShapeDtypeStruct((r, c), F32), compiler_params=_params(),
    )(parts)


def _block_diag(blocks):
    g, a, b = blocks.shape
    eye = jnp.eye(g, dtype=blocks.dtype)
    return (blocks[:, :, None, :] * eye[:, None, :, None]).reshape(g * a, g * b)


def _diag_blocks(dense, g):
    a, b = dense.shape[0] // g, dense.shape[1] // g
    d4 = dense.reshape(g, a, g, b)
    eye = jnp.eye(g, dtype=dense.dtype)
    return jnp.sum(d4 * eye[:, None, :, None], axis=2)


def _pack(parts, cols):
    flat = jnp.concatenate([p.reshape(-1) for p in parts])
    unit = N_DEV * SUBLANES * cols
    total = -(-flat.shape[0] // unit) * unit
    flat = jnp.pad(flat, (0, total - flat.shape[0]))
    return flat.reshape(N_DEV, total // (N_DEV * cols), cols)


def _unpack(flat, shapes):
    out, pos = [], 0
    for s in shapes:
        n = math.prod(s)
        out.append(flat[pos:pos + n].reshape(s))
        pos += n
    return out


SMALL = ("norm1_g", "norm2_g", "sgu_w", "sgu_b", "pool_w", "pool_scale", "s5_lambda_re", "s5_lambda_im",
         "s5_b_re", "s5_b_im", "s5_c_re", "s5_c_im", "s5_d", "s5_log_dt", "s5_glu_b", "mix_norm_g",
         "norm3_g", "final_norm_g", "conv_w", "s5_glu_w")
N_SMALL_WHOLE = len(SMALL) - 2
BIG = ("ffn1_w_in", "ffn1_w_out", "w_mix_in", "w_mix_out", "ffn2_w_in", "ffn2_w_out")
TRANSPOSED = ("ffn1_w_in", "w_mix_in", "ffn2_w_in")
WEIGHTS = ("ada_w", "ada_b", "norm1_g", "ffn1_w_in", "ffn1_w_out", "norm2_g", "w_mix_in", "sgu_w", "sgu_b", "pool_w",
           "pool_scale", "conv_w", "s5_lambda_re", "s5_lambda_im", "s5_b_re", "s5_b_im", "s5_c_re", "s5_c_im", "s5_d",
           "s5_log_dt", "s5_glu_w", "s5_glu_b", "mix_norm_g", "w_mix_out", "norm3_g", "ffn2_w_in", "ffn2_w_out",
           "final_norm_g")
PACK_COLS = 1024


def kernel(x, c, ada_w, ada_b, norm1_g, ffn1_w_in, ffn1_w_out, norm2_g, w_mix_in, sgu_w, sgu_b, pool_w, pool_scale, conv_w, s5_lambda_re, s5_lambda_im, s5_b_re, s5_b_im, s5_c_re, s5_c_im, s5_d, s5_log_dt, s5_glu_w, s5_glu_b, mix_norm_g, w_mix_out, norm3_g, ffn2_w_in, ffn2_w_out, final_norm_g, loss_target, m_ada_w, m_ada_b, m_norm1_g, m_ffn1_w_in, m_ffn1_w_out, m_norm2_g, m_w_mix_in, m_sgu_w, m_sgu_b, m_pool_w, m_pool_scale, m_conv_w, m_s5_lambda_re, m_s5_lambda_im, m_s5_b_re, m_s5_b_im, m_s5_c_re, m_s5_c_im, m_s5_d, m_s5_log_dt, m_s5_glu_w, m_s5_glu_b, m_mix_norm_g, m_w_mix_out, m_norm3_g, m_ffn2_w_in, m_ffn2_w_out, m_final_norm_g, v_ada_w, v_ada_b, v_norm1_g, v_ffn1_w_in, v_ffn1_w_out, v_norm2_g, v_w_mix_in, v_sgu_w, v_sgu_b, v_pool_w, v_pool_scale, v_conv_w, v_s5_lambda_re, v_s5_lambda_im, v_s5_b_re, v_s5_b_im, v_s5_c_re, v_s5_c_im, v_s5_d, v_s5_log_dt, v_s5_glu_w, v_s5_glu_b, v_mix_norm_g, v_w_mix_out, v_norm3_g, v_ffn2_w_in, v_ffn2_w_out, v_final_norm_g):
    args = dict(locals())
    W = {n: args[n] for n in WEIGHTS}
    M = {n: args["m_" + n] for n in WEIGHTS}
    V = {n: args["v_" + n] for n in WEIGHTS}
    me = _me()
    L = DEPTH
    x0 = x[0]
    target = loss_target[0]

    conv_cols = conv_w.shape[-1]
    glu_rows = s5_glu_w.shape[1]
    c_g, conv_g, glu_g = _exchange(
        [c.reshape(SUBLANES, LANES), conv_w.reshape(L * 3, conv_cols), s5_glu_w.reshape(L * glu_rows, W_GRP)],
        False, "gather_small")
    c_all = c_g.reshape(N_DEV, D_MODEL)
    conv_full = conv_g.reshape(N_DEV, L, 3, conv_cols).transpose(1, 2, 0, 3).reshape(L, 3, W_GRP)
    glu_full = glu_g.reshape(N_DEV, L, glu_rows, W_GRP).transpose(1, 0, 2, 3).reshape(L, W_GRP, W_GRP)

    ncol = ada_w.shape[-1]
    ada_b_mine = lax.dynamic_slice_in_dim(ada_b, me * ncol, ncol, axis=1).reshape(L, 1, ncol)
    cond_part = _cond_fwd(c_all, ada_w, ada_b_mine)
    (cond_g,) = _exchange([cond_part.reshape(L * N_DEV, ncol)], False, "gather_cond")
    cond_g = cond_g.reshape(N_DEV, L, N_DEV, ncol)
    cond_mine = lax.dynamic_index_in_dim(cond_g, me, axis=2, keepdims=False)
    cond = cond_mine.transpose(1, 0, 2).reshape(L, N_ADA, D_MODEL)

    lg = L * S5_GROUPS
    lre3 = s5_lambda_re.reshape(lg, S5_STATE, 1)
    lim3 = s5_lambda_im.reshape(lg, S5_STATE, 1)
    ldt3 = s5_log_dt.reshape(lg, 1, 1)
    br3 = s5_b_re.reshape(lg, S5_STATE, S5_GROUP_CH)
    bi3 = s5_b_im.reshape(lg, S5_STATE, S5_GROUP_CH)

    def b_mat(bb3, l):
        return _block_diag(bb3.reshape(L, S5_GROUPS, S5_STATE, S5_GROUP_CH)[l].transpose(0, 2, 1)).astype(BF16)

    def c_mat(cw, l):
        return _block_diag(cw[l].transpose(0, 2, 1)).astype(BF16)

    for n in TRANSPOSED:
        W[n], M[n], V[n] = (a.transpose(0, 2, 1) for a in (W[n], M[n], V[n]))

    gathered_shape = {"ffn1_w_out": (N_FF_CHUNK, FF_PIECE, D_MODEL), "ffn2_w_out": (N_FF_CHUNK, FF_PIECE, D_MODEL),
                      "w_mix_in": (P_IN, D_MODEL), "w_mix_out": (D_MODEL, D_MODEL)}

    def gather_start(l, after, names=BIG):
        srcs = [W[n][l].astype(BF16) for n in names]
        lands = _place_own(srcs, False, "gather_weights_own")
        return _exchange_start(srcs, lands, after, False, "gather_weights_start", GATHER_PEERS)

    def gather_finish(handle, after, names=BIG):
        g = _exchange_wait(handle, after, False, "gather_weights_wait", GATHER_PEERS)
        out = dict(zip(names, _gather_sibling(g, "gather_weights_sibling")))
        return {n: (a.reshape(gathered_shape[n]) if n in gathered_shape else a) for n, a in out.items()}

    saved = []
    xc = x0
    first_ffn = BIG[:2]
    handle, token = gather_start(0, cond, first_ffn)
    handle_rest, token = gather_start(0, token, BIG[2:])

    zero = token[0, 0]
    a_re3, a_im3, bb_re3, bb_im3 = _s5_prep_fwd(lre3 + zero, lim3, ldt3, br3, bi3)
    a_re = a_re3.reshape(L, 1, S5_LANES)
    a_im = a_im3.reshape(L, 1, S5_LANES)
    sgu_w_t, pool_w_t, c_re_t, c_im_t = (a + zero for a in (sgu_w, pool_w, s5_c_re, s5_c_im))

    def mixer_consts(l):
        w_cat = sgu_w_t[l].transpose(1, 0, 2).reshape(CHUNK, SGU_HEADS * CHUNK)
        bias = jnp.repeat(sgu_b[l].T, SGU_HEAD_DIM, axis=1)
        return dict(
            w_cat=w_cat, bias=bias, pool_bd=_block_diag(pool_w_t[l]).astype(BF16), pool_scale=pool_scale[l][None],
            conv=conv_full[l], b_re=b_mat(bb_re3, l), b_im=b_mat(bb_im3, l), c_re=c_mat(c_re_t, l),
            c_im=c_mat(c_im_t, l), a_re=a_re[l], a_im=a_im[l], d=s5_d[l][None], glu_w=glu_full[l].astype(BF16),
            glu_b=s5_glu_b[l][None])

    mcs = [mixer_consts(l) for l in range(L)]
    small_names = SMALL + ("ada_b",)
    first_small = W[small_names[0]] + zero
    packed_w = _pack([first_small] + [W[n] for n in small_names[1:]], PACK_COLS)
    packed_m = _pack([M[n] + zero if n == small_names[0] else M[n] for n in small_names], PACK_COLS)
    packed_v = _pack([V[n] + zero if n == small_names[0] else V[n] for n in small_names], PACK_COLS)
    wl = gather_finish(handle, [x0] + [a for mc in mcs for a in mc.values()], first_ffn)
    for l in range(L):
        cl = cond[l]
        if 0 < l < L - 1:
            handle, token = gather_start(l + 1, wl["ffn1_w_in"])
            cl = cl + token[0, 0]
        mc = mcs[l]
        x_a = xc
        x_b, *ffn1_kept = _ffn_fwd(x_a, cl[0:3], norm1_g[l][None], wl["ffn1_w_in"], wl["ffn1_w_out"])
        if l == 0:
            wl = {**wl, **gather_finish(handle_rest, [x_b, packed_w, packed_m, packed_v], BIG[2:])}
            handle, token = gather_start(1, wl["w_mix_in"])
            cl = cl + token[0, 0]
        z = _mix_in_fwd(x_b, cl[3:5], norm2_g[l][None], wl["w_mix_in"])
        ya = _sgu_fwd(z, mc["w_cat"], mc["bias"])
        yb = _pool_fwd(z, mc["pool_bd"], mc["pool_scale"])
        yc = _conv_fwd(z, mc["conv"])
        yd, y0, sxr, sxi = _s5_fwd(z, mc["b_re"], mc["b_im"], mc["c_re"], mc["c_im"], mc["a_re"], mc["a_im"],
                                   mc["d"], mc["glu_w"], mc["glu_b"])
        ys = (ya, yb, yc, yd)
        x_c, mo = _mix_out_fwd(x_b, ys, mix_norm_g[l][None], cl[5:6], wl["w_mix_out"])
        x_d, *ffn2_kept = _ffn_fwd(x_c, cl[6:9], norm3_g[l][None], wl["ffn2_w_in"], wl["ffn2_w_out"])
        saved.append(dict(wl=wl, mc=mc, x_a=x_a, x_b=x_b, x_c=x_c, ffn1=ffn1_kept, ffn2=ffn2_kept, z=z, ys=ys, y0=y0,
                          xr=sxr, xi=sxi, mo=mo))
        xc = x_d
        if l + 1 < L:
            wl = gather_finish(handle, x_d)

    dx, stats = _head(xc, final_norm_g[None], target)
    loss = lax.psum(stats[0, 0], MESH_AXES)

    small_grads = {n: [None] * L for n in SMALL if n != "final_norm_g"}
    small_grads["final_norm_g"] = stats[1]
    big_out = {n: None for n in BIG}
    pending = None

    def swap_begin(pieces, after):
        bufs = _place_own(pieces, True, "scatter_grads_own")
        return _exchange_start(pieces, bufs, after, True, "swap_grads_start", SIBLING_SWAP)

    def scatter_begin(swap_handle, after):
        mine, theirs = _exchange_wait(swap_handle, after, True, "swap_grads_wait", SIBLING_SWAP, with_srcs=True)
        sums = _presum(mine, theirs, "presum_grads")
        lands = _place_own(sums, True, "scatter_grads_own")
        return _exchange_start(sums, lands, theirs[0], True, "scatter_grads_start", OTHER_CHIPS)

    def finish_scatter(pend, after):
        layer, hnd = pend
        recv = _exchange_wait(hnd, after, True, "scatter_grads_wait", OTHER_CHIPS)
        for n, r in zip(BIG, recv):
            big_out[n] = _adamw_layer(layer, W[n], r, M[n], V[n], big_out[n], "adamw_" + n)
        return recv[0]

    dcond_rows = [None] * L
    d_are, d_aim, d_bbre, d_bbim = [None] * L, [None] * L, [None] * L, [None] * L
    swap = None
    for l in reversed(range(L)):
        sv = saved[l]
        wl, mc, cl = sv["wl"], sv["mc"], cond[l]
        if swap is not None:
            cl = cl + swap[1][0, 0]
        f2, a2, b2, hb = sv["ffn2"]
        dx, da, db, act, dob, part3 = _ffn_bwd(dx, sv["x_c"], f2, a2, b2, cl[6:9], norm3_g[l][None],
                                               wl["ffn2_w_in"], wl["ffn2_w_out"])
        if swap is not None:
            handle, token = scatter_begin(swap[0], dx)
            pending = (l + 1, handle)
            cl = cl + token[0, 0]
        g_ffn2_in = _dw(da, hb, db, name="dw_ffn_in")
        g_ffn2_out = _dw(act, dob, name="dw_ffn_out").reshape(N_DEV, D_FF // N_DEV, D_MODEL)
        dya, dyb, dyc, dyd, ynb, dmob, part_mo = _mix_out_bwd(dx, sv["mo"], sv["ys"], mix_norm_g[l][None], cl[5:6],
                                                              wl["w_mix_out"])
        g_mix_out = _dw(ynb[None], dmob, name="dw_mix_out").reshape(N_DEV, D_MODEL // N_DEV, D_MODEL)
        z = sv["z"]
        dza, dw_cat, dbias = _sgu_bwd(z, dya, mc["w_cat"], mc["bias"])
        dzb, dpool_dense, dpool_scale = _pool_bwd(z, dyb, mc["pool_bd"], mc["pool_scale"])
        dzc, dconv8 = _conv_bwd(z, dyc, mc["conv"])
        (dzd, dbre_d, dbim_d, dcre_d, dcim_d, dar8, dai8, dd, dglu_w, dglu_b) = _s5_bwd(
            z, sv["y0"], dyd, sv["xr"], sv["xi"], mc["b_re"], mc["b_im"], mc["c_re"], mc["c_im"],
            mc["a_re"], mc["a_im"], mc["d"], mc["glu_w"], mc["glu_b"])
        dx, h2b, dzbf, part2 = _mix_in_bwd((dza, dzb, dzc, dzd), sv["x_b"], dx, cl[3:5], norm2_g[l][None],
                                           wl["w_mix_in"])
        g_mix_in = _dw(dzbf[None], h2b, name="dw_mix_in").reshape(N_DEV, P_IN // N_DEV, D_MODEL)
        f1, a1, b1, hb = sv["ffn1"]
        dx, da, db, act, dob, part1 = _ffn_bwd(dx, sv["x_a"], f1, a1, b1, cl[0:3], norm1_g[l][None],
                                               wl["ffn1_w_in"], wl["ffn1_w_out"])
        g_ffn1_in = _dw(da, hb, db, name="dw_ffn_in")
        g_ffn1_out = _dw(act, dob, name="dw_ffn_out").reshape(N_DEV, D_FF // N_DEV, D_MODEL)
        last = finish_scatter(pending, g_ffn1_in) if pending is not None else dx
        swap = swap_begin([g_ffn1_in, g_ffn1_out, g_mix_in, g_mix_out, g_ffn2_in, g_ffn2_out], last)
        dcond_rows[l] = jnp.concatenate([part1[0:3], part2[0:2], part_mo[0:1], part3[0:3]], axis=0)
        sg = small_grads
        sg["norm1_g"][l] = part1[3]
        sg["norm2_g"][l] = part2[2]
        sg["norm3_g"][l] = part3[3]
        sg["mix_norm_g"][l] = part_mo[1]
        sg["sgu_w"][l] = dw_cat.reshape(CHUNK, SGU_HEADS, CHUNK).transpose(1, 0, 2)
        sg["sgu_b"][l] = dbias[:, ::SGU_HEAD_DIM].T
        sg["pool_w"][l] = _diag_blocks(dpool_dense, len(POOL_WINDOWS))
        sg["pool_scale"][l] = dpool_scale[0]
        sg["conv_w"][l] = dconv8[0:3]
        sg["s5_c_re"][l] = _diag_blocks(dcre_d, S5_GROUPS).transpose(0, 2, 1)
        sg["s5_c_im"][l] = _diag_blocks(dcim_d, S5_GROUPS).transpose(0, 2, 1)
        sg["s5_d"][l] = dd[0]
        sg["s5_glu_w"][l] = dglu_w
        sg["s5_glu_b"][l] = dglu_b[0]
        d_are[l], d_aim[l] = dar8[0], dai8[0]
        d_bbre[l] = _diag_blocks(dbre_d, S5_GROUPS).transpose(0, 2, 1)
        d_bbim[l] = _diag_blocks(dbim_d, S5_GROUPS).transpose(0, 2, 1)
    grad_x = dx

    g_lre, g_lim, g_ldt, g_br, g_bi = _s5_prep_bwd(
        lre3, lim3, ldt3, br3, bi3,
        jnp.stack(d_are).reshape(lg, S5_STATE, 1), jnp.stack(d_aim).reshape(lg, S5_STATE, 1),
        jnp.stack(d_bbre).reshape(lg, S5_STATE, S5_GROUP_CH), jnp.stack(d_bbim).reshape(lg, S5_STATE, S5_GROUP_CH))
    small = {n: (jnp.stack(v) if isinstance(v, list) and v[0] is not None else v) for n, v in small_grads.items()}
    small["s5_lambda_re"] = g_lre.reshape(s5_lambda_re.shape)
    small["s5_lambda_im"] = g_lim.reshape(s5_lambda_im.shape)
    small["s5_log_dt"] = g_ldt.reshape(s5_log_dt.shape)
    small["s5_b_re"] = g_br.reshape(s5_b_re.shape)
    small["s5_b_im"] = g_bi.reshape(s5_b_im.shape)

    small_shapes = [(L, 3, W_GRP) if n == "conv_w" else (L, W_GRP, W_GRP) if n == "s5_glu_w" else W[n].shape
                    for n in SMALL]
    packed = _pack([small[n].reshape(s) for n, s in zip(SMALL, small_shapes)], PACK_COLS) + swap[1][0, 0]
    (pieces,) = _exchange([packed], True, "scatter_small")
    mine = _sum_parts(pieces, "sum_small")
    dcond = jnp.stack(dcond_rows).reshape(L * N_ADA, D_MODEL)
    summed, dcond_g = _exchange([mine, dcond], False, "gather_small_sums")
    summed_flat = summed.reshape(-1)
    n_whole = sum(math.prod(s) for s in small_shapes[:N_SMALL_WHOLE])
    conv_sum, glu_sum = _unpack(summed_flat[n_whole:], small_shapes[N_SMALL_WHOLE:])
    conv_sum = lax.dynamic_slice_in_dim(conv_sum, me * conv_cols, conv_cols, axis=2)
    glu_sum = lax.dynamic_slice_in_dim(glu_sum, me * glu_rows, glu_rows, axis=1)

    handle, token = scatter_begin(swap[0], [summed, dcond_g])
    pending = (0, handle)
    dcond_all = dcond_g.reshape(N_DEV, L, N_ADA * D_MODEL).transpose(1, 0, 2)
    dcond_mine = lax.dynamic_slice_in_dim(dcond_all, me * ncol, ncol, axis=2) + token[0, 0]
    g_ada_w, g_ada_b = _cond_bwd(c_all.T, dcond_mine, dcond_all)

    grads, deltas, new_m, new_v = {}, {}, {}, {}
    out = _adamw(ada_w.reshape(L * D_MODEL, ncol), g_ada_w.reshape(1, L * D_MODEL, ncol),
                 m_ada_w.reshape(L * D_MODEL, ncol), v_ada_w.reshape(L * D_MODEL, ncol), "adamw_ada_w")
    grads["ada_w"], deltas["ada_w"], new_m["ada_w"], new_v["ada_w"] = (o.reshape(ada_w.shape) for o in out)
    shapes = [W[n].shape for n in small_names]
    rows = packed_w.shape[0] * packed_w.shape[1]
    packed_g = _pack([summed_flat[:n_whole], conv_sum, glu_sum, g_ada_b], PACK_COLS)
    out = _adamw(packed_w.reshape(rows, PACK_COLS), packed_g.reshape(1, rows, PACK_COLS),
                 packed_m.reshape(rows, PACK_COLS), packed_v.reshape(rows, PACK_COLS), "adamw_small")
    for store, o in zip((grads, deltas, new_m, new_v), out):
        store.update(zip(small_names, _unpack(o.reshape(-1), shapes)))
    done = [store[n] for store in (grads, deltas, new_m, new_v) for n in small_names + ("ada_w",)]
    finish_scatter(pending, done + [big_out[n][0] for n in BIG])
    for n in BIG:
        res = big_out[n]
        if n in TRANSPOSED:
            res = tuple(r.transpose(0, 2, 1) for r in res)
        grads[n], deltas[n], new_m[n], new_v[n] = res

    return (loss, grad_x[None], *[grads[n] for n in WEIGHTS], *[deltas[n] for n in WEIGHTS],
            *[new_m[n] for n in WEIGHTS], *[new_v[n] for n in WEIGHTS])
```
